```python
import math
import jax, jax.numpy as jnp
from jax import lax
import numpy as np

D_MODEL = 1024
BATCH = 8
SEQ = 4096
DEPTH = 2

HEAD_DIM = 64
N_HEADS = D_MODEL // HEAD_DIM
SB_HEADS = N_HEADS // 4
DIL_HEADS = N_HEADS - SB_HEADS
DIL_PATTERNS = ((128, 1), (512, 4), (2048, 16))
DIL_GROUP_HEADS = DIL_HEADS // len(DIL_PATTERNS)
OUT_WIDTH = (SB_HEADS + DIL_GROUP_HEADS) * HEAD_DIM
BLOCK = 128
N_BUCKETS = 32
MAX_DISTANCE = 2048
D_FF = 2816
RWKV_HEAD = 64
RWKV_HEADS = D_MODEL // RWKV_HEAD
D_DECAY_LORA = 64
D_AAA_LORA = 64
D_GATE_LORA = 160
NORM_EPS = 1e-6
GN_EPS = 64e-5
NEG_INF = -1e30
N_EVEN = (DEPTH + 1) // 2
N_ODD = DEPTH // 2

kernel_name = "hybrid_stickbreak_dilated_rwkv7_macaron"


def rms_norm(x, g, eps=NORM_EPS):
    x32 = x.astype(jnp.float32)
    y = x32 * lax.rsqrt(jnp.mean(x32 * x32, axis=-1, keepdims=True) + eps)
    return (y * g.astype(jnp.float32)).astype(x.dtype)


def swiglu(h, w_gate, w_up, w_down):
    return (jax.nn.silu(h @ w_gate) * (h @ w_up)) @ w_down


def t5_bucket(dist):
    max_exact = N_BUCKETS // 2
    d = jnp.maximum(dist, 1).astype(jnp.float32)
    large = max_exact + (jnp.log(d / max_exact) / math.log(MAX_DISTANCE / max_exact)
                         * (N_BUCKETS - max_exact)).astype(jnp.int32)
    large = jnp.minimum(large, N_BUCKETS - 1)
    return jnp.where(dist < max_exact, dist, large)


def stick_breaking_attention(q, k, v):
    B, S, H, Dh = q.shape
    nb = S // BLOCK
    scale = Dh ** -0.5
    qb = q.reshape(B, nb, BLOCK, H, Dh).transpose(1, 0, 3, 2, 4)
    kpos = jnp.arange(S)

    def one_block(args):
        qi, n = args
        z = jnp.einsum('bhqd,bshd->bhqs', qi, k).astype(jnp.float32) * scale
        qpos = n * BLOCK + jnp.arange(BLOCK)
        strict = kpos[None, :] < qpos[:, None]
        log_keep = jnp.where(strict, jax.nn.log_sigmoid(-z), 0.0)
        after = lax.cumsum(log_keep, axis=3, reverse=True) - log_keep
        weight = jnp.where(strict, jnp.exp(jax.nn.log_sigmoid(z) + after), 0.0)
        return jnp.einsum('bhqs,bshd->bqhd', weight.astype(v.dtype), v)

    out = lax.map(one_block, (qb, jnp.arange(nb)))
    return out.transpose(1, 0, 2, 3, 4).reshape(B, S, H, Dh)


def dilated_window_attention(q, k, v, bias_table, window, dilation):
    B, S, G, Dh = q.shape
    r = dilation
    L = S // r
    span = window // r
    nb = -(-L // BLOCK)
    Lp = nb * BLOCK

    def to_sub(t):
        t = t.reshape(B, L, r, G, Dh).transpose(0, 2, 1, 3, 4)
        t = jnp.pad(t, ((0, 0), (0, 0), (0, Lp - L), (0, 0), (0, 0)))
        return t.reshape(B, r, nb, BLOCK, G, Dh)

    def with_prev(t):
        prev = jnp.pad(t, ((0, 0), (0, 0), (1, 0), (0, 0), (0, 0), (0, 0)))[:, :, :-1]
        return jnp.concatenate([prev, t], axis=3)

    qs = to_sub(q)
    kw, vw = with_prev(to_sub(k)), with_prev(to_sub(v))
    logits = jnp.einsum('brnqgd,brnkgd->brngqk', qs, kw).astype(jnp.float32) * (Dh ** -0.5)
    qi = jnp.arange(BLOCK)[:, None]
    kj = jnp.arange(2 * BLOCK)[None, :] - BLOCK
    dist = qi - kj
    in_window = (dist >= 0) & (dist <= span)
    has_key = (jnp.arange(nb)[:, None, None] > 0) | (kj[None] >= 0)
    mask = (in_window[None] & has_key)[:, None]
    bias = bias_table[t5_bucket(jnp.maximum(dist, 0) * r)].astype(jnp.float32).transpose(2, 0, 1)
    logits = jnp.where(mask, logits + bias, NEG_INF)
    m = jnp.max(logits, axis=-1, keepdims=True)
    p = jnp.exp(logits - m)
    den = jnp.sum(p, axis=-1, keepdims=True)
    o = jnp.einsum('brngqk,brnkgd->brnqgd', (p / den).astype(v.dtype), vw)
    lse = (m + jnp.log(den))[..., 0].transpose(0, 1, 2, 4, 3)

    def from_sub(t):
        t = t.reshape((B, r, Lp) + t.shape[4:])[:, :, :L]
        return jnp.swapaxes(t, 1, 2).reshape((B, S) + t.shape[3:])

    return from_sub(o), from_sub(lse)


def parallel_attention_mixer(h, w_in, q_norm, k_norm, w_out, rel_bias):
    B, S, _ = h.shape
    proj = h @ w_in
    a_cols = 3 * SB_HEADS * HEAD_DIM
    sb = proj[..., :a_cols].reshape(B, S, 3, SB_HEADS, HEAD_DIM)
    dl = proj[..., a_cols:].reshape(B, S, 3, DIL_HEADS, HEAD_DIM)
    out_a = stick_breaking_attention(sb[:, :, 0], sb[:, :, 1], sb[:, :, 2])
    q = rms_norm(dl[:, :, 0], q_norm)
    k = rms_norm(dl[:, :, 1], k_norm)
    v = dl[:, :, 2]
    outs, lses = [], []
    for g, (window, dilation) in enumerate(DIL_PATTERNS):
        sl = slice(g * DIL_GROUP_HEADS, (g + 1) * DIL_GROUP_HEADS)
        o, l = dilated_window_attention(q[:, :, sl], k[:, :, sl], v[:, :, sl],
                                        rel_bias[:, sl], window, dilation)
        outs.append(o)
        lses.append(l)
    alpha = jax.nn.softmax(jnp.stack(lses, axis=0), axis=0)
    out_b = jnp.sum(alpha[..., None].astype(v.dtype) * jnp.stack(outs, axis=0), axis=0)
    merged = jnp.concatenate([out_a.reshape(B, S, -1), out_b.reshape(B, S, -1)], axis=-1)
    return merged @ w_out


def rwkv7_step(state, inp):
    r_t, w_t, k_t, v_t, a_t, b_t = inp
    sa = jnp.einsum('bhvk,bhk->bhv', state, a_t)
    state = (state * w_t[:, :, None, :] + sa[..., None] * b_t[:, :, None, :]
             + v_t[..., None] * k_t[:, :, None, :])
    return state, jnp.einsum('bhvk,bhk->bhv', state, r_t)


def rwkv7_time_mix(h, mix, w0, w1, w2, a0, a1, a2, g1, g2, k_k, k_a, r_k,
                   w_r, w_k, w_v, w_o, lnx_g, lnx_b):
    B, S, D = h.shape
    H, N = RWKV_HEADS, RWKV_HEAD
    f32 = jnp.float32
    xx = jnp.pad(h, ((0, 0), (1, 0), (0, 0)))[:, :-1] - h
    xr, xw, xk, xv, xa, xg = [h + xx * mix[i] for i in range(6)]
    r = (xr @ w_r).astype(f32)
    k = (xk @ w_k).astype(f32)
    v = (xv @ w_v).astype(f32)
    w_log = -jax.nn.softplus(-(w0 + jnp.tanh(xw @ w1) @ w2).astype(f32)) - 0.5
    decay = jnp.exp(-jnp.exp(w_log))
    a = jax.nn.sigmoid((a0 + (xa @ a1) @ a2).astype(f32))
    g = jax.nn.sigmoid(xg @ g1) @ g2
    heads = lambda t: t.reshape(B, S, H, N)
    kk = heads(k * k_k.astype(f32))
    kk = kk / jnp.maximum(jnp.sqrt(jnp.sum(kk * kk, axis=-1, keepdims=True)), 1e-12)
    k = k * (1.0 + (a - 1.0) * k_a.astype(f32))
    r_h, k_h, v_h, a_h, w_h = heads(r), heads(k), heads(v), heads(a), heads(decay)
    tm = lambda t: jnp.swapaxes(t, 0, 1)
    seqs = (tm(r_h), tm(w_h), tm(k_h), tm(v_h), tm(-kk), tm(kk * a_h))
    _, y = lax.scan(rwkv7_step, jnp.zeros((B, H, N, N), f32), seqs)
    y = jnp.swapaxes(y, 0, 1)
    mu = jnp.mean(y, axis=-1, keepdims=True)
    var = jnp.mean(jnp.square(y - mu), axis=-1, keepdims=True)
    y = ((y - mu) * lax.rsqrt(var + GN_EPS)).reshape(B, S, D)
    y = y * lnx_g.astype(f32) + lnx_b.astype(f32)
    y = y + (jnp.sum(r_h * k_h * r_k.astype(f32), axis=-1, keepdims=True) * v_h).reshape(B, S, D)
    return (y.astype(h.dtype) * g) @ w_o


def _fwd_setup_inputs(seed: int = 0) -> dict:
    key = jax.random.key(seed)
    ks = iter(jax.random.split(key, 48))
    D = D_MODEL
    nrm = lambda shape, scale: jax.random.normal(next(ks), shape, jnp.float32) * scale
    uni = lambda shape, lo, hi: jax.random.uniform(next(ks), shape, jnp.float32, lo, hi)
    return {
        "x": nrm((BATCH, SEQ, D), 1.0),
        "ffn_norm": 1.0 + nrm((DEPTH, 2, D), 0.05),
        "ffn_w_gate": nrm((DEPTH, 2, D, D_FF), D ** -0.5),
        "ffn_w_up": nrm((DEPTH, 2, D, D_FF), D ** -0.5),
        "ffn_w_down": nrm((DEPTH, 2, D_FF, D), D_FF ** -0.5),
        "mix_norm": 1.0 + nrm((DEPTH, D), 0.05),
        "rel_bias": nrm((N_BUCKETS, DIL_HEADS), 0.3),
        "attn_w_in": nrm((N_EVEN, D, 3 * N_HEADS * HEAD_DIM), D ** -0.5),
        "attn_q_norm": 1.0 + nrm((N_EVEN, HEAD_DIM), 0.05),
        "attn_k_norm": 1.0 + nrm((N_EVEN, HEAD_DIM), 0.05),
        "attn_w_out": nrm((N_EVEN, OUT_WIDTH, D), OUT_WIDTH ** -0.5),
        "rw_mix": uni((N_ODD, 6, D), 0.0, 1.0),
        "rw_w0": uni((N_ODD, D), -4.0, 0.0),
        "rw_w1": nrm((N_ODD, D, D_DECAY_LORA), D ** -0.5),
        "rw_w2": nrm((N_ODD, D_DECAY_LORA, D), 0.1 * D_DECAY_LORA ** -0.5),
        "rw_a0": nrm((N_ODD, D), 0.1),
        "rw_a1": nrm((N_ODD, D, D_AAA_LORA), D ** -0.5),
        "rw_a2": nrm((N_ODD, D_AAA_LORA, D), 0.1 * D_AAA_LORA ** -0.5),
        "rw_g1": nrm((N_ODD, D, D_GATE_LORA), D ** -0.5),
        "rw_g2": nrm((N_ODD, D_GATE_LORA, D), D_GATE_LORA ** -0.5),
        "rw_kk": 0.85 + nrm((N_ODD, D), 0.05),
        "rw_ka": 1.0 + nrm((N_ODD, D), 0.05),
        "rw_rk": nrm((N_ODD, RWKV_HEADS, RWKV_HEAD), 0.1),
        "rw_wr": nrm((N_ODD, D, D), D ** -0.5),
        "rw_wk": nrm((N_ODD, D, D), D ** -0.5),
        "rw_wv": nrm((N_ODD, D, D), D ** -0.5),
        "rw_wo": nrm((N_ODD, D, D), D ** -0.5),
        "rw_lnx_g": 1.0 + nrm((N_ODD, D), 0.05),
        "rw_lnx_b": nrm((N_ODD, D), 0.01),
    }


def _fwd_reference(x, ffn_norm, ffn_w_gate, ffn_w_up, ffn_w_down, mix_norm, rel_bias,
              attn_w_in, attn_q_norm, attn_k_norm, attn_w_out,
              rw_mix, rw_w0, rw_w1, rw_w2, rw_a0, rw_a1, rw_a2, rw_g1, rw_g2,
              rw_kk, rw_ka, rw_rk, rw_wr, rw_wk, rw_wv, rw_wo, rw_lnx_g, rw_lnx_b):
    for layer in range(DEPTH):
        x = x + 0.5 * swiglu(rms_norm(x, ffn_norm[layer, 0]), ffn_w_gate[layer, 0],
                             ffn_w_up[layer, 0], ffn_w_down[layer, 0])
        h = rms_norm(x, mix_norm[layer])
        if layer % 2 == 0:
            e = layer // 2
            x = x + parallel_attention_mixer(h, attn_w_in[e], attn_q_norm[e], attn_k_norm[e],
                                             attn_w_out[e], rel_bias)
        else:
            o = layer // 2
            x = x + rwkv7_time_mix(h, rw_mix[o], rw_w0[o], rw_w1[o], rw_w2[o], rw_a0[o],
                                   rw_a1[o], rw_a2[o], rw_g1[o], rw_g2[o], rw_kk[o], rw_ka[o],
                                   rw_rk[o], rw_wr[o], rw_wk[o], rw_wv[o], rw_wo[o],
                                   rw_lnx_g[o], rw_lnx_b[o])
        x = x + 0.5 * swiglu(rms_norm(x, ffn_norm[layer, 1]), ffn_w_gate[layer, 1],
                             ffn_w_up[layer, 1], ffn_w_down[layer, 1])
    return x


import jax as _jax
import jax.numpy as _jnp

TWIN_FORMAT = 'train_step'
FWD_PARAMS = ['x', 'ffn_norm', 'ffn_w_gate', 'ffn_w_up', 'ffn_w_down', 'mix_norm', 'rel_bias', 'attn_w_in', 'attn_q_norm', 'attn_k_norm', 'attn_w_out', 'rw_mix', 'rw_w0', 'rw_w1', 'rw_w2', 'rw_a0', 'rw_a1', 'rw_a2', 'rw_g1', 'rw_g2', 'rw_kk', 'rw_ka', 'rw_rk', 'rw_wr', 'rw_wk', 'rw_wv', 'rw_wo', 'rw_lnx_g', 'rw_lnx_b']
TWIN_WEIGHTS = ['ffn_norm', 'ffn_w_gate', 'ffn_w_up', 'ffn_w_down', 'mix_norm', 'rel_bias', 'attn_w_in', 'attn_q_norm', 'attn_k_norm', 'attn_w_out', 'rw_mix', 'rw_w0', 'rw_w1', 'rw_w2', 'rw_a0', 'rw_a1', 'rw_a2', 'rw_g1', 'rw_g2', 'rw_kk', 'rw_ka', 'rw_rk', 'rw_wr', 'rw_wk', 'rw_wv', 'rw_wo', 'rw_lnx_g', 'rw_lnx_b']
TWIN_DIFF_INPUT = 'x'
TWIN_INPUTS = ['x', 'ffn_norm', 'ffn_w_gate', 'ffn_w_up', 'ffn_w_down', 'mix_norm', 'rel_bias', 'attn_w_in', 'attn_q_norm', 'attn_k_norm', 'attn_w_out', 'rw_mix', 'rw_w0', 'rw_w1', 'rw_w2', 'rw_a0', 'rw_a1', 'rw_a2', 'rw_g1', 'rw_g2', 'rw_kk', 'rw_ka', 'rw_rk', 'rw_wr', 'rw_wk', 'rw_wv', 'rw_wo', 'rw_lnx_g', 'rw_lnx_b', 'loss_target', 'm_ffn_norm', 'm_ffn_w_gate', 'm_ffn_w_up', 'm_ffn_w_down', 'm_mix_norm', 'm_rel_bias', 'm_attn_w_in', 'm_attn_q_norm', 'm_attn_k_norm', 'm_attn_w_out', 'm_rw_mix', 'm_rw_w0', 'm_rw_w1', 'm_rw_w2', 'm_rw_a0', 'm_rw_a1', 'm_rw_a2', 'm_rw_g1', 'm_rw_g2', 'm_rw_kk', 'm_rw_ka', 'm_rw_rk', 'm_rw_wr', 'm_rw_wk', 'm_rw_wv', 'm_rw_wo', 'm_rw_lnx_g', 'm_rw_lnx_b', 'v_ffn_norm', 'v_ffn_w_gate', 'v_ffn_w_up', 'v_ffn_w_down', 'v_mix_norm', 'v_rel_bias', 'v_attn_w_in', 'v_attn_q_norm', 'v_attn_k_norm', 'v_attn_w_out', 'v_rw_mix', 'v_rw_w0', 'v_rw_w1', 'v_rw_w2', 'v_rw_a0', 'v_rw_a1', 'v_rw_a2', 'v_rw_g1', 'v_rw_g2', 'v_rw_kk', 'v_rw_ka', 'v_rw_rk', 'v_rw_wr', 'v_rw_wk', 'v_rw_wv', 'v_rw_wo', 'v_rw_lnx_g', 'v_rw_lnx_b']
TWIN_OUTPUTS = ['loss', 'grad_x', 'grad_ffn_norm', 'grad_ffn_w_gate', 'grad_ffn_w_up', 'grad_ffn_w_down', 'grad_mix_norm', 'grad_rel_bias', 'grad_attn_w_in', 'grad_attn_q_norm', 'grad_attn_k_norm', 'grad_attn_w_out', 'grad_rw_mix', 'grad_rw_w0', 'grad_rw_w1', 'grad_rw_w2', 'grad_rw_a0', 'grad_rw_a1', 'grad_rw_a2', 'grad_rw_g1', 'grad_rw_g2', 'grad_rw_kk', 'grad_rw_ka', 'grad_rw_rk', 'grad_rw_wr', 'grad_rw_wk', 'grad_rw_wv', 'grad_rw_wo', 'grad_rw_lnx_g', 'grad_rw_lnx_b', 'delta_ffn_norm', 'delta_ffn_w_gate', 'delta_ffn_w_up', 'delta_ffn_w_down', 'delta_mix_norm', 'delta_rel_bias', 'delta_attn_w_in', 'delta_attn_q_norm', 'delta_attn_k_norm', 'delta_attn_w_out', 'delta_rw_mix', 'delta_rw_w0', 'delta_rw_w1', 'delta_rw_w2', 'delta_rw_a0', 'delta_rw_a1', 'delta_rw_a2', 'delta_rw_g1', 'delta_rw_g2', 'delta_rw_kk', 'delta_rw_ka', 'delta_rw_rk', 'delta_rw_wr', 'delta_rw_wk', 'delta_rw_wv', 'delta_rw_wo', 'delta_rw_lnx_g', 'delta_rw_lnx_b', 'new_m_ffn_norm', 'new_m_ffn_w_gate', 'new_m_ffn_w_up', 'new_m_ffn_w_down', 'new_m_mix_norm', 'new_m_rel_bias', 'new_m_attn_w_in', 'new_m_attn_q_norm', 'new_m_attn_k_norm', 'new_m_attn_w_out', 'new_m_rw_mix', 'new_m_rw_w0', 'new_m_rw_w1', 'new_m_rw_w2', 'new_m_rw_a0', 'new_m_rw_a1', 'new_m_rw_a2', 'new_m_rw_g1', 'new_m_rw_g2', 'new_m_rw_kk', 'new_m_rw_ka', 'new_m_rw_rk', 'new_m_rw_wr', 'new_m_rw_wk', 'new_m_rw_wv', 'new_m_rw_wo', 'new_m_rw_lnx_g', 'new_m_rw_lnx_b', 'new_v_ffn_norm', 'new_v_ffn_w_gate', 'new_v_ffn_w_up', 'new_v_ffn_w_down', 'new_v_mix_norm', 'new_v_rel_bias', 'new_v_attn_w_in', 'new_v_attn_q_norm', 'new_v_attn_k_norm', 'new_v_attn_w_out', 'new_v_rw_mix', 'new_v_rw_w0', 'new_v_rw_w1', 'new_v_rw_w2', 'new_v_rw_a0', 'new_v_rw_a1', 'new_v_rw_a2', 'new_v_rw_g1', 'new_v_rw_g2', 'new_v_rw_kk', 'new_v_rw_ka', 'new_v_rw_rk', 'new_v_rw_wr', 'new_v_rw_wk', 'new_v_rw_wv', 'new_v_rw_wo', 'new_v_rw_lnx_g', 'new_v_rw_lnx_b']
TWIN_LEAF_KINDS = {'loss': 'loss', 'grad_x': 'grad_x', 'grad_ffn_norm': 'grad_w', 'grad_ffn_w_gate': 'grad_w', 'grad_ffn_w_up': 'grad_w', 'grad_ffn_w_down': 'grad_w', 'grad_mix_norm': 'grad_w', 'grad_rel_bias': 'grad_w', 'grad_attn_w_in': 'grad_w', 'grad_attn_q_norm': 'grad_w', 'grad_attn_k_norm': 'grad_w', 'grad_attn_w_out': 'grad_w', 'grad_rw_mix': 'grad_w', 'grad_rw_w0': 'grad_w', 'grad_rw_w1': 'grad_w', 'grad_rw_w2': 'grad_w', 'grad_rw_a0': 'grad_w', 'grad_rw_a1': 'grad_w', 'grad_rw_a2': 'grad_w', 'grad_rw_g1': 'grad_w', 'grad_rw_g2': 'grad_w', 'grad_rw_kk': 'grad_w', 'grad_rw_ka': 'grad_w', 'grad_rw_rk': 'grad_w', 'grad_rw_wr': 'grad_w', 'grad_rw_wk': 'grad_w', 'grad_rw_wv': 'grad_w', 'grad_rw_wo': 'grad_w', 'grad_rw_lnx_g': 'grad_w', 'grad_rw_lnx_b': 'grad_w', 'delta_ffn_norm': 'delta_w', 'delta_ffn_w_gate': 'delta_w', 'delta_ffn_w_up': 'delta_w', 'delta_ffn_w_down': 'delta_w', 'delta_mix_norm': 'delta_w', 'delta_rel_bias': 'delta_w', 'delta_attn_w_in': 'delta_w', 'delta_attn_q_norm': 'delta_w', 'delta_attn_k_norm': 'delta_w', 'delta_attn_w_out': 'delta_w', 'delta_rw_mix': 'delta_w', 'delta_rw_w0': 'delta_w', 'delta_rw_w1': 'delta_w', 'delta_rw_w2': 'delta_w', 'delta_rw_a0': 'delta_w', 'delta_rw_a1': 'delta_w', 'delta_rw_a2': 'delta_w', 'delta_rw_g1': 'delta_w', 'delta_rw_g2': 'delta_w', 'delta_rw_kk': 'delta_w', 'delta_rw_ka': 'delta_w', 'delta_rw_rk': 'delta_w', 'delta_rw_wr': 'delta_w', 'delta_rw_wk': 'delta_w', 'delta_rw_wv': 'delta_w', 'delta_rw_wo': 'delta_w', 'delta_rw_lnx_g': 'delta_w', 'delta_rw_lnx_b': 'delta_w', 'new_m_ffn_norm': 'new_m', 'new_m_ffn_w_gate': 'new_m', 'new_m_ffn_w_up': 'new_m', 'new_m_ffn_w_down': 'new_m', 'new_m_mix_norm': 'new_m', 'new_m_rel_bias': 'new_m', 'new_m_attn_w_in': 'new_m', 'new_m_attn_q_norm': 'new_m', 'new_m_attn_k_norm': 'new_m', 'new_m_attn_w_out': 'new_m', 'new_m_rw_mix': 'new_m', 'new_m_rw_w0': 'new_m', 'new_m_rw_w1': 'new_m', 'new_m_rw_w2': 'new_m', 'new_m_rw_a0': 'new_m', 'new_m_rw_a1': 'new_m', 'new_m_rw_a2': 'new_m', 'new_m_rw_g1': 'new_m', 'new_m_rw_g2': 'new_m', 'new_m_rw_kk': 'new_m', 'new_m_rw_ka': 'new_m', 'new_m_rw_rk': 'new_m', 'new_m_rw_wr': 'new_m', 'new_m_rw_wk': 'new_m', 'new_m_rw_wv': 'new_m', 'new_m_rw_wo': 'new_m', 'new_m_rw_lnx_g': 'new_m', 'new_m_rw_lnx_b': 'new_m', 'new_v_ffn_norm': 'new_v', 'new_v_ffn_w_gate': 'new_v', 'new_v_ffn_w_up': 'new_v', 'new_v_ffn_w_down': 'new_v', 'new_v_mix_norm': 'new_v', 'new_v_rel_bias': 'new_v', 'new_v_attn_w_in': 'new_v', 'new_v_attn_q_norm': 'new_v', 'new_v_attn_k_norm': 'new_v', 'new_v_attn_w_out': 'new_v', 'new_v_rw_mix': 'new_v', 'new_v_rw_w0': 'new_v', 'new_v_rw_w1': 'new_v', 'new_v_rw_w2': 'new_v', 'new_v_rw_a0': 'new_v', 'new_v_rw_a1': 'new_v', 'new_v_rw_a2': 'new_v', 'new_v_rw_g1': 'new_v', 'new_v_rw_g2': 'new_v', 'new_v_rw_kk': 'new_v', 'new_v_rw_ka': 'new_v', 'new_v_rw_rk': 'new_v', 'new_v_rw_wr': 'new_v', 'new_v_rw_wk': 'new_v', 'new_v_rw_wv': 'new_v', 'new_v_rw_wo': 'new_v', 'new_v_rw_lnx_g': 'new_v', 'new_v_rw_lnx_b': 'new_v'}


def _forward(args):
    return _fwd_reference(*[args[k] for k in FWD_PARAMS])


def _output_shape():
    out = _jax.eval_shape(lambda: _forward(_fwd_setup_inputs(0)))
    return out.shape, out.dtype

N_MICROBATCH = 1
ADAM_LR = 0.001
ADAM_B1 = 0.9
ADAM_B2 = 0.999
ADAM_EPS = 1e-08
ADAM_WD = 0.01
ADAM_STEP = 10
PER_EXAMPLE_BATCH_AXIS = {'x': 0, 'loss_target': 0}
SHARED_INPUTS = []
_WEIGHT_DTYPES = {'ffn_norm': _jnp.float32, 'ffn_w_gate': _jnp.float32, 'ffn_w_up': _jnp.float32, 'ffn_w_down': _jnp.float32, 'mix_norm': _jnp.float32, 'rel_bias': _jnp.float32, 'attn_w_in': _jnp.float32, 'attn_q_norm': _jnp.float32, 'attn_k_norm': _jnp.float32, 'attn_w_out': _jnp.float32, 'rw_mix': _jnp.float32, 'rw_w0': _jnp.float32, 'rw_w1': _jnp.float32, 'rw_w2': _jnp.float32, 'rw_a0': _jnp.float32, 'rw_a1': _jnp.float32, 'rw_a2': _jnp.float32, 'rw_g1': _jnp.float32, 'rw_g2': _jnp.float32, 'rw_kk': _jnp.float32, 'rw_ka': _jnp.float32, 'rw_rk': _jnp.float32, 'rw_wr': _jnp.float32, 'rw_wk': _jnp.float32, 'rw_wv': _jnp.float32, 'rw_wo': _jnp.float32, 'rw_lnx_g': _jnp.float32, 'rw_lnx_b': _jnp.float32}
MOMENT_SCALE = {'ffn_norm': 6.144318e+00, 'ffn_w_gate': 9.702471e-02, 'ffn_w_up': 9.514949e-02, 'ffn_w_down': 1.558608e-01, 'mix_norm': 6.050438e+00, 'rel_bias': 1.478720e-01, 'attn_w_in': 2.408539e-01, 'attn_q_norm': 1.179091e+00, 'attn_k_norm': 1.172657e+00, 'attn_w_out': 3.569325e-01, 'rw_mix': 2.961066e-01, 'rw_w0': 1.359884e-01, 'rw_w1': 7.540794e-03, 'rw_w2': 1.904624e-02, 'rw_a0': 6.124247e-01, 'rw_a1': 4.015769e-02, 'rw_a2': 9.882648e-02, 'rw_g1': 3.367319e-01, 'rw_g2': 8.917382e+00, 'rw_kk': 1.548095e-01, 'rw_ka': 1.222373e+00, 'rw_rk': 7.784981e+00, 'rw_wr': 2.646789e-01, 'rw_wk': 2.829811e-01, 'rw_wv': 3.531217e-01, 'rw_wo': 3.853783e-01, 'rw_lnx_g': 1.569225e+01, 'rw_lnx_b': 5.589051e-01}


def _to_microbatches(a, axis):
    t = _jnp.moveaxis(a, axis, 0)
    t = t.reshape((N_MICROBATCH, t.shape[0] // N_MICROBATCH) + t.shape[1:])
    return _jnp.moveaxis(t, 1, axis + 1)


def setup_inputs(seed: int = 0) -> dict:
    inp = _fwd_setup_inputs(seed)
    key = _jax.random.fold_in(_jax.random.key(seed), 7919)
    shape, _ = _output_shape()
    out = dict(inp)
    out["loss_target"] = _jax.random.normal(_jax.random.fold_in(key, 0), shape, _jnp.float32)
    for i, name in enumerate(TWIN_WEIGHTS):
        w = inp[name].astype(_jnp.float32)
        if MOMENT_SCALE is None:
            s = _jnp.sqrt(_jnp.mean(_jnp.square(w)) + 1e-30)
        else:
            s = MOMENT_SCALE[name]
        km, kv = _jax.random.split(_jax.random.fold_in(key, i + 1))
        out[name] = w
        out["m_" + name] = s * _jax.random.normal(km, w.shape, _jnp.float32)
        out["v_" + name] = (s * s) * _jax.random.uniform(kv, w.shape, _jnp.float32, 0.5, 1.5)
    if N_MICROBATCH > 1:
        for name, axis in PER_EXAMPLE_BATCH_AXIS.items():
            out[name] = _to_microbatches(out[name], axis)
    return {'x': out['x'], 'ffn_norm': out['ffn_norm'], 'ffn_w_gate': out['ffn_w_gate'], 'ffn_w_up': out['ffn_w_up'], 'ffn_w_down': out['ffn_w_down'], 'mix_norm': out['mix_norm'], 'rel_bias': out['rel_bias'], 'attn_w_in': out['attn_w_in'], 'attn_q_norm': out['attn_q_norm'], 'attn_k_norm': out['attn_k_norm'], 'attn_w_out': out['attn_w_out'], 'rw_mix': out['rw_mix'], 'rw_w0': out['rw_w0'], 'rw_w1': out['rw_w1'], 'rw_w2': out['rw_w2'], 'rw_a0': out['rw_a0'], 'rw_a1': out['rw_a1'], 'rw_a2': out['rw_a2'], 'rw_g1': out['rw_g1'], 'rw_g2': out['rw_g2'], 'rw_kk': out['rw_kk'], 'rw_ka': out['rw_ka'], 'rw_rk': out['rw_rk'], 'rw_wr': out['rw_wr'], 'rw_wk': out['rw_wk'], 'rw_wv': out['rw_wv'], 'rw_wo': out['rw_wo'], 'rw_lnx_g': out['rw_lnx_g'], 'rw_lnx_b': out['rw_lnx_b'], 'loss_target': out['loss_target'], 'm_ffn_norm': out['m_ffn_norm'], 'm_ffn_w_gate': out['m_ffn_w_gate'], 'm_ffn_w_up': out['m_ffn_w_up'], 'm_ffn_w_down': out['m_ffn_w_down'], 'm_mix_norm': out['m_mix_norm'], 'm_rel_bias': out['m_rel_bias'], 'm_attn_w_in': out['m_attn_w_in'], 'm_attn_q_norm': out['m_attn_q_norm'], 'm_attn_k_norm': out['m_attn_k_norm'], 'm_attn_w_out': out['m_attn_w_out'], 'm_rw_mix': out['m_rw_mix'], 'm_rw_w0': out['m_rw_w0'], 'm_rw_w1': out['m_rw_w1'], 'm_rw_w2': out['m_rw_w2'], 'm_rw_a0': out['m_rw_a0'], 'm_rw_a1': out['m_rw_a1'], 'm_rw_a2': out['m_rw_a2'], 'm_rw_g1': out['m_rw_g1'], 'm_rw_g2': out['m_rw_g2'], 'm_rw_kk': out['m_rw_kk'], 'm_rw_ka': out['m_rw_ka'], 'm_rw_rk': out['m_rw_rk'], 'm_rw_wr': out['m_rw_wr'], 'm_rw_wk': out['m_rw_wk'], 'm_rw_wv': out['m_rw_wv'], 'm_rw_wo': out['m_rw_wo'], 'm_rw_lnx_g': out['m_rw_lnx_g'], 'm_rw_lnx_b': out['m_rw_lnx_b'], 'v_ffn_norm': out['v_ffn_norm'], 'v_ffn_w_gate': out['v_ffn_w_gate'], 'v_ffn_w_up': out['v_ffn_w_up'], 'v_ffn_w_down': out['v_ffn_w_down'], 'v_mix_norm': out['v_mix_norm'], 'v_rel_bias': out['v_rel_bias'], 'v_attn_w_in': out['v_attn_w_in'], 'v_attn_q_norm': out['v_attn_q_norm'], 'v_attn_k_norm': out['v_attn_k_norm'], 'v_attn_w_out': out['v_attn_w_out'], 'v_rw_mix': out['v_rw_mix'], 'v_rw_w0': out['v_rw_w0'], 'v_rw_w1': out['v_rw_w1'], 'v_rw_w2': out['v_rw_w2'], 'v_rw_a0': out['v_rw_a0'], 'v_rw_a1': out['v_rw_a1'], 'v_rw_a2': out['v_rw_a2'], 'v_rw_g1': out['v_rw_g1'], 'v_rw_g2': out['v_rw_g2'], 'v_rw_kk': out['v_rw_kk'], 'v_rw_ka': out['v_rw_ka'], 'v_rw_rk': out['v_rw_rk'], 'v_rw_wr': out['v_rw_wr'], 'v_rw_wk': out['v_rw_wk'], 'v_rw_wv': out['v_rw_wv'], 'v_rw_wo': out['v_rw_wo'], 'v_rw_lnx_g': out['v_rw_lnx_g'], 'v_rw_lnx_b': out['v_rw_lnx_b']}


def _loss(weights, diff, rest, loss_target):
    with _jax.named_scope("forward"):
        args = {**rest, TWIN_DIFF_INPUT: diff, **{k: w.astype(_WEIGHT_DTYPES[k]) for k, w in weights.items()}}
        y = _forward(args)
    with _jax.named_scope("loss_head"):
        err = _jnp.square(y.astype(_jnp.float32) - loss_target)
        return 0.5 * _jnp.sum(_jnp.mean(err, axis=-1)) if err.ndim else 0.5 * err


def _adamw(w, g, m, v):
    m = ADAM_B1 * m + (1.0 - ADAM_B1) * g
    v = ADAM_B2 * v + (1.0 - ADAM_B2) * _jnp.square(g)
    m_hat = m / (1.0 - ADAM_B1 ** ADAM_STEP)
    v_hat = v / (1.0 - ADAM_B2 ** ADAM_STEP)
    delta = -ADAM_LR * (m_hat / (_jnp.sqrt(v_hat) + ADAM_EPS) + ADAM_WD * w)
    return delta, m, v


def reference(x, ffn_norm, ffn_w_gate, ffn_w_up, ffn_w_down, mix_norm, rel_bias, attn_w_in, attn_q_norm, attn_k_norm, attn_w_out, rw_mix, rw_w0, rw_w1, rw_w2, rw_a0, rw_a1, rw_a2, rw_g1, rw_g2, rw_kk, rw_ka, rw_rk, rw_wr, rw_wk, rw_wv, rw_wo, rw_lnx_g, rw_lnx_b, loss_target, m_ffn_norm, m_ffn_w_gate, m_ffn_w_up, m_ffn_w_down, m_mix_norm, m_rel_bias, m_attn_w_in, m_attn_q_norm, m_attn_k_norm, m_attn_w_out, m_rw_mix, m_rw_w0, m_rw_w1, m_rw_w2, m_rw_a0, m_rw_a1, m_rw_a2, m_rw_g1, m_rw_g2, m_rw_kk, m_rw_ka, m_rw_rk, m_rw_wr, m_rw_wk, m_rw_wv, m_rw_wo, m_rw_lnx_g, m_rw_lnx_b, v_ffn_norm, v_ffn_w_gate, v_ffn_w_up, v_ffn_w_down, v_mix_norm, v_rel_bias, v_attn_w_in, v_attn_q_norm, v_attn_k_norm, v_attn_w_out, v_rw_mix, v_rw_w0, v_rw_w1, v_rw_w2, v_rw_a0, v_rw_a1, v_rw_a2, v_rw_g1, v_rw_g2, v_rw_kk, v_rw_ka, v_rw_rk, v_rw_wr, v_rw_wk, v_rw_wv, v_rw_wo, v_rw_lnx_g, v_rw_lnx_b):
    given = dict(x=x, ffn_norm=ffn_norm, ffn_w_gate=ffn_w_gate, ffn_w_up=ffn_w_up, ffn_w_down=ffn_w_down, mix_norm=mix_norm, rel_bias=rel_bias, attn_w_in=attn_w_in, attn_q_norm=attn_q_norm, attn_k_norm=attn_k_norm, attn_w_out=attn_w_out, rw_mix=rw_mix, rw_w0=rw_w0, rw_w1=rw_w1, rw_w2=rw_w2, rw_a0=rw_a0, rw_a1=rw_a1, rw_a2=rw_a2, rw_g1=rw_g1, rw_g2=rw_g2, rw_kk=rw_kk, rw_ka=rw_ka, rw_rk=rw_rk, rw_wr=rw_wr, rw_wk=rw_wk, rw_wv=rw_wv, rw_wo=rw_wo, rw_lnx_g=rw_lnx_g, rw_lnx_b=rw_lnx_b, loss_target=loss_target, m_ffn_norm=m_ffn_norm, m_ffn_w_gate=m_ffn_w_gate, m_ffn_w_up=m_ffn_w_up, m_ffn_w_down=m_ffn_w_down, m_mix_norm=m_mix_norm, m_rel_bias=m_rel_bias, m_attn_w_in=m_attn_w_in, m_attn_q_norm=m_attn_q_norm, m_attn_k_norm=m_attn_k_norm, m_attn_w_out=m_attn_w_out, m_rw_mix=m_rw_mix, m_rw_w0=m_rw_w0, m_rw_w1=m_rw_w1, m_rw_w2=m_rw_w2, m_rw_a0=m_rw_a0, m_rw_a1=m_rw_a1, m_rw_a2=m_rw_a2, m_rw_g1=m_rw_g1, m_rw_g2=m_rw_g2, m_rw_kk=m_rw_kk, m_rw_ka=m_rw_ka, m_rw_rk=m_rw_rk, m_rw_wr=m_rw_wr, m_rw_wk=m_rw_wk, m_rw_wv=m_rw_wv, m_rw_wo=m_rw_wo, m_rw_lnx_g=m_rw_lnx_g, m_rw_lnx_b=m_rw_lnx_b, v_ffn_norm=v_ffn_norm, v_ffn_w_gate=v_ffn_w_gate, v_ffn_w_up=v_ffn_w_up, v_ffn_w_down=v_ffn_w_down, v_mix_norm=v_mix_norm, v_rel_bias=v_rel_bias, v_attn_w_in=v_attn_w_in, v_attn_q_norm=v_attn_q_norm, v_attn_k_norm=v_attn_k_norm, v_attn_w_out=v_attn_w_out, v_rw_mix=v_rw_mix, v_rw_w0=v_rw_w0, v_rw_w1=v_rw_w1, v_rw_w2=v_rw_w2, v_rw_a0=v_rw_a0, v_rw_a1=v_rw_a1, v_rw_a2=v_rw_a2, v_rw_g1=v_rw_g1, v_rw_g2=v_rw_g2, v_rw_kk=v_rw_kk, v_rw_ka=v_rw_ka, v_rw_rk=v_rw_rk, v_rw_wr=v_rw_wr, v_rw_wk=v_rw_wk, v_rw_wv=v_rw_wv, v_rw_wo=v_rw_wo, v_rw_lnx_g=v_rw_lnx_g, v_rw_lnx_b=v_rw_lnx_b)
    weights = {n: given[n] for n in TWIN_WEIGHTS}
    shared = {n: given[n] for n in SHARED_INPUTS}
    per_example = {n: given[n] for n in ['x']}
    grad_fn = _jax.value_and_grad(_loss, argnums=(0, 1))

    def one_microbatch(ex, loss_target):
        ex = dict(ex)
        diff = ex.pop(TWIN_DIFF_INPUT)
        return grad_fn(weights, diff, {**shared, **ex}, loss_target)

    if N_MICROBATCH == 1:
        loss, (grad_w, grad_x) = one_microbatch(per_example, given["loss_target"])
    else:
        def body(carry, xs):
            loss_sum, grad_sum = carry
            l_k, (gw_k, gx_k) = one_microbatch(xs[0], xs[1])
            with _jax.named_scope("update"):
                return (loss_sum + l_k, _jax.tree.map(_jnp.add, grad_sum, gw_k)), gx_k

        init = (_jnp.zeros((), _jnp.float32), _jax.tree.map(_jnp.zeros_like, weights))
        (loss, grad_w), grad_x = _jax.lax.scan(body, init, (per_example, given["loss_target"]))
    with _jax.named_scope("update"):
        delta_w, new_m, new_v = {}, {}, {}
        for n in TWIN_WEIGHTS:
            delta_w[n], new_m[n], new_v[n] = _adamw(weights[n], grad_w[n], given["m_" + n], given["v_" + n])
    return (loss, grad_x, *[grad_w[n] for n in TWIN_WEIGHTS], *[delta_w[n] for n in TWIN_WEIGHTS],
            *[new_m[n] for n in TWIN_WEIGHTS], *[new_v[n] for n in TWIN_WEIGHTS])
```

```python
import functools
import math

import numpy as np
import jax
import jax.numpy as jnp
from jax import lax
from jax.experimental import pallas as pl
from jax.experimental.pallas import tpu as pltpu

F32, BF16 = jnp.float32, jnp.bfloat16
HI = lax.Precision.HIGH

N_DEV = 8
D_MODEL = 1024
HEAD = 64
N_HEADS = 16
SB_HEADS = 4
DIL_GROUP = 4
DIL_PATTERNS = ((128, 1), (512, 4), (2048, 16))
QBLK = 128
N_BUCKETS = 32
MAX_DISTANCE = 2048
NORM_EPS = 1e-6
GN_EPS = 64e-5
NEG_INF = -1e30
RW_CHUNK = 64
RW_HB = 16
ADAM_LR, ADAM_B1, ADAM_B2, ADAM_EPS, ADAM_WD, ADAM_STEP = 0.001, 0.9, 0.999, 1e-08, 0.01, 10
MESH_AXES = ("x", "y", "c")
VMEM_LIMIT_BYTES = 56 * 1024 * 1024

NN2 = (((1,), (0,)), ((), ()))
NT2 = (((1,), (1,)), ((), ()))
TN2 = (((0,), (0,)), ((), ()))
NN3 = (((2,), (1,)), ((0,), (0,)))
NT3 = (((2,), (2,)), ((0,), (0,)))
TN3 = (((1,), (1,)), ((0,), (0,)))


def _dot(a, b, dims=NN2, prec=None):
    return lax.dot_general(a, b, dims, precision=prec, preferred_element_type=F32)


def _params(sem=None):
    return pltpu.CompilerParams(dimension_semantics=sem, vmem_limit_bytes=VMEM_LIMIT_BYTES)


@jax.custom_vjp
def _mm(x, w):
    return _dot(x.astype(BF16), w.astype(BF16))


def _mm_fwd(x, w):
    return _mm(x, w), (x, w)


def _mm_bwd(res, dy):
    x, w = res
    dyb = dy.astype(BF16)
    return (_dot(dyb, w.astype(BF16), NT2).astype(x.dtype), _dot(x.astype(BF16), dyb, TN2).astype(w.dtype))


_mm.defvjp(_mm_fwd, _mm_bwd)


def _rms(x, g):
    return x * lax.rsqrt(jnp.mean(x * x, axis=-1, keepdims=True) + NORM_EPS) * g


def _log_sigmoid(z):
    return jnp.minimum(z, 0.0) - jnp.log(1.0 + jnp.exp(-jnp.abs(z)))


def _heads_to_nat(v3):
    return jnp.concatenate([v3[h] for h in range(v3.shape[0])], axis=-1)


def _nat_to_heads(v2):
    return jnp.stack([v2[:, h * HEAD:(h + 1) * HEAD] for h in range(v2.shape[1] // HEAD)], axis=0)


def _exchange(name, gathers, scatters):
    n_g = len(gathers)
    arrays = list(gathers) + list(scatters)
    n = len(arrays)
    out_shape = [jax.ShapeDtypeStruct((N_DEV,) + a.shape, a.dtype) for a in gathers]
    out_shape += [jax.ShapeDtypeStruct(a.shape, a.dtype) for a in scatters]

    def body(*refs):
        ins, outs = refs[:n], refs[n:2 * n]
        send_sems, recv_sems, local_sems = refs[2 * n:]
        x, y, c = lax.axis_index("x"), lax.axis_index("y"), lax.axis_index("c")
        me = 4 * x + 2 * y + c

        def src(i, idx):
            return ins[i] if i < n_g else ins[i].at[idx]

        local = [pltpu.make_async_copy(src(i, me), outs[i].at[me], local_sems.at[i]) for i in range(n)]
        for cp in local:
            cp.start()
        remote = []
        for m in range(1, N_DEV):
            px, py, pc = x ^ ((m >> 2) & 1), y ^ ((m >> 1) & 1), c ^ (m & 1)
            peer = 4 * px + 2 * py + pc
            for i in range(n):
                cp = pltpu.make_async_remote_copy(
                    src_ref=src(i, peer), dst_ref=outs[i].at[me],
                    send_sem=send_sems.at[i, m - 1], recv_sem=recv_sems.at[i, m - 1],
                    device_id=(px, py, pc), device_id_type=pl.DeviceIdType.MESH)
                cp.start()
                arrival = pltpu.make_async_remote_copy(
                    src_ref=src(i, peer), dst_ref=outs[i].at[peer],
                    send_sem=send_sems.at[i, m - 1], recv_sem=recv_sems.at[i, m - 1],
                    device_id=(px, py, pc), device_id_type=pl.DeviceIdType.MESH)
                remote.append((cp, arrival))
        for cp, arrival in remote:
            cp.wait_send()
            arrival.wait_recv()
        for cp in local:
            cp.wait()

    hbm = pl.BlockSpec(memory_space=pltpu.HBM)
    outs = pl.pallas_call(
        body, name=name, out_shape=out_shape,
        in_specs=[hbm] * n, out_specs=[hbm] * n,
        scratch_shapes=[pltpu.SemaphoreType.DMA((n, N_DEV - 1)), pltpu.SemaphoreType.DMA((n, N_DEV - 1)),
                        pltpu.SemaphoreType.DMA((n,))],
    )(*arrays)
    return list(outs[:n_g]), list(outs[n_g:])


def _mesh_peers():
    x, y, c = lax.axis_index("x"), lax.axis_index("y"), lax.axis_index("c")
    peers = []
    for m in range(1, N_DEV):
        px, py, pc = x ^ ((m >> 2) & 1), y ^ ((m >> 1) & 1), c ^ (m & 1)
        peers.append((m, (px, py, pc), 4 * px + 2 * py + pc))
    return 4 * x + 2 * y + c, peers


_HBM_SPEC = pl.BlockSpec(memory_space=pltpu.HBM)
_SEM_SPEC = pl.BlockSpec(memory_space=pltpu.SEMAPHORE)
_DATAFLOW = pltpu.SideEffectType.DATAFLOW_SIDE_EFFECTING


def _exchange_start(name, gathers, scatters):
    n_g = len(gathers)
    arrays = list(gathers) + list(scatters)
    n = len(arrays)
    lands = ([lax.empty((N_DEV,) + a.shape, a.dtype) for a in gathers] + [lax.empty(a.shape, a.dtype) for a in scatters])

    def body(*refs):
        ins, land = refs[:n], refs[n:2 * n]
        send_sems, recv_sems, local_sems, token = refs[2 * n], refs[2 * n + 1], refs[2 * n + 2], refs[-1]
        me, peers = _mesh_peers()
        for m, dev, peer in peers:
            for i in range(n):
                k = i * (N_DEV - 1) + m - 1
                pltpu.make_async_remote_copy(
                    src_ref=ins[i] if i < n_g else ins[i].at[peer], dst_ref=land[i].at[me],
                    send_sem=send_sems.at[k], recv_sem=recv_sems.at[k],
                    device_id=dev, device_id_type=pl.DeviceIdType.MESH).start()
        for i in range(n):
            pltpu.make_async_copy(ins[i] if i < n_g else ins[i].at[me], land[i].at[me], local_sems.at[i]).start()
        token[...] = jnp.zeros_like(token)

    sem = pltpu.SemaphoreType.DMA((n * (N_DEV - 1),))
    outs = pl.pallas_call(
        body, name=name,
        out_shape=([sem, sem, pltpu.SemaphoreType.DMA((n,))] + [pltpu.HBM(a.shape, a.dtype) for a in arrays]
                   + [pltpu.HBM(l.shape, l.dtype) for l in lands] + [jax.ShapeDtypeStruct((8, 128), F32)]),
        in_specs=[_HBM_SPEC] * (2 * n),
        out_specs=[_SEM_SPEC] * 3 + [_HBM_SPEC] * (2 * n) + [pl.BlockSpec(memory_space=pltpu.VMEM)],
        input_output_aliases={i: i + 3 for i in range(2 * n)},
        compiler_params=pltpu.CompilerParams(has_side_effects=_DATAFLOW),
    )(*[pltpu.with_memory_space_constraint(a, pltpu.HBM) for a in arrays],
      *[pltpu.with_memory_space_constraint(l, pltpu.HBM) for l in lands])
    return dict(n_g=n_g, n=n, send=outs[0], recv=outs[1], local=outs[2], srcs=list(outs[3:3 + n]),
                lands=list(outs[3 + n:3 + 2 * n]), token=outs[-1])


def _exchange_wait(name, started, after):
    n, n_g = started["n"], started["n_g"]

    def body(*refs):
        srcs, lands = refs[:n], refs[n:2 * n]
        send_sems, recv_sems, local_sems = refs[2 * n], refs[2 * n + 1], refs[2 * n + 2]
        me, peers = _mesh_peers()
        local = [pltpu.make_async_copy(srcs[i] if i < n_g else srcs[i].at[me], lands[i].at[me], local_sems.at[i])
                 for i in range(n)]
        for m, dev, peer in peers:
            for i in range(n):
                k = i * (N_DEV - 1) + m - 1
                cp = pltpu.make_async_remote_copy(
                    src_ref=srcs[i] if i < n_g else srcs[i].at[peer], dst_ref=lands[i].at[peer],
                    send_sem=send_sems.at[k], recv_sem=recv_sems.at[k],
                    device_id=dev, device_id_type=pl.DeviceIdType.MESH)
                cp.wait_send()
                cp.wait_recv()
        for cp in local:
            cp.wait()

    outs = pl.pallas_call(
        body, name=name,
        out_shape=([pltpu.HBM(a.shape, a.dtype) for a in started["srcs"]]
                   + [pltpu.HBM(l.shape, l.dtype) for l in started["lands"]]),
        in_specs=[_HBM_SPEC] * (2 * n) + [_SEM_SPEC] * 3 + [pl.BlockSpec(memory_space=pl.ANY)],
        out_specs=[_HBM_SPEC] * (2 * n), input_output_aliases={i: i for i in range(2 * n)},
        compiler_params=pltpu.CompilerParams(has_side_effects=_DATAFLOW),
    )(*started["srcs"], *started["lands"], started["send"], started["recv"], started["local"], after)
    return list(outs[n:n + n_g]), list(outs[n + n_g:])


def _tile_spec(shape, layout, t):
    if layout == "nat":
        return pl.BlockSpec((t, shape[1]), lambda i: (i, 0))
    return pl.BlockSpec((shape[0], t, shape[2]), lambda i: (0, i, 0))


def _full_spec(shape):
    nd = len(shape)
    return pl.BlockSpec(tuple(shape), lambda i: (0,) * nd)


def _seq_len(a, layout):
    return a.shape[0] if layout == "nat" else a.shape[1]


def _tile_fwd(name, f, tiled, params, outs, t):
    nt, npar = len(tiled), len(params)
    s = _seq_len(*tiled[0])

    def body(*refs):
        vals = [r[...] for r in refs[:nt + npar]]
        res = f(*vals)
        for r, o in zip(refs[nt + npar:], res):
            r[...] = o.astype(r.dtype)

    return pl.pallas_call(
        body, name=name, grid=(s // t,),
        in_specs=[_tile_spec(a.shape, l, t) for a, l in tiled] + [_full_spec(p.shape) for p in params],
        out_specs=[_tile_spec(sh, l, t) for sh, _, l in outs],
        out_shape=[jax.ShapeDtypeStruct(sh, dt) for sh, dt, _ in outs],
        compiler_params=_params(("arbitrary",)),
    )(*[a for a, _ in tiled], *params)


def _tile_bwd(name, f, tiled, params, cts, t, need, adds=None):
    nt, npar, nc = len(tiled), len(params), len(cts)
    s = _seq_len(*tiled[0])
    need_idx = [k for k in range(nt) if need[k]]
    adds = adds or [None] * len(need_idx)
    add_arrays = [(a, tiled[k][1]) for a, k in zip(adds, need_idx) if a is not None]
    n_add = len(add_arrays)

    def body(*refs):
        i = pl.program_id(0)
        vals = [r[...] for r in refs[:nt + npar]]
        ct_refs = refs[nt + npar:nt + npar + nc]
        add_refs = refs[nt + npar + nc:nt + npar + nc + n_add]
        out_refs = refs[nt + npar + nc + n_add:]
        res, vjp = jax.vjp(f, *vals)
        grads = vjp(tuple(r[...].astype(o.dtype) for r, o in zip(ct_refs, res)))
        a = 0
        for j, k in enumerate(need_idx):
            g = grads[k]
            if adds[j] is not None:
                g = g + add_refs[a][...]
                a += 1
            out_refs[j][...] = g.astype(out_refs[j].dtype)
        for j in range(npar):
            r = out_refs[len(need_idx) + j]

            @pl.when(i == 0)
            def _():
                r[...] = jnp.zeros_like(r)

            r[...] += grads[nt + j]

    outs = pl.pallas_call(
        body, name=name, grid=(s // t,),
        in_specs=([_tile_spec(a.shape, l, t) for a, l in tiled] + [_full_spec(p.shape) for p in params]
                  + [_tile_spec(a.shape, l, t) for a, l in cts] + [_tile_spec(a.shape, l, t) for a, l in add_arrays]),
        out_specs=([_tile_spec(tiled[k][0].shape, tiled[k][1], t) for k in need_idx]
                   + [_full_spec(p.shape) for p in params]),
        out_shape=([jax.ShapeDtypeStruct(tiled[k][0].shape, F32) for k in need_idx]
                   + [jax.ShapeDtypeStruct(p.shape, F32) for p in params]),
        compiler_params=_params(("arbitrary",)),
    )(*[a for a, _ in tiled], *params, *[a for a, _ in cts], *[a for a, _ in add_arrays])
    return list(outs[:len(need_idx)]), list(outs[len(need_idx):])


def _linear_fwd(name, x, w, t, out_layout="nat", residual=None):
    s, k = x.shape
    n = w.shape[1]
    has_res = residual is not None

    def body(*refs):
        x_ref, w_ref = refs[0], refs[1]
        o_ref = refs[-1]
        y = _dot(x_ref[...].astype(BF16), w_ref[...])
        if has_res:
            y = y + refs[2][...]
        if out_layout == "hm":
            for h in range(n // HEAD):
                o_ref[h] = y[:, h * HEAD:(h + 1) * HEAD]
        else:
            o_ref[...] = y

    out_sh = (s, n) if out_layout == "nat" else (n // HEAD, s, HEAD)
    ins = [x, w] + ([residual] if has_res else [])
    in_specs = [_tile_spec(x.shape, "nat", t), _full_spec(w.shape)] + ([_tile_spec((s, n), "nat", t)] if has_res else [])
    return pl.pallas_call(
        body, name=name, grid=(s // t,), in_specs=in_specs,
        out_specs=_tile_spec(out_sh, out_layout, t), out_shape=jax.ShapeDtypeStruct(out_sh, F32),
        compiler_params=_params(("arbitrary",)),
    )(*ins)


def _linear_dx(name, dy, w, t, dy_layout="nat"):
    k, n = w.shape
    s = _seq_len(dy, dy_layout)

    def body(dy_ref, w_ref, o_ref):
        dy = _heads_to_nat(dy_ref[...].astype(BF16)) if dy_layout == "hm" else dy_ref[...].astype(BF16)
        o_ref[...] = _dot(dy, w_ref[...], NT2)

    return pl.pallas_call(
        body, name=name, grid=(s // t,),
        in_specs=[_tile_spec(dy.shape, dy_layout, t), _full_spec(w.shape)],
        out_specs=_tile_spec((s, k), "nat", t), out_shape=jax.ShapeDtypeStruct((s, k), F32),
        compiler_params=_params(("arbitrary",)),
    )(dy, w)


def _linear_dw(name, x, dy, t, nb, dy_layout="nat"):
    s, k = x.shape
    n = dy.shape[1] if dy_layout == "nat" else dy.shape[0] * HEAD

    def body(x_ref, dy_ref, o_ref):
        i = pl.program_id(1)

        @pl.when(i == 0)
        def _():
            o_ref[...] = jnp.zeros_like(o_ref)

        dy = _heads_to_nat(dy_ref[...].astype(BF16)) if dy_layout == "hm" else dy_ref[...].astype(BF16)
        o_ref[...] += _dot(x_ref[...].astype(BF16), dy, TN2)

    if dy_layout == "hm":
        dy_spec = pl.BlockSpec((nb // HEAD, t, HEAD), lambda j, i: (j, i, 0))
    else:
        dy_spec = pl.BlockSpec((t, nb), lambda j, i: (i, j))
    return pl.pallas_call(
        body, name=name, grid=(n // nb, s // t),
        in_specs=[pl.BlockSpec((t, k), lambda j, i: (i, 0)), dy_spec],
        out_specs=pl.BlockSpec((k, nb), lambda j, i: (0, j)), out_shape=jax.ShapeDtypeStruct((k, n), F32),
        compiler_params=_params(("arbitrary", "arbitrary")),
    )(x, dy)


def _ffn_fwd(name, x, gn, wg, wu, wd, t):
    s, d = x.shape
    f8 = wg.shape[-1]

    def body(x_ref, g_ref, wg_ref, wu_ref, wd_ref, o_ref, gk_ref, uk_ref, h_scr, acc):
        k = pl.program_id(1)

        @pl.when(k == 0)
        def _():
            h_scr[...] = _rms(x_ref[...], g_ref[...]).astype(BF16)
            acc[...] = jnp.zeros_like(acc)

        hb = h_scr[...]
        gk = _dot(hb, wg_ref[0])
        uk = _dot(hb, wu_ref[0])
        gk_ref[0] = gk
        uk_ref[0] = uk
        a = gk * jax.nn.sigmoid(gk) * uk
        acc[...] += _dot(a.astype(BF16), wd_ref[0])

        @pl.when(k == N_DEV - 1)
        def _():
            o_ref[...] = x_ref[...] + 0.5 * acc[...]

    wspec = lambda shp: pl.BlockSpec((1,) + shp, lambda i, k: (k, 0, 0))
    act = pl.BlockSpec((1, t, f8), lambda i, k: (k, i, 0))
    act_sh = jax.ShapeDtypeStruct((N_DEV, s, f8), F32)
    return pl.pallas_call(
        body, name=name, grid=(s // t, N_DEV),
        in_specs=[pl.BlockSpec((t, d), lambda i, k: (i, 0)), pl.BlockSpec((1, d), lambda i, k: (0, 0)),
                  wspec((d, f8)), wspec((d, f8)), wspec((f8, d))],
        out_specs=[pl.BlockSpec((t, d), lambda i, k: (i, 0)), act, act],
        out_shape=[jax.ShapeDtypeStruct((s, d), F32), act_sh, act_sh],
        scratch_shapes=[pltpu.VMEM((t, d), BF16), pltpu.VMEM((t, d), F32)],
        compiler_params=_params(("arbitrary", "arbitrary")),
    )(x, gn, wg, wu, wd)


def _ffn_bwd(name, x, dy, gn, wg, wu, wd, gact, uact, t):
    s, d = x.shape
    f8 = wg.shape[-1]
    last = N_DEV - 1

    def body(x_ref, dy_ref, g_ref, wg_ref, wu_ref, wd_ref, gk_ref, uk_ref,
             dx_ref, dg_ref, dwg_ref, dwu_ref, dwd_ref, dh_scr):
        k, i = pl.program_id(0), pl.program_id(1)
        x = x_ref[...]
        rs = lax.rsqrt(jnp.mean(x * x, axis=-1, keepdims=True) + NORM_EPS)
        xn = x * rs
        hb = (xn * g_ref[...]).astype(BF16)
        dob = (0.5 * dy_ref[...]).astype(BF16)
        wgk, wuk, wdk = wg_ref[0], wu_ref[0], wd_ref[0]
        gk, uk = gk_ref[0], uk_ref[0]
        sg = jax.nn.sigmoid(gk)
        sk = gk * sg
        da = _dot(dob, wdk, NT2)
        du = (da * sk).astype(BF16)
        dg = (da * uk * (sg * (1.0 + gk * (1.0 - sg)))).astype(BF16)

        @pl.when(i == 0)
        def _():
            dwg_ref[...] = jnp.zeros_like(dwg_ref)
            dwu_ref[...] = jnp.zeros_like(dwu_ref)
            dwd_ref[...] = jnp.zeros_like(dwd_ref)

        dwd_ref[0] += _dot((sk * uk).astype(BF16), dob, TN2)
        dwg_ref[0] += _dot(hb, dg, TN2)
        dwu_ref[0] += _dot(hb, du, TN2)
        dh = _dot(dg, wgk, NT2) + _dot(du, wuk, NT2)
        rows = pl.ds(pl.multiple_of(i * t, t), t)

        @pl.when(k == 0)
        def _():
            dh_scr[rows, :] = dh

        @pl.when(k > 0)
        def _():
            dh_scr[rows, :] += dh

        @pl.when(jnp.logical_and(k == last, i == 0))
        def _():
            dg_ref[...] = jnp.zeros_like(dg_ref)

        @pl.when(k == last)
        def _():
            dht = dh_scr[rows, :]
            dg_ref[...] += jnp.sum(dht * xn, axis=0, keepdims=True)
            dxn = dht * g_ref[...]
            dx_ref[...] = dy_ref[...] + rs * (dxn - xn * jnp.mean(dxn * xn, axis=-1, keepdims=True))

    wspec = lambda shp: pl.BlockSpec((1,) + shp, lambda k, i: (k, 0, 0))
    tile = pl.BlockSpec((t, d), lambda k, i: (i, 0))
    act = pl.BlockSpec((1, t, f8), lambda k, i: (k, i, 0))
    return pl.pallas_call(
        body, name=name, grid=(N_DEV, s // t),
        in_specs=[tile, tile, pl.BlockSpec((1, d), lambda k, i: (0, 0)), wspec((d, f8)), wspec((d, f8)), wspec((f8, d)),
                  act, act],
        out_specs=[pl.BlockSpec((t, d), lambda k, i: (jnp.where(k == last, i, 0), 0)),
                   pl.BlockSpec((1, d), lambda k, i: (0, 0)),
                   pl.BlockSpec((1, d, f8), lambda k, i: (k, 0, 0)), pl.BlockSpec((1, d, f8), lambda k, i: (k, 0, 0)),
                   pl.BlockSpec((1, f8, d), lambda k, i: (k, 0, 0))],
        out_shape=[jax.ShapeDtypeStruct((s, d), F32), jax.ShapeDtypeStruct((1, d), F32),
                   jax.ShapeDtypeStruct((N_DEV, d, f8), F32), jax.ShapeDtypeStruct((N_DEV, d, f8), F32),
                   jax.ShapeDtypeStruct((N_DEV, f8, d), F32)],
        scratch_shapes=[pltpu.VMEM((s, d), F32)],
        compiler_params=_params(("arbitrary", "arbitrary")),
    )(x, dy, gn, wg, wu, wd, gact, uact)


def _loss_head(y, target, t):
    s, d = y.shape

    def body(y_ref, t_ref, dy_ref, l_ref):
        i = pl.program_id(0)
        err = y_ref[...] - t_ref[...]
        dy_ref[...] = err * (1.0 / d)

        @pl.when(i == 0)
        def _():
            l_ref[...] = jnp.zeros_like(l_ref)

        l_ref[...] += 0.5 * jnp.sum(jnp.mean(err * err, axis=-1, keepdims=True), axis=0, keepdims=True)

    tile = pl.BlockSpec((t, d), lambda i: (i, 0))
    dy, l = pl.pallas_call(
        body, name="loss_head", grid=(s // t,), in_specs=[tile, tile],
        out_specs=[tile, pl.BlockSpec((1, 1), lambda i: (0, 0))],
        out_shape=[jax.ShapeDtypeStruct((s, d), F32), jax.ShapeDtypeStruct((1, 1), F32)],
        compiler_params=_params(("arbitrary",)),
    )(y, target)
    return dy, l[0, 0]


SB_KEY_TILE = 512


def _sb_scan_mats():
    row = lax.broadcasted_iota(jnp.int32, (QBLK, QBLK), 0)
    col = lax.broadcasted_iota(jnp.int32, (QBLK, QBLK), 1)
    return (row > col).astype(F32).astype(BF16), (row < col).astype(F32).astype(BF16)


def _sb_tile_scan(x, mat, reverse):
    nsub = x.shape[1] // QBLK
    outs, carry = [None] * nsub, jnp.zeros((x.shape[0], 1), F32)
    for i in (reversed(range(nsub)) if reverse else range(nsub)):
        xs = x[:, i * QBLK:(i + 1) * QBLK]
        hi = xs.astype(BF16)
        lo = (xs - hi.astype(F32)).astype(BF16)
        outs[i] = _dot(hi, mat) + _dot(lo, mat) + carry
        carry = carry + jnp.sum(xs, axis=1, keepdims=True)
    return jnp.concatenate(outs, axis=1), carry


def _sb_before_query(n, t, kt):
    row = lax.broadcasted_iota(jnp.int32, (QBLK, kt), 0)
    col = lax.broadcasted_iota(jnp.int32, (QBLK, kt), 1)
    return t * kt + col < n * QBLK + row


def _sb_fwd(q, k, v):
    _, s, _ = q.shape
    scale = HEAD ** -0.5
    kt = min(SB_KEY_TILE, s)

    def body(q_ref, k_ref, v_ref, o_ref):
        n = pl.program_id(1)
        qb = q_ref[0]
        suffix, _ = _sb_scan_mats()
        n_tiles = lax.div(n, jnp.int32(kt // QBLK)) + 1

        def tile(t, c, acc, diagonal):
            rows = pl.ds(pl.multiple_of(t * kt, kt), kt)
            kb, vb = k_ref[0, rows, :], v_ref[0, rows, :]
            z = _dot(qb, kb, NT2) * scale
            lk = _log_sigmoid(-z)
            log_beta = z + lk
            if diagonal:
                ok = _sb_before_query(n, t, kt)
                lk = jnp.where(ok, lk, 0.0)
            later, total = _sb_tile_scan(lk, suffix, True)
            w = jnp.exp(log_beta + later + c)
            if diagonal:
                w = jnp.where(ok, w, 0.0)
            return c + total, acc + _dot(w.astype(BF16), vb)

        carry = tile(n_tiles - 1, jnp.zeros((QBLK, 1), F32), jnp.zeros((QBLK, HEAD), F32), True)
        _, acc = lax.fori_loop(1, n_tiles, lambda jj, cr: tile(n_tiles - 1 - jj, cr[0], cr[1], False), carry)
        o_ref[0] = acc

    return pl.pallas_call(
        body, name="sb_fwd", grid=(SB_HEADS, s // QBLK),
        in_specs=[pl.BlockSpec((1, QBLK, HEAD), lambda h, n: (h, n, 0)),
                  pl.BlockSpec((1, s, HEAD), lambda h, n: (h, 0, 0)),
                  pl.BlockSpec((1, s, HEAD), lambda h, n: (h, 0, 0))],
        out_specs=pl.BlockSpec((1, QBLK, HEAD), lambda h, n: (h, n, 0)),
        out_shape=jax.ShapeDtypeStruct((SB_HEADS, s, HEAD), F32),
        compiler_params=_params(("arbitrary", "arbitrary")),
    )(q, k, v)


def _sb_bwd(q, k, v, do):
    _, s, _ = q.shape
    scale = HEAD ** -0.5
    kt = min(SB_KEY_TILE, s)

    def body(q_ref, k_ref, v_ref, do_ref, dq_ref, dk_ref, dv_ref, e_scr, beta_scr):
        n = pl.program_id(1)

        @pl.when(n == 0)
        def _():
            dk_ref[...] = jnp.zeros_like(dk_ref)
            dv_ref[...] = jnp.zeros_like(dv_ref)

        qb = q_ref[0]
        dob = do_ref[0].astype(BF16)
        suffix, prefix = _sb_scan_mats()
        n_tiles = lax.div(n, jnp.int32(kt // QBLK)) + 1

        def weights(t, c, diagonal):
            rows = pl.ds(pl.multiple_of(t * kt, kt), kt)
            kb, vb = k_ref[0, rows, :], v_ref[0, rows, :]
            z = _dot(qb, kb, NT2) * scale
            lk = _log_sigmoid(-z)
            log_beta = z + lk
            if diagonal:
                ok = _sb_before_query(n, t, kt)
                lk = jnp.where(ok, lk, 0.0)
            later, total = _sb_tile_scan(lk, suffix, True)
            w = jnp.exp(log_beta + later + c)
            if diagonal:
                w = jnp.where(ok, w, 0.0)
            e_scr[t] = w * _dot(dob, vb, NT2)
            beta_scr[t] = jnp.exp(log_beta)
            dv_ref[0, rows, :] += _dot(w.astype(BF16), dob, TN2)
            return c + total

        c_diag = weights(n_tiles - 1, jnp.zeros((QBLK, 1), F32), True)
        lax.fori_loop(1, n_tiles, lambda jj, c: weights(n_tiles - 1 - jj, c, False), c_diag)

        def grads(t, pc, dq, diagonal):
            rows = pl.ds(pl.multiple_of(t * kt, kt), kt)
            kb = k_ref[0, rows, :]
            e, beta = e_scr[t], beta_scr[t]
            before, total = _sb_tile_scan(e, prefix, False)
            dz = e * (1.0 - beta) - beta * (before + pc)
            if diagonal:
                dz = jnp.where(_sb_before_query(n, t, kt), dz, 0.0)
            dz = (dz * scale).astype(BF16)
            dk_ref[0, rows, :] += _dot(dz, qb, TN2)
            return pc + total, dq + _dot(dz, kb)

        carry = lax.fori_loop(0, n_tiles - 1, lambda t, cr: grads(t, cr[0], cr[1], False),
                              (jnp.zeros((QBLK, 1), F32), jnp.zeros((QBLK, HEAD), F32)))
        _, dq = grads(n_tiles - 1, carry[0], carry[1], True)
        dq_ref[0] = dq

    qspec = pl.BlockSpec((1, QBLK, HEAD), lambda h, n: (h, n, 0))
    full = pl.BlockSpec((1, s, HEAD), lambda h, n: (h, 0, 0))
    sh = jax.ShapeDtypeStruct((SB_HEADS, s, HEAD), F32)
    return pl.pallas_call(
        body, name="sb_bwd", grid=(SB_HEADS, s // QBLK),
        in_specs=[qspec, full, full, qspec],
        out_specs=[qspec, full, full], out_shape=[sh, sh, sh],
        scratch_shapes=[pltpu.VMEM((s // kt, QBLK, kt), F32), pltpu.VMEM((s // kt, QBLK, kt), F32)],
        compiler_params=_params(("arbitrary", "arbitrary")),
    )(q, k, v, do)


def _t5_bucket_np(dist):
    max_exact = N_BUCKETS // 2
    d = np.maximum(dist, 1).astype(np.float32)
    large = max_exact + (np.log(d / np.float32(max_exact)) / np.float32(math.log(MAX_DISTANCE / max_exact))
                         * np.float32(N_BUCKETS - max_exact)).astype(np.int32)
    large = np.minimum(large, N_BUCKETS - 1)
    return np.where(dist < max_exact, dist, large)


def _dil_layout(s):
    tiles, buckets = [], []
    i = np.arange(QBLK)[:, None]
    j = np.arange(QBLK)[None, :]
    for g, (window, r) in enumerate(DIL_PATTERNS):
        for off in range(min(window // QBLK + 1, s // QBLK)):
            dist = QBLK * off + i - j
            ok = (dist >= 0) & (dist <= window) & (dist % r == 0)
            tiles.append((g, off))
            buckets.append(np.where(ok, _t5_bucket_np(np.maximum(dist, 0)), -1).astype(np.int32))
    return tiles, np.stack(buckets)


def _bias_tiles(rel_bias, s):
    tiles, buckets = _dil_layout(s)
    nt = len(tiles)
    present = [sorted(set(np.unique(buckets[k]).tolist()) - {-1}) for k in range(nt)]

    def body(rel_ref, b_ref, o_ref):
        j = pl.program_id(0)
        for k, (g, _) in enumerate(tiles):
            bk = b_ref[k]
            tile = jnp.full((QBLK, QBLK), NEG_INF, F32)
            for b in present[k]:
                tile = jnp.where(bk == b, rel_ref[b, g * DIL_GROUP + j], tile)
            o_ref[0, k] = tile

    return pl.pallas_call(
        body, name="bias_tiles", grid=(DIL_GROUP,),
        in_specs=[pl.BlockSpec(memory_space=pltpu.SMEM), pl.BlockSpec((nt, QBLK, QBLK), lambda j: (0, 0, 0))],
        out_specs=pl.BlockSpec((1, nt, QBLK, QBLK), lambda j: (j, 0, 0, 0)),
        out_shape=jax.ShapeDtypeStruct((DIL_GROUP, nt, QBLK, QBLK), F32),
        compiler_params=_params(("arbitrary",)),
    )(rel_bias, jnp.asarray(buckets))


def _bias_tiles_bwd(dbias, s):
    tiles, buckets = _dil_layout(s)
    nt = len(tiles)
    present = [sorted(set(np.unique(buckets[k]).tolist()) - {-1}) for k in range(nt)]

    def body(d_ref, b_ref, o_ref):
        j = pl.program_id(0)

        @pl.when(j == 0)
        def _():
            for b in range(N_BUCKETS):
                for col in range(3 * DIL_GROUP):
                    o_ref[b, col] = jnp.float32(0.0)

        for k, (g, _) in enumerate(tiles):
            bk, dk = b_ref[k], d_ref[0, k]
            for b in present[k]:
                o_ref[b, g * DIL_GROUP + j] += jnp.sum(jnp.where(bk == b, dk, 0.0))

    return pl.pallas_call(
        body, name="bias_tiles_bwd", grid=(DIL_GROUP,),
        in_specs=[pl.BlockSpec((1, nt, QBLK, QBLK), lambda j: (j, 0, 0, 0)),
                  pl.BlockSpec((nt, QBLK, QBLK), lambda j: (0, 0, 0))],
        out_specs=pl.BlockSpec(memory_space=pltpu.SMEM),
        out_shape=jax.ShapeDtypeStruct((N_BUCKETS, 3 * DIL_GROUP), F32),
        compiler_params=_params(("arbitrary",)),
    )(dbias, jnp.asarray(buckets))


def _dil_fn(tiles, qs, kblks, vblks, bias, valid):
    scale = HEAD ** -0.5
    logits = []
    for k, (g, _) in enumerate(tiles):
        l = _bdot(qs[g], kblks[k], NT2) * scale + bias[k]
        logits.append(jnp.where(valid[k] > 0.5, l, NEG_INF))
    m = lax.stop_gradient(jnp.max(functools.reduce(jnp.maximum, logits), axis=1, keepdims=True))
    ps = [jnp.exp(l - m) for l in logits]
    den = jnp.sum(functools.reduce(jnp.add, ps), axis=1, keepdims=True)
    inv = 1.0 / den
    out = jnp.zeros((QBLK, HEAD), F32)
    for k in range(len(tiles)):
        out = out + _bdot(ps[k] * inv, vblks[k], NN2)
    return out


def _dil_specs(s, nt):
    qspecs = [pl.BlockSpec((1, QBLK, HEAD), functools.partial(lambda j, n, g: (DIL_GROUP * g + j, n, 0), g=g))
              for g in range(3)]
    kvspecs = [pl.BlockSpec((1, s, HEAD), functools.partial(lambda j, n, g: (DIL_GROUP * g + j, 0, 0), g=g),
                            pipeline_mode=pl.Buffered(1)) for g in range(3)]
    return qspecs + kvspecs + kvspecs + [pl.BlockSpec((1, nt, QBLK, QBLK), lambda j, n: (j, 0, 0, 0))]


def _dil_load(tiles, n, q_refs, k_refs, v_refs):
    qs = [r[0].astype(F32) for r in q_refs]
    kblks, vblks, valid, rows = [], [], [], []
    for g, off in tiles:
        kb = jnp.maximum(n - off, 0)
        rw = pl.ds(pl.multiple_of(kb * QBLK, QBLK), QBLK)
        rows.append(rw)
        kblks.append(k_refs[g][0, rw, :].astype(F32))
        vblks.append(v_refs[g][0, rw, :].astype(F32))
        valid.append((n >= off).astype(F32))
    return qs, kblks, vblks, valid, rows


def _dil_fwd(qn, kn, v, bias):
    _, s, _ = qn.shape
    tiles, _ = _dil_layout(s)
    nt = len(tiles)

    def body(*refs):
        q_refs, k_refs, v_refs, b_ref, o_ref = refs[0:3], refs[3:6], refs[6:9], refs[9], refs[10]
        n = pl.program_id(1)
        qs, kblks, vblks, valid, _ = _dil_load(tiles, n, q_refs, k_refs, v_refs)
        o_ref[0] = _dil_fn(tiles, qs, kblks, vblks, b_ref[0], valid)

    return pl.pallas_call(
        body, name="dil_fwd", grid=(DIL_GROUP, s // QBLK), in_specs=_dil_specs(s, nt),
        out_specs=pl.BlockSpec((1, QBLK, HEAD), lambda j, n: (j, n, 0)),
        out_shape=jax.ShapeDtypeStruct((DIL_GROUP, s, HEAD), F32),
        compiler_params=_params(("arbitrary", "arbitrary")),
    )(qn, qn, qn, kn, kn, kn, v, v, v, bias)


def _dil_bwd(qn, kn, v, bias, do):
    _, s, _ = qn.shape
    tiles, _ = _dil_layout(s)
    nt = len(tiles)

    def body(*refs):
        q_refs, k_refs, v_refs, b_ref, do_ref = refs[0:3], refs[3:6], refs[6:9], refs[9], refs[10]
        dq_ref, dk_ref, dv_ref, db_ref = refs[11:]
        n = pl.program_id(1)

        @pl.when(n == 0)
        def _():
            for r in (dk_ref, dv_ref, db_ref):
                r[...] = jnp.zeros_like(r)

        qs, kblks, vblks, valid, rows = _dil_load(tiles, n, q_refs, k_refs, v_refs)
        fn = functools.partial(_dil_fn, tiles)
        _, vjp = jax.vjp(lambda a, b, c, d: fn(a, b, c, d, valid), qs, kblks, vblks, b_ref[0])
        dqs, dks, dvs, db = vjp(do_ref[0])
        for g in range(3):
            dq_ref[g, 0] = dqs[g]
        for k, (g, _) in enumerate(tiles):
            dk_ref[g, 0, rows[k], :] += dks[k]
            dv_ref[g, 0, rows[k], :] += dvs[k]
        db_ref[0] += db

    gsh = jax.ShapeDtypeStruct((3, DIL_GROUP, s, HEAD), F32)
    full = pl.BlockSpec((3, 1, s, HEAD), lambda j, n: (0, j, 0, 0), pipeline_mode=pl.Buffered(1))
    dq, dk, dv, db = pl.pallas_call(
        body, name="dil_bwd", grid=(DIL_GROUP, s // QBLK),
        in_specs=_dil_specs(s, nt) + [pl.BlockSpec((1, QBLK, HEAD), lambda j, n: (j, n, 0))],
        out_specs=[pl.BlockSpec((3, 1, QBLK, HEAD), lambda j, n: (0, j, n, 0)), full, full,
                   pl.BlockSpec((1, nt, QBLK, QBLK), lambda j, n: (j, 0, 0, 0))],
        out_shape=[gsh, gsh, gsh, jax.ShapeDtypeStruct((DIL_GROUP, nt, QBLK, QBLK), F32)],
        compiler_params=_params(("arbitrary", "arbitrary")),
    )(qn, qn, qn, kn, kn, kn, v, v, v, bias, do)
    flat = lambda a: a.reshape(3 * DIL_GROUP, s, HEAD)
    return flat(dq), flat(dk), flat(dv), db


@functools.partial(jax.custom_vjp, nondiff_argnums=(2,))
def _bdot(a, b, dims):
    return _dot(a.astype(BF16), b.astype(BF16), dims)


def _bdot_fwd(a, b, dims):
    return _bdot(a, b, dims), (a, b)


def _bdot_bwd(dims, res, dc):
    a, b = res
    nn, nt, tn = (NN2, NT2, TN2) if dims in (NN2, NT2, TN2) else (NN3, NT3, TN3)
    if dims == nn:
        return _bdot(dc, b, nt), _bdot(a, dc, tn)
    if dims == nt:
        return _bdot(dc, b, nn), _bdot(dc, a, tn)
    return _bdot(b, dc, nt), _bdot(a, dc, nn)


_bdot.defvjp(_bdot_fwd, _bdot_bwd)


def _rwkv_chunk(s0, r, lw, kraw, v, ag, kk_w, ka_w, rk_w, lng, lnb):
    hb, c, _ = r.shape
    kk = kraw * kk_w
    kk = kk / jnp.maximum(jnp.sqrt(jnp.sum(kk * kk, axis=-1, keepdims=True)), 1e-12)
    k = kraw * (1.0 + (ag - 1.0) * ka_w)
    a = -kk
    b = kk * ag
    row = lax.broadcasted_iota(jnp.int32, (hb, c, c), 1)
    col = lax.broadcasted_iota(jnp.int32, (hb, c, c), 2)
    lower, strict = row >= col, row > col
    cum = _dot(lower.astype(F32), lw, NN3, lax.Precision.HIGHEST)
    ecum, einv = jnp.exp(cum), jnp.exp(-cum)
    rt, kt, bt = r * ecum, k * einv, b * einv
    at = a * jnp.exp(cum - lw)
    ar = jnp.concatenate([at, rt], axis=1)
    scores = _bdot(ar, jnp.concatenate([bt, kt], axis=1), NT3)
    a_ab = jnp.where(strict, scores[:, :c, :c], 0.0)
    a_ak = jnp.where(strict, scores[:, :c, c:], 0.0)
    p_rb = jnp.where(lower, scores[:, c:, :c], 0.0)
    p_rk = jnp.where(lower, scores[:, c:, c:], 0.0)
    from_s0 = _bdot(ar, s0, NT3)
    rhs = from_s0[:, :c] + _bdot(a_ak, v, NN3)
    inv = (row == col).astype(F32) + a_ab
    pw = a_ab
    for _ in range(int(math.log2(c)) - 1):
        pw = _bdot(pw, pw, NN3)
        inv = inv + _bdot(inv, pw, NN3)
    u = _bdot(inv, rhs, NN3)
    uv = jnp.concatenate([u, v], axis=1)
    y = from_s0[:, c:] + _bdot(jnp.concatenate([p_rb, p_rk], axis=2), uv, NN3)
    cum_end = cum[:, c - 1:c, :]
    dec = jnp.exp(cum_end - cum)
    s_end = s0 * jnp.exp(cum_end) + _bdot(uv, jnp.concatenate([b * dec, k * dec], axis=1), TN3)
    mu = jnp.mean(y, axis=-1, keepdims=True)
    var = jnp.mean(jnp.square(y - mu), axis=-1, keepdims=True)
    z = (y - mu) * lax.rsqrt(var + GN_EPS) * lng + lnb + jnp.sum(r * k * rk_w, axis=-1, keepdims=True) * v
    return z, s_end


def _rwkv_specs(nc, rev):
    cidx = (lambda c: nc - 1 - c) if rev else (lambda c: c)
    seq = pl.BlockSpec((RW_HB, RW_CHUNK, HEAD), lambda hg, c: (hg, cidx(c), 0))
    par = pl.BlockSpec((RW_HB, 1, HEAD), lambda hg, c: (hg, 0, 0))
    st = pl.BlockSpec((1, RW_HB, HEAD, HEAD), lambda hg, c: (cidx(c), hg, 0, 0))
    return seq, par, st


def _rwkv_fwd(seqs, pars):
    s = seqs[0].shape[1]
    nc = s // RW_CHUNK

    def body(*refs):
        seq_refs, par_refs = refs[:5], refs[5:10]
        z_ref, st_ref, state = refs[10:]
        c = pl.program_id(1)

        @pl.when(c == 0)
        def _():
            state[...] = jnp.zeros_like(state)

        s0 = state[...]
        st_ref[0] = s0
        z, s_end = _rwkv_chunk(s0, *[r[...] for r in seq_refs], *[r[...] for r in par_refs])
        z_ref[...] = z
        state[...] = s_end

    seq, par, st = _rwkv_specs(nc, False)
    return pl.pallas_call(
        body, name="rwkv_fwd", grid=(N_HEADS // RW_HB, nc),
        in_specs=[seq] * 5 + [par] * 5, out_specs=[seq, st],
        out_shape=[jax.ShapeDtypeStruct((N_HEADS, s, HEAD), F32), jax.ShapeDtypeStruct((nc, N_HEADS, HEAD, HEAD), F32)],
        scratch_shapes=[pltpu.VMEM((RW_HB, HEAD, HEAD), F32)],
        compiler_params=_params(("arbitrary", "arbitrary")),
    )(*seqs, *pars)


def _rwkv_bwd(seqs, pars, states, dz):
    s = seqs[0].shape[1]
    nc = s // RW_CHUNK

    def body(*refs):
        seq_refs, par_refs = refs[:5], refs[5:10]
        st_ref, dz_ref = refs[10:12]
        dseq_refs, dpar_refs, dstate = refs[12:17], refs[17:22], refs[22]
        c = pl.program_id(1)

        @pl.when(c == 0)
        def _():
            dstate[...] = jnp.zeros_like(dstate)
            for r in dpar_refs:
                r[...] = jnp.zeros_like(r)

        _, vjp = jax.vjp(_rwkv_chunk, st_ref[0], *[r[...] for r in seq_refs], *[r[...] for r in par_refs])
        g = vjp((dz_ref[...], dstate[...]))
        dstate[...] = g[0]
        for r, gs in zip(dseq_refs, g[1:6]):
            r[...] = gs
        for r, gp in zip(dpar_refs, g[6:]):
            r[...] += gp

    seq, par, st = _rwkv_specs(nc, True)
    seq_sh = jax.ShapeDtypeStruct((N_HEADS, s, HEAD), F32)
    par_sh = jax.ShapeDtypeStruct((N_HEADS, 1, HEAD), F32)
    outs = pl.pallas_call(
        body, name="rwkv_bwd", grid=(N_HEADS // RW_HB, nc),
        in_specs=[seq] * 5 + [par] * 5 + [st, seq],
        out_specs=[seq] * 5 + [par] * 5, out_shape=[seq_sh] * 5 + [par_sh] * 5,
        scratch_shapes=[pltpu.VMEM((RW_HB, HEAD, HEAD), F32)],
        compiler_params=_params(("arbitrary", "arbitrary")),
    )(*seqs, *pars, states, dz)
    return list(outs[:5]), list(outs[5:])


def _norm_fn(x, g):
    return (_rms(x, g),)


def _attn_prep_fn(proj, qn_w, kn_w):
    a, b = SB_HEADS, 3 * DIL_GROUP
    return (proj[0:a], proj[a:2 * a], proj[2 * a:3 * a],
            _rms(proj[3 * a:3 * a + b], qn_w), _rms(proj[3 * a + b:3 * a + 2 * b], kn_w), proj[3 * a + 2 * b:])


def _merge_fn(o_sb, o_dl):
    return (jnp.concatenate([_heads_to_nat(o_sb), _heads_to_nat(o_dl)], axis=-1),)


def _rw_mix_fn(x, xp, gn, mix, w0, w1, w2, a0, a1, a2, g1, g2):
    h = _rms(x, gn)
    xx = _rms(xp, gn) - h
    xr, xw, xk, xv, xa, xg = [h + xx * mix[i:i + 1] for i in range(6)]
    w_log = -jax.nn.softplus(-(w0 + _mm(jnp.tanh(_mm(xw, w1)), w2))) - 0.5
    lw = -jnp.exp(w_log)
    ag = jax.nn.sigmoid(a0 + _mm(_mm(xa, a1), a2))
    gate = _mm(jax.nn.sigmoid(_mm(xg, g1)), g2)
    return xr, xk, xv, _nat_to_heads(lw), _nat_to_heads(ag), gate


def _rw_gate_fn(z, gate):
    return (_heads_to_nat(z) * gate,)


def _adamw(name, w, m, v, gparts, row0=0, prev=None):
    big_r, c = w.shape
    r = gparts.shape[1]
    tr = r
    if r % 8 == 0:
        tr = max(t for t in range(8, r + 1, 8) if r % t == 0 and (t * c * 4 <= (1 << 20) or t == 8))
    assert row0 % tr == 0 and (r == big_r or r % 8 == 0)
    off = row0 // tr

    def body(w_ref, m_ref, v_ref, g_ref, *rest):
        go_ref, d_ref, mo_ref, vo_ref = rest[-4:]
        g = g_ref[0].astype(F32)
        for j in range(1, N_DEV):
            g = g + g_ref[j].astype(F32)
        mn = ADAM_B1 * m_ref[...] + (1.0 - ADAM_B1) * g
        vn = ADAM_B2 * v_ref[...] + (1.0 - ADAM_B2) * jnp.square(g)
        m_hat = mn / (1.0 - ADAM_B1 ** ADAM_STEP)
        v_hat = vn / (1.0 - ADAM_B2 ** ADAM_STEP)
        go_ref[...] = g
        d_ref[...] = -ADAM_LR * (m_hat / (jnp.sqrt(v_hat) + ADAM_EPS) + ADAM_WD * w_ref[...])
        mo_ref[...] = mn
        vo_ref[...] = vn

    tile = pl.BlockSpec((tr, c), lambda i: (i + off, 0))
    sh = jax.ShapeDtypeStruct((big_r, c), F32)
    prev = list(prev) if prev is not None else []
    return pl.pallas_call(
        body, name=name, grid=(r // tr,),
        in_specs=([tile, tile, tile, pl.BlockSpec((N_DEV, tr, c), lambda i: (0, i, 0))]
                  + [pl.BlockSpec(memory_space=pl.ANY)] * len(prev)),
        out_specs=[tile] * 4, out_shape=[sh] * 4,
        input_output_aliases={4 + j: j for j in range(len(prev))},
        compiler_params=_params(("arbitrary",)),
    )(w, m, v, gparts, *prev)


def _col_blocks_to_nat(g):
    return jnp.moveaxis(g, 0, 1).reshape(g.shape[1], -1)


def _nat_to_col_blocks(a):
    return jnp.moveaxis(a.reshape(a.shape[0], N_DEV, -1), 1, 0)


AG_GROUPS = ("f00", "att", "f01", "f10", "rw", "f11")
RS_GROUPS = ("f11", "rw", "f10", "f01", "att", "f00")
BF16_GRAD_GROUPS = ("att", "f00")
RW_SHARDED = ('rw_mix', 'rw_w0', 'rw_w1', 'rw_w2', 'rw_a0', 'rw_a1', 'rw_a2', 'rw_g1', 'rw_g2', 'rw_kk', 'rw_ka',
              'rw_wr', 'rw_wk', 'rw_wv', 'rw_wo', 'rw_lnx_g', 'rw_lnx_b')


def _step(x, target, rep, get, put):
    tied = lambda a, zero: a + zero[0, 0].astype(a.dtype)
    s, d = x.shape
    tf = min(512, s)
    tt = min(256, s)
    row = lambda a: a.reshape(1, -1)
    mix_norm = rep["mix_norm"]
    ffw = {(0, 0): get("f00", None)}
    ffn_norm = _col_blocks_to_nat(ffw[(0, 0)]["ffn_norm"].reshape(N_DEV, 4, -1))

    acts = {}

    def ffn(nm, xin, l, h):
        g = ffw[(l, h)]
        out, *acts[(l, h)] = _ffn_fwd(nm, xin, ffn_norm[2 * l + h][None], g["gate"], g["up"], g["down"], min(2 * tf, s))
        return out

    x1 = ffn("ffn00_fwd", x, 0, 0)
    att = get("att", x1)
    w_in = _col_blocks_to_nat(att["attn_w_in"])
    w_out = _col_blocks_to_nat(att["attn_w_out"])
    (h0,) = _tile_fwd("mixnorm0_fwd", _norm_fn, [(x1, "nat")], [mix_norm[0:1]], [((s, d), BF16, "nat")], tt)
    proj = _linear_fwd("attn_in_fwd", h0, w_in, tt, out_layout="hm")
    bias = _bias_tiles(rep["rel_bias"], s)
    prep_pars = [rep["attn_q_norm"], rep["attn_k_norm"]]
    sb_sh, dl_sh = (SB_HEADS, s, HEAD), (3 * DIL_GROUP, s, HEAD)
    sq, sk, sv, qn, kn, vd = _tile_fwd("attn_prep_fwd", _attn_prep_fn, [(proj, "hm")], prep_pars,
                                       [(sb_sh, BF16, "hm")] * 3 + [(dl_sh, BF16, "hm")] * 3, tt // 2)
    o_sb = _sb_fwd(sq, sk, sv)
    o_dl = _dil_fwd(qn, kn, vd, bias)
    (merged,) = _tile_fwd("merge_fwd", _merge_fn, [(o_sb, "hm"), (o_dl, "hm")], [], [((s, 512), BF16, "nat")], tt)
    x2 = _linear_fwd("attn_out_fwd", merged, w_out, tt, residual=x1)
    ffw[(0, 1)] = get("f01", x2)
    x3 = ffn("ffn01_fwd", x2, 0, 1)
    ffw[(1, 0)] = get("f10", x3)
    x4 = ffn("ffn10_fwd", x3, 1, 0)
    rw = get("rw", x4)
    rw_mix = _col_blocks_to_nat(rw["rw_mix"])
    rw_w1, rw_a1, rw_g1 = (rw[k].reshape(d, -1) for k in ("rw_w1", "rw_a1", "rw_g1"))
    rw_w2, rw_a2, rw_g2 = (_col_blocks_to_nat(rw[k]) for k in ("rw_w2", "rw_a2", "rw_g2"))
    rw_w0, rw_a0 = row(rw["rw_w0"]), row(rw["rw_a0"])
    head_par = lambda a: a.reshape(N_HEADS, 1, HEAD)
    scan_pars = [head_par(rw["rw_kk"]), head_par(rw["rw_ka"]), head_par(rep["rw_rk"]),
                 head_par(rw["rw_lnx_g"]), head_par(rw["rw_lnx_b"])]
    w_rkv = [rw[k].reshape(d, d) for k in ("rw_wr", "rw_wk", "rw_wv")]
    w_o = rw["rw_wo"].reshape(d, d)
    x4p = jnp.pad(x4, ((1, 0), (0, 0)))[:-1]
    mix_tiled = [(x4, "nat"), (x4p, "nat")]
    mix_pars = [mix_norm[1:2], rw_mix, rw_w0, rw_w1, rw_w2, rw_a0, rw_a1, rw_a2, rw_g1, rw_g2]
    hm_sh = (N_HEADS, s, HEAD)
    xr, xk, xv, lw, ag, gate = _tile_fwd(
        "rw_mix_fwd", _rw_mix_fn, mix_tiled, mix_pars,
        [((s, d), BF16, "nat")] * 3 + [(hm_sh, F32, "hm")] * 2 + [((s, d), F32, "nat")], tt)
    r_h, k_h, v_h = [_linear_fwd("rw_%s_fwd" % nm, xi, wi, tt, out_layout="hm")
                     for nm, xi, wi in zip("rkv", (xr, xk, xv), w_rkv)]
    scan_seqs = [r_h, lw, k_h, v_h, ag]
    z, states = _rwkv_fwd(scan_seqs, scan_pars)
    (zg,) = _tile_fwd("rw_gate_fwd", _rw_gate_fn, [(z, "hm"), (gate, "nat")], [], [((s, d), BF16, "nat")], tt)
    x5 = _linear_fwd("rw_out_fwd", zg, w_o, tt, residual=x4)
    ffw[(1, 1)] = get("f11", x5)
    y = ffn("ffn11_fwd", x5, 1, 1)
    dy, loss = _loss_head(y, target, tf)

    G = {}
    dgn = {}

    def fb(nm, group, xin, dout, l, h, zero=None, extra=None):
        g = ffw[(l, h)]
        gn = ffn_norm[2 * l + h][None]
        dxin, dgn[(l, h)], dwg, dwu, dwd = _ffn_bwd(nm, xin, dout, gn if zero is None else tied(gn, zero),
                                                   g["gate"], g["up"], g["down"], *acts[(l, h)], tf)
        shard = {"gate": dwg, "up": dwu, "down": dwd}
        if extra is not None:
            shard.update(extra())
        return dxin, put(group, {}, shard)

    dx5, zero = fb("ffn11_bwd", "f11", x5, dy, 1, 1)
    dzg = _linear_dx("rw_out_dx", dx5, w_o, tt)
    G["rw_wo"] = _linear_dw("rw_out_dw", zg, dx5, tf, 512)
    (dz, dgate), _ = _tile_bwd("rw_gate_bwd", _rw_gate_fn, [(z, "hm"), (gate, "nat")], [], [(dzg, "nat")], tt, [True, True])
    (dr_h, dlw, dk_h, dv_h, dag), dscan = _rwkv_bwd(scan_seqs, [tied(scan_pars[0], zero)] + scan_pars[1:], states, dz)
    drkv = (dr_h, dk_h, dv_h)
    for k, gpar in zip(("rw_kk", "rw_ka", "rw_rk", "rw_lnx_g", "rw_lnx_b"), dscan):
        G[k] = gpar
    dxs = []
    for j, (nm, xi, wi) in enumerate(zip("rkv", (xr, xk, xv), w_rkv)):
        dxs.append(_linear_dx("rw_%s_dx" % nm, drkv[j], wi, tt, dy_layout="hm"))
        G["rw_w" + nm] = _linear_dw("rw_%s_dw" % nm, xi, drkv[j], tf, 512, dy_layout="hm")
    (dx4a, dx4p), dmix = _tile_bwd(
        "rw_mix_bwd", _rw_mix_fn, mix_tiled, mix_pars,
        [(dxs[0], "nat"), (dxs[1], "nat"), (dxs[2], "nat"), (dlw, "hm"), (dag, "hm"), (dgate, "nat")],
        tt, [True, True], adds=[dx5, None])
    d_mixn1 = dmix[0]
    for k, gpar in zip(("rw_mix", "rw_w0", "rw_w1", "rw_w2", "rw_a0", "rw_a1", "rw_a2", "rw_g1", "rw_g2"), dmix[1:]):
        G[k] = gpar
    dx4 = dx4a + jnp.pad(dx4p[1:], ((0, 1), (0, 0)))
    for k in ("rw_mix", "rw_w2", "rw_a2", "rw_g2"):
        G[k] = _nat_to_col_blocks(G[k])
    for k in ("rw_w1", "rw_a1", "rw_g1", "rw_wr", "rw_wk", "rw_wv", "rw_wo"):
        G[k] = G[k].reshape(N_DEV, d // N_DEV, -1)
    for k in ("rw_w0", "rw_a0", "rw_kk", "rw_ka", "rw_lnx_g", "rw_lnx_b"):
        G[k] = G[k].reshape(N_DEV, 1, d // N_DEV)
    zero = put("rw", {"rw_rk": G["rw_rk"].reshape(N_HEADS, HEAD)}, {k: G[k] for k in RW_SHARDED})
    dx3, zero = fb("ffn10_bwd", "f10", x3, dx4, 1, 0, zero)
    dx2, zero = fb("ffn01_bwd", "f01", x2, dx3, 0, 1, zero)
    dmerged = _linear_dx("attn_out_dx", dx2, tied(w_out, zero), tt)
    G["attn_w_out"] = _linear_dw("attn_out_dw", merged, dx2, tf, 512)
    (do_sb, do_dl), _ = _tile_bwd("merge_bwd", _merge_fn, [(o_sb, "hm"), (o_dl, "hm")], [], [(dmerged, "nat")], tt, [True, True])
    dq_sb, dk_sb, dv_sb = _sb_bwd(sq, sk, sv, do_sb)
    dqn, dkn, dvd, dbias = _dil_bwd(qn, kn, vd, bias, do_dl)
    (dproj,), (dqn_w, dkn_w) = _tile_bwd(
        "attn_prep_bwd", _attn_prep_fn, [(proj, "hm")], prep_pars,
        [(dq_sb, "hm"), (dk_sb, "hm"), (dv_sb, "hm"), (dqn, "hm"), (dkn, "hm"), (dvd, "hm")], tt // 2, [True])
    dh0 = _linear_dx("attn_in_dx", dproj, w_in, tt, dy_layout="hm")
    G["attn_w_in"] = _linear_dw("attn_in_dw", h0, dproj, tf, 512, dy_layout="hm")
    (dx1,), (d_mixn0,) = _tile_bwd("mixnorm0_bwd", _norm_fn, [(x1, "nat")], [mix_norm[0:1]], [(dh0, "nat")], tt,
                                   [True], adds=[dx2])
    rep_grads = {"mix_norm": jnp.concatenate([d_mixn0, d_mixn1], axis=0), "rel_bias": _bias_tiles_bwd(dbias, s),
                 "attn_q_norm": dqn_w, "attn_k_norm": dkn_w}
    zero = put("att", rep_grads, {k: _nat_to_col_blocks(G[k]) for k in ("attn_w_in", "attn_w_out")})
    order = [(0, 0), (0, 1), (1, 0), (1, 1)]
    norm_grads = lambda: {"ffn_norm": _nat_to_col_blocks(jnp.concatenate([dgn[o] for o in order], axis=0))}
    dx0, zero = fb("ffn00_bwd", "f00", x, dx1, 0, 0, zero, extra=norm_grads)
    return loss, dx0, zero


WEIGHTS = ['ffn_norm', 'ffn_w_gate', 'ffn_w_up', 'ffn_w_down', 'mix_norm', 'rel_bias', 'attn_w_in', 'attn_q_norm',
           'attn_k_norm', 'attn_w_out', 'rw_mix', 'rw_w0', 'rw_w1', 'rw_w2', 'rw_a0', 'rw_a1', 'rw_a2', 'rw_g1', 'rw_g2',
           'rw_kk', 'rw_ka', 'rw_rk', 'rw_wr', 'rw_wk', 'rw_wv', 'rw_wo', 'rw_lnx_g', 'rw_lnx_b']
REPLICATED = ('mix_norm', 'rel_bias', 'attn_q_norm', 'attn_k_norm', 'rw_rk')
BF16_WEIGHTS = ('ffn_w_gate', 'ffn_w_up', 'ffn_w_down', 'attn_w_in', 'attn_w_out', 'rw_wr', 'rw_wk', 'rw_wv', 'rw_wo')


def kernel(x, ffn_norm, ffn_w_gate, ffn_w_up, ffn_w_down, mix_norm, rel_bias, attn_w_in, attn_q_norm, attn_k_norm, attn_w_out, rw_mix, rw_w0, rw_w1, rw_w2, rw_a0, rw_a1, rw_a2, rw_g1, rw_g2, rw_kk, rw_ka, rw_rk, rw_wr, rw_wk, rw_wv, rw_wo, rw_lnx_g, rw_lnx_b, loss_target, m_ffn_norm, m_ffn_w_gate, m_ffn_w_up, m_ffn_w_down, m_mix_norm, m_rel_bias, m_attn_w_in, m_attn_q_norm, m_attn_k_norm, m_attn_w_out, m_rw_mix, m_rw_w0, m_rw_w1, m_rw_w2, m_rw_a0, m_rw_a1, m_rw_a2, m_rw_g1, m_rw_g2, m_rw_kk, m_rw_ka, m_rw_rk, m_rw_wr, m_rw_wk, m_rw_wv, m_rw_wo, m_rw_lnx_g, m_rw_lnx_b, v_ffn_norm, v_ffn_w_gate, v_ffn_w_up, v_ffn_w_down, v_mix_norm, v_rel_bias, v_attn_w_in, v_attn_q_norm, v_attn_k_norm, v_attn_w_out, v_rw_mix, v_rw_w0, v_rw_w1, v_rw_w2, v_rw_a0, v_rw_a1, v_rw_a2, v_rw_g1, v_rw_g2, v_rw_kk, v_rw_ka, v_rw_rk, v_rw_wr, v_rw_wk, v_rw_wv, v_rw_wo, v_rw_lnx_g, v_rw_lnx_b):
    args = locals()
    w = {k: args[k] for k in WEIGHTS}
    cast = lambda k, a: a.astype(BF16) if k in BF16_WEIGHTS else a

    sources = {}
    for l, h in ((0, 0), (0, 1), (1, 0), (1, 1)):
        sources["f%d%d" % (l, h)] = {"gate": cast("ffn_w_gate", ffn_w_gate[l, h]), "up": cast("ffn_w_up", ffn_w_up[l, h]),
                                     "down": cast("ffn_w_down", ffn_w_down[l, h])}
    sources["f00"]["ffn_norm"] = ffn_norm
    drop_lead = lambda a: a[0] if a.ndim == 3 else a
    sources["att"] = {k: cast(k, w[k][0]) for k in ("attn_w_in", "attn_w_out")}
    sources["rw"] = {k: cast(k, drop_lead(w[k])) for k in RW_SHARDED}
    ag, token = {}, None
    for group in AG_GROUPS:
        names, arrays = list(sources[group]), list(sources[group].values())
        if token is not None:
            arrays[0] = arrays[0] + token[0, 0].astype(arrays[0].dtype)
        ag[group] = (names, _exchange_start("ag_start_" + group, arrays, []))
        token = ag[group][1]["token"]
    last_ag_token = token

    def get(group, after):
        names, started = ag[group]
        gathered, _ = _exchange_wait("ag_wait_" + group, started, last_ag_token if after is None else after)
        return dict(zip(names, gathered))

    rs = {}

    def put(group, rep_grads, shard_grads):
        if group in BF16_GRAD_GROUPS:
            shard_grads = {k: v.astype(BF16) for k, v in shard_grads.items()}
        started = _exchange_start("rs_start_" + group, list(rep_grads.values()), list(shard_grads.values()))
        rs[group] = (list(rep_grads), list(shard_grads), started)
        return started["token"]

    loss, dx, last_zero = _step(x[0], loss_target[0], {k: w[k] for k in REPLICATED}, get, put)
    loss = lax.psum(loss, MESH_AXES)

    results = {}
    ffn_prev = {}

    def update(k, parts, row0=0, prev=None):
        c = w[k].shape[-1]
        as2d = lambda a: a.reshape(-1, c)
        return _adamw("adamw_%s_%d" % (k, row0), as2d(w[k]), as2d(args["m_" + k]), as2d(args["v_" + k]),
                      parts.reshape(N_DEV, -1, c), row0, prev)

    after = last_zero
    for group in RS_GROUPS:
        rep_names, shard_names, started = rs[group]
        rep_parts, shard_parts = _exchange_wait("rs_wait_" + group, started, after)
        for k, parts in list(zip(rep_names, rep_parts)) + list(zip(shard_names, shard_parts)):
            if k in ("gate", "up", "down"):
                full = "ffn_w_" + k
                piece = 2 * int(group[1]) + int(group[2])
                ffn_prev[full] = update(full, parts, piece * parts.shape[1], ffn_prev.get(full))
                results[full] = ffn_prev[full]
            else:
                results[k] = update(k, parts)
            after = results[k if k in results else "ffn_w_" + k][0]

    outs = [[results[k][j].reshape(w[k].shape) for k in WEIGHTS] for j in range(4)]
    return (loss, dx[None], *outs[0], *outs[1], *outs[2], *outs[3])
```

```python
import functools
import math

import numpy as np
import jax
import jax.numpy as jnp
from jax import lax
from jax.experimental import pallas as pl
from jax.experimental.pallas import tpu as pltpu

F32, BF16 = jnp.float32, jnp.bfloat16
HI = lax.Precision.HIGH

N_DEV = 8
D_MODEL = 1024
HEAD = 64
N_HEADS = 16
SB_HEADS = 4
DIL_GROUP = 4
DIL_PATTERNS = ((128, 1), (512, 4), (2048, 16))
QBLK = 128
N_BUCKETS = 32
MAX_DISTANCE = 2048
NORM_EPS = 1e-6
GN_EPS = 64e-5
NEG_INF = -1e30
RW_CHUNK = 64
RW_HB = 16
ADAM_LR, ADAM_B1, ADAM_B2, ADAM_EPS, ADAM_WD, ADAM_STEP = 0.001, 0.9, 0.999, 1e-08, 0.01, 10
MESH_AXES = ("x", "y", "c")
VMEM_LIMIT_BYTES = 56 * 1024 * 1024

NN2 = (((1,), (0,)), ((), ()))
NT2 = (((1,), (1,)), ((), ()))
TN2 = (((0,), (0,)), ((), ()))
NN3 = (((2,), (1,)), ((0,), (0,)))
NT3 = (((2,), (2,)), ((0,), (0,)))
TN3 = (((1,), (1,)), ((0,), (0,)))


def _dot(a, b, dims=NN2, prec=None):
    return lax.dot_general(a, b, dims, precision=prec, preferred_element_type=F32)


def _params(sem=None):
    return pltpu.CompilerParams(dimension_semantics=sem, vmem_limit_bytes=VMEM_LIMIT_BYTES)


@jax.custom_vjp
def _mm(x, w):
    return _dot(x.astype(BF16), w.astype(BF16))


def _mm_fwd(x, w):
    return _mm(x, w), (x, w)


def _mm_bwd(res, dy):
    x, w = res
    dyb = dy.astype(BF16)
    return (_dot(dyb, w.astype(BF16), NT2).astype(x.dtype), _dot(x.astype(BF16), dyb, TN2).astype(w.dtype))


_mm.defvjp(_mm_fwd, _mm_bwd)


def _rms(x, g):
    return x * lax.rsqrt(jnp.mean(x * x, axis=-1, keepdims=True) + NORM_EPS) * g


def _log_sigmoid(z):
    return jnp.minimum(z, 0.0) - jnp.log(1.0 + jnp.exp(-jnp.abs(z)))


def _heads_to_nat(v3):
    return jnp.concatenate([v3[h] for h in range(v3.shape[0])], axis=-1)


def _nat_to_heads(v2):
    return jnp.stack([v2[:, h * HEAD:(h + 1) * HEAD] for h in range(v2.shape[1] // HEAD)], axis=0)


def _exchange(name, gathers, scatters):
    n_g = len(gathers)
    arrays = list(gathers) + list(scatters)
    n = len(arrays)
    out_shape = [jax.ShapeDtypeStruct((N_DEV,) + a.shape, a.dtype) for a in gathers]
    out_shape += [jax.ShapeDtypeStruct(a.shape, a.dtype) for a in scatters]

    def body(*refs):
        ins, outs = refs[:n], refs[n:2 * n]
        send_sems, recv_sems, local_sems = refs[2 * n:]
        x, y, c = lax.axis_index("x"), lax.axis_index("y"), lax.axis_index("c")
        me = 4 * x + 2 * y + c

        def src(i, idx):
            return ins[i] if i < n_g else ins[i].at[idx]

        local = [pltpu.make_async_copy(src(i, me), outs[i].at[me], local_sems.at[i]) for i in range(n)]
        for cp in local:
            cp.start()
        remote = []
        for m in range(1, N_DEV):
            px, py, pc = x ^ ((m >> 2) & 1), y ^ ((m >> 1) & 1), c ^ (m & 1)
            peer = 4 * px + 2 * py + pc
            for i in range(n):
                cp = pltpu.make_async_remote_copy(
                    src_ref=src(i, peer), dst_ref=outs[i].at[me],
                    send_sem=send_sems.at[i, m - 1], recv_sem=recv_sems.at[i, m - 1],
                    device_id=(px, py, pc), device_id_type=pl.DeviceIdType.MESH)
                cp.start()
                arrival = pltpu.make_async_remote_copy(
                    src_ref=src(i, peer), dst_ref=outs[i].at[peer],
                    send_sem=send_sems.at[i, m - 1], recv_sem=recv_sems.at[i, m - 1],
                    device_id=(px, py, pc), device_id_type=pl.DeviceIdType.MESH)
                remote.append((cp, arrival))
        for cp, arrival in remote:
            cp.wait_send()
            arrival.wait_recv()
        for cp in local:
            cp.wait()

    hbm = pl.BlockSpec(memory_space=pltpu.HBM)
    outs = pl.pallas_call(
        body, name=name, out_shape=out_shape,
        in_specs=[hbm] * n, out_specs=[hbm] * n,
        scratch_shapes=[pltpu.SemaphoreType.DMA((n, N_DEV - 1)), pltpu.SemaphoreType.DMA((n, N_DEV - 1)),
                        pltpu.SemaphoreType.DMA((n,))],
    )(*arrays)
    return list(outs[:n_g]), list(outs[n_g:])


def _mesh_peers():
    x, y, c = lax.axis_index("x"), lax.axis_index("y"), lax.axis_index("c")
    peers = []
    for m in range(1, N_DEV):
        px, py, pc = x ^ ((m >> 2) & 1), y ^ ((m >> 1) & 1), c ^ (m & 1)
        peers.append((m, (px, py, pc), 4 * px + 2 * py + pc))
    return 4 * x + 2 * y + c, peers


_HBM_SPEC = pl.BlockSpec(memory_space=pltpu.HBM)
_SEM_SPEC = pl.BlockSpec(memory_space=pltpu.SEMAPHORE)
_DATAFLOW = pltpu.SideEffectType.DATAFLOW_SIDE_EFFECTING


def _exchange_start(name, gathers, scatters):
    n_g = len(gathers)
    arrays = list(gathers) + list(scatters)
    n = len(arrays)
    lands = ([lax.empty((N_DEV,) + a.shape, a.dtype) for a in gathers] + [lax.empty(a.shape, a.dtype) for a in scatters])

    def body(*refs):
        ins, land = refs[:n], refs[n:2 * n]
        send_sems, recv_sems, local_sems, token = refs[2 * n], refs[2 * n + 1], refs[2 * n + 2], refs[-1]
        me, peers = _mesh_peers()
        for m, dev, peer in peers:
            for i in range(n):
                k = i * (N_DEV - 1) + m - 1
                pltpu.make_async_remote_copy(
                    src_ref=ins[i] if i < n_g else ins[i].at[peer], dst_ref=land[i].at[me],
                    send_sem=send_sems.at[k], recv_sem=recv_sems.at[k],
                    device_id=dev, device_id_type=pl.DeviceIdType.MESH).start()
        for i in range(n):
            pltpu.make_async_copy(ins[i] if i < n_g else ins[i].at[me], land[i].at[me], local_sems.at[i]).start()
        token[...] = jnp.zeros_like(token)

    sem = pltpu.SemaphoreType.DMA((n * (N_DEV - 1),))
    outs = pl.pallas_call(
        body, name=name,
        out_shape=([sem, sem, pltpu.SemaphoreType.DMA((n,))] + [pltpu.HBM(a.shape, a.dtype) for a in arrays]
                   + [pltpu.HBM(l.shape, l.dtype) for l in lands] + [jax.ShapeDtypeStruct((8, 128), F32)]),
        in_specs=[_HBM_SPEC] * (2 * n),
        out_specs=[_SEM_SPEC] * 3 + [_HBM_SPEC] * (2 * n) + [pl.BlockSpec(memory_space=pltpu.VMEM)],
        input_output_aliases={i: i + 3 for i in range(2 * n)},
        compiler_params=pltpu.CompilerParams(has_side_effects=_DATAFLOW),
    )(*[pltpu.with_memory_space_constraint(a, pltpu.HBM) for a in arrays],
      *[pltpu.with_memory_space_constraint(l, pltpu.HBM) for l in lands])
    return dict(n_g=n_g, n=n, send=outs[0], recv=outs[1], local=outs[2], srcs=list(outs[3:3 + n]),
                lands=list(outs[3 + n:3 + 2 * n]), token=outs[-1])


def _exchange_wait(name, started, after):
    n, n_g = started["n"], started["n_g"]

    def body(*refs):
        srcs, lands = refs[:n], refs[n:2 * n]
        send_sems, recv_sems, local_sems = refs[2 * n], refs[2 * n + 1], refs[2 * n + 2]
        me, peers = _mesh_peers()
        local = [pltpu.make_async_copy(srcs[i] if i < n_g else srcs[i].at[me], lands[i].at[me], local_sems.at[i])
                 for i in range(n)]
        for m, dev, peer in peers:
            for i in range(n):
                k = i * (N_DEV - 1) + m - 1
                cp = pltpu.make_async_remote_copy(
                    src_ref=srcs[i] if i < n_g else srcs[i].at[peer], dst_ref=lands[i].at[peer],
                    send_sem=send_sems.at[k], recv_sem=recv_sems.at[k],
                    device_id=dev, device_id_type=pl.DeviceIdType.MESH)
                cp.wait_send()
                cp.wait_recv()
        for cp in local:
            cp.wait()

    outs = pl.pallas_call(
        body, name=name,
        out_shape=([pltpu.HBM(a.shape, a.dtype) for a in started["srcs"]]
                   + [pltpu.HBM(l.shape, l.dtype) for l in started["lands"]]),
        in_specs=[_HBM_SPEC] * (2 * n) + [_SEM_SPEC] * 3 + [pl.BlockSpec(memory_space=pl.ANY)],
        out_specs=[_HBM_SPEC] * (2 * n), input_output_aliases={i: i for i in range(2 * n)},
        compiler_params=pltpu.CompilerParams(has_side_effects=_DATAFLOW),
    )(*started["srcs"], *started["lands"], started["send"], started["recv"], started["local"], after)
    return list(outs[n:n + n_g]), list(outs[n + n_g:])


def _tile_spec(shape, layout, t):
    if layout == "nat":
        return pl.BlockSpec((t, shape[1]), lambda i: (i, 0))
    return pl.BlockSpec((shape[0], t, shape[2]), lambda i: (0, i, 0))


def _full_spec(shape):
    nd = len(shape)
    return pl.BlockSpec(tuple(shape), lambda i: (0,) * nd)


def _seq_len(a, layout):
    return a.shape[0] if layout == "nat" else a.shape[1]


def _tile_fwd(name, f, tiled, params, outs, t):
    nt, npar = len(tiled), len(params)
    s = _seq_len(*tiled[0])

    def body(*refs):
        vals = [r[...] for r in refs[:nt + npar]]
        res = f(*vals)
        for r, o in zip(refs[nt + npar:], res):
            r[...] = o.astype(r.dtype)

    return pl.pallas_call(
        body, name=name, grid=(s // t,),
        in_specs=[_tile_spec(a.shape, l, t) for a, l in tiled] + [_full_spec(p.shape) for p in params],
        out_specs=[_tile_spec(sh, l, t) for sh, _, l in outs],
        out_shape=[jax.ShapeDtypeStruct(sh, dt) for sh, dt, _ in outs],
        compiler_params=_params(("arbitrary",)),
    )(*[a for a, _ in tiled], *params)


def _tile_bwd(name, f, tiled, params, cts, t, need, adds=None):
    nt, npar, nc = len(tiled), len(params), len(cts)
    s = _seq_len(*tiled[0])
    need_idx = [k for k in range(nt) if need[k]]
    adds = adds or [None] * len(need_idx)
    add_arrays = [(a, tiled[k][1]) for a, k in zip(adds, need_idx) if a is not None]
    n_add = len(add_arrays)

    def body(*refs):
        i = pl.program_id(0)
        vals = [r[...] for r in refs[:nt + npar]]
        ct_refs = refs[nt + npar:nt + npar + nc]
        add_refs = refs[nt + npar + nc:nt + npar + nc + n_add]
        out_refs = refs[nt + npar + nc + n_add:]
        res, vjp = jax.vjp(f, *vals)
        grads = vjp(tuple(r[...].astype(o.dtype) for r, o in zip(ct_refs, res)))
        a = 0
        for j, k in enumerate(need_idx):
            g = grads[k]
            if adds[j] is not None:
                g = g + add_refs[a][...]
                a += 1
            out_refs[j][...] = g.astype(out_refs[j].dtype)
        for j in range(npar):
            r = out_refs[len(need_idx) + j]

            @pl.when(i == 0)
            def _():
                r[...] = jnp.zeros_like(r)

            r[...] += grads[nt + j]

    outs = pl.pallas_call(
        body, name=name, grid=(s // t,),
        in_specs=([_tile_spec(a.shape, l, t) for a, l in tiled] + [_full_spec(p.shape) for p in params]
                  + [_tile_spec(a.shape, l, t) for a, l in cts] + [_tile_spec(a.shape, l, t) for a, l in add_arrays]),
        out_specs=([_tile_spec(tiled[k][0].shape, tiled[k][1], t) for k in need_idx]
                   + [_full_spec(p.shape) for p in params]),
        out_shape=([jax.ShapeDtypeStruct(tiled[k][0].shape, F32) for k in need_idx]
                   + [jax.ShapeDtypeStruct(p.shape, F32) for p in params]),
        compiler_params=_params(("arbitrary",)),
    )(*[a for a, _ in tiled], *params, *[a for a, _ in cts], *[a for a, _ in add_arrays])
    return list(outs[:len(need_idx)]), list(outs[len(need_idx):])


def _linear_fwd(name, x, w, t, out_layout="nat", residual=None):
    s, k = x.shape
    n = w.shape[1]
    has_res = residual is not None

    def body(*refs):
        x_ref, w_ref = refs[0], refs[1]
        o_ref = refs[-1]
        y = _dot(x_ref[...].astype(BF16), w_ref[...])
        if has_res:
            y = y + refs[2][...]
        if out_layout == "hm":
            for h in range(n // HEAD):
                o_ref[h] = y[:, h * HEAD:(h + 1) * HEAD]
        else:
            o_ref[...] = y

    out_sh = (s, n) if out_layout == "nat" else (n // HEAD, s, HEAD)
    ins = [x, w] + ([residual] if has_res else [])
    in_specs = [_tile_spec(x.shape, "nat", t), _full_spec(w.shape)] + ([_tile_spec((s, n), "nat", t)] if has_res else [])
    return pl.pallas_call(
        body, name=name, grid=(s // t,), in_specs=in_specs,
        out_specs=_tile_spec(out_sh, out_layout, t), out_shape=jax.ShapeDtypeStruct(out_sh, F32),
        compiler_params=_params(("arbitrary",)),
    )(*ins)


def _linear_dx(name, dy, w, t, dy_layout="nat"):
    k, n = w.shape
    s = _seq_len(dy, dy_layout)

    def body(dy_ref, w_ref, o_ref):
        dy = _heads_to_nat(dy_ref[...].astype(BF16)) if dy_layout == "hm" else dy_ref[...].astype(BF16)
        o_ref[...] = _dot(dy, w_ref[...], NT2)

    return pl.pallas_call(
        body, name=name, grid=(s // t,),
        in_specs=[_tile_spec(dy.shape, dy_layout, t), _full_spec(w.shape)],
        out_specs=_tile_spec((s, k), "nat", t), out_shape=jax.ShapeDtypeStruct((s, k), F32),
        compiler_params=_params(("arbitrary",)),
    )(dy, w)


def _linear_dw(name, x, dy, t, nb, dy_layout="nat"):
    s, k = x.shape
    n = dy.shape[1] if dy_layout == "nat" else dy.shape[0] * HEAD

    def body(x_ref, dy_ref, o_ref):
        i = pl.program_id(1)

        @pl.when(i == 0)
        def _():
            o_ref[...] = jnp.zeros_like(o_ref)

        dy = _heads_to_nat(dy_ref[...].astype(BF16)) if dy_layout == "hm" else dy_ref[...].astype(BF16)
        o_ref[...] += _dot(x_ref[...].astype(BF16), dy, TN2)

    if dy_layout == "hm":
        dy_spec = pl.BlockSpec((nb // HEAD, t, HEAD), lambda j, i: (j, i, 0))
    else:
        dy_spec = pl.BlockSpec((t, nb), lambda j, i: (i, j))
    return pl.pallas_call(
        body, name=name, grid=(n // nb, s // t),
        in_specs=[pl.BlockSpec((t, k), lambda j, i: (i, 0)), dy_spec],
        out_specs=pl.BlockSpec((k, nb), lambda j, i: (0, j)), out_shape=jax.ShapeDtypeStruct((k, n), F32),
        compiler_params=_params(("arbitrary", "arbitrary")),
    )(x, dy)


def _ffn_fwd(name, x, gn, wg, wu, wd, t):
    s, d = x.shape
    f8 = wg.shape[-1]

    def body(x_ref, g_ref, wg_ref, wu_ref, wd_ref, o_ref, gk_ref, uk_ref, h_scr, acc):
        k = pl.program_id(1)

        @pl.when(k == 0)
        def _():
            h_scr[...] = _rms(x_ref[...], g_ref[...]).astype(BF16)
            acc[...] = jnp.zeros_like(acc)

        hb = h_scr[...]
        gk = _dot(hb, wg_ref[0])
        uk = _dot(hb, wu_ref[0])
        gk_ref[0] = gk
        uk_ref[0] = uk
        a = gk * jax.nn.sigmoid(gk) * uk
        acc[...] += _dot(a.astype(BF16), wd_ref[0])

        @pl.when(k == N_DEV - 1)
        def _():
            o_ref[...] = x_ref[...] + 0.5 * acc[...]

    wspec = lambda shp: pl.BlockSpec((1,) + shp, lambda i, k: (k, 0, 0))
    act = pl.BlockSpec((1, t, f8), lambda i, k: (k, i, 0))
    act_sh = jax.ShapeDtypeStruct((N_DEV, s, f8), F32)
    return pl.pallas_call(
        body, name=name, grid=(s // t, N_DEV),
        in_specs=[pl.BlockSpec((t, d), lambda i, k: (i, 0)), pl.BlockSpec((1, d), lambda i, k: (0, 0)),
                  wspec((d, f8)), wspec((d, f8)), wspec((f8, d))],
        out_specs=[pl.BlockSpec((t, d), lambda i, k: (i, 0)), act, act],
        out_shape=[jax.ShapeDtypeStruct((s, d), F32), act_sh, act_sh],
        scratch_shapes=[pltpu.VMEM((t, d), BF16), pltpu.VMEM((t, d), F32)],
        compiler_params=_params(("arbitrary", "arbitrary")),
    )(x, gn, wg, wu, wd)


def _ffn_bwd(name, x, dy, gn, wg, wu, wd, gact, uact, t):
    s, d = x.shape
    f8 = wg.shape[-1]
    last = N_DEV - 1

    def body(x_ref, dy_ref, g_ref, wg_ref, wu_ref, wd_ref, gk_ref, uk_ref,
             dx_ref, dg_ref, dwg_ref, dwu_ref, dwd_ref, dh_scr):
        k, i = pl.program_id(0), pl.program_id(1)
        x = x_ref[...]
        rs = lax.rsqrt(jnp.mean(x * x, axis=-1, keepdims=True) + NORM_EPS)
        xn = x * rs
        hb = (xn * g_ref[...]).astype(BF16)
        dob = (0.5 * dy_ref[...]).astype(BF16)
        wgk, wuk, wdk = wg_ref[0], wu_ref[0], wd_ref[0]
        gk, uk = gk_ref[0], uk_ref[0]
        sg = jax.nn.sigmoid(gk)
        sk = gk * sg
        da = _dot(dob, wdk, NT2)
        du = (da * sk).astype(BF16)
        dg = (da * uk * (sg * (1.0 + gk * (1.0 - sg)))).astype(BF16)

        @pl.when(i == 0)
        def _():
            dwg_ref[...] = jnp.zeros_like(dwg_ref)
            dwu_ref[...] = jnp.zeros_like(dwu_ref)
            dwd_ref[...] = jnp.zeros_like(dwd_ref)

        dwd_ref[0] += _dot((sk * uk).astype(BF16), dob, TN2)
        dwg_ref[0] += _dot(hb, dg, TN2)
        dwu_ref[0] += _dot(hb, du, TN2)
        dh = _dot(dg, wgk, NT2) + _dot(du, wuk, NT2)
        rows = pl.ds(pl.multiple_of(i * t, t), t)

        @pl.when(k == 0)
        def _():
            dh_scr[rows, :] = dh

        @pl.when(k > 0)
        def _():
            dh_scr[rows, :] += dh

        @pl.when(jnp.logical_and(k == last, i == 0))
        def _():
            dg_ref[...] = jnp.zeros_like(dg_ref)

        @pl.when(k == last)
        def _():
            dht = dh_scr[rows, :]
            dg_ref[...] += jnp.sum(dht * xn, axis=0, keepdims=True)
            dxn = dht * g_ref[...]
            dx_ref[...] = dy_ref[...] + rs * (dxn - xn * jnp.mean(dxn * xn, axis=-1, keepdims=True))

    wspec = lambda shp: pl.BlockSpec((1,) + shp, lambda k, i: (k, 0, 0))
    tile = pl.BlockSpec((t, d), lambda k, i: (i, 0))
    act = pl.BlockSpec((1, t, f8), lambda k, i: (k, i, 0))
    return pl.pallas_call(
        body, name=name, grid=(N_DEV, s // t),
        in_specs=[tile, tile, pl.BlockSpec((1, d), lambda k, i: (0, 0)), wspec((d, f8)), wspec((d, f8)), wspec((f8, d)),
                  act, act],
        out_specs=[pl.BlockSpec((t, d), lambda k, i: (jnp.where(k == last, i, 0), 0)),
                   pl.BlockSpec((1, d), lambda k, i: (0, 0)),
                   pl.BlockSpec((1, d, f8), lambda k, i: (k, 0, 0)), pl.BlockSpec((1, d, f8), lambda k, i: (k, 0, 0)),
                   pl.BlockSpec((1, f8, d), lambda k, i: (k, 0, 0))],
        out_shape=[jax.ShapeDtypeStruct((s, d), F32), jax.ShapeDtypeStruct((1, d), F32),
                   jax.ShapeDtypeStruct((N_DEV, d, f8), F32), jax.ShapeDtypeStruct((N_DEV, d, f8), F32),
                   jax.ShapeDtypeStruct((N_DEV, f8, d), F32)],
        scratch_shapes=[pltpu.VMEM((s, d), F32)],
        compiler_params=_params(("arbitrary", "arbitrary")),
    )(x, dy, gn, wg, wu, wd, gact, uact)


def _loss_head(y, target, t):
    s, d = y.shape

    def body(y_ref, t_ref, dy_ref, l_ref):
        i = pl.program_id(0)
        err = y_ref[...] - t_ref[...]
        dy_ref[...] = err * (1.0 / d)

        @pl.when(i == 0)
        def _():
            l_ref[...] = jnp.zeros_like(l_ref)

        l_ref[...] += 0.5 * jnp.sum(jnp.mean(err * err, axis=-1, keepdims=True), axis=0, keepdims=True)

    tile = pl.BlockSpec((t, d), lambda i: (i, 0))
    dy, l = pl.pallas_call(
        body, name="loss_head", grid=(s // t,), in_specs=[tile, tile],
        out_specs=[tile, pl.BlockSpec((1, 1), lambda i: (0, 0))],
        out_shape=[jax.ShapeDtypeStruct((s, d), F32), jax.ShapeDtypeStruct((1, 1), F32)],
        compiler_params=_params(("arbitrary",)),
    )(y, target)
    return dy, l[0, 0]


SB_KEY_TILE = 512


def _sb_scan_mats():
    row = lax.broadcasted_iota(jnp.int32, (QBLK, QBLK), 0)
    col = lax.broadcasted_iota(jnp.int32, (QBLK, QBLK), 1)
    return (row > col).astype(F32).astype(BF16), (row < col).astype(F32).astype(BF16)


def _sb_tile_scan(x, mat, reverse):
    nsub = x.shape[1] // QBLK
    outs, carry = [None] * nsub, jnp.zeros((x.shape[0], 1), F32)
    for i in (reversed(range(nsub)) if reverse else range(nsub)):
        xs = x[:, i * QBLK:(i + 1) * QBLK]
        hi = xs.astype(BF16)
        lo = (xs - hi.astype(F32)).astype(BF16)
        outs[i] = _dot(hi, mat) + _dot(lo, mat) + carry
        carry = carry + jnp.sum(xs, axis=1, keepdims=True)
    return jnp.concatenate(outs, axis=1), carry


def _sb_before_query(n, t, kt):
    row = lax.broadcasted_iota(jnp.int32, (QBLK, kt), 0)
    col = lax.broadcasted_iota(jnp.int32, (QBLK, kt), 1)
    return t * kt + col < n * QBLK + row


def _sb_fwd(q, k, v):
    _, s, _ = q.shape
    scale = HEAD ** -0.5
    kt = min(SB_KEY_TILE, s)

    def body(q_ref, k_ref, v_ref, o_ref):
        n = pl.program_id(1)
        qb = (q_ref[0] * scale).astype(q_ref.dtype)
        suffix, _ = _sb_scan_mats()
        n_tiles = lax.div(n, jnp.int32(kt // QBLK)) + 1

        def tile(t, c, acc, diagonal):
            rows = pl.ds(pl.multiple_of(t * kt, kt), kt)
            kb, vb = k_ref[0, rows, :], v_ref[0, rows, :]
            z = _dot(qb, kb, NT2)
            lk = _log_sigmoid(-z)
            log_beta = z + lk
            if diagonal:
                ok = _sb_before_query(n, t, kt)
                lk = jnp.where(ok, lk, 0.0)
            later, total = _sb_tile_scan(lk, suffix, True)
            w = jnp.exp(log_beta + later + c)
            if diagonal:
                w = jnp.where(ok, w, 0.0)
            return c + total, acc + _dot(w.astype(BF16), vb)

        carry = tile(n_tiles - 1, jnp.zeros((QBLK, 1), F32), jnp.zeros((QBLK, HEAD), F32), True)
        _, acc = lax.fori_loop(1, n_tiles, lambda jj, cr: tile(n_tiles - 1 - jj, cr[0], cr[1], False), carry)
        o_ref[0] = acc

    return pl.pallas_call(
        body, name="sb_fwd", grid=(SB_HEADS, s // QBLK),
        in_specs=[pl.BlockSpec((1, QBLK, HEAD), lambda h, n: (h, n, 0)),
                  pl.BlockSpec((1, s, HEAD), lambda h, n: (h, 0, 0)),
                  pl.BlockSpec((1, s, HEAD), lambda h, n: (h, 0, 0))],
        out_specs=pl.BlockSpec((1, QBLK, HEAD), lambda h, n: (h, n, 0)),
        out_shape=jax.ShapeDtypeStruct((SB_HEADS, s, HEAD), F32),
        compiler_params=_params(("arbitrary", "arbitrary")),
    )(q, k, v)


def _sb_bwd(q, k, v, do):
    _, s, _ = q.shape
    scale = HEAD ** -0.5
    kt = min(SB_KEY_TILE, s)

    def body(q_ref, k_ref, v_ref, do_ref, dq_ref, dk_ref, dv_ref, e_scr, beta_scr):
        n = pl.program_id(1)

        @pl.when(n == 0)
        def _():
            dk_ref[...] = jnp.zeros_like(dk_ref)
            dv_ref[...] = jnp.zeros_like(dv_ref)

        qb = (q_ref[0] * scale).astype(q_ref.dtype)
        dob = do_ref[0].astype(BF16)
        suffix, prefix = _sb_scan_mats()
        n_tiles = lax.div(n, jnp.int32(kt // QBLK)) + 1

        def weights(t, c, diagonal):
            rows = pl.ds(pl.multiple_of(t * kt, kt), kt)
            kb, vb = k_ref[0, rows, :], v_ref[0, rows, :]
            z = _dot(qb, kb, NT2)
            lk = _log_sigmoid(-z)
            log_beta = z + lk
            if diagonal:
                ok = _sb_before_query(n, t, kt)
                lk = jnp.where(ok, lk, 0.0)
            later, total = _sb_tile_scan(lk, suffix, True)
            w = jnp.exp(log_beta + later + c)
            if diagonal:
                w = jnp.where(ok, w, 0.0)
            e_scr[t] = w * _dot(dob, vb, NT2)
            beta_scr[t] = jnp.exp(log_beta)
            dv_ref[0, rows, :] += _dot(w.astype(BF16), dob, TN2)
            return c + total

        c_diag = weights(n_tiles - 1, jnp.zeros((QBLK, 1), F32), True)
        lax.fori_loop(1, n_tiles, lambda jj, c: weights(n_tiles - 1 - jj, c, False), c_diag)

        def grads(t, pc, dq, diagonal):
            rows = pl.ds(pl.multiple_of(t * kt, kt), kt)
            kb = k_ref[0, rows, :]
            e, beta = e_scr[t], beta_scr[t]
            before, total = _sb_tile_scan(e, prefix, False)
            dz = e * (1.0 - beta) - beta * (before + pc)
            if diagonal:
                dz = jnp.where(_sb_before_query(n, t, kt), dz, 0.0)
            dz = dz.astype(BF16)
            dk_ref[0, rows, :] += _dot(dz, qb, TN2)
            return pc + total, dq + _dot(dz, kb)

        carry = lax.fori_loop(0, n_tiles - 1, lambda t, cr: grads(t, cr[0], cr[1], False),
                              (jnp.zeros((QBLK, 1), F32), jnp.zeros((QBLK, HEAD), F32)))
        _, dq = grads(n_tiles - 1, carry[0], carry[1], True)
        dq_ref[0] = dq * scale

    qspec = pl.BlockSpec((1, QBLK, HEAD), lambda h, n: (h, n, 0))
    full = pl.BlockSpec((1, s, HEAD), lambda h, n: (h, 0, 0))
    sh = jax.ShapeDtypeStruct((SB_HEADS, s, HEAD), F32)
    return pl.pallas_call(
        body, name="sb_bwd", grid=(SB_HEADS, s // QBLK),
        in_specs=[qspec, full, full, qspec],
        out_specs=[qspec, full, full], out_shape=[sh, sh, sh],
        scratch_shapes=[pltpu.VMEM((s // kt, QBLK, kt), F32), pltpu.VMEM((s // kt, QBLK, kt), F32)],
        compiler_params=_params(("arbitrary", "arbitrary")),
    )(q, k, v, do)


def _t5_bucket_np(dist):
    max_exact = N_BUCKETS // 2
    d = np.maximum(dist, 1).astype(np.float32)
    large = max_exact + (np.log(d / np.float32(max_exact)) / np.float32(math.log(MAX_DISTANCE / max_exact))
                         * np.float32(N_BUCKETS - max_exact)).astype(np.int32)
    large = np.minimum(large, N_BUCKETS - 1)
    return np.where(dist < max_exact, dist, large)


def _dil_layout(s):
    tiles, buckets = [], []
    i = np.arange(QBLK)[:, None]
    j = np.arange(QBLK)[None, :]
    for g, (window, r) in enumerate(DIL_PATTERNS):
        for off in range(min(window // QBLK + 1, s // QBLK)):
            dist = QBLK * off + i - j
            ok = (dist >= 0) & (dist <= window) & (dist % r == 0)
            tiles.append((g, off))
            buckets.append(np.where(ok, _t5_bucket_np(np.maximum(dist, 0)), -1).astype(np.int32))
    return tiles, np.stack(buckets)


def _bias_tiles(rel_bias, s):
    tiles, buckets = _dil_layout(s)
    nt = len(tiles)
    present = [sorted(set(np.unique(buckets[k]).tolist()) - {-1}) for k in range(nt)]

    def body(rel_ref, b_ref, o_ref):
        j = pl.program_id(0)
        for k, (g, _) in enumerate(tiles):
            bk = b_ref[k]
            tile = jnp.full((QBLK, QBLK), NEG_INF, F32)
            for b in present[k]:
                tile = jnp.where(bk == b, rel_ref[b, g * DIL_GROUP + j], tile)
            o_ref[0, k] = tile

    return pl.pallas_call(
        body, name="bias_tiles", grid=(DIL_GROUP,),
        in_specs=[pl.BlockSpec(memory_space=pltpu.SMEM), pl.BlockSpec((nt, QBLK, QBLK), lambda j: (0, 0, 0))],
        out_specs=pl.BlockSpec((1, nt, QBLK, QBLK), lambda j: (j, 0, 0, 0)),
        out_shape=jax.ShapeDtypeStruct((DIL_GROUP, nt, QBLK, QBLK), F32),
        compiler_params=_params(("arbitrary",)),
    )(rel_bias, jnp.asarray(buckets))


def _bias_tiles_bwd(dbias, s):
    tiles, buckets = _dil_layout(s)
    nt = len(tiles)
    present = [sorted(set(np.unique(buckets[k]).tolist()) - {-1}) for k in range(nt)]

    def body(d_ref, b_ref, o_ref):
        j = pl.program_id(0)

        @pl.when(j == 0)
        def _():
            for b in range(N_BUCKETS):
                for col in range(3 * DIL_GROUP):
                    o_ref[b, col] = jnp.float32(0.0)

        for k, (g, _) in enumerate(tiles):
            bk, dk = b_ref[k], d_ref[0, k]
            for b in present[k]:
                o_ref[b, g * DIL_GROUP + j] += jnp.sum(jnp.where(bk == b, dk, 0.0))

    return pl.pallas_call(
        body, name="bias_tiles_bwd", grid=(DIL_GROUP,),
        in_specs=[pl.BlockSpec((1, nt, QBLK, QBLK), lambda j: (j, 0, 0, 0)),
                  pl.BlockSpec((nt, QBLK, QBLK), lambda j: (0, 0, 0))],
        out_specs=pl.BlockSpec(memory_space=pltpu.SMEM),
        out_shape=jax.ShapeDtypeStruct((N_BUCKETS, 3 * DIL_GROUP), F32),
        compiler_params=_params(("arbitrary",)),
    )(dbias, jnp.asarray(buckets))


def _dil_fn(tiles, qs, kblks, vblks, bias, valid):
    scale = HEAD ** -0.5
    logits = []
    for k, (g, _) in enumerate(tiles):
        l = _bdot(qs[g], kblks[k], NT2) * scale + bias[k]
        logits.append(jnp.where(valid[k] > 0.5, l, NEG_INF))
    m = lax.stop_gradient(jnp.max(functools.reduce(jnp.maximum, logits), axis=1, keepdims=True))
    ps = [jnp.exp(l - m) for l in logits]
    den = jnp.sum(functools.reduce(jnp.add, ps), axis=1, keepdims=True)
    inv = 1.0 / den
    out = jnp.zeros((QBLK, HEAD), F32)
    for k in range(len(tiles)):
        out = out + _bdot(ps[k] * inv, vblks[k], NN2)
    return out


def _dil_specs(s, nt):
    qspecs = [pl.BlockSpec((1, QBLK, HEAD), functools.partial(lambda j, n, g: (DIL_GROUP * g + j, n, 0), g=g))
              for g in range(3)]
    kvspecs = [pl.BlockSpec((1, s, HEAD), functools.partial(lambda j, n, g: (DIL_GROUP * g + j, 0, 0), g=g),
                            pipeline_mode=pl.Buffered(1)) for g in range(3)]
    return qspecs + kvspecs + kvspecs + [pl.BlockSpec((1, nt, QBLK, QBLK), lambda j, n: (j, 0, 0, 0))]


def _dil_load(tiles, n, q_refs, k_refs, v_refs):
    qs = [r[0].astype(F32) for r in q_refs]
    kblks, vblks, valid, rows = [], [], [], []
    for g, off in tiles:
        kb = jnp.maximum(n - off, 0)
        rw = pl.ds(pl.multiple_of(kb * QBLK, QBLK), QBLK)
        rows.append(rw)
        kblks.append(k_refs[g][0, rw, :].astype(F32))
        vblks.append(v_refs[g][0, rw, :].astype(F32))
        valid.append((n >= off).astype(F32))
    return qs, kblks, vblks, valid, rows


def _dil_fwd(qn, kn, v, bias):
    _, s, _ = qn.shape
    tiles, _ = _dil_layout(s)
    nt = len(tiles)

    def body(*refs):
        q_refs, k_refs, v_refs, b_ref, o_ref = refs[0:3], refs[3:6], refs[6:9], refs[9], refs[10]
        n = pl.program_id(1)
        qs, kblks, vblks, valid, _ = _dil_load(tiles, n, q_refs, k_refs, v_refs)
        o_ref[0] = _dil_fn(tiles, qs, kblks, vblks, b_ref[0], valid)

    return pl.pallas_call(
        body, name="dil_fwd", grid=(DIL_GROUP, s // QBLK), in_specs=_dil_specs(s, nt),
        out_specs=pl.BlockSpec((1, QBLK, HEAD), lambda j, n: (j, n, 0)),
        out_shape=jax.ShapeDtypeStruct((DIL_GROUP, s, HEAD), F32),
        compiler_params=_params(("arbitrary", "arbitrary")),
    )(qn, qn, qn, kn, kn, kn, v, v, v, bias)


def _dil_bwd(qn, kn, v, bias, do):
    _, s, _ = qn.shape
    tiles, _ = _dil_layout(s)
    nt = len(tiles)

    def body(*refs):
        q_refs, k_refs, v_refs, b_ref, do_ref = refs[0:3], refs[3:6], refs[6:9], refs[9], refs[10]
        dq_ref, dk_ref, dv_ref, db_ref = refs[11:]
        n = pl.program_id(1)

        @pl.when(n == 0)
        def _():
            for r in (dk_ref, dv_ref, db_ref):
                r[...] = jnp.zeros_like(r)

        qs, kblks, vblks, valid, rows = _dil_load(tiles, n, q_refs, k_refs, v_refs)
        fn = functools.partial(_dil_fn, tiles)
        _, vjp = jax.vjp(lambda a, b, c, d: fn(a, b, c, d, valid), qs, kblks, vblks, b_ref[0])
        dqs, dks, dvs, db = vjp(do_ref[0])
        for g in range(3):
            dq_ref[g, 0] = dqs[g]
        for k, (g, _) in enumerate(tiles):
            dk_ref[g, 0, rows[k], :] += dks[k]
            dv_ref[g, 0, rows[k], :] += dvs[k]
        db_ref[0] += db

    gsh = jax.ShapeDtypeStruct((3, DIL_GROUP, s, HEAD), F32)
    full = pl.BlockSpec((3, 1, s, HEAD), lambda j, n: (0, j, 0, 0), pipeline_mode=pl.Buffered(1))
    dq, dk, dv, db = pl.pallas_call(
        body, name="dil_bwd", grid=(DIL_GROUP, s // QBLK),
        in_specs=_dil_specs(s, nt) + [pl.BlockSpec((1, QBLK, HEAD), lambda j, n: (j, n, 0))],
        out_specs=[pl.BlockSpec((3, 1, QBLK, HEAD), lambda j, n: (0, j, n, 0)), full, full,
                   pl.BlockSpec((1, nt, QBLK, QBLK), lambda j, n: (j, 0, 0, 0))],
        out_shape=[gsh, gsh, gsh, jax.ShapeDtypeStruct((DIL_GROUP, nt, QBLK, QBLK), F32)],
        compiler_params=_params(("arbitrary", "arbitrary")),
    )(qn, qn, qn, kn, kn, kn, v, v, v, bias, do)
    flat = lambda a: a.reshape(3 * DIL_GROUP, s, HEAD)
    return flat(dq), flat(dk), flat(dv), db


@functools.partial(jax.custom_vjp, nondiff_argnums=(2,))
def _bdot(a, b, dims):
    return _dot(a.astype(BF16), b.astype(BF16), dims)


def _bdot_fwd(a, b, dims):
    return _bdot(a, b, dims), (a, b)


def _bdot_bwd(dims, res, dc):
    a, b = res
    nn, nt, tn = (NN2, NT2, TN2) if dims in (NN2, NT2, TN2) else (NN3, NT3, TN3)
    if dims == nn:
        return _bdot(dc, b, nt), _bdot(a, dc, tn)
    if dims == nt:
        return _bdot(dc, b, nn), _bdot(dc, a, tn)
    return _bdot(b, dc, nt), _bdot(a, dc, nn)


_bdot.defvjp(_bdot_fwd, _bdot_bwd)


def _rwkv_chunk(s0, r, lw, kraw, v, ag, kk_w, ka_w, rk_w, lng, lnb):
    hb, c, _ = r.shape
    kk = kraw * kk_w
    kk = kk / jnp.maximum(jnp.sqrt(jnp.sum(kk * kk, axis=-1, keepdims=True)), 1e-12)
    k = kraw * (1.0 + (ag - 1.0) * ka_w)
    a = -kk
    b = kk * ag
    row = lax.broadcasted_iota(jnp.int32, (hb, c, c), 1)
    col = lax.broadcasted_iota(jnp.int32, (hb, c, c), 2)
    lower, strict = row >= col, row > col
    cum = _dot(lower.astype(F32), lw, NN3, lax.Precision.HIGHEST)
    ecum, einv = jnp.exp(cum), jnp.exp(-cum)
    rt, kt, bt = r * ecum, k * einv, b * einv
    at = a * jnp.exp(cum - lw)
    ar = jnp.concatenate([at, rt], axis=1)
    scores = _bdot(ar, jnp.concatenate([bt, kt], axis=1), NT3)
    a_ab = jnp.where(strict, scores[:, :c, :c], 0.0)
    a_ak = jnp.where(strict, scores[:, :c, c:], 0.0)
    p_rb = jnp.where(lower, scores[:, c:, :c], 0.0)
    p_rk = jnp.where(lower, scores[:, c:, c:], 0.0)
    from_s0 = _bdot(ar, s0, NT3)
    rhs = from_s0[:, :c] + _bdot(a_ak, v, NN3)
    inv = (row == col).astype(F32) + a_ab
    pw = a_ab
    for _ in range(int(math.log2(c)) - 1):
        pw = _bdot(pw, pw, NN3)
        inv = inv + _bdot(inv, pw, NN3)
    u = _bdot(inv, rhs, NN3)
    uv = jnp.concatenate([u, v], axis=1)
    y = from_s0[:, c:] + _bdot(jnp.concatenate([p_rb, p_rk], axis=2), uv, NN3)
    cum_end = cum[:, c - 1:c, :]
    dec = jnp.exp(cum_end - cum)
    s_end = s0 * jnp.exp(cum_end) + _bdot(uv, jnp.concatenate([b * dec, k * dec], axis=1), TN3)
    mu = jnp.mean(y, axis=-1, keepdims=True)
    var = jnp.mean(jnp.square(y - mu), axis=-1, keepdims=True)
    z = (y - mu) * lax.rsqrt(var + GN_EPS) * lng + lnb + jnp.sum(r * k * rk_w, axis=-1, keepdims=True) * v
    return z, s_end


def _rwkv_specs(nc, rev):
    cidx = (lambda c: nc - 1 - c) if rev else (lambda c: c)
    seq = pl.BlockSpec((RW_HB, RW_CHUNK, HEAD), lambda hg, c: (hg, cidx(c), 0))
    par = pl.BlockSpec((RW_HB, 1, HEAD), lambda hg, c: (hg, 0, 0))
    st = pl.BlockSpec((1, RW_HB, HEAD, HEAD), lambda hg, c: (cidx(c), hg, 0, 0))
    return seq, par, st


def _rwkv_fwd(seqs, pars):
    s = seqs[0].shape[1]
    nc = s // RW_CHUNK

    def body(*refs):
        seq_refs, par_refs = refs[:5], refs[5:10]
        z_ref, st_ref, state = refs[10:]
        c = pl.program_id(1)

        @pl.when(c == 0)
        def _():
            state[...] = jnp.zeros_like(state)

        s0 = state[...]
        st_ref[0] = s0
        z, s_end = _rwkv_chunk(s0, *[r[...] for r in seq_refs], *[r[...] for r in par_refs])
        z_ref[...] = z
        state[...] = s_end

    seq, par, st = _rwkv_specs(nc, False)
    return pl.pallas_call(
        body, name="rwkv_fwd", grid=(N_HEADS // RW_HB, nc),
        in_specs=[seq] * 5 + [par] * 5, out_specs=[seq, st],
        out_shape=[jax.ShapeDtypeStruct((N_HEADS, s, HEAD), F32), jax.ShapeDtypeStruct((nc, N_HEADS, HEAD, HEAD), F32)],
        scratch_shapes=[pltpu.VMEM((RW_HB, HEAD, HEAD), F32)],
        compiler_params=_params(("arbitrary", "arbitrary")),
    )(*seqs, *pars)


def _rwkv_bwd(seqs, pars, states, dz):
    s = seqs[0].shape[1]
    nc = s // RW_CHUNK

    def body(*refs):
        seq_refs, par_refs = refs[:5], refs[5:10]
        st_ref, dz_ref = refs[10:12]
        dseq_refs, dpar_refs, dstate = refs[12:17], refs[17:22], refs[22]
        c = pl.program_id(1)

        @pl.when(c == 0)
        def _():
            dstate[...] = jnp.zeros_like(dstate)
            for r in dpar_refs:
                r[...] = jnp.zeros_like(r)

        _, vjp = jax.vjp(_rwkv_chunk, st_ref[0], *[r[...] for r in seq_refs], *[r[...] for r in par_refs])
        g = vjp((dz_ref[...], dstate[...]))
        dstate[...] = g[0]
        for r, gs in zip(dseq_refs, g[1:6]):
            r[...] = gs
        for r, gp in zip(dpar_refs, g[6:]):
            r[...] += gp

    seq, par, st = _rwkv_specs(nc, True)
    seq_sh = jax.ShapeDtypeStruct((N_HEADS, s, HEAD), F32)
    par_sh = jax.ShapeDtypeStruct((N_HEADS, 1, HEAD), F32)
    outs = pl.pallas_call(
        body, name="rwkv_bwd", grid=(N_HEADS // RW_HB, nc),
        in_specs=[seq] * 5 + [par] * 5 + [st, seq],
        out_specs=[seq] * 5 + [par] * 5, out_shape=[seq_sh] * 5 + [par_sh] * 5,
        scratch_shapes=[pltpu.VMEM((RW_HB, HEAD, HEAD), F32)],
        compiler_params=_params(("arbitrary", "arbitrary")),
    )(*seqs, *pars, states, dz)
    return list(outs[:5]), list(outs[5:])


def _norm_fn(x, g):
    return (_rms(x, g),)


def _attn_prep_fn(proj, qn_w, kn_w):
    a, b = SB_HEADS, 3 * DIL_GROUP
    return (proj[0:a], proj[a:2 * a], proj[2 * a:3 * a],
            _rms(proj[3 * a:3 * a + b], qn_w), _rms(proj[3 * a + b:3 * a + 2 * b], kn_w), proj[3 * a + 2 * b:])


def _merge_fn(o_sb, o_dl):
    return (jnp.concatenate([_heads_to_nat(o_sb), _heads_to_nat(o_dl)], axis=-1),)


def _rw_mix_fn(x, xp, gn, mix, w0, w1, w2, a0, a1, a2, g1, g2):
    h = _rms(x, gn)
    xx = _rms(xp, gn) - h
    xr, xw, xk, xv, xa, xg = [h + xx * mix[i:i + 1] for i in range(6)]
    w_log = -jax.nn.softplus(-(w0 + _mm(jnp.tanh(_mm(xw, w1)), w2))) - 0.5
    lw = -jnp.exp(w_log)
    ag = jax.nn.sigmoid(a0 + _mm(_mm(xa, a1), a2))
    gate = _mm(jax.nn.sigmoid(_mm(xg, g1)), g2)
    return xr, xk, xv, _nat_to_heads(lw), _nat_to_heads(ag), gate


def _rw_gate_fn(z, gate):
    return (_heads_to_nat(z) * gate,)


def _adamw(name, w, m, v, gparts, row0=0, prev=None):
    big_r, c = w.shape
    r = gparts.shape[1]
    tr = r
    if r % 8 == 0:
        tr = max(t for t in range(8, r + 1, 8) if r % t == 0 and (t * c * 4 <= (1 << 20) or t == 8))
    assert row0 % tr == 0 and (r == big_r or r % 8 == 0)
    off = row0 // tr

    def body(w_ref, m_ref, v_ref, g_ref, *rest):
        go_ref, d_ref, mo_ref, vo_ref = rest[-4:]
        g = g_ref[0].astype(F32)
        for j in range(1, N_DEV):
            g = g + g_ref[j].astype(F32)
        mn = ADAM_B1 * m_ref[...] + (1.0 - ADAM_B1) * g
        vn = ADAM_B2 * v_ref[...] + (1.0 - ADAM_B2) * jnp.square(g)
        m_hat = mn / (1.0 - ADAM_B1 ** ADAM_STEP)
        v_hat = vn / (1.0 - ADAM_B2 ** ADAM_STEP)
        go_ref[...] = g
        d_ref[...] = -ADAM_LR * (m_hat / (jnp.sqrt(v_hat) + ADAM_EPS) + ADAM_WD * w_ref[...])
        mo_ref[...] = mn
        vo_ref[...] = vn

    tile = pl.BlockSpec((tr, c), lambda i: (i + off, 0))
    sh = jax.ShapeDtypeStruct((big_r, c), F32)
    prev = list(prev) if prev is not None else []
    return pl.pallas_call(
        body, name=name, grid=(r // tr,),
        in_specs=([tile, tile, tile, pl.BlockSpec((N_DEV, tr, c), lambda i: (0, i, 0))]
                  + [pl.BlockSpec(memory_space=pl.ANY)] * len(prev)),
        out_specs=[tile] * 4, out_shape=[sh] * 4,
        input_output_aliases={4 + j: j for j in range(len(prev))},
        compiler_params=_params(("arbitrary",)),
    )(w, m, v, gparts, *prev)


def _col_blocks_to_nat(g):
    return jnp.moveaxis(g, 0, 1).reshape(g.shape[1], -1)


def _nat_to_col_blocks(a):
    return jnp.moveaxis(a.reshape(a.shape[0], N_DEV, -1), 1, 0)


AG_GROUPS = ("f00", "att", "f01", "f10", "rw", "f11")
RS_GROUPS = ("f11", "rw", "f10", "f01", "f00", "att")
BF16_GRAD_GROUPS = ("att", "f00")
RW_SHARDED = ('rw_mix', 'rw_w0', 'rw_w1', 'rw_w2', 'rw_a0', 'rw_a1', 'rw_a2', 'rw_g1', 'rw_g2', 'rw_kk', 'rw_ka',
              'rw_wr', 'rw_wk', 'rw_wv', 'rw_wo', 'rw_lnx_g', 'rw_lnx_b')


def _step(x, target, rep, get, put):
    tied = lambda a, zero: a + zero[0, 0].astype(a.dtype)
    s, d = x.shape
    tf = min(512, s)
    tt = min(256, s)
    row = lambda a: a.reshape(1, -1)
    mix_norm = rep["mix_norm"]
    ffw = {(0, 0): get("f00", None)}
    ffn_norm = _col_blocks_to_nat(ffw[(0, 0)]["ffn_norm"].reshape(N_DEV, 4, -1))

    acts = {}

    def ffn(nm, xin, l, h):
        g = ffw[(l, h)]
        out, *acts[(l, h)] = _ffn_fwd(nm, xin, ffn_norm[2 * l + h][None], g["gate"], g["up"], g["down"], min(2 * tf, s))
        return out

    x1 = ffn("ffn00_fwd", x, 0, 0)
    att = get("att", x1)
    w_in = _col_blocks_to_nat(att["attn_w_in"])
    w_out = _col_blocks_to_nat(att["attn_w_out"])
    (h0,) = _tile_fwd("mixnorm0_fwd", _norm_fn, [(x1, "nat")], [mix_norm[0:1]], [((s, d), BF16, "nat")], tt)
    proj = _linear_fwd("attn_in_fwd", h0, w_in, tt, out_layout="hm")
    bias = _bias_tiles(rep["rel_bias"], s)
    prep_pars = [rep["attn_q_norm"], rep["attn_k_norm"]]
    sb_sh, dl_sh = (SB_HEADS, s, HEAD), (3 * DIL_GROUP, s, HEAD)
    sq, sk, sv, qn, kn, vd = _tile_fwd("attn_prep_fwd", _attn_prep_fn, [(proj, "hm")], prep_pars,
                                       [(sb_sh, BF16, "hm")] * 3 + [(dl_sh, BF16, "hm")] * 3, tt // 2)
    o_sb = _sb_fwd(sq, sk, sv)
    o_dl = _dil_fwd(qn, kn, vd, bias)
    (merged,) = _tile_fwd("merge_fwd", _merge_fn, [(o_sb, "hm"), (o_dl, "hm")], [], [((s, 512), BF16, "nat")], tt)
    x2 = _linear_fwd("attn_out_fwd", merged, w_out, tt, residual=x1)
    ffw[(0, 1)] = get("f01", x2)
    x3 = ffn("ffn01_fwd", x2, 0, 1)
    ffw[(1, 0)] = get("f10", x3)
    x4 = ffn("ffn10_fwd", x3, 1, 0)
    rw = get("rw", x4)
    rw_mix = _col_blocks_to_nat(rw["rw_mix"])
    rw_w1, rw_a1, rw_g1 = (rw[k].reshape(d, -1) for k in ("rw_w1", "rw_a1", "rw_g1"))
    rw_w2, rw_a2, rw_g2 = (_col_blocks_to_nat(rw[k]) for k in ("rw_w2", "rw_a2", "rw_g2"))
    rw_w0, rw_a0 = row(rw["rw_w0"]), row(rw["rw_a0"])
    head_par = lambda a: a.reshape(N_HEADS, 1, HEAD)
    scan_pars = [head_par(rw["rw_kk"]), head_par(rw["rw_ka"]), head_par(rep["rw_rk"]),
                 head_par(rw["rw_lnx_g"]), head_par(rw["rw_lnx_b"])]
    w_rkv = [rw[k].reshape(d, d) for k in ("rw_wr", "rw_wk", "rw_wv")]
    w_o = rw["rw_wo"].reshape(d, d)
    x4p = jnp.pad(x4, ((1, 0), (0, 0)))[:-1]
    mix_tiled = [(x4, "nat"), (x4p, "nat")]
    mix_pars = [mix_norm[1:2], rw_mix, rw_w0, rw_w1, rw_w2, rw_a0, rw_a1, rw_a2, rw_g1, rw_g2]
    hm_sh = (N_HEADS, s, HEAD)
    xr, xk, xv, lw, ag, gate = _tile_fwd(
        "rw_mix_fwd", _rw_mix_fn, mix_tiled, mix_pars,
        [((s, d), BF16, "nat")] * 3 + [(hm_sh, F32, "hm")] * 2 + [((s, d), F32, "nat")], tt)
    r_h, k_h, v_h = [_linear_fwd("rw_%s_fwd" % nm, xi, wi, tt, out_layout="hm")
                     for nm, xi, wi in zip("rkv", (xr, xk, xv), w_rkv)]
    scan_seqs = [r_h, lw, k_h, v_h, ag]
    z, states = _rwkv_fwd(scan_seqs, scan_pars)
    (zg,) = _tile_fwd("rw_gate_fwd", _rw_gate_fn, [(z, "hm"), (gate, "nat")], [], [((s, d), BF16, "nat")], tt)
    x5 = _linear_fwd("rw_out_fwd", zg, w_o, tt, residual=x4)
    ffw[(1, 1)] = get("f11", x5)
    y = ffn("ffn11_fwd", x5, 1, 1)
    dy, loss = _loss_head(y, target, tf)

    G = {}
    dgn = {}

    def fb(nm, group, xin, dout, l, h, zero=None, extra=None):
        g = ffw[(l, h)]
        gn = ffn_norm[2 * l + h][None]
        dxin, dgn[(l, h)], dwg, dwu, dwd = _ffn_bwd(nm, xin, dout, gn if zero is None else tied(gn, zero),
                                                   g["gate"], g["up"], g["down"], *acts[(l, h)], tf)
        shard = {"gate": dwg, "up": dwu, "down": dwd}
        if extra is not None:
            shard.update(extra())
        return dxin, put(group, {}, shard)

    dx5, zero = fb("ffn11_bwd", "f11", x5, dy, 1, 1)
    dzg = _linear_dx("rw_out_dx", dx5, w_o, tt)
    G["rw_wo"] = _linear_dw("rw_out_dw", zg, dx5, tf, 512)
    (dz, dgate), _ = _tile_bwd("rw_gate_bwd", _rw_gate_fn, [(z, "hm"), (gate, "nat")], [], [(dzg, "nat")], tt, [True, True])
    (dr_h, dlw, dk_h, dv_h, dag), dscan = _rwkv_bwd(scan_seqs, [tied(scan_pars[0], zero)] + scan_pars[1:], states, dz)
    drkv = (dr_h, dk_h, dv_h)
    for k, gpar in zip(("rw_kk", "rw_ka", "rw_rk", "rw_lnx_g", "rw_lnx_b"), dscan):
        G[k] = gpar
    dxs = []
    for j, (nm, xi, wi) in enumerate(zip("rkv", (xr, xk, xv), w_rkv)):
        dxs.append(_linear_dx("rw_%s_dx" % nm, drkv[j], wi, tt, dy_layout="hm"))
        G["rw_w" + nm] = _linear_dw("rw_%s_dw" % nm, xi, drkv[j], tf, 512, dy_layout="hm")
    (dx4a, dx4p), dmix = _tile_bwd(
        "rw_mix_bwd", _rw_mix_fn, mix_tiled, mix_pars,
        [(dxs[0], "nat"), (dxs[1], "nat"), (dxs[2], "nat"), (dlw, "hm"), (dag, "hm"), (dgate, "nat")],
        tt, [True, True], adds=[dx5, None])
    d_mixn1 = dmix[0]
    for k, gpar in zip(("rw_mix", "rw_w0", "rw_w1", "rw_w2", "rw_a0", "rw_a1", "rw_a2", "rw_g1", "rw_g2"), dmix[1:]):
        G[k] = gpar
    dx4 = dx4a + jnp.pad(dx4p[1:], ((0, 1), (0, 0)))
    for k in ("rw_mix", "rw_w2", "rw_a2", "rw_g2"):
        G[k] = _nat_to_col_blocks(G[k])
    for k in ("rw_w1", "rw_a1", "rw_g1", "rw_wr", "rw_wk", "rw_wv", "rw_wo"):
        G[k] = G[k].reshape(N_DEV, d // N_DEV, -1)
    for k in ("rw_w0", "rw_a0", "rw_kk", "rw_ka", "rw_lnx_g", "rw_lnx_b"):
        G[k] = G[k].reshape(N_DEV, 1, d // N_DEV)
    zero = put("rw", {"rw_rk": G["rw_rk"].reshape(N_HEADS, HEAD)}, {k: G[k] for k in RW_SHARDED})
    dx3, zero = fb("ffn10_bwd", "f10", x3, dx4, 1, 0, zero)
    dx2, zero = fb("ffn01_bwd", "f01", x2, dx3, 0, 1, zero)
    dmerged = _linear_dx("attn_out_dx", dx2, tied(w_out, zero), tt)
    (do_sb, do_dl), _ = _tile_bwd("merge_bwd", _merge_fn, [(o_sb, "hm"), (o_dl, "hm")], [], [(dmerged, "nat")], tt, [True, True])
    dq_sb, dk_sb, dv_sb = _sb_bwd(sq, sk, sv, do_sb)
    dqn, dkn, dvd, dbias = _dil_bwd(qn, kn, vd, bias, do_dl)
    (dproj,), (dqn_w, dkn_w) = _tile_bwd(
        "attn_prep_bwd", _attn_prep_fn, [(proj, "hm")], prep_pars,
        [(dq_sb, "hm"), (dk_sb, "hm"), (dv_sb, "hm"), (dqn, "hm"), (dkn, "hm"), (dvd, "hm")], tt // 2, [True])
    dh0 = _linear_dx("attn_in_dx", dproj, w_in, tt, dy_layout="hm")
    (dx1,), (d_mixn0,) = _tile_bwd("mixnorm0_bwd", _norm_fn, [(x1, "nat")], [mix_norm[0:1]], [(dh0, "nat")], tt,
                                   [True], adds=[dx2])
    order = [(0, 0), (0, 1), (1, 0), (1, 1)]
    norm_grads = lambda: {"ffn_norm": _nat_to_col_blocks(jnp.concatenate([dgn[o] for o in order], axis=0))}
    dx0, zero = fb("ffn00_bwd", "f00", x, dx1, 0, 0, extra=norm_grads)
    G["attn_w_out"] = _linear_dw("attn_out_dw", tied(merged, zero), dx2, tf, 512)
    G["attn_w_in"] = _linear_dw("attn_in_dw", tied(h0, zero), dproj, tf, 512, dy_layout="hm")
    rep_grads = {"mix_norm": jnp.concatenate([d_mixn0, d_mixn1], axis=0), "rel_bias": _bias_tiles_bwd(dbias, s),
                 "attn_q_norm": dqn_w, "attn_k_norm": dkn_w}
    zero = put("att", rep_grads, {k: _nat_to_col_blocks(G[k]) for k in ("attn_w_in", "attn_w_out")})
    return loss, dx0, zero


WEIGHTS = ['ffn_norm', 'ffn_w_gate', 'ffn_w_up', 'ffn_w_down', 'mix_norm', 'rel_bias', 'attn_w_in', 'attn_q_norm',
           'attn_k_norm', 'attn_w_out', 'rw_mix', 'rw_w0', 'rw_w1', 'rw_w2', 'rw_a0', 'rw_a1', 'rw_a2', 'rw_g1', 'rw_g2',
           'rw_kk', 'rw_ka', 'rw_rk', 'rw_wr', 'rw_wk', 'rw_wv', 'rw_wo', 'rw_lnx_g', 'rw_lnx_b']
REPLICATED = ('mix_norm', 'rel_bias', 'attn_q_norm', 'attn_k_norm', 'rw_rk')
BF16_WEIGHTS = ('ffn_w_gate', 'ffn_w_up', 'ffn_w_down', 'attn_w_in', 'attn_w_out', 'rw_wr', 'rw_wk', 'rw_wv', 'rw_wo')


def kernel(x, ffn_norm, ffn_w_gate, ffn_w_up, ffn_w_down, mix_norm, rel_bias, attn_w_in, attn_q_norm, attn_k_norm, attn_w_out, rw_mix, rw_w0, rw_w1, rw_w2, rw_a0, rw_a1, rw_a2, rw_g1, rw_g2, rw_kk, rw_ka, rw_rk, rw_wr, rw_wk, rw_wv, rw_wo, rw_lnx_g, rw_lnx_b, loss_target, m_ffn_norm, m_ffn_w_gate, m_ffn_w_up, m_ffn_w_down, m_mix_norm, m_rel_bias, m_attn_w_in, m_attn_q_norm, m_attn_k_norm, m_attn_w_out, m_rw_mix, m_rw_w0, m_rw_w1, m_rw_w2, m_rw_a0, m_rw_a1, m_rw_a2, m_rw_g1, m_rw_g2, m_rw_kk, m_rw_ka, m_rw_rk, m_rw_wr, m_rw_wk, m_rw_wv, m_rw_wo, m_rw_lnx_g, m_rw_lnx_b, v_ffn_norm, v_ffn_w_gate, v_ffn_w_up, v_ffn_w_down, v_mix_norm, v_rel_bias, v_attn_w_in, v_attn_q_norm, v_attn_k_norm, v_attn_w_out, v_rw_mix, v_rw_w0, v_rw_w1, v_rw_w2, v_rw_a0, v_rw_a1, v_rw_a2, v_rw_g1, v_rw_g2, v_rw_kk, v_rw_ka, v_rw_rk, v_rw_wr, v_rw_wk, v_rw_wv, v_rw_wo, v_rw_lnx_g, v_rw_lnx_b):
    args = locals()
    w = {k: args[k] for k in WEIGHTS}
    cast = lambda k, a: a.astype(BF16) if k in BF16_WEIGHTS else a

    sources = {}
    for l, h in ((0, 0), (0, 1), (1, 0), (1, 1)):
        sources["f%d%d" % (l, h)] = {"gate": cast("ffn_w_gate", ffn_w_gate[l, h]), "up": cast("ffn_w_up", ffn_w_up[l, h]),
                                     "down": cast("ffn_w_down", ffn_w_down[l, h])}
    sources["f00"]["ffn_norm"] = ffn_norm
    drop_lead = lambda a: a[0] if a.ndim == 3 else a
    sources["att"] = {k: cast(k, w[k][0]) for k in ("attn_w_in", "attn_w_out")}
    sources["rw"] = {k: cast(k, drop_lead(w[k])) for k in RW_SHARDED}
    ag, token = {}, None
    for group in AG_GROUPS:
        names, arrays = list(sources[group]), list(sources[group].values())
        if token is not None:
            arrays[0] = arrays[0] + token[0, 0].astype(arrays[0].dtype)
        ag[group] = (names, _exchange_start("ag_start_" + group, arrays, []))
        token = ag[group][1]["token"]
    last_ag_token = token

    def get(group, after):
        names, started = ag[group]
        gathered, _ = _exchange_wait("ag_wait_" + group, started, last_ag_token if after is None else after)
        return dict(zip(names, gathered))

    rs = {}

    def put(group, rep_grads, shard_grads):
        if group in BF16_GRAD_GROUPS:
            shard_grads = {k: v.astype(BF16) for k, v in shard_grads.items()}
        started = _exchange_start("rs_start_" + group, list(rep_grads.values()), list(shard_grads.values()))
        rs[group] = (list(rep_grads), list(shard_grads), started)
        return started["token"]

    loss, dx, last_zero = _step(x[0], loss_target[0], {k: w[k] for k in REPLICATED}, get, put)
    loss = lax.psum(loss, MESH_AXES)

    results = {}
    ffn_prev = {}

    def update(k, parts, row0=0, prev=None):
        c = w[k].shape[-1]
        as2d = lambda a: a.reshape(-1, c)
        return _adamw("adamw_%s_%d" % (k, row0), as2d(w[k]), as2d(args["m_" + k]), as2d(args["v_" + k]),
                      parts.reshape(N_DEV, -1, c), row0, prev)

    after = last_zero
    for group in RS_GROUPS:
        rep_names, shard_names, started = rs[group]
        rep_parts, shard_parts = _exchange_wait("rs_wait_" + group, started, after)
        for k, parts in list(zip(rep_names, rep_parts)) + list(zip(shard_names, shard_parts)):
            if k in ("gate", "up", "down"):
                full = "ffn_w_" + k
                piece = 2 * int(group[1]) + int(group[2])
                ffn_prev[full] = update(full, parts, piece * parts.shape[1], ffn_prev.get(full))
                results[full] = ffn_prev[full]
            else:
                results[k] = update(k, parts)
            after = results[k if k in results else "ffn_w_" + k][0]

    outs = [[results[k][j].reshape(w[k].shape) for k in WEIGHTS] for j in range(4)]
    return (loss, dx[None], *outs[0], *outs[1], *outs[2], *outs[3])
```

```python
import functools
import math

import numpy as np
import jax
import jax.numpy as jnp
from jax import lax
from jax.experimental import pallas as pl
from jax.experimental.pallas import tpu as pltpu

F32, BF16 = jnp.float32, jnp.bfloat16
HI = lax.Precision.HIGH

N_DEV = 8
D_MODEL = 1024
HEAD = 64
N_HEADS = 16
SB_HEADS = 4
DIL_GROUP = 4
DIL_PATTERNS = ((128, 1), (512, 4), (2048, 16))
QBLK = 128
N_BUCKETS = 32
MAX_DISTANCE = 2048
NORM_EPS = 1e-6
GN_EPS = 64e-5
NEG_INF = -1e30
RW_CHUNK = 64
RW_HB = 16
ADAM_LR, ADAM_B1, ADAM_B2, ADAM_EPS, ADAM_WD, ADAM_STEP = 0.001, 0.9, 0.999, 1e-08, 0.01, 10
MESH_AXES = ("x", "y", "c")
VMEM_LIMIT_BYTES = 56 * 1024 * 1024

NN2 = (((1,), (0,)), ((), ()))
NT2 = (((1,), (1,)), ((), ()))
TN2 = (((0,), (0,)), ((), ()))
NN3 = (((2,), (1,)), ((0,), (0,)))
NT3 = (((2,), (2,)), ((0,), (0,)))
TN3 = (((1,), (1,)), ((0,), (0,)))


def _dot(a, b, dims=NN2, prec=None):
    return lax.dot_general(a, b, dims, precision=prec, preferred_element_type=F32)


def _params(sem=None):
    return pltpu.CompilerParams(dimension_semantics=sem, vmem_limit_bytes=VMEM_LIMIT_BYTES)


@jax.custom_vjp
def _mm(x, w):
    return _dot(x.astype(BF16), w.astype(BF16))


def _mm_fwd(x, w):
    return _mm(x, w), (x, w)


def _mm_bwd(res, dy):
    x, w = res
    dyb = dy.astype(BF16)
    return (_dot(dyb, w.astype(BF16), NT2).astype(x.dtype), _dot(x.astype(BF16), dyb, TN2).astype(w.dtype))


_mm.defvjp(_mm_fwd, _mm_bwd)


def _rms(x, g):
    return x * lax.rsqrt(jnp.mean(x * x, axis=-1, keepdims=True) + NORM_EPS) * g


def _log_sigmoid(z):
    return jnp.minimum(z, 0.0) - jnp.log(1.0 + jnp.exp(-jnp.abs(z)))


def _heads_to_nat(v3):
    return jnp.concatenate([v3[h] for h in range(v3.shape[0])], axis=-1)


def _nat_to_heads(v2):
    return jnp.stack([v2[:, h * HEAD:(h + 1) * HEAD] for h in range(v2.shape[1] // HEAD)], axis=0)


def _exchange(name, gathers, scatters):
    n_g = len(gathers)
    arrays = list(gathers) + list(scatters)
    n = len(arrays)
    out_shape = [jax.ShapeDtypeStruct((N_DEV,) + a.shape, a.dtype) for a in gathers]
    out_shape += [jax.ShapeDtypeStruct(a.shape, a.dtype) for a in scatters]

    def body(*refs):
        ins, outs = refs[:n], refs[n:2 * n]
        send_sems, recv_sems, local_sems = refs[2 * n:]
        x, y, c = lax.axis_index("x"), lax.axis_index("y"), lax.axis_index("c")
        me = 4 * x + 2 * y + c

        def src(i, idx):
            return ins[i] if i < n_g else ins[i].at[idx]

        local = [pltpu.make_async_copy(src(i, me), outs[i].at[me], local_sems.at[i]) for i in range(n)]
        for cp in local:
            cp.start()
        remote = []
        for m in range(1, N_DEV):
            px, py, pc = x ^ ((m >> 2) & 1), y ^ ((m >> 1) & 1), c ^ (m & 1)
            peer = 4 * px + 2 * py + pc
            for i in range(n):
                cp = pltpu.make_async_remote_copy(
                    src_ref=src(i, peer), dst_ref=outs[i].at[me],
                    send_sem=send_sems.at[i, m - 1], recv_sem=recv_sems.at[i, m - 1],
                    device_id=(px, py, pc), device_id_type=pl.DeviceIdType.MESH)
                cp.start()
                arrival = pltpu.make_async_remote_copy(
                    src_ref=src(i, peer), dst_ref=outs[i].at[peer],
                    send_sem=send_sems.at[i, m - 1], recv_sem=recv_sems.at[i, m - 1],
                    device_id=(px, py, pc), device_id_type=pl.DeviceIdType.MESH)
                remote.append((cp, arrival))
        for cp, arrival in remote:
            cp.wait_send()
            arrival.wait_recv()
        for cp in local:
            cp.wait()

    hbm = pl.BlockSpec(memory_space=pltpu.HBM)
    outs = pl.pallas_call(
        body, name=name, out_shape=out_shape,
        in_specs=[hbm] * n, out_specs=[hbm] * n,
        scratch_shapes=[pltpu.SemaphoreType.DMA((n, N_DEV - 1)), pltpu.SemaphoreType.DMA((n, N_DEV - 1)),
                        pltpu.SemaphoreType.DMA((n,))],
    )(*arrays)
    return list(outs[:n_g]), list(outs[n_g:])


def _mesh_peers():
    x, y, c = lax.axis_index("x"), lax.axis_index("y"), lax.axis_index("c")
    peers = []
    for m in range(1, N_DEV):
        px, py, pc = x ^ ((m >> 2) & 1), y ^ ((m >> 1) & 1), c ^ (m & 1)
        peers.append((m, (px, py, pc), 4 * px + 2 * py + pc))
    return 4 * x + 2 * y + c, peers


_HBM_SPEC = pl.BlockSpec(memory_space=pltpu.HBM)
_SEM_SPEC = pl.BlockSpec(memory_space=pltpu.SEMAPHORE)
_DATAFLOW = pltpu.SideEffectType.DATAFLOW_SIDE_EFFECTING


def _exchange_start(name, gathers, scatters):
    n_g = len(gathers)
    arrays = list(gathers) + list(scatters)
    n = len(arrays)
    lands = ([lax.empty((N_DEV,) + a.shape, a.dtype) for a in gathers] + [lax.empty(a.shape, a.dtype) for a in scatters])

    def body(*refs):
        ins, land = refs[:n], refs[n:2 * n]
        send_sems, recv_sems, local_sems, token = refs[2 * n], refs[2 * n + 1], refs[2 * n + 2], refs[-1]
        me, peers = _mesh_peers()
        for m, dev, peer in peers:
            for i in range(n):
                k = i * (N_DEV - 1) + m - 1
                pltpu.make_async_remote_copy(
                    src_ref=ins[i] if i < n_g else ins[i].at[peer], dst_ref=land[i].at[me],
                    send_sem=send_sems.at[k], recv_sem=recv_sems.at[k],
                    device_id=dev, device_id_type=pl.DeviceIdType.MESH).start()
        for i in range(n):
            pltpu.make_async_copy(ins[i] if i < n_g else ins[i].at[me], land[i].at[me], local_sems.at[i]).start()
        token[...] = jnp.zeros_like(token)

    sem = pltpu.SemaphoreType.DMA((n * (N_DEV - 1),))
    outs = pl.pallas_call(
        body, name=name,
        out_shape=([sem, sem, pltpu.SemaphoreType.DMA((n,))] + [pltpu.HBM(a.shape, a.dtype) for a in arrays]
                   + [pltpu.HBM(l.shape, l.dtype) for l in lands] + [jax.ShapeDtypeStruct((8, 128), F32)]),
        in_specs=[_HBM_SPEC] * (2 * n),
        out_specs=[_SEM_SPEC] * 3 + [_HBM_SPEC] * (2 * n) + [pl.BlockSpec(memory_space=pltpu.VMEM)],
        input_output_aliases={i: i + 3 for i in range(2 * n)},
        compiler_params=pltpu.CompilerParams(has_side_effects=_DATAFLOW),
    )(*[pltpu.with_memory_space_constraint(a, pltpu.HBM) for a in arrays],
      *[pltpu.with_memory_space_constraint(l, pltpu.HBM) for l in lands])
    return dict(n_g=n_g, n=n, send=outs[0], recv=outs[1], local=outs[2], srcs=list(outs[3:3 + n]),
                lands=list(outs[3 + n:3 + 2 * n]), token=outs[-1])


def _exchange_wait(name, started, after):
    n, n_g = started["n"], started["n_g"]

    def body(*refs):
        srcs, lands = refs[:n], refs[n:2 * n]
        send_sems, recv_sems, local_sems = refs[2 * n], refs[2 * n + 1], refs[2 * n + 2]
        me, peers = _mesh_peers()
        local = [pltpu.make_async_copy(srcs[i] if i < n_g else srcs[i].at[me], lands[i].at[me], local_sems.at[i])
                 for i in range(n)]
        for m, dev, peer in peers:
            for i in range(n):
                k = i * (N_DEV - 1) + m - 1
                cp = pltpu.make_async_remote_copy(
                    src_ref=srcs[i] if i < n_g else srcs[i].at[peer], dst_ref=lands[i].at[peer],
                    send_sem=send_sems.at[k], recv_sem=recv_sems.at[k],
                    device_id=dev, device_id_type=pl.DeviceIdType.MESH)
                cp.wait_send()
                cp.wait_recv()
        for cp in local:
            cp.wait()

    outs = pl.pallas_call(
        body, name=name,
        out_shape=([pltpu.HBM(a.shape, a.dtype) for a in started["srcs"]]
                   + [pltpu.HBM(l.shape, l.dtype) for l in started["lands"]]),
        in_specs=[_HBM_SPEC] * (2 * n) + [_SEM_SPEC] * 3 + [pl.BlockSpec(memory_space=pl.ANY)],
        out_specs=[_HBM_SPEC] * (2 * n), input_output_aliases={i: i for i in range(2 * n)},
        compiler_params=pltpu.CompilerParams(has_side_effects=_DATAFLOW),
    )(*started["srcs"], *started["lands"], started["send"], started["recv"], started["local"], after)
    return list(outs[n:n + n_g]), list(outs[n + n_g:])


def _tile_spec(shape, layout, t):
    if layout == "nat":
        return pl.BlockSpec((t, shape[1]), lambda i: (i, 0))
    return pl.BlockSpec((shape[0], t, shape[2]), lambda i: (0, i, 0))


def _full_spec(shape):
    nd = len(shape)
    return pl.BlockSpec(tuple(shape), lambda i: (0,) * nd)


def _seq_len(a, layout):
    return a.shape[0] if layout == "nat" else a.shape[1]


def _tile_fwd(name, f, tiled, params, outs, t):
    nt, npar = len(tiled), len(params)
    s = _seq_len(*tiled[0])

    def body(*refs):
        vals = [r[...] for r in refs[:nt + npar]]
        res = f(*vals)
        for r, o in zip(refs[nt + npar:], res):
            r[...] = o.astype(r.dtype)

    return pl.pallas_call(
        body, name=name, grid=(s // t,),
        in_specs=[_tile_spec(a.shape, l, t) for a, l in tiled] + [_full_spec(p.shape) for p in params],
        out_specs=[_tile_spec(sh, l, t) for sh, _, l in outs],
        out_shape=[jax.ShapeDtypeStruct(sh, dt) for sh, dt, _ in outs],
        compiler_params=_params(("arbitrary",)),
    )(*[a for a, _ in tiled], *params)


def _tile_bwd(name, f, tiled, params, cts, t, need, adds=None):
    nt, npar, nc = len(tiled), len(params), len(cts)
    s = _seq_len(*tiled[0])
    need_idx = [k for k in range(nt) if need[k]]
    adds = adds or [None] * len(need_idx)
    add_arrays = [(a, tiled[k][1]) for a, k in zip(adds, need_idx) if a is not None]
    n_add = len(add_arrays)

    def body(*refs):
        i = pl.program_id(0)
        vals = [r[...] for r in refs[:nt + npar]]
        ct_refs = refs[nt + npar:nt + npar + nc]
        add_refs = refs[nt + npar + nc:nt + npar + nc + n_add]
        out_refs = refs[nt + npar + nc + n_add:]
        res, vjp = jax.vjp(f, *vals)
        grads = vjp(tuple(r[...].astype(o.dtype) for r, o in zip(ct_refs, res)))
        a = 0
        for j, k in enumerate(need_idx):
            g = grads[k]
            if adds[j] is not None:
                g = g + add_refs[a][...]
                a += 1
            out_refs[j][...] = g.astype(out_refs[j].dtype)
        for j in range(npar):
            r = out_refs[len(need_idx) + j]

            @pl.when(i == 0)
            def _():
                r[...] = jnp.zeros_like(r)

            r[...] += grads[nt + j]

    outs = pl.pallas_call(
        body, name=name, grid=(s // t,),
        in_specs=([_tile_spec(a.shape, l, t) for a, l in tiled] + [_full_spec(p.shape) for p in params]
                  + [_tile_spec(a.shape, l, t) for a, l in cts] + [_tile_spec(a.shape, l, t) for a, l in add_arrays]),
        out_specs=([_tile_spec(tiled[k][0].shape, tiled[k][1], t) for k in need_idx]
                   + [_full_spec(p.shape) for p in params]),
        out_shape=([jax.ShapeDtypeStruct(tiled[k][0].shape, F32) for k in need_idx]
                   + [jax.ShapeDtypeStruct(p.shape, F32) for p in params]),
        compiler_params=_params(("arbitrary",)),
    )(*[a for a, _ in tiled], *params, *[a for a, _ in cts], *[a for a, _ in add_arrays])
    return list(outs[:len(need_idx)]), list(outs[len(need_idx):])


def _linear_fwd(name, x, w, t, out_layout="nat", residual=None):
    s, k = x.shape
    n = w.shape[1]
    has_res = residual is not None

    def body(*refs):
        x_ref, w_ref = refs[0], refs[1]
        o_ref = refs[-1]
        y = _dot(x_ref[...].astype(BF16), w_ref[...])
        if has_res:
            y = y + refs[2][...]
        if out_layout == "hm":
            for h in range(n // HEAD):
                o_ref[h] = y[:, h * HEAD:(h + 1) * HEAD]
        else:
            o_ref[...] = y

    out_sh = (s, n) if out_layout == "nat" else (n // HEAD, s, HEAD)
    ins = [x, w] + ([residual] if has_res else [])
    in_specs = [_tile_spec(x.shape, "nat", t), _full_spec(w.shape)] + ([_tile_spec((s, n), "nat", t)] if has_res else [])
    return pl.pallas_call(
        body, name=name, grid=(s // t,), in_specs=in_specs,
        out_specs=_tile_spec(out_sh, out_layout, t), out_shape=jax.ShapeDtypeStruct(out_sh, F32),
        compiler_params=_params(("arbitrary",)),
    )(*ins)


def _linear_dx(name, dy, w, t, dy_layout="nat"):
    k, n = w.shape
    s = _seq_len(dy, dy_layout)

    def body(dy_ref, w_ref, o_ref):
        dy = _heads_to_nat(dy_ref[...].astype(BF16)) if dy_layout == "hm" else dy_ref[...].astype(BF16)
        o_ref[...] = _dot(dy, w_ref[...], NT2)

    return pl.pallas_call(
        body, name=name, grid=(s // t,),
        in_specs=[_tile_spec(dy.shape, dy_layout, t), _full_spec(w.shape)],
        out_specs=_tile_spec((s, k), "nat", t), out_shape=jax.ShapeDtypeStruct((s, k), F32),
        compiler_params=_params(("arbitrary",)),
    )(dy, w)


def _linear_dw(name, x, dy, t, nb, dy_layout="nat"):
    s, k = x.shape
    n = dy.shape[1] if dy_layout == "nat" else dy.shape[0] * HEAD

    def body(x_ref, dy_ref, o_ref):
        i = pl.program_id(1)

        @pl.when(i == 0)
        def _():
            o_ref[...] = jnp.zeros_like(o_ref)

        dy = _heads_to_nat(dy_ref[...].astype(BF16)) if dy_layout == "hm" else dy_ref[...].astype(BF16)
        o_ref[...] += _dot(x_ref[...].astype(BF16), dy, TN2)

    if dy_layout == "hm":
        dy_spec = pl.BlockSpec((nb // HEAD, t, HEAD), lambda j, i: (j, i, 0))
    else:
        dy_spec = pl.BlockSpec((t, nb), lambda j, i: (i, j))
    return pl.pallas_call(
        body, name=name, grid=(n // nb, s // t),
        in_specs=[pl.BlockSpec((t, k), lambda j, i: (i, 0)), dy_spec],
        out_specs=pl.BlockSpec((k, nb), lambda j, i: (0, j)), out_shape=jax.ShapeDtypeStruct((k, n), F32),
        compiler_params=_params(("arbitrary", "arbitrary")),
    )(x, dy)


def _ffn_fwd(name, x, gn, wg, wu, wd, t):
    s, d = x.shape
    f8 = wg.shape[-1]

    def body(x_ref, g_ref, wg_ref, wu_ref, wd_ref, o_ref, gk_ref, uk_ref, h_scr, acc):
        k = pl.program_id(1)

        @pl.when(k == 0)
        def _():
            h_scr[...] = _rms(x_ref[...], g_ref[...]).astype(BF16)
            acc[...] = jnp.zeros_like(acc)

        hb = h_scr[...]
        gk = _dot(hb, wg_ref[0])
        uk = _dot(hb, wu_ref[0])
        gk_ref[0] = gk
        uk_ref[0] = uk
        a = gk * jax.nn.sigmoid(gk) * uk
        acc[...] += _dot(a.astype(BF16), wd_ref[0])

        @pl.when(k == N_DEV - 1)
        def _():
            o_ref[...] = x_ref[...] + 0.5 * acc[...]

    wspec = lambda shp: pl.BlockSpec((1,) + shp, lambda i, k: (k, 0, 0))
    act = pl.BlockSpec((1, t, f8), lambda i, k: (k, i, 0))
    act_sh = jax.ShapeDtypeStruct((N_DEV, s, f8), F32)
    return pl.pallas_call(
        body, name=name, grid=(s // t, N_DEV),
        in_specs=[pl.BlockSpec((t, d), lambda i, k: (i, 0)), pl.BlockSpec((1, d), lambda i, k: (0, 0)),
                  wspec((d, f8)), wspec((d, f8)), wspec((f8, d))],
        out_specs=[pl.BlockSpec((t, d), lambda i, k: (i, 0)), act, act],
        out_shape=[jax.ShapeDtypeStruct((s, d), F32), act_sh, act_sh],
        scratch_shapes=[pltpu.VMEM((t, d), BF16), pltpu.VMEM((t, d), F32)],
        compiler_params=_params(("arbitrary", "arbitrary")),
    )(x, gn, wg, wu, wd)


def _ffn_bwd(name, x, dy, gn, wg, wu, wd, gact, uact, t):
    s, d = x.shape
    f8 = wg.shape[-1]
    last = N_DEV - 1

    def body(x_ref, dy_ref, g_ref, wg_ref, wu_ref, wd_ref, gk_ref, uk_ref,
             dx_ref, dg_ref, dwg_ref, dwu_ref, dwd_ref, dh_scr):
        k, i = pl.program_id(0), pl.program_id(1)
        x = x_ref[...]
        rs = lax.rsqrt(jnp.mean(x * x, axis=-1, keepdims=True) + NORM_EPS)
        xn = x * rs
        hb = (xn * g_ref[...]).astype(BF16)
        dob = (0.5 * dy_ref[...]).astype(BF16)
        wgk, wuk, wdk = wg_ref[0], wu_ref[0], wd_ref[0]
        gk, uk = gk_ref[0], uk_ref[0]
        sg = jax.nn.sigmoid(gk)
        sk = gk * sg
        da = _dot(dob, wdk, NT2)
        du = (da * sk).astype(BF16)
        dg = (da * uk * (sg * (1.0 + gk * (1.0 - sg)))).astype(BF16)

        @pl.when(i == 0)
        def _():
            dwg_ref[...] = jnp.zeros_like(dwg_ref)
            dwu_ref[...] = jnp.zeros_like(dwu_ref)
            dwd_ref[...] = jnp.zeros_like(dwd_ref)

        dwd_ref[0] += _dot((sk * uk).astype(BF16), dob, TN2)
        dwg_ref[0] += _dot(hb, dg, TN2)
        dwu_ref[0] += _dot(hb, du, TN2)
        dh = _dot(dg, wgk, NT2) + _dot(du, wuk, NT2)
        rows = pl.ds(pl.multiple_of(i * t, t), t)

        @pl.when(k == 0)
        def _():
            dh_scr[rows, :] = dh

        @pl.when(k > 0)
        def _():
            dh_scr[rows, :] += dh

        @pl.when(jnp.logical_and(k == last, i == 0))
        def _():
            dg_ref[...] = jnp.zeros_like(dg_ref)

        @pl.when(k == last)
        def _():
            dht = dh_scr[rows, :]
            dg_ref[...] += jnp.sum(dht * xn, axis=0, keepdims=True)
            dxn = dht * g_ref[...]
            dx_ref[...] = dy_ref[...] + rs * (dxn - xn * jnp.mean(dxn * xn, axis=-1, keepdims=True))

    wspec = lambda shp: pl.BlockSpec((1,) + shp, lambda k, i: (k, 0, 0))
    tile = pl.BlockSpec((t, d), lambda k, i: (i, 0))
    act = pl.BlockSpec((1, t, f8), lambda k, i: (k, i, 0))
    return pl.pallas_call(
        body, name=name, grid=(N_DEV, s // t),
        in_specs=[tile, tile, pl.BlockSpec((1, d), lambda k, i: (0, 0)), wspec((d, f8)), wspec((d, f8)), wspec((f8, d)),
                  act, act],
        out_specs=[pl.BlockSpec((t, d), lambda k, i: (jnp.where(k == last, i, 0), 0)),
                   pl.BlockSpec((1, d), lambda k, i: (0, 0)),
                   pl.BlockSpec((1, d, f8), lambda k, i: (k, 0, 0)), pl.BlockSpec((1, d, f8), lambda k, i: (k, 0, 0)),
                   pl.BlockSpec((1, f8, d), lambda k, i: (k, 0, 0))],
        out_shape=[jax.ShapeDtypeStruct((s, d), F32), jax.ShapeDtypeStruct((1, d), F32),
                   jax.ShapeDtypeStruct((N_DEV, d, f8), F32), jax.ShapeDtypeStruct((N_DEV, d, f8), F32),
                   jax.ShapeDtypeStruct((N_DEV, f8, d), F32)],
        scratch_shapes=[pltpu.VMEM((s, d), F32)],
        compiler_params=_params(("arbitrary", "arbitrary")),
    )(x, dy, gn, wg, wu, wd, gact, uact)


def _loss_head(y, target, t):
    s, d = y.shape

    def body(y_ref, t_ref, dy_ref, l_ref):
        i = pl.program_id(0)
        err = y_ref[...] - t_ref[...]
        dy_ref[...] = err * (1.0 / d)

        @pl.when(i == 0)
        def _():
            l_ref[...] = jnp.zeros_like(l_ref)

        l_ref[...] += 0.5 * jnp.sum(jnp.mean(err * err, axis=-1, keepdims=True), axis=0, keepdims=True)

    tile = pl.BlockSpec((t, d), lambda i: (i, 0))
    dy, l = pl.pallas_call(
        body, name="loss_head", grid=(s // t,), in_specs=[tile, tile],
        out_specs=[tile, pl.BlockSpec((1, 1), lambda i: (0, 0))],
        out_shape=[jax.ShapeDtypeStruct((s, d), F32), jax.ShapeDtypeStruct((1, 1), F32)],
        compiler_params=_params(("arbitrary",)),
    )(y, target)
    return dy, l[0, 0]


SB_KEY_TILE = 512


def _sb_scan_mats():
    row = lax.broadcasted_iota(jnp.int32, (QBLK, QBLK), 0)
    col = lax.broadcasted_iota(jnp.int32, (QBLK, QBLK), 1)
    return (row > col).astype(F32).astype(BF16), (row < col).astype(F32).astype(BF16)


def _sb_tile_scan(x, mat, reverse):
    nsub = x.shape[1] // QBLK
    outs, carry = [None] * nsub, jnp.zeros((x.shape[0], 1), F32)
    for i in (reversed(range(nsub)) if reverse else range(nsub)):
        xs = x[:, i * QBLK:(i + 1) * QBLK]
        hi = xs.astype(BF16)
        lo = (xs - hi.astype(F32)).astype(BF16)
        outs[i] = _dot(hi, mat) + _dot(lo, mat) + carry
        carry = carry + jnp.sum(xs, axis=1, keepdims=True)
    return jnp.concatenate(outs, axis=1), carry


def _sb_before_query(n, t, kt):
    row = lax.broadcasted_iota(jnp.int32, (QBLK, kt), 0)
    col = lax.broadcasted_iota(jnp.int32, (QBLK, kt), 1)
    return t * kt + col < n * QBLK + row


def _sb_fwd(q, k, v):
    _, s, _ = q.shape
    scale = HEAD ** -0.5
    kt = min(SB_KEY_TILE, s)

    def body(q_ref, k_ref, v_ref, o_ref):
        n = pl.program_id(1)
        qb = (q_ref[0] * scale).astype(q_ref.dtype)
        suffix, _ = _sb_scan_mats()
        n_tiles = lax.div(n, jnp.int32(kt // QBLK)) + 1

        def tile(t, c, acc, diagonal):
            rows = pl.ds(pl.multiple_of(t * kt, kt), kt)
            kb, vb = k_ref[0, rows, :], v_ref[0, rows, :]
            z = _dot(qb, kb, NT2)
            lk = _log_sigmoid(-z)
            log_beta = z + lk
            if diagonal:
                ok = _sb_before_query(n, t, kt)
                lk = jnp.where(ok, lk, 0.0)
            later, total = _sb_tile_scan(lk, suffix, True)
            w = jnp.exp(log_beta + later + c)
            if diagonal:
                w = jnp.where(ok, w, 0.0)
            return c + total, acc + _dot(w.astype(BF16), vb)

        carry = tile(n_tiles - 1, jnp.zeros((QBLK, 1), F32), jnp.zeros((QBLK, HEAD), F32), True)
        _, acc = lax.fori_loop(1, n_tiles, lambda jj, cr: tile(n_tiles - 1 - jj, cr[0], cr[1], False), carry)
        o_ref[0] = acc

    return pl.pallas_call(
        body, name="sb_fwd", grid=(SB_HEADS, s // QBLK),
        in_specs=[pl.BlockSpec((1, QBLK, HEAD), lambda h, n: (h, n, 0)),
                  pl.BlockSpec((1, s, HEAD), lambda h, n: (h, 0, 0)),
                  pl.BlockSpec((1, s, HEAD), lambda h, n: (h, 0, 0))],
        out_specs=pl.BlockSpec((1, QBLK, HEAD), lambda h, n: (h, n, 0)),
        out_shape=jax.ShapeDtypeStruct((SB_HEADS, s, HEAD), F32),
        compiler_params=_params(("arbitrary", "arbitrary")),
    )(q, k, v)


def _sb_bwd(q, k, v, do):
    _, s, _ = q.shape
    scale = HEAD ** -0.5
    kt = min(SB_KEY_TILE, s)

    def body(q_ref, k_ref, v_ref, do_ref, dq_ref, dk_ref, dv_ref, e_scr, beta_scr):
        n = pl.program_id(1)

        @pl.when(n == 0)
        def _():
            dk_ref[...] = jnp.zeros_like(dk_ref)
            dv_ref[...] = jnp.zeros_like(dv_ref)

        qb = (q_ref[0] * scale).astype(q_ref.dtype)
        dob = do_ref[0].astype(BF16)
        suffix, prefix = _sb_scan_mats()
        n_tiles = lax.div(n, jnp.int32(kt // QBLK)) + 1

        def weights(t, c, diagonal):
            rows = pl.ds(pl.multiple_of(t * kt, kt), kt)
            kb, vb = k_ref[0, rows, :], v_ref[0, rows, :]
            z = _dot(qb, kb, NT2)
            lk = _log_sigmoid(-z)
            log_beta = z + lk
            if diagonal:
                ok = _sb_before_query(n, t, kt)
                lk = jnp.where(ok, lk, 0.0)
            later, total = _sb_tile_scan(lk, suffix, True)
            w = jnp.exp(log_beta + later + c)
            if diagonal:
                w = jnp.where(ok, w, 0.0)
            e_scr[t] = w * _dot(dob, vb, NT2)
            beta_scr[t] = jnp.exp(log_beta)
            dv_ref[0, rows, :] += _dot(w.astype(BF16), dob, TN2)
            return c + total

        c_diag = weights(n_tiles - 1, jnp.zeros((QBLK, 1), F32), True)
        lax.fori_loop(1, n_tiles, lambda jj, c: weights(n_tiles - 1 - jj, c, False), c_diag)

        def grads(t, pc, dq, diagonal):
            rows = pl.ds(pl.multiple_of(t * kt, kt), kt)
            kb = k_ref[0, rows, :]
            e, beta = e_scr[t], beta_scr[t]
            before, total = _sb_tile_scan(e, prefix, False)
            dz = e * (1.0 - beta) - beta * (before + pc)
            if diagonal:
                dz = jnp.where(_sb_before_query(n, t, kt), dz, 0.0)
            dz = dz.astype(BF16)
            dk_ref[0, rows, :] += _dot(dz, qb, TN2)
            return pc + total, dq + _dot(dz, kb)

        carry = lax.fori_loop(0, n_tiles - 1, lambda t, cr: grads(t, cr[0], cr[1], False),
                              (jnp.zeros((QBLK, 1), F32), jnp.zeros((QBLK, HEAD), F32)))
        _, dq = grads(n_tiles - 1, carry[0], carry[1], True)
        dq_ref[0] = dq * scale

    qspec = pl.BlockSpec((1, QBLK, HEAD), lambda h, n: (h, n, 0))
    full = pl.BlockSpec((1, s, HEAD), lambda h, n: (h, 0, 0))
    sh = jax.ShapeDtypeStruct((SB_HEADS, s, HEAD), F32)
    return pl.pallas_call(
        body, name="sb_bwd", grid=(SB_HEADS, s // QBLK),
        in_specs=[qspec, full, full, qspec],
        out_specs=[qspec, full, full], out_shape=[sh, sh, sh],
        scratch_shapes=[pltpu.VMEM((s // kt, QBLK, kt), F32), pltpu.VMEM((s // kt, QBLK, kt), F32)],
        compiler_params=_params(("arbitrary", "arbitrary")),
    )(q, k, v, do)


def _t5_bucket_np(dist):
    max_exact = N_BUCKETS // 2
    d = np.maximum(dist, 1).astype(np.float32)
    large = max_exact + (np.log(d / np.float32(max_exact)) / np.float32(math.log(MAX_DISTANCE / max_exact))
                         * np.float32(N_BUCKETS - max_exact)).astype(np.int32)
    large = np.minimum(large, N_BUCKETS - 1)
    return np.where(dist < max_exact, dist, large)


def _dil_layout(s):
    assert all(s % (QBLK * r) == 0 and window // r == QBLK for window, r in DIL_PATTERNS)
    tiles, buckets = [], []
    i = np.arange(QBLK)[:, None]
    j = np.arange(QBLK)[None, :]
    for g, (window, r) in enumerate(DIL_PATTERNS):
        for off in (0, 1):
            dist = QBLK * off + i - j
            ok = (dist >= 0) & (dist <= window // r)
            tiles.append((g, off))
            buckets.append(np.where(ok, _t5_bucket_np(np.maximum(dist, 0) * r), -1).astype(np.int32))
    return tiles, np.stack(buckets)


def _bias_tiles(rel_bias, s):
    tiles, buckets = _dil_layout(s)
    nt = len(tiles)
    present = [sorted(set(np.unique(buckets[k]).tolist()) - {-1}) for k in range(nt)]

    def body(rel_ref, b_ref, o_ref):
        j = pl.program_id(0)
        for k, (g, _) in enumerate(tiles):
            bk = b_ref[k]
            tile = jnp.full((QBLK, QBLK), NEG_INF, F32)
            for b in present[k]:
                tile = jnp.where(bk == b, rel_ref[b, g * DIL_GROUP + j], tile)
            o_ref[0, k] = tile

    return pl.pallas_call(
        body, name="bias_tiles", grid=(DIL_GROUP,),
        in_specs=[pl.BlockSpec(memory_space=pltpu.SMEM), pl.BlockSpec((nt, QBLK, QBLK), lambda j: (0, 0, 0))],
        out_specs=pl.BlockSpec((1, nt, QBLK, QBLK), lambda j: (j, 0, 0, 0)),
        out_shape=jax.ShapeDtypeStruct((DIL_GROUP, nt, QBLK, QBLK), F32),
        compiler_params=_params(("arbitrary",)),
    )(rel_bias, jnp.asarray(buckets))


def _bias_tiles_bwd(dbias, s):
    tiles, buckets = _dil_layout(s)
    nt = len(tiles)
    present = [sorted(set(np.unique(buckets[k]).tolist()) - {-1}) for k in range(nt)]

    def body(d_ref, b_ref, o_ref):
        j = pl.program_id(0)

        @pl.when(j == 0)
        def _():
            for b in range(N_BUCKETS):
                for col in range(3 * DIL_GROUP):
                    o_ref[b, col] = jnp.float32(0.0)

        for k, (g, _) in enumerate(tiles):
            bk, dk = b_ref[k], d_ref[0, k]
            for b in present[k]:
                o_ref[b, g * DIL_GROUP + j] += jnp.sum(jnp.where(bk == b, dk, 0.0))

    return pl.pallas_call(
        body, name="bias_tiles_bwd", grid=(DIL_GROUP,),
        in_specs=[pl.BlockSpec((1, nt, QBLK, QBLK), lambda j: (j, 0, 0, 0)),
                  pl.BlockSpec((nt, QBLK, QBLK), lambda j: (0, 0, 0))],
        out_specs=pl.BlockSpec(memory_space=pltpu.SMEM),
        out_shape=jax.ShapeDtypeStruct((N_BUCKETS, 3 * DIL_GROUP), F32),
        compiler_params=_params(("arbitrary",)),
    )(dbias, jnp.asarray(buckets))


DIL_PAIRS_PER_STEP = 4


def _dil_rows(g, s, pair):
    _, r = DIL_PATTERNS[g]
    nb = s // (QBLK * r)
    c, n = lax.div(pair, jnp.int32(nb)), lax.rem(pair, jnp.int32(nb))
    start = c + (r * QBLK) * n
    before = jnp.where(n > 0, start - r * QBLK, start)
    if r == 1:
        return pl.ds(start, QBLK), pl.ds(before, QBLK), n > 0
    return pl.ds(start, QBLK, stride=r), pl.ds(before, QBLK, stride=r), n > 0


def _dil_logits(qb, k_ref, rows, before, has_before, b_ref):
    k0, k1 = k_ref[0, rows, :].astype(BF16), k_ref[0, before, :].astype(BF16)
    l0 = _dot(qb, k0, NT2) + b_ref[0, 0]
    l1 = jnp.where(has_before, _dot(qb, k1, NT2) + b_ref[0, 1], NEG_INF)
    return k0, k1, l0, l1


def _dil_group_specs(g, s):
    head = pl.BlockSpec((1, s, HEAD), lambda j, p: (DIL_GROUP * g + j, 0, 0), pipeline_mode=pl.Buffered(1))
    return [head, head, head, pl.BlockSpec((1, 2, QBLK, QBLK), lambda j, p: (j, g, 0, 0))]


def _dil_group_fwd(g, qn, kn, v, bias):
    _, s, _ = qn.shape
    scale = HEAD ** -0.5
    steps = (s // QBLK) // DIL_PAIRS_PER_STEP

    def body(q_ref, k_ref, v_ref, b_ref, o_ref):
        for u in range(DIL_PAIRS_PER_STEP):
            rows, before, has_before = _dil_rows(g, s, pl.program_id(1) * DIL_PAIRS_PER_STEP + u)
            qb = (q_ref[0, rows, :] * scale).astype(BF16)
            _, _, l0, l1 = _dil_logits(qb, k_ref, rows, before, has_before, b_ref)
            m = jnp.max(jnp.maximum(l0, l1), axis=1, keepdims=True)
            p0, p1 = jnp.exp(l0 - m), jnp.exp(l1 - m)
            den = jnp.sum(p0 + p1, axis=1, keepdims=True)
            inv = 1.0 / den
            o = (_dot((p0 * inv).astype(BF16), v_ref[0, rows, :].astype(BF16))
                 + _dot((p1 * inv).astype(BF16), v_ref[0, before, :].astype(BF16)))
            lse = jnp.broadcast_to(m + jnp.log(den), (QBLK, HEAD))
            o_ref[0, rows, :] = jnp.concatenate([o, lse], axis=1)

    return pl.pallas_call(
        body, name="dil%d_fwd" % g, grid=(DIL_GROUP, steps), in_specs=_dil_group_specs(g, s),
        out_specs=pl.BlockSpec((1, s, 2 * HEAD), lambda j, p: (j, 0, 0)),
        out_shape=jax.ShapeDtypeStruct((DIL_GROUP, s, 2 * HEAD), F32),
        compiler_params=_params(("arbitrary", "arbitrary")),
    )(qn, kn, v, bias)


def _dil_group_bwd(g, qn, kn, v, bias, ol, dol, prev):
    _, s, _ = qn.shape
    scale = HEAD ** -0.5
    steps = (s // QBLK) // DIL_PAIRS_PER_STEP
    prev = list(prev) if prev is not None else []

    def body(q_ref, k_ref, v_ref, b_ref, ol_ref, dol_ref, *rest):
        dq_ref, dk_ref, dv_ref, db_ref = rest[-4:]

        @pl.when(pl.program_id(1) == 0)
        def _():
            for r in (dk_ref, dv_ref, db_ref):
                r[...] = jnp.zeros_like(r)

        for u in range(DIL_PAIRS_PER_STEP):
            rows, before, has_before = _dil_rows(g, s, pl.program_id(1) * DIL_PAIRS_PER_STEP + u)
            qb = (q_ref[0, rows, :] * scale).astype(BF16)
            k0, k1, l0, l1 = _dil_logits(qb, k_ref, rows, before, has_before, b_ref)
            v0, v1 = v_ref[0, rows, :].astype(BF16), v_ref[0, before, :].astype(BF16)
            out_lse, d_out_lse = ol_ref[0, rows, :], dol_ref[0, rows, :]
            o, lse = out_lse[:, :HEAD], out_lse[:, HEAD:HEAD + 1]
            do, dlse = d_out_lse[:, :HEAD], d_out_lse[:, HEAD:HEAD + 1]
            dob = do.astype(BF16)
            p0, p1 = jnp.exp(l0 - lse), jnp.exp(l1 - lse)
            shift = dlse - jnp.sum(do * o, axis=1, keepdims=True)
            dl0 = p0 * (_dot(dob, v0, NT2) + shift)
            dl1 = p1 * (_dot(dob, v1, NT2) + shift)
            dl0b, dl1b = dl0.astype(BF16), dl1.astype(BF16)
            dq_ref[0, rows, :] = (_dot(dl0b, k0) + _dot(dl1b, k1)) * scale
            dk_ref[0, rows, :] += _dot(dl0b, qb, TN2)
            dk_ref[0, before, :] += _dot(dl1b, qb, TN2)
            dv_ref[0, rows, :] += _dot(p0.astype(BF16), dob, TN2)
            dv_ref[0, before, :] += _dot(p1.astype(BF16), dob, TN2)
            db_ref[0, 0] += dl0
            db_ref[0, 1] += dl1

    head_out = pl.BlockSpec((1, s, HEAD), lambda j, p: (DIL_GROUP * g + j, 0, 0), pipeline_mode=pl.Buffered(1))
    rows128 = pl.BlockSpec((1, s, 2 * HEAD), lambda j, p: (j, 0, 0), pipeline_mode=pl.Buffered(1))
    full_sh = jax.ShapeDtypeStruct(qn.shape, F32)
    return pl.pallas_call(
        body, name="dil%d_bwd" % g, grid=(DIL_GROUP, steps),
        in_specs=_dil_group_specs(g, s) + [rows128, rows128] + [pl.BlockSpec(memory_space=pl.ANY)] * len(prev),
        out_specs=[head_out, head_out, head_out, pl.BlockSpec((1, 2, QBLK, QBLK), lambda j, p: (j, 0, 0, 0))],
        out_shape=[full_sh, full_sh, full_sh, jax.ShapeDtypeStruct((DIL_GROUP, 2, QBLK, QBLK), F32)],
        input_output_aliases={6 + i: i for i in range(len(prev))},
        compiler_params=_params(("arbitrary", "arbitrary")),
    )(qn, kn, v, bias, ol, dol, *prev)


@functools.partial(jax.custom_vjp, nondiff_argnums=(2,))
def _bdot(a, b, dims):
    return _dot(a.astype(BF16), b.astype(BF16), dims)


def _bdot_fwd(a, b, dims):
    return _bdot(a, b, dims), (a, b)


def _bdot_bwd(dims, res, dc):
    a, b = res
    nn, nt, tn = (NN2, NT2, TN2) if dims in (NN2, NT2, TN2) else (NN3, NT3, TN3)
    if dims == nn:
        return _bdot(dc, b, nt), _bdot(a, dc, tn)
    if dims == nt:
        return _bdot(dc, b, nn), _bdot(dc, a, tn)
    return _bdot(b, dc, nt), _bdot(a, dc, nn)


_bdot.defvjp(_bdot_fwd, _bdot_bwd)


def _rwkv_chunk(s0, r, lw, kraw, v, ag, kk_w, ka_w, rk_w, lng, lnb):
    hb, c, _ = r.shape
    kk = kraw * kk_w
    kk = kk / jnp.maximum(jnp.sqrt(jnp.sum(kk * kk, axis=-1, keepdims=True)), 1e-12)
    k = kraw * (1.0 + (ag - 1.0) * ka_w)
    a = -kk
    b = kk * ag
    row = lax.broadcasted_iota(jnp.int32, (hb, c, c), 1)
    col = lax.broadcasted_iota(jnp.int32, (hb, c, c), 2)
    lower, strict = row >= col, row > col
    cum = _dot(lower.astype(F32), lw, NN3, lax.Precision.HIGHEST)
    ecum, einv = jnp.exp(cum), jnp.exp(-cum)
    rt, kt, bt = r * ecum, k * einv, b * einv
    at = a * jnp.exp(cum - lw)
    ar = jnp.concatenate([at, rt], axis=1)
    scores = _bdot(ar, jnp.concatenate([bt, kt], axis=1), NT3)
    a_ab = jnp.where(strict, scores[:, :c, :c], 0.0)
    a_ak = jnp.where(strict, scores[:, :c, c:], 0.0)
    p_rb = jnp.where(lower, scores[:, c:, :c], 0.0)
    p_rk = jnp.where(lower, scores[:, c:, c:], 0.0)
    from_s0 = _bdot(ar, s0, NT3)
    rhs = from_s0[:, :c] + _bdot(a_ak, v, NN3)
    inv = (row == col).astype(F32) + a_ab
    pw = a_ab
    for _ in range(int(math.log2(c)) - 1):
        pw = _bdot(pw, pw, NN3)
        inv = inv + _bdot(inv, pw, NN3)
    u = _bdot(inv, rhs, NN3)
    uv = jnp.concatenate([u, v], axis=1)
    y = from_s0[:, c:] + _bdot(jnp.concatenate([p_rb, p_rk], axis=2), uv, NN3)
    cum_end = cum[:, c - 1:c, :]
    dec = jnp.exp(cum_end - cum)
    s_end = s0 * jnp.exp(cum_end) + _bdot(uv, jnp.concatenate([b * dec, k * dec], axis=1), TN3)
    mu = jnp.mean(y, axis=-1, keepdims=True)
    var = jnp.mean(jnp.square(y - mu), axis=-1, keepdims=True)
    z = (y - mu) * lax.rsqrt(var + GN_EPS) * lng + lnb + jnp.sum(r * k * rk_w, axis=-1, keepdims=True) * v
    return z, s_end


def _rwkv_specs(nc, rev):
    cidx = (lambda c: nc - 1 - c) if rev else (lambda c: c)
    seq = pl.BlockSpec((RW_HB, RW_CHUNK, HEAD), lambda hg, c: (hg, cidx(c), 0))
    par = pl.BlockSpec((RW_HB, 1, HEAD), lambda hg, c: (hg, 0, 0))
    st = pl.BlockSpec((1, RW_HB, HEAD, HEAD), lambda hg, c: (cidx(c), hg, 0, 0))
    return seq, par, st


def _rwkv_fwd(seqs, pars):
    s = seqs[0].shape[1]
    nc = s // RW_CHUNK

    def body(*refs):
        seq_refs, par_refs = refs[:5], refs[5:10]
        z_ref, st_ref, state = refs[10:]
        c = pl.program_id(1)

        @pl.when(c == 0)
        def _():
            state[...] = jnp.zeros_like(state)

        s0 = state[...]
        st_ref[0] = s0
        z, s_end = _rwkv_chunk(s0, *[r[...] for r in seq_refs], *[r[...] for r in par_refs])
        z_ref[...] = z
        state[...] = s_end

    seq, par, st = _rwkv_specs(nc, False)
    return pl.pallas_call(
        body, name="rwkv_fwd", grid=(N_HEADS // RW_HB, nc),
        in_specs=[seq] * 5 + [par] * 5, out_specs=[seq, st],
        out_shape=[jax.ShapeDtypeStruct((N_HEADS, s, HEAD), F32), jax.ShapeDtypeStruct((nc, N_HEADS, HEAD, HEAD), F32)],
        scratch_shapes=[pltpu.VMEM((RW_HB, HEAD, HEAD), F32)],
        compiler_params=_params(("arbitrary", "arbitrary")),
    )(*seqs, *pars)


def _rwkv_bwd(seqs, pars, states, dz):
    s = seqs[0].shape[1]
    nc = s // RW_CHUNK

    def body(*refs):
        seq_refs, par_refs = refs[:5], refs[5:10]
        st_ref, dz_ref = refs[10:12]
        dseq_refs, dpar_refs, dstate = refs[12:17], refs[17:22], refs[22]
        c = pl.program_id(1)

        @pl.when(c == 0)
        def _():
            dstate[...] = jnp.zeros_like(dstate)
            for r in dpar_refs:
                r[...] = jnp.zeros_like(r)

        _, vjp = jax.vjp(_rwkv_chunk, st_ref[0], *[r[...] for r in seq_refs], *[r[...] for r in par_refs])
        g = vjp((dz_ref[...], dstate[...]))
        dstate[...] = g[0]
        for r, gs in zip(dseq_refs, g[1:6]):
            r[...] = gs
        for r, gp in zip(dpar_refs, g[6:]):
            r[...] += gp

    seq, par, st = _rwkv_specs(nc, True)
    seq_sh = jax.ShapeDtypeStruct((N_HEADS, s, HEAD), F32)
    par_sh = jax.ShapeDtypeStruct((N_HEADS, 1, HEAD), F32)
    outs = pl.pallas_call(
        body, name="rwkv_bwd", grid=(N_HEADS // RW_HB, nc),
        in_specs=[seq] * 5 + [par] * 5 + [st, seq],
        out_specs=[seq] * 5 + [par] * 5, out_shape=[seq_sh] * 5 + [par_sh] * 5,
        scratch_shapes=[pltpu.VMEM((RW_HB, HEAD, HEAD), F32)],
        compiler_params=_params(("arbitrary", "arbitrary")),
    )(*seqs, *pars, states, dz)
    return list(outs[:5]), list(outs[5:])


def _norm_fn(x, g):
    return (_rms(x, g),)


def _attn_prep_fn(proj, qn_w, kn_w):
    a, b = SB_HEADS, 3 * DIL_GROUP
    return (proj[0:a], proj[a:2 * a], proj[2 * a:3 * a],
            _rms(proj[3 * a:3 * a + b], qn_w), _rms(proj[3 * a + b:3 * a + 2 * b], kn_w), proj[3 * a + 2 * b:])


def _attn_merge_fn(o_sb, ol0, ol1, ol2):
    groups = (ol0, ol1, ol2)
    merged = []
    for j in range(DIL_GROUP):
        lses = [ol[j][:, HEAD:HEAD + 1] for ol in groups]
        m = functools.reduce(jnp.maximum, lses)
        es = [jnp.exp(l - m) for l in lses]
        inv = 1.0 / functools.reduce(jnp.add, es)
        merged.append(functools.reduce(jnp.add, [(e * inv) * ol[j][:, :HEAD] for e, ol in zip(es, groups)]))
    return (jnp.concatenate([_heads_to_nat(o_sb)] + merged, axis=-1),)


def _rw_mix_fn(x, xp, gn, mix, w0, w1, w2, a0, a1, a2, g1, g2):
    h = _rms(x, gn)
    xx = _rms(xp, gn) - h
    xr, xw, xk, xv, xa, xg = [h + xx * mix[i:i + 1] for i in range(6)]
    w_log = -jax.nn.softplus(-(w0 + _mm(jnp.tanh(_mm(xw, w1)), w2))) - 0.5
    lw = -jnp.exp(w_log)
    ag = jax.nn.sigmoid(a0 + _mm(_mm(xa, a1), a2))
    gate = _mm(jax.nn.sigmoid(_mm(xg, g1)), g2)
    return xr, xk, xv, _nat_to_heads(lw), _nat_to_heads(ag), gate


def _rw_gate_fn(z, gate):
    return (_heads_to_nat(z) * gate,)


def _adamw(name, w, m, v, gparts, row0=0, prev=None):
    big_r, c = w.shape
    r = gparts.shape[1]
    tr = r
    if r % 8 == 0:
        tr = max(t for t in range(8, r + 1, 8) if r % t == 0 and (t * c * 4 <= (1 << 20) or t == 8))
    assert row0 % tr == 0 and (r == big_r or r % 8 == 0)
    off = row0 // tr

    def body(w_ref, m_ref, v_ref, g_ref, *rest):
        go_ref, d_ref, mo_ref, vo_ref = rest[-4:]
        g = g_ref[0].astype(F32)
        for j in range(1, N_DEV):
            g = g + g_ref[j].astype(F32)
        mn = ADAM_B1 * m_ref[...] + (1.0 - ADAM_B1) * g
        vn = ADAM_B2 * v_ref[...] + (1.0 - ADAM_B2) * jnp.square(g)
        m_hat = mn / (1.0 - ADAM_B1 ** ADAM_STEP)
        v_hat = vn / (1.0 - ADAM_B2 ** ADAM_STEP)
        go_ref[...] = g
        d_ref[...] = -ADAM_LR * (m_hat / (jnp.sqrt(v_hat) + ADAM_EPS) + ADAM_WD * w_ref[...])
        mo_ref[...] = mn
        vo_ref[...] = vn

    tile = pl.BlockSpec((tr, c), lambda i: (i + off, 0))
    sh = jax.ShapeDtypeStruct((big_r, c), F32)
    prev = list(prev) if prev is not None else []
    return pl.pallas_call(
        body, name=name, grid=(r // tr,),
        in_specs=([tile, tile, tile, pl.BlockSpec((N_DEV, tr, c), lambda i: (0, i, 0))]
                  + [pl.BlockSpec(memory_space=pl.ANY)] * len(prev)),
        out_specs=[tile] * 4, out_shape=[sh] * 4,
        input_output_aliases={4 + j: j for j in range(len(prev))},
        compiler_params=_params(("arbitrary",)),
    )(w, m, v, gparts, *prev)


def _col_blocks_to_nat(g):
    return jnp.moveaxis(g, 0, 1).reshape(g.shape[1], -1)


def _nat_to_col_blocks(a):
    return jnp.moveaxis(a.reshape(a.shape[0], N_DEV, -1), 1, 0)


AG_GROUPS = ("f00", "att", "f01", "f10", "rw", "f11")
RS_GROUPS = ("f11", "rw", "f10", "f01", "f00", "att")
BF16_GRAD_GROUPS = ("att", "f00")
RW_SHARDED = ('rw_mix', 'rw_w0', 'rw_w1', 'rw_w2', 'rw_a0', 'rw_a1', 'rw_a2', 'rw_g1', 'rw_g2', 'rw_kk', 'rw_ka',
              'rw_wr', 'rw_wk', 'rw_wv', 'rw_wo', 'rw_lnx_g', 'rw_lnx_b')


def _step(x, target, rep, get, put):
    tied = lambda a, zero: a + zero[0, 0].astype(a.dtype)
    s, d = x.shape
    tf = min(512, s)
    tt = min(256, s)
    row = lambda a: a.reshape(1, -1)
    mix_norm = rep["mix_norm"]
    ffw = {(0, 0): get("f00", None)}
    ffn_norm = _col_blocks_to_nat(ffw[(0, 0)]["ffn_norm"].reshape(N_DEV, 4, -1))

    acts = {}

    def ffn(nm, xin, l, h):
        g = ffw[(l, h)]
        out, *acts[(l, h)] = _ffn_fwd(nm, xin, ffn_norm[2 * l + h][None], g["gate"], g["up"], g["down"], min(2 * tf, s))
        return out

    x1 = ffn("ffn00_fwd", x, 0, 0)
    att = get("att", x1)
    w_in = _col_blocks_to_nat(att["attn_w_in"])
    w_out = _col_blocks_to_nat(att["attn_w_out"])
    (h0,) = _tile_fwd("mixnorm0_fwd", _norm_fn, [(x1, "nat")], [mix_norm[0:1]], [((s, d), BF16, "nat")], tt)
    proj = _linear_fwd("attn_in_fwd", h0, w_in, tt, out_layout="hm")
    bias = _bias_tiles(rep["rel_bias"], s)
    prep_pars = [rep["attn_q_norm"], rep["attn_k_norm"]]
    sb_sh, dl_sh = (SB_HEADS, s, HEAD), (3 * DIL_GROUP, s, HEAD)
    sq, sk, sv, qn, kn, vd = _tile_fwd("attn_prep_fwd", _attn_prep_fn, [(proj, "hm")], prep_pars,
                                       [(sb_sh, BF16, "hm")] * 3 + [(dl_sh, F32, "hm")] * 3, tt // 2)
    o_sb = _sb_fwd(sq, sk, sv)
    ols = [_dil_group_fwd(g, qn, kn, vd, bias) for g in range(3)]
    merge_tiled = [(o_sb, "hm")] + [(ol, "hm") for ol in ols]
    (merged,) = _tile_fwd("merge_fwd", _attn_merge_fn, merge_tiled, [], [((s, 512), BF16, "nat")], tt)
    x2 = _linear_fwd("attn_out_fwd", merged, w_out, tt, residual=x1)
    ffw[(0, 1)] = get("f01", x2)
    x3 = ffn("ffn01_fwd", x2, 0, 1)
    ffw[(1, 0)] = get("f10", x3)
    x4 = ffn("ffn10_fwd", x3, 1, 0)
    rw = get("rw", x4)
    rw_mix = _col_blocks_to_nat(rw["rw_mix"])
    rw_w1, rw_a1, rw_g1 = (rw[k].reshape(d, -1) for k in ("rw_w1", "rw_a1", "rw_g1"))
    rw_w2, rw_a2, rw_g2 = (_col_blocks_to_nat(rw[k]) for k in ("rw_w2", "rw_a2", "rw_g2"))
    rw_w0, rw_a0 = row(rw["rw_w0"]), row(rw["rw_a0"])
    head_par = lambda a: a.reshape(N_HEADS, 1, HEAD)
    scan_pars = [head_par(rw["rw_kk"]), head_par(rw["rw_ka"]), head_par(rep["rw_rk"]),
                 head_par(rw["rw_lnx_g"]), head_par(rw["rw_lnx_b"])]
    w_rkv = [rw[k].reshape(d, d) for k in ("rw_wr", "rw_wk", "rw_wv")]
    w_o = rw["rw_wo"].reshape(d, d)
    x4p = jnp.pad(x4, ((1, 0), (0, 0)))[:-1]
    mix_tiled = [(x4, "nat"), (x4p, "nat")]
    mix_pars = [mix_norm[1:2], rw_mix, rw_w0, rw_w1, rw_w2, rw_a0, rw_a1, rw_a2, rw_g1, rw_g2]
    hm_sh = (N_HEADS, s, HEAD)
    xr, xk, xv, lw, ag, gate = _tile_fwd(
        "rw_mix_fwd", _rw_mix_fn, mix_tiled, mix_pars,
        [((s, d), BF16, "nat")] * 3 + [(hm_sh, F32, "hm")] * 2 + [((s, d), F32, "nat")], tt)
    r_h, k_h, v_h = [_linear_fwd("rw_%s_fwd" % nm, xi, wi, tt, out_layout="hm")
                     for nm, xi, wi in zip("rkv", (xr, xk, xv), w_rkv)]
    scan_seqs = [r_h, lw, k_h, v_h, ag]
    z, states = _rwkv_fwd(scan_seqs, scan_pars)
    (zg,) = _tile_fwd("rw_gate_fwd", _rw_gate_fn, [(z, "hm"), (gate, "nat")], [], [((s, d), BF16, "nat")], tt)
    x5 = _linear_fwd("rw_out_fwd", zg, w_o, tt, residual=x4)
    ffw[(1, 1)] = get("f11", x5)
    y = ffn("ffn11_fwd", x5, 1, 1)
    dy, loss = _loss_head(y, target, tf)

    G = {}
    dgn = {}

    def fb(nm, group, xin, dout, l, h, zero=None, extra=None):
        g = ffw[(l, h)]
        gn = ffn_norm[2 * l + h][None]
        dxin, dgn[(l, h)], dwg, dwu, dwd = _ffn_bwd(nm, xin, dout, gn if zero is None else tied(gn, zero),
                                                   g["gate"], g["up"], g["down"], *acts[(l, h)], tf)
        shard = {"gate": dwg, "up": dwu, "down": dwd}
        if extra is not None:
            shard.update(extra())
        return dxin, put(group, {}, shard)

    dx5, zero = fb("ffn11_bwd", "f11", x5, dy, 1, 1)
    dzg = _linear_dx("rw_out_dx", dx5, w_o, tt)
    G["rw_wo"] = _linear_dw("rw_out_dw", zg, dx5, tf, 512)
    (dz, dgate), _ = _tile_bwd("rw_gate_bwd", _rw_gate_fn, [(z, "hm"), (gate, "nat")], [], [(dzg, "nat")], tt, [True, True])
    (dr_h, dlw, dk_h, dv_h, dag), dscan = _rwkv_bwd(scan_seqs, [tied(scan_pars[0], zero)] + scan_pars[1:], states, dz)
    drkv = (dr_h, dk_h, dv_h)
    for k, gpar in zip(("rw_kk", "rw_ka", "rw_rk", "rw_lnx_g", "rw_lnx_b"), dscan):
        G[k] = gpar
    dxs = []
    for j, (nm, xi, wi) in enumerate(zip("rkv", (xr, xk, xv), w_rkv)):
        dxs.append(_linear_dx("rw_%s_dx" % nm, drkv[j], wi, tt, dy_layout="hm"))
        G["rw_w" + nm] = _linear_dw("rw_%s_dw" % nm, xi, drkv[j], tf, 512, dy_layout="hm")
    (dx4a, dx4p), dmix = _tile_bwd(
        "rw_mix_bwd", _rw_mix_fn, mix_tiled, mix_pars,
        [(dxs[0], "nat"), (dxs[1], "nat"), (dxs[2], "nat"), (dlw, "hm"), (dag, "hm"), (dgate, "nat")],
        tt, [True, True], adds=[dx5, None])
    d_mixn1 = dmix[0]
    for k, gpar in zip(("rw_mix", "rw_w0", "rw_w1", "rw_w2", "rw_a0", "rw_a1", "rw_a2", "rw_g1", "rw_g2"), dmix[1:]):
        G[k] = gpar
    dx4 = dx4a + jnp.pad(dx4p[1:], ((0, 1), (0, 0)))
    for k in ("rw_mix", "rw_w2", "rw_a2", "rw_g2"):
        G[k] = _nat_to_col_blocks(G[k])
    for k in ("rw_w1", "rw_a1", "rw_g1", "rw_wr", "rw_wk", "rw_wv", "rw_wo"):
        G[k] = G[k].reshape(N_DEV, d // N_DEV, -1)
    for k in ("rw_w0", "rw_a0", "rw_kk", "rw_ka", "rw_lnx_g", "rw_lnx_b"):
        G[k] = G[k].reshape(N_DEV, 1, d // N_DEV)
    zero = put("rw", {"rw_rk": G["rw_rk"].reshape(N_HEADS, HEAD)}, {k: G[k] for k in RW_SHARDED})
    dx3, zero = fb("ffn10_bwd", "f10", x3, dx4, 1, 0, zero)
    dx2, zero = fb("ffn01_bwd", "f01", x2, dx3, 0, 1, zero)
    dmerged = _linear_dx("attn_out_dx", dx2, tied(w_out, zero), tt)
    (do_sb, *dols), _ = _tile_bwd("merge_bwd", _attn_merge_fn, merge_tiled, [], [(dmerged, "nat")], tt, [True] * 4)
    dq_sb, dk_sb, dv_sb = _sb_bwd(sq, sk, sv, do_sb)
    dil_grads, dbias = None, []
    for g in range(3):
        *dil_grads, db = _dil_group_bwd(g, qn, kn, vd, bias, ols[g], dols[g], dil_grads)
        dbias.append(db)
    dqn, dkn, dvd = dil_grads
    dbias = jnp.concatenate(dbias, axis=1)
    (dproj,), (dqn_w, dkn_w) = _tile_bwd(
        "attn_prep_bwd", _attn_prep_fn, [(proj, "hm")], prep_pars,
        [(dq_sb, "hm"), (dk_sb, "hm"), (dv_sb, "hm"), (dqn, "hm"), (dkn, "hm"), (dvd, "hm")], tt // 2, [True])
    dh0 = _linear_dx("attn_in_dx", dproj, w_in, tt, dy_layout="hm")
    (dx1,), (d_mixn0,) = _tile_bwd("mixnorm0_bwd", _norm_fn, [(x1, "nat")], [mix_norm[0:1]], [(dh0, "nat")], tt,
                                   [True], adds=[dx2])
    order = [(0, 0), (0, 1), (1, 0), (1, 1)]
    norm_grads = lambda: {"ffn_norm": _nat_to_col_blocks(jnp.concatenate([dgn[o] for o in order], axis=0))}
    dx0, zero = fb("ffn00_bwd", "f00", x, dx1, 0, 0, extra=norm_grads)
    G["attn_w_out"] = _linear_dw("attn_out_dw", tied(merged, zero), dx2, tf, 512)
    G["attn_w_in"] = _linear_dw("attn_in_dw", tied(h0, zero), dproj, tf, 512, dy_layout="hm")
    rep_grads = {"mix_norm": jnp.concatenate([d_mixn0, d_mixn1], axis=0), "rel_bias": _bias_tiles_bwd(dbias, s),
                 "attn_q_norm": dqn_w, "attn_k_norm": dkn_w}
    zero = put("att", rep_grads, {k: _nat_to_col_blocks(G[k]) for k in ("attn_w_in", "attn_w_out")})
    return loss, dx0, zero


WEIGHTS = ['ffn_norm', 'ffn_w_gate', 'ffn_w_up', 'ffn_w_down', 'mix_norm', 'rel_bias', 'attn_w_in', 'attn_q_norm',
           'attn_k_norm', 'attn_w_out', 'rw_mix', 'rw_w0', 'rw_w1', 'rw_w2', 'rw_a0', 'rw_a1', 'rw_a2', 'rw_g1', 'rw_g2',
           'rw_kk', 'rw_ka', 'rw_rk', 'rw_wr', 'rw_wk', 'rw_wv', 'rw_wo', 'rw_lnx_g', 'rw_lnx_b']
REPLICATED = ('mix_norm', 'rel_bias', 'attn_q_norm', 'attn_k_norm', 'rw_rk')
BF16_WEIGHTS = ('ffn_w_gate', 'ffn_w_up', 'ffn_w_down', 'attn_w_in', 'attn_w_out', 'rw_wr', 'rw_wk', 'rw_wv', 'rw_wo')


def kernel(x, ffn_norm, ffn_w_gate, ffn_w_up, ffn_w_down, mix_norm, rel_bias, attn_w_in, attn_q_norm, attn_k_norm, attn_w_out, rw_mix, rw_w0, rw_w1, rw_w2, rw_a0, rw_a1, rw_a2, rw_g1, rw_g2, rw_kk, rw_ka, rw_rk, rw_wr, rw_wk, rw_wv, rw_wo, rw_lnx_g, rw_lnx_b, loss_target, m_ffn_norm, m_ffn_w_gate, m_ffn_w_up, m_ffn_w_down, m_mix_norm, m_rel_bias, m_attn_w_in, m_attn_q_norm, m_attn_k_norm, m_attn_w_out, m_rw_mix, m_rw_w0, m_rw_w1, m_rw_w2, m_rw_a0, m_rw_a1, m_rw_a2, m_rw_g1, m_rw_g2, m_rw_kk, m_rw_ka, m_rw_rk, m_rw_wr, m_rw_wk, m_rw_wv, m_rw_wo, m_rw_lnx_g, m_rw_lnx_b, v_ffn_norm, v_ffn_w_gate, v_ffn_w_up, v_ffn_w_down, v_mix_norm, v_rel_bias, v_attn_w_in, v_attn_q_norm, v_attn_k_norm, v_attn_w_out, v_rw_mix, v_rw_w0, v_rw_w1, v_rw_w2, v_rw_a0, v_rw_a1, v_rw_a2, v_rw_g1, v_rw_g2, v_rw_kk, v_rw_ka, v_rw_rk, v_rw_wr, v_rw_wk, v_rw_wv, v_rw_wo, v_rw_lnx_g, v_rw_lnx_b):
    args = locals()
    w = {k: args[k] for k in WEIGHTS}
    cast = lambda k, a: a.astype(BF16) if k in BF16_WEIGHTS else a

    sources = {}
    for l, h in ((0, 0), (0, 1), (1, 0), (1, 1)):
        sources["f%d%d" % (l, h)] = {"gate": cast("ffn_w_gate", ffn_w_gate[l, h]), "up": cast("ffn_w_up", ffn_w_up[l, h]),
                                     "down": cast("ffn_w_down", ffn_w_down[l, h])}
    sources["f00"]["ffn_norm"] = ffn_norm
    drop_lead = lambda a: a[0] if a.ndim == 3 else a
    sources["att"] = {k: cast(k, w[k][0]) for k in ("attn_w_in", "attn_w_out")}
    sources["rw"] = {k: cast(k, drop_lead(w[k])) for k in RW_SHARDED}
    ag, token = {}, None
    for group in AG_GROUPS:
        names, arrays = list(sources[group]), list(sources[group].values())
        if token is not None:
            arrays[0] = arrays[0] + token[0, 0].astype(arrays[0].dtype)
        ag[group] = (names, _exchange_start("ag_start_" + group, arrays, []))
        token = ag[group][1]["token"]
    last_ag_token = token

    def get(group, after):
        names, started = ag[group]
        gathered, _ = _exchange_wait("ag_wait_" + group, started, last_ag_token if after is None else after)
        return dict(zip(names, gathered))

    rs = {}

    def put(group, rep_grads, shard_grads):
        if group in BF16_GRAD_GROUPS:
            shard_grads = {k: v.astype(BF16) for k, v in shard_grads.items()}
        started = _exchange_start("rs_start_" + group, list(rep_grads.values()), list(shard_grads.values()))
        rs[group] = (list(rep_grads), list(shard_grads), started)
        return started["token"]

    loss, dx, last_zero = _step(x[0], loss_target[0], {k: w[k] for k in REPLICATED}, get, put)
    loss = lax.psum(loss, MESH_AXES)

    results = {}
    ffn_prev = {}

    def update(k, parts, row0=0, prev=None):
        c = w[k].shape[-1]
        as2d = lambda a: a.reshape(-1, c)
        return _adamw("adamw_%s_%d" % (k, row0), as2d(w[k]), as2d(args["m_" + k]), as2d(args["v_" + k]),
                      parts.reshape(N_DEV, -1, c), row0, prev)

    after = last_zero
    for group in RS_GROUPS:
        rep_names, shard_names, started = rs[group]
        rep_parts, shard_parts = _exchange_wait("rs_wait_" + group, started, after)
        for k, parts in list(zip(rep_names, rep_parts)) + list(zip(shard_names, shard_parts)):
            if k in ("gate", "up", "down"):
                full = "ffn_w_" + k
                piece = 2 * int(group[1]) + int(group[2])
                ffn_prev[full] = update(full, parts, piece * parts.shape[1], ffn_prev.get(full))
                results[full] = ffn_prev[full]
            else:
                results[k] = update(k, parts)
            after = results[k if k in results else "ffn_w_" + k][0]

    outs = [[results[k][j].reshape(w[k].shape) for k in WEIGHTS] for j in range(4)]
    return (loss, dx[None], *outs[0], *outs[1], *outs[2], *outs[3])
```

```python
import functools
import math

import numpy as np
import jax
import jax.numpy as jnp
from jax import lax
from jax.experimental import pallas as pl
from jax.experimental.pallas import tpu as pltpu

F32, BF16 = jnp.float32, jnp.bfloat16
HI = lax.Precision.HIGH

N_DEV = 8
D_MODEL = 1024
HEAD = 64
N_HEADS = 16
SB_HEADS = 4
DIL_GROUP = 4
DIL_PATTERNS = ((128, 1), (512, 4), (2048, 16))
QBLK = 128
N_BUCKETS = 32
MAX_DISTANCE = 2048
NORM_EPS = 1e-6
GN_EPS = 64e-5
NEG_INF = -1e30
RW_CHUNK = 64
RW_HB = 16
ADAM_LR, ADAM_B1, ADAM_B2, ADAM_EPS, ADAM_WD, ADAM_STEP = 0.001, 0.9, 0.999, 1e-08, 0.01, 10
MESH_AXES = ("x", "y", "c")
VMEM_LIMIT_BYTES = 56 * 1024 * 1024

NN2 = (((1,), (0,)), ((), ()))
NT2 = (((1,), (1,)), ((), ()))
TN2 = (((0,), (0,)), ((), ()))
NN3 = (((2,), (1,)), ((0,), (0,)))
NT3 = (((2,), (2,)), ((0,), (0,)))
TN3 = (((1,), (1,)), ((0,), (0,)))


def _dot(a, b, dims=NN2, prec=None):
    return lax.dot_general(a, b, dims, precision=prec, preferred_element_type=F32)


def _params(sem=None):
    return pltpu.CompilerParams(dimension_semantics=sem, vmem_limit_bytes=VMEM_LIMIT_BYTES)


@jax.custom_vjp
def _mm(x, w):
    return _dot(x.astype(BF16), w.astype(BF16))


def _mm_fwd(x, w):
    return _mm(x, w), (x, w)


def _mm_bwd(res, dy):
    x, w = res
    dyb = dy.astype(BF16)
    return (_dot(dyb, w.astype(BF16), NT2).astype(x.dtype), _dot(x.astype(BF16), dyb, TN2).astype(w.dtype))


_mm.defvjp(_mm_fwd, _mm_bwd)


def _rms(x, g):
    return x * lax.rsqrt(jnp.mean(x * x, axis=-1, keepdims=True) + NORM_EPS) * g


def _log_sigmoid(z):
    return jnp.minimum(z, 0.0) - jnp.log(1.0 + jnp.exp(-jnp.abs(z)))


def _heads_to_nat(v3):
    return jnp.concatenate([v3[h] for h in range(v3.shape[0])], axis=-1)


def _nat_to_heads(v2):
    return jnp.stack([v2[:, h * HEAD:(h + 1) * HEAD] for h in range(v2.shape[1] // HEAD)], axis=0)


def _exchange(name, gathers, scatters):
    n_g = len(gathers)
    arrays = list(gathers) + list(scatters)
    n = len(arrays)
    out_shape = [jax.ShapeDtypeStruct((N_DEV,) + a.shape, a.dtype) for a in gathers]
    out_shape += [jax.ShapeDtypeStruct(a.shape, a.dtype) for a in scatters]

    def body(*refs):
        ins, outs = refs[:n], refs[n:2 * n]
        send_sems, recv_sems, local_sems = refs[2 * n:]
        x, y, c = lax.axis_index("x"), lax.axis_index("y"), lax.axis_index("c")
        me = 4 * x + 2 * y + c

        def src(i, idx):
            return ins[i] if i < n_g else ins[i].at[idx]

        local = [pltpu.make_async_copy(src(i, me), outs[i].at[me], local_sems.at[i]) for i in range(n)]
        for cp in local:
            cp.start()
        remote = []
        for m in range(1, N_DEV):
            px, py, pc = x ^ ((m >> 2) & 1), y ^ ((m >> 1) & 1), c ^ (m & 1)
            peer = 4 * px + 2 * py + pc
            for i in range(n):
                cp = pltpu.make_async_remote_copy(
                    src_ref=src(i, peer), dst_ref=outs[i].at[me],
                    send_sem=send_sems.at[i, m - 1], recv_sem=recv_sems.at[i, m - 1],
                    device_id=(px, py, pc), device_id_type=pl.DeviceIdType.MESH)
                cp.start()
                arrival = pltpu.make_async_remote_copy(
                    src_ref=src(i, peer), dst_ref=outs[i].at[peer],
                    send_sem=send_sems.at[i, m - 1], recv_sem=recv_sems.at[i, m - 1],
                    device_id=(px, py, pc), device_id_type=pl.DeviceIdType.MESH)
                remote.append((cp, arrival))
        for cp, arrival in remote:
            cp.wait_send()
            arrival.wait_recv()
        for cp in local:
            cp.wait()

    hbm = pl.BlockSpec(memory_space=pltpu.HBM)
    outs = pl.pallas_call(
        body, name=name, out_shape=out_shape,
        in_specs=[hbm] * n, out_specs=[hbm] * n,
        scratch_shapes=[pltpu.SemaphoreType.DMA((n, N_DEV - 1)), pltpu.SemaphoreType.DMA((n, N_DEV - 1)),
                        pltpu.SemaphoreType.DMA((n,))],
    )(*arrays)
    return list(outs[:n_g]), list(outs[n_g:])


def _mesh_peers():
    x, y, c = lax.axis_index("x"), lax.axis_index("y"), lax.axis_index("c")
    peers = []
    for m in range(1, N_DEV):
        px, py, pc = x ^ ((m >> 2) & 1), y ^ ((m >> 1) & 1), c ^ (m & 1)
        peers.append((m, (px, py, pc), 4 * px + 2 * py + pc))
    return 4 * x + 2 * y + c, peers


_HBM_SPEC = pl.BlockSpec(memory_space=pltpu.HBM)
_SEM_SPEC = pl.BlockSpec(memory_space=pltpu.SEMAPHORE)
_DATAFLOW = pltpu.SideEffectType.DATAFLOW_SIDE_EFFECTING


def _exchange_start(name, gathers, scatters):
    n_g = len(gathers)
    arrays = list(gathers) + list(scatters)
    n = len(arrays)
    lands = ([lax.empty((N_DEV,) + a.shape, a.dtype) for a in gathers] + [lax.empty(a.shape, a.dtype) for a in scatters])

    def body(*refs):
        ins, land = refs[:n], refs[n:2 * n]
        send_sems, recv_sems, local_sems, token = refs[2 * n], refs[2 * n + 1], refs[2 * n + 2], refs[-1]
        me, peers = _mesh_peers()
        for m, dev, peer in peers:
            for i in range(n):
                k = i * (N_DEV - 1) + m - 1
                pltpu.make_async_remote_copy(
                    src_ref=ins[i] if i < n_g else ins[i].at[peer], dst_ref=land[i].at[me],
                    send_sem=send_sems.at[k], recv_sem=recv_sems.at[k],
                    device_id=dev, device_id_type=pl.DeviceIdType.MESH).start()
        for i in range(n):
            pltpu.make_async_copy(ins[i] if i < n_g else ins[i].at[me], land[i].at[me], local_sems.at[i]).start()
        token[...] = jnp.zeros_like(token)

    sem = pltpu.SemaphoreType.DMA((n * (N_DEV - 1),))
    outs = pl.pallas_call(
        body, name=name,
        out_shape=([sem, sem, pltpu.SemaphoreType.DMA((n,))] + [pltpu.HBM(a.shape, a.dtype) for a in arrays]
                   + [pltpu.HBM(l.shape, l.dtype) for l in lands] + [jax.ShapeDtypeStruct((8, 128), F32)]),
        in_specs=[_HBM_SPEC] * (2 * n),
        out_specs=[_SEM_SPEC] * 3 + [_HBM_SPEC] * (2 * n) + [pl.BlockSpec(memory_space=pltpu.VMEM)],
        input_output_aliases={i: i + 3 for i in range(2 * n)},
        compiler_params=pltpu.CompilerParams(has_side_effects=_DATAFLOW),
    )(*[pltpu.with_memory_space_constraint(a, pltpu.HBM) for a in arrays],
      *[pltpu.with_memory_space_constraint(l, pltpu.HBM) for l in lands])
    return dict(n_g=n_g, n=n, send=outs[0], recv=outs[1], local=outs[2], srcs=list(outs[3:3 + n]),
                lands=list(outs[3 + n:3 + 2 * n]), token=outs[-1])


def _exchange_wait(name, started, after):
    n, n_g = started["n"], started["n_g"]

    def body(*refs):
        srcs, lands = refs[:n], refs[n:2 * n]
        send_sems, recv_sems, local_sems = refs[2 * n], refs[2 * n + 1], refs[2 * n + 2]
        me, peers = _mesh_peers()
        local = [pltpu.make_async_copy(srcs[i] if i < n_g else srcs[i].at[me], lands[i].at[me], local_sems.at[i])
                 for i in range(n)]
        for m, dev, peer in peers:
            for i in range(n):
                k = i * (N_DEV - 1) + m - 1
                cp = pltpu.make_async_remote_copy(
                    src_ref=srcs[i] if i < n_g else srcs[i].at[peer], dst_ref=lands[i].at[peer],
                    send_sem=send_sems.at[k], recv_sem=recv_sems.at[k],
                    device_id=dev, device_id_type=pl.DeviceIdType.MESH)
                cp.wait_send()
                cp.wait_recv()
        for cp in local:
            cp.wait()

    outs = pl.pallas_call(
        body, name=name,
        out_shape=([pltpu.HBM(a.shape, a.dtype) for a in started["srcs"]]
                   + [pltpu.HBM(l.shape, l.dtype) for l in started["lands"]]),
        in_specs=[_HBM_SPEC] * (2 * n) + [_SEM_SPEC] * 3 + [pl.BlockSpec(memory_space=pl.ANY)],
        out_specs=[_HBM_SPEC] * (2 * n), input_output_aliases={i: i for i in range(2 * n)},
        compiler_params=pltpu.CompilerParams(has_side_effects=_DATAFLOW),
    )(*started["srcs"], *started["lands"], started["send"], started["recv"], started["local"], after)
    return list(outs[n:n + n_g]), list(outs[n + n_g:])


def _tile_spec(shape, layout, t):
    if layout == "nat":
        return pl.BlockSpec((t, shape[1]), lambda i: (i, 0))
    return pl.BlockSpec((shape[0], t, shape[2]), lambda i: (0, i, 0))


def _full_spec(shape):
    nd = len(shape)
    return pl.BlockSpec(tuple(shape), lambda i: (0,) * nd)


def _seq_len(a, layout):
    return a.shape[0] if layout == "nat" else a.shape[1]


def _tile_fwd(name, f, tiled, params, outs, t):
    nt, npar = len(tiled), len(params)
    s = _seq_len(*tiled[0])

    def body(*refs):
        vals = [r[...] for r in refs[:nt + npar]]
        res = f(*vals)
        for r, o in zip(refs[nt + npar:], res):
            r[...] = o.astype(r.dtype)

    return pl.pallas_call(
        body, name=name, grid=(s // t,),
        in_specs=[_tile_spec(a.shape, l, t) for a, l in tiled] + [_full_spec(p.shape) for p in params],
        out_specs=[_tile_spec(sh, l, t) for sh, _, l in outs],
        out_shape=[jax.ShapeDtypeStruct(sh, dt) for sh, dt, _ in outs],
        compiler_params=_params(("arbitrary",)),
    )(*[a for a, _ in tiled], *params)


def _tile_bwd(name, f, tiled, params, cts, t, need, adds=None):
    nt, npar, nc = len(tiled), len(params), len(cts)
    s = _seq_len(*tiled[0])
    need_idx = [k for k in range(nt) if need[k]]
    adds = adds or [None] * len(need_idx)
    add_arrays = [(a, tiled[k][1]) for a, k in zip(adds, need_idx) if a is not None]
    n_add = len(add_arrays)

    def body(*refs):
        i = pl.program_id(0)
        vals = [r[...] for r in refs[:nt + npar]]
        ct_refs = refs[nt + npar:nt + npar + nc]
        add_refs = refs[nt + npar + nc:nt + npar + nc + n_add]
        out_refs = refs[nt + npar + nc + n_add:]
        res, vjp = jax.vjp(f, *vals)
        grads = vjp(tuple(r[...].astype(o.dtype) for r, o in zip(ct_refs, res)))
        a = 0
        for j, k in enumerate(need_idx):
            g = grads[k]
            if adds[j] is not None:
                g = g + add_refs[a][...]
                a += 1
            out_refs[j][...] = g.astype(out_refs[j].dtype)
        for j in range(npar):
            r = out_refs[len(need_idx) + j]

            @pl.when(i == 0)
            def _():
                r[...] = jnp.zeros_like(r)

            r[...] += grads[nt + j]

    outs = pl.pallas_call(
        body, name=name, grid=(s // t,),
        in_specs=([_tile_spec(a.shape, l, t) for a, l in tiled] + [_full_spec(p.shape) for p in params]
                  + [_tile_spec(a.shape, l, t) for a, l in cts] + [_tile_spec(a.shape, l, t) for a, l in add_arrays]),
        out_specs=([_tile_spec(tiled[k][0].shape, tiled[k][1], t) for k in need_idx]
                   + [_full_spec(p.shape) for p in params]),
        out_shape=([jax.ShapeDtypeStruct(tiled[k][0].shape, F32) for k in need_idx]
                   + [jax.ShapeDtypeStruct(p.shape, F32) for p in params]),
        compiler_params=_params(("arbitrary",)),
    )(*[a for a, _ in tiled], *params, *[a for a, _ in cts], *[a for a, _ in add_arrays])
    return list(outs[:len(need_idx)]), list(outs[len(need_idx):])


def _linear_fwd(name, x, w, t, out_layout="nat", residual=None):
    s, k = x.shape
    n = w.shape[1]
    has_res = residual is not None

    def body(*refs):
        x_ref, w_ref = refs[0], refs[1]
        o_ref = refs[-1]
        y = _dot(x_ref[...].astype(BF16), w_ref[...])
        if has_res:
            y = y + refs[2][...]
        if out_layout == "hm":
            for h in range(n // HEAD):
                o_ref[h] = y[:, h * HEAD:(h + 1) * HEAD]
        else:
            o_ref[...] = y

    out_sh = (s, n) if out_layout == "nat" else (n // HEAD, s, HEAD)
    ins = [x, w] + ([residual] if has_res else [])
    in_specs = [_tile_spec(x.shape, "nat", t), _full_spec(w.shape)] + ([_tile_spec((s, n), "nat", t)] if has_res else [])
    return pl.pallas_call(
        body, name=name, grid=(s // t,), in_specs=in_specs,
        out_specs=_tile_spec(out_sh, out_layout, t), out_shape=jax.ShapeDtypeStruct(out_sh, F32),
        compiler_params=_params(("arbitrary",)),
    )(*ins)


def _linear_dx(name, dy, w, t, dy_layout="nat"):
    k, n = w.shape
    s = _seq_len(dy, dy_layout)

    def body(dy_ref, w_ref, o_ref):
        dy = _heads_to_nat(dy_ref[...].astype(BF16)) if dy_layout == "hm" else dy_ref[...].astype(BF16)
        o_ref[...] = _dot(dy, w_ref[...], NT2)

    return pl.pallas_call(
        body, name=name, grid=(s // t,),
        in_specs=[_tile_spec(dy.shape, dy_layout, t), _full_spec(w.shape)],
        out_specs=_tile_spec((s, k), "nat", t), out_shape=jax.ShapeDtypeStruct((s, k), F32),
        compiler_params=_params(("arbitrary",)),
    )(dy, w)


def _linear_dw(name, x, dy, t, nb, dy_layout="nat"):
    s, k = x.shape
    n = dy.shape[1] if dy_layout == "nat" else dy.shape[0] * HEAD

    def body(x_ref, dy_ref, o_ref):
        i = pl.program_id(1)

        @pl.when(i == 0)
        def _():
            o_ref[...] = jnp.zeros_like(o_ref)

        dy = _heads_to_nat(dy_ref[...].astype(BF16)) if dy_layout == "hm" else dy_ref[...].astype(BF16)
        o_ref[...] += _dot(x_ref[...].astype(BF16), dy, TN2)

    if dy_layout == "hm":
        dy_spec = pl.BlockSpec((nb // HEAD, t, HEAD), lambda j, i: (j, i, 0))
    else:
        dy_spec = pl.BlockSpec((t, nb), lambda j, i: (i, j))
    return pl.pallas_call(
        body, name=name, grid=(n // nb, s // t),
        in_specs=[pl.BlockSpec((t, k), lambda j, i: (i, 0)), dy_spec],
        out_specs=pl.BlockSpec((k, nb), lambda j, i: (0, j)), out_shape=jax.ShapeDtypeStruct((k, n), F32),
        compiler_params=_params(("arbitrary", "arbitrary")),
    )(x, dy)


def _ffn_fwd(name, x, gn, wg, wu, wd, t):
    s, d = x.shape
    f8 = wg.shape[-1]

    def body(x_ref, g_ref, wg_ref, wu_ref, wd_ref, o_ref, gk_ref, uk_ref, h_scr, acc):
        k = pl.program_id(1)

        @pl.when(k == 0)
        def _():
            h_scr[...] = _rms(x_ref[...], g_ref[...]).astype(BF16)
            acc[...] = jnp.zeros_like(acc)

        hb = h_scr[...]
        gk = _dot(hb, wg_ref[0])
        uk = _dot(hb, wu_ref[0])
        gk_ref[0] = gk
        uk_ref[0] = uk
        a = gk * jax.nn.sigmoid(gk) * uk
        acc[...] += _dot(a.astype(BF16), wd_ref[0])

        @pl.when(k == N_DEV - 1)
        def _():
            o_ref[...] = x_ref[...] + 0.5 * acc[...]

    wspec = lambda shp: pl.BlockSpec((1,) + shp, lambda i, k: (k, 0, 0))
    act = pl.BlockSpec((1, t, f8), lambda i, k: (k, i, 0))
    act_sh = jax.ShapeDtypeStruct((N_DEV, s, f8), F32)
    return pl.pallas_call(
        body, name=name, grid=(s // t, N_DEV),
        in_specs=[pl.BlockSpec((t, d), lambda i, k: (i, 0)), pl.BlockSpec((1, d), lambda i, k: (0, 0)),
                  wspec((d, f8)), wspec((d, f8)), wspec((f8, d))],
        out_specs=[pl.BlockSpec((t, d), lambda i, k: (i, 0)), act, act],
        out_shape=[jax.ShapeDtypeStruct((s, d), F32), act_sh, act_sh],
        scratch_shapes=[pltpu.VMEM((t, d), BF16), pltpu.VMEM((t, d), F32)],
        compiler_params=_params(("arbitrary", "arbitrary")),
    )(x, gn, wg, wu, wd)


def _ffn_bwd(name, x, dy, gn, wg, wu, wd, gact, uact, t):
    s, d = x.shape
    f8 = wg.shape[-1]
    last = N_DEV - 1

    def body(x_ref, dy_ref, g_ref, wg_ref, wu_ref, wd_ref, gk_ref, uk_ref,
             dx_ref, dg_ref, dwg_ref, dwu_ref, dwd_ref, dh_scr):
        k, i = pl.program_id(0), pl.program_id(1)
        x = x_ref[...]
        rs = lax.rsqrt(jnp.mean(x * x, axis=-1, keepdims=True) + NORM_EPS)
        xn = x * rs
        hb = (xn * g_ref[...]).astype(BF16)
        dob = (0.5 * dy_ref[...]).astype(BF16)
        wgk, wuk, wdk = wg_ref[0], wu_ref[0], wd_ref[0]
        gk, uk = gk_ref[0], uk_ref[0]
        sg = jax.nn.sigmoid(gk)
        sk = gk * sg
        da = _dot(dob, wdk, NT2)
        du = (da * sk).astype(BF16)
        dg = (da * uk * (sg * (1.0 + gk * (1.0 - sg)))).astype(BF16)

        @pl.when(i == 0)
        def _():
            dwg_ref[...] = jnp.zeros_like(dwg_ref)
            dwu_ref[...] = jnp.zeros_like(dwu_ref)
            dwd_ref[...] = jnp.zeros_like(dwd_ref)

        dwd_ref[0] += _dot((sk * uk).astype(BF16), dob, TN2)
        dwg_ref[0] += _dot(hb, dg, TN2)
        dwu_ref[0] += _dot(hb, du, TN2)
        dh = _dot(dg, wgk, NT2) + _dot(du, wuk, NT2)
        rows = pl.ds(pl.multiple_of(i * t, t), t)

        @pl.when(k == 0)
        def _():
            dh_scr[rows, :] = dh

        @pl.when(k > 0)
        def _():
            dh_scr[rows, :] += dh

        @pl.when(jnp.logical_and(k == last, i == 0))
        def _():
            dg_ref[...] = jnp.zeros_like(dg_ref)

        @pl.when(k == last)
        def _():
            dht = dh_scr[rows, :]
            dg_ref[...] += jnp.sum(dht * xn, axis=0, keepdims=True)
            dxn = dht * g_ref[...]
            dx_ref[...] = dy_ref[...] + rs * (dxn - xn * jnp.mean(dxn * xn, axis=-1, keepdims=True))

    wspec = lambda shp: pl.BlockSpec((1,) + shp, lambda k, i: (k, 0, 0))
    tile = pl.BlockSpec((t, d), lambda k, i: (i, 0))
    act = pl.BlockSpec((1, t, f8), lambda k, i: (k, i, 0))
    return pl.pallas_call(
        body, name=name, grid=(N_DEV, s // t),
        in_specs=[tile, tile, pl.BlockSpec((1, d), lambda k, i: (0, 0)), wspec((d, f8)), wspec((d, f8)), wspec((f8, d)),
                  act, act],
        out_specs=[pl.BlockSpec((t, d), lambda k, i: (jnp.where(k == last, i, 0), 0)),
                   pl.BlockSpec((1, d), lambda k, i: (0, 0)),
                   pl.BlockSpec((1, d, f8), lambda k, i: (k, 0, 0)), pl.BlockSpec((1, d, f8), lambda k, i: (k, 0, 0)),
                   pl.BlockSpec((1, f8, d), lambda k, i: (k, 0, 0))],
        out_shape=[jax.ShapeDtypeStruct((s, d), F32), jax.ShapeDtypeStruct((1, d), F32),
                   jax.ShapeDtypeStruct((N_DEV, d, f8), F32), jax.ShapeDtypeStruct((N_DEV, d, f8), F32),
                   jax.ShapeDtypeStruct((N_DEV, f8, d), F32)],
        scratch_shapes=[pltpu.VMEM((s, d), F32)],
        compiler_params=_params(("arbitrary", "arbitrary")),
    )(x, dy, gn, wg, wu, wd, gact, uact)


def _loss_head(y, target, t):
    s, d = y.shape

    def body(y_ref, t_ref, dy_ref, l_ref):
        i = pl.program_id(0)
        err = y_ref[...] - t_ref[...]
        dy_ref[...] = err * (1.0 / d)

        @pl.when(i == 0)
        def _():
            l_ref[...] = jnp.zeros_like(l_ref)

        l_ref[...] += 0.5 * jnp.sum(jnp.mean(err * err, axis=-1, keepdims=True), axis=0, keepdims=True)

    tile = pl.BlockSpec((t, d), lambda i: (i, 0))
    dy, l = pl.pallas_call(
        body, name="loss_head", grid=(s // t,), in_specs=[tile, tile],
        out_specs=[tile, pl.BlockSpec((1, 1), lambda i: (0, 0))],
        out_shape=[jax.ShapeDtypeStruct((s, d), F32), jax.ShapeDtypeStruct((1, 1), F32)],
        compiler_params=_params(("arbitrary",)),
    )(y, target)
    return dy, l[0, 0]


SB_KEY_TILE = 512
SB_HEADS_PER_STEP = 4


def _sb_scan_mats():
    row = lax.broadcasted_iota(jnp.int32, (QBLK, QBLK), 0)
    col = lax.broadcasted_iota(jnp.int32, (QBLK, QBLK), 1)
    return (row > col).astype(F32).astype(BF16), (row < col).astype(F32).astype(BF16)


def _sb_tile_scan(x, mat, reverse):
    nsub = x.shape[1] // QBLK
    outs, carry = [None] * nsub, jnp.zeros((x.shape[0], 1), F32)
    for i in (reversed(range(nsub)) if reverse else range(nsub)):
        xs = x[:, i * QBLK:(i + 1) * QBLK]
        hi = xs.astype(BF16)
        lo = (xs - hi.astype(F32)).astype(BF16)
        outs[i] = _dot(hi, mat) + _dot(lo, mat) + carry
        carry = carry + jnp.sum(xs, axis=1, keepdims=True)
    return jnp.concatenate(outs, axis=1), carry


def _sb_before_query(n, t, kt):
    row = lax.broadcasted_iota(jnp.int32, (QBLK, kt), 0)
    col = lax.broadcasted_iota(jnp.int32, (QBLK, kt), 1)
    return t * kt + col < n * QBLK + row


def _sb_fwd(q, k, v):
    _, s, _ = q.shape
    scale = HEAD ** -0.5
    kt = min(SB_KEY_TILE, s)

    def body(q_ref, k_ref, v_ref, o_ref):
        n = pl.program_id(1)
        suffix, _ = _sb_scan_mats()
        n_tiles = lax.div(n, jnp.int32(kt // QBLK)) + 1
        heads = range(SB_HEADS_PER_STEP)
        qb = [(q_ref[h] * scale).astype(q_ref.dtype) for h in heads]

        def tile(t, carry, diagonal):
            rows = pl.ds(pl.multiple_of(t * kt, kt), kt)
            out = []
            for h in heads:
                c, acc = carry[h]
                z = _dot(qb[h], k_ref[h, rows, :], NT2)
                lk = _log_sigmoid(-z)
                log_beta = z + lk
                if diagonal:
                    ok = _sb_before_query(n, t, kt)
                    lk = jnp.where(ok, lk, 0.0)
                later, total = _sb_tile_scan(lk, suffix, True)
                w = jnp.exp(log_beta + later + c)
                if diagonal:
                    w = jnp.where(ok, w, 0.0)
                out.append((c + total, acc + _dot(w.astype(BF16), v_ref[h, rows, :])))
            return tuple(out)

        zero = (jnp.zeros((QBLK, 1), F32), jnp.zeros((QBLK, HEAD), F32))
        carry = tile(n_tiles - 1, (zero,) * len(heads), True)
        carry = lax.fori_loop(1, n_tiles, lambda jj, cr: tile(n_tiles - 1 - jj, cr, False), carry)
        for h in heads:
            o_ref[h] = carry[h][1]

    hp = SB_HEADS_PER_STEP
    return pl.pallas_call(
        body, name="sb_fwd", grid=(SB_HEADS // hp, s // QBLK),
        in_specs=[pl.BlockSpec((hp, QBLK, HEAD), lambda h, n: (h, n, 0)),
                  pl.BlockSpec((hp, s, HEAD), lambda h, n: (h, 0, 0)),
                  pl.BlockSpec((hp, s, HEAD), lambda h, n: (h, 0, 0))],
        out_specs=pl.BlockSpec((hp, QBLK, HEAD), lambda h, n: (h, n, 0)),
        out_shape=jax.ShapeDtypeStruct((SB_HEADS, s, HEAD), F32),
        compiler_params=_params(("arbitrary", "arbitrary")),
    )(q, k, v)


def _sb_bwd(q, k, v, do):
    _, s, _ = q.shape
    scale = HEAD ** -0.5
    kt = min(SB_KEY_TILE, s)

    def body(q_ref, k_ref, v_ref, do_ref, dq_ref, dk_ref, dv_ref, e_scr, beta_scr):
        n = pl.program_id(1)

        @pl.when(n == 0)
        def _():
            dk_ref[...] = jnp.zeros_like(dk_ref)
            dv_ref[...] = jnp.zeros_like(dv_ref)

        suffix, prefix = _sb_scan_mats()
        n_tiles = lax.div(n, jnp.int32(kt // QBLK)) + 1
        heads = range(SB_HEADS_PER_STEP)
        qb = [(q_ref[h] * scale).astype(q_ref.dtype) for h in heads]
        dob = [do_ref[h].astype(BF16) for h in heads]

        def weights(t, cs, diagonal):
            rows = pl.ds(pl.multiple_of(t * kt, kt), kt)
            out = []
            for h in heads:
                vb = v_ref[h, rows, :]
                z = _dot(qb[h], k_ref[h, rows, :], NT2)
                lk = _log_sigmoid(-z)
                log_beta = z + lk
                if diagonal:
                    ok = _sb_before_query(n, t, kt)
                    lk = jnp.where(ok, lk, 0.0)
                later, total = _sb_tile_scan(lk, suffix, True)
                w = jnp.exp(log_beta + later + cs[h])
                if diagonal:
                    w = jnp.where(ok, w, 0.0)
                e_scr[h, t] = w * _dot(dob[h], vb, NT2)
                beta_scr[h, t] = jnp.exp(log_beta)
                dv_ref[h, rows, :] += _dot(w.astype(BF16), dob[h], TN2)
                out.append(cs[h] + total)
            return tuple(out)

        col0 = jnp.zeros((QBLK, 1), F32)
        cs = weights(n_tiles - 1, (col0,) * len(heads), True)
        lax.fori_loop(1, n_tiles, lambda jj, c: weights(n_tiles - 1 - jj, c, False), cs)

        def grads(t, carry, diagonal):
            rows = pl.ds(pl.multiple_of(t * kt, kt), kt)
            out = []
            for h in heads:
                pc, dq = carry[h]
                kb = k_ref[h, rows, :]
                e, beta = e_scr[h, t], beta_scr[h, t]
                before, total = _sb_tile_scan(e, prefix, False)
                dz = e * (1.0 - beta) - beta * (before + pc)
                if diagonal:
                    dz = jnp.where(_sb_before_query(n, t, kt), dz, 0.0)
                dz = dz.astype(BF16)
                dk_ref[h, rows, :] += _dot(dz, qb[h], TN2)
                out.append((pc + total, dq + _dot(dz, kb)))
            return tuple(out)

        zero = (col0, jnp.zeros((QBLK, HEAD), F32))
        carry = lax.fori_loop(0, n_tiles - 1, lambda t, cr: grads(t, cr, False), (zero,) * len(heads))
        carry = grads(n_tiles - 1, carry, True)
        for h in heads:
            dq_ref[h] = carry[h][1] * scale

    hp = SB_HEADS_PER_STEP
    qspec = pl.BlockSpec((hp, QBLK, HEAD), lambda h, n: (h, n, 0))
    full = pl.BlockSpec((hp, s, HEAD), lambda h, n: (h, 0, 0))
    sh = jax.ShapeDtypeStruct((SB_HEADS, s, HEAD), F32)
    tiles_sh = (hp, s // kt, QBLK, kt)
    return pl.pallas_call(
        body, name="sb_bwd", grid=(SB_HEADS // hp, s // QBLK),
        in_specs=[qspec, full, full, qspec],
        out_specs=[qspec, full, full], out_shape=[sh, sh, sh],
        scratch_shapes=[pltpu.VMEM(tiles_sh, F32), pltpu.VMEM(tiles_sh, F32)],
        compiler_params=_params(("arbitrary", "arbitrary")),
    )(q, k, v, do)


def _t5_bucket_np(dist):
    max_exact = N_BUCKETS // 2
    d = np.maximum(dist, 1).astype(np.float32)
    large = max_exact + (np.log(d / np.float32(max_exact)) / np.float32(math.log(MAX_DISTANCE / max_exact))
                         * np.float32(N_BUCKETS - max_exact)).astype(np.int32)
    large = np.minimum(large, N_BUCKETS - 1)
    return np.where(dist < max_exact, dist, large)


def _dil_layout(s):
    assert all(s % (QBLK * r) == 0 and window // r == QBLK for window, r in DIL_PATTERNS)
    tiles, buckets = [], []
    i = np.arange(QBLK)[:, None]
    j = np.arange(QBLK)[None, :]
    for g, (window, r) in enumerate(DIL_PATTERNS):
        for off in (0, 1):
            dist = QBLK * off + i - j
            ok = (dist >= 0) & (dist <= window // r)
            tiles.append((g, off))
            buckets.append(np.where(ok, _t5_bucket_np(np.maximum(dist, 0) * r), -1).astype(np.int32))
    return tiles, np.stack(buckets)


def _bias_tiles(rel_bias, s):
    tiles, buckets = _dil_layout(s)
    nt = len(tiles)
    present = [sorted(set(np.unique(buckets[k]).tolist()) - {-1}) for k in range(nt)]

    def body(rel_ref, b_ref, o_ref):
        j = pl.program_id(0)
        for k, (g, _) in enumerate(tiles):
            bk = b_ref[k]
            tile = jnp.full((QBLK, QBLK), NEG_INF, F32)
            for b in present[k]:
                tile = jnp.where(bk == b, rel_ref[b, g * DIL_GROUP + j], tile)
            o_ref[0, k] = tile

    return pl.pallas_call(
        body, name="bias_tiles", grid=(DIL_GROUP,),
        in_specs=[pl.BlockSpec(memory_space=pltpu.SMEM), pl.BlockSpec((nt, QBLK, QBLK), lambda j: (0, 0, 0))],
        out_specs=pl.BlockSpec((1, nt, QBLK, QBLK), lambda j: (j, 0, 0, 0)),
        out_shape=jax.ShapeDtypeStruct((DIL_GROUP, nt, QBLK, QBLK), F32),
        compiler_params=_params(("arbitrary",)),
    )(rel_bias, jnp.asarray(buckets))


def _bias_tiles_bwd(dbias, s):
    tiles, buckets = _dil_layout(s)
    nt = len(tiles)
    present = [sorted(set(np.unique(buckets[k]).tolist()) - {-1}) for k in range(nt)]

    def body(d_ref, b_ref, o_ref):
        j = pl.program_id(0)

        @pl.when(j == 0)
        def _():
            for b in range(N_BUCKETS):
                for col in range(3 * DIL_GROUP):
                    o_ref[b, col] = jnp.float32(0.0)

        for k, (g, _) in enumerate(tiles):
            bk, dk = b_ref[k], d_ref[0, k]
            for b in present[k]:
                o_ref[b, g * DIL_GROUP + j] += jnp.sum(jnp.where(bk == b, dk, 0.0))

    return pl.pallas_call(
        body, name="bias_tiles_bwd", grid=(DIL_GROUP,),
        in_specs=[pl.BlockSpec((1, nt, QBLK, QBLK), lambda j: (j, 0, 0, 0)),
                  pl.BlockSpec((nt, QBLK, QBLK), lambda j: (0, 0, 0))],
        out_specs=pl.BlockSpec(memory_space=pltpu.SMEM),
        out_shape=jax.ShapeDtypeStruct((N_BUCKETS, 3 * DIL_GROUP), F32),
        compiler_params=_params(("arbitrary",)),
    )(dbias, jnp.asarray(buckets))


DIL_PAIRS_PER_STEP = 4


def _dil_rows(g, s, pair):
    _, r = DIL_PATTERNS[g]
    nb = s // (QBLK * r)
    c, n = lax.div(pair, jnp.int32(nb)), lax.rem(pair, jnp.int32(nb))
    start = c + (r * QBLK) * n
    before = jnp.where(n > 0, start - r * QBLK, start)
    if r == 1:
        return pl.ds(start, QBLK), pl.ds(before, QBLK), n > 0
    return pl.ds(start, QBLK, stride=r), pl.ds(before, QBLK, stride=r), n > 0


def _dil_logits(qb, k_ref, rows, before, has_before, b_ref):
    k0, k1 = k_ref[0, rows, :].astype(BF16), k_ref[0, before, :].astype(BF16)
    l0 = _dot(qb, k0, NT2) + b_ref[0, 0]
    l1 = jnp.where(has_before, _dot(qb, k1, NT2) + b_ref[0, 1], NEG_INF)
    return k0, k1, l0, l1


def _dil_group_specs(g, s):
    head = pl.BlockSpec((1, s, HEAD), lambda j, p: (DIL_GROUP * g + j, 0, 0), pipeline_mode=pl.Buffered(1))
    return [head, head, head, pl.BlockSpec((1, 2, QBLK, QBLK), lambda j, p: (j, g, 0, 0))]


def _dil_group_fwd(g, qn, kn, v, bias):
    _, s, _ = qn.shape
    scale = HEAD ** -0.5
    steps = (s // QBLK) // DIL_PAIRS_PER_STEP

    def body(q_ref, k_ref, v_ref, b_ref, o_ref):
        for u in range(DIL_PAIRS_PER_STEP):
            rows, before, has_before = _dil_rows(g, s, pl.program_id(1) * DIL_PAIRS_PER_STEP + u)
            qb = (q_ref[0, rows, :] * scale).astype(BF16)
            _, _, l0, l1 = _dil_logits(qb, k_ref, rows, before, has_before, b_ref)
            m = jnp.max(jnp.maximum(l0, l1), axis=1, keepdims=True)
            p0, p1 = jnp.exp(l0 - m), jnp.exp(l1 - m)
            den = jnp.sum(p0 + p1, axis=1, keepdims=True)
            inv = 1.0 / den
            o = (_dot((p0 * inv).astype(BF16), v_ref[0, rows, :].astype(BF16))
                 + _dot((p1 * inv).astype(BF16), v_ref[0, before, :].astype(BF16)))
            lse = jnp.broadcast_to(m + jnp.log(den), (QBLK, HEAD))
            o_ref[0, rows, :] = jnp.concatenate([o, lse], axis=1)

    return pl.pallas_call(
        body, name="dil%d_fwd" % g, grid=(DIL_GROUP, steps), in_specs=_dil_group_specs(g, s),
        out_specs=pl.BlockSpec((1, s, 2 * HEAD), lambda j, p: (j, 0, 0)),
        out_shape=jax.ShapeDtypeStruct((DIL_GROUP, s, 2 * HEAD), F32),
        compiler_params=_params(("arbitrary", "arbitrary")),
    )(qn, kn, v, bias)


def _dil_group_bwd(g, qn, kn, v, bias, ol, dol, prev):
    _, s, _ = qn.shape
    scale = HEAD ** -0.5
    steps = (s // QBLK) // DIL_PAIRS_PER_STEP
    prev = list(prev) if prev is not None else []

    def body(q_ref, k_ref, v_ref, b_ref, ol_ref, dol_ref, *rest):
        dq_ref, dk_ref, dv_ref, db_ref = rest[-4:]

        @pl.when(pl.program_id(1) == 0)
        def _():
            for r in (dk_ref, dv_ref, db_ref):
                r[...] = jnp.zeros_like(r)

        for u in range(DIL_PAIRS_PER_STEP):
            rows, before, has_before = _dil_rows(g, s, pl.program_id(1) * DIL_PAIRS_PER_STEP + u)
            qb = (q_ref[0, rows, :] * scale).astype(BF16)
            k0, k1, l0, l1 = _dil_logits(qb, k_ref, rows, before, has_before, b_ref)
            v0, v1 = v_ref[0, rows, :].astype(BF16), v_ref[0, before, :].astype(BF16)
            out_lse, d_out_lse = ol_ref[0, rows, :], dol_ref[0, rows, :]
            o, lse = out_lse[:, :HEAD], out_lse[:, HEAD:HEAD + 1]
            do, dlse = d_out_lse[:, :HEAD], d_out_lse[:, HEAD:HEAD + 1]
            dob = do.astype(BF16)
            p0, p1 = jnp.exp(l0 - lse), jnp.exp(l1 - lse)
            shift = dlse - jnp.sum(do * o, axis=1, keepdims=True)
            dl0 = p0 * (_dot(dob, v0, NT2) + shift)
            dl1 = p1 * (_dot(dob, v1, NT2) + shift)
            dl0b, dl1b = dl0.astype(BF16), dl1.astype(BF16)
            dq_ref[0, rows, :] = (_dot(dl0b, k0) + _dot(dl1b, k1)) * scale
            dk_ref[0, rows, :] += _dot(dl0b, qb, TN2)
            dk_ref[0, before, :] += _dot(dl1b, qb, TN2)
            dv_ref[0, rows, :] += _dot(p0.astype(BF16), dob, TN2)
            dv_ref[0, before, :] += _dot(p1.astype(BF16), dob, TN2)
            db_ref[0, 0] += dl0
            db_ref[0, 1] += dl1

    head_out = pl.BlockSpec((1, s, HEAD), lambda j, p: (DIL_GROUP * g + j, 0, 0), pipeline_mode=pl.Buffered(1))
    rows128 = pl.BlockSpec((1, s, 2 * HEAD), lambda j, p: (j, 0, 0), pipeline_mode=pl.Buffered(1))
    full_sh = jax.ShapeDtypeStruct(qn.shape, F32)
    return pl.pallas_call(
        body, name="dil%d_bwd" % g, grid=(DIL_GROUP, steps),
        in_specs=_dil_group_specs(g, s) + [rows128, rows128] + [pl.BlockSpec(memory_space=pl.ANY)] * len(prev),
        out_specs=[head_out, head_out, head_out, pl.BlockSpec((1, 2, QBLK, QBLK), lambda j, p: (j, 0, 0, 0))],
        out_shape=[full_sh, full_sh, full_sh, jax.ShapeDtypeStruct((DIL_GROUP, 2, QBLK, QBLK), F32)],
        input_output_aliases={6 + i: i for i in range(len(prev))},
        compiler_params=_params(("arbitrary", "arbitrary")),
    )(qn, kn, v, bias, ol, dol, *prev)


@functools.partial(jax.custom_vjp, nondiff_argnums=(2,))
def _bdot(a, b, dims):
    return _dot(a.astype(BF16), b.astype(BF16), dims)


def _bdot_fwd(a, b, dims):
    return _bdot(a, b, dims), (a, b)


def _bdot_bwd(dims, res, dc):
    a, b = res
    nn, nt, tn = (NN2, NT2, TN2) if dims in (NN2, NT2, TN2) else (NN3, NT3, TN3)
    if dims == nn:
        return _bdot(dc, b, nt), _bdot(a, dc, tn)
    if dims == nt:
        return _bdot(dc, b, nn), _bdot(dc, a, tn)
    return _bdot(b, dc, nt), _bdot(a, dc, nn)


_bdot.defvjp(_bdot_fwd, _bdot_bwd)


def _rwkv_chunk(s0, r, lw, kraw, v, ag, kk_w, ka_w, rk_w, lng, lnb):
    hb, c, _ = r.shape
    kk = kraw * kk_w
    kk = kk / jnp.maximum(jnp.sqrt(jnp.sum(kk * kk, axis=-1, keepdims=True)), 1e-12)
    k = kraw * (1.0 + (ag - 1.0) * ka_w)
    a = -kk
    b = kk * ag
    row = lax.broadcasted_iota(jnp.int32, (hb, c, c), 1)
    col = lax.broadcasted_iota(jnp.int32, (hb, c, c), 2)
    lower, strict = row >= col, row > col
    cum = _dot(lower.astype(F32), lw, NN3, lax.Precision.HIGHEST)
    ecum, einv = jnp.exp(cum), jnp.exp(-cum)
    rt, kt, bt = r * ecum, k * einv, b * einv
    at = a * jnp.exp(cum - lw)
    ar = jnp.concatenate([at, rt], axis=1)
    scores = _bdot(ar, jnp.concatenate([bt, kt], axis=1), NT3)
    a_ab = jnp.where(strict, scores[:, :c, :c], 0.0)
    a_ak = jnp.where(strict, scores[:, :c, c:], 0.0)
    p_rb = jnp.where(lower, scores[:, c:, :c], 0.0)
    p_rk = jnp.where(lower, scores[:, c:, c:], 0.0)
    from_s0 = _bdot(ar, s0, NT3)
    rhs = from_s0[:, :c] + _bdot(a_ak, v, NN3)
    inv = (row == col).astype(F32) + a_ab
    pw = a_ab
    for _ in range(int(math.log2(c)) - 1):
        pw = _bdot(pw, pw, NN3)
        inv = inv + _bdot(inv, pw, NN3)
    u = _bdot(inv, rhs, NN3)
    uv = jnp.concatenate([u, v], axis=1)
    y = from_s0[:, c:] + _bdot(jnp.concatenate([p_rb, p_rk], axis=2), uv, NN3)
    cum_end = cum[:, c - 1:c, :]
    dec = jnp.exp(cum_end - cum)
    s_end = s0 * jnp.exp(cum_end) + _bdot(uv, jnp.concatenate([b * dec, k * dec], axis=1), TN3)
    mu = jnp.mean(y, axis=-1, keepdims=True)
    var = jnp.mean(jnp.square(y - mu), axis=-1, keepdims=True)
    z = (y - mu) * lax.rsqrt(var + GN_EPS) * lng + lnb + jnp.sum(r * k * rk_w, axis=-1, keepdims=True) * v
    return z, s_end


def _rwkv_specs(nc, rev):
    cidx = (lambda c: nc - 1 - c) if rev else (lambda c: c)
    seq = pl.BlockSpec((RW_HB, RW_CHUNK, HEAD), lambda hg, c: (hg, cidx(c), 0))
    par = pl.BlockSpec((RW_HB, 1, HEAD), lambda hg, c: (hg, 0, 0))
    st = pl.BlockSpec((1, RW_HB, HEAD, HEAD), lambda hg, c: (cidx(c), hg, 0, 0))
    return seq, par, st


def _rwkv_fwd(seqs, pars):
    s = seqs[0].shape[1]
    nc = s // RW_CHUNK

    def body(*refs):
        seq_refs, par_refs = refs[:5], refs[5:10]
        z_ref, st_ref, state = refs[10:]
        c = pl.program_id(1)

        @pl.when(c == 0)
        def _():
            state[...] = jnp.zeros_like(state)

        s0 = state[...]
        st_ref[0] = s0
        z, s_end = _rwkv_chunk(s0, *[r[...] for r in seq_refs], *[r[...] for r in par_refs])
        z_ref[...] = z
        state[...] = s_end

    seq, par, st = _rwkv_specs(nc, False)
    return pl.pallas_call(
        body, name="rwkv_fwd", grid=(N_HEADS // RW_HB, nc),
        in_specs=[seq] * 5 + [par] * 5, out_specs=[seq, st],
        out_shape=[jax.ShapeDtypeStruct((N_HEADS, s, HEAD), F32), jax.ShapeDtypeStruct((nc, N_HEADS, HEAD, HEAD), F32)],
        scratch_shapes=[pltpu.VMEM((RW_HB, HEAD, HEAD), F32)],
        compiler_params=_params(("arbitrary", "arbitrary")),
    )(*seqs, *pars)


def _rwkv_bwd(seqs, pars, states, dz):
    s = seqs[0].shape[1]
    nc = s // RW_CHUNK

    def body(*refs):
        seq_refs, par_refs = refs[:5], refs[5:10]
        st_ref, dz_ref = refs[10:12]
        dseq_refs, dpar_refs, dstate = refs[12:17], refs[17:22], refs[22]
        c = pl.program_id(1)

        @pl.when(c == 0)
        def _():
            dstate[...] = jnp.zeros_like(dstate)
            for r in dpar_refs:
                r[...] = jnp.zeros_like(r)

        _, vjp = jax.vjp(_rwkv_chunk, st_ref[0], *[r[...] for r in seq_refs], *[r[...] for r in par_refs])
        g = vjp((dz_ref[...], dstate[...]))
        dstate[...] = g[0]
        for r, gs in zip(dseq_refs, g[1:6]):
            r[...] = gs
        for r, gp in zip(dpar_refs, g[6:]):
            r[...] += gp

    seq, par, st = _rwkv_specs(nc, True)
    seq_sh = jax.ShapeDtypeStruct((N_HEADS, s, HEAD), F32)
    par_sh = jax.ShapeDtypeStruct((N_HEADS, 1, HEAD), F32)
    outs = pl.pallas_call(
        body, name="rwkv_bwd", grid=(N_HEADS // RW_HB, nc),
        in_specs=[seq] * 5 + [par] * 5 + [st, seq],
        out_specs=[seq] * 5 + [par] * 5, out_shape=[seq_sh] * 5 + [par_sh] * 5,
        scratch_shapes=[pltpu.VMEM((RW_HB, HEAD, HEAD), F32)],
        compiler_params=_params(("arbitrary", "arbitrary")),
    )(*seqs, *pars, states, dz)
    return list(outs[:5]), list(outs[5:])


def _norm_fn(x, g):
    return (_rms(x, g),)


def _attn_prep_fn(proj, qn_w, kn_w):
    a, b = SB_HEADS, 3 * DIL_GROUP
    return (proj[0:a], proj[a:2 * a], proj[2 * a:3 * a],
            _rms(proj[3 * a:3 * a + b], qn_w), _rms(proj[3 * a + b:3 * a + 2 * b], kn_w), proj[3 * a + 2 * b:])


def _attn_merge_fn(o_sb, ol0, ol1, ol2):
    groups = (ol0, ol1, ol2)
    merged = []
    for j in range(DIL_GROUP):
        lses = [ol[j][:, HEAD:HEAD + 1] for ol in groups]
        m = functools.reduce(jnp.maximum, lses)
        es = [jnp.exp(l - m) for l in lses]
        inv = 1.0 / functools.reduce(jnp.add, es)
        merged.append(functools.reduce(jnp.add, [(e * inv) * ol[j][:, :HEAD] for e, ol in zip(es, groups)]))
    return (jnp.concatenate([_heads_to_nat(o_sb)] + merged, axis=-1),)


def _rw_mix_fn(x, xp, gn, mix, w0, w1, w2, a0, a1, a2, g1, g2):
    h = _rms(x, gn)
    xx = _rms(xp, gn) - h
    xr, xw, xk, xv, xa, xg = [h + xx * mix[i:i + 1] for i in range(6)]
    w_log = -jax.nn.softplus(-(w0 + _mm(jnp.tanh(_mm(xw, w1)), w2))) - 0.5
    lw = -jnp.exp(w_log)
    ag = jax.nn.sigmoid(a0 + _mm(_mm(xa, a1), a2))
    gate = _mm(jax.nn.sigmoid(_mm(xg, g1)), g2)
    return xr, xk, xv, _nat_to_heads(lw), _nat_to_heads(ag), gate


def _rw_gate_fn(z, gate):
    return (_heads_to_nat(z) * gate,)


def _adamw(name, w, m, v, gparts, row0=0, prev=None):
    big_r, c = w.shape
    r = gparts.shape[1]
    tr = r
    if r % 8 == 0:
        tr = max(t for t in range(8, r + 1, 8) if r % t == 0 and (t * c * 4 <= (1 << 20) or t == 8))
    assert row0 % tr == 0 and (r == big_r or r % 8 == 0)
    off = row0 // tr

    def body(w_ref, m_ref, v_ref, g_ref, *rest):
        go_ref, d_ref, mo_ref, vo_ref = rest[-4:]
        g = g_ref[0].astype(F32)
        for j in range(1, N_DEV):
            g = g + g_ref[j].astype(F32)
        mn = ADAM_B1 * m_ref[...] + (1.0 - ADAM_B1) * g
        vn = ADAM_B2 * v_ref[...] + (1.0 - ADAM_B2) * jnp.square(g)
        m_hat = mn / (1.0 - ADAM_B1 ** ADAM_STEP)
        v_hat = vn / (1.0 - ADAM_B2 ** ADAM_STEP)
        go_ref[...] = g
        d_ref[...] = -ADAM_LR * (m_hat / (jnp.sqrt(v_hat) + ADAM_EPS) + ADAM_WD * w_ref[...])
        mo_ref[...] = mn
        vo_ref[...] = vn

    tile = pl.BlockSpec((tr, c), lambda i: (i + off, 0))
    sh = jax.ShapeDtypeStruct((big_r, c), F32)
    prev = list(prev) if prev is not None else []
    return pl.pallas_call(
        body, name=name, grid=(r // tr,),
        in_specs=([tile, tile, tile, pl.BlockSpec((N_DEV, tr, c), lambda i: (0, i, 0))]
                  + [pl.BlockSpec(memory_space=pl.ANY)] * len(prev)),
        out_specs=[tile] * 4, out_shape=[sh] * 4,
        input_output_aliases={4 + j: j for j in range(len(prev))},
        compiler_params=_params(("arbitrary",)),
    )(w, m, v, gparts, *prev)


def _col_blocks_to_nat(g):
    return jnp.moveaxis(g, 0, 1).reshape(g.shape[1], -1)


def _nat_to_col_blocks(a):
    return jnp.moveaxis(a.reshape(a.shape[0], N_DEV, -1), 1, 0)


AG_GROUPS = ("f00", "att", "f01", "f10", "rw", "f11")
RS_GROUPS = ("f11", "rw", "f10", "f01", "f00", "att")
BF16_GRAD_GROUPS = ("att", "f00")
RW_SHARDED = ('rw_mix', 'rw_w0', 'rw_w1', 'rw_w2', 'rw_a0', 'rw_a1', 'rw_a2', 'rw_g1', 'rw_g2', 'rw_kk', 'rw_ka',
              'rw_wr', 'rw_wk', 'rw_wv', 'rw_wo', 'rw_lnx_g', 'rw_lnx_b')


def _step(x, target, rep, get, put):
    tied = lambda a, zero: a + zero[0, 0].astype(a.dtype)
    s, d = x.shape
    tf = min(512, s)
    tt = min(256, s)
    row = lambda a: a.reshape(1, -1)
    mix_norm = rep["mix_norm"]
    ffw = {(0, 0): get("f00", None)}
    ffn_norm = _col_blocks_to_nat(ffw[(0, 0)]["ffn_norm"].reshape(N_DEV, 4, -1))

    acts = {}

    def ffn(nm, xin, l, h):
        g = ffw[(l, h)]
        out, *acts[(l, h)] = _ffn_fwd(nm, xin, ffn_norm[2 * l + h][None], g["gate"], g["up"], g["down"], min(2 * tf, s))
        return out

    x1 = ffn("ffn00_fwd", x, 0, 0)
    att = get("att", x1)
    w_in = _col_blocks_to_nat(att["attn_w_in"])
    w_out = _col_blocks_to_nat(att["attn_w_out"])
    (h0,) = _tile_fwd("mixnorm0_fwd", _norm_fn, [(x1, "nat")], [mix_norm[0:1]], [((s, d), BF16, "nat")], tt)
    proj = _linear_fwd("attn_in_fwd", h0, w_in, tt, out_layout="hm")
    bias = _bias_tiles(rep["rel_bias"], s)
    prep_pars = [rep["attn_q_norm"], rep["attn_k_norm"]]
    sb_sh, dl_sh = (SB_HEADS, s, HEAD), (3 * DIL_GROUP, s, HEAD)
    sq, sk, sv, qn, kn, vd = _tile_fwd("attn_prep_fwd", _attn_prep_fn, [(proj, "hm")], prep_pars,
                                       [(sb_sh, BF16, "hm")] * 3 + [(dl_sh, F32, "hm")] * 3, tt // 2)
    o_sb = _sb_fwd(sq, sk, sv)
    ols = [_dil_group_fwd(g, qn, kn, vd, bias) for g in range(3)]
    merge_tiled = [(o_sb, "hm")] + [(ol, "hm") for ol in ols]
    (merged,) = _tile_fwd("merge_fwd", _attn_merge_fn, merge_tiled, [], [((s, 512), BF16, "nat")], tt)
    x2 = _linear_fwd("attn_out_fwd", merged, w_out, tt, residual=x1)
    ffw[(0, 1)] = get("f01", x2)
    x3 = ffn("ffn01_fwd", x2, 0, 1)
    ffw[(1, 0)] = get("f10", x3)
    x4 = ffn("ffn10_fwd", x3, 1, 0)
    rw = get("rw", x4)
    rw_mix = _col_blocks_to_nat(rw["rw_mix"])
    rw_w1, rw_a1, rw_g1 = (rw[k].reshape(d, -1) for k in ("rw_w1", "rw_a1", "rw_g1"))
    rw_w2, rw_a2, rw_g2 = (_col_blocks_to_nat(rw[k]) for k in ("rw_w2", "rw_a2", "rw_g2"))
    rw_w0, rw_a0 = row(rw["rw_w0"]), row(rw["rw_a0"])
    head_par = lambda a: a.reshape(N_HEADS, 1, HEAD)
    scan_pars = [head_par(rw["rw_kk"]), head_par(rw["rw_ka"]), head_par(rep["rw_rk"]),
                 head_par(rw["rw_lnx_g"]), head_par(rw["rw_lnx_b"])]
    w_rkv = [rw[k].reshape(d, d) for k in ("rw_wr", "rw_wk", "rw_wv")]
    w_o = rw["rw_wo"].reshape(d, d)
    x4p = jnp.pad(x4, ((1, 0), (0, 0)))[:-1]
    mix_tiled = [(x4, "nat"), (x4p, "nat")]
    mix_pars = [mix_norm[1:2], rw_mix, rw_w0, rw_w1, rw_w2, rw_a0, rw_a1, rw_a2, rw_g1, rw_g2]
    hm_sh = (N_HEADS, s, HEAD)
    xr, xk, xv, lw, ag, gate = _tile_fwd(
        "rw_mix_fwd", _rw_mix_fn, mix_tiled, mix_pars,
        [((s, d), BF16, "nat")] * 3 + [(hm_sh, F32, "hm")] * 2 + [((s, d), F32, "nat")], tt)
    r_h, k_h, v_h = [_linear_fwd("rw_%s_fwd" % nm, xi, wi, tt, out_layout="hm")
                     for nm, xi, wi in zip("rkv", (xr, xk, xv), w_rkv)]
    scan_seqs = [r_h, lw, k_h, v_h, ag]
    z, states = _rwkv_fwd(scan_seqs, scan_pars)
    (zg,) = _tile_fwd("rw_gate_fwd", _rw_gate_fn, [(z, "hm"), (gate, "nat")], [], [((s, d), BF16, "nat")], tt)
    x5 = _linear_fwd("rw_out_fwd", zg, w_o, tt, residual=x4)
    ffw[(1, 1)] = get("f11", x5)
    y = ffn("ffn11_fwd", x5, 1, 1)
    dy, loss = _loss_head(y, target, tf)

    G = {}
    dgn = {}

    def fb(nm, group, xin, dout, l, h, zero=None, extra=None):
        g = ffw[(l, h)]
        gn = ffn_norm[2 * l + h][None]
        dxin, dgn[(l, h)], dwg, dwu, dwd = _ffn_bwd(nm, xin, dout, gn if zero is None else tied(gn, zero),
                                                   g["gate"], g["up"], g["down"], *acts[(l, h)], tf)
        shard = {"gate": dwg, "up": dwu, "down": dwd}
        if extra is not None:
            shard.update(extra())
        return dxin, put(group, {}, shard)

    dx5, zero = fb("ffn11_bwd", "f11", x5, dy, 1, 1)
    dzg = _linear_dx("rw_out_dx", dx5, w_o, tt)
    G["rw_wo"] = _linear_dw("rw_out_dw", zg, dx5, tf, 512)
    (dz, dgate), _ = _tile_bwd("rw_gate_bwd", _rw_gate_fn, [(z, "hm"), (gate, "nat")], [], [(dzg, "nat")], tt, [True, True])
    (dr_h, dlw, dk_h, dv_h, dag), dscan = _rwkv_bwd(scan_seqs, [tied(scan_pars[0], zero)] + scan_pars[1:], states, dz)
    drkv = (dr_h, dk_h, dv_h)
    for k, gpar in zip(("rw_kk", "rw_ka", "rw_rk", "rw_lnx_g", "rw_lnx_b"), dscan):
        G[k] = gpar
    dxs = []
    for j, (nm, xi, wi) in enumerate(zip("rkv", (xr, xk, xv), w_rkv)):
        dxs.append(_linear_dx("rw_%s_dx" % nm, drkv[j], wi, tt, dy_layout="hm"))
        G["rw_w" + nm] = _linear_dw("rw_%s_dw" % nm, xi, drkv[j], tf, 512, dy_layout="hm")
    (dx4a, dx4p), dmix = _tile_bwd(
        "rw_mix_bwd", _rw_mix_fn, mix_tiled, mix_pars,
        [(dxs[0], "nat"), (dxs[1], "nat"), (dxs[2], "nat"), (dlw, "hm"), (dag, "hm"), (dgate, "nat")],
        tt, [True, True], adds=[dx5, None])
    d_mixn1 = dmix[0]
    for k, gpar in zip(("rw_mix", "rw_w0", "rw_w1", "rw_w2", "rw_a0", "rw_a1", "rw_a2", "rw_g1", "rw_g2"), dmix[1:]):
        G[k] = gpar
    dx4 = dx4a + jnp.pad(dx4p[1:], ((0, 1), (0, 0)))
    for k in ("rw_mix", "rw_w2", "rw_a2", "rw_g2"):
        G[k] = _nat_to_col_blocks(G[k])
    for k in ("rw_w1", "rw_a1", "rw_g1", "rw_wr", "rw_wk", "rw_wv", "rw_wo"):
        G[k] = G[k].reshape(N_DEV, d // N_DEV, -1)
    for k in ("rw_w0", "rw_a0", "rw_kk", "rw_ka", "rw_lnx_g", "rw_lnx_b"):
        G[k] = G[k].reshape(N_DEV, 1, d // N_DEV)
    zero = put("rw", {"rw_rk": G["rw_rk"].reshape(N_HEADS, HEAD)}, {k: G[k] for k in RW_SHARDED})
    dx3, zero = fb("ffn10_bwd", "f10", x3, dx4, 1, 0, zero)
    dx2, zero = fb("ffn01_bwd", "f01", x2, dx3, 0, 1, zero)
    dmerged = _linear_dx("attn_out_dx", dx2, tied(w_out, zero), tt)
    (do_sb, *dols), _ = _tile_bwd("merge_bwd", _attn_merge_fn, merge_tiled, [], [(dmerged, "nat")], tt, [True] * 4)
    dq_sb, dk_sb, dv_sb = _sb_bwd(sq, sk, sv, do_sb)
    dil_grads, dbias = None, []
    for g in range(3):
        *dil_grads, db = _dil_group_bwd(g, qn, kn, vd, bias, ols[g], dols[g], dil_grads)
        dbias.append(db)
    dqn, dkn, dvd = dil_grads
    dbias = jnp.concatenate(dbias, axis=1)
    (dproj,), (dqn_w, dkn_w) = _tile_bwd(
        "attn_prep_bwd", _attn_prep_fn, [(proj, "hm")], prep_pars,
        [(dq_sb, "hm"), (dk_sb, "hm"), (dv_sb, "hm"), (dqn, "hm"), (dkn, "hm"), (dvd, "hm")], tt // 2, [True])
    dh0 = _linear_dx("attn_in_dx", dproj, w_in, tt, dy_layout="hm")
    (dx1,), (d_mixn0,) = _tile_bwd("mixnorm0_bwd", _norm_fn, [(x1, "nat")], [mix_norm[0:1]], [(dh0, "nat")], tt,
                                   [True], adds=[dx2])
    order = [(0, 0), (0, 1), (1, 0), (1, 1)]
    norm_grads = lambda: {"ffn_norm": _nat_to_col_blocks(jnp.concatenate([dgn[o] for o in order], axis=0))}
    dx0, zero = fb("ffn00_bwd", "f00", x, dx1, 0, 0, extra=norm_grads)
    G["attn_w_out"] = _linear_dw("attn_out_dw", tied(merged, zero), dx2, tf, 512)
    G["attn_w_in"] = _linear_dw("attn_in_dw", tied(h0, zero), dproj, tf, 512, dy_layout="hm")
    rep_grads = {"mix_norm": jnp.concatenate([d_mixn0, d_mixn1], axis=0), "rel_bias": _bias_tiles_bwd(dbias, s),
                 "attn_q_norm": dqn_w, "attn_k_norm": dkn_w}
    zero = put("att", rep_grads, {k: _nat_to_col_blocks(G[k]) for k in ("attn_w_in", "attn_w_out")})
    return loss, dx0, zero


WEIGHTS = ['ffn_norm', 'ffn_w_gate', 'ffn_w_up', 'ffn_w_down', 'mix_norm', 'rel_bias', 'attn_w_in', 'attn_q_norm',
           'attn_k_norm', 'attn_w_out', 'rw_mix', 'rw_w0', 'rw_w1', 'rw_w2', 'rw_a0', 'rw_a1', 'rw_a2', 'rw_g1', 'rw_g2',
           'rw_kk', 'rw_ka', 'rw_rk', 'rw_wr', 'rw_wk', 'rw_wv', 'rw_wo', 'rw_lnx_g', 'rw_lnx_b']
REPLICATED = ('mix_norm', 'rel_bias', 'attn_q_norm', 'attn_k_norm', 'rw_rk')
BF16_WEIGHTS = ('ffn_w_gate', 'ffn_w_up', 'ffn_w_down', 'attn_w_in', 'attn_w_out', 'rw_wr', 'rw_wk', 'rw_wv', 'rw_wo')


def kernel(x, ffn_norm, ffn_w_gate, ffn_w_up, ffn_w_down, mix_norm, rel_bias, attn_w_in, attn_q_norm, attn_k_norm, attn_w_out, rw_mix, rw_w0, rw_w1, rw_w2, rw_a0, rw_a1, rw_a2, rw_g1, rw_g2, rw_kk, rw_ka, rw_rk, rw_wr, rw_wk, rw_wv, rw_wo, rw_lnx_g, rw_lnx_b, loss_target, m_ffn_norm, m_ffn_w_gate, m_ffn_w_up, m_ffn_w_down, m_mix_norm, m_rel_bias, m_attn_w_in, m_attn_q_norm, m_attn_k_norm, m_attn_w_out, m_rw_mix, m_rw_w0, m_rw_w1, m_rw_w2, m_rw_a0, m_rw_a1, m_rw_a2, m_rw_g1, m_rw_g2, m_rw_kk, m_rw_ka, m_rw_rk, m_rw_wr, m_rw_wk, m_rw_wv, m_rw_wo, m_rw_lnx_g, m_rw_lnx_b, v_ffn_norm, v_ffn_w_gate, v_ffn_w_up, v_ffn_w_down, v_mix_norm, v_rel_bias, v_attn_w_in, v_attn_q_norm, v_attn_k_norm, v_attn_w_out, v_rw_mix, v_rw_w0, v_rw_w1, v_rw_w2, v_rw_a0, v_rw_a1, v_rw_a2, v_rw_g1, v_rw_g2, v_rw_kk, v_rw_ka, v_rw_rk, v_rw_wr, v_rw_wk, v_rw_wv, v_rw_wo, v_rw_lnx_g, v_rw_lnx_b):
    args = locals()
    w = {k: args[k] for k in WEIGHTS}
    cast = lambda k, a: a.astype(BF16) if k in BF16_WEIGHTS else a

    sources = {}
    for l, h in ((0, 0), (0, 1), (1, 0), (1, 1)):
        sources["f%d%d" % (l, h)] = {"gate": cast("ffn_w_gate", ffn_w_gate[l, h]), "up": cast("ffn_w_up", ffn_w_up[l, h]),
                                     "down": cast("ffn_w_down", ffn_w_down[l, h])}
    sources["f00"]["ffn_norm"] = ffn_norm
    drop_lead = lambda a: a[0] if a.ndim == 3 else a
    sources["att"] = {k: cast(k, w[k][0]) for k in ("attn_w_in", "attn_w_out")}
    sources["rw"] = {k: cast(k, drop_lead(w[k])) for k in RW_SHARDED}
    ag, token = {}, None
    for group in AG_GROUPS:
        names, arrays = list(sources[group]), list(sources[group].values())
        if token is not None:
            arrays[0] = arrays[0] + token[0, 0].astype(arrays[0].dtype)
        ag[group] = (names, _exchange_start("ag_start_" + group, arrays, []))
        token = ag[group][1]["token"]
    last_ag_token = token

    def get(group, after):
        names, started = ag[group]
        gathered, _ = _exchange_wait("ag_wait_" + group, started, last_ag_token if after is None else after)
        return dict(zip(names, gathered))

    rs = {}

    def put(group, rep_grads, shard_grads):
        if group in BF16_GRAD_GROUPS:
            shard_grads = {k: v.astype(BF16) for k, v in shard_grads.items()}
        started = _exchange_start("rs_start_" + group, list(rep_grads.values()), list(shard_grads.values()))
        rs[group] = (list(rep_grads), list(shard_grads), started)
        return started["token"]

    loss, dx, last_zero = _step(x[0], loss_target[0], {k: w[k] for k in REPLICATED}, get, put)
    loss = lax.psum(loss, MESH_AXES)

    results = {}
    ffn_prev = {}

    def update(k, parts, row0=0, prev=None):
        c = w[k].shape[-1]
        as2d = lambda a: a.reshape(-1, c)
        return _adamw("adamw_%s_%d" % (k, row0), as2d(w[k]), as2d(args["m_" + k]), as2d(args["v_" + k]),
                      parts.reshape(N_DEV, -1, c), row0, prev)

    after = last_zero
    for group in RS_GROUPS:
        rep_names, shard_names, started = rs[group]
        rep_parts, shard_parts = _exchange_wait("rs_wait_" + group, started, after)
        for k, parts in list(zip(rep_names, rep_parts)) + list(zip(shard_names, shard_parts)):
            if k in ("gate", "up", "down"):
                full = "ffn_w_" + k
                piece = 2 * int(group[1]) + int(group[2])
                ffn_prev[full] = update(full, parts, piece * parts.shape[1], ffn_prev.get(full))
                results[full] = ffn_prev[full]
            else:
                results[k] = update(k, parts)
            after = results[k if k in results else "ffn_w_" + k][0]

    outs = [[results[k][j].reshape(w[k].shape) for k in WEIGHTS] for j in range(4)]
    return (loss, dx[None], *outs[0], *outs[1], *outs[2], *outs[3])
```

```python
import functools
import math

import numpy as np
import jax
import jax.numpy as jnp
from jax import lax
from jax.experimental import pallas as pl
from jax.experimental.pallas import tpu as pltpu

F32, BF16 = jnp.float32, jnp.bfloat16
HI = lax.Precision.HIGH

N_DEV = 8
D_MODEL = 1024
HEAD = 64
N_HEADS = 16
SB_HEADS = 4
DIL_GROUP = 4
DIL_PATTERNS = ((128, 1), (512, 4), (2048, 16))
QBLK = 128
N_BUCKETS = 32
MAX_DISTANCE = 2048
NORM_EPS = 1e-6
GN_EPS = 64e-5
NEG_INF = -1e30
RW_CHUNK = 64
RW_HB = 16
ADAM_LR, ADAM_B1, ADAM_B2, ADAM_EPS, ADAM_WD, ADAM_STEP = 0.001, 0.9, 0.999, 1e-08, 0.01, 10
MESH_AXES = ("x", "y", "c")
VMEM_LIMIT_BYTES = 56 * 1024 * 1024

NN2 = (((1,), (0,)), ((), ()))
NT2 = (((1,), (1,)), ((), ()))
TN2 = (((0,), (0,)), ((), ()))
NN3 = (((2,), (1,)), ((0,), (0,)))
NT3 = (((2,), (2,)), ((0,), (0,)))
TN3 = (((1,), (1,)), ((0,), (0,)))


def _dot(a, b, dims=NN2, prec=None):
    return lax.dot_general(a, b, dims, precision=prec, preferred_element_type=F32)


def _params(sem=None):
    return pltpu.CompilerParams(dimension_semantics=sem, vmem_limit_bytes=VMEM_LIMIT_BYTES)


@jax.custom_vjp
def _mm(x, w):
    return _dot(x.astype(BF16), w.astype(BF16))


def _mm_fwd(x, w):
    return _mm(x, w), (x, w)


def _mm_bwd(res, dy):
    x, w = res
    dyb = dy.astype(BF16)
    return (_dot(dyb, w.astype(BF16), NT2).astype(x.dtype), _dot(x.astype(BF16), dyb, TN2).astype(w.dtype))


_mm.defvjp(_mm_fwd, _mm_bwd)


def _rms(x, g):
    return x * lax.rsqrt(jnp.mean(x * x, axis=-1, keepdims=True) + NORM_EPS) * g


def _log_sigmoid(z):
    return jnp.minimum(z, 0.0) - jnp.log(1.0 + jnp.exp(-jnp.abs(z)))


def _heads_to_nat(v3):
    return jnp.concatenate([v3[h] for h in range(v3.shape[0])], axis=-1)


def _nat_to_heads(v2):
    return jnp.stack([v2[:, h * HEAD:(h + 1) * HEAD] for h in range(v2.shape[1] // HEAD)], axis=0)


def _exchange(name, gathers, scatters):
    n_g = len(gathers)
    arrays = list(gathers) + list(scatters)
    n = len(arrays)
    out_shape = [jax.ShapeDtypeStruct((N_DEV,) + a.shape, a.dtype) for a in gathers]
    out_shape += [jax.ShapeDtypeStruct(a.shape, a.dtype) for a in scatters]

    def body(*refs):
        ins, outs = refs[:n], refs[n:2 * n]
        send_sems, recv_sems, local_sems = refs[2 * n:]
        x, y, c = lax.axis_index("x"), lax.axis_index("y"), lax.axis_index("c")
        me = 4 * x + 2 * y + c

        def src(i, idx):
            return ins[i] if i < n_g else ins[i].at[idx]

        local = [pltpu.make_async_copy(src(i, me), outs[i].at[me], local_sems.at[i]) for i in range(n)]
        for cp in local:
            cp.start()
        remote = []
        for m in range(1, N_DEV):
            px, py, pc = x ^ ((m >> 2) & 1), y ^ ((m >> 1) & 1), c ^ (m & 1)
            peer = 4 * px + 2 * py + pc
            for i in range(n):
                cp = pltpu.make_async_remote_copy(
                    src_ref=src(i, peer), dst_ref=outs[i].at[me],
                    send_sem=send_sems.at[i, m - 1], recv_sem=recv_sems.at[i, m - 1],
                    device_id=(px, py, pc), device_id_type=pl.DeviceIdType.MESH)
                cp.start()
                arrival = pltpu.make_async_remote_copy(
                    src_ref=src(i, peer), dst_ref=outs[i].at[peer],
                    send_sem=send_sems.at[i, m - 1], recv_sem=recv_sems.at[i, m - 1],
                    device_id=(px, py, pc), device_id_type=pl.DeviceIdType.MESH)
                remote.append((cp, arrival))
        for cp, arrival in remote:
            cp.wait_send()
            arrival.wait_recv()
        for cp in local:
            cp.wait()

    hbm = pl.BlockSpec(memory_space=pltpu.HBM)
    outs = pl.pallas_call(
        body, name=name, out_shape=out_shape,
        in_specs=[hbm] * n, out_specs=[hbm] * n,
        scratch_shapes=[pltpu.SemaphoreType.DMA((n, N_DEV - 1)), pltpu.SemaphoreType.DMA((n, N_DEV - 1)),
                        pltpu.SemaphoreType.DMA((n,))],
    )(*arrays)
    return list(outs[:n_g]), list(outs[n_g:])


def _mesh_peers():
    x, y, c = lax.axis_index("x"), lax.axis_index("y"), lax.axis_index("c")
    peers = []
    for m in range(1, N_DEV):
        px, py, pc = x ^ ((m >> 2) & 1), y ^ ((m >> 1) & 1), c ^ (m & 1)
        peers.append((m, (px, py, pc), 4 * px + 2 * py + pc))
    return 4 * x + 2 * y + c, peers


_HBM_SPEC = pl.BlockSpec(memory_space=pltpu.HBM)
_SEM_SPEC = pl.BlockSpec(memory_space=pltpu.SEMAPHORE)
_DATAFLOW = pltpu.SideEffectType.DATAFLOW_SIDE_EFFECTING


def _exchange_start(name, gathers, scatters):
    n_g = len(gathers)
    arrays = list(gathers) + list(scatters)
    n = len(arrays)
    lands = ([lax.empty((N_DEV,) + a.shape, a.dtype) for a in gathers] + [lax.empty(a.shape, a.dtype) for a in scatters])

    def body(*refs):
        ins, land = refs[:n], refs[n:2 * n]
        send_sems, recv_sems, local_sems, token = refs[2 * n], refs[2 * n + 1], refs[2 * n + 2], refs[-1]
        me, peers = _mesh_peers()
        for m, dev, peer in peers:
            for i in range(n):
                k = i * (N_DEV - 1) + m - 1
                pltpu.make_async_remote_copy(
                    src_ref=ins[i] if i < n_g else ins[i].at[peer], dst_ref=land[i].at[me],
                    send_sem=send_sems.at[k], recv_sem=recv_sems.at[k],
                    device_id=dev, device_id_type=pl.DeviceIdType.MESH).start()
        for i in range(n):
            pltpu.make_async_copy(ins[i] if i < n_g else ins[i].at[me], land[i].at[me], local_sems.at[i]).start()
        token[...] = jnp.zeros_like(token)

    sem = pltpu.SemaphoreType.DMA((n * (N_DEV - 1),))
    outs = pl.pallas_call(
        body, name=name,
        out_shape=([sem, sem, pltpu.SemaphoreType.DMA((n,))] + [pltpu.HBM(a.shape, a.dtype) for a in arrays]
                   + [pltpu.HBM(l.shape, l.dtype) for l in lands] + [jax.ShapeDtypeStruct((8, 128), F32)]),
        in_specs=[_HBM_SPEC] * (2 * n),
        out_specs=[_SEM_SPEC] * 3 + [_HBM_SPEC] * (2 * n) + [pl.BlockSpec(memory_space=pltpu.VMEM)],
        input_output_aliases={i: i + 3 for i in range(2 * n)},
        compiler_params=pltpu.CompilerParams(has_side_effects=_DATAFLOW),
    )(*[pltpu.with_memory_space_constraint(a, pltpu.HBM) for a in arrays],
      *[pltpu.with_memory_space_constraint(l, pltpu.HBM) for l in lands])
    return dict(n_g=n_g, n=n, send=outs[0], recv=outs[1], local=outs[2], srcs=list(outs[3:3 + n]),
                lands=list(outs[3 + n:3 + 2 * n]), token=outs[-1])


def _exchange_wait(name, started, after):
    n, n_g = started["n"], started["n_g"]

    def body(*refs):
        srcs, lands = refs[:n], refs[n:2 * n]
        send_sems, recv_sems, local_sems = refs[2 * n], refs[2 * n + 1], refs[2 * n + 2]
        me, peers = _mesh_peers()
        local = [pltpu.make_async_copy(srcs[i] if i < n_g else srcs[i].at[me], lands[i].at[me], local_sems.at[i])
                 for i in range(n)]
        for m, dev, peer in peers:
            for i in range(n):
                k = i * (N_DEV - 1) + m - 1
                cp = pltpu.make_async_remote_copy(
                    src_ref=srcs[i] if i < n_g else srcs[i].at[peer], dst_ref=lands[i].at[peer],
                    send_sem=send_sems.at[k], recv_sem=recv_sems.at[k],
                    device_id=dev, device_id_type=pl.DeviceIdType.MESH)
                cp.wait_send()
                cp.wait_recv()
        for cp in local:
            cp.wait()

    outs = pl.pallas_call(
        body, name=name,
        out_shape=([pltpu.HBM(a.shape, a.dtype) for a in started["srcs"]]
                   + [pltpu.HBM(l.shape, l.dtype) for l in started["lands"]]),
        in_specs=[_HBM_SPEC] * (2 * n) + [_SEM_SPEC] * 3 + [pl.BlockSpec(memory_space=pl.ANY)],
        out_specs=[_HBM_SPEC] * (2 * n), input_output_aliases={i: i for i in range(2 * n)},
        compiler_params=pltpu.CompilerParams(has_side_effects=_DATAFLOW),
    )(*started["srcs"], *started["lands"], started["send"], started["recv"], started["local"], after)
    return list(outs[n:n + n_g]), list(outs[n + n_g:])


def _tile_spec(shape, layout, t):
    if layout == "nat":
        return pl.BlockSpec((t, shape[1]), lambda i: (i, 0))
    return pl.BlockSpec((shape[0], t, shape[2]), lambda i: (0, i, 0))


def _full_spec(shape):
    nd = len(shape)
    return pl.BlockSpec(tuple(shape), lambda i: (0,) * nd)


def _seq_len(a, layout):
    return a.shape[0] if layout == "nat" else a.shape[1]


def _tile_fwd(name, f, tiled, params, outs, t):
    nt, npar = len(tiled), len(params)
    s = _seq_len(*tiled[0])

    def body(*refs):
        vals = [r[...] for r in refs[:nt + npar]]
        res = f(*vals)
        for r, o in zip(refs[nt + npar:], res):
            r[...] = o.astype(r.dtype)

    return pl.pallas_call(
        body, name=name, grid=(s // t,),
        in_specs=[_tile_spec(a.shape, l, t) for a, l in tiled] + [_full_spec(p.shape) for p in params],
        out_specs=[_tile_spec(sh, l, t) for sh, _, l in outs],
        out_shape=[jax.ShapeDtypeStruct(sh, dt) for sh, dt, _ in outs],
        compiler_params=_params(("arbitrary",)),
    )(*[a for a, _ in tiled], *params)


def _tile_bwd(name, f, tiled, params, cts, t, need, adds=None):
    nt, npar, nc = len(tiled), len(params), len(cts)
    s = _seq_len(*tiled[0])
    need_idx = [k for k in range(nt) if need[k]]
    adds = adds or [None] * len(need_idx)
    add_arrays = [(a, tiled[k][1]) for a, k in zip(adds, need_idx) if a is not None]
    n_add = len(add_arrays)

    def body(*refs):
        i = pl.program_id(0)
        vals = [r[...] for r in refs[:nt + npar]]
        ct_refs = refs[nt + npar:nt + npar + nc]
        add_refs = refs[nt + npar + nc:nt + npar + nc + n_add]
        out_refs = refs[nt + npar + nc + n_add:]
        res, vjp = jax.vjp(f, *vals)
        grads = vjp(tuple(r[...].astype(o.dtype) for r, o in zip(ct_refs, res)))
        a = 0
        for j, k in enumerate(need_idx):
            g = grads[k]
            if adds[j] is not None:
                g = g + add_refs[a][...]
                a += 1
            out_refs[j][...] = g.astype(out_refs[j].dtype)
        for j in range(npar):
            r = out_refs[len(need_idx) + j]

            @pl.when(i == 0)
            def _():
                r[...] = jnp.zeros_like(r)

            r[...] += grads[nt + j]

    outs = pl.pallas_call(
        body, name=name, grid=(s // t,),
        in_specs=([_tile_spec(a.shape, l, t) for a, l in tiled] + [_full_spec(p.shape) for p in params]
                  + [_tile_spec(a.shape, l, t) for a, l in cts] + [_tile_spec(a.shape, l, t) for a, l in add_arrays]),
        out_specs=([_tile_spec(tiled[k][0].shape, tiled[k][1], t) for k in need_idx]
                   + [_full_spec(p.shape) for p in params]),
        out_shape=([jax.ShapeDtypeStruct(tiled[k][0].shape, F32) for k in need_idx]
                   + [jax.ShapeDtypeStruct(p.shape, F32) for p in params]),
        compiler_params=_params(("arbitrary",)),
    )(*[a for a, _ in tiled], *params, *[a for a, _ in cts], *[a for a, _ in add_arrays])
    return list(outs[:len(need_idx)]), list(outs[len(need_idx):])


def _linear_fwd(name, x, w, t, out_layout="nat", residual=None):
    s, k = x.shape
    n = w.shape[1]
    has_res = residual is not None

    def body(*refs):
        x_ref, w_ref = refs[0], refs[1]
        o_ref = refs[-1]
        y = _dot(x_ref[...].astype(BF16), w_ref[...])
        if has_res:
            y = y + refs[2][...]
        if out_layout == "hm":
            for h in range(n // HEAD):
                o_ref[h] = y[:, h * HEAD:(h + 1) * HEAD]
        else:
            o_ref[...] = y

    out_sh = (s, n) if out_layout == "nat" else (n // HEAD, s, HEAD)
    ins = [x, w] + ([residual] if has_res else [])
    in_specs = [_tile_spec(x.shape, "nat", t), _full_spec(w.shape)] + ([_tile_spec((s, n), "nat", t)] if has_res else [])
    return pl.pallas_call(
        body, name=name, grid=(s // t,), in_specs=in_specs,
        out_specs=_tile_spec(out_sh, out_layout, t), out_shape=jax.ShapeDtypeStruct(out_sh, F32),
        compiler_params=_params(("arbitrary",)),
    )(*ins)


def _linear_dx(name, dy, w, t, dy_layout="nat"):
    k, n = w.shape
    s = _seq_len(dy, dy_layout)

    def body(dy_ref, w_ref, o_ref):
        dy = _heads_to_nat(dy_ref[...].astype(BF16)) if dy_layout == "hm" else dy_ref[...].astype(BF16)
        o_ref[...] = _dot(dy, w_ref[...], NT2)

    return pl.pallas_call(
        body, name=name, grid=(s // t,),
        in_specs=[_tile_spec(dy.shape, dy_layout, t), _full_spec(w.shape)],
        out_specs=_tile_spec((s, k), "nat", t), out_shape=jax.ShapeDtypeStruct((s, k), F32),
        compiler_params=_params(("arbitrary",)),
    )(dy, w)


def _linear_dw(name, x, dy, t, nb, dy_layout="nat"):
    s, k = x.shape
    n = dy.shape[1] if dy_layout == "nat" else dy.shape[0] * HEAD

    def body(x_ref, dy_ref, o_ref):
        i = pl.program_id(1)

        @pl.when(i == 0)
        def _():
            o_ref[...] = jnp.zeros_like(o_ref)

        dy = _heads_to_nat(dy_ref[...].astype(BF16)) if dy_layout == "hm" else dy_ref[...].astype(BF16)
        o_ref[...] += _dot(x_ref[...].astype(BF16), dy, TN2)

    if dy_layout == "hm":
        dy_spec = pl.BlockSpec((nb // HEAD, t, HEAD), lambda j, i: (j, i, 0))
    else:
        dy_spec = pl.BlockSpec((t, nb), lambda j, i: (i, j))
    return pl.pallas_call(
        body, name=name, grid=(n // nb, s // t),
        in_specs=[pl.BlockSpec((t, k), lambda j, i: (i, 0)), dy_spec],
        out_specs=pl.BlockSpec((k, nb), lambda j, i: (0, j)), out_shape=jax.ShapeDtypeStruct((k, n), F32),
        compiler_params=_params(("arbitrary", "arbitrary")),
    )(x, dy)


def _ffn_fwd(name, x, gn, wg, wu, wd, t):
    s, d = x.shape
    f8 = wg.shape[-1]

    def body(x_ref, g_ref, wg_ref, wu_ref, wd_ref, o_ref, gk_ref, uk_ref, h_scr, acc):
        k = pl.program_id(1)

        @pl.when(k == 0)
        def _():
            h_scr[...] = _rms(x_ref[...], g_ref[...]).astype(BF16)
            acc[...] = jnp.zeros_like(acc)

        hb = h_scr[...]
        gk = _dot(hb, wg_ref[0])
        uk = _dot(hb, wu_ref[0])
        gk_ref[0] = gk
        uk_ref[0] = uk
        a = gk * jax.nn.sigmoid(gk) * uk
        acc[...] += _dot(a.astype(BF16), wd_ref[0])

        @pl.when(k == N_DEV - 1)
        def _():
            o_ref[...] = x_ref[...] + 0.5 * acc[...]

    wspec = lambda shp: pl.BlockSpec((1,) + shp, lambda i, k: (k, 0, 0))
    act = pl.BlockSpec((1, t, f8), lambda i, k: (k, i, 0))
    act_sh = jax.ShapeDtypeStruct((N_DEV, s, f8), F32)
    return pl.pallas_call(
        body, name=name, grid=(s // t, N_DEV),
        in_specs=[pl.BlockSpec((t, d), lambda i, k: (i, 0)), pl.BlockSpec((1, d), lambda i, k: (0, 0)),
                  wspec((d, f8)), wspec((d, f8)), wspec((f8, d))],
        out_specs=[pl.BlockSpec((t, d), lambda i, k: (i, 0)), act, act],
        out_shape=[jax.ShapeDtypeStruct((s, d), F32), act_sh, act_sh],
        scratch_shapes=[pltpu.VMEM((t, d), BF16), pltpu.VMEM((t, d), F32)],
        compiler_params=_params(("arbitrary", "arbitrary")),
    )(x, gn, wg, wu, wd)


def _ffn_bwd(name, x, dy, gn, wg, wu, wd, gact, uact, t):
    s, d = x.shape
    f8 = wg.shape[-1]
    last = N_DEV - 1

    def body(x_ref, dy_ref, g_ref, wg_ref, wu_ref, wd_ref, gk_ref, uk_ref,
             dx_ref, dg_ref, dwg_ref, dwu_ref, dwd_ref, dh_scr):
        k, i = pl.program_id(0), pl.program_id(1)
        x = x_ref[...]
        rs = lax.rsqrt(jnp.mean(x * x, axis=-1, keepdims=True) + NORM_EPS)
        xn = x * rs
        hb = (xn * g_ref[...]).astype(BF16)
        dob = (0.5 * dy_ref[...]).astype(BF16)
        wgk, wuk, wdk = wg_ref[0], wu_ref[0], wd_ref[0]
        gk, uk = gk_ref[0], uk_ref[0]
        sg = jax.nn.sigmoid(gk)
        sk = gk * sg
        da = _dot(dob, wdk, NT2)
        du = (da * sk).astype(BF16)
        dg = (da * uk * (sg * (1.0 + gk * (1.0 - sg)))).astype(BF16)

        @pl.when(i == 0)
        def _():
            dwg_ref[...] = jnp.zeros_like(dwg_ref)
            dwu_ref[...] = jnp.zeros_like(dwu_ref)
            dwd_ref[...] = jnp.zeros_like(dwd_ref)

        dwd_ref[0] += _dot((sk * uk).astype(BF16), dob, TN2)
        dwg_ref[0] += _dot(hb, dg, TN2)
        dwu_ref[0] += _dot(hb, du, TN2)
        dh = _dot(dg, wgk, NT2) + _dot(du, wuk, NT2)
        rows = pl.ds(pl.multiple_of(i * t, t), t)

        @pl.when(k == 0)
        def _():
            dh_scr[rows, :] = dh

        @pl.when(k > 0)
        def _():
            dh_scr[rows, :] += dh

        @pl.when(jnp.logical_and(k == last, i == 0))
        def _():
            dg_ref[...] = jnp.zeros_like(dg_ref)

        @pl.when(k == last)
        def _():
            dht = dh_scr[rows, :]
            dg_ref[...] += jnp.sum(dht * xn, axis=0, keepdims=True)
            dxn = dht * g_ref[...]
            dx_ref[...] = dy_ref[...] + rs * (dxn - xn * jnp.mean(dxn * xn, axis=-1, keepdims=True))

    wspec = lambda shp: pl.BlockSpec((1,) + shp, lambda k, i: (k, 0, 0))
    tile = pl.BlockSpec((t, d), lambda k, i: (i, 0))
    act = pl.BlockSpec((1, t, f8), lambda k, i: (k, i, 0))
    return pl.pallas_call(
        body, name=name, grid=(N_DEV, s // t),
        in_specs=[tile, tile, pl.BlockSpec((1, d), lambda k, i: (0, 0)), wspec((d, f8)), wspec((d, f8)), wspec((f8, d)),
                  act, act],
        out_specs=[pl.BlockSpec((t, d), lambda k, i: (jnp.where(k == last, i, 0), 0)),
                   pl.BlockSpec((1, d), lambda k, i: (0, 0)),
                   pl.BlockSpec((1, d, f8), lambda k, i: (k, 0, 0)), pl.BlockSpec((1, d, f8), lambda k, i: (k, 0, 0)),
                   pl.BlockSpec((1, f8, d), lambda k, i: (k, 0, 0))],
        out_shape=[jax.ShapeDtypeStruct((s, d), F32), jax.ShapeDtypeStruct((1, d), F32),
                   jax.ShapeDtypeStruct((N_DEV, d, f8), F32), jax.ShapeDtypeStruct((N_DEV, d, f8), F32),
                   jax.ShapeDtypeStruct((N_DEV, f8, d), F32)],
        scratch_shapes=[pltpu.VMEM((s, d), F32)],
        compiler_params=_params(("arbitrary", "arbitrary")),
    )(x, dy, gn, wg, wu, wd, gact, uact)


def _loss_head(y, target, t):
    s, d = y.shape

    def body(y_ref, t_ref, dy_ref, l_ref):
        i = pl.program_id(0)
        err = y_ref[...] - t_ref[...]
        dy_ref[...] = err * (1.0 / d)

        @pl.when(i == 0)
        def _():
            l_ref[...] = jnp.zeros_like(l_ref)

        l_ref[...] += 0.5 * jnp.sum(jnp.mean(err * err, axis=-1, keepdims=True), axis=0, keepdims=True)

    tile = pl.BlockSpec((t, d), lambda i: (i, 0))
    dy, l = pl.pallas_call(
        body, name="loss_head", grid=(s // t,), in_specs=[tile, tile],
        out_specs=[tile, pl.BlockSpec((1, 1), lambda i: (0, 0))],
        out_shape=[jax.ShapeDtypeStruct((s, d), F32), jax.ShapeDtypeStruct((1, 1), F32)],
        compiler_params=_params(("arbitrary",)),
    )(y, target)
    return dy, l[0, 0]


SB_KEY_TILE = 1024
SB_HEADS_PER_STEP = 4


def _sb_scan_mats():
    row = lax.broadcasted_iota(jnp.int32, (QBLK, QBLK), 0)
    col = lax.broadcasted_iota(jnp.int32, (QBLK, QBLK), 1)
    return (row > col).astype(F32).astype(BF16), (row < col).astype(F32).astype(BF16)


def _sb_tile_scan(x, mat, reverse):
    nsub = x.shape[1] // QBLK
    outs, carry = [None] * nsub, jnp.zeros((x.shape[0], 1), F32)
    for i in (reversed(range(nsub)) if reverse else range(nsub)):
        xs = x[:, i * QBLK:(i + 1) * QBLK]
        hi = xs.astype(BF16)
        lo = (xs - hi.astype(F32)).astype(BF16)
        outs[i] = _dot(hi, mat) + _dot(lo, mat) + carry
        carry = carry + jnp.sum(xs, axis=1, keepdims=True)
    return jnp.concatenate(outs, axis=1), carry


def _sb_before_query(n, t, kt):
    row = lax.broadcasted_iota(jnp.int32, (QBLK, kt), 0)
    col = lax.broadcasted_iota(jnp.int32, (QBLK, kt), 1)
    return t * kt + col < n * QBLK + row


def _sb_fwd(q, k, v):
    _, s, _ = q.shape
    scale = HEAD ** -0.5
    kt = min(SB_KEY_TILE, s)

    def body(q_ref, k_ref, v_ref, o_ref):
        n = pl.program_id(1)
        suffix, _ = _sb_scan_mats()
        n_tiles = lax.div(n, jnp.int32(kt // QBLK)) + 1
        heads = range(SB_HEADS_PER_STEP)
        qb = [(q_ref[h] * scale).astype(q_ref.dtype) for h in heads]

        def tile(t, carry, diagonal):
            rows = pl.ds(pl.multiple_of(t * kt, kt), kt)
            out = []
            for h in heads:
                c, acc = carry[h]
                z = _dot(qb[h], k_ref[h, rows, :], NT2)
                lk = _log_sigmoid(-z)
                log_beta = z + lk
                if diagonal:
                    ok = _sb_before_query(n, t, kt)
                    lk = jnp.where(ok, lk, 0.0)
                later, total = _sb_tile_scan(lk, suffix, True)
                w = jnp.exp(log_beta + later + c)
                if diagonal:
                    w = jnp.where(ok, w, 0.0)
                out.append((c + total, acc + _dot(w.astype(BF16), v_ref[h, rows, :])))
            return tuple(out)

        zero = (jnp.zeros((QBLK, 1), F32), jnp.zeros((QBLK, HEAD), F32))
        carry = tile(n_tiles - 1, (zero,) * len(heads), True)
        carry = lax.fori_loop(1, n_tiles, lambda jj, cr: tile(n_tiles - 1 - jj, cr, False), carry)
        for h in heads:
            o_ref[h] = carry[h][1]

    hp = SB_HEADS_PER_STEP
    return pl.pallas_call(
        body, name="sb_fwd", grid=(SB_HEADS // hp, s // QBLK),
        in_specs=[pl.BlockSpec((hp, QBLK, HEAD), lambda h, n: (h, n, 0)),
                  pl.BlockSpec((hp, s, HEAD), lambda h, n: (h, 0, 0)),
                  pl.BlockSpec((hp, s, HEAD), lambda h, n: (h, 0, 0))],
        out_specs=pl.BlockSpec((hp, QBLK, HEAD), lambda h, n: (h, n, 0)),
        out_shape=jax.ShapeDtypeStruct((SB_HEADS, s, HEAD), F32),
        compiler_params=_params(("arbitrary", "arbitrary")),
    )(q, k, v)


def _sb_bwd(q, k, v, do):
    _, s, _ = q.shape
    scale = HEAD ** -0.5
    kt = min(SB_KEY_TILE, s)

    def body(q_ref, k_ref, v_ref, do_ref, dq_ref, dk_ref, dv_ref, e_scr, beta_scr):
        n = pl.program_id(1)

        @pl.when(n == 0)
        def _():
            dk_ref[...] = jnp.zeros_like(dk_ref)
            dv_ref[...] = jnp.zeros_like(dv_ref)

        suffix, prefix = _sb_scan_mats()
        n_tiles = lax.div(n, jnp.int32(kt // QBLK)) + 1
        heads = range(SB_HEADS_PER_STEP)
        qb = [(q_ref[h] * scale).astype(q_ref.dtype) for h in heads]
        dob = [do_ref[h].astype(BF16) for h in heads]

        def weights(t, cs, diagonal):
            rows = pl.ds(pl.multiple_of(t * kt, kt), kt)
            out = []
            for h in heads:
                vb = v_ref[h, rows, :]
                z = _dot(qb[h], k_ref[h, rows, :], NT2)
                lk = _log_sigmoid(-z)
                log_beta = z + lk
                if diagonal:
                    ok = _sb_before_query(n, t, kt)
                    lk = jnp.where(ok, lk, 0.0)
                later, total = _sb_tile_scan(lk, suffix, True)
                w = jnp.exp(log_beta + later + cs[h])
                if diagonal:
                    w = jnp.where(ok, w, 0.0)
                e_scr[h, t] = w * _dot(dob[h], vb, NT2)
                beta_scr[h, t] = jnp.exp(log_beta)
                dv_ref[h, rows, :] += _dot(w.astype(BF16), dob[h], TN2)
                out.append(cs[h] + total)
            return tuple(out)

        col0 = jnp.zeros((QBLK, 1), F32)
        cs = weights(n_tiles - 1, (col0,) * len(heads), True)
        lax.fori_loop(1, n_tiles, lambda jj, c: weights(n_tiles - 1 - jj, c, False), cs)

        def grads(t, carry, diagonal):
            rows = pl.ds(pl.multiple_of(t * kt, kt), kt)
            out = []
            for h in heads:
                pc, dq = carry[h]
                kb = k_ref[h, rows, :]
                e, beta = e_scr[h, t], beta_scr[h, t]
                before, total = _sb_tile_scan(e, prefix, False)
                dz = e * (1.0 - beta) - beta * (before + pc)
                if diagonal:
                    dz = jnp.where(_sb_before_query(n, t, kt), dz, 0.0)
                dz = dz.astype(BF16)
                dk_ref[h, rows, :] += _dot(dz, qb[h], TN2)
                out.append((pc + total, dq + _dot(dz, kb)))
            return tuple(out)

        zero = (col0, jnp.zeros((QBLK, HEAD), F32))
        carry = lax.fori_loop(0, n_tiles - 1, lambda t, cr: grads(t, cr, False), (zero,) * len(heads))
        carry = grads(n_tiles - 1, carry, True)
        for h in heads:
            dq_ref[h] = carry[h][1] * scale

    hp = SB_HEADS_PER_STEP
    qspec = pl.BlockSpec((hp, QBLK, HEAD), lambda h, n: (h, n, 0))
    full = pl.BlockSpec((hp, s, HEAD), lambda h, n: (h, 0, 0))
    sh = jax.ShapeDtypeStruct((SB_HEADS, s, HEAD), F32)
    tiles_sh = (hp, s // kt, QBLK, kt)
    return pl.pallas_call(
        body, name="sb_bwd", grid=(SB_HEADS // hp, s // QBLK),
        in_specs=[qspec, full, full, qspec],
        out_specs=[qspec, full, full], out_shape=[sh, sh, sh],
        scratch_shapes=[pltpu.VMEM(tiles_sh, F32), pltpu.VMEM(tiles_sh, F32)],
        compiler_params=_params(("arbitrary", "arbitrary")),
    )(q, k, v, do)


def _t5_bucket_np(dist):
    max_exact = N_BUCKETS // 2
    d = np.maximum(dist, 1).astype(np.float32)
    large = max_exact + (np.log(d / np.float32(max_exact)) / np.float32(math.log(MAX_DISTANCE / max_exact))
                         * np.float32(N_BUCKETS - max_exact)).astype(np.int32)
    large = np.minimum(large, N_BUCKETS - 1)
    return np.where(dist < max_exact, dist, large)


def _dil_layout(s):
    assert all(s % (QBLK * r) == 0 and window // r == QBLK for window, r in DIL_PATTERNS)
    tiles, buckets = [], []
    i = np.arange(QBLK)[:, None]
    j = np.arange(QBLK)[None, :]
    for g, (window, r) in enumerate(DIL_PATTERNS):
        for off in (0, 1):
            dist = QBLK * off + i - j
            ok = (dist >= 0) & (dist <= window // r)
            tiles.append((g, off))
            buckets.append(np.where(ok, _t5_bucket_np(np.maximum(dist, 0) * r), -1).astype(np.int32))
    return tiles, np.stack(buckets)


def _bias_tiles(rel_bias, s):
    tiles, buckets = _dil_layout(s)
    nt = len(tiles)
    present = [sorted(set(np.unique(buckets[k]).tolist()) - {-1}) for k in range(nt)]

    def body(rel_ref, b_ref, o_ref):
        j = pl.program_id(0)
        for k, (g, _) in enumerate(tiles):
            bk = b_ref[k]
            tile = jnp.full((QBLK, QBLK), NEG_INF, F32)
            for b in present[k]:
                tile = jnp.where(bk == b, rel_ref[b, g * DIL_GROUP + j], tile)
            o_ref[0, k] = tile

    return pl.pallas_call(
        body, name="bias_tiles", grid=(DIL_GROUP,),
        in_specs=[pl.BlockSpec(memory_space=pltpu.SMEM), pl.BlockSpec((nt, QBLK, QBLK), lambda j: (0, 0, 0))],
        out_specs=pl.BlockSpec((1, nt, QBLK, QBLK), lambda j: (j, 0, 0, 0)),
        out_shape=jax.ShapeDtypeStruct((DIL_GROUP, nt, QBLK, QBLK), F32),
        compiler_params=_params(("arbitrary",)),
    )(rel_bias, jnp.asarray(buckets))


def _bias_tiles_bwd(dbias, s):
    tiles, buckets = _dil_layout(s)
    nt = len(tiles)
    present = [sorted(set(np.unique(buckets[k]).tolist()) - {-1}) for k in range(nt)]

    def body(d_ref, b_ref, o_ref):
        j = pl.program_id(0)

        @pl.when(j == 0)
        def _():
            for b in range(N_BUCKETS):
                for col in range(3 * DIL_GROUP):
                    o_ref[b, col] = jnp.float32(0.0)

        for k, (g, _) in enumerate(tiles):
            bk, dk = b_ref[k], d_ref[0, k]
            for b in present[k]:
                o_ref[b, g * DIL_GROUP + j] += jnp.sum(jnp.where(bk == b, dk, 0.0))

    return pl.pallas_call(
        body, name="bias_tiles_bwd", grid=(DIL_GROUP,),
        in_specs=[pl.BlockSpec((1, nt, QBLK, QBLK), lambda j: (j, 0, 0, 0)),
                  pl.BlockSpec((nt, QBLK, QBLK), lambda j: (0, 0, 0))],
        out_specs=pl.BlockSpec(memory_space=pltpu.SMEM),
        out_shape=jax.ShapeDtypeStruct((N_BUCKETS, 3 * DIL_GROUP), F32),
        compiler_params=_params(("arbitrary",)),
    )(dbias, jnp.asarray(buckets))


DIL_PAIRS_PER_STEP = 4


def _dil_rows(g, s, pair):
    _, r = DIL_PATTERNS[g]
    nb = s // (QBLK * r)
    c, n = lax.div(pair, jnp.int32(nb)), lax.rem(pair, jnp.int32(nb))
    start = c + (r * QBLK) * n
    before = jnp.where(n > 0, start - r * QBLK, start)
    if r == 1:
        return pl.ds(start, QBLK), pl.ds(before, QBLK), n > 0
    return pl.ds(start, QBLK, stride=r), pl.ds(before, QBLK, stride=r), n > 0


def _dil_logits(qb, k_ref, rows, before, has_before, b_ref):
    k0, k1 = k_ref[0, rows, :].astype(BF16), k_ref[0, before, :].astype(BF16)
    l0 = _dot(qb, k0, NT2) + b_ref[0, 0]
    l1 = jnp.where(has_before, _dot(qb, k1, NT2) + b_ref[0, 1], NEG_INF)
    return k0, k1, l0, l1


def _dil_group_specs(g, s):
    head = pl.BlockSpec((1, s, HEAD), lambda j, p: (DIL_GROUP * g + j, 0, 0), pipeline_mode=pl.Buffered(1))
    return [head, head, head, pl.BlockSpec((1, 2, QBLK, QBLK), lambda j, p: (j, g, 0, 0))]


def _dil_group_fwd(g, qn, kn, v, bias):
    _, s, _ = qn.shape
    scale = HEAD ** -0.5
    steps = (s // QBLK) // DIL_PAIRS_PER_STEP

    def body(q_ref, k_ref, v_ref, b_ref, o_ref):
        for u in range(DIL_PAIRS_PER_STEP):
            rows, before, has_before = _dil_rows(g, s, pl.program_id(1) * DIL_PAIRS_PER_STEP + u)
            qb = (q_ref[0, rows, :] * scale).astype(BF16)
            _, _, l0, l1 = _dil_logits(qb, k_ref, rows, before, has_before, b_ref)
            m = jnp.max(jnp.maximum(l0, l1), axis=1, keepdims=True)
            p0, p1 = jnp.exp(l0 - m), jnp.exp(l1 - m)
            den = jnp.sum(p0 + p1, axis=1, keepdims=True)
            inv = 1.0 / den
            o = (_dot((p0 * inv).astype(BF16), v_ref[0, rows, :].astype(BF16))
                 + _dot((p1 * inv).astype(BF16), v_ref[0, before, :].astype(BF16)))
            lse = jnp.broadcast_to(m + jnp.log(den), (QBLK, HEAD))
            o_ref[0, rows, :] = jnp.concatenate([o, lse], axis=1)

    return pl.pallas_call(
        body, name="dil%d_fwd" % g, grid=(DIL_GROUP, steps), in_specs=_dil_group_specs(g, s),
        out_specs=pl.BlockSpec((1, s, 2 * HEAD), lambda j, p: (j, 0, 0)),
        out_shape=jax.ShapeDtypeStruct((DIL_GROUP, s, 2 * HEAD), F32),
        compiler_params=_params(("arbitrary", "arbitrary")),
    )(qn, kn, v, bias)


def _dil_group_bwd(g, qn, kn, v, bias, ol, dol, prev):
    _, s, _ = qn.shape
    scale = HEAD ** -0.5
    steps = (s // QBLK) // DIL_PAIRS_PER_STEP
    prev = list(prev) if prev is not None else []

    def body(q_ref, k_ref, v_ref, b_ref, ol_ref, dol_ref, *rest):
        dq_ref, dk_ref, dv_ref, db_ref = rest[-4:]

        @pl.when(pl.program_id(1) == 0)
        def _():
            for r in (dk_ref, dv_ref, db_ref):
                r[...] = jnp.zeros_like(r)

        for u in range(DIL_PAIRS_PER_STEP):
            rows, before, has_before = _dil_rows(g, s, pl.program_id(1) * DIL_PAIRS_PER_STEP + u)
            qb = (q_ref[0, rows, :] * scale).astype(BF16)
            k0, k1, l0, l1 = _dil_logits(qb, k_ref, rows, before, has_before, b_ref)
            v0, v1 = v_ref[0, rows, :].astype(BF16), v_ref[0, before, :].astype(BF16)
            out_lse, d_out_lse = ol_ref[0, rows, :], dol_ref[0, rows, :]
            o, lse = out_lse[:, :HEAD], out_lse[:, HEAD:HEAD + 1]
            do, dlse = d_out_lse[:, :HEAD], d_out_lse[:, HEAD:HEAD + 1]
            dob = do.astype(BF16)
            p0, p1 = jnp.exp(l0 - lse), jnp.exp(l1 - lse)
            shift = dlse - jnp.sum(do * o, axis=1, keepdims=True)
            dl0 = p0 * (_dot(dob, v0, NT2) + shift)
            dl1 = p1 * (_dot(dob, v1, NT2) + shift)
            dl0b, dl1b = dl0.astype(BF16), dl1.astype(BF16)
            dq_ref[0, rows, :] = (_dot(dl0b, k0) + _dot(dl1b, k1)) * scale
            dk_ref[0, rows, :] += _dot(dl0b, qb, TN2)
            dk_ref[0, before, :] += _dot(dl1b, qb, TN2)
            dv_ref[0, rows, :] += _dot(p0.astype(BF16), dob, TN2)
            dv_ref[0, before, :] += _dot(p1.astype(BF16), dob, TN2)
            db_ref[0, 0] += dl0
            db_ref[0, 1] += dl1

    head_out = pl.BlockSpec((1, s, HEAD), lambda j, p: (DIL_GROUP * g + j, 0, 0), pipeline_mode=pl.Buffered(1))
    rows128 = pl.BlockSpec((1, s, 2 * HEAD), lambda j, p: (j, 0, 0), pipeline_mode=pl.Buffered(1))
    full_sh = jax.ShapeDtypeStruct(qn.shape, F32)
    return pl.pallas_call(
        body, name="dil%d_bwd" % g, grid=(DIL_GROUP, steps),
        in_specs=_dil_group_specs(g, s) + [rows128, rows128] + [pl.BlockSpec(memory_space=pl.ANY)] * len(prev),
        out_specs=[head_out, head_out, head_out, pl.BlockSpec((1, 2, QBLK, QBLK), lambda j, p: (j, 0, 0, 0))],
        out_shape=[full_sh, full_sh, full_sh, jax.ShapeDtypeStruct((DIL_GROUP, 2, QBLK, QBLK), F32)],
        input_output_aliases={6 + i: i for i in range(len(prev))},
        compiler_params=_params(("arbitrary", "arbitrary")),
    )(qn, kn, v, bias, ol, dol, *prev)


@functools.partial(jax.custom_vjp, nondiff_argnums=(2,))
def _bdot(a, b, dims):
    return _dot(a.astype(BF16), b.astype(BF16), dims)


def _bdot_fwd(a, b, dims):
    return _bdot(a, b, dims), (a, b)


def _bdot_bwd(dims, res, dc):
    a, b = res
    nn, nt, tn = (NN2, NT2, TN2) if dims in (NN2, NT2, TN2) else (NN3, NT3, TN3)
    if dims == nn:
        return _bdot(dc, b, nt), _bdot(a, dc, tn)
    if dims == nt:
        return _bdot(dc, b, nn), _bdot(dc, a, tn)
    return _bdot(b, dc, nt), _bdot(a, dc, nn)


_bdot.defvjp(_bdot_fwd, _bdot_bwd)


def _ones_dot(ones, x, dims):
    o = ones.astype(BF16)
    hi = x.astype(BF16)
    r1 = x - hi.astype(F32)
    mid = r1.astype(BF16)
    lo = (r1 - mid.astype(F32)).astype(BF16)
    return _dot(o, hi, dims) + _dot(o, mid, dims) + _dot(o, lo, dims)


@jax.custom_vjp
def _prefix_sums(x):
    c = x.shape[1]
    row = lax.broadcasted_iota(jnp.int32, (x.shape[0], c, c), 1)
    col = lax.broadcasted_iota(jnp.int32, (x.shape[0], c, c), 2)
    return _ones_dot((row >= col).astype(F32), x, NN3)


def _prefix_sums_fwd(x):
    return _prefix_sums(x), None


def _prefix_sums_bwd(_, dy):
    c = dy.shape[1]
    row = lax.broadcasted_iota(jnp.int32, (dy.shape[0], c, c), 1)
    col = lax.broadcasted_iota(jnp.int32, (dy.shape[0], c, c), 2)
    return (_ones_dot((row <= col).astype(F32), dy, NN3),)


_prefix_sums.defvjp(_prefix_sums_fwd, _prefix_sums_bwd)


def _rwkv_chunk(s0, r, lw, kraw, v, ag, kk_w, ka_w, rk_w, lng, lnb):
    hb, c, _ = r.shape
    kk = kraw * kk_w
    kk = kk / jnp.maximum(jnp.sqrt(jnp.sum(kk * kk, axis=-1, keepdims=True)), 1e-12)
    k = kraw * (1.0 + (ag - 1.0) * ka_w)
    a = -kk
    b = kk * ag
    row = lax.broadcasted_iota(jnp.int32, (hb, c, c), 1)
    col = lax.broadcasted_iota(jnp.int32, (hb, c, c), 2)
    lower, strict = row >= col, row > col
    cum = _prefix_sums(lw)
    ecum, einv = jnp.exp(cum), jnp.exp(-cum)
    rt, kt, bt = r * ecum, k * einv, b * einv
    at = a * jnp.exp(cum - lw)
    ar = jnp.concatenate([at, rt], axis=1)
    scores = _bdot(ar, jnp.concatenate([bt, kt], axis=1), NT3)
    a_ab = jnp.where(strict, scores[:, :c, :c], 0.0)
    a_ak = jnp.where(strict, scores[:, :c, c:], 0.0)
    p_rb = jnp.where(lower, scores[:, c:, :c], 0.0)
    p_rk = jnp.where(lower, scores[:, c:, c:], 0.0)
    from_s0 = _bdot(ar, s0, NT3)
    rhs = from_s0[:, :c] + _bdot(a_ak, v, NN3)
    inv = (row == col).astype(F32) + a_ab
    pw = a_ab
    for _ in range(int(math.log2(c)) - 1):
        pw = _bdot(pw, pw, NN3)
        inv = inv + _bdot(inv, pw, NN3)
    u = _bdot(inv, rhs, NN3)
    uv = jnp.concatenate([u, v], axis=1)
    y = from_s0[:, c:] + _bdot(jnp.concatenate([p_rb, p_rk], axis=2), uv, NN3)
    cum_end = cum[:, c - 1:c, :]
    dec = jnp.exp(cum_end - cum)
    s_end = s0 * jnp.exp(cum_end) + _bdot(uv, jnp.concatenate([b * dec, k * dec], axis=1), TN3)
    mu = jnp.mean(y, axis=-1, keepdims=True)
    var = jnp.mean(jnp.square(y - mu), axis=-1, keepdims=True)
    z = (y - mu) * lax.rsqrt(var + GN_EPS) * lng + lnb + jnp.sum(r * k * rk_w, axis=-1, keepdims=True) * v
    return z, s_end


def _rwkv_specs(nc, rev):
    cidx = (lambda c: nc - 1 - c) if rev else (lambda c: c)
    seq = pl.BlockSpec((RW_HB, RW_CHUNK, HEAD), lambda hg, c: (hg, cidx(c), 0))
    par = pl.BlockSpec((RW_HB, 1, HEAD), lambda hg, c: (hg, 0, 0))
    st = pl.BlockSpec((1, RW_HB, HEAD, HEAD), lambda hg, c: (cidx(c), hg, 0, 0))
    return seq, par, st


def _rwkv_fwd(seqs, pars):
    s = seqs[0].shape[1]
    nc = s // RW_CHUNK

    def body(*refs):
        seq_refs, par_refs = refs[:5], refs[5:10]
        z_ref, st_ref, state = refs[10:]
        c = pl.program_id(1)

        @pl.when(c == 0)
        def _():
            state[...] = jnp.zeros_like(state)

        s0 = state[...]
        st_ref[0] = s0
        z, s_end = _rwkv_chunk(s0, *[r[...] for r in seq_refs], *[r[...] for r in par_refs])
        z_ref[...] = z
        state[...] = s_end

    seq, par, st = _rwkv_specs(nc, False)
    return pl.pallas_call(
        body, name="rwkv_fwd", grid=(N_HEADS // RW_HB, nc),
        in_specs=[seq] * 5 + [par] * 5, out_specs=[seq, st],
        out_shape=[jax.ShapeDtypeStruct((N_HEADS, s, HEAD), F32), jax.ShapeDtypeStruct((nc, N_HEADS, HEAD, HEAD), F32)],
        scratch_shapes=[pltpu.VMEM((RW_HB, HEAD, HEAD), F32)],
        compiler_params=_params(("arbitrary", "arbitrary")),
    )(*seqs, *pars)


def _rwkv_bwd(seqs, pars, states, dz):
    s = seqs[0].shape[1]
    nc = s // RW_CHUNK

    def body(*refs):
        seq_refs, par_refs = refs[:5], refs[5:10]
        st_ref, dz_ref = refs[10:12]
        dseq_refs, dpar_refs, dstate = refs[12:17], refs[17:22], refs[22]
        c = pl.program_id(1)

        @pl.when(c == 0)
        def _():
            dstate[...] = jnp.zeros_like(dstate)
            for r in dpar_refs:
                r[...] = jnp.zeros_like(r)

        _, vjp = jax.vjp(_rwkv_chunk, st_ref[0], *[r[...] for r in seq_refs], *[r[...] for r in par_refs])
        g = vjp((dz_ref[...], dstate[...]))
        dstate[...] = g[0]
        for r, gs in zip(dseq_refs, g[1:6]):
            r[...] = gs
        for r, gp in zip(dpar_refs, g[6:]):
            r[...] += gp

    seq, par, st = _rwkv_specs(nc, True)
    seq_sh = jax.ShapeDtypeStruct((N_HEADS, s, HEAD), F32)
    par_sh = jax.ShapeDtypeStruct((N_HEADS, 1, HEAD), F32)
    outs = pl.pallas_call(
        body, name="rwkv_bwd", grid=(N_HEADS // RW_HB, nc),
        in_specs=[seq] * 5 + [par] * 5 + [st, seq],
        out_specs=[seq] * 5 + [par] * 5, out_shape=[seq_sh] * 5 + [par_sh] * 5,
        scratch_shapes=[pltpu.VMEM((RW_HB, HEAD, HEAD), F32)],
        compiler_params=_params(("arbitrary", "arbitrary")),
    )(*seqs, *pars, states, dz)
    return list(outs[:5]), list(outs[5:])


def _norm_fn(x, g):
    return (_rms(x, g),)


def _attn_prep_fn(proj, qn_w, kn_w):
    a, b = SB_HEADS, 3 * DIL_GROUP
    return (proj[0:a], proj[a:2 * a], proj[2 * a:3 * a],
            _rms(proj[3 * a:3 * a + b], qn_w), _rms(proj[3 * a + b:3 * a + 2 * b], kn_w), proj[3 * a + 2 * b:])


def _attn_merge_fn(o_sb, ol0, ol1, ol2):
    groups = (ol0, ol1, ol2)
    merged = []
    for j in range(DIL_GROUP):
        lses = [ol[j][:, HEAD:HEAD + 1] for ol in groups]
        m = functools.reduce(jnp.maximum, lses)
        es = [jnp.exp(l - m) for l in lses]
        inv = 1.0 / functools.reduce(jnp.add, es)
        merged.append(functools.reduce(jnp.add, [(e * inv) * ol[j][:, :HEAD] for e, ol in zip(es, groups)]))
    return (jnp.concatenate([_heads_to_nat(o_sb)] + merged, axis=-1),)


def _rw_mix_fn(x, xp, gn, mix, w0, w1, w2, a0, a1, a2, g1, g2):
    h = _rms(x, gn)
    xx = _rms(xp, gn) - h
    xr, xw, xk, xv, xa, xg = [h + xx * mix[i:i + 1] for i in range(6)]
    w_log = -jax.nn.softplus(-(w0 + _mm(jnp.tanh(_mm(xw, w1)), w2))) - 0.5
    lw = -jnp.exp(w_log)
    ag = jax.nn.sigmoid(a0 + _mm(_mm(xa, a1), a2))
    gate = _mm(jax.nn.sigmoid(_mm(xg, g1)), g2)
    return xr, xk, xv, _nat_to_heads(lw), _nat_to_heads(ag), gate


def _rw_gate_fn(z, gate):
    return (_heads_to_nat(z) * gate,)


def _adamw(name, w, m, v, gparts, row0=0, prev=None):
    big_r, c = w.shape
    r = gparts.shape[1]
    tr = r
    if r % 8 == 0:
        tr = max(t for t in range(8, r + 1, 8) if r % t == 0 and (t * c * 4 <= (1 << 20) or t == 8))
    assert row0 % tr == 0 and (r == big_r or r % 8 == 0)
    off = row0 // tr

    def body(w_ref, m_ref, v_ref, g_ref, *rest):
        go_ref, d_ref, mo_ref, vo_ref = rest[-4:]
        g = g_ref[0].astype(F32)
        for j in range(1, N_DEV):
            g = g + g_ref[j].astype(F32)
        mn = ADAM_B1 * m_ref[...] + (1.0 - ADAM_B1) * g
        vn = ADAM_B2 * v_ref[...] + (1.0 - ADAM_B2) * jnp.square(g)
        m_hat = mn / (1.0 - ADAM_B1 ** ADAM_STEP)
        v_hat = vn / (1.0 - ADAM_B2 ** ADAM_STEP)
        go_ref[...] = g
        d_ref[...] = -ADAM_LR * (m_hat / (jnp.sqrt(v_hat) + ADAM_EPS) + ADAM_WD * w_ref[...])
        mo_ref[...] = mn
        vo_ref[...] = vn

    tile = pl.BlockSpec((tr, c), lambda i: (i + off, 0))
    sh = jax.ShapeDtypeStruct((big_r, c), F32)
    prev = list(prev) if prev is not None else []
    return pl.pallas_call(
        body, name=name, grid=(r // tr,),
        in_specs=([tile, tile, tile, pl.BlockSpec((N_DEV, tr, c), lambda i: (0, i, 0))]
                  + [pl.BlockSpec(memory_space=pl.ANY)] * len(prev)),
        out_specs=[tile] * 4, out_shape=[sh] * 4,
        input_output_aliases={4 + j: j for j in range(len(prev))},
        compiler_params=_params(("arbitrary",)),
    )(w, m, v, gparts, *prev)


def _col_blocks_to_nat(g):
    return jnp.moveaxis(g, 0, 1).reshape(g.shape[1], -1)


def _nat_to_col_blocks(a):
    return jnp.moveaxis(a.reshape(a.shape[0], N_DEV, -1), 1, 0)


AG_GROUPS = ("f00", "att", "f01", "f10", "rw", "f11")
RS_GROUPS = ("f11", "rw", "f10", "f01", "f00", "att")
BF16_GRAD_GROUPS = ("att", "f00")
RW_SHARDED = ('rw_mix', 'rw_w0', 'rw_w1', 'rw_w2', 'rw_a0', 'rw_a1', 'rw_a2', 'rw_g1', 'rw_g2', 'rw_kk', 'rw_ka',
              'rw_wr', 'rw_wk', 'rw_wv', 'rw_wo', 'rw_lnx_g', 'rw_lnx_b')


def _step(x, target, rep, get, put):
    tied = lambda a, zero: a + zero[0, 0].astype(a.dtype)
    s, d = x.shape
    tf = min(512, s)
    tt = min(256, s)
    row = lambda a: a.reshape(1, -1)
    mix_norm = rep["mix_norm"]
    ffw = {(0, 0): get("f00", None)}
    ffn_norm = _col_blocks_to_nat(ffw[(0, 0)]["ffn_norm"].reshape(N_DEV, 4, -1))

    acts = {}

    def ffn(nm, xin, l, h):
        g = ffw[(l, h)]
        out, *acts[(l, h)] = _ffn_fwd(nm, xin, ffn_norm[2 * l + h][None], g["gate"], g["up"], g["down"], min(2 * tf, s))
        return out

    x1 = ffn("ffn00_fwd", x, 0, 0)
    att = get("att", x1)
    w_in = _col_blocks_to_nat(att["attn_w_in"])
    w_out = _col_blocks_to_nat(att["attn_w_out"])
    (h0,) = _tile_fwd("mixnorm0_fwd", _norm_fn, [(x1, "nat")], [mix_norm[0:1]], [((s, d), BF16, "nat")], tt)
    proj = _linear_fwd("attn_in_fwd", h0, w_in, tt, out_layout="hm")
    bias = _bias_tiles(rep["rel_bias"], s)
    prep_pars = [rep["attn_q_norm"], rep["attn_k_norm"]]
    sb_sh, dl_sh = (SB_HEADS, s, HEAD), (3 * DIL_GROUP, s, HEAD)
    sq, sk, sv, qn, kn, vd = _tile_fwd("attn_prep_fwd", _attn_prep_fn, [(proj, "hm")], prep_pars,
                                       [(sb_sh, BF16, "hm")] * 3 + [(dl_sh, F32, "hm")] * 3, tt // 2)
    o_sb = _sb_fwd(sq, sk, sv)
    ols = [_dil_group_fwd(g, qn, kn, vd, bias) for g in range(3)]
    merge_tiled = [(o_sb, "hm")] + [(ol, "hm") for ol in ols]
    (merged,) = _tile_fwd("merge_fwd", _attn_merge_fn, merge_tiled, [], [((s, 512), BF16, "nat")], tt)
    x2 = _linear_fwd("attn_out_fwd", merged, w_out, tt, residual=x1)
    ffw[(0, 1)] = get("f01", x2)
    x3 = ffn("ffn01_fwd", x2, 0, 1)
    ffw[(1, 0)] = get("f10", x3)
    x4 = ffn("ffn10_fwd", x3, 1, 0)
    rw = get("rw", x4)
    rw_mix = _col_blocks_to_nat(rw["rw_mix"])
    rw_w1, rw_a1, rw_g1 = (rw[k].reshape(d, -1) for k in ("rw_w1", "rw_a1", "rw_g1"))
    rw_w2, rw_a2, rw_g2 = (_col_blocks_to_nat(rw[k]) for k in ("rw_w2", "rw_a2", "rw_g2"))
    rw_w0, rw_a0 = row(rw["rw_w0"]), row(rw["rw_a0"])
    head_par = lambda a: a.reshape(N_HEADS, 1, HEAD)
    scan_pars = [head_par(rw["rw_kk"]), head_par(rw["rw_ka"]), head_par(rep["rw_rk"]),
                 head_par(rw["rw_lnx_g"]), head_par(rw["rw_lnx_b"])]
    w_rkv = [rw[k].reshape(d, d) for k in ("rw_wr", "rw_wk", "rw_wv")]
    w_o = rw["rw_wo"].reshape(d, d)
    x4p = jnp.pad(x4, ((1, 0), (0, 0)))[:-1]
    mix_tiled = [(x4, "nat"), (x4p, "nat")]
    mix_pars = [mix_norm[1:2], rw_mix, rw_w0, rw_w1, rw_w2, rw_a0, rw_a1, rw_a2, rw_g1, rw_g2]
    hm_sh = (N_HEADS, s, HEAD)
    xr, xk, xv, lw, ag, gate = _tile_fwd(
        "rw_mix_fwd", _rw_mix_fn, mix_tiled, mix_pars,
        [((s, d), BF16, "nat")] * 3 + [(hm_sh, F32, "hm")] * 2 + [((s, d), F32, "nat")], tt)
    r_h, k_h, v_h = [_linear_fwd("rw_%s_fwd" % nm, xi, wi, tt, out_layout="hm")
                     for nm, xi, wi in zip("rkv", (xr, xk, xv), w_rkv)]
    scan_seqs = [r_h, lw, k_h, v_h, ag]
    z, states = _rwkv_fwd(scan_seqs, scan_pars)
    (zg,) = _tile_fwd("rw_gate_fwd", _rw_gate_fn, [(z, "hm"), (gate, "nat")], [], [((s, d), BF16, "nat")], tt)
    x5 = _linear_fwd("rw_out_fwd", zg, w_o, tt, residual=x4)
    ffw[(1, 1)] = get("f11", x5)
    y = ffn("ffn11_fwd", x5, 1, 1)
    dy, loss = _loss_head(y, target, tf)

    G = {}
    dgn = {}

    def fb(nm, group, xin, dout, l, h, zero=None, extra=None):
        g = ffw[(l, h)]
        gn = ffn_norm[2 * l + h][None]
        dxin, dgn[(l, h)], dwg, dwu, dwd = _ffn_bwd(nm, xin, dout, gn if zero is None else tied(gn, zero),
                                                   g["gate"], g["up"], g["down"], *acts[(l, h)], tf)
        shard = {"gate": dwg, "up": dwu, "down": dwd}
        if extra is not None:
            shard.update(extra())
        return dxin, put(group, {}, shard)

    dx5, zero = fb("ffn11_bwd", "f11", x5, dy, 1, 1)
    dzg = _linear_dx("rw_out_dx", dx5, w_o, tt)
    G["rw_wo"] = _linear_dw("rw_out_dw", zg, dx5, tf, 512)
    (dz, dgate), _ = _tile_bwd("rw_gate_bwd", _rw_gate_fn, [(z, "hm"), (gate, "nat")], [], [(dzg, "nat")], tt, [True, True])
    (dr_h, dlw, dk_h, dv_h, dag), dscan = _rwkv_bwd(scan_seqs, [tied(scan_pars[0], zero)] + scan_pars[1:], states, dz)
    drkv = (dr_h, dk_h, dv_h)
    for k, gpar in zip(("rw_kk", "rw_ka", "rw_rk", "rw_lnx_g", "rw_lnx_b"), dscan):
        G[k] = gpar
    dxs = []
    for j, (nm, xi, wi) in enumerate(zip("rkv", (xr, xk, xv), w_rkv)):
        dxs.append(_linear_dx("rw_%s_dx" % nm, drkv[j], wi, tt, dy_layout="hm"))
        G["rw_w" + nm] = _linear_dw("rw_%s_dw" % nm, xi, drkv[j], tf, 512, dy_layout="hm")
    (dx4a, dx4p), dmix = _tile_bwd(
        "rw_mix_bwd", _rw_mix_fn, mix_tiled, mix_pars,
        [(dxs[0], "nat"), (dxs[1], "nat"), (dxs[2], "nat"), (dlw, "hm"), (dag, "hm"), (dgate, "nat")],
        tt, [True, True], adds=[dx5, None])
    d_mixn1 = dmix[0]
    for k, gpar in zip(("rw_mix", "rw_w0", "rw_w1", "rw_w2", "rw_a0", "rw_a1", "rw_a2", "rw_g1", "rw_g2"), dmix[1:]):
        G[k] = gpar
    dx4 = dx4a + jnp.pad(dx4p[1:], ((0, 1), (0, 0)))
    for k in ("rw_mix", "rw_w2", "rw_a2", "rw_g2"):
        G[k] = _nat_to_col_blocks(G[k])
    for k in ("rw_w1", "rw_a1", "rw_g1", "rw_wr", "rw_wk", "rw_wv", "rw_wo"):
        G[k] = G[k].reshape(N_DEV, d // N_DEV, -1)
    for k in ("rw_w0", "rw_a0", "rw_kk", "rw_ka", "rw_lnx_g", "rw_lnx_b"):
        G[k] = G[k].reshape(N_DEV, 1, d // N_DEV)
    zero = put("rw", {"rw_rk": G["rw_rk"].reshape(N_HEADS, HEAD)}, {k: G[k] for k in RW_SHARDED})
    dx3, zero = fb("ffn10_bwd", "f10", x3, dx4, 1, 0, zero)
    dx2, zero = fb("ffn01_bwd", "f01", x2, dx3, 0, 1, zero)
    dmerged = _linear_dx("attn_out_dx", dx2, tied(w_out, zero), tt)
    (do_sb, *dols), _ = _tile_bwd("merge_bwd", _attn_merge_fn, merge_tiled, [], [(dmerged, "nat")], tt, [True] * 4)
    dq_sb, dk_sb, dv_sb = _sb_bwd(sq, sk, sv, do_sb)
    dil_grads, dbias = None, []
    for g in range(3):
        *dil_grads, db = _dil_group_bwd(g, qn, kn, vd, bias, ols[g], dols[g], dil_grads)
        dbias.append(db)
    dqn, dkn, dvd = dil_grads
    dbias = jnp.concatenate(dbias, axis=1)
    (dproj,), (dqn_w, dkn_w) = _tile_bwd(
        "attn_prep_bwd", _attn_prep_fn, [(proj, "hm")], prep_pars,
        [(dq_sb, "hm"), (dk_sb, "hm"), (dv_sb, "hm"), (dqn, "hm"), (dkn, "hm"), (dvd, "hm")], tt // 2, [True])
    dh0 = _linear_dx("attn_in_dx", dproj, w_in, tt, dy_layout="hm")
    (dx1,), (d_mixn0,) = _tile_bwd("mixnorm0_bwd", _norm_fn, [(x1, "nat")], [mix_norm[0:1]], [(dh0, "nat")], tt,
                                   [True], adds=[dx2])
    order = [(0, 0), (0, 1), (1, 0), (1, 1)]
    norm_grads = lambda: {"ffn_norm": _nat_to_col_blocks(jnp.concatenate([dgn[o] for o in order], axis=0))}
    dx0, zero = fb("ffn00_bwd", "f00", x, dx1, 0, 0, extra=norm_grads)
    G["attn_w_out"] = _linear_dw("attn_out_dw", tied(merged, zero), dx2, tf, 512)
    G["attn_w_in"] = _linear_dw("attn_in_dw", tied(h0, zero), dproj, tf, 512, dy_layout="hm")
    rep_grads = {"mix_norm": jnp.concatenate([d_mixn0, d_mixn1], axis=0), "rel_bias": _bias_tiles_bwd(dbias, s),
                 "attn_q_norm": dqn_w, "attn_k_norm": dkn_w}
    zero = put("att", rep_grads, {k: _nat_to_col_blocks(G[k]) for k in ("attn_w_in", "attn_w_out")})
    return loss, dx0, zero


WEIGHTS = ['ffn_norm', 'ffn_w_gate', 'ffn_w_up', 'ffn_w_down', 'mix_norm', 'rel_bias', 'attn_w_in', 'attn_q_norm',
           'attn_k_norm', 'attn_w_out', 'rw_mix', 'rw_w0', 'rw_w1', 'rw_w2', 'rw_a0', 'rw_a1', 'rw_a2', 'rw_g1', 'rw_g2',
           'rw_kk', 'rw_ka', 'rw_rk', 'rw_wr', 'rw_wk', 'rw_wv', 'rw_wo', 'rw_lnx_g', 'rw_lnx_b']
REPLICATED = ('mix_norm', 'rel_bias', 'attn_q_norm', 'attn_k_norm', 'rw_rk')
BF16_WEIGHTS = ('ffn_w_gate', 'ffn_w_up', 'ffn_w_down', 'attn_w_in', 'attn_w_out', 'rw_wr', 'rw_wk', 'rw_wv', 'rw_wo')


def kernel(x, ffn_norm, ffn_w_gate, ffn_w_up, ffn_w_down, mix_norm, rel_bias, attn_w_in, attn_q_norm, attn_k_norm, attn_w_out, rw_mix, rw_w0, rw_w1, rw_w2, rw_a0, rw_a1, rw_a2, rw_g1, rw_g2, rw_kk, rw_ka, rw_rk, rw_wr, rw_wk, rw_wv, rw_wo, rw_lnx_g, rw_lnx_b, loss_target, m_ffn_norm, m_ffn_w_gate, m_ffn_w_up, m_ffn_w_down, m_mix_norm, m_rel_bias, m_attn_w_in, m_attn_q_norm, m_attn_k_norm, m_attn_w_out, m_rw_mix, m_rw_w0, m_rw_w1, m_rw_w2, m_rw_a0, m_rw_a1, m_rw_a2, m_rw_g1, m_rw_g2, m_rw_kk, m_rw_ka, m_rw_rk, m_rw_wr, m_rw_wk, m_rw_wv, m_rw_wo, m_rw_lnx_g, m_rw_lnx_b, v_ffn_norm, v_ffn_w_gate, v_ffn_w_up, v_ffn_w_down, v_mix_norm, v_rel_bias, v_attn_w_in, v_attn_q_norm, v_attn_k_norm, v_attn_w_out, v_rw_mix, v_rw_w0, v_rw_w1, v_rw_w2, v_rw_a0, v_rw_a1, v_rw_a2, v_rw_g1, v_rw_g2, v_rw_kk, v_rw_ka, v_rw_rk, v_rw_wr, v_rw_wk, v_rw_wv, v_rw_wo, v_rw_lnx_g, v_rw_lnx_b):
    args = locals()
    w = {k: args[k] for k in WEIGHTS}
    cast = lambda k, a: a.astype(BF16) if k in BF16_WEIGHTS else a

    sources = {}
    for l, h in ((0, 0), (0, 1), (1, 0), (1, 1)):
        sources["f%d%d" % (l, h)] = {"gate": cast("ffn_w_gate", ffn_w_gate[l, h]), "up": cast("ffn_w_up", ffn_w_up[l, h]),
                                     "down": cast("ffn_w_down", ffn_w_down[l, h])}
    sources["f00"]["ffn_norm"] = ffn_norm
    drop_lead = lambda a: a[0] if a.ndim == 3 else a
    sources["att"] = {k: cast(k, w[k][0]) for k in ("attn_w_in", "attn_w_out")}
    sources["rw"] = {k: cast(k, drop_lead(w[k])) for k in RW_SHARDED}
    ag, token = {}, None
    for group in AG_GROUPS:
        names, arrays = list(sources[group]), list(sources[group].values())
        if token is not None:
            arrays[0] = arrays[0] + token[0, 0].astype(arrays[0].dtype)
        ag[group] = (names, _exchange_start("ag_start_" + group, arrays, []))
        token = ag[group][1]["token"]
    last_ag_token = token

    def get(group, after):
        names, started = ag[group]
        gathered, _ = _exchange_wait("ag_wait_" + group, started, last_ag_token if after is None else after)
        return dict(zip(names, gathered))

    rs = {}

    def put(group, rep_grads, shard_grads):
        if group in BF16_GRAD_GROUPS:
            shard_grads = {k: v.astype(BF16) for k, v in shard_grads.items()}
        started = _exchange_start("rs_start_" + group, list(rep_grads.values()), list(shard_grads.values()))
        rs[group] = (list(rep_grads), list(shard_grads), started)
        return started["token"]

    loss, dx, last_zero = _step(x[0], loss_target[0], {k: w[k] for k in REPLICATED}, get, put)
    loss = lax.psum(loss, MESH_AXES)

    results = {}
    ffn_prev = {}

    def update(k, parts, row0=0, prev=None):
        c = w[k].shape[-1]
        as2d = lambda a: a.reshape(-1, c)
        return _adamw("adamw_%s_%d" % (k, row0), as2d(w[k]), as2d(args["m_" + k]), as2d(args["v_" + k]),
                      parts.reshape(N_DEV, -1, c), row0, prev)

    after = last_zero
    for group in RS_GROUPS:
        rep_names, shard_names, started = rs[group]
        rep_parts, shard_parts = _exchange_wait("rs_wait_" + group, started, after)
        for k, parts in list(zip(rep_names, rep_parts)) + list(zip(shard_names, shard_parts)):
            if k in ("gate", "up", "down"):
                full = "ffn_w_" + k
                piece = 2 * int(group[1]) + int(group[2])
                ffn_prev[full] = update(full, parts, piece * parts.shape[1], ffn_prev.get(full))
                results[full] = ffn_prev[full]
            else:
                results[k] = update(k, parts)
            after = results[k if k in results else "ffn_w_" + k][0]

    outs = [[results[k][j].reshape(w[k].shape) for k in WEIGHTS] for j in range(4)]
    return (loss, dx[None], *outs[0], *outs[1], *outs[2], *outs[3])
```

```python
import functools
import math

import numpy as np
import jax
import jax.numpy as jnp
from jax import lax
from jax.experimental import pallas as pl
from jax.experimental.pallas import tpu as pltpu

F32, BF16 = jnp.float32, jnp.bfloat16
HI = lax.Precision.HIGH

N_DEV = 8
D_MODEL = 1024
HEAD = 64
N_HEADS = 16
SB_HEADS = 4
DIL_GROUP = 4
DIL_PATTERNS = ((128, 1), (512, 4), (2048, 16))
QBLK = 128
N_BUCKETS = 32
MAX_DISTANCE = 2048
NORM_EPS = 1e-6
GN_EPS = 64e-5
NEG_INF = -1e30
RW_CHUNK = 64
RW_HB = 16
ADAM_LR, ADAM_B1, ADAM_B2, ADAM_EPS, ADAM_WD, ADAM_STEP = 0.001, 0.9, 0.999, 1e-08, 0.01, 10
MESH_AXES = ("x", "y", "c")
VMEM_LIMIT_BYTES = 56 * 1024 * 1024

NN2 = (((1,), (0,)), ((), ()))
NT2 = (((1,), (1,)), ((), ()))
TN2 = (((0,), (0,)), ((), ()))
NN3 = (((2,), (1,)), ((0,), (0,)))
NT3 = (((2,), (2,)), ((0,), (0,)))
TN3 = (((1,), (1,)), ((0,), (0,)))


def _dot(a, b, dims=NN2, prec=None):
    return lax.dot_general(a, b, dims, precision=prec, preferred_element_type=F32)


def _params(sem=None):
    return pltpu.CompilerParams(dimension_semantics=sem, vmem_limit_bytes=VMEM_LIMIT_BYTES)


@jax.custom_vjp
def _mm(x, w):
    return _dot(x.astype(BF16), w.astype(BF16))


def _mm_fwd(x, w):
    return _mm(x, w), (x, w)


def _mm_bwd(res, dy):
    x, w = res
    dyb = dy.astype(BF16)
    return (_dot(dyb, w.astype(BF16), NT2).astype(x.dtype), _dot(x.astype(BF16), dyb, TN2).astype(w.dtype))


_mm.defvjp(_mm_fwd, _mm_bwd)


def _rms(x, g):
    return x * lax.rsqrt(jnp.mean(x * x, axis=-1, keepdims=True) + NORM_EPS) * g


def _log_sigmoid(z):
    return jnp.minimum(z, 0.0) - jnp.log(1.0 + jnp.exp(-jnp.abs(z)))


def _heads_to_nat(v3):
    return jnp.concatenate([v3[h] for h in range(v3.shape[0])], axis=-1)


def _nat_to_heads(v2):
    return jnp.stack([v2[:, h * HEAD:(h + 1) * HEAD] for h in range(v2.shape[1] // HEAD)], axis=0)


def _exchange(name, gathers, scatters):
    n_g = len(gathers)
    arrays = list(gathers) + list(scatters)
    n = len(arrays)
    out_shape = [jax.ShapeDtypeStruct((N_DEV,) + a.shape, a.dtype) for a in gathers]
    out_shape += [jax.ShapeDtypeStruct(a.shape, a.dtype) for a in scatters]

    def body(*refs):
        ins, outs = refs[:n], refs[n:2 * n]
        send_sems, recv_sems, local_sems = refs[2 * n:]
        x, y, c = lax.axis_index("x"), lax.axis_index("y"), lax.axis_index("c")
        me = 4 * x + 2 * y + c

        def src(i, idx):
            return ins[i] if i < n_g else ins[i].at[idx]

        local = [pltpu.make_async_copy(src(i, me), outs[i].at[me], local_sems.at[i]) for i in range(n)]
        for cp in local:
            cp.start()
        remote = []
        for m in range(1, N_DEV):
            px, py, pc = x ^ ((m >> 2) & 1), y ^ ((m >> 1) & 1), c ^ (m & 1)
            peer = 4 * px + 2 * py + pc
            for i in range(n):
                cp = pltpu.make_async_remote_copy(
                    src_ref=src(i, peer), dst_ref=outs[i].at[me],
                    send_sem=send_sems.at[i, m - 1], recv_sem=recv_sems.at[i, m - 1],
                    device_id=(px, py, pc), device_id_type=pl.DeviceIdType.MESH)
                cp.start()
                arrival = pltpu.make_async_remote_copy(
                    src_ref=src(i, peer), dst_ref=outs[i].at[peer],
                    send_sem=send_sems.at[i, m - 1], recv_sem=recv_sems.at[i, m - 1],
                    device_id=(px, py, pc), device_id_type=pl.DeviceIdType.MESH)
                remote.append((cp, arrival))
        for cp, arrival in remote:
            cp.wait_send()
            arrival.wait_recv()
        for cp in local:
            cp.wait()

    hbm = pl.BlockSpec(memory_space=pltpu.HBM)
    outs = pl.pallas_call(
        body, name=name, out_shape=out_shape,
        in_specs=[hbm] * n, out_specs=[hbm] * n,
        scratch_shapes=[pltpu.SemaphoreType.DMA((n, N_DEV - 1)), pltpu.SemaphoreType.DMA((n, N_DEV - 1)),
                        pltpu.SemaphoreType.DMA((n,))],
    )(*arrays)
    return list(outs[:n_g]), list(outs[n_g:])


def _mesh_peers():
    x, y, c = lax.axis_index("x"), lax.axis_index("y"), lax.axis_index("c")
    peers = []
    for m in range(1, N_DEV):
        px, py, pc = x ^ ((m >> 2) & 1), y ^ ((m >> 1) & 1), c ^ (m & 1)
        peers.append((m, (px, py, pc), 4 * px + 2 * py + pc))
    return 4 * x + 2 * y + c, peers


_HBM_SPEC = pl.BlockSpec(memory_space=pltpu.HBM)
_SEM_SPEC = pl.BlockSpec(memory_space=pltpu.SEMAPHORE)
_DATAFLOW = pltpu.SideEffectType.DATAFLOW_SIDE_EFFECTING


def _exchange_start(name, gathers, scatters):
    n_g = len(gathers)
    arrays = list(gathers) + list(scatters)
    n = len(arrays)
    lands = ([lax.empty((N_DEV,) + a.shape, a.dtype) for a in gathers] + [lax.empty(a.shape, a.dtype) for a in scatters])

    def body(*refs):
        ins, land = refs[:n], refs[n:2 * n]
        send_sems, recv_sems, local_sems, token = refs[2 * n], refs[2 * n + 1], refs[2 * n + 2], refs[-1]
        me, peers = _mesh_peers()
        for m, dev, peer in peers:
            for i in range(n):
                k = i * (N_DEV - 1) + m - 1
                pltpu.make_async_remote_copy(
                    src_ref=ins[i] if i < n_g else ins[i].at[peer], dst_ref=land[i].at[me],
                    send_sem=send_sems.at[k], recv_sem=recv_sems.at[k],
                    device_id=dev, device_id_type=pl.DeviceIdType.MESH).start()
        for i in range(n):
            pltpu.make_async_copy(ins[i] if i < n_g else ins[i].at[me], land[i].at[me], local_sems.at[i]).start()
        token[...] = jnp.zeros_like(token)

    sem = pltpu.SemaphoreType.DMA((n * (N_DEV - 1),))
    outs = pl.pallas_call(
        body, name=name,
        out_shape=([sem, sem, pltpu.SemaphoreType.DMA((n,))] + [pltpu.HBM(a.shape, a.dtype) for a in arrays]
                   + [pltpu.HBM(l.shape, l.dtype) for l in lands] + [jax.ShapeDtypeStruct((8, 128), F32)]),
        in_specs=[_HBM_SPEC] * (2 * n),
        out_specs=[_SEM_SPEC] * 3 + [_HBM_SPEC] * (2 * n) + [pl.BlockSpec(memory_space=pltpu.VMEM)],
        input_output_aliases={i: i + 3 for i in range(2 * n)},
        compiler_params=pltpu.CompilerParams(has_side_effects=_DATAFLOW),
    )(*[pltpu.with_memory_space_constraint(a, pltpu.HBM) for a in arrays],
      *[pltpu.with_memory_space_constraint(l, pltpu.HBM) for l in lands])
    return dict(n_g=n_g, n=n, send=outs[0], recv=outs[1], local=outs[2], srcs=list(outs[3:3 + n]),
                lands=list(outs[3 + n:3 + 2 * n]), token=outs[-1])


def _exchange_wait(name, started, after):
    n, n_g = started["n"], started["n_g"]

    def body(*refs):
        srcs, lands = refs[:n], refs[n:2 * n]
        send_sems, recv_sems, local_sems = refs[2 * n], refs[2 * n + 1], refs[2 * n + 2]
        me, peers = _mesh_peers()
        local = [pltpu.make_async_copy(srcs[i] if i < n_g else srcs[i].at[me], lands[i].at[me], local_sems.at[i])
                 for i in range(n)]
        for m, dev, peer in peers:
            for i in range(n):
                k = i * (N_DEV - 1) + m - 1
                cp = pltpu.make_async_remote_copy(
                    src_ref=srcs[i] if i < n_g else srcs[i].at[peer], dst_ref=lands[i].at[peer],
                    send_sem=send_sems.at[k], recv_sem=recv_sems.at[k],
                    device_id=dev, device_id_type=pl.DeviceIdType.MESH)
                cp.wait_send()
                cp.wait_recv()
        for cp in local:
            cp.wait()

    outs = pl.pallas_call(
        body, name=name,
        out_shape=([pltpu.HBM(a.shape, a.dtype) for a in started["srcs"]]
                   + [pltpu.HBM(l.shape, l.dtype) for l in started["lands"]]),
        in_specs=[_HBM_SPEC] * (2 * n) + [_SEM_SPEC] * 3 + [pl.BlockSpec(memory_space=pl.ANY)],
        out_specs=[_HBM_SPEC] * (2 * n), input_output_aliases={i: i for i in range(2 * n)},
        compiler_params=pltpu.CompilerParams(has_side_effects=_DATAFLOW),
    )(*started["srcs"], *started["lands"], started["send"], started["recv"], started["local"], after)
    return list(outs[n:n + n_g]), list(outs[n + n_g:])


def _tile_spec(shape, layout, t):
    if layout == "nat":
        return pl.BlockSpec((t, shape[1]), lambda i: (i, 0))
    return pl.BlockSpec((shape[0], t, shape[2]), lambda i: (0, i, 0))


def _full_spec(shape):
    nd = len(shape)
    return pl.BlockSpec(tuple(shape), lambda i: (0,) * nd)


def _seq_len(a, layout):
    return a.shape[0] if layout == "nat" else a.shape[1]


def _tile_fwd(name, f, tiled, params, outs, t):
    nt, npar = len(tiled), len(params)
    s = _seq_len(*tiled[0])

    def body(*refs):
        vals = [r[...] for r in refs[:nt + npar]]
        res = f(*vals)
        for r, o in zip(refs[nt + npar:], res):
            r[...] = o.astype(r.dtype)

    return pl.pallas_call(
        body, name=name, grid=(s // t,),
        in_specs=[_tile_spec(a.shape, l, t) for a, l in tiled] + [_full_spec(p.shape) for p in params],
        out_specs=[_tile_spec(sh, l, t) for sh, _, l in outs],
        out_shape=[jax.ShapeDtypeStruct(sh, dt) for sh, dt, _ in outs],
        compiler_params=_params(("arbitrary",)),
    )(*[a for a, _ in tiled], *params)


def _tile_bwd(name, f, tiled, params, cts, t, need, adds=None):
    nt, npar, nc = len(tiled), len(params), len(cts)
    s = _seq_len(*tiled[0])
    need_idx = [k for k in range(nt) if need[k]]
    adds = adds or [None] * len(need_idx)
    add_arrays = [(a, tiled[k][1]) for a, k in zip(adds, need_idx) if a is not None]
    n_add = len(add_arrays)

    def body(*refs):
        i = pl.program_id(0)
        vals = [r[...] for r in refs[:nt + npar]]
        ct_refs = refs[nt + npar:nt + npar + nc]
        add_refs = refs[nt + npar + nc:nt + npar + nc + n_add]
        out_refs = refs[nt + npar + nc + n_add:]
        res, vjp = jax.vjp(f, *vals)
        grads = vjp(tuple(r[...].astype(o.dtype) for r, o in zip(ct_refs, res)))
        a = 0
        for j, k in enumerate(need_idx):
            g = grads[k]
            if adds[j] is not None:
                g = g + add_refs[a][...]
                a += 1
            out_refs[j][...] = g.astype(out_refs[j].dtype)
        for j in range(npar):
            r = out_refs[len(need_idx) + j]

            @pl.when(i == 0)
            def _():
                r[...] = jnp.zeros_like(r)

            r[...] += grads[nt + j]

    outs = pl.pallas_call(
        body, name=name, grid=(s // t,),
        in_specs=([_tile_spec(a.shape, l, t) for a, l in tiled] + [_full_spec(p.shape) for p in params]
                  + [_tile_spec(a.shape, l, t) for a, l in cts] + [_tile_spec(a.shape, l, t) for a, l in add_arrays]),
        out_specs=([_tile_spec(tiled[k][0].shape, tiled[k][1], t) for k in need_idx]
                   + [_full_spec(p.shape) for p in params]),
        out_shape=([jax.ShapeDtypeStruct(tiled[k][0].shape, F32) for k in need_idx]
                   + [jax.ShapeDtypeStruct(p.shape, F32) for p in params]),
        compiler_params=_params(("arbitrary",)),
    )(*[a for a, _ in tiled], *params, *[a for a, _ in cts], *[a for a, _ in add_arrays])
    return list(outs[:len(need_idx)]), list(outs[len(need_idx):])


def _linear_fwd(name, x, w, t, out_layout="nat", residual=None):
    s, k = x.shape
    n = w.shape[1]
    has_res = residual is not None

    def body(*refs):
        x_ref, w_ref = refs[0], refs[1]
        o_ref = refs[-1]
        y = _dot(x_ref[...].astype(BF16), w_ref[...])
        if has_res:
            y = y + refs[2][...]
        if out_layout == "hm":
            for h in range(n // HEAD):
                o_ref[h] = y[:, h * HEAD:(h + 1) * HEAD]
        else:
            o_ref[...] = y

    out_sh = (s, n) if out_layout == "nat" else (n // HEAD, s, HEAD)
    ins = [x, w] + ([residual] if has_res else [])
    in_specs = [_tile_spec(x.shape, "nat", t), _full_spec(w.shape)] + ([_tile_spec((s, n), "nat", t)] if has_res else [])
    return pl.pallas_call(
        body, name=name, grid=(s // t,), in_specs=in_specs,
        out_specs=_tile_spec(out_sh, out_layout, t), out_shape=jax.ShapeDtypeStruct(out_sh, F32),
        compiler_params=_params(("arbitrary",)),
    )(*ins)


def _linear_dx(name, dy, w, t, dy_layout="nat"):
    k, n = w.shape
    s = _seq_len(dy, dy_layout)

    def body(dy_ref, w_ref, o_ref):
        dy = _heads_to_nat(dy_ref[...].astype(BF16)) if dy_layout == "hm" else dy_ref[...].astype(BF16)
        o_ref[...] = _dot(dy, w_ref[...], NT2)

    return pl.pallas_call(
        body, name=name, grid=(s // t,),
        in_specs=[_tile_spec(dy.shape, dy_layout, t), _full_spec(w.shape)],
        out_specs=_tile_spec((s, k), "nat", t), out_shape=jax.ShapeDtypeStruct((s, k), F32),
        compiler_params=_params(("arbitrary",)),
    )(dy, w)


def _linear_dw(name, x, dy, t, nb, dy_layout="nat"):
    s, k = x.shape
    n = dy.shape[1] if dy_layout == "nat" else dy.shape[0] * HEAD

    def body(x_ref, dy_ref, o_ref):
        i = pl.program_id(1)

        @pl.when(i == 0)
        def _():
            o_ref[...] = jnp.zeros_like(o_ref)

        dy = _heads_to_nat(dy_ref[...].astype(BF16)) if dy_layout == "hm" else dy_ref[...].astype(BF16)
        o_ref[...] += _dot(x_ref[...].astype(BF16), dy, TN2)

    if dy_layout == "hm":
        dy_spec = pl.BlockSpec((nb // HEAD, t, HEAD), lambda j, i: (j, i, 0))
    else:
        dy_spec = pl.BlockSpec((t, nb), lambda j, i: (i, j))
    return pl.pallas_call(
        body, name=name, grid=(n // nb, s // t),
        in_specs=[pl.BlockSpec((t, k), lambda j, i: (i, 0)), dy_spec],
        out_specs=pl.BlockSpec((k, nb), lambda j, i: (0, j)), out_shape=jax.ShapeDtypeStruct((k, n), F32),
        compiler_params=_params(("arbitrary", "arbitrary")),
    )(x, dy)


def _ffn_fwd(name, x, gn, wg, wu, wd, t):
    s, d = x.shape
    f8 = wg.shape[-1]

    def body(x_ref, g_ref, wg_ref, wu_ref, wd_ref, o_ref, gk_ref, uk_ref, h_scr, acc):
        k = pl.program_id(1)

        @pl.when(k == 0)
        def _():
            h_scr[...] = _rms(x_ref[...], g_ref[...]).astype(BF16)
            acc[...] = jnp.zeros_like(acc)

        hb = h_scr[...]
        gk = _dot(hb, wg_ref[0])
        uk = _dot(hb, wu_ref[0])
        gk_ref[0] = gk
        uk_ref[0] = uk
        a = gk * jax.nn.sigmoid(gk) * uk
        acc[...] += _dot(a.astype(BF16), wd_ref[0])

        @pl.when(k == N_DEV - 1)
        def _():
            o_ref[...] = x_ref[...] + 0.5 * acc[...]

    wspec = lambda shp: pl.BlockSpec((1,) + shp, lambda i, k: (k, 0, 0))
    act = pl.BlockSpec((1, t, f8), lambda i, k: (k, i, 0))
    act_sh = jax.ShapeDtypeStruct((N_DEV, s, f8), F32)
    return pl.pallas_call(
        body, name=name, grid=(s // t, N_DEV),
        in_specs=[pl.BlockSpec((t, d), lambda i, k: (i, 0)), pl.BlockSpec((1, d), lambda i, k: (0, 0)),
                  wspec((d, f8)), wspec((d, f8)), wspec((f8, d))],
        out_specs=[pl.BlockSpec((t, d), lambda i, k: (i, 0)), act, act],
        out_shape=[jax.ShapeDtypeStruct((s, d), F32), act_sh, act_sh],
        scratch_shapes=[pltpu.VMEM((t, d), BF16), pltpu.VMEM((t, d), F32)],
        compiler_params=_params(("arbitrary", "arbitrary")),
    )(x, gn, wg, wu, wd)


def _ffn_bwd(name, x, dy, gn, wg, wu, wd, gact, uact, t):
    s, d = x.shape
    f8 = wg.shape[-1]
    last = N_DEV - 1

    def body(x_ref, dy_ref, g_ref, wg_ref, wu_ref, wd_ref, gk_ref, uk_ref,
             dx_ref, dg_ref, dwg_ref, dwu_ref, dwd_ref, dh_scr):
        k, i = pl.program_id(0), pl.program_id(1)
        x = x_ref[...]
        rs = lax.rsqrt(jnp.mean(x * x, axis=-1, keepdims=True) + NORM_EPS)
        xn = x * rs
        hb = (xn * g_ref[...]).astype(BF16)
        dob = (0.5 * dy_ref[...]).astype(BF16)
        wgk, wuk, wdk = wg_ref[0], wu_ref[0], wd_ref[0]
        gk, uk = gk_ref[0], uk_ref[0]
        sg = jax.nn.sigmoid(gk)
        sk = gk * sg
        da = _dot(dob, wdk, NT2)
        du = (da * sk).astype(BF16)
        dg = (da * uk * (sg * (1.0 + gk * (1.0 - sg)))).astype(BF16)

        @pl.when(i == 0)
        def _():
            dwg_ref[...] = jnp.zeros_like(dwg_ref)
            dwu_ref[...] = jnp.zeros_like(dwu_ref)
            dwd_ref[...] = jnp.zeros_like(dwd_ref)

        dwd_ref[0] += _dot((sk * uk).astype(BF16), dob, TN2)
        dwg_ref[0] += _dot(hb, dg, TN2)
        dwu_ref[0] += _dot(hb, du, TN2)
        dh = _dot(dg, wgk, NT2) + _dot(du, wuk, NT2)
        rows = pl.ds(pl.multiple_of(i * t, t), t)

        @pl.when(k == 0)
        def _():
            dh_scr[rows, :] = dh

        @pl.when(k > 0)
        def _():
            dh_scr[rows, :] += dh

        @pl.when(jnp.logical_and(k == last, i == 0))
        def _():
            dg_ref[...] = jnp.zeros_like(dg_ref)

        @pl.when(k == last)
        def _():
            dht = dh_scr[rows, :]
            dg_ref[...] += jnp.sum(dht * xn, axis=0, keepdims=True)
            dxn = dht * g_ref[...]
            dx_ref[...] = dy_ref[...] + rs * (dxn - xn * jnp.mean(dxn * xn, axis=-1, keepdims=True))

    wspec = lambda shp: pl.BlockSpec((1,) + shp, lambda k, i: (k, 0, 0))
    tile = pl.BlockSpec((t, d), lambda k, i: (i, 0))
    act = pl.BlockSpec((1, t, f8), lambda k, i: (k, i, 0))
    return pl.pallas_call(
        body, name=name, grid=(N_DEV, s // t),
        in_specs=[tile, tile, pl.BlockSpec((1, d), lambda k, i: (0, 0)), wspec((d, f8)), wspec((d, f8)), wspec((f8, d)),
                  act, act],
        out_specs=[pl.BlockSpec((t, d), lambda k, i: (jnp.where(k == last, i, 0), 0)),
                   pl.BlockSpec((1, d), lambda k, i: (0, 0)),
                   pl.BlockSpec((1, d, f8), lambda k, i: (k, 0, 0)), pl.BlockSpec((1, d, f8), lambda k, i: (k, 0, 0)),
                   pl.BlockSpec((1, f8, d), lambda k, i: (k, 0, 0))],
        out_shape=[jax.ShapeDtypeStruct((s, d), F32), jax.ShapeDtypeStruct((1, d), F32),
                   jax.ShapeDtypeStruct((N_DEV, d, f8), F32), jax.ShapeDtypeStruct((N_DEV, d, f8), F32),
                   jax.ShapeDtypeStruct((N_DEV, f8, d), F32)],
        scratch_shapes=[pltpu.VMEM((s, d), F32)],
        compiler_params=_params(("arbitrary", "arbitrary")),
    )(x, dy, gn, wg, wu, wd, gact, uact)


def _loss_head(y, target, t):
    s, d = y.shape

    def body(y_ref, t_ref, dy_ref, l_ref):
        i = pl.program_id(0)
        err = y_ref[...] - t_ref[...]
        dy_ref[...] = err * (1.0 / d)

        @pl.when(i == 0)
        def _():
            l_ref[...] = jnp.zeros_like(l_ref)

        l_ref[...] += 0.5 * jnp.sum(jnp.mean(err * err, axis=-1, keepdims=True), axis=0, keepdims=True)

    tile = pl.BlockSpec((t, d), lambda i: (i, 0))
    dy, l = pl.pallas_call(
        body, name="loss_head", grid=(s // t,), in_specs=[tile, tile],
        out_specs=[tile, pl.BlockSpec((1, 1), lambda i: (0, 0))],
        out_shape=[jax.ShapeDtypeStruct((s, d), F32), jax.ShapeDtypeStruct((1, 1), F32)],
        compiler_params=_params(("arbitrary",)),
    )(y, target)
    return dy, l[0, 0]


SB_KEY_TILE = 1024
SB_HEADS_PER_STEP = 4


def _sb_scan_mats():
    row = lax.broadcasted_iota(jnp.int32, (QBLK, QBLK), 0)
    col = lax.broadcasted_iota(jnp.int32, (QBLK, QBLK), 1)
    return (row > col).astype(F32).astype(BF16), (row < col).astype(F32).astype(BF16)


def _sb_tile_scan(x, mat, reverse):
    nsub = x.shape[1] // QBLK
    outs, carry = [None] * nsub, jnp.zeros((x.shape[0], 1), F32)
    for i in (reversed(range(nsub)) if reverse else range(nsub)):
        xs = x[:, i * QBLK:(i + 1) * QBLK]
        hi = xs.astype(BF16)
        lo = (xs - hi.astype(F32)).astype(BF16)
        outs[i] = _dot(hi, mat) + _dot(lo, mat) + carry
        carry = carry + jnp.sum(xs, axis=1, keepdims=True)
    return jnp.concatenate(outs, axis=1), carry


def _sb_before_query(n, t, kt):
    row = lax.broadcasted_iota(jnp.int32, (QBLK, kt), 0)
    col = lax.broadcasted_iota(jnp.int32, (QBLK, kt), 1)
    return t * kt + col < n * QBLK + row


def _sb_fwd(q, k, v):
    _, s, _ = q.shape
    scale = HEAD ** -0.5
    kt = min(SB_KEY_TILE, s)

    def body(q_ref, k_ref, v_ref, o_ref):
        n = pl.program_id(1)
        suffix, _ = _sb_scan_mats()
        n_tiles = lax.div(n, jnp.int32(kt // QBLK)) + 1
        heads = range(SB_HEADS_PER_STEP)
        qb = [(q_ref[h] * scale).astype(q_ref.dtype) for h in heads]

        def tile(t, carry, diagonal):
            rows = pl.ds(pl.multiple_of(t * kt, kt), kt)
            out = []
            for h in heads:
                c, acc = carry[h]
                z = _dot(qb[h], k_ref[h, rows, :], NT2)
                lk = _log_sigmoid(-z)
                log_beta = z + lk
                if diagonal:
                    ok = _sb_before_query(n, t, kt)
                    lk = jnp.where(ok, lk, 0.0)
                later, total = _sb_tile_scan(lk, suffix, True)
                w = jnp.exp(log_beta + later + c)
                if diagonal:
                    w = jnp.where(ok, w, 0.0)
                out.append((c + total, acc + _dot(w.astype(BF16), v_ref[h, rows, :])))
            return tuple(out)

        zero = (jnp.zeros((QBLK, 1), F32), jnp.zeros((QBLK, HEAD), F32))
        carry = tile(n_tiles - 1, (zero,) * len(heads), True)
        carry = lax.fori_loop(1, n_tiles, lambda jj, cr: tile(n_tiles - 1 - jj, cr, False), carry)
        for h in heads:
            o_ref[h] = carry[h][1]

    hp = SB_HEADS_PER_STEP
    return pl.pallas_call(
        body, name="sb_fwd", grid=(SB_HEADS // hp, s // QBLK),
        in_specs=[pl.BlockSpec((hp, QBLK, HEAD), lambda h, n: (h, n, 0)),
                  pl.BlockSpec((hp, s, HEAD), lambda h, n: (h, 0, 0)),
                  pl.BlockSpec((hp, s, HEAD), lambda h, n: (h, 0, 0))],
        out_specs=pl.BlockSpec((hp, QBLK, HEAD), lambda h, n: (h, n, 0)),
        out_shape=jax.ShapeDtypeStruct((SB_HEADS, s, HEAD), F32),
        compiler_params=_params(("arbitrary", "arbitrary")),
    )(q, k, v)


def _sb_bwd(q, k, v, do):
    _, s, _ = q.shape
    scale = HEAD ** -0.5
    kt = min(SB_KEY_TILE, s)

    def body(q_ref, k_ref, v_ref, do_ref, dq_ref, dk_ref, dv_ref, e_scr, beta_scr):
        n = pl.program_id(1)

        @pl.when(n == 0)
        def _():
            dk_ref[...] = jnp.zeros_like(dk_ref)
            dv_ref[...] = jnp.zeros_like(dv_ref)

        suffix, prefix = _sb_scan_mats()
        n_tiles = lax.div(n, jnp.int32(kt // QBLK)) + 1
        heads = range(SB_HEADS_PER_STEP)
        qb = [(q_ref[h] * scale).astype(q_ref.dtype) for h in heads]
        dob = [do_ref[h].astype(BF16) for h in heads]

        def weights(t, cs, diagonal):
            rows = pl.ds(pl.multiple_of(t * kt, kt), kt)
            out = []
            for h in heads:
                vb = v_ref[h, rows, :]
                z = _dot(qb[h], k_ref[h, rows, :], NT2)
                lk = _log_sigmoid(-z)
                log_beta = z + lk
                if diagonal:
                    ok = _sb_before_query(n, t, kt)
                    lk = jnp.where(ok, lk, 0.0)
                later, total = _sb_tile_scan(lk, suffix, True)
                w = jnp.exp(log_beta + later + cs[h])
                if diagonal:
                    w = jnp.where(ok, w, 0.0)
                e_scr[h, t] = w * _dot(dob[h], vb, NT2)
                beta_scr[h, t] = jnp.exp(log_beta)
                dv_ref[h, rows, :] += _dot(w.astype(BF16), dob[h], TN2)
                out.append(cs[h] + total)
            return tuple(out)

        col0 = jnp.zeros((QBLK, 1), F32)
        cs = weights(n_tiles - 1, (col0,) * len(heads), True)
        lax.fori_loop(1, n_tiles, lambda jj, c: weights(n_tiles - 1 - jj, c, False), cs)

        def grads(t, carry, diagonal):
            rows = pl.ds(pl.multiple_of(t * kt, kt), kt)
            out = []
            for h in heads:
                pc, dq = carry[h]
                kb = k_ref[h, rows, :]
                e, beta = e_scr[h, t], beta_scr[h, t]
                before, total = _sb_tile_scan(e, prefix, False)
                dz = e * (1.0 - beta) - beta * (before + pc)
                if diagonal:
                    dz = jnp.where(_sb_before_query(n, t, kt), dz, 0.0)
                dz = dz.astype(BF16)
                dk_ref[h, rows, :] += _dot(dz, qb[h], TN2)
                out.append((pc + total, dq + _dot(dz, kb)))
            return tuple(out)

        zero = (col0, jnp.zeros((QBLK, HEAD), F32))
        carry = lax.fori_loop(0, n_tiles - 1, lambda t, cr: grads(t, cr, False), (zero,) * len(heads))
        carry = grads(n_tiles - 1, carry, True)
        for h in heads:
            dq_ref[h] = carry[h][1] * scale

    hp = SB_HEADS_PER_STEP
    qspec = pl.BlockSpec((hp, QBLK, HEAD), lambda h, n: (h, n, 0))
    full = pl.BlockSpec((hp, s, HEAD), lambda h, n: (h, 0, 0))
    sh = jax.ShapeDtypeStruct((SB_HEADS, s, HEAD), F32)
    tiles_sh = (hp, s // kt, QBLK, kt)
    return pl.pallas_call(
        body, name="sb_bwd", grid=(SB_HEADS // hp, s // QBLK),
        in_specs=[qspec, full, full, qspec],
        out_specs=[qspec, full, full], out_shape=[sh, sh, sh],
        scratch_shapes=[pltpu.VMEM(tiles_sh, F32), pltpu.VMEM(tiles_sh, F32)],
        compiler_params=_params(("arbitrary", "arbitrary")),
    )(q, k, v, do)


def _t5_bucket_np(dist):
    max_exact = N_BUCKETS // 2
    d = np.maximum(dist, 1).astype(np.float32)
    large = max_exact + (np.log(d / np.float32(max_exact)) / np.float32(math.log(MAX_DISTANCE / max_exact))
                         * np.float32(N_BUCKETS - max_exact)).astype(np.int32)
    large = np.minimum(large, N_BUCKETS - 1)
    return np.where(dist < max_exact, dist, large)


def _dil_layout(s):
    assert all(s % (QBLK * r) == 0 and window // r == QBLK for window, r in DIL_PATTERNS)
    tiles, buckets = [], []
    i = np.arange(QBLK)[:, None]
    j = np.arange(QBLK)[None, :]
    for g, (window, r) in enumerate(DIL_PATTERNS):
        for off in (0, 1):
            dist = QBLK * off + i - j
            ok = (dist >= 0) & (dist <= window // r)
            tiles.append((g, off))
            buckets.append(np.where(ok, _t5_bucket_np(np.maximum(dist, 0) * r), -1).astype(np.int32))
    return tiles, np.stack(buckets)


def _bias_tiles(rel_bias, s):
    tiles, buckets = _dil_layout(s)
    nt = len(tiles)
    present = [sorted(set(np.unique(buckets[k]).tolist()) - {-1}) for k in range(nt)]

    def body(rel_ref, b_ref, o_ref):
        j = pl.program_id(0)
        for k, (g, _) in enumerate(tiles):
            bk = b_ref[k]
            tile = jnp.full((QBLK, QBLK), NEG_INF, F32)
            for b in present[k]:
                tile = jnp.where(bk == b, rel_ref[b, g * DIL_GROUP + j], tile)
            o_ref[0, k] = tile

    return pl.pallas_call(
        body, name="bias_tiles", grid=(DIL_GROUP,),
        in_specs=[pl.BlockSpec(memory_space=pltpu.SMEM), pl.BlockSpec((nt, QBLK, QBLK), lambda j: (0, 0, 0))],
        out_specs=pl.BlockSpec((1, nt, QBLK, QBLK), lambda j: (j, 0, 0, 0)),
        out_shape=jax.ShapeDtypeStruct((DIL_GROUP, nt, QBLK, QBLK), F32),
        compiler_params=_params(("arbitrary",)),
    )(rel_bias, jnp.asarray(buckets))


def _bias_tiles_bwd(dbias, s):
    tiles, buckets = _dil_layout(s)
    nt = len(tiles)
    present = [sorted(set(np.unique(buckets[k]).tolist()) - {-1}) for k in range(nt)]

    def body(d_ref, b_ref, o_ref):
        j = pl.program_id(0)

        @pl.when(j == 0)
        def _():
            for b in range(N_BUCKETS):
                for col in range(3 * DIL_GROUP):
                    o_ref[b, col] = jnp.float32(0.0)

        for k, (g, _) in enumerate(tiles):
            bk, dk = b_ref[k], d_ref[0, k]
            for b in present[k]:
                o_ref[b, g * DIL_GROUP + j] += jnp.sum(jnp.where(bk == b, dk, 0.0))

    return pl.pallas_call(
        body, name="bias_tiles_bwd", grid=(DIL_GROUP,),
        in_specs=[pl.BlockSpec((1, nt, QBLK, QBLK), lambda j: (j, 0, 0, 0)),
                  pl.BlockSpec((nt, QBLK, QBLK), lambda j: (0, 0, 0))],
        out_specs=pl.BlockSpec(memory_space=pltpu.SMEM),
        out_shape=jax.ShapeDtypeStruct((N_BUCKETS, 3 * DIL_GROUP), F32),
        compiler_params=_params(("arbitrary",)),
    )(dbias, jnp.asarray(buckets))


DIL_PAIRS_PER_STEP = 4


def _dil_rows(g, s, pair):
    _, r = DIL_PATTERNS[g]
    nb = s // (QBLK * r)
    c, n = lax.div(pair, jnp.int32(nb)), lax.rem(pair, jnp.int32(nb))
    start = c + (r * QBLK) * n
    before = jnp.where(n > 0, start - r * QBLK, start)
    if r == 1:
        return pl.ds(start, QBLK), pl.ds(before, QBLK), n > 0
    return pl.ds(start, QBLK, stride=r), pl.ds(before, QBLK, stride=r), n > 0


def _dil_logits(qb, k_ref, rows, before, has_before, b_ref):
    k0, k1 = k_ref[0, rows, :].astype(BF16), k_ref[0, before, :].astype(BF16)
    l0 = _dot(qb, k0, NT2) + b_ref[0, 0]
    l1 = jnp.where(has_before, _dot(qb, k1, NT2) + b_ref[0, 1], NEG_INF)
    return k0, k1, l0, l1


def _dil_group_specs(g, s):
    head = pl.BlockSpec((1, s, HEAD), lambda j, p: (DIL_GROUP * g + j, 0, 0))
    return [head, head, head, pl.BlockSpec((1, 2, QBLK, QBLK), lambda j, p: (j, g, 0, 0))]


def _dil_group_fwd(g, qn, kn, v, bias):
    _, s, _ = qn.shape
    scale = HEAD ** -0.5
    steps = (s // QBLK) // DIL_PAIRS_PER_STEP

    def body(q_ref, k_ref, v_ref, b_ref, o_ref):
        for u in range(DIL_PAIRS_PER_STEP):
            rows, before, has_before = _dil_rows(g, s, pl.program_id(1) * DIL_PAIRS_PER_STEP + u)
            qb = (q_ref[0, rows, :] * scale).astype(BF16)
            _, _, l0, l1 = _dil_logits(qb, k_ref, rows, before, has_before, b_ref)
            m = jnp.max(jnp.maximum(l0, l1), axis=1, keepdims=True)
            p0, p1 = jnp.exp(l0 - m), jnp.exp(l1 - m)
            den = jnp.sum(p0 + p1, axis=1, keepdims=True)
            inv = 1.0 / den
            o = (_dot((p0 * inv).astype(BF16), v_ref[0, rows, :].astype(BF16))
                 + _dot((p1 * inv).astype(BF16), v_ref[0, before, :].astype(BF16)))
            lse = jnp.broadcast_to(m + jnp.log(den), (QBLK, HEAD))
            o_ref[0, rows, :] = jnp.concatenate([o, lse], axis=1)

    return pl.pallas_call(
        body, name="dil%d_fwd" % g, grid=(DIL_GROUP, steps), in_specs=_dil_group_specs(g, s),
        out_specs=pl.BlockSpec((1, s, 2 * HEAD), lambda j, p: (j, 0, 0)),
        out_shape=jax.ShapeDtypeStruct((DIL_GROUP, s, 2 * HEAD), F32),
        compiler_params=_params(("arbitrary", "arbitrary")),
    )(qn, kn, v, bias)


def _dil_group_bwd(g, qn, kn, v, bias, ol, dol, prev):
    _, s, _ = qn.shape
    scale = HEAD ** -0.5
    steps = (s // QBLK) // DIL_PAIRS_PER_STEP
    prev = list(prev) if prev is not None else []

    def body(q_ref, k_ref, v_ref, b_ref, ol_ref, dol_ref, *rest):
        dq_ref, dk_ref, dv_ref, db_ref = rest[-4:]

        @pl.when(pl.program_id(1) == 0)
        def _():
            for r in (dk_ref, dv_ref, db_ref):
                r[...] = jnp.zeros_like(r)

        for u in range(DIL_PAIRS_PER_STEP):
            rows, before, has_before = _dil_rows(g, s, pl.program_id(1) * DIL_PAIRS_PER_STEP + u)
            qb = (q_ref[0, rows, :] * scale).astype(BF16)
            k0, k1, l0, l1 = _dil_logits(qb, k_ref, rows, before, has_before, b_ref)
            v0, v1 = v_ref[0, rows, :].astype(BF16), v_ref[0, before, :].astype(BF16)
            out_lse, d_out_lse = ol_ref[0, rows, :], dol_ref[0, rows, :]
            o, lse = out_lse[:, :HEAD], out_lse[:, HEAD:HEAD + 1]
            do, dlse = d_out_lse[:, :HEAD], d_out_lse[:, HEAD:HEAD + 1]
            dob = do.astype(BF16)
            p0, p1 = jnp.exp(l0 - lse), jnp.exp(l1 - lse)
            shift = dlse - jnp.sum(do * o, axis=1, keepdims=True)
            dl0 = p0 * (_dot(dob, v0, NT2) + shift)
            dl1 = p1 * (_dot(dob, v1, NT2) + shift)
            dl0b, dl1b = dl0.astype(BF16), dl1.astype(BF16)
            dq_ref[0, rows, :] = (_dot(dl0b, k0) + _dot(dl1b, k1)) * scale
            dk_ref[0, rows, :] += _dot(dl0b, qb, TN2)
            dk_ref[0, before, :] += _dot(dl1b, qb, TN2)
            dv_ref[0, rows, :] += _dot(p0.astype(BF16), dob, TN2)
            dv_ref[0, before, :] += _dot(p1.astype(BF16), dob, TN2)
            db_ref[0, 0] += dl0
            db_ref[0, 1] += dl1

    head_out = pl.BlockSpec((1, s, HEAD), lambda j, p: (DIL_GROUP * g + j, 0, 0))
    rows128 = pl.BlockSpec((1, s, 2 * HEAD), lambda j, p: (j, 0, 0))
    full_sh = jax.ShapeDtypeStruct(qn.shape, F32)
    return pl.pallas_call(
        body, name="dil%d_bwd" % g, grid=(DIL_GROUP, steps),
        in_specs=_dil_group_specs(g, s) + [rows128, rows128] + [pl.BlockSpec(memory_space=pl.ANY)] * len(prev),
        out_specs=[head_out, head_out, head_out, pl.BlockSpec((1, 2, QBLK, QBLK), lambda j, p: (j, 0, 0, 0))],
        out_shape=[full_sh, full_sh, full_sh, jax.ShapeDtypeStruct((DIL_GROUP, 2, QBLK, QBLK), F32)],
        input_output_aliases={6 + i: i for i in range(len(prev))},
        compiler_params=_params(("arbitrary", "arbitrary")),
    )(qn, kn, v, bias, ol, dol, *prev)


@functools.partial(jax.custom_vjp, nondiff_argnums=(2,))
def _bdot(a, b, dims):
    return _dot(a.astype(BF16), b.astype(BF16), dims)


def _bdot_fwd(a, b, dims):
    return _bdot(a, b, dims), (a, b)


def _bdot_bwd(dims, res, dc):
    a, b = res
    nn, nt, tn = (NN2, NT2, TN2) if dims in (NN2, NT2, TN2) else (NN3, NT3, TN3)
    if dims == nn:
        return _bdot(dc, b, nt), _bdot(a, dc, tn)
    if dims == nt:
        return _bdot(dc, b, nn), _bdot(dc, a, tn)
    return _bdot(b, dc, nt), _bdot(a, dc, nn)


_bdot.defvjp(_bdot_fwd, _bdot_bwd)


def _ones_dot(ones, x, dims):
    o = ones.astype(BF16)
    hi = x.astype(BF16)
    r1 = x - hi.astype(F32)
    mid = r1.astype(BF16)
    lo = (r1 - mid.astype(F32)).astype(BF16)
    return _dot(o, hi, dims) + _dot(o, mid, dims) + _dot(o, lo, dims)


@jax.custom_vjp
def _prefix_sums(x):
    c = x.shape[1]
    row = lax.broadcasted_iota(jnp.int32, (x.shape[0], c, c), 1)
    col = lax.broadcasted_iota(jnp.int32, (x.shape[0], c, c), 2)
    return _ones_dot((row >= col).astype(F32), x, NN3)


def _prefix_sums_fwd(x):
    return _prefix_sums(x), None


def _prefix_sums_bwd(_, dy):
    c = dy.shape[1]
    row = lax.broadcasted_iota(jnp.int32, (dy.shape[0], c, c), 1)
    col = lax.broadcasted_iota(jnp.int32, (dy.shape[0], c, c), 2)
    return (_ones_dot((row <= col).astype(F32), dy, NN3),)


_prefix_sums.defvjp(_prefix_sums_fwd, _prefix_sums_bwd)


def _rwkv_chunk(s0, r, lw, kraw, v, ag, kk_w, ka_w, rk_w, lng, lnb):
    hb, c, _ = r.shape
    kk = kraw * kk_w
    kk = kk / jnp.maximum(jnp.sqrt(jnp.sum(kk * kk, axis=-1, keepdims=True)), 1e-12)
    k = kraw * (1.0 + (ag - 1.0) * ka_w)
    a = -kk
    b = kk * ag
    row = lax.broadcasted_iota(jnp.int32, (hb, c, c), 1)
    col = lax.broadcasted_iota(jnp.int32, (hb, c, c), 2)
    lower, strict = row >= col, row > col
    cum = _prefix_sums(lw)
    ecum, einv = jnp.exp(cum), jnp.exp(-cum)
    rt, kt, bt = r * ecum, k * einv, b * einv
    at = a * jnp.exp(cum - lw)
    ar = jnp.concatenate([at, rt], axis=1)
    scores = _bdot(ar, jnp.concatenate([bt, kt], axis=1), NT3)
    a_ab = jnp.where(strict, scores[:, :c, :c], 0.0)
    a_ak = jnp.where(strict, scores[:, :c, c:], 0.0)
    p_rb = jnp.where(lower, scores[:, c:, :c], 0.0)
    p_rk = jnp.where(lower, scores[:, c:, c:], 0.0)
    from_s0 = _bdot(ar, s0, NT3)
    rhs = from_s0[:, :c] + _bdot(a_ak, v, NN3)
    inv = (row == col).astype(F32) + a_ab
    pw = a_ab
    for _ in range(int(math.log2(c)) - 1):
        pw = _bdot(pw, pw, NN3)
        inv = inv + _bdot(inv, pw, NN3)
    u = _bdot(inv, rhs, NN3)
    uv = jnp.concatenate([u, v], axis=1)
    y = from_s0[:, c:] + _bdot(jnp.concatenate([p_rb, p_rk], axis=2), uv, NN3)
    cum_end = cum[:, c - 1:c, :]
    dec = jnp.exp(cum_end - cum)
    s_end = s0 * jnp.exp(cum_end) + _bdot(uv, jnp.concatenate([b * dec, k * dec], axis=1), TN3)
    mu = jnp.mean(y, axis=-1, keepdims=True)
    var = jnp.mean(jnp.square(y - mu), axis=-1, keepdims=True)
    z = (y - mu) * lax.rsqrt(var + GN_EPS) * lng + lnb + jnp.sum(r * k * rk_w, axis=-1, keepdims=True) * v
    return z, s_end


def _rwkv_specs(nc, rev):
    cidx = (lambda c: nc - 1 - c) if rev else (lambda c: c)
    seq = pl.BlockSpec((RW_HB, RW_CHUNK, HEAD), lambda hg, c: (hg, cidx(c), 0))
    par = pl.BlockSpec((RW_HB, 1, HEAD), lambda hg, c: (hg, 0, 0))
    st = pl.BlockSpec((1, RW_HB, HEAD, HEAD), lambda hg, c: (cidx(c), hg, 0, 0))
    return seq, par, st


def _rwkv_fwd(seqs, pars):
    s = seqs[0].shape[1]
    nc = s // RW_CHUNK

    def body(*refs):
        seq_refs, par_refs = refs[:5], refs[5:10]
        z_ref, st_ref, state = refs[10:]
        c = pl.program_id(1)

        @pl.when(c == 0)
        def _():
            state[...] = jnp.zeros_like(state)

        s0 = state[...]
        st_ref[0] = s0
        z, s_end = _rwkv_chunk(s0, *[r[...] for r in seq_refs], *[r[...] for r in par_refs])
        z_ref[...] = z
        state[...] = s_end

    seq, par, st = _rwkv_specs(nc, False)
    return pl.pallas_call(
        body, name="rwkv_fwd", grid=(N_HEADS // RW_HB, nc),
        in_specs=[seq] * 5 + [par] * 5, out_specs=[seq, st],
        out_shape=[jax.ShapeDtypeStruct((N_HEADS, s, HEAD), F32), jax.ShapeDtypeStruct((nc, N_HEADS, HEAD, HEAD), F32)],
        scratch_shapes=[pltpu.VMEM((RW_HB, HEAD, HEAD), F32)],
        compiler_params=_params(("arbitrary", "arbitrary")),
    )(*seqs, *pars)


def _rwkv_bwd(seqs, pars, states, dz):
    s = seqs[0].shape[1]
    nc = s // RW_CHUNK

    def body(*refs):
        seq_refs, par_refs = refs[:5], refs[5:10]
        st_ref, dz_ref = refs[10:12]
        dseq_refs, dpar_refs, dstate = refs[12:17], refs[17:22], refs[22]
        c = pl.program_id(1)

        @pl.when(c == 0)
        def _():
            dstate[...] = jnp.zeros_like(dstate)
            for r in dpar_refs:
                r[...] = jnp.zeros_like(r)

        _, vjp = jax.vjp(_rwkv_chunk, st_ref[0], *[r[...] for r in seq_refs], *[r[...] for r in par_refs])
        g = vjp((dz_ref[...], dstate[...]))
        dstate[...] = g[0]
        for r, gs in zip(dseq_refs, g[1:6]):
            r[...] = gs
        for r, gp in zip(dpar_refs, g[6:]):
            r[...] += gp

    seq, par, st = _rwkv_specs(nc, True)
    seq_sh = jax.ShapeDtypeStruct((N_HEADS, s, HEAD), F32)
    par_sh = jax.ShapeDtypeStruct((N_HEADS, 1, HEAD), F32)
    outs = pl.pallas_call(
        body, name="rwkv_bwd", grid=(N_HEADS // RW_HB, nc),
        in_specs=[seq] * 5 + [par] * 5 + [st, seq],
        out_specs=[seq] * 5 + [par] * 5, out_shape=[seq_sh] * 5 + [par_sh] * 5,
        scratch_shapes=[pltpu.VMEM((RW_HB, HEAD, HEAD), F32)],
        compiler_params=_params(("arbitrary", "arbitrary")),
    )(*seqs, *pars, states, dz)
    return list(outs[:5]), list(outs[5:])


def _norm_fn(x, g):
    return (_rms(x, g),)


def _attn_prep_fn(proj, qn_w, kn_w):
    a, b = SB_HEADS, 3 * DIL_GROUP
    return (proj[0:a], proj[a:2 * a], proj[2 * a:3 * a],
            _rms(proj[3 * a:3 * a + b], qn_w), _rms(proj[3 * a + b:3 * a + 2 * b], kn_w), proj[3 * a + 2 * b:])


def _attn_merge_fn(o_sb, ol0, ol1, ol2):
    groups = (ol0, ol1, ol2)
    merged = []
    for j in range(DIL_GROUP):
        lses = [ol[j][:, HEAD:HEAD + 1] for ol in groups]
        m = functools.reduce(jnp.maximum, lses)
        es = [jnp.exp(l - m) for l in lses]
        inv = 1.0 / functools.reduce(jnp.add, es)
        merged.append(functools.reduce(jnp.add, [(e * inv) * ol[j][:, :HEAD] for e, ol in zip(es, groups)]))
    return (jnp.concatenate([_heads_to_nat(o_sb)] + merged, axis=-1),)


def _rw_mix_fn(x, xp, gn, mix, w0, w1, w2, a0, a1, a2, g1, g2):
    h = _rms(x, gn)
    xx = _rms(xp, gn) - h
    xr, xw, xk, xv, xa, xg = [h + xx * mix[i:i + 1] for i in range(6)]
    w_log = -jax.nn.softplus(-(w0 + _mm(jnp.tanh(_mm(xw, w1)), w2))) - 0.5
    lw = -jnp.exp(w_log)
    ag = jax.nn.sigmoid(a0 + _mm(_mm(xa, a1), a2))
    gate = _mm(jax.nn.sigmoid(_mm(xg, g1)), g2)
    return xr, xk, xv, _nat_to_heads(lw), _nat_to_heads(ag), gate


def _rw_gate_fn(z, gate):
    return (_heads_to_nat(z) * gate,)


def _adamw(name, w, m, v, gparts, row0=0, prev=None):
    big_r, c = w.shape
    r = gparts.shape[1]
    tr = r
    if r % 8 == 0:
        tr = max(t for t in range(8, r + 1, 8) if r % t == 0 and (t * c * 4 <= (1 << 20) or t == 8))
    assert row0 % tr == 0 and (r == big_r or r % 8 == 0)
    off = row0 // tr

    def body(w_ref, m_ref, v_ref, g_ref, *rest):
        go_ref, d_ref, mo_ref, vo_ref = rest[-4:]
        g = g_ref[0].astype(F32)
        for j in range(1, N_DEV):
            g = g + g_ref[j].astype(F32)
        mn = ADAM_B1 * m_ref[...] + (1.0 - ADAM_B1) * g
        vn = ADAM_B2 * v_ref[...] + (1.0 - ADAM_B2) * jnp.square(g)
        m_hat = mn / (1.0 - ADAM_B1 ** ADAM_STEP)
        v_hat = vn / (1.0 - ADAM_B2 ** ADAM_STEP)
        go_ref[...] = g
        d_ref[...] = -ADAM_LR * (m_hat / (jnp.sqrt(v_hat) + ADAM_EPS) + ADAM_WD * w_ref[...])
        mo_ref[...] = mn
        vo_ref[...] = vn

    tile = pl.BlockSpec((tr, c), lambda i: (i + off, 0))
    sh = jax.ShapeDtypeStruct((big_r, c), F32)
    prev = list(prev) if prev is not None else []
    return pl.pallas_call(
        body, name=name, grid=(r // tr,),
        in_specs=([tile, tile, tile, pl.BlockSpec((N_DEV, tr, c), lambda i: (0, i, 0))]
                  + [pl.BlockSpec(memory_space=pl.ANY)] * len(prev)),
        out_specs=[tile] * 4, out_shape=[sh] * 4,
        input_output_aliases={4 + j: j for j in range(len(prev))},
        compiler_params=_params(("arbitrary",)),
    )(w, m, v, gparts, *prev)


def _col_blocks_to_nat(g):
    return jnp.moveaxis(g, 0, 1).reshape(g.shape[1], -1)


def _nat_to_col_blocks(a):
    return jnp.moveaxis(a.reshape(a.shape[0], N_DEV, -1), 1, 0)


AG_GROUPS = ("f00", "att", "f01", "f10", "rw", "f11")
RS_GROUPS = ("f11", "rw", "f10", "f01", "f00", "att")
BF16_GRAD_GROUPS = ("att", "f00")
RW_SHARDED = ('rw_mix', 'rw_w0', 'rw_w1', 'rw_w2', 'rw_a0', 'rw_a1', 'rw_a2', 'rw_g1', 'rw_g2', 'rw_kk', 'rw_ka',
              'rw_wr', 'rw_wk', 'rw_wv', 'rw_wo', 'rw_lnx_g', 'rw_lnx_b')


def _step(x, target, rep, get, put):
    tied = lambda a, zero: a + zero[0, 0].astype(a.dtype)
    s, d = x.shape
    tf = min(512, s)
    tt = min(256, s)
    row = lambda a: a.reshape(1, -1)
    mix_norm = rep["mix_norm"]
    ffw = {(0, 0): get("f00", None)}
    ffn_norm = _col_blocks_to_nat(ffw[(0, 0)]["ffn_norm"].reshape(N_DEV, 4, -1))

    acts = {}

    def ffn(nm, xin, l, h):
        g = ffw[(l, h)]
        out, *acts[(l, h)] = _ffn_fwd(nm, xin, ffn_norm[2 * l + h][None], g["gate"], g["up"], g["down"], min(2 * tf, s))
        return out

    x1 = ffn("ffn00_fwd", x, 0, 0)
    att = get("att", x1)
    w_in = _col_blocks_to_nat(att["attn_w_in"])
    w_out = _col_blocks_to_nat(att["attn_w_out"])
    (h0,) = _tile_fwd("mixnorm0_fwd", _norm_fn, [(x1, "nat")], [mix_norm[0:1]], [((s, d), BF16, "nat")], tt)
    proj = _linear_fwd("attn_in_fwd", h0, w_in, tt, out_layout="hm")
    bias = _bias_tiles(rep["rel_bias"], s)
    prep_pars = [rep["attn_q_norm"], rep["attn_k_norm"]]
    sb_sh, dl_sh = (SB_HEADS, s, HEAD), (3 * DIL_GROUP, s, HEAD)
    sq, sk, sv, qn, kn, vd = _tile_fwd("attn_prep_fwd", _attn_prep_fn, [(proj, "hm")], prep_pars,
                                       [(sb_sh, BF16, "hm")] * 3 + [(dl_sh, F32, "hm")] * 3, tt // 2)
    o_sb = _sb_fwd(sq, sk, sv)
    ols = [_dil_group_fwd(g, qn, kn, vd, bias) for g in range(3)]
    merge_tiled = [(o_sb, "hm")] + [(ol, "hm") for ol in ols]
    (merged,) = _tile_fwd("merge_fwd", _attn_merge_fn, merge_tiled, [], [((s, 512), BF16, "nat")], tt)
    x2 = _linear_fwd("attn_out_fwd", merged, w_out, tt, residual=x1)
    ffw[(0, 1)] = get("f01", x2)
    x3 = ffn("ffn01_fwd", x2, 0, 1)
    ffw[(1, 0)] = get("f10", x3)
    x4 = ffn("ffn10_fwd", x3, 1, 0)
    rw = get("rw", x4)
    rw_mix = _col_blocks_to_nat(rw["rw_mix"])
    rw_w1, rw_a1, rw_g1 = (rw[k].reshape(d, -1) for k in ("rw_w1", "rw_a1", "rw_g1"))
    rw_w2, rw_a2, rw_g2 = (_col_blocks_to_nat(rw[k]) for k in ("rw_w2", "rw_a2", "rw_g2"))
    rw_w0, rw_a0 = row(rw["rw_w0"]), row(rw["rw_a0"])
    head_par = lambda a: a.reshape(N_HEADS, 1, HEAD)
    scan_pars = [head_par(rw["rw_kk"]), head_par(rw["rw_ka"]), head_par(rep["rw_rk"]),
                 head_par(rw["rw_lnx_g"]), head_par(rw["rw_lnx_b"])]
    w_rkv = [rw[k].reshape(d, d) for k in ("rw_wr", "rw_wk", "rw_wv")]
    w_o = rw["rw_wo"].reshape(d, d)
    x4p = jnp.pad(x4, ((1, 0), (0, 0)))[:-1]
    mix_tiled = [(x4, "nat"), (x4p, "nat")]
    mix_pars = [mix_norm[1:2], rw_mix, rw_w0, rw_w1, rw_w2, rw_a0, rw_a1, rw_a2, rw_g1, rw_g2]
    hm_sh = (N_HEADS, s, HEAD)
    xr, xk, xv, lw, ag, gate = _tile_fwd(
        "rw_mix_fwd", _rw_mix_fn, mix_tiled, mix_pars,
        [((s, d), BF16, "nat")] * 3 + [(hm_sh, F32, "hm")] * 2 + [((s, d), F32, "nat")], tt)
    r_h, k_h, v_h = [_linear_fwd("rw_%s_fwd" % nm, xi, wi, tt, out_layout="hm")
                     for nm, xi, wi in zip("rkv", (xr, xk, xv), w_rkv)]
    scan_seqs = [r_h, lw, k_h, v_h, ag]
    z, states = _rwkv_fwd(scan_seqs, scan_pars)
    (zg,) = _tile_fwd("rw_gate_fwd", _rw_gate_fn, [(z, "hm"), (gate, "nat")], [], [((s, d), BF16, "nat")], tt)
    x5 = _linear_fwd("rw_out_fwd", zg, w_o, tt, residual=x4)
    ffw[(1, 1)] = get("f11", x5)
    y = ffn("ffn11_fwd", x5, 1, 1)
    dy, loss = _loss_head(y, target, tf)

    G = {}
    dgn = {}

    def fb(nm, group, xin, dout, l, h, zero=None, extra=None):
        g = ffw[(l, h)]
        gn = ffn_norm[2 * l + h][None]
        dxin, dgn[(l, h)], dwg, dwu, dwd = _ffn_bwd(nm, xin, dout, gn if zero is None else tied(gn, zero),
                                                   g["gate"], g["up"], g["down"], *acts[(l, h)], tf)
        shard = {"gate": dwg, "up": dwu, "down": dwd}
        if extra is not None:
            shard.update(extra())
        return dxin, put(group, {}, shard)

    dx5, zero = fb("ffn11_bwd", "f11", x5, dy, 1, 1)
    dzg = _linear_dx("rw_out_dx", dx5, w_o, tt)
    G["rw_wo"] = _linear_dw("rw_out_dw", zg, dx5, tf, 512)
    (dz, dgate), _ = _tile_bwd("rw_gate_bwd", _rw_gate_fn, [(z, "hm"), (gate, "nat")], [], [(dzg, "nat")], tt, [True, True])
    (dr_h, dlw, dk_h, dv_h, dag), dscan = _rwkv_bwd(scan_seqs, [tied(scan_pars[0], zero)] + scan_pars[1:], states, dz)
    drkv = (dr_h, dk_h, dv_h)
    for k, gpar in zip(("rw_kk", "rw_ka", "rw_rk", "rw_lnx_g", "rw_lnx_b"), dscan):
        G[k] = gpar
    dxs = []
    for j, (nm, xi, wi) in enumerate(zip("rkv", (xr, xk, xv), w_rkv)):
        dxs.append(_linear_dx("rw_%s_dx" % nm, drkv[j], wi, tt, dy_layout="hm"))
        G["rw_w" + nm] = _linear_dw("rw_%s_dw" % nm, xi, drkv[j], tf, 512, dy_layout="hm")
    (dx4a, dx4p), dmix = _tile_bwd(
        "rw_mix_bwd", _rw_mix_fn, mix_tiled, mix_pars,
        [(dxs[0], "nat"), (dxs[1], "nat"), (dxs[2], "nat"), (dlw, "hm"), (dag, "hm"), (dgate, "nat")],
        tt, [True, True], adds=[dx5, None])
    d_mixn1 = dmix[0]
    for k, gpar in zip(("rw_mix", "rw_w0", "rw_w1", "rw_w2", "rw_a0", "rw_a1", "rw_a2", "rw_g1", "rw_g2"), dmix[1:]):
        G[k] = gpar
    dx4 = dx4a + jnp.pad(dx4p[1:], ((0, 1), (0, 0)))
    for k in ("rw_mix", "rw_w2", "rw_a2", "rw_g2"):
        G[k] = _nat_to_col_blocks(G[k])
    for k in ("rw_w1", "rw_a1", "rw_g1", "rw_wr", "rw_wk", "rw_wv", "rw_wo"):
        G[k] = G[k].reshape(N_DEV, d // N_DEV, -1)
    for k in ("rw_w0", "rw_a0", "rw_kk", "rw_ka", "rw_lnx_g", "rw_lnx_b"):
        G[k] = G[k].reshape(N_DEV, 1, d // N_DEV)
    zero = put("rw", {"rw_rk": G["rw_rk"].reshape(N_HEADS, HEAD)}, {k: G[k] for k in RW_SHARDED})
    dx3, zero = fb("ffn10_bwd", "f10", x3, dx4, 1, 0, zero)
    dx2, zero = fb("ffn01_bwd", "f01", x2, dx3, 0, 1, zero)
    dmerged = _linear_dx("attn_out_dx", dx2, tied(w_out, zero), tt)
    (do_sb, *dols), _ = _tile_bwd("merge_bwd", _attn_merge_fn, merge_tiled, [], [(dmerged, "nat")], tt, [True] * 4)
    dq_sb, dk_sb, dv_sb = _sb_bwd(sq, sk, sv, do_sb)
    dil_grads, dbias = None, []
    for g in range(3):
        *dil_grads, db = _dil_group_bwd(g, qn, kn, vd, bias, ols[g], dols[g], dil_grads)
        dbias.append(db)
    dqn, dkn, dvd = dil_grads
    dbias = jnp.concatenate(dbias, axis=1)
    (dproj,), (dqn_w, dkn_w) = _tile_bwd(
        "attn_prep_bwd", _attn_prep_fn, [(proj, "hm")], prep_pars,
        [(dq_sb, "hm"), (dk_sb, "hm"), (dv_sb, "hm"), (dqn, "hm"), (dkn, "hm"), (dvd, "hm")], tt // 2, [True])
    dh0 = _linear_dx("attn_in_dx", dproj, w_in, tt, dy_layout="hm")
    (dx1,), (d_mixn0,) = _tile_bwd("mixnorm0_bwd", _norm_fn, [(x1, "nat")], [mix_norm[0:1]], [(dh0, "nat")], tt,
                                   [True], adds=[dx2])
    order = [(0, 0), (0, 1), (1, 0), (1, 1)]
    norm_grads = lambda: {"ffn_norm": _nat_to_col_blocks(jnp.concatenate([dgn[o] for o in order], axis=0))}
    dx0, zero = fb("ffn00_bwd", "f00", x, dx1, 0, 0, extra=norm_grads)
    G["attn_w_out"] = _linear_dw("attn_out_dw", tied(merged, zero), dx2, tf, 512)
    G["attn_w_in"] = _linear_dw("attn_in_dw", tied(h0, zero), dproj, tf, 512, dy_layout="hm")
    rep_grads = {"mix_norm": jnp.concatenate([d_mixn0, d_mixn1], axis=0), "rel_bias": _bias_tiles_bwd(dbias, s),
                 "attn_q_norm": dqn_w, "attn_k_norm": dkn_w}
    zero = put("att", rep_grads, {k: _nat_to_col_blocks(G[k]) for k in ("attn_w_in", "attn_w_out")})
    return loss, dx0, zero


WEIGHTS = ['ffn_norm', 'ffn_w_gate', 'ffn_w_up', 'ffn_w_down', 'mix_norm', 'rel_bias', 'attn_w_in', 'attn_q_norm',
           'attn_k_norm', 'attn_w_out', 'rw_mix', 'rw_w0', 'rw_w1', 'rw_w2', 'rw_a0', 'rw_a1', 'rw_a2', 'rw_g1', 'rw_g2',
           'rw_kk', 'rw_ka', 'rw_rk', 'rw_wr', 'rw_wk', 'rw_wv', 'rw_wo', 'rw_lnx_g', 'rw_lnx_b']
REPLICATED = ('mix_norm', 'rel_bias', 'attn_q_norm', 'attn_k_norm', 'rw_rk')
BF16_WEIGHTS = ('ffn_w_gate', 'ffn_w_up', 'ffn_w_down', 'attn_w_in', 'attn_w_out', 'rw_wr', 'rw_wk', 'rw_wv', 'rw_wo')


def kernel(x, ffn_norm, ffn_w_gate, ffn_w_up, ffn_w_down, mix_norm, rel_bias, attn_w_in, attn_q_norm, attn_k_norm, attn_w_out, rw_mix, rw_w0, rw_w1, rw_w2, rw_a0, rw_a1, rw_a2, rw_g1, rw_g2, rw_kk, rw_ka, rw_rk, rw_wr, rw_wk, rw_wv, rw_wo, rw_lnx_g, rw_lnx_b, loss_target, m_ffn_norm, m_ffn_w_gate, m_ffn_w_up, m_ffn_w_down, m_mix_norm, m_rel_bias, m_attn_w_in, m_attn_q_norm, m_attn_k_norm, m_attn_w_out, m_rw_mix, m_rw_w0, m_rw_w1, m_rw_w2, m_rw_a0, m_rw_a1, m_rw_a2, m_rw_g1, m_rw_g2, m_rw_kk, m_rw_ka, m_rw_rk, m_rw_wr, m_rw_wk, m_rw_wv, m_rw_wo, m_rw_lnx_g, m_rw_lnx_b, v_ffn_norm, v_ffn_w_gate, v_ffn_w_up, v_ffn_w_down, v_mix_norm, v_rel_bias, v_attn_w_in, v_attn_q_norm, v_attn_k_norm, v_attn_w_out, v_rw_mix, v_rw_w0, v_rw_w1, v_rw_w2, v_rw_a0, v_rw_a1, v_rw_a2, v_rw_g1, v_rw_g2, v_rw_kk, v_rw_ka, v_rw_rk, v_rw_wr, v_rw_wk, v_rw_wv, v_rw_wo, v_rw_lnx_g, v_rw_lnx_b):
    args = locals()
    w = {k: args[k] for k in WEIGHTS}
    cast = lambda k, a: a.astype(BF16) if k in BF16_WEIGHTS else a

    sources = {}
    for l, h in ((0, 0), (0, 1), (1, 0), (1, 1)):
        sources["f%d%d" % (l, h)] = {"gate": cast("ffn_w_gate", ffn_w_gate[l, h]), "up": cast("ffn_w_up", ffn_w_up[l, h]),
                                     "down": cast("ffn_w_down", ffn_w_down[l, h])}
    sources["f00"]["ffn_norm"] = ffn_norm
    drop_lead = lambda a: a[0] if a.ndim == 3 else a
    sources["att"] = {k: cast(k, w[k][0]) for k in ("attn_w_in", "attn_w_out")}
    sources["rw"] = {k: cast(k, drop_lead(w[k])) for k in RW_SHARDED}
    ag, token = {}, None
    for group in AG_GROUPS:
        names, arrays = list(sources[group]), list(sources[group].values())
        if token is not None:
            arrays[0] = arrays[0] + token[0, 0].astype(arrays[0].dtype)
        ag[group] = (names, _exchange_start("ag_start_" + group, arrays, []))
        token = ag[group][1]["token"]
    last_ag_token = token

    def get(group, after):
        names, started = ag[group]
        gathered, _ = _exchange_wait("ag_wait_" + group, started, last_ag_token if after is None else after)
        return dict(zip(names, gathered))

    rs = {}

    def put(group, rep_grads, shard_grads):
        if group in BF16_GRAD_GROUPS:
            shard_grads = {k: v.astype(BF16) for k, v in shard_grads.items()}
        started = _exchange_start("rs_start_" + group, list(rep_grads.values()), list(shard_grads.values()))
        rs[group] = (list(rep_grads), list(shard_grads), started)
        return started["token"]

    loss, dx, last_zero = _step(x[0], loss_target[0], {k: w[k] for k in REPLICATED}, get, put)
    loss = lax.psum(loss, MESH_AXES)

    results = {}
    ffn_prev = {}

    def update(k, parts, row0=0, prev=None):
        c = w[k].shape[-1]
        as2d = lambda a: a.reshape(-1, c)
        return _adamw("adamw_%s_%d" % (k, row0), as2d(w[k]), as2d(args["m_" + k]), as2d(args["v_" + k]),
                      parts.reshape(N_DEV, -1, c), row0, prev)

    after = last_zero
    for group in RS_GROUPS:
        rep_names, shard_names, started = rs[group]
        rep_parts, shard_parts = _exchange_wait("rs_wait_" + group, started, after)
        for k, parts in list(zip(rep_names, rep_parts)) + list(zip(shard_names, shard_parts)):
            if k in ("gate", "up", "down"):
                full = "ffn_w_" + k
                piece = 2 * int(group[1]) + int(group[2])
                ffn_prev[full] = update(full, parts, piece * parts.shape[1], ffn_prev.get(full))
                results[full] = ffn_prev[full]
            else:
                results[k] = update(k, parts)
            after = results[k if k in results else "ffn_w_" + k][0]

    outs = [[results[k][j].reshape(w[k].shape) for k in WEIGHTS] for j in range(4)]
    return (loss, dx[None], *outs[0], *outs[1], *outs[2], *outs[3])
```

```python
import functools
import math

import numpy as np
import jax
import jax.numpy as jnp
from jax import lax
from jax.experimental import pallas as pl
from jax.experimental.pallas import tpu as pltpu

F32, BF16 = jnp.float32, jnp.bfloat16
HI = lax.Precision.HIGH

N_DEV = 8
D_MODEL = 1024
HEAD = 64
N_HEADS = 16
SB_HEADS = 4
DIL_GROUP = 4
DIL_PATTERNS = ((128, 1), (512, 4), (2048, 16))
QBLK = 128
N_BUCKETS = 32
MAX_DISTANCE = 2048
NORM_EPS = 1e-6
GN_EPS = 64e-5
NEG_INF = -1e30
RW_CHUNK = 64
RW_HB = 16
ADAM_LR, ADAM_B1, ADAM_B2, ADAM_EPS, ADAM_WD, ADAM_STEP = 0.001, 0.9, 0.999, 1e-08, 0.01, 10
MESH_AXES = ("x", "y", "c")
VMEM_LIMIT_BYTES = 56 * 1024 * 1024

NN2 = (((1,), (0,)), ((), ()))
NT2 = (((1,), (1,)), ((), ()))
TN2 = (((0,), (0,)), ((), ()))
NN3 = (((2,), (1,)), ((0,), (0,)))
NT3 = (((2,), (2,)), ((0,), (0,)))
TN3 = (((1,), (1,)), ((0,), (0,)))


def _dot(a, b, dims=NN2, prec=None):
    return lax.dot_general(a, b, dims, precision=prec, preferred_element_type=F32)


def _params(sem=None):
    return pltpu.CompilerParams(dimension_semantics=sem, vmem_limit_bytes=VMEM_LIMIT_BYTES)


@jax.custom_vjp
def _mm(x, w):
    return _dot(x.astype(BF16), w.astype(BF16))


def _mm_fwd(x, w):
    return _mm(x, w), (x, w)


def _mm_bwd(res, dy):
    x, w = res
    dyb = dy.astype(BF16)
    return (_dot(dyb, w.astype(BF16), NT2).astype(x.dtype), _dot(x.astype(BF16), dyb, TN2).astype(w.dtype))


_mm.defvjp(_mm_fwd, _mm_bwd)


def _rms(x, g):
    return x * lax.rsqrt(jnp.mean(x * x, axis=-1, keepdims=True) + NORM_EPS) * g


def _log_sigmoid(z):
    return jnp.minimum(z, 0.0) - jnp.log(1.0 + jnp.exp(-jnp.abs(z)))


def _heads_to_nat(v3):
    return jnp.concatenate([v3[h] for h in range(v3.shape[0])], axis=-1)


def _nat_to_heads(v2):
    return jnp.stack([v2[:, h * HEAD:(h + 1) * HEAD] for h in range(v2.shape[1] // HEAD)], axis=0)


def _exchange(name, gathers, scatters):
    n_g = len(gathers)
    arrays = list(gathers) + list(scatters)
    n = len(arrays)
    out_shape = [jax.ShapeDtypeStruct((N_DEV,) + a.shape, a.dtype) for a in gathers]
    out_shape += [jax.ShapeDtypeStruct(a.shape, a.dtype) for a in scatters]

    def body(*refs):
        ins, outs = refs[:n], refs[n:2 * n]
        send_sems, recv_sems, local_sems = refs[2 * n:]
        x, y, c = lax.axis_index("x"), lax.axis_index("y"), lax.axis_index("c")
        me = 4 * x + 2 * y + c

        def src(i, idx):
            return ins[i] if i < n_g else ins[i].at[idx]

        local = [pltpu.make_async_copy(src(i, me), outs[i].at[me], local_sems.at[i]) for i in range(n)]
        for cp in local:
            cp.start()
        remote = []
        for m in range(1, N_DEV):
            px, py, pc = x ^ ((m >> 2) & 1), y ^ ((m >> 1) & 1), c ^ (m & 1)
            peer = 4 * px + 2 * py + pc
            for i in range(n):
                cp = pltpu.make_async_remote_copy(
                    src_ref=src(i, peer), dst_ref=outs[i].at[me],
                    send_sem=send_sems.at[i, m - 1], recv_sem=recv_sems.at[i, m - 1],
                    device_id=(px, py, pc), device_id_type=pl.DeviceIdType.MESH)
                cp.start()
                arrival = pltpu.make_async_remote_copy(
                    src_ref=src(i, peer), dst_ref=outs[i].at[peer],
                    send_sem=send_sems.at[i, m - 1], recv_sem=recv_sems.at[i, m - 1],
                    device_id=(px, py, pc), device_id_type=pl.DeviceIdType.MESH)
                remote.append((cp, arrival))
        for cp, arrival in remote:
            cp.wait_send()
            arrival.wait_recv()
        for cp in local:
            cp.wait()

    hbm = pl.BlockSpec(memory_space=pltpu.HBM)
    outs = pl.pallas_call(
        body, name=name, out_shape=out_shape,
        in_specs=[hbm] * n, out_specs=[hbm] * n,
        scratch_shapes=[pltpu.SemaphoreType.DMA((n, N_DEV - 1)), pltpu.SemaphoreType.DMA((n, N_DEV - 1)),
                        pltpu.SemaphoreType.DMA((n,))],
    )(*arrays)
    return list(outs[:n_g]), list(outs[n_g:])


def _mesh_peers():
    x, y, c = lax.axis_index("x"), lax.axis_index("y"), lax.axis_index("c")
    peers = []
    for m in range(1, N_DEV):
        px, py, pc = x ^ ((m >> 2) & 1), y ^ ((m >> 1) & 1), c ^ (m & 1)
        peers.append((m, (px, py, pc), 4 * px + 2 * py + pc))
    return 4 * x + 2 * y + c, peers


_HBM_SPEC = pl.BlockSpec(memory_space=pltpu.HBM)
_SEM_SPEC = pl.BlockSpec(memory_space=pltpu.SEMAPHORE)
_DATAFLOW = pltpu.SideEffectType.DATAFLOW_SIDE_EFFECTING


def _exchange_start(name, gathers, scatters):
    n_g = len(gathers)
    arrays = list(gathers) + list(scatters)
    n = len(arrays)
    lands = ([lax.empty((N_DEV,) + a.shape, a.dtype) for a in gathers] + [lax.empty(a.shape, a.dtype) for a in scatters])

    def body(*refs):
        ins, land = refs[:n], refs[n:2 * n]
        send_sems, recv_sems, local_sems, token = refs[2 * n], refs[2 * n + 1], refs[2 * n + 2], refs[-1]
        me, peers = _mesh_peers()
        for m, dev, peer in peers:
            for i in range(n):
                k = i * (N_DEV - 1) + m - 1
                pltpu.make_async_remote_copy(
                    src_ref=ins[i] if i < n_g else ins[i].at[peer], dst_ref=land[i].at[me],
                    send_sem=send_sems.at[k], recv_sem=recv_sems.at[k],
                    device_id=dev, device_id_type=pl.DeviceIdType.MESH).start()
        for i in range(n):
            pltpu.make_async_copy(ins[i] if i < n_g else ins[i].at[me], land[i].at[me], local_sems.at[i]).start()
        token[...] = jnp.zeros_like(token)

    sem = pltpu.SemaphoreType.DMA((n * (N_DEV - 1),))
    outs = pl.pallas_call(
        body, name=name,
        out_shape=([sem, sem, pltpu.SemaphoreType.DMA((n,))] + [pltpu.HBM(a.shape, a.dtype) for a in arrays]
                   + [pltpu.HBM(l.shape, l.dtype) for l in lands] + [jax.ShapeDtypeStruct((8, 128), F32)]),
        in_specs=[_HBM_SPEC] * (2 * n),
        out_specs=[_SEM_SPEC] * 3 + [_HBM_SPEC] * (2 * n) + [pl.BlockSpec(memory_space=pltpu.VMEM)],
        input_output_aliases={i: i + 3 for i in range(2 * n)},
        compiler_params=pltpu.CompilerParams(has_side_effects=_DATAFLOW),
    )(*[pltpu.with_memory_space_constraint(a, pltpu.HBM) for a in arrays],
      *[pltpu.with_memory_space_constraint(l, pltpu.HBM) for l in lands])
    return dict(n_g=n_g, n=n, send=outs[0], recv=outs[1], local=outs[2], srcs=list(outs[3:3 + n]),
                lands=list(outs[3 + n:3 + 2 * n]), token=outs[-1])


def _exchange_wait(name, started, after):
    n, n_g = started["n"], started["n_g"]

    def body(*refs):
        srcs, lands = refs[:n], refs[n:2 * n]
        send_sems, recv_sems, local_sems = refs[2 * n], refs[2 * n + 1], refs[2 * n + 2]
        me, peers = _mesh_peers()
        local = [pltpu.make_async_copy(srcs[i] if i < n_g else srcs[i].at[me], lands[i].at[me], local_sems.at[i])
                 for i in range(n)]
        for m, dev, peer in peers:
            for i in range(n):
                k = i * (N_DEV - 1) + m - 1
                cp = pltpu.make_async_remote_copy(
                    src_ref=srcs[i] if i < n_g else srcs[i].at[peer], dst_ref=lands[i].at[peer],
                    send_sem=send_sems.at[k], recv_sem=recv_sems.at[k],
                    device_id=dev, device_id_type=pl.DeviceIdType.MESH)
                cp.wait_send()
                cp.wait_recv()
        for cp in local:
            cp.wait()

    outs = pl.pallas_call(
        body, name=name,
        out_shape=([pltpu.HBM(a.shape, a.dtype) for a in started["srcs"]]
                   + [pltpu.HBM(l.shape, l.dtype) for l in started["lands"]]),
        in_specs=[_HBM_SPEC] * (2 * n) + [_SEM_SPEC] * 3 + [pl.BlockSpec(memory_space=pl.ANY)],
        out_specs=[_HBM_SPEC] * (2 * n), input_output_aliases={i: i for i in range(2 * n)},
        compiler_params=pltpu.CompilerParams(has_side_effects=_DATAFLOW),
    )(*started["srcs"], *started["lands"], started["send"], started["recv"], started["local"], after)
    return list(outs[n:n + n_g]), list(outs[n + n_g:])


def _tile_spec(shape, layout, t):
    if layout == "nat":
        return pl.BlockSpec((t, shape[1]), lambda i: (i, 0))
    return pl.BlockSpec((shape[0], t, shape[2]), lambda i: (0, i, 0))


def _full_spec(shape):
    nd = len(shape)
    return pl.BlockSpec(tuple(shape), lambda i: (0,) * nd)


def _seq_len(a, layout):
    return a.shape[0] if layout == "nat" else a.shape[1]


def _tile_fwd(name, f, tiled, params, outs, t):
    nt, npar = len(tiled), len(params)
    s = _seq_len(*tiled[0])

    def body(*refs):
        vals = [r[...] for r in refs[:nt + npar]]
        res = f(*vals)
        for r, o in zip(refs[nt + npar:], res):
            r[...] = o.astype(r.dtype)

    return pl.pallas_call(
        body, name=name, grid=(s // t,),
        in_specs=[_tile_spec(a.shape, l, t) for a, l in tiled] + [_full_spec(p.shape) for p in params],
        out_specs=[_tile_spec(sh, l, t) for sh, _, l in outs],
        out_shape=[jax.ShapeDtypeStruct(sh, dt) for sh, dt, _ in outs],
        compiler_params=_params(("arbitrary",)),
    )(*[a for a, _ in tiled], *params)


def _tile_bwd(name, f, tiled, params, cts, t, need, adds=None):
    nt, npar, nc = len(tiled), len(params), len(cts)
    s = _seq_len(*tiled[0])
    need_idx = [k for k in range(nt) if need[k]]
    adds = adds or [None] * len(need_idx)
    add_arrays = [(a, tiled[k][1]) for a, k in zip(adds, need_idx) if a is not None]
    n_add = len(add_arrays)

    def body(*refs):
        i = pl.program_id(0)
        vals = [r[...] for r in refs[:nt + npar]]
        ct_refs = refs[nt + npar:nt + npar + nc]
        add_refs = refs[nt + npar + nc:nt + npar + nc + n_add]
        out_refs = refs[nt + npar + nc + n_add:]
        res, vjp = jax.vjp(f, *vals)
        grads = vjp(tuple(r[...].astype(o.dtype) for r, o in zip(ct_refs, res)))
        a = 0
        for j, k in enumerate(need_idx):
            g = grads[k]
            if adds[j] is not None:
                g = g + add_refs[a][...]
                a += 1
            out_refs[j][...] = g.astype(out_refs[j].dtype)
        for j in range(npar):
            r = out_refs[len(need_idx) + j]

            @pl.when(i == 0)
            def _():
                r[...] = jnp.zeros_like(r)

            r[...] += grads[nt + j]

    outs = pl.pallas_call(
        body, name=name, grid=(s // t,),
        in_specs=([_tile_spec(a.shape, l, t) for a, l in tiled] + [_full_spec(p.shape) for p in params]
                  + [_tile_spec(a.shape, l, t) for a, l in cts] + [_tile_spec(a.shape, l, t) for a, l in add_arrays]),
        out_specs=([_tile_spec(tiled[k][0].shape, tiled[k][1], t) for k in need_idx]
                   + [_full_spec(p.shape) for p in params]),
        out_shape=([jax.ShapeDtypeStruct(tiled[k][0].shape, F32) for k in need_idx]
                   + [jax.ShapeDtypeStruct(p.shape, F32) for p in params]),
        compiler_params=_params(("arbitrary",)),
    )(*[a for a, _ in tiled], *params, *[a for a, _ in cts], *[a for a, _ in add_arrays])
    return list(outs[:len(need_idx)]), list(outs[len(need_idx):])


def _linear_fwd(name, x, w, t, out_layout="nat", residual=None):
    s, k = x.shape
    n = w.shape[1]
    has_res = residual is not None

    def body(*refs):
        x_ref, w_ref = refs[0], refs[1]
        o_ref = refs[-1]
        y = _dot(x_ref[...].astype(BF16), w_ref[...])
        if has_res:
            y = y + refs[2][...]
        if out_layout == "hm":
            for h in range(n // HEAD):
                o_ref[h] = y[:, h * HEAD:(h + 1) * HEAD]
        else:
            o_ref[...] = y

    out_sh = (s, n) if out_layout == "nat" else (n // HEAD, s, HEAD)
    ins = [x, w] + ([residual] if has_res else [])
    in_specs = [_tile_spec(x.shape, "nat", t), _full_spec(w.shape)] + ([_tile_spec((s, n), "nat", t)] if has_res else [])
    return pl.pallas_call(
        body, name=name, grid=(s // t,), in_specs=in_specs,
        out_specs=_tile_spec(out_sh, out_layout, t), out_shape=jax.ShapeDtypeStruct(out_sh, F32),
        compiler_params=_params(("arbitrary",)),
    )(*ins)


def _linear_dx(name, dy, w, t, dy_layout="nat"):
    k, n = w.shape
    s = _seq_len(dy, dy_layout)

    def body(dy_ref, w_ref, o_ref):
        dy = _heads_to_nat(dy_ref[...].astype(BF16)) if dy_layout == "hm" else dy_ref[...].astype(BF16)
        o_ref[...] = _dot(dy, w_ref[...], NT2)

    return pl.pallas_call(
        body, name=name, grid=(s // t,),
        in_specs=[_tile_spec(dy.shape, dy_layout, t), _full_spec(w.shape)],
        out_specs=_tile_spec((s, k), "nat", t), out_shape=jax.ShapeDtypeStruct((s, k), F32),
        compiler_params=_params(("arbitrary",)),
    )(dy, w)


def _linear_dw(name, x, dy, t, nb, dy_layout="nat"):
    s, k = x.shape
    n = dy.shape[1] if dy_layout == "nat" else dy.shape[0] * HEAD

    def body(x_ref, dy_ref, o_ref):
        i = pl.program_id(1)

        @pl.when(i == 0)
        def _():
            o_ref[...] = jnp.zeros_like(o_ref)

        dy = _heads_to_nat(dy_ref[...].astype(BF16)) if dy_layout == "hm" else dy_ref[...].astype(BF16)
        o_ref[...] += _dot(x_ref[...].astype(BF16), dy, TN2)

    if dy_layout == "hm":
        dy_spec = pl.BlockSpec((nb // HEAD, t, HEAD), lambda j, i: (j, i, 0))
    else:
        dy_spec = pl.BlockSpec((t, nb), lambda j, i: (i, j))
    return pl.pallas_call(
        body, name=name, grid=(n // nb, s // t),
        in_specs=[pl.BlockSpec((t, k), lambda j, i: (i, 0)), dy_spec],
        out_specs=pl.BlockSpec((k, nb), lambda j, i: (0, j)), out_shape=jax.ShapeDtypeStruct((k, n), F32),
        compiler_params=_params(("arbitrary", "arbitrary")),
    )(x, dy)


def _ffn_fwd(name, x, gn, wg, wu, wd, t):
    s, d = x.shape
    f8 = wg.shape[-1]

    def body(x_ref, g_ref, wg_ref, wu_ref, wd_ref, o_ref, gk_ref, uk_ref, h_scr, acc):
        k = pl.program_id(1)

        @pl.when(k == 0)
        def _():
            h_scr[...] = _rms(x_ref[...], g_ref[...]).astype(BF16)
            acc[...] = jnp.zeros_like(acc)

        hb = h_scr[...]
        gk = _dot(hb, wg_ref[0])
        uk = _dot(hb, wu_ref[0])
        gk_ref[0] = gk
        uk_ref[0] = uk
        a = gk * jax.nn.sigmoid(gk) * uk
        acc[...] += _dot(a.astype(BF16), wd_ref[0])

        @pl.when(k == N_DEV - 1)
        def _():
            o_ref[...] = x_ref[...] + 0.5 * acc[...]

    wspec = lambda shp: pl.BlockSpec((1,) + shp, lambda i, k: (k, 0, 0))
    act = pl.BlockSpec((1, t, f8), lambda i, k: (k, i, 0))
    act_sh = jax.ShapeDtypeStruct((N_DEV, s, f8), F32)
    return pl.pallas_call(
        body, name=name, grid=(s // t, N_DEV),
        in_specs=[pl.BlockSpec((t, d), lambda i, k: (i, 0)), pl.BlockSpec((1, d), lambda i, k: (0, 0)),
                  wspec((d, f8)), wspec((d, f8)), wspec((f8, d))],
        out_specs=[pl.BlockSpec((t, d), lambda i, k: (i, 0)), act, act],
        out_shape=[jax.ShapeDtypeStruct((s, d), F32), act_sh, act_sh],
        scratch_shapes=[pltpu.VMEM((t, d), BF16), pltpu.VMEM((t, d), F32)],
        compiler_params=_params(("arbitrary", "arbitrary")),
    )(x, gn, wg, wu, wd)


def _ffn_bwd(name, x, dy, gn, wg, wu, wd, gact, uact, t):
    s, d = x.shape
    f8 = wg.shape[-1]
    last = N_DEV - 1

    def body(x_ref, dy_ref, g_ref, wg_ref, wu_ref, wd_ref, gk_ref, uk_ref,
             dx_ref, dg_ref, dwg_ref, dwu_ref, dwd_ref, dh_scr):
        k, i = pl.program_id(0), pl.program_id(1)
        x = x_ref[...]
        rs = lax.rsqrt(jnp.mean(x * x, axis=-1, keepdims=True) + NORM_EPS)
        xn = x * rs
        hb = (xn * g_ref[...]).astype(BF16)
        dob = (0.5 * dy_ref[...]).astype(BF16)
        wgk, wuk, wdk = wg_ref[0], wu_ref[0], wd_ref[0]
        gk, uk = gk_ref[0], uk_ref[0]
        sg = jax.nn.sigmoid(gk)
        sk = gk * sg
        da = _dot(dob, wdk, NT2)
        du = (da * sk).astype(BF16)
        dg = (da * uk * (sg * (1.0 + gk * (1.0 - sg)))).astype(BF16)

        @pl.when(i == 0)
        def _():
            dwg_ref[...] = jnp.zeros_like(dwg_ref)
            dwu_ref[...] = jnp.zeros_like(dwu_ref)
            dwd_ref[...] = jnp.zeros_like(dwd_ref)

        dwd_ref[0] += _dot((sk * uk).astype(BF16), dob, TN2)
        dwg_ref[0] += _dot(hb, dg, TN2)
        dwu_ref[0] += _dot(hb, du, TN2)
        dh = _dot(dg, wgk, NT2) + _dot(du, wuk, NT2)
        rows = pl.ds(pl.multiple_of(i * t, t), t)

        @pl.when(k == 0)
        def _():
            dh_scr[rows, :] = dh

        @pl.when(k > 0)
        def _():
            dh_scr[rows, :] += dh

        @pl.when(jnp.logical_and(k == last, i == 0))
        def _():
            dg_ref[...] = jnp.zeros_like(dg_ref)

        @pl.when(k == last)
        def _():
            dht = dh_scr[rows, :]
            dg_ref[...] += jnp.sum(dht * xn, axis=0, keepdims=True)
            dxn = dht * g_ref[...]
            dx_ref[...] = dy_ref[...] + rs * (dxn - xn * jnp.mean(dxn * xn, axis=-1, keepdims=True))

    wspec = lambda shp: pl.BlockSpec((1,) + shp, lambda k, i: (k, 0, 0))
    tile = pl.BlockSpec((t, d), lambda k, i: (i, 0))
    act = pl.BlockSpec((1, t, f8), lambda k, i: (k, i, 0))
    return pl.pallas_call(
        body, name=name, grid=(N_DEV, s // t),
        in_specs=[tile, tile, pl.BlockSpec((1, d), lambda k, i: (0, 0)), wspec((d, f8)), wspec((d, f8)), wspec((f8, d)),
                  act, act],
        out_specs=[pl.BlockSpec((t, d), lambda k, i: (jnp.where(k == last, i, 0), 0)),
                   pl.BlockSpec((1, d), lambda k, i: (0, 0)),
                   pl.BlockSpec((1, d, f8), lambda k, i: (k, 0, 0)), pl.BlockSpec((1, d, f8), lambda k, i: (k, 0, 0)),
                   pl.BlockSpec((1, f8, d), lambda k, i: (k, 0, 0))],
        out_shape=[jax.ShapeDtypeStruct((s, d), F32), jax.ShapeDtypeStruct((1, d), F32),
                   jax.ShapeDtypeStruct((N_DEV, d, f8), F32), jax.ShapeDtypeStruct((N_DEV, d, f8), F32),
                   jax.ShapeDtypeStruct((N_DEV, f8, d), F32)],
        scratch_shapes=[pltpu.VMEM((s, d), F32)],
        compiler_params=_params(("arbitrary", "arbitrary")),
    )(x, dy, gn, wg, wu, wd, gact, uact)


def _loss_head(y, target, t):
    s, d = y.shape

    def body(y_ref, t_ref, dy_ref, l_ref):
        i = pl.program_id(0)
        err = y_ref[...] - t_ref[...]
        dy_ref[...] = err * (1.0 / d)

        @pl.when(i == 0)
        def _():
            l_ref[...] = jnp.zeros_like(l_ref)

        l_ref[...] += 0.5 * jnp.sum(jnp.mean(err * err, axis=-1, keepdims=True), axis=0, keepdims=True)

    tile = pl.BlockSpec((t, d), lambda i: (i, 0))
    dy, l = pl.pallas_call(
        body, name="loss_head", grid=(s // t,), in_specs=[tile, tile],
        out_specs=[tile, pl.BlockSpec((1, 1), lambda i: (0, 0))],
        out_shape=[jax.ShapeDtypeStruct((s, d), F32), jax.ShapeDtypeStruct((1, 1), F32)],
        compiler_params=_params(("arbitrary",)),
    )(y, target)
    return dy, l[0, 0]


SB_KEY_TILE = 1024
SB_HEADS_PER_STEP = 4


def _sb_scan_mats():
    row = lax.broadcasted_iota(jnp.int32, (QBLK, QBLK), 0)
    col = lax.broadcasted_iota(jnp.int32, (QBLK, QBLK), 1)
    return (row > col).astype(F32).astype(BF16), (row < col).astype(F32).astype(BF16)


def _sb_tile_scan(x, mat, reverse):
    nsub = x.shape[1] // QBLK
    outs, carry = [None] * nsub, jnp.zeros((x.shape[0], 1), F32)
    for i in (reversed(range(nsub)) if reverse else range(nsub)):
        xs = x[:, i * QBLK:(i + 1) * QBLK]
        hi = xs.astype(BF16)
        lo = (xs - hi.astype(F32)).astype(BF16)
        outs[i] = _dot(hi, mat) + _dot(lo, mat) + carry
        carry = carry + jnp.sum(xs, axis=1, keepdims=True)
    return jnp.concatenate(outs, axis=1), carry


def _sb_before_query(n, t, kt):
    row = lax.broadcasted_iota(jnp.int32, (QBLK, kt), 0)
    col = lax.broadcasted_iota(jnp.int32, (QBLK, kt), 1)
    return t * kt + col < n * QBLK + row


def _sb_fwd(q, k, v):
    _, s, _ = q.shape
    scale = HEAD ** -0.5
    kt = min(SB_KEY_TILE, s)

    def body(q_ref, k_ref, v_ref, o_ref):
        n = pl.program_id(1)
        suffix, _ = _sb_scan_mats()
        n_tiles = lax.div(n, jnp.int32(kt // QBLK)) + 1
        heads = range(SB_HEADS_PER_STEP)
        qb = [(q_ref[h] * scale).astype(q_ref.dtype) for h in heads]

        def tile(t, carry, diagonal):
            rows = pl.ds(pl.multiple_of(t * kt, kt), kt)
            out = []
            for h in heads:
                c, acc = carry[h]
                z = _dot(qb[h], k_ref[h, rows, :], NT2)
                lk = _log_sigmoid(-z)
                log_beta = z + lk
                if diagonal:
                    ok = _sb_before_query(n, t, kt)
                    lk = jnp.where(ok, lk, 0.0)
                later, total = _sb_tile_scan(lk, suffix, True)
                w = jnp.exp(log_beta + later + c)
                if diagonal:
                    w = jnp.where(ok, w, 0.0)
                out.append((c + total, acc + _dot(w.astype(BF16), v_ref[h, rows, :])))
            return tuple(out)

        zero = (jnp.zeros((QBLK, 1), F32), jnp.zeros((QBLK, HEAD), F32))
        carry = tile(n_tiles - 1, (zero,) * len(heads), True)
        carry = lax.fori_loop(1, n_tiles, lambda jj, cr: tile(n_tiles - 1 - jj, cr, False), carry)
        for h in heads:
            o_ref[h] = carry[h][1]

    hp = SB_HEADS_PER_STEP
    return pl.pallas_call(
        body, name="sb_fwd", grid=(SB_HEADS // hp, s // QBLK),
        in_specs=[pl.BlockSpec((hp, QBLK, HEAD), lambda h, n: (h, n, 0)),
                  pl.BlockSpec((hp, s, HEAD), lambda h, n: (h, 0, 0)),
                  pl.BlockSpec((hp, s, HEAD), lambda h, n: (h, 0, 0))],
        out_specs=pl.BlockSpec((hp, QBLK, HEAD), lambda h, n: (h, n, 0)),
        out_shape=jax.ShapeDtypeStruct((SB_HEADS, s, HEAD), F32),
        compiler_params=_params(("arbitrary", "arbitrary")),
    )(q, k, v)


def _sb_bwd(q, k, v, do):
    _, s, _ = q.shape
    scale = HEAD ** -0.5
    kt = min(SB_KEY_TILE, s)

    def body(q_ref, k_ref, v_ref, do_ref, dq_ref, dk_ref, dv_ref, e_scr, beta_scr):
        n = pl.program_id(1)

        @pl.when(n == 0)
        def _():
            dk_ref[...] = jnp.zeros_like(dk_ref)
            dv_ref[...] = jnp.zeros_like(dv_ref)

        suffix, prefix = _sb_scan_mats()
        n_tiles = lax.div(n, jnp.int32(kt // QBLK)) + 1
        heads = range(SB_HEADS_PER_STEP)
        qb = [(q_ref[h] * scale).astype(q_ref.dtype) for h in heads]
        dob = [do_ref[h].astype(BF16) for h in heads]

        def weights(t, cs, diagonal):
            rows = pl.ds(pl.multiple_of(t * kt, kt), kt)
            out, stores = [], []
            for h in heads:
                vb = v_ref[h, rows, :]
                z = _dot(qb[h], k_ref[h, rows, :], NT2)
                lk = _log_sigmoid(-z)
                log_beta = z + lk
                if diagonal:
                    ok = _sb_before_query(n, t, kt)
                    lk = jnp.where(ok, lk, 0.0)
                later, total = _sb_tile_scan(lk, suffix, True)
                w = jnp.exp(log_beta + later + cs[h])
                if diagonal:
                    w = jnp.where(ok, w, 0.0)
                stores.append((w * _dot(dob[h], vb, NT2), jnp.exp(log_beta), _dot(w.astype(BF16), dob[h], TN2)))
                out.append(cs[h] + total)
            for h in heads:
                e_scr[h, t], beta_scr[h, t] = stores[h][0], stores[h][1]
                dv_ref[h, rows, :] += stores[h][2]
            return tuple(out)

        col0 = jnp.zeros((QBLK, 1), F32)
        cs = weights(n_tiles - 1, (col0,) * len(heads), True)
        lax.fori_loop(1, n_tiles, lambda jj, c: weights(n_tiles - 1 - jj, c, False), cs)

        def grads(t, carry, diagonal):
            rows = pl.ds(pl.multiple_of(t * kt, kt), kt)
            out, dks = [], []
            for h in heads:
                pc, dq = carry[h]
                kb = k_ref[h, rows, :]
                e, beta = e_scr[h, t], beta_scr[h, t]
                before, total = _sb_tile_scan(e, prefix, False)
                dz = e * (1.0 - beta) - beta * (before + pc)
                if diagonal:
                    dz = jnp.where(_sb_before_query(n, t, kt), dz, 0.0)
                dz = dz.astype(BF16)
                dks.append(_dot(dz, qb[h], TN2))
                out.append((pc + total, dq + _dot(dz, kb)))
            for h in heads:
                dk_ref[h, rows, :] += dks[h]
            return tuple(out)

        zero = (col0, jnp.zeros((QBLK, HEAD), F32))
        carry = lax.fori_loop(0, n_tiles - 1, lambda t, cr: grads(t, cr, False), (zero,) * len(heads))
        carry = grads(n_tiles - 1, carry, True)
        for h in heads:
            dq_ref[h] = carry[h][1] * scale

    hp = SB_HEADS_PER_STEP
    qspec = pl.BlockSpec((hp, QBLK, HEAD), lambda h, n: (h, n, 0))
    full = pl.BlockSpec((hp, s, HEAD), lambda h, n: (h, 0, 0))
    sh = jax.ShapeDtypeStruct((SB_HEADS, s, HEAD), F32)
    tiles_sh = (hp, s // kt, QBLK, kt)
    return pl.pallas_call(
        body, name="sb_bwd", grid=(SB_HEADS // hp, s // QBLK),
        in_specs=[qspec, full, full, qspec],
        out_specs=[qspec, full, full], out_shape=[sh, sh, sh],
        scratch_shapes=[pltpu.VMEM(tiles_sh, F32), pltpu.VMEM(tiles_sh, F32)],
        compiler_params=_params(("arbitrary", "arbitrary")),
    )(q, k, v, do)


def _t5_bucket_np(dist):
    max_exact = N_BUCKETS // 2
    d = np.maximum(dist, 1).astype(np.float32)
    large = max_exact + (np.log(d / np.float32(max_exact)) / np.float32(math.log(MAX_DISTANCE / max_exact))
                         * np.float32(N_BUCKETS - max_exact)).astype(np.int32)
    large = np.minimum(large, N_BUCKETS - 1)
    return np.where(dist < max_exact, dist, large)


def _dil_layout(s):
    assert all(s % (QBLK * r) == 0 and window // r == QBLK for window, r in DIL_PATTERNS)
    tiles, buckets = [], []
    i = np.arange(QBLK)[:, None]
    j = np.arange(QBLK)[None, :]
    for g, (window, r) in enumerate(DIL_PATTERNS):
        for off in (0, 1):
            dist = QBLK * off + i - j
            ok = (dist >= 0) & (dist <= window // r)
            tiles.append((g, off))
            buckets.append(np.where(ok, _t5_bucket_np(np.maximum(dist, 0) * r), -1).astype(np.int32))
    return tiles, np.stack(buckets)


def _bias_tiles(rel_bias, s):
    tiles, buckets = _dil_layout(s)
    nt = len(tiles)
    present = [sorted(set(np.unique(buckets[k]).tolist()) - {-1}) for k in range(nt)]

    def body(rel_ref, b_ref, o_ref):
        j = pl.program_id(0)
        for k, (g, _) in enumerate(tiles):
            bk = b_ref[k]
            tile = jnp.full((QBLK, QBLK), NEG_INF, F32)
            for b in present[k]:
                tile = jnp.where(bk == b, rel_ref[b, g * DIL_GROUP + j], tile)
            o_ref[0, k] = tile

    return pl.pallas_call(
        body, name="bias_tiles", grid=(DIL_GROUP,),
        in_specs=[pl.BlockSpec(memory_space=pltpu.SMEM), pl.BlockSpec((nt, QBLK, QBLK), lambda j: (0, 0, 0))],
        out_specs=pl.BlockSpec((1, nt, QBLK, QBLK), lambda j: (j, 0, 0, 0)),
        out_shape=jax.ShapeDtypeStruct((DIL_GROUP, nt, QBLK, QBLK), F32),
        compiler_params=_params(("arbitrary",)),
    )(rel_bias, jnp.asarray(buckets))


def _bias_tiles_bwd(dbias, s):
    tiles, buckets = _dil_layout(s)
    nt = len(tiles)
    present = [sorted(set(np.unique(buckets[k]).tolist()) - {-1}) for k in range(nt)]

    def body(d_ref, b_ref, o_ref):
        j = pl.program_id(0)

        @pl.when(j == 0)
        def _():
            for b in range(N_BUCKETS):
                for col in range(3 * DIL_GROUP):
                    o_ref[b, col] = jnp.float32(0.0)

        for k, (g, _) in enumerate(tiles):
            bk, dk = b_ref[k], d_ref[0, k]
            for b in present[k]:
                o_ref[b, g * DIL_GROUP + j] += jnp.sum(jnp.where(bk == b, dk, 0.0))

    return pl.pallas_call(
        body, name="bias_tiles_bwd", grid=(DIL_GROUP,),
        in_specs=[pl.BlockSpec((1, nt, QBLK, QBLK), lambda j: (j, 0, 0, 0)),
                  pl.BlockSpec((nt, QBLK, QBLK), lambda j: (0, 0, 0))],
        out_specs=pl.BlockSpec(memory_space=pltpu.SMEM),
        out_shape=jax.ShapeDtypeStruct((N_BUCKETS, 3 * DIL_GROUP), F32),
        compiler_params=_params(("arbitrary",)),
    )(dbias, jnp.asarray(buckets))


DIL_PAIRS_PER_STEP = 8


def _dil_rows(g, s, pair):
    _, r = DIL_PATTERNS[g]
    nb = s // (QBLK * r)
    c, n = lax.div(pair, jnp.int32(nb)), lax.rem(pair, jnp.int32(nb))
    start = c + (r * QBLK) * n
    before = jnp.where(n > 0, start - r * QBLK, start)
    if r == 1:
        return pl.ds(start, QBLK), pl.ds(before, QBLK), n > 0
    return pl.ds(start, QBLK, stride=r), pl.ds(before, QBLK, stride=r), n > 0


def _dil_logits(qb, k_ref, rows, before, has_before, b_ref):
    k0, k1 = k_ref[0, rows, :].astype(BF16), k_ref[0, before, :].astype(BF16)
    l0 = _dot(qb, k0, NT2) + b_ref[0, 0]
    l1 = jnp.where(has_before, _dot(qb, k1, NT2) + b_ref[0, 1], NEG_INF)
    return k0, k1, l0, l1


def _dil_group_specs(g, s):
    head = pl.BlockSpec((1, s, HEAD), lambda j, p: (DIL_GROUP * g + j, 0, 0))
    return [head, head, head, pl.BlockSpec((1, 2, QBLK, QBLK), lambda j, p: (j, g, 0, 0))]


def _dil_group_fwd(g, qn, kn, v, bias):
    _, s, _ = qn.shape
    scale = HEAD ** -0.5
    steps = (s // QBLK) // DIL_PAIRS_PER_STEP

    def body(q_ref, k_ref, v_ref, b_ref, o_ref):
        pairs = [_dil_rows(g, s, pl.program_id(1) * DIL_PAIRS_PER_STEP + u) for u in range(DIL_PAIRS_PER_STEP)]
        loaded = []
        for rows, before, has_before in pairs:
            qb = (q_ref[0, rows, :] * scale).astype(BF16)
            _, _, l0, l1 = _dil_logits(qb, k_ref, rows, before, has_before, b_ref)
            loaded.append((l0, l1, v_ref[0, rows, :].astype(BF16), v_ref[0, before, :].astype(BF16)))
        results = []
        for l0, l1, v0, v1 in loaded:
            m = jnp.max(jnp.maximum(l0, l1), axis=1, keepdims=True)
            p0, p1 = jnp.exp(l0 - m), jnp.exp(l1 - m)
            den = jnp.sum(p0 + p1, axis=1, keepdims=True)
            inv = 1.0 / den
            o = _dot((p0 * inv).astype(BF16), v0) + _dot((p1 * inv).astype(BF16), v1)
            results.append(jnp.concatenate([o, jnp.broadcast_to(m + jnp.log(den), (QBLK, HEAD))], axis=1))
        for (rows, _, _), res in zip(pairs, results):
            o_ref[0, rows, :] = res

    return pl.pallas_call(
        body, name="dil%d_fwd" % g, grid=(DIL_GROUP, steps), in_specs=_dil_group_specs(g, s),
        out_specs=pl.BlockSpec((1, s, 2 * HEAD), lambda j, p: (j, 0, 0)),
        out_shape=jax.ShapeDtypeStruct((DIL_GROUP, s, 2 * HEAD), F32),
        compiler_params=_params(("arbitrary", "arbitrary")),
    )(qn, kn, v, bias)


def _dil_group_bwd(g, qn, kn, v, bias, ol, dol, prev):
    _, s, _ = qn.shape
    scale = HEAD ** -0.5
    steps = (s // QBLK) // DIL_PAIRS_PER_STEP
    prev = list(prev) if prev is not None else []

    def body(q_ref, k_ref, v_ref, b_ref, ol_ref, dol_ref, *rest):
        dq_ref, dk_ref, dv_ref, db_ref = rest[-4:]

        @pl.when(pl.program_id(1) == 0)
        def _():
            for r in (dk_ref, dv_ref, db_ref):
                r[...] = jnp.zeros_like(r)

        pairs = [_dil_rows(g, s, pl.program_id(1) * DIL_PAIRS_PER_STEP + u) for u in range(DIL_PAIRS_PER_STEP)]
        loaded = []
        for rows, before, has_before in pairs:
            qb = (q_ref[0, rows, :] * scale).astype(BF16)
            k0, k1, l0, l1 = _dil_logits(qb, k_ref, rows, before, has_before, b_ref)
            v0, v1 = v_ref[0, rows, :].astype(BF16), v_ref[0, before, :].astype(BF16)
            loaded.append((qb, k0, k1, l0, l1, v0, v1, ol_ref[0, rows, :], dol_ref[0, rows, :]))
        grads = []
        for qb, k0, k1, l0, l1, v0, v1, out_lse, d_out_lse in loaded:
            o, lse = out_lse[:, :HEAD], out_lse[:, HEAD:HEAD + 1]
            do, dlse = d_out_lse[:, :HEAD], d_out_lse[:, HEAD:HEAD + 1]
            dob = do.astype(BF16)
            p0, p1 = jnp.exp(l0 - lse), jnp.exp(l1 - lse)
            shift = dlse - jnp.sum(do * o, axis=1, keepdims=True)
            dl0 = p0 * (_dot(dob, v0, NT2) + shift)
            dl1 = p1 * (_dot(dob, v1, NT2) + shift)
            dl0b, dl1b = dl0.astype(BF16), dl1.astype(BF16)
            grads.append(((_dot(dl0b, k0) + _dot(dl1b, k1)) * scale,
                          _dot(dl0b, qb, TN2), _dot(dl1b, qb, TN2),
                          _dot(p0.astype(BF16), dob, TN2), _dot(p1.astype(BF16), dob, TN2), dl0, dl1))
        db0 = functools.reduce(jnp.add, [gr[5] for gr in grads])
        db1 = functools.reduce(jnp.add, [gr[6] for gr in grads])
        for (rows, before, _), (dq, dk0, dk1, dv0, dv1, _, _) in zip(pairs, grads):
            dq_ref[0, rows, :] = dq
            dk_ref[0, rows, :] += dk0
            dk_ref[0, before, :] += dk1
            dv_ref[0, rows, :] += dv0
            dv_ref[0, before, :] += dv1
        db_ref[0, 0] += db0
        db_ref[0, 1] += db1

    head_out = pl.BlockSpec((1, s, HEAD), lambda j, p: (DIL_GROUP * g + j, 0, 0))
    rows128 = pl.BlockSpec((1, s, 2 * HEAD), lambda j, p: (j, 0, 0))
    full_sh = jax.ShapeDtypeStruct(qn.shape, F32)
    return pl.pallas_call(
        body, name="dil%d_bwd" % g, grid=(DIL_GROUP, steps),
        in_specs=_dil_group_specs(g, s) + [rows128, rows128] + [pl.BlockSpec(memory_space=pl.ANY)] * len(prev),
        out_specs=[head_out, head_out, head_out, pl.BlockSpec((1, 2, QBLK, QBLK), lambda j, p: (j, 0, 0, 0))],
        out_shape=[full_sh, full_sh, full_sh, jax.ShapeDtypeStruct((DIL_GROUP, 2, QBLK, QBLK), F32)],
        input_output_aliases={6 + i: i for i in range(len(prev))},
        compiler_params=_params(("arbitrary", "arbitrary")),
    )(qn, kn, v, bias, ol, dol, *prev)


@functools.partial(jax.custom_vjp, nondiff_argnums=(2,))
def _bdot(a, b, dims):
    return _dot(a.astype(BF16), b.astype(BF16), dims)


def _bdot_fwd(a, b, dims):
    return _bdot(a, b, dims), (a, b)


def _bdot_bwd(dims, res, dc):
    a, b = res
    nn, nt, tn = (NN2, NT2, TN2) if dims in (NN2, NT2, TN2) else (NN3, NT3, TN3)
    if dims == nn:
        return _bdot(dc, b, nt), _bdot(a, dc, tn)
    if dims == nt:
        return _bdot(dc, b, nn), _bdot(dc, a, tn)
    return _bdot(b, dc, nt), _bdot(a, dc, nn)


_bdot.defvjp(_bdot_fwd, _bdot_bwd)


def _ones_dot(ones, x, dims):
    o = ones.astype(BF16)
    hi = x.astype(BF16)
    r1 = x - hi.astype(F32)
    mid = r1.astype(BF16)
    lo = (r1 - mid.astype(F32)).astype(BF16)
    return _dot(o, hi, dims) + _dot(o, mid, dims) + _dot(o, lo, dims)


@jax.custom_vjp
def _prefix_sums(x):
    c = x.shape[1]
    row = lax.broadcasted_iota(jnp.int32, (x.shape[0], c, c), 1)
    col = lax.broadcasted_iota(jnp.int32, (x.shape[0], c, c), 2)
    return _ones_dot((row >= col).astype(F32), x, NN3)


def _prefix_sums_fwd(x):
    return _prefix_sums(x), None


def _prefix_sums_bwd(_, dy):
    c = dy.shape[1]
    row = lax.broadcasted_iota(jnp.int32, (dy.shape[0], c, c), 1)
    col = lax.broadcasted_iota(jnp.int32, (dy.shape[0], c, c), 2)
    return (_ones_dot((row <= col).astype(F32), dy, NN3),)


_prefix_sums.defvjp(_prefix_sums_fwd, _prefix_sums_bwd)


def _rwkv_chunk(s0, r, lw, kraw, v, ag, kk_w, ka_w, rk_w, lng, lnb):
    hb, c, _ = r.shape
    kk = kraw * kk_w
    kk = kk / jnp.maximum(jnp.sqrt(jnp.sum(kk * kk, axis=-1, keepdims=True)), 1e-12)
    k = kraw * (1.0 + (ag - 1.0) * ka_w)
    a = -kk
    b = kk * ag
    row = lax.broadcasted_iota(jnp.int32, (hb, c, c), 1)
    col = lax.broadcasted_iota(jnp.int32, (hb, c, c), 2)
    lower, strict = row >= col, row > col
    cum = _prefix_sums(lw)
    ecum, einv = jnp.exp(cum), jnp.exp(-cum)
    rt, kt, bt = r * ecum, k * einv, b * einv
    at = a * jnp.exp(cum - lw)
    ar = jnp.concatenate([at, rt], axis=1)
    scores = _bdot(ar, jnp.concatenate([bt, kt], axis=1), NT3)
    a_ab = jnp.where(strict, scores[:, :c, :c], 0.0)
    a_ak = jnp.where(strict, scores[:, :c, c:], 0.0)
    p_rb = jnp.where(lower, scores[:, c:, :c], 0.0)
    p_rk = jnp.where(lower, scores[:, c:, c:], 0.0)
    from_s0 = _bdot(ar, s0, NT3)
    rhs = from_s0[:, :c] + _bdot(a_ak, v, NN3)
    inv = (row == col).astype(F32) + a_ab
    pw = a_ab
    for _ in range(int(math.log2(c)) - 1):
        pw = _bdot(pw, pw, NN3)
        inv = inv + _bdot(inv, pw, NN3)
    u = _bdot(inv, rhs, NN3)
    uv = jnp.concatenate([u, v], axis=1)
    y = from_s0[:, c:] + _bdot(jnp.concatenate([p_rb, p_rk], axis=2), uv, NN3)
    cum_end = cum[:, c - 1:c, :]
    dec = jnp.exp(cum_end - cum)
    s_end = s0 * jnp.exp(cum_end) + _bdot(uv, jnp.concatenate([b * dec, k * dec], axis=1), TN3)
    mu = jnp.mean(y, axis=-1, keepdims=True)
    var = jnp.mean(jnp.square(y - mu), axis=-1, keepdims=True)
    z = (y - mu) * lax.rsqrt(var + GN_EPS) * lng + lnb + jnp.sum(r * k * rk_w, axis=-1, keepdims=True) * v
    return z, s_end


def _rwkv_specs(nc, rev):
    cidx = (lambda c: nc - 1 - c) if rev else (lambda c: c)
    seq = pl.BlockSpec((RW_HB, RW_CHUNK, HEAD), lambda hg, c: (hg, cidx(c), 0))
    par = pl.BlockSpec((RW_HB, 1, HEAD), lambda hg, c: (hg, 0, 0))
    st = pl.BlockSpec((1, RW_HB, HEAD, HEAD), lambda hg, c: (cidx(c), hg, 0, 0))
    return seq, par, st


def _rwkv_fwd(seqs, pars):
    s = seqs[0].shape[1]
    nc = s // RW_CHUNK

    def body(*refs):
        seq_refs, par_refs = refs[:5], refs[5:10]
        z_ref, st_ref, state = refs[10:]
        c = pl.program_id(1)

        @pl.when(c == 0)
        def _():
            state[...] = jnp.zeros_like(state)

        s0 = state[...]
        st_ref[0] = s0
        z, s_end = _rwkv_chunk(s0, *[r[...] for r in seq_refs], *[r[...] for r in par_refs])
        z_ref[...] = z
        state[...] = s_end

    seq, par, st = _rwkv_specs(nc, False)
    return pl.pallas_call(
        body, name="rwkv_fwd", grid=(N_HEADS // RW_HB, nc),
        in_specs=[seq] * 5 + [par] * 5, out_specs=[seq, st],
        out_shape=[jax.ShapeDtypeStruct((N_HEADS, s, HEAD), F32), jax.ShapeDtypeStruct((nc, N_HEADS, HEAD, HEAD), F32)],
        scratch_shapes=[pltpu.VMEM((RW_HB, HEAD, HEAD), F32)],
        compiler_params=_params(("arbitrary", "arbitrary")),
    )(*seqs, *pars)


def _rwkv_bwd(seqs, pars, states, dz):
    s = seqs[0].shape[1]
    nc = s // RW_CHUNK

    def body(*refs):
        seq_refs, par_refs = refs[:5], refs[5:10]
        st_ref, dz_ref = refs[10:12]
        dseq_refs, dpar_refs, dstate = refs[12:17], refs[17:22], refs[22]
        c = pl.program_id(1)

        @pl.when(c == 0)
        def _():
            dstate[...] = jnp.zeros_like(dstate)
            for r in dpar_refs:
                r[...] = jnp.zeros_like(r)

        _, vjp = jax.vjp(_rwkv_chunk, st_ref[0], *[r[...] for r in seq_refs], *[r[...] for r in par_refs])
        g = vjp((dz_ref[...], dstate[...]))
        dstate[...] = g[0]
        for r, gs in zip(dseq_refs, g[1:6]):
            r[...] = gs
        for r, gp in zip(dpar_refs, g[6:]):
            r[...] += gp

    seq, par, st = _rwkv_specs(nc, True)
    seq_sh = jax.ShapeDtypeStruct((N_HEADS, s, HEAD), F32)
    par_sh = jax.ShapeDtypeStruct((N_HEADS, 1, HEAD), F32)
    outs = pl.pallas_call(
        body, name="rwkv_bwd", grid=(N_HEADS // RW_HB, nc),
        in_specs=[seq] * 5 + [par] * 5 + [st, seq],
        out_specs=[seq] * 5 + [par] * 5, out_shape=[seq_sh] * 5 + [par_sh] * 5,
        scratch_shapes=[pltpu.VMEM((RW_HB, HEAD, HEAD), F32)],
        compiler_params=_params(("arbitrary", "arbitrary")),
    )(*seqs, *pars, states, dz)
    return list(outs[:5]), list(outs[5:])


def _norm_fn(x, g):
    return (_rms(x, g),)


def _attn_prep_fn(proj, qn_w, kn_w):
    a, b = SB_HEADS, 3 * DIL_GROUP
    return (proj[0:a], proj[a:2 * a], proj[2 * a:3 * a],
            _rms(proj[3 * a:3 * a + b], qn_w), _rms(proj[3 * a + b:3 * a + 2 * b], kn_w), proj[3 * a + 2 * b:])


def _attn_merge_fn(o_sb, ol0, ol1, ol2):
    groups = (ol0, ol1, ol2)
    merged = []
    for j in range(DIL_GROUP):
        lses = [ol[j][:, HEAD:HEAD + 1] for ol in groups]
        m = functools.reduce(jnp.maximum, lses)
        es = [jnp.exp(l - m) for l in lses]
        inv = 1.0 / functools.reduce(jnp.add, es)
        merged.append(functools.reduce(jnp.add, [(e * inv) * ol[j][:, :HEAD] for e, ol in zip(es, groups)]))
    return (jnp.concatenate([_heads_to_nat(o_sb)] + merged, axis=-1),)


def _rw_mix_fn(x, xp, gn, mix, w0, w1, w2, a0, a1, a2, g1, g2):
    h = _rms(x, gn)
    xx = _rms(xp, gn) - h
    xr, xw, xk, xv, xa, xg = [h + xx * mix[i:i + 1] for i in range(6)]
    w_log = -jax.nn.softplus(-(w0 + _mm(jnp.tanh(_mm(xw, w1)), w2))) - 0.5
    lw = -jnp.exp(w_log)
    ag = jax.nn.sigmoid(a0 + _mm(_mm(xa, a1), a2))
    gate = _mm(jax.nn.sigmoid(_mm(xg, g1)), g2)
    return xr, xk, xv, _nat_to_heads(lw), _nat_to_heads(ag), gate


def _rw_gate_fn(z, gate):
    return (_heads_to_nat(z) * gate,)


def _adamw(name, w, m, v, gparts, row0=0, prev=None):
    big_r, c = w.shape
    r = gparts.shape[1]
    tr = r
    if r % 8 == 0:
        tr = max(t for t in range(8, r + 1, 8) if r % t == 0 and (t * c * 4 <= (1 << 20) or t == 8))
    assert row0 % tr == 0 and (r == big_r or r % 8 == 0)
    off = row0 // tr

    def body(w_ref, m_ref, v_ref, g_ref, *rest):
        go_ref, d_ref, mo_ref, vo_ref = rest[-4:]
        g = g_ref[0].astype(F32)
        for j in range(1, N_DEV):
            g = g + g_ref[j].astype(F32)
        mn = ADAM_B1 * m_ref[...] + (1.0 - ADAM_B1) * g
        vn = ADAM_B2 * v_ref[...] + (1.0 - ADAM_B2) * jnp.square(g)
        m_hat = mn / (1.0 - ADAM_B1 ** ADAM_STEP)
        v_hat = vn / (1.0 - ADAM_B2 ** ADAM_STEP)
        go_ref[...] = g
        d_ref[...] = -ADAM_LR * (m_hat / (jnp.sqrt(v_hat) + ADAM_EPS) + ADAM_WD * w_ref[...])
        mo_ref[...] = mn
        vo_ref[...] = vn

    tile = pl.BlockSpec((tr, c), lambda i: (i + off, 0))
    sh = jax.ShapeDtypeStruct((big_r, c), F32)
    prev = list(prev) if prev is not None else []
    return pl.pallas_call(
        body, name=name, grid=(r // tr,),
        in_specs=([tile, tile, tile, pl.BlockSpec((N_DEV, tr, c), lambda i: (0, i, 0))]
                  + [pl.BlockSpec(memory_space=pl.ANY)] * len(prev)),
        out_specs=[tile] * 4, out_shape=[sh] * 4,
        input_output_aliases={4 + j: j for j in range(len(prev))},
        compiler_params=_params(("arbitrary",)),
    )(w, m, v, gparts, *prev)


def _col_blocks_to_nat(g):
    return jnp.moveaxis(g, 0, 1).reshape(g.shape[1], -1)


def _nat_to_col_blocks(a):
    return jnp.moveaxis(a.reshape(a.shape[0], N_DEV, -1), 1, 0)


AG_GROUPS = ("f00", "att", "f01", "f10", "rw", "f11")
RS_GROUPS = ("f11", "rw", "f10", "f01", "f00", "att")
BF16_GRAD_GROUPS = ("att", "f00")
RW_SHARDED = ('rw_mix', 'rw_w0', 'rw_w1', 'rw_w2', 'rw_a0', 'rw_a1', 'rw_a2', 'rw_g1', 'rw_g2', 'rw_kk', 'rw_ka',
              'rw_wr', 'rw_wk', 'rw_wv', 'rw_wo', 'rw_lnx_g', 'rw_lnx_b')


def _step(x, target, rep, get, put):
    tied = lambda a, zero: a + zero[0, 0].astype(a.dtype)
    s, d = x.shape
    tf = min(512, s)
    tt = min(256, s)
    row = lambda a: a.reshape(1, -1)
    mix_norm = rep["mix_norm"]
    ffw = {(0, 0): get("f00", None)}
    ffn_norm = _col_blocks_to_nat(ffw[(0, 0)]["ffn_norm"].reshape(N_DEV, 4, -1))

    acts = {}

    def ffn(nm, xin, l, h):
        g = ffw[(l, h)]
        out, *acts[(l, h)] = _ffn_fwd(nm, xin, ffn_norm[2 * l + h][None], g["gate"], g["up"], g["down"], min(2 * tf, s))
        return out

    x1 = ffn("ffn00_fwd", x, 0, 0)
    att = get("att", x1)
    w_in = _col_blocks_to_nat(att["attn_w_in"])
    w_out = _col_blocks_to_nat(att["attn_w_out"])
    (h0,) = _tile_fwd("mixnorm0_fwd", _norm_fn, [(x1, "nat")], [mix_norm[0:1]], [((s, d), BF16, "nat")], tt)
    proj = _linear_fwd("attn_in_fwd", h0, w_in, tt, out_layout="hm")
    bias = _bias_tiles(rep["rel_bias"], s)
    prep_pars = [rep["attn_q_norm"], rep["attn_k_norm"]]
    sb_sh, dl_sh = (SB_HEADS, s, HEAD), (3 * DIL_GROUP, s, HEAD)
    sq, sk, sv, qn, kn, vd = _tile_fwd("attn_prep_fwd", _attn_prep_fn, [(proj, "hm")], prep_pars,
                                       [(sb_sh, BF16, "hm")] * 3 + [(dl_sh, F32, "hm")] * 3, tt // 2)
    o_sb = _sb_fwd(sq, sk, sv)
    ols = [_dil_group_fwd(g, qn, kn, vd, bias) for g in range(3)]
    merge_tiled = [(o_sb, "hm")] + [(ol, "hm") for ol in ols]
    (merged,) = _tile_fwd("merge_fwd", _attn_merge_fn, merge_tiled, [], [((s, 512), BF16, "nat")], tt)
    x2 = _linear_fwd("attn_out_fwd", merged, w_out, tt, residual=x1)
    ffw[(0, 1)] = get("f01", x2)
    x3 = ffn("ffn01_fwd", x2, 0, 1)
    ffw[(1, 0)] = get("f10", x3)
    x4 = ffn("ffn10_fwd", x3, 1, 0)
    rw = get("rw", x4)
    rw_mix = _col_blocks_to_nat(rw["rw_mix"])
    rw_w1, rw_a1, rw_g1 = (rw[k].reshape(d, -1) for k in ("rw_w1", "rw_a1", "rw_g1"))
    rw_w2, rw_a2, rw_g2 = (_col_blocks_to_nat(rw[k]) for k in ("rw_w2", "rw_a2", "rw_g2"))
    rw_w0, rw_a0 = row(rw["rw_w0"]), row(rw["rw_a0"])
    head_par = lambda a: a.reshape(N_HEADS, 1, HEAD)
    scan_pars = [head_par(rw["rw_kk"]), head_par(rw["rw_ka"]), head_par(rep["rw_rk"]),
                 head_par(rw["rw_lnx_g"]), head_par(rw["rw_lnx_b"])]
    w_rkv = [rw[k].reshape(d, d) for k in ("rw_wr", "rw_wk", "rw_wv")]
    w_o = rw["rw_wo"].reshape(d, d)
    x4p = jnp.pad(x4, ((1, 0), (0, 0)))[:-1]
    mix_tiled = [(x4, "nat"), (x4p, "nat")]
    mix_pars = [mix_norm[1:2], rw_mix, rw_w0, rw_w1, rw_w2, rw_a0, rw_a1, rw_a2, rw_g1, rw_g2]
    hm_sh = (N_HEADS, s, HEAD)
    xr, xk, xv, lw, ag, gate = _tile_fwd(
        "rw_mix_fwd", _rw_mix_fn, mix_tiled, mix_pars,
        [((s, d), BF16, "nat")] * 3 + [(hm_sh, F32, "hm")] * 2 + [((s, d), F32, "nat")], tt)
    r_h, k_h, v_h = [_linear_fwd("rw_%s_fwd" % nm, xi, wi, tt, out_layout="hm")
                     for nm, xi, wi in zip("rkv", (xr, xk, xv), w_rkv)]
    scan_seqs = [r_h, lw, k_h, v_h, ag]
    z, states = _rwkv_fwd(scan_seqs, scan_pars)
    (zg,) = _tile_fwd("rw_gate_fwd", _rw_gate_fn, [(z, "hm"), (gate, "nat")], [], [((s, d), BF16, "nat")], tt)
    x5 = _linear_fwd("rw_out_fwd", zg, w_o, tt, residual=x4)
    ffw[(1, 1)] = get("f11", x5)
    y = ffn("ffn11_fwd", x5, 1, 1)
    dy, loss = _loss_head(y, target, tf)

    G = {}
    dgn = {}

    def fb(nm, group, xin, dout, l, h, zero=None, extra=None):
        g = ffw[(l, h)]
        gn = ffn_norm[2 * l + h][None]
        dxin, dgn[(l, h)], dwg, dwu, dwd = _ffn_bwd(nm, xin, dout, gn if zero is None else tied(gn, zero),
                                                   g["gate"], g["up"], g["down"], *acts[(l, h)], tf)
        shard = {"gate": dwg, "up": dwu, "down": dwd}
        if extra is not None:
            shard.update(extra())
        return dxin, put(group, {}, shard)

    dx5, zero = fb("ffn11_bwd", "f11", x5, dy, 1, 1)
    dzg = _linear_dx("rw_out_dx", dx5, w_o, tt)
    G["rw_wo"] = _linear_dw("rw_out_dw", zg, dx5, tf, 512)
    (dz, dgate), _ = _tile_bwd("rw_gate_bwd", _rw_gate_fn, [(z, "hm"), (gate, "nat")], [], [(dzg, "nat")], tt, [True, True])
    (dr_h, dlw, dk_h, dv_h, dag), dscan = _rwkv_bwd(scan_seqs, [tied(scan_pars[0], zero)] + scan_pars[1:], states, dz)
    drkv = (dr_h, dk_h, dv_h)
    for k, gpar in zip(("rw_kk", "rw_ka", "rw_rk", "rw_lnx_g", "rw_lnx_b"), dscan):
        G[k] = gpar
    dxs = []
    for j, (nm, xi, wi) in enumerate(zip("rkv", (xr, xk, xv), w_rkv)):
        dxs.append(_linear_dx("rw_%s_dx" % nm, drkv[j], wi, tt, dy_layout="hm"))
        G["rw_w" + nm] = _linear_dw("rw_%s_dw" % nm, xi, drkv[j], tf, 512, dy_layout="hm")
    (dx4a, dx4p), dmix = _tile_bwd(
        "rw_mix_bwd", _rw_mix_fn, mix_tiled, mix_pars,
        [(dxs[0], "nat"), (dxs[1], "nat"), (dxs[2], "nat"), (dlw, "hm"), (dag, "hm"), (dgate, "nat")],
        tt, [True, True], adds=[dx5, None])
    d_mixn1 = dmix[0]
    for k, gpar in zip(("rw_mix", "rw_w0", "rw_w1", "rw_w2", "rw_a0", "rw_a1", "rw_a2", "rw_g1", "rw_g2"), dmix[1:]):
        G[k] = gpar
    dx4 = dx4a + jnp.pad(dx4p[1:], ((0, 1), (0, 0)))
    for k in ("rw_mix", "rw_w2", "rw_a2", "rw_g2"):
        G[k] = _nat_to_col_blocks(G[k])
    for k in ("rw_w1", "rw_a1", "rw_g1", "rw_wr", "rw_wk", "rw_wv", "rw_wo"):
        G[k] = G[k].reshape(N_DEV, d // N_DEV, -1)
    for k in ("rw_w0", "rw_a0", "rw_kk", "rw_ka", "rw_lnx_g", "rw_lnx_b"):
        G[k] = G[k].reshape(N_DEV, 1, d // N_DEV)
    zero = put("rw", {"rw_rk": G["rw_rk"].reshape(N_HEADS, HEAD)}, {k: G[k] for k in RW_SHARDED})
    dx3, zero = fb("ffn10_bwd", "f10", x3, dx4, 1, 0, zero)
    dx2, zero = fb("ffn01_bwd", "f01", x2, dx3, 0, 1, zero)
    dmerged = _linear_dx("attn_out_dx", dx2, tied(w_out, zero), tt)
    (do_sb, *dols), _ = _tile_bwd("merge_bwd", _attn_merge_fn, merge_tiled, [], [(dmerged, "nat")], tt, [True] * 4)
    dq_sb, dk_sb, dv_sb = _sb_bwd(sq, sk, sv, do_sb)
    dil_grads, dbias = None, []
    for g in range(3):
        *dil_grads, db = _dil_group_bwd(g, qn, kn, vd, bias, ols[g], dols[g], dil_grads)
        dbias.append(db)
    dqn, dkn, dvd = dil_grads
    dbias = jnp.concatenate(dbias, axis=1)
    (dproj,), (dqn_w, dkn_w) = _tile_bwd(
        "attn_prep_bwd", _attn_prep_fn, [(proj, "hm")], prep_pars,
        [(dq_sb, "hm"), (dk_sb, "hm"), (dv_sb, "hm"), (dqn, "hm"), (dkn, "hm"), (dvd, "hm")], tt // 2, [True])
    dh0 = _linear_dx("attn_in_dx", dproj, w_in, tt, dy_layout="hm")
    (dx1,), (d_mixn0,) = _tile_bwd("mixnorm0_bwd", _norm_fn, [(x1, "nat")], [mix_norm[0:1]], [(dh0, "nat")], tt,
                                   [True], adds=[dx2])
    order = [(0, 0), (0, 1), (1, 0), (1, 1)]
    norm_grads = lambda: {"ffn_norm": _nat_to_col_blocks(jnp.concatenate([dgn[o] for o in order], axis=0))}
    dx0, zero = fb("ffn00_bwd", "f00", x, dx1, 0, 0, extra=norm_grads)
    G["attn_w_out"] = _linear_dw("attn_out_dw", tied(merged, zero), dx2, tf, 512)
    G["attn_w_in"] = _linear_dw("attn_in_dw", tied(h0, zero), dproj, tf, 512, dy_layout="hm")
    rep_grads = {"mix_norm": jnp.concatenate([d_mixn0, d_mixn1], axis=0), "rel_bias": _bias_tiles_bwd(dbias, s),
                 "attn_q_norm": dqn_w, "attn_k_norm": dkn_w}
    zero = put("att", rep_grads, {k: _nat_to_col_blocks(G[k]) for k in ("attn_w_in", "attn_w_out")})
    return loss, dx0, zero


WEIGHTS = ['ffn_norm', 'ffn_w_gate', 'ffn_w_up', 'ffn_w_down', 'mix_norm', 'rel_bias', 'attn_w_in', 'attn_q_norm',
           'attn_k_norm', 'attn_w_out', 'rw_mix', 'rw_w0', 'rw_w1', 'rw_w2', 'rw_a0', 'rw_a1', 'rw_a2', 'rw_g1', 'rw_g2',
           'rw_kk', 'rw_ka', 'rw_rk', 'rw_wr', 'rw_wk', 'rw_wv', 'rw_wo', 'rw_lnx_g', 'rw_lnx_b']
REPLICATED = ('mix_norm', 'rel_bias', 'attn_q_norm', 'attn_k_norm', 'rw_rk')
BF16_WEIGHTS = ('ffn_w_gate', 'ffn_w_up', 'ffn_w_down', 'attn_w_in', 'attn_w_out', 'rw_wr', 'rw_wk', 'rw_wv', 'rw_wo')


def kernel(x, ffn_norm, ffn_w_gate, ffn_w_up, ffn_w_down, mix_norm, rel_bias, attn_w_in, attn_q_norm, attn_k_norm, attn_w_out, rw_mix, rw_w0, rw_w1, rw_w2, rw_a0, rw_a1, rw_a2, rw_g1, rw_g2, rw_kk, rw_ka, rw_rk, rw_wr, rw_wk, rw_wv, rw_wo, rw_lnx_g, rw_lnx_b, loss_target, m_ffn_norm, m_ffn_w_gate, m_ffn_w_up, m_ffn_w_down, m_mix_norm, m_rel_bias, m_attn_w_in, m_attn_q_norm, m_attn_k_norm, m_attn_w_out, m_rw_mix, m_rw_w0, m_rw_w1, m_rw_w2, m_rw_a0, m_rw_a1, m_rw_a2, m_rw_g1, m_rw_g2, m_rw_kk, m_rw_ka, m_rw_rk, m_rw_wr, m_rw_wk, m_rw_wv, m_rw_wo, m_rw_lnx_g, m_rw_lnx_b, v_ffn_norm, v_ffn_w_gate, v_ffn_w_up, v_ffn_w_down, v_mix_norm, v_rel_bias, v_attn_w_in, v_attn_q_norm, v_attn_k_norm, v_attn_w_out, v_rw_mix, v_rw_w0, v_rw_w1, v_rw_w2, v_rw_a0, v_rw_a1, v_rw_a2, v_rw_g1, v_rw_g2, v_rw_kk, v_rw_ka, v_rw_rk, v_rw_wr, v_rw_wk, v_rw_wv, v_rw_wo, v_rw_lnx_g, v_rw_lnx_b):
    args = locals()
    w = {k: args[k] for k in WEIGHTS}
    cast = lambda k, a: a.astype(BF16) if k in BF16_WEIGHTS else a

    sources = {}
    for l, h in ((0, 0), (0, 1), (1, 0), (1, 1)):
        sources["f%d%d" % (l, h)] = {"gate": cast("ffn_w_gate", ffn_w_gate[l, h]), "up": cast("ffn_w_up", ffn_w_up[l, h]),
                                     "down": cast("ffn_w_down", ffn_w_down[l, h])}
    sources["f00"]["ffn_norm"] = ffn_norm
    drop_lead = lambda a: a[0] if a.ndim == 3 else a
    sources["att"] = {k: cast(k, w[k][0]) for k in ("attn_w_in", "attn_w_out")}
    sources["rw"] = {k: cast(k, drop_lead(w[k])) for k in RW_SHARDED}
    ag, token = {}, None
    for group in AG_GROUPS:
        names, arrays = list(sources[group]), list(sources[group].values())
        if token is not None:
            arrays[0] = arrays[0] + token[0, 0].astype(arrays[0].dtype)
        ag[group] = (names, _exchange_start("ag_start_" + group, arrays, []))
        token = ag[group][1]["token"]
    last_ag_token = token

    def get(group, after):
        names, started = ag[group]
        gathered, _ = _exchange_wait("ag_wait_" + group, started, last_ag_token if after is None else after)
        return dict(zip(names, gathered))

    rs = {}

    def put(group, rep_grads, shard_grads):
        if group in BF16_GRAD_GROUPS:
            shard_grads = {k: v.astype(BF16) for k, v in shard_grads.items()}
        started = _exchange_start("rs_start_" + group, list(rep_grads.values()), list(shard_grads.values()))
        rs[group] = (list(rep_grads), list(shard_grads), started)
        return started["token"]

    loss, dx, last_zero = _step(x[0], loss_target[0], {k: w[k] for k in REPLICATED}, get, put)
    loss = lax.psum(loss, MESH_AXES)

    results = {}
    ffn_prev = {}

    def update(k, parts, row0=0, prev=None):
        c = w[k].shape[-1]
        as2d = lambda a: a.reshape(-1, c)
        return _adamw("adamw_%s_%d" % (k, row0), as2d(w[k]), as2d(args["m_" + k]), as2d(args["v_" + k]),
                      parts.reshape(N_DEV, -1, c), row0, prev)

    after = last_zero
    for group in RS_GROUPS:
        rep_names, shard_names, started = rs[group]
        rep_parts, shard_parts = _exchange_wait("rs_wait_" + group, started, after)
        for k, parts in list(zip(rep_names, rep_parts)) + list(zip(shard_names, shard_parts)):
            if k in ("gate", "up", "down"):
                full = "ffn_w_" + k
                piece = 2 * int(group[1]) + int(group[2])
                ffn_prev[full] = update(full, parts, piece * parts.shape[1], ffn_prev.get(full))
                results[full] = ffn_prev[full]
            else:
                results[k] = update(k, parts)
            after = results[k if k in results else "ffn_w_" + k][0]

    outs = [[results[k][j].reshape(w[k].shape) for k in WEIGHTS] for j in range(4)]
    return (loss, dx[None], *outs[0], *outs[1], *outs[2], *outs[3])
```

```python
import functools
import math

import numpy as np
import jax
import jax.numpy as jnp
from jax import lax
from jax.experimental import pallas as pl
from jax.experimental.pallas import tpu as pltpu

F32, BF16 = jnp.float32, jnp.bfloat16
HI = lax.Precision.HIGH

N_DEV = 8
D_MODEL = 1024
HEAD = 64
N_HEADS = 16
SB_HEADS = 4
DIL_GROUP = 4
DIL_PATTERNS = ((128, 1), (512, 4), (2048, 16))
QBLK = 128
N_BUCKETS = 32
MAX_DISTANCE = 2048
NORM_EPS = 1e-6
GN_EPS = 64e-5
NEG_INF = -1e30
RW_CHUNK = 64
RW_HB = 16
ADAM_LR, ADAM_B1, ADAM_B2, ADAM_EPS, ADAM_WD, ADAM_STEP = 0.001, 0.9, 0.999, 1e-08, 0.01, 10
MESH_AXES = ("x", "y", "c")
VMEM_LIMIT_BYTES = 56 * 1024 * 1024

NN2 = (((1,), (0,)), ((), ()))
NT2 = (((1,), (1,)), ((), ()))
TN2 = (((0,), (0,)), ((), ()))
NN3 = (((2,), (1,)), ((0,), (0,)))
NT3 = (((2,), (2,)), ((0,), (0,)))
TN3 = (((1,), (1,)), ((0,), (0,)))


def _dot(a, b, dims=NN2, prec=None):
    return lax.dot_general(a, b, dims, precision=prec, preferred_element_type=F32)


def _params(sem=None):
    return pltpu.CompilerParams(dimension_semantics=sem, vmem_limit_bytes=VMEM_LIMIT_BYTES)


@jax.custom_vjp
def _mm(x, w):
    return _dot(x.astype(BF16), w.astype(BF16))


def _mm_fwd(x, w):
    return _mm(x, w), (x, w)


def _mm_bwd(res, dy):
    x, w = res
    dyb = dy.astype(BF16)
    return (_dot(dyb, w.astype(BF16), NT2).astype(x.dtype), _dot(x.astype(BF16), dyb, TN2).astype(w.dtype))


_mm.defvjp(_mm_fwd, _mm_bwd)


def _rms(x, g):
    return x * lax.rsqrt(jnp.mean(x * x, axis=-1, keepdims=True) + NORM_EPS) * g


def _log_sigmoid(z):
    return jnp.minimum(z, 0.0) - jnp.log(1.0 + jnp.exp(-jnp.abs(z)))


def _heads_to_nat(v3):
    return jnp.concatenate([v3[h] for h in range(v3.shape[0])], axis=-1)


def _nat_to_heads(v2):
    return jnp.stack([v2[:, h * HEAD:(h + 1) * HEAD] for h in range(v2.shape[1] // HEAD)], axis=0)


def _exchange(name, gathers, scatters):
    n_g = len(gathers)
    arrays = list(gathers) + list(scatters)
    n = len(arrays)
    out_shape = [jax.ShapeDtypeStruct((N_DEV,) + a.shape, a.dtype) for a in gathers]
    out_shape += [jax.ShapeDtypeStruct(a.shape, a.dtype) for a in scatters]

    def body(*refs):
        ins, outs = refs[:n], refs[n:2 * n]
        send_sems, recv_sems, local_sems = refs[2 * n:]
        x, y, c = lax.axis_index("x"), lax.axis_index("y"), lax.axis_index("c")
        me = 4 * x + 2 * y + c

        def src(i, idx):
            return ins[i] if i < n_g else ins[i].at[idx]

        local = [pltpu.make_async_copy(src(i, me), outs[i].at[me], local_sems.at[i]) for i in range(n)]
        for cp in local:
            cp.start()
        remote = []
        for m in range(1, N_DEV):
            px, py, pc = x ^ ((m >> 2) & 1), y ^ ((m >> 1) & 1), c ^ (m & 1)
            peer = 4 * px + 2 * py + pc
            for i in range(n):
                cp = pltpu.make_async_remote_copy(
                    src_ref=src(i, peer), dst_ref=outs[i].at[me],
                    send_sem=send_sems.at[i, m - 1], recv_sem=recv_sems.at[i, m - 1],
                    device_id=(px, py, pc), device_id_type=pl.DeviceIdType.MESH)
                cp.start()
                arrival = pltpu.make_async_remote_copy(
                    src_ref=src(i, peer), dst_ref=outs[i].at[peer],
                    send_sem=send_sems.at[i, m - 1], recv_sem=recv_sems.at[i, m - 1],
                    device_id=(px, py, pc), device_id_type=pl.DeviceIdType.MESH)
                remote.append((cp, arrival))
        for cp, arrival in remote:
            cp.wait_send()
            arrival.wait_recv()
        for cp in local:
            cp.wait()

    hbm = pl.BlockSpec(memory_space=pltpu.HBM)
    outs = pl.pallas_call(
        body, name=name, out_shape=out_shape,
        in_specs=[hbm] * n, out_specs=[hbm] * n,
        scratch_shapes=[pltpu.SemaphoreType.DMA((n, N_DEV - 1)), pltpu.SemaphoreType.DMA((n, N_DEV - 1)),
                        pltpu.SemaphoreType.DMA((n,))],
    )(*arrays)
    return list(outs[:n_g]), list(outs[n_g:])


def _mesh_peers():
    x, y, c = lax.axis_index("x"), lax.axis_index("y"), lax.axis_index("c")
    peers = []
    for m in range(1, N_DEV):
        px, py, pc = x ^ ((m >> 2) & 1), y ^ ((m >> 1) & 1), c ^ (m & 1)
        peers.append((m, (px, py, pc), 4 * px + 2 * py + pc))
    return 4 * x + 2 * y + c, peers


_HBM_SPEC = pl.BlockSpec(memory_space=pltpu.HBM)
_SEM_SPEC = pl.BlockSpec(memory_space=pltpu.SEMAPHORE)
_DATAFLOW = pltpu.SideEffectType.DATAFLOW_SIDE_EFFECTING


def _exchange_start(name, gathers, scatters):
    n_g = len(gathers)
    arrays = list(gathers) + list(scatters)
    n = len(arrays)
    lands = ([lax.empty((N_DEV,) + a.shape, a.dtype) for a in gathers] + [lax.empty(a.shape, a.dtype) for a in scatters])

    def body(*refs):
        ins, land = refs[:n], refs[n:2 * n]
        send_sems, recv_sems, local_sems, token = refs[2 * n], refs[2 * n + 1], refs[2 * n + 2], refs[-1]
        me, peers = _mesh_peers()
        for m, dev, peer in peers:
            for i in range(n):
                k = i * (N_DEV - 1) + m - 1
                pltpu.make_async_remote_copy(
                    src_ref=ins[i] if i < n_g else ins[i].at[peer], dst_ref=land[i].at[me],
                    send_sem=send_sems.at[k], recv_sem=recv_sems.at[k],
                    device_id=dev, device_id_type=pl.DeviceIdType.MESH).start()
        for i in range(n):
            pltpu.make_async_copy(ins[i] if i < n_g else ins[i].at[me], land[i].at[me], local_sems.at[i]).start()
        token[...] = jnp.zeros_like(token)

    sem = pltpu.SemaphoreType.DMA((n * (N_DEV - 1),))
    outs = pl.pallas_call(
        body, name=name,
        out_shape=([sem, sem, pltpu.SemaphoreType.DMA((n,))] + [pltpu.HBM(a.shape, a.dtype) for a in arrays]
                   + [pltpu.HBM(l.shape, l.dtype) for l in lands] + [jax.ShapeDtypeStruct((8, 128), F32)]),
        in_specs=[_HBM_SPEC] * (2 * n),
        out_specs=[_SEM_SPEC] * 3 + [_HBM_SPEC] * (2 * n) + [pl.BlockSpec(memory_space=pltpu.VMEM)],
        input_output_aliases={i: i + 3 for i in range(2 * n)},
        compiler_params=pltpu.CompilerParams(has_side_effects=_DATAFLOW),
    )(*[pltpu.with_memory_space_constraint(a, pltpu.HBM) for a in arrays],
      *[pltpu.with_memory_space_constraint(l, pltpu.HBM) for l in lands])
    return dict(n_g=n_g, n=n, send=outs[0], recv=outs[1], local=outs[2], srcs=list(outs[3:3 + n]),
                lands=list(outs[3 + n:3 + 2 * n]), token=outs[-1])


def _exchange_wait(name, started, after):
    n, n_g = started["n"], started["n_g"]

    def body(*refs):
        srcs, lands = refs[:n], refs[n:2 * n]
        send_sems, recv_sems, local_sems = refs[2 * n], refs[2 * n + 1], refs[2 * n + 2]
        me, peers = _mesh_peers()
        local = [pltpu.make_async_copy(srcs[i] if i < n_g else srcs[i].at[me], lands[i].at[me], local_sems.at[i])
                 for i in range(n)]
        for m, dev, peer in peers:
            for i in range(n):
                k = i * (N_DEV - 1) + m - 1
                cp = pltpu.make_async_remote_copy(
                    src_ref=srcs[i] if i < n_g else srcs[i].at[peer], dst_ref=lands[i].at[peer],
                    send_sem=send_sems.at[k], recv_sem=recv_sems.at[k],
                    device_id=dev, device_id_type=pl.DeviceIdType.MESH)
                cp.wait_send()
                cp.wait_recv()
        for cp in local:
            cp.wait()

    outs = pl.pallas_call(
        body, name=name,
        out_shape=([pltpu.HBM(a.shape, a.dtype) for a in started["srcs"]]
                   + [pltpu.HBM(l.shape, l.dtype) for l in started["lands"]]),
        in_specs=[_HBM_SPEC] * (2 * n) + [_SEM_SPEC] * 3 + [pl.BlockSpec(memory_space=pl.ANY)],
        out_specs=[_HBM_SPEC] * (2 * n), input_output_aliases={i: i for i in range(2 * n)},
        compiler_params=pltpu.CompilerParams(has_side_effects=_DATAFLOW),
    )(*started["srcs"], *started["lands"], started["send"], started["recv"], started["local"], after)
    return list(outs[n:n + n_g]), list(outs[n + n_g:])


def _tile_spec(shape, layout, t):
    if layout == "nat":
        return pl.BlockSpec((t, shape[1]), lambda i: (i, 0))
    return pl.BlockSpec((shape[0], t, shape[2]), lambda i: (0, i, 0))


def _full_spec(shape):
    nd = len(shape)
    return pl.BlockSpec(tuple(shape), lambda i: (0,) * nd)


def _seq_len(a, layout):
    return a.shape[0] if layout == "nat" else a.shape[1]


def _tile_fwd(name, f, tiled, params, outs, t):
    nt, npar = len(tiled), len(params)
    s = _seq_len(*tiled[0])

    def body(*refs):
        vals = [r[...] for r in refs[:nt + npar]]
        res = f(*vals)
        for r, o in zip(refs[nt + npar:], res):
            r[...] = o.astype(r.dtype)

    return pl.pallas_call(
        body, name=name, grid=(s // t,),
        in_specs=[_tile_spec(a.shape, l, t) for a, l in tiled] + [_full_spec(p.shape) for p in params],
        out_specs=[_tile_spec(sh, l, t) for sh, _, l in outs],
        out_shape=[jax.ShapeDtypeStruct(sh, dt) for sh, dt, _ in outs],
        compiler_params=_params(("arbitrary",)),
    )(*[a for a, _ in tiled], *params)


def _tile_bwd(name, f, tiled, params, cts, t, need, adds=None):
    nt, npar, nc = len(tiled), len(params), len(cts)
    s = _seq_len(*tiled[0])
    need_idx = [k for k in range(nt) if need[k]]
    adds = adds or [None] * len(need_idx)
    add_arrays = [(a, tiled[k][1]) for a, k in zip(adds, need_idx) if a is not None]
    n_add = len(add_arrays)

    def body(*refs):
        i = pl.program_id(0)
        vals = [r[...] for r in refs[:nt + npar]]
        ct_refs = refs[nt + npar:nt + npar + nc]
        add_refs = refs[nt + npar + nc:nt + npar + nc + n_add]
        out_refs = refs[nt + npar + nc + n_add:]
        res, vjp = jax.vjp(f, *vals)
        grads = vjp(tuple(r[...].astype(o.dtype) for r, o in zip(ct_refs, res)))
        a = 0
        for j, k in enumerate(need_idx):
            g = grads[k]
            if adds[j] is not None:
                g = g + add_refs[a][...]
                a += 1
            out_refs[j][...] = g.astype(out_refs[j].dtype)
        for j in range(npar):
            r = out_refs[len(need_idx) + j]

            @pl.when(i == 0)
            def _():
                r[...] = jnp.zeros_like(r)

            r[...] += grads[nt + j]

    outs = pl.pallas_call(
        body, name=name, grid=(s // t,),
        in_specs=([_tile_spec(a.shape, l, t) for a, l in tiled] + [_full_spec(p.shape) for p in params]
                  + [_tile_spec(a.shape, l, t) for a, l in cts] + [_tile_spec(a.shape, l, t) for a, l in add_arrays]),
        out_specs=([_tile_spec(tiled[k][0].shape, tiled[k][1], t) for k in need_idx]
                   + [_full_spec(p.shape) for p in params]),
        out_shape=([jax.ShapeDtypeStruct(tiled[k][0].shape, F32) for k in need_idx]
                   + [jax.ShapeDtypeStruct(p.shape, F32) for p in params]),
        compiler_params=_params(("arbitrary",)),
    )(*[a for a, _ in tiled], *params, *[a for a, _ in cts], *[a for a, _ in add_arrays])
    return list(outs[:len(need_idx)]), list(outs[len(need_idx):])


def _linear_fwd(name, x, w, t, out_layout="nat", residual=None):
    s, k = x.shape
    n = w.shape[1]
    has_res = residual is not None

    def body(*refs):
        x_ref, w_ref = refs[0], refs[1]
        o_ref = refs[-1]
        y = _dot(x_ref[...].astype(BF16), w_ref[...])
        if has_res:
            y = y + refs[2][...]
        if out_layout == "hm":
            for h in range(n // HEAD):
                o_ref[h] = y[:, h * HEAD:(h + 1) * HEAD]
        else:
            o_ref[...] = y

    out_sh = (s, n) if out_layout == "nat" else (n // HEAD, s, HEAD)
    ins = [x, w] + ([residual] if has_res else [])
    in_specs = [_tile_spec(x.shape, "nat", t), _full_spec(w.shape)] + ([_tile_spec((s, n), "nat", t)] if has_res else [])
    return pl.pallas_call(
        body, name=name, grid=(s // t,), in_specs=in_specs,
        out_specs=_tile_spec(out_sh, out_layout, t), out_shape=jax.ShapeDtypeStruct(out_sh, F32),
        compiler_params=_params(("arbitrary",)),
    )(*ins)


def _linear_dx(name, dy, w, t, dy_layout="nat"):
    k, n = w.shape
    s = _seq_len(dy, dy_layout)

    def body(dy_ref, w_ref, o_ref):
        dy = _heads_to_nat(dy_ref[...].astype(BF16)) if dy_layout == "hm" else dy_ref[...].astype(BF16)
        o_ref[...] = _dot(dy, w_ref[...], NT2)

    return pl.pallas_call(
        body, name=name, grid=(s // t,),
        in_specs=[_tile_spec(dy.shape, dy_layout, t), _full_spec(w.shape)],
        out_specs=_tile_spec((s, k), "nat", t), out_shape=jax.ShapeDtypeStruct((s, k), F32),
        compiler_params=_params(("arbitrary",)),
    )(dy, w)


def _linear_dw(name, x, dy, t, nb, dy_layout="nat"):
    s, k = x.shape
    n = dy.shape[1] if dy_layout == "nat" else dy.shape[0] * HEAD

    def body(x_ref, dy_ref, o_ref):
        i = pl.program_id(1)

        @pl.when(i == 0)
        def _():
            o_ref[...] = jnp.zeros_like(o_ref)

        dy = _heads_to_nat(dy_ref[...].astype(BF16)) if dy_layout == "hm" else dy_ref[...].astype(BF16)
        o_ref[...] += _dot(x_ref[...].astype(BF16), dy, TN2)

    if dy_layout == "hm":
        dy_spec = pl.BlockSpec((nb // HEAD, t, HEAD), lambda j, i: (j, i, 0))
    else:
        dy_spec = pl.BlockSpec((t, nb), lambda j, i: (i, j))
    return pl.pallas_call(
        body, name=name, grid=(n // nb, s // t),
        in_specs=[pl.BlockSpec((t, k), lambda j, i: (i, 0)), dy_spec],
        out_specs=pl.BlockSpec((k, nb), lambda j, i: (0, j)), out_shape=jax.ShapeDtypeStruct((k, n), F32),
        compiler_params=_params(("arbitrary", "arbitrary")),
    )(x, dy)


def _ffn_fwd(name, x, gn, wg, wu, wd, t):
    s, d = x.shape
    f8 = wg.shape[-1]

    def body(x_ref, g_ref, wg_ref, wu_ref, wd_ref, o_ref, gk_ref, uk_ref, h_scr, acc):
        k = pl.program_id(1)

        @pl.when(k == 0)
        def _():
            h_scr[...] = _rms(x_ref[...], g_ref[...]).astype(BF16)
            acc[...] = jnp.zeros_like(acc)

        hb = h_scr[...]
        gk = _dot(hb, wg_ref[0])
        uk = _dot(hb, wu_ref[0])
        gk_ref[0] = gk
        uk_ref[0] = uk
        a = gk * jax.nn.sigmoid(gk) * uk
        acc[...] += _dot(a.astype(BF16), wd_ref[0])

        @pl.when(k == N_DEV - 1)
        def _():
            o_ref[...] = x_ref[...] + 0.5 * acc[...]

    wspec = lambda shp: pl.BlockSpec((1,) + shp, lambda i, k: (k, 0, 0))
    act = pl.BlockSpec((1, t, f8), lambda i, k: (k, i, 0))
    act_sh = jax.ShapeDtypeStruct((N_DEV, s, f8), F32)
    return pl.pallas_call(
        body, name=name, grid=(s // t, N_DEV),
        in_specs=[pl.BlockSpec((t, d), lambda i, k: (i, 0)), pl.BlockSpec((1, d), lambda i, k: (0, 0)),
                  wspec((d, f8)), wspec((d, f8)), wspec((f8, d))],
        out_specs=[pl.BlockSpec((t, d), lambda i, k: (i, 0)), act, act],
        out_shape=[jax.ShapeDtypeStruct((s, d), F32), act_sh, act_sh],
        scratch_shapes=[pltpu.VMEM((t, d), BF16), pltpu.VMEM((t, d), F32)],
        compiler_params=_params(("arbitrary", "arbitrary")),
    )(x, gn, wg, wu, wd)


def _ffn_bwd(name, x, dy, gn, wg, wu, wd, gact, uact, t):
    s, d = x.shape
    f8 = wg.shape[-1]
    last = N_DEV - 1

    def body(x_ref, dy_ref, g_ref, wg_ref, wu_ref, wd_ref, gk_ref, uk_ref,
             dx_ref, dg_ref, dwg_ref, dwu_ref, dwd_ref, dh_scr):
        k, i = pl.program_id(0), pl.program_id(1)
        x = x_ref[...]
        rs = lax.rsqrt(jnp.mean(x * x, axis=-1, keepdims=True) + NORM_EPS)
        xn = x * rs
        hb = (xn * g_ref[...]).astype(BF16)
        dob = (0.5 * dy_ref[...]).astype(BF16)
        wgk, wuk, wdk = wg_ref[0], wu_ref[0], wd_ref[0]
        gk, uk = gk_ref[0], uk_ref[0]
        sg = jax.nn.sigmoid(gk)
        sk = gk * sg
        da = _dot(dob, wdk, NT2)
        du = (da * sk).astype(BF16)
        dg = (da * uk * (sg * (1.0 + gk * (1.0 - sg)))).astype(BF16)

        dwd_c = _dot((sk * uk).astype(BF16), dob, TN2)
        dwg_c = _dot(hb, dg, TN2)
        dwu_c = _dot(hb, du, TN2)
        dh = _dot(dg, wgk, NT2) + _dot(du, wuk, NT2)
        rows = pl.ds(pl.multiple_of(i * t, t), t)

        @pl.when(i == 0)
        def _():
            dwg_ref[0], dwu_ref[0], dwd_ref[0] = dwg_c, dwu_c, dwd_c

        @pl.when(i > 0)
        def _():
            dwg_ref[0] += dwg_c
            dwu_ref[0] += dwu_c
            dwd_ref[0] += dwd_c

        @pl.when(k == 0)
        def _():
            dh_scr[rows, :] = dh

        @pl.when(k > 0)
        def _():
            dh_scr[rows, :] += dh

        @pl.when(jnp.logical_and(k == last, i == 0))
        def _():
            dg_ref[...] = jnp.zeros_like(dg_ref)

        @pl.when(k == last)
        def _():
            dht = dh_scr[rows, :]
            dg_ref[...] += jnp.sum(dht * xn, axis=0, keepdims=True)
            dxn = dht * g_ref[...]
            dx_ref[...] = dy_ref[...] + rs * (dxn - xn * jnp.mean(dxn * xn, axis=-1, keepdims=True))

    wspec = lambda shp: pl.BlockSpec((1,) + shp, lambda k, i: (k, 0, 0))
    tile = pl.BlockSpec((t, d), lambda k, i: (i, 0))
    act = pl.BlockSpec((1, t, f8), lambda k, i: (k, i, 0))
    return pl.pallas_call(
        body, name=name, grid=(N_DEV, s // t),
        in_specs=[tile, tile, pl.BlockSpec((1, d), lambda k, i: (0, 0)), wspec((d, f8)), wspec((d, f8)), wspec((f8, d)),
                  act, act],
        out_specs=[pl.BlockSpec((t, d), lambda k, i: (jnp.where(k == last, i, 0), 0)),
                   pl.BlockSpec((1, d), lambda k, i: (0, 0)),
                   pl.BlockSpec((1, d, f8), lambda k, i: (k, 0, 0)), pl.BlockSpec((1, d, f8), lambda k, i: (k, 0, 0)),
                   pl.BlockSpec((1, f8, d), lambda k, i: (k, 0, 0))],
        out_shape=[jax.ShapeDtypeStruct((s, d), F32), jax.ShapeDtypeStruct((1, d), F32),
                   jax.ShapeDtypeStruct((N_DEV, d, f8), F32), jax.ShapeDtypeStruct((N_DEV, d, f8), F32),
                   jax.ShapeDtypeStruct((N_DEV, f8, d), F32)],
        scratch_shapes=[pltpu.VMEM((s, d), F32)],
        compiler_params=_params(("arbitrary", "arbitrary")),
    )(x, dy, gn, wg, wu, wd, gact, uact)


def _loss_head(y, target, t):
    s, d = y.shape

    def body(y_ref, t_ref, dy_ref, l_ref):
        i = pl.program_id(0)
        err = y_ref[...] - t_ref[...]
        dy_ref[...] = err * (1.0 / d)

        @pl.when(i == 0)
        def _():
            l_ref[...] = jnp.zeros_like(l_ref)

        l_ref[...] += 0.5 * jnp.sum(jnp.mean(err * err, axis=-1, keepdims=True), axis=0, keepdims=True)

    tile = pl.BlockSpec((t, d), lambda i: (i, 0))
    dy, l = pl.pallas_call(
        body, name="loss_head", grid=(s // t,), in_specs=[tile, tile],
        out_specs=[tile, pl.BlockSpec((1, 1), lambda i: (0, 0))],
        out_shape=[jax.ShapeDtypeStruct((s, d), F32), jax.ShapeDtypeStruct((1, 1), F32)],
        compiler_params=_params(("arbitrary",)),
    )(y, target)
    return dy, l[0, 0]


SB_KEY_TILE = 1024
SB_HEADS_PER_STEP = 4


def _sb_scan_mats():
    row = lax.broadcasted_iota(jnp.int32, (QBLK, QBLK), 0)
    col = lax.broadcasted_iota(jnp.int32, (QBLK, QBLK), 1)
    return (row > col).astype(F32).astype(BF16), (row < col).astype(F32).astype(BF16)


def _sb_tile_scan(x, mat, reverse):
    nsub = x.shape[1] // QBLK
    outs, carry = [None] * nsub, jnp.zeros((x.shape[0], 1), F32)
    for i in (reversed(range(nsub)) if reverse else range(nsub)):
        xs = x[:, i * QBLK:(i + 1) * QBLK]
        hi = xs.astype(BF16)
        lo = (xs - hi.astype(F32)).astype(BF16)
        outs[i] = _dot(hi, mat) + _dot(lo, mat) + carry
        carry = carry + jnp.sum(xs, axis=1, keepdims=True)
    return jnp.concatenate(outs, axis=1), carry


def _sb_before_query(n, t, kt):
    row = lax.broadcasted_iota(jnp.int32, (QBLK, kt), 0)
    col = lax.broadcasted_iota(jnp.int32, (QBLK, kt), 1)
    return t * kt + col < n * QBLK + row


def _sb_fwd(q, k, v):
    _, s, _ = q.shape
    scale = HEAD ** -0.5
    kt = min(SB_KEY_TILE, s)

    def body(q_ref, k_ref, v_ref, o_ref):
        n = pl.program_id(1)
        suffix, _ = _sb_scan_mats()
        n_tiles = lax.div(n, jnp.int32(kt // QBLK)) + 1
        heads = range(SB_HEADS_PER_STEP)
        qb = [(q_ref[h] * scale).astype(q_ref.dtype) for h in heads]

        def tile(t, carry, diagonal):
            rows = pl.ds(pl.multiple_of(t * kt, kt), kt)
            out = []
            for h in heads:
                c, acc = carry[h]
                z = _dot(qb[h], k_ref[h, rows, :], NT2)
                lk = _log_sigmoid(-z)
                log_beta = z + lk
                if diagonal:
                    ok = _sb_before_query(n, t, kt)
                    lk = jnp.where(ok, lk, 0.0)
                later, total = _sb_tile_scan(lk, suffix, True)
                w = jnp.exp(log_beta + later + c)
                if diagonal:
                    w = jnp.where(ok, w, 0.0)
                out.append((c + total, acc + _dot(w.astype(BF16), v_ref[h, rows, :])))
            return tuple(out)

        zero = (jnp.zeros((QBLK, 1), F32), jnp.zeros((QBLK, HEAD), F32))
        carry = tile(n_tiles - 1, (zero,) * len(heads), True)
        carry = lax.fori_loop(1, n_tiles, lambda jj, cr: tile(n_tiles - 1 - jj, cr, False), carry)
        for h in heads:
            o_ref[h] = carry[h][1]

    hp = SB_HEADS_PER_STEP
    return pl.pallas_call(
        body, name="sb_fwd", grid=(SB_HEADS // hp, s // QBLK),
        in_specs=[pl.BlockSpec((hp, QBLK, HEAD), lambda h, n: (h, n, 0)),
                  pl.BlockSpec((hp, s, HEAD), lambda h, n: (h, 0, 0)),
                  pl.BlockSpec((hp, s, HEAD), lambda h, n: (h, 0, 0))],
        out_specs=pl.BlockSpec((hp, QBLK, HEAD), lambda h, n: (h, n, 0)),
        out_shape=jax.ShapeDtypeStruct((SB_HEADS, s, HEAD), F32),
        compiler_params=_params(("arbitrary", "arbitrary")),
    )(q, k, v)


def _sb_bwd(q, k, v, do):
    _, s, _ = q.shape
    scale = HEAD ** -0.5
    kt = min(SB_KEY_TILE, s)

    def body(q_ref, k_ref, v_ref, do_ref, dq_ref, dk_ref, dv_ref, e_scr, beta_scr):
        n = pl.program_id(1)

        @pl.when(n == 0)
        def _():
            dk_ref[...] = jnp.zeros_like(dk_ref)
            dv_ref[...] = jnp.zeros_like(dv_ref)

        suffix, prefix = _sb_scan_mats()
        n_tiles = lax.div(n, jnp.int32(kt // QBLK)) + 1
        heads = range(SB_HEADS_PER_STEP)
        qb = [(q_ref[h] * scale).astype(q_ref.dtype) for h in heads]
        dob = [do_ref[h].astype(BF16) for h in heads]

        def weights(t, cs, diagonal):
            rows = pl.ds(pl.multiple_of(t * kt, kt), kt)
            out, stores = [], []
            for h in heads:
                vb = v_ref[h, rows, :]
                z = _dot(qb[h], k_ref[h, rows, :], NT2)
                lk = _log_sigmoid(-z)
                log_beta = z + lk
                if diagonal:
                    ok = _sb_before_query(n, t, kt)
                    lk = jnp.where(ok, lk, 0.0)
                later, total = _sb_tile_scan(lk, suffix, True)
                w = jnp.exp(log_beta + later + cs[h])
                if diagonal:
                    w = jnp.where(ok, w, 0.0)
                stores.append((w * _dot(dob[h], vb, NT2), jnp.exp(log_beta), _dot(w.astype(BF16), dob[h], TN2)))
                out.append(cs[h] + total)
            for h in heads:
                e_scr[h, t], beta_scr[h, t] = stores[h][0], stores[h][1]
                dv_ref[h, rows, :] += stores[h][2]
            return tuple(out)

        col0 = jnp.zeros((QBLK, 1), F32)
        cs = weights(n_tiles - 1, (col0,) * len(heads), True)
        lax.fori_loop(1, n_tiles, lambda jj, c: weights(n_tiles - 1 - jj, c, False), cs)

        def grads(t, carry, diagonal):
            rows = pl.ds(pl.multiple_of(t * kt, kt), kt)
            out, dks = [], []
            for h in heads:
                pc, dq = carry[h]
                kb = k_ref[h, rows, :]
                e, beta = e_scr[h, t], beta_scr[h, t]
                before, total = _sb_tile_scan(e, prefix, False)
                dz = e * (1.0 - beta) - beta * (before + pc)
                if diagonal:
                    dz = jnp.where(_sb_before_query(n, t, kt), dz, 0.0)
                dz = dz.astype(BF16)
                dks.append(_dot(dz, qb[h], TN2))
                out.append((pc + total, dq + _dot(dz, kb)))
            for h in heads:
                dk_ref[h, rows, :] += dks[h]
            return tuple(out)

        zero = (col0, jnp.zeros((QBLK, HEAD), F32))
        carry = lax.fori_loop(0, n_tiles - 1, lambda t, cr: grads(t, cr, False), (zero,) * len(heads))
        carry = grads(n_tiles - 1, carry, True)
        for h in heads:
            dq_ref[h] = carry[h][1] * scale

    hp = SB_HEADS_PER_STEP
    qspec = pl.BlockSpec((hp, QBLK, HEAD), lambda h, n: (h, n, 0))
    full = pl.BlockSpec((hp, s, HEAD), lambda h, n: (h, 0, 0))
    sh = jax.ShapeDtypeStruct((SB_HEADS, s, HEAD), F32)
    tiles_sh = (hp, s // kt, QBLK, kt)
    return pl.pallas_call(
        body, name="sb_bwd", grid=(SB_HEADS // hp, s // QBLK),
        in_specs=[qspec, full, full, qspec],
        out_specs=[qspec, full, full], out_shape=[sh, sh, sh],
        scratch_shapes=[pltpu.VMEM(tiles_sh, F32), pltpu.VMEM(tiles_sh, F32)],
        compiler_params=_params(("arbitrary", "arbitrary")),
    )(q, k, v, do)


def _t5_bucket_np(dist):
    max_exact = N_BUCKETS // 2
    d = np.maximum(dist, 1).astype(np.float32)
    large = max_exact + (np.log(d / np.float32(max_exact)) / np.float32(math.log(MAX_DISTANCE / max_exact))
                         * np.float32(N_BUCKETS - max_exact)).astype(np.int32)
    large = np.minimum(large, N_BUCKETS - 1)
    return np.where(dist < max_exact, dist, large)


def _dil_layout(s):
    assert all(s % (QBLK * r) == 0 and window // r == QBLK for window, r in DIL_PATTERNS)
    tiles, buckets = [], []
    i = np.arange(QBLK)[:, None]
    j = np.arange(QBLK)[None, :]
    for g, (window, r) in enumerate(DIL_PATTERNS):
        for off in (0, 1):
            dist = QBLK * off + i - j
            ok = (dist >= 0) & (dist <= window // r)
            tiles.append((g, off))
            buckets.append(np.where(ok, _t5_bucket_np(np.maximum(dist, 0) * r), -1).astype(np.int32))
    return tiles, np.stack(buckets)


def _bias_tiles(rel_bias, s):
    tiles, buckets = _dil_layout(s)
    nt = len(tiles)
    present = [sorted(set(np.unique(buckets[k]).tolist()) - {-1}) for k in range(nt)]

    def body(rel_ref, b_ref, o_ref):
        j = pl.program_id(0)
        for k, (g, _) in enumerate(tiles):
            bk = b_ref[k]
            tile = jnp.full((QBLK, QBLK), NEG_INF, F32)
            for b in present[k]:
                tile = jnp.where(bk == b, rel_ref[b, g * DIL_GROUP + j], tile)
            o_ref[0, k] = tile

    return pl.pallas_call(
        body, name="bias_tiles", grid=(DIL_GROUP,),
        in_specs=[pl.BlockSpec(memory_space=pltpu.SMEM), pl.BlockSpec((nt, QBLK, QBLK), lambda j: (0, 0, 0))],
        out_specs=pl.BlockSpec((1, nt, QBLK, QBLK), lambda j: (j, 0, 0, 0)),
        out_shape=jax.ShapeDtypeStruct((DIL_GROUP, nt, QBLK, QBLK), F32),
        compiler_params=_params(("arbitrary",)),
    )(rel_bias, jnp.asarray(buckets))


def _bias_tiles_bwd(dbias, s):
    tiles, buckets = _dil_layout(s)
    nt = len(tiles)
    present = [sorted(set(np.unique(buckets[k]).tolist()) - {-1}) for k in range(nt)]

    def body(d_ref, b_ref, o_ref):
        j = pl.program_id(0)

        @pl.when(j == 0)
        def _():
            for b in range(N_BUCKETS):
                for col in range(3 * DIL_GROUP):
                    o_ref[b, col] = jnp.float32(0.0)

        for k, (g, _) in enumerate(tiles):
            bk, dk = b_ref[k], d_ref[0, k]
            for b in present[k]:
                o_ref[b, g * DIL_GROUP + j] += jnp.sum(jnp.where(bk == b, dk, 0.0))

    return pl.pallas_call(
        body, name="bias_tiles_bwd", grid=(DIL_GROUP,),
        in_specs=[pl.BlockSpec((1, nt, QBLK, QBLK), lambda j: (j, 0, 0, 0)),
                  pl.BlockSpec((nt, QBLK, QBLK), lambda j: (0, 0, 0))],
        out_specs=pl.BlockSpec(memory_space=pltpu.SMEM),
        out_shape=jax.ShapeDtypeStruct((N_BUCKETS, 3 * DIL_GROUP), F32),
        compiler_params=_params(("arbitrary",)),
    )(dbias, jnp.asarray(buckets))


DIL_PAIRS_PER_STEP = 8


def _dil_rows(g, s, pair):
    _, r = DIL_PATTERNS[g]
    nb = s // (QBLK * r)
    c, n = lax.div(pair, jnp.int32(nb)), lax.rem(pair, jnp.int32(nb))
    start = c + (r * QBLK) * n
    before = jnp.where(n > 0, start - r * QBLK, start)
    if r == 1:
        return pl.ds(start, QBLK), pl.ds(before, QBLK), n > 0
    return pl.ds(start, QBLK, stride=r), pl.ds(before, QBLK, stride=r), n > 0


def _dil_logits(qb, k_ref, rows, before, has_before, b_ref):
    k0, k1 = k_ref[0, rows, :].astype(BF16), k_ref[0, before, :].astype(BF16)
    l0 = _dot(qb, k0, NT2) + b_ref[0, 0]
    l1 = jnp.where(has_before, _dot(qb, k1, NT2) + b_ref[0, 1], NEG_INF)
    return k0, k1, l0, l1


def _dil_group_specs(g, s):
    head = pl.BlockSpec((1, s, HEAD), lambda j, p: (DIL_GROUP * g + j, 0, 0))
    return [head, head, head, pl.BlockSpec((1, 2, QBLK, QBLK), lambda j, p: (j, g, 0, 0))]


def _dil_group_fwd(g, qn, kn, v, bias):
    _, s, _ = qn.shape
    scale = HEAD ** -0.5
    steps = (s // QBLK) // DIL_PAIRS_PER_STEP

    def body(q_ref, k_ref, v_ref, b_ref, o_ref):
        pairs = [_dil_rows(g, s, pl.program_id(1) * DIL_PAIRS_PER_STEP + u) for u in range(DIL_PAIRS_PER_STEP)]
        loaded = []
        for rows, before, has_before in pairs:
            qb = (q_ref[0, rows, :] * scale).astype(BF16)
            _, _, l0, l1 = _dil_logits(qb, k_ref, rows, before, has_before, b_ref)
            loaded.append((l0, l1, v_ref[0, rows, :].astype(BF16), v_ref[0, before, :].astype(BF16)))
        results = []
        for l0, l1, v0, v1 in loaded:
            m = jnp.max(jnp.maximum(l0, l1), axis=1, keepdims=True)
            p0, p1 = jnp.exp(l0 - m), jnp.exp(l1 - m)
            den = jnp.sum(p0 + p1, axis=1, keepdims=True)
            inv = 1.0 / den
            o = _dot((p0 * inv).astype(BF16), v0) + _dot((p1 * inv).astype(BF16), v1)
            results.append(jnp.concatenate([o, jnp.broadcast_to(m + jnp.log(den), (QBLK, HEAD))], axis=1))
        for (rows, _, _), res in zip(pairs, results):
            o_ref[0, rows, :] = res

    return pl.pallas_call(
        body, name="dil%d_fwd" % g, grid=(DIL_GROUP, steps), in_specs=_dil_group_specs(g, s),
        out_specs=pl.BlockSpec((1, s, 2 * HEAD), lambda j, p: (j, 0, 0)),
        out_shape=jax.ShapeDtypeStruct((DIL_GROUP, s, 2 * HEAD), F32),
        compiler_params=_params(("arbitrary", "arbitrary")),
    )(qn, kn, v, bias)


def _dil_group_bwd(g, qn, kn, v, bias, ol, dol, prev):
    _, s, _ = qn.shape
    scale = HEAD ** -0.5
    steps = (s // QBLK) // DIL_PAIRS_PER_STEP
    prev = list(prev) if prev is not None else []

    def body(q_ref, k_ref, v_ref, b_ref, ol_ref, dol_ref, *rest):
        dq_ref, dk_ref, dv_ref, db_ref = rest[-4:]

        @pl.when(pl.program_id(1) == 0)
        def _():
            for r in (dk_ref, dv_ref, db_ref):
                r[...] = jnp.zeros_like(r)

        pairs = [_dil_rows(g, s, pl.program_id(1) * DIL_PAIRS_PER_STEP + u) for u in range(DIL_PAIRS_PER_STEP)]
        loaded = []
        for rows, before, has_before in pairs:
            qb = (q_ref[0, rows, :] * scale).astype(BF16)
            k0, k1, l0, l1 = _dil_logits(qb, k_ref, rows, before, has_before, b_ref)
            v0, v1 = v_ref[0, rows, :].astype(BF16), v_ref[0, before, :].astype(BF16)
            loaded.append((qb, k0, k1, l0, l1, v0, v1, ol_ref[0, rows, :], dol_ref[0, rows, :]))
        grads = []
        for qb, k0, k1, l0, l1, v0, v1, out_lse, d_out_lse in loaded:
            o, lse = out_lse[:, :HEAD], out_lse[:, HEAD:HEAD + 1]
            do, dlse = d_out_lse[:, :HEAD], d_out_lse[:, HEAD:HEAD + 1]
            dob = do.astype(BF16)
            p0, p1 = jnp.exp(l0 - lse), jnp.exp(l1 - lse)
            shift = dlse - jnp.sum(do * o, axis=1, keepdims=True)
            dl0 = p0 * (_dot(dob, v0, NT2) + shift)
            dl1 = p1 * (_dot(dob, v1, NT2) + shift)
            dl0b, dl1b = dl0.astype(BF16), dl1.astype(BF16)
            grads.append(((_dot(dl0b, k0) + _dot(dl1b, k1)) * scale,
                          _dot(dl0b, qb, TN2), _dot(dl1b, qb, TN2),
                          _dot(p0.astype(BF16), dob, TN2), _dot(p1.astype(BF16), dob, TN2), dl0, dl1))
        db0 = functools.reduce(jnp.add, [gr[5] for gr in grads])
        db1 = functools.reduce(jnp.add, [gr[6] for gr in grads])
        for (rows, before, _), (dq, dk0, dk1, dv0, dv1, _, _) in zip(pairs, grads):
            dq_ref[0, rows, :] = dq
            dk_ref[0, rows, :] += dk0
            dk_ref[0, before, :] += dk1
            dv_ref[0, rows, :] += dv0
            dv_ref[0, before, :] += dv1
        db_ref[0, 0] += db0
        db_ref[0, 1] += db1

    head_out = pl.BlockSpec((1, s, HEAD), lambda j, p: (DIL_GROUP * g + j, 0, 0))
    rows128 = pl.BlockSpec((1, s, 2 * HEAD), lambda j, p: (j, 0, 0))
    full_sh = jax.ShapeDtypeStruct(qn.shape, F32)
    return pl.pallas_call(
        body, name="dil%d_bwd" % g, grid=(DIL_GROUP, steps),
        in_specs=_dil_group_specs(g, s) + [rows128, rows128] + [pl.BlockSpec(memory_space=pl.ANY)] * len(prev),
        out_specs=[head_out, head_out, head_out, pl.BlockSpec((1, 2, QBLK, QBLK), lambda j, p: (j, 0, 0, 0))],
        out_shape=[full_sh, full_sh, full_sh, jax.ShapeDtypeStruct((DIL_GROUP, 2, QBLK, QBLK), F32)],
        input_output_aliases={6 + i: i for i in range(len(prev))},
        compiler_params=_params(("arbitrary", "arbitrary")),
    )(qn, kn, v, bias, ol, dol, *prev)


@functools.partial(jax.custom_vjp, nondiff_argnums=(2,))
def _bdot(a, b, dims):
    return _dot(a.astype(BF16), b.astype(BF16), dims)


def _bdot_fwd(a, b, dims):
    return _bdot(a, b, dims), (a, b)


def _bdot_bwd(dims, res, dc):
    a, b = res
    nn, nt, tn = (NN2, NT2, TN2) if dims in (NN2, NT2, TN2) else (NN3, NT3, TN3)
    if dims == nn:
        return _bdot(dc, b, nt), _bdot(a, dc, tn)
    if dims == nt:
        return _bdot(dc, b, nn), _bdot(dc, a, tn)
    return _bdot(b, dc, nt), _bdot(a, dc, nn)


_bdot.defvjp(_bdot_fwd, _bdot_bwd)


def _ones_dot(ones, x, dims):
    o = ones.astype(BF16)
    hi = x.astype(BF16)
    r1 = x - hi.astype(F32)
    mid = r1.astype(BF16)
    lo = (r1 - mid.astype(F32)).astype(BF16)
    return _dot(o, hi, dims) + _dot(o, mid, dims) + _dot(o, lo, dims)


@jax.custom_vjp
def _prefix_sums(x):
    c = x.shape[1]
    row = lax.broadcasted_iota(jnp.int32, (x.shape[0], c, c), 1)
    col = lax.broadcasted_iota(jnp.int32, (x.shape[0], c, c), 2)
    return _ones_dot((row >= col).astype(F32), x, NN3)


def _prefix_sums_fwd(x):
    return _prefix_sums(x), None


def _prefix_sums_bwd(_, dy):
    c = dy.shape[1]
    row = lax.broadcasted_iota(jnp.int32, (dy.shape[0], c, c), 1)
    col = lax.broadcasted_iota(jnp.int32, (dy.shape[0], c, c), 2)
    return (_ones_dot((row <= col).astype(F32), dy, NN3),)


_prefix_sums.defvjp(_prefix_sums_fwd, _prefix_sums_bwd)


def _rwkv_chunk(s0, r, lw, kraw, v, ag, kk_w, ka_w, rk_w, lng, lnb):
    hb, c, _ = r.shape
    kk = kraw * kk_w
    kk = kk / jnp.maximum(jnp.sqrt(jnp.sum(kk * kk, axis=-1, keepdims=True)), 1e-12)
    k = kraw * (1.0 + (ag - 1.0) * ka_w)
    a = -kk
    b = kk * ag
    row = lax.broadcasted_iota(jnp.int32, (hb, c, c), 1)
    col = lax.broadcasted_iota(jnp.int32, (hb, c, c), 2)
    lower, strict = row >= col, row > col
    cum = _prefix_sums(lw)
    ecum, einv = jnp.exp(cum), jnp.exp(-cum)
    rt, kt, bt = r * ecum, k * einv, b * einv
    at = a * jnp.exp(cum - lw)
    ar = jnp.concatenate([at, rt], axis=1)
    scores = _bdot(ar, jnp.concatenate([bt, kt], axis=1), NT3)
    a_ab = jnp.where(strict, scores[:, :c, :c], 0.0)
    a_ak = jnp.where(strict, scores[:, :c, c:], 0.0)
    p_rb = jnp.where(lower, scores[:, c:, :c], 0.0)
    p_rk = jnp.where(lower, scores[:, c:, c:], 0.0)
    from_s0 = _bdot(ar, s0, NT3)
    rhs = from_s0[:, :c] + _bdot(a_ak, v, NN3)
    inv = (row == col).astype(F32) + a_ab
    pw = a_ab
    for _ in range(int(math.log2(c)) - 1):
        pw = _bdot(pw, pw, NN3)
        inv = inv + _bdot(inv, pw, NN3)
    u = _bdot(inv, rhs, NN3)
    uv = jnp.concatenate([u, v], axis=1)
    y = from_s0[:, c:] + _bdot(jnp.concatenate([p_rb, p_rk], axis=2), uv, NN3)
    cum_end = cum[:, c - 1:c, :]
    dec = jnp.exp(cum_end - cum)
    s_end = s0 * jnp.exp(cum_end) + _bdot(uv, jnp.concatenate([b * dec, k * dec], axis=1), TN3)
    mu = jnp.mean(y, axis=-1, keepdims=True)
    var = jnp.mean(jnp.square(y - mu), axis=-1, keepdims=True)
    z = (y - mu) * lax.rsqrt(var + GN_EPS) * lng + lnb + jnp.sum(r * k * rk_w, axis=-1, keepdims=True) * v
    return z, s_end


def _rwkv_specs(nc, rev):
    cidx = (lambda c: nc - 1 - c) if rev else (lambda c: c)
    seq = pl.BlockSpec((RW_HB, RW_CHUNK, HEAD), lambda hg, c: (hg, cidx(c), 0))
    par = pl.BlockSpec((RW_HB, 1, HEAD), lambda hg, c: (hg, 0, 0))
    st = pl.BlockSpec((1, RW_HB, HEAD, HEAD), lambda hg, c: (cidx(c), hg, 0, 0))
    return seq, par, st


def _rwkv_fwd(seqs, pars):
    s = seqs[0].shape[1]
    nc = s // RW_CHUNK

    def body(*refs):
        seq_refs, par_refs = refs[:5], refs[5:10]
        z_ref, st_ref, state = refs[10:]
        c = pl.program_id(1)

        @pl.when(c == 0)
        def _():
            state[...] = jnp.zeros_like(state)

        s0 = state[...]
        st_ref[0] = s0
        z, s_end = _rwkv_chunk(s0, *[r[...] for r in seq_refs], *[r[...] for r in par_refs])
        z_ref[...] = z
        state[...] = s_end

    seq, par, st = _rwkv_specs(nc, False)
    return pl.pallas_call(
        body, name="rwkv_fwd", grid=(N_HEADS // RW_HB, nc),
        in_specs=[seq] * 5 + [par] * 5, out_specs=[seq, st],
        out_shape=[jax.ShapeDtypeStruct((N_HEADS, s, HEAD), F32), jax.ShapeDtypeStruct((nc, N_HEADS, HEAD, HEAD), F32)],
        scratch_shapes=[pltpu.VMEM((RW_HB, HEAD, HEAD), F32)],
        compiler_params=_params(("arbitrary", "arbitrary")),
    )(*seqs, *pars)


def _rwkv_bwd(seqs, pars, states, dz):
    s = seqs[0].shape[1]
    nc = s // RW_CHUNK

    def body(*refs):
        seq_refs, par_refs = refs[:5], refs[5:10]
        st_ref, dz_ref = refs[10:12]
        dseq_refs, dpar_refs, dstate = refs[12:17], refs[17:22], refs[22]
        c = pl.program_id(1)

        @pl.when(c == 0)
        def _():
            dstate[...] = jnp.zeros_like(dstate)
            for r in dpar_refs:
                r[...] = jnp.zeros_like(r)

        _, vjp = jax.vjp(_rwkv_chunk, st_ref[0], *[r[...] for r in seq_refs], *[r[...] for r in par_refs])
        g = vjp((dz_ref[...], dstate[...]))
        dstate[...] = g[0]
        for r, gs in zip(dseq_refs, g[1:6]):
            r[...] = gs
        for r, gp in zip(dpar_refs, g[6:]):
            r[...] += gp

    seq, par, st = _rwkv_specs(nc, True)
    seq_sh = jax.ShapeDtypeStruct((N_HEADS, s, HEAD), F32)
    par_sh = jax.ShapeDtypeStruct((N_HEADS, 1, HEAD), F32)
    outs = pl.pallas_call(
        body, name="rwkv_bwd", grid=(N_HEADS // RW_HB, nc),
        in_specs=[seq] * 5 + [par] * 5 + [st, seq],
        out_specs=[seq] * 5 + [par] * 5, out_shape=[seq_sh] * 5 + [par_sh] * 5,
        scratch_shapes=[pltpu.VMEM((RW_HB, HEAD, HEAD), F32)],
        compiler_params=_params(("arbitrary", "arbitrary")),
    )(*seqs, *pars, states, dz)
    return list(outs[:5]), list(outs[5:])


def _norm_fn(x, g):
    return (_rms(x, g),)


def _attn_prep_fn(proj, qn_w, kn_w):
    a, b = SB_HEADS, 3 * DIL_GROUP
    return (proj[0:a], proj[a:2 * a], proj[2 * a:3 * a],
            _rms(proj[3 * a:3 * a + b], qn_w), _rms(proj[3 * a + b:3 * a + 2 * b], kn_w), proj[3 * a + 2 * b:])


def _attn_merge_fn(o_sb, ol0, ol1, ol2):
    groups = (ol0, ol1, ol2)
    merged = []
    for j in range(DIL_GROUP):
        lses = [ol[j][:, HEAD:HEAD + 1] for ol in groups]
        m = functools.reduce(jnp.maximum, lses)
        es = [jnp.exp(l - m) for l in lses]
        inv = 1.0 / functools.reduce(jnp.add, es)
        merged.append(functools.reduce(jnp.add, [(e * inv) * ol[j][:, :HEAD] for e, ol in zip(es, groups)]))
    return (jnp.concatenate([_heads_to_nat(o_sb)] + merged, axis=-1),)


def _rw_mix_fn(x, xp, gn, mix, w0, w1, w2, a0, a1, a2, g1, g2):
    h = _rms(x, gn)
    xx = _rms(xp, gn) - h
    xr, xw, xk, xv, xa, xg = [h + xx * mix[i:i + 1] for i in range(6)]
    w_log = -jax.nn.softplus(-(w0 + _mm(jnp.tanh(_mm(xw, w1)), w2))) - 0.5
    lw = -jnp.exp(w_log)
    ag = jax.nn.sigmoid(a0 + _mm(_mm(xa, a1), a2))
    gate = _mm(jax.nn.sigmoid(_mm(xg, g1)), g2)
    return xr, xk, xv, _nat_to_heads(lw), _nat_to_heads(ag), gate


def _rw_gate_fn(z, gate):
    return (_heads_to_nat(z) * gate,)


def _adamw(name, w, m, v, gparts, row0=0, prev=None):
    big_r, c = w.shape
    r = gparts.shape[1]
    tr = r
    if r % 8 == 0:
        tr = max(t for t in range(8, r + 1, 8) if r % t == 0 and (t * c * 4 <= (1 << 20) or t == 8))
    assert row0 % tr == 0 and (r == big_r or r % 8 == 0)
    off = row0 // tr

    def body(w_ref, m_ref, v_ref, g_ref, *rest):
        go_ref, d_ref, mo_ref, vo_ref = rest[-4:]
        g = g_ref[0].astype(F32)
        for j in range(1, N_DEV):
            g = g + g_ref[j].astype(F32)
        mn = ADAM_B1 * m_ref[...] + (1.0 - ADAM_B1) * g
        vn = ADAM_B2 * v_ref[...] + (1.0 - ADAM_B2) * jnp.square(g)
        m_hat = mn / (1.0 - ADAM_B1 ** ADAM_STEP)
        v_hat = vn / (1.0 - ADAM_B2 ** ADAM_STEP)
        go_ref[...] = g
        d_ref[...] = -ADAM_LR * (m_hat / (jnp.sqrt(v_hat) + ADAM_EPS) + ADAM_WD * w_ref[...])
        mo_ref[...] = mn
        vo_ref[...] = vn

    tile = pl.BlockSpec((tr, c), lambda i: (i + off, 0))
    sh = jax.ShapeDtypeStruct((big_r, c), F32)
    prev = list(prev) if prev is not None else []
    return pl.pallas_call(
        body, name=name, grid=(r // tr,),
        in_specs=([tile, tile, tile, pl.BlockSpec((N_DEV, tr, c), lambda i: (0, i, 0))]
                  + [pl.BlockSpec(memory_space=pl.ANY)] * len(prev)),
        out_specs=[tile] * 4, out_shape=[sh] * 4,
        input_output_aliases={4 + j: j for j in range(len(prev))},
        compiler_params=_params(("arbitrary",)),
    )(w, m, v, gparts, *prev)


def _col_blocks_to_nat(g):
    return jnp.moveaxis(g, 0, 1).reshape(g.shape[1], -1)


def _nat_to_col_blocks(a):
    return jnp.moveaxis(a.reshape(a.shape[0], N_DEV, -1), 1, 0)


AG_GROUPS = ("f00", "att", "f01", "f10", "rw", "f11")
RS_GROUPS = ("f11", "rw", "f10", "f01", "f00", "att")
BF16_GRAD_GROUPS = ("att", "f00")
RW_SHARDED = ('rw_mix', 'rw_w0', 'rw_w1', 'rw_w2', 'rw_a0', 'rw_a1', 'rw_a2', 'rw_g1', 'rw_g2', 'rw_kk', 'rw_ka',
              'rw_wr', 'rw_wk', 'rw_wv', 'rw_wo', 'rw_lnx_g', 'rw_lnx_b')


def _step(x, target, rep, get, put):
    tied = lambda a, zero: a + zero[0, 0].astype(a.dtype)
    s, d = x.shape
    tf = min(512, s)
    tt = min(256, s)
    row = lambda a: a.reshape(1, -1)
    mix_norm = rep["mix_norm"]
    ffw = {(0, 0): get("f00", None)}
    ffn_norm = _col_blocks_to_nat(ffw[(0, 0)]["ffn_norm"].reshape(N_DEV, 4, -1))

    acts = {}

    def ffn(nm, xin, l, h):
        g = ffw[(l, h)]
        out, *acts[(l, h)] = _ffn_fwd(nm, xin, ffn_norm[2 * l + h][None], g["gate"], g["up"], g["down"], min(2 * tf, s))
        return out

    x1 = ffn("ffn00_fwd", x, 0, 0)
    att = get("att", x1)
    w_in = _col_blocks_to_nat(att["attn_w_in"])
    w_out = _col_blocks_to_nat(att["attn_w_out"])
    (h0,) = _tile_fwd("mixnorm0_fwd", _norm_fn, [(x1, "nat")], [mix_norm[0:1]], [((s, d), BF16, "nat")], tt)
    proj = _linear_fwd("attn_in_fwd", h0, w_in, tt, out_layout="hm")
    bias = _bias_tiles(rep["rel_bias"], s)
    prep_pars = [rep["attn_q_norm"], rep["attn_k_norm"]]
    sb_sh, dl_sh = (SB_HEADS, s, HEAD), (3 * DIL_GROUP, s, HEAD)
    sq, sk, sv, qn, kn, vd = _tile_fwd("attn_prep_fwd", _attn_prep_fn, [(proj, "hm")], prep_pars,
                                       [(sb_sh, BF16, "hm")] * 3 + [(dl_sh, F32, "hm")] * 3, tt // 2)
    o_sb = _sb_fwd(sq, sk, sv)
    ols = [_dil_group_fwd(g, qn, kn, vd, bias) for g in range(3)]
    merge_tiled = [(o_sb, "hm")] + [(ol, "hm") for ol in ols]
    (merged,) = _tile_fwd("merge_fwd", _attn_merge_fn, merge_tiled, [], [((s, 512), BF16, "nat")], tt)
    x2 = _linear_fwd("attn_out_fwd", merged, w_out, tt, residual=x1)
    ffw[(0, 1)] = get("f01", x2)
    x3 = ffn("ffn01_fwd", x2, 0, 1)
    ffw[(1, 0)] = get("f10", x3)
    x4 = ffn("ffn10_fwd", x3, 1, 0)
    rw = get("rw", x4)
    rw_mix = _col_blocks_to_nat(rw["rw_mix"])
    rw_w1, rw_a1, rw_g1 = (rw[k].reshape(d, -1) for k in ("rw_w1", "rw_a1", "rw_g1"))
    rw_w2, rw_a2, rw_g2 = (_col_blocks_to_nat(rw[k]) for k in ("rw_w2", "rw_a2", "rw_g2"))
    rw_w0, rw_a0 = row(rw["rw_w0"]), row(rw["rw_a0"])
    head_par = lambda a: a.reshape(N_HEADS, 1, HEAD)
    scan_pars = [head_par(rw["rw_kk"]), head_par(rw["rw_ka"]), head_par(rep["rw_rk"]),
                 head_par(rw["rw_lnx_g"]), head_par(rw["rw_lnx_b"])]
    w_rkv = [rw[k].reshape(d, d) for k in ("rw_wr", "rw_wk", "rw_wv")]
    w_o = rw["rw_wo"].reshape(d, d)
    x4p = jnp.pad(x4, ((1, 0), (0, 0)))[:-1]
    mix_tiled = [(x4, "nat"), (x4p, "nat")]
    mix_pars = [mix_norm[1:2], rw_mix, rw_w0, rw_w1, rw_w2, rw_a0, rw_a1, rw_a2, rw_g1, rw_g2]
    hm_sh = (N_HEADS, s, HEAD)
    xr, xk, xv, lw, ag, gate = _tile_fwd(
        "rw_mix_fwd", _rw_mix_fn, mix_tiled, mix_pars,
        [((s, d), BF16, "nat")] * 3 + [(hm_sh, F32, "hm")] * 2 + [((s, d), F32, "nat")], tt)
    r_h, k_h, v_h = [_linear_fwd("rw_%s_fwd" % nm, xi, wi, tt, out_layout="hm")
                     for nm, xi, wi in zip("rkv", (xr, xk, xv), w_rkv)]
    scan_seqs = [r_h, lw, k_h, v_h, ag]
    z, states = _rwkv_fwd(scan_seqs, scan_pars)
    (zg,) = _tile_fwd("rw_gate_fwd", _rw_gate_fn, [(z, "hm"), (gate, "nat")], [], [((s, d), BF16, "nat")], tt)
    x5 = _linear_fwd("rw_out_fwd", zg, w_o, tt, residual=x4)
    ffw[(1, 1)] = get("f11", x5)
    y = ffn("ffn11_fwd", x5, 1, 1)
    dy, loss = _loss_head(y, target, tf)

    G = {}
    dgn = {}

    def fb(nm, group, xin, dout, l, h, zero=None, extra=None):
        g = ffw[(l, h)]
        gn = ffn_norm[2 * l + h][None]
        dxin, dgn[(l, h)], dwg, dwu, dwd = _ffn_bwd(nm, xin, dout, gn if zero is None else tied(gn, zero),
                                                   g["gate"], g["up"], g["down"], *acts[(l, h)], tf)
        shard = {"gate": dwg, "up": dwu, "down": dwd}
        if extra is not None:
            shard.update(extra())
        return dxin, put(group, {}, shard)

    dx5, zero = fb("ffn11_bwd", "f11", x5, dy, 1, 1)
    dzg = _linear_dx("rw_out_dx", dx5, w_o, tt)
    G["rw_wo"] = _linear_dw("rw_out_dw", zg, dx5, tf, 512)
    (dz, dgate), _ = _tile_bwd("rw_gate_bwd", _rw_gate_fn, [(z, "hm"), (gate, "nat")], [], [(dzg, "nat")], tt, [True, True])
    (dr_h, dlw, dk_h, dv_h, dag), dscan = _rwkv_bwd(scan_seqs, [tied(scan_pars[0], zero)] + scan_pars[1:], states, dz)
    drkv = (dr_h, dk_h, dv_h)
    for k, gpar in zip(("rw_kk", "rw_ka", "rw_rk", "rw_lnx_g", "rw_lnx_b"), dscan):
        G[k] = gpar
    dxs = []
    for j, (nm, xi, wi) in enumerate(zip("rkv", (xr, xk, xv), w_rkv)):
        dxs.append(_linear_dx("rw_%s_dx" % nm, drkv[j], wi, tt, dy_layout="hm"))
        G["rw_w" + nm] = _linear_dw("rw_%s_dw" % nm, xi, drkv[j], tf, 512, dy_layout="hm")
    (dx4a, dx4p), dmix = _tile_bwd(
        "rw_mix_bwd", _rw_mix_fn, mix_tiled, mix_pars,
        [(dxs[0], "nat"), (dxs[1], "nat"), (dxs[2], "nat"), (dlw, "hm"), (dag, "hm"), (dgate, "nat")],
        tt, [True, True], adds=[dx5, None])
    d_mixn1 = dmix[0]
    for k, gpar in zip(("rw_mix", "rw_w0", "rw_w1", "rw_w2", "rw_a0", "rw_a1", "rw_a2", "rw_g1", "rw_g2"), dmix[1:]):
        G[k] = gpar
    dx4 = dx4a + jnp.pad(dx4p[1:], ((0, 1), (0, 0)))
    for k in ("rw_mix", "rw_w2", "rw_a2", "rw_g2"):
        G[k] = _nat_to_col_blocks(G[k])
    for k in ("rw_w1", "rw_a1", "rw_g1", "rw_wr", "rw_wk", "rw_wv", "rw_wo"):
        G[k] = G[k].reshape(N_DEV, d // N_DEV, -1)
    for k in ("rw_w0", "rw_a0", "rw_kk", "rw_ka", "rw_lnx_g", "rw_lnx_b"):
        G[k] = G[k].reshape(N_DEV, 1, d // N_DEV)
    zero = put("rw", {"rw_rk": G["rw_rk"].reshape(N_HEADS, HEAD)}, {k: G[k] for k in RW_SHARDED})
    dx3, zero = fb("ffn10_bwd", "f10", x3, dx4, 1, 0, zero)
    dx2, zero = fb("ffn01_bwd", "f01", x2, dx3, 0, 1, zero)
    dmerged = _linear_dx("attn_out_dx", dx2, tied(w_out, zero), tt)
    (do_sb, *dols), _ = _tile_bwd("merge_bwd", _attn_merge_fn, merge_tiled, [], [(dmerged, "nat")], tt, [True] * 4)
    dq_sb, dk_sb, dv_sb = _sb_bwd(sq, sk, sv, do_sb)
    dil_grads, dbias = None, []
    for g in range(3):
        *dil_grads, db = _dil_group_bwd(g, qn, kn, vd, bias, ols[g], dols[g], dil_grads)
        dbias.append(db)
    dqn, dkn, dvd = dil_grads
    dbias = jnp.concatenate(dbias, axis=1)
    (dproj,), (dqn_w, dkn_w) = _tile_bwd(
        "attn_prep_bwd", _attn_prep_fn, [(proj, "hm")], prep_pars,
        [(dq_sb, "hm"), (dk_sb, "hm"), (dv_sb, "hm"), (dqn, "hm"), (dkn, "hm"), (dvd, "hm")], tt // 2, [True])
    dh0 = _linear_dx("attn_in_dx", dproj, w_in, tt, dy_layout="hm")
    (dx1,), (d_mixn0,) = _tile_bwd("mixnorm0_bwd", _norm_fn, [(x1, "nat")], [mix_norm[0:1]], [(dh0, "nat")], tt,
                                   [True], adds=[dx2])
    order = [(0, 0), (0, 1), (1, 0), (1, 1)]
    norm_grads = lambda: {"ffn_norm": _nat_to_col_blocks(jnp.concatenate([dgn[o] for o in order], axis=0))}
    dx0, zero = fb("ffn00_bwd", "f00", x, dx1, 0, 0, extra=norm_grads)
    G["attn_w_out"] = _linear_dw("attn_out_dw", tied(merged, zero), dx2, tf, 512)
    G["attn_w_in"] = _linear_dw("attn_in_dw", tied(h0, zero), dproj, tf, 512, dy_layout="hm")
    rep_grads = {"mix_norm": jnp.concatenate([d_mixn0, d_mixn1], axis=0), "rel_bias": _bias_tiles_bwd(dbias, s),
                 "attn_q_norm": dqn_w, "attn_k_norm": dkn_w}
    zero = put("att", rep_grads, {k: _nat_to_col_blocks(G[k]) for k in ("attn_w_in", "attn_w_out")})
    return loss, dx0, zero


WEIGHTS = ['ffn_norm', 'ffn_w_gate', 'ffn_w_up', 'ffn_w_down', 'mix_norm', 'rel_bias', 'attn_w_in', 'attn_q_norm',
           'attn_k_norm', 'attn_w_out', 'rw_mix', 'rw_w0', 'rw_w1', 'rw_w2', 'rw_a0', 'rw_a1', 'rw_a2', 'rw_g1', 'rw_g2',
           'rw_kk', 'rw_ka', 'rw_rk', 'rw_wr', 'rw_wk', 'rw_wv', 'rw_wo', 'rw_lnx_g', 'rw_lnx_b']
REPLICATED = ('mix_norm', 'rel_bias', 'attn_q_norm', 'attn_k_norm', 'rw_rk')
BF16_WEIGHTS = ('ffn_w_gate', 'ffn_w_up', 'ffn_w_down', 'attn_w_in', 'attn_w_out', 'rw_wr', 'rw_wk', 'rw_wv', 'rw_wo')


def kernel(x, ffn_norm, ffn_w_gate, ffn_w_up, ffn_w_down, mix_norm, rel_bias, attn_w_in, attn_q_norm, attn_k_norm, attn_w_out, rw_mix, rw_w0, rw_w1, rw_w2, rw_a0, rw_a1, rw_a2, rw_g1, rw_g2, rw_kk, rw_ka, rw_rk, rw_wr, rw_wk, rw_wv, rw_wo, rw_lnx_g, rw_lnx_b, loss_target, m_ffn_norm, m_ffn_w_gate, m_ffn_w_up, m_ffn_w_down, m_mix_norm, m_rel_bias, m_attn_w_in, m_attn_q_norm, m_attn_k_norm, m_attn_w_out, m_rw_mix, m_rw_w0, m_rw_w1, m_rw_w2, m_rw_a0, m_rw_a1, m_rw_a2, m_rw_g1, m_rw_g2, m_rw_kk, m_rw_ka, m_rw_rk, m_rw_wr, m_rw_wk, m_rw_wv, m_rw_wo, m_rw_lnx_g, m_rw_lnx_b, v_ffn_norm, v_ffn_w_gate, v_ffn_w_up, v_ffn_w_down, v_mix_norm, v_rel_bias, v_attn_w_in, v_attn_q_norm, v_attn_k_norm, v_attn_w_out, v_rw_mix, v_rw_w0, v_rw_w1, v_rw_w2, v_rw_a0, v_rw_a1, v_rw_a2, v_rw_g1, v_rw_g2, v_rw_kk, v_rw_ka, v_rw_rk, v_rw_wr, v_rw_wk, v_rw_wv, v_rw_wo, v_rw_lnx_g, v_rw_lnx_b):
    args = locals()
    w = {k: args[k] for k in WEIGHTS}
    cast = lambda k, a: a.astype(BF16) if k in BF16_WEIGHTS else a

    sources = {}
    for l, h in ((0, 0), (0, 1), (1, 0), (1, 1)):
        sources["f%d%d" % (l, h)] = {"gate": cast("ffn_w_gate", ffn_w_gate[l, h]), "up": cast("ffn_w_up", ffn_w_up[l, h]),
                                     "down": cast("ffn_w_down", ffn_w_down[l, h])}
    sources["f00"]["ffn_norm"] = ffn_norm
    drop_lead = lambda a: a[0] if a.ndim == 3 else a
    sources["att"] = {k: cast(k, w[k][0]) for k in ("attn_w_in", "attn_w_out")}
    sources["rw"] = {k: cast(k, drop_lead(w[k])) for k in RW_SHARDED}
    ag, token = {}, None
    for group in AG_GROUPS:
        names, arrays = list(sources[group]), list(sources[group].values())
        if token is not None:
            arrays[0] = arrays[0] + token[0, 0].astype(arrays[0].dtype)
        ag[group] = (names, _exchange_start("ag_start_" + group, arrays, []))
        token = ag[group][1]["token"]
    last_ag_token = token

    def get(group, after):
        names, started = ag[group]
        gathered, _ = _exchange_wait("ag_wait_" + group, started, last_ag_token if after is None else after)
        return dict(zip(names, gathered))

    rs = {}

    def put(group, rep_grads, shard_grads):
        if group in BF16_GRAD_GROUPS:
            shard_grads = {k: v.astype(BF16) for k, v in shard_grads.items()}
        started = _exchange_start("rs_start_" + group, list(rep_grads.values()), list(shard_grads.values()))
        rs[group] = (list(rep_grads), list(shard_grads), started)
        return started["token"]

    loss, dx, last_zero = _step(x[0], loss_target[0], {k: w[k] for k in REPLICATED}, get, put)
    loss = lax.psum(loss, MESH_AXES)

    results = {}
    ffn_prev = {}

    def update(k, parts, row0=0, prev=None):
        c = w[k].shape[-1]
        as2d = lambda a: a.reshape(-1, c)
        return _adamw("adamw_%s_%d" % (k, row0), as2d(w[k]), as2d(args["m_" + k]), as2d(args["v_" + k]),
                      parts.reshape(N_DEV, -1, c), row0, prev)

    after = last_zero
    for group in RS_GROUPS:
        rep_names, shard_names, started = rs[group]
        rep_parts, shard_parts = _exchange_wait("rs_wait_" + group, started, after)
        for k, parts in list(zip(rep_names, rep_parts)) + list(zip(shard_names, shard_parts)):
            if k in ("gate", "up", "down"):
                full = "ffn_w_" + k
                piece = 2 * int(group[1]) + int(group[2])
                ffn_prev[full] = update(full, parts, piece * parts.shape[1], ffn_prev.get(full))
                results[full] = ffn_prev[full]
            else:
                results[k] = update(k, parts)
            after = results[k if k in results else "ffn_w_" + k][0]

    outs = [[results[k][j].reshape(w[k].shape) for k in WEIGHTS] for j in range(4)]
    return (loss, dx[None], *outs[0], *outs[1], *outs[2], *outs[3])
```

```python
import functools
import math

import numpy as np
import jax
import jax.numpy as jnp
from jax import lax
from jax.experimental import pallas as pl
from jax.experimental.pallas import tpu as pltpu

F32, BF16 = jnp.float32, jnp.bfloat16
HI = lax.Precision.HIGH

N_DEV = 8
D_MODEL = 1024
HEAD = 64
N_HEADS = 16
SB_HEADS = 4
DIL_GROUP = 4
DIL_PATTERNS = ((128, 1), (512, 4), (2048, 16))
QBLK = 128
N_BUCKETS = 32
MAX_DISTANCE = 2048
NORM_EPS = 1e-6
GN_EPS = 64e-5
NEG_INF = -1e30
RW_CHUNK = 64
RW_HB = 16
ADAM_LR, ADAM_B1, ADAM_B2, ADAM_EPS, ADAM_WD, ADAM_STEP = 0.001, 0.9, 0.999, 1e-08, 0.01, 10
MESH_AXES = ("x", "y", "c")
VMEM_LIMIT_BYTES = 56 * 1024 * 1024

NN2 = (((1,), (0,)), ((), ()))
NT2 = (((1,), (1,)), ((), ()))
TN2 = (((0,), (0,)), ((), ()))
NN3 = (((2,), (1,)), ((0,), (0,)))
NT3 = (((2,), (2,)), ((0,), (0,)))
TN3 = (((1,), (1,)), ((0,), (0,)))


def _dot(a, b, dims=NN2, prec=None):
    return lax.dot_general(a, b, dims, precision=prec, preferred_element_type=F32)


def _params(sem=None):
    return pltpu.CompilerParams(dimension_semantics=sem, vmem_limit_bytes=VMEM_LIMIT_BYTES)


@jax.custom_vjp
def _mm(x, w):
    return _dot(x.astype(BF16), w.astype(BF16))


def _mm_fwd(x, w):
    return _mm(x, w), (x, w)


def _mm_bwd(res, dy):
    x, w = res
    dyb = dy.astype(BF16)
    return (_dot(dyb, w.astype(BF16), NT2).astype(x.dtype), _dot(x.astype(BF16), dyb, TN2).astype(w.dtype))


_mm.defvjp(_mm_fwd, _mm_bwd)


def _rms(x, g):
    return x * lax.rsqrt(jnp.mean(x * x, axis=-1, keepdims=True) + NORM_EPS) * g


def _log_sigmoid(z):
    return jnp.minimum(z, 0.0) - jnp.log(1.0 + jnp.exp(-jnp.abs(z)))


def _heads_to_nat(v3):
    return jnp.concatenate([v3[h] for h in range(v3.shape[0])], axis=-1)


def _nat_to_heads(v2):
    return jnp.stack([v2[:, h * HEAD:(h + 1) * HEAD] for h in range(v2.shape[1] // HEAD)], axis=0)


def _exchange(name, gathers, scatters):
    n_g = len(gathers)
    arrays = list(gathers) + list(scatters)
    n = len(arrays)
    out_shape = [jax.ShapeDtypeStruct((N_DEV,) + a.shape, a.dtype) for a in gathers]
    out_shape += [jax.ShapeDtypeStruct(a.shape, a.dtype) for a in scatters]

    def body(*refs):
        ins, outs = refs[:n], refs[n:2 * n]
        send_sems, recv_sems, local_sems = refs[2 * n:]
        x, y, c = lax.axis_index("x"), lax.axis_index("y"), lax.axis_index("c")
        me = 4 * x + 2 * y + c

        def src(i, idx):
            return ins[i] if i < n_g else ins[i].at[idx]

        local = [pltpu.make_async_copy(src(i, me), outs[i].at[me], local_sems.at[i]) for i in range(n)]
        for cp in local:
            cp.start()
        remote = []
        for m in range(1, N_DEV):
            px, py, pc = x ^ ((m >> 2) & 1), y ^ ((m >> 1) & 1), c ^ (m & 1)
            peer = 4 * px + 2 * py + pc
            for i in range(n):
                cp = pltpu.make_async_remote_copy(
                    src_ref=src(i, peer), dst_ref=outs[i].at[me],
                    send_sem=send_sems.at[i, m - 1], recv_sem=recv_sems.at[i, m - 1],
                    device_id=(px, py, pc), device_id_type=pl.DeviceIdType.MESH)
                cp.start()
                arrival = pltpu.make_async_remote_copy(
                    src_ref=src(i, peer), dst_ref=outs[i].at[peer],
                    send_sem=send_sems.at[i, m - 1], recv_sem=recv_sems.at[i, m - 1],
                    device_id=(px, py, pc), device_id_type=pl.DeviceIdType.MESH)
                remote.append((cp, arrival))
        for cp, arrival in remote:
            cp.wait_send()
            arrival.wait_recv()
        for cp in local:
            cp.wait()

    hbm = pl.BlockSpec(memory_space=pltpu.HBM)
    outs = pl.pallas_call(
        body, name=name, out_shape=out_shape,
        in_specs=[hbm] * n, out_specs=[hbm] * n,
        scratch_shapes=[pltpu.SemaphoreType.DMA((n, N_DEV - 1)), pltpu.SemaphoreType.DMA((n, N_DEV - 1)),
                        pltpu.SemaphoreType.DMA((n,))],
    )(*arrays)
    return list(outs[:n_g]), list(outs[n_g:])


def _mesh_peers():
    x, y, c = lax.axis_index("x"), lax.axis_index("y"), lax.axis_index("c")
    peers = []
    for m in range(1, N_DEV):
        px, py, pc = x ^ ((m >> 2) & 1), y ^ ((m >> 1) & 1), c ^ (m & 1)
        peers.append((m, (px, py, pc), 4 * px + 2 * py + pc))
    return 4 * x + 2 * y + c, peers


_HBM_SPEC = pl.BlockSpec(memory_space=pltpu.HBM)
_SEM_SPEC = pl.BlockSpec(memory_space=pltpu.SEMAPHORE)
_DATAFLOW = pltpu.SideEffectType.DATAFLOW_SIDE_EFFECTING


def _exchange_start(name, gathers, scatters):
    n_g = len(gathers)
    arrays = list(gathers) + list(scatters)
    n = len(arrays)
    lands = ([lax.empty((N_DEV,) + a.shape, a.dtype) for a in gathers] + [lax.empty(a.shape, a.dtype) for a in scatters])

    def body(*refs):
        ins, land = refs[:n], refs[n:2 * n]
        send_sems, recv_sems, local_sems, token = refs[2 * n], refs[2 * n + 1], refs[2 * n + 2], refs[-1]
        me, peers = _mesh_peers()
        for m, dev, peer in peers:
            for i in range(n):
                k = i * (N_DEV - 1) + m - 1
                pltpu.make_async_remote_copy(
                    src_ref=ins[i] if i < n_g else ins[i].at[peer], dst_ref=land[i].at[me],
                    send_sem=send_sems.at[k], recv_sem=recv_sems.at[k],
                    device_id=dev, device_id_type=pl.DeviceIdType.MESH).start()
        for i in range(n):
            pltpu.make_async_copy(ins[i] if i < n_g else ins[i].at[me], land[i].at[me], local_sems.at[i]).start()
        token[...] = jnp.zeros_like(token)

    sem = pltpu.SemaphoreType.DMA((n * (N_DEV - 1),))
    outs = pl.pallas_call(
        body, name=name,
        out_shape=([sem, sem, pltpu.SemaphoreType.DMA((n,))] + [pltpu.HBM(a.shape, a.dtype) for a in arrays]
                   + [pltpu.HBM(l.shape, l.dtype) for l in lands] + [jax.ShapeDtypeStruct((8, 128), F32)]),
        in_specs=[_HBM_SPEC] * (2 * n),
        out_specs=[_SEM_SPEC] * 3 + [_HBM_SPEC] * (2 * n) + [pl.BlockSpec(memory_space=pltpu.VMEM)],
        input_output_aliases={i: i + 3 for i in range(2 * n)},
        compiler_params=pltpu.CompilerParams(has_side_effects=_DATAFLOW),
    )(*[pltpu.with_memory_space_constraint(a, pltpu.HBM) for a in arrays],
      *[pltpu.with_memory_space_constraint(l, pltpu.HBM) for l in lands])
    return dict(n_g=n_g, n=n, send=outs[0], recv=outs[1], local=outs[2], srcs=list(outs[3:3 + n]),
                lands=list(outs[3 + n:3 + 2 * n]), token=outs[-1])


def _exchange_wait(name, started, after):
    n, n_g = started["n"], started["n_g"]

    def body(*refs):
        srcs, lands = refs[:n], refs[n:2 * n]
        send_sems, recv_sems, local_sems = refs[2 * n], refs[2 * n + 1], refs[2 * n + 2]
        me, peers = _mesh_peers()
        local = [pltpu.make_async_copy(srcs[i] if i < n_g else srcs[i].at[me], lands[i].at[me], local_sems.at[i])
                 for i in range(n)]
        for m, dev, peer in peers:
            for i in range(n):
                k = i * (N_DEV - 1) + m - 1
                cp = pltpu.make_async_remote_copy(
                    src_ref=srcs[i] if i < n_g else srcs[i].at[peer], dst_ref=lands[i].at[peer],
                    send_sem=send_sems.at[k], recv_sem=recv_sems.at[k],
                    device_id=dev, device_id_type=pl.DeviceIdType.MESH)
                cp.wait_send()
                cp.wait_recv()
        for cp in local:
            cp.wait()

    outs = pl.pallas_call(
        body, name=name,
        out_shape=([pltpu.HBM(a.shape, a.dtype) for a in started["srcs"]]
                   + [pltpu.HBM(l.shape, l.dtype) for l in started["lands"]]),
        in_specs=[_HBM_SPEC] * (2 * n) + [_SEM_SPEC] * 3 + [pl.BlockSpec(memory_space=pl.ANY)],
        out_specs=[_HBM_SPEC] * (2 * n), input_output_aliases={i: i for i in range(2 * n)},
        compiler_params=pltpu.CompilerParams(has_side_effects=_DATAFLOW),
    )(*started["srcs"], *started["lands"], started["send"], started["recv"], started["local"], after)
    return list(outs[n:n + n_g]), list(outs[n + n_g:])


def _tile_spec(shape, layout, t):
    if layout == "nat":
        return pl.BlockSpec((t, shape[1]), lambda i: (i, 0))
    return pl.BlockSpec((shape[0], t, shape[2]), lambda i: (0, i, 0))


def _full_spec(shape):
    nd = len(shape)
    return pl.BlockSpec(tuple(shape), lambda i: (0,) * nd)


def _seq_len(a, layout):
    return a.shape[0] if layout == "nat" else a.shape[1]


def _tile_fwd(name, f, tiled, params, outs, t):
    nt, npar = len(tiled), len(params)
    s = _seq_len(*tiled[0])

    def body(*refs):
        vals = [r[...] for r in refs[:nt + npar]]
        res = f(*vals)
        for r, o in zip(refs[nt + npar:], res):
            r[...] = o.astype(r.dtype)

    return pl.pallas_call(
        body, name=name, grid=(s // t,),
        in_specs=[_tile_spec(a.shape, l, t) for a, l in tiled] + [_full_spec(p.shape) for p in params],
        out_specs=[_tile_spec(sh, l, t) for sh, _, l in outs],
        out_shape=[jax.ShapeDtypeStruct(sh, dt) for sh, dt, _ in outs],
        compiler_params=_params(("arbitrary",)),
    )(*[a for a, _ in tiled], *params)


def _tile_bwd(name, f, tiled, params, cts, t, need, adds=None):
    nt, npar, nc = len(tiled), len(params), len(cts)
    s = _seq_len(*tiled[0])
    need_idx = [k for k in range(nt) if need[k]]
    adds = adds or [None] * len(need_idx)
    add_arrays = [(a, tiled[k][1]) for a, k in zip(adds, need_idx) if a is not None]
    n_add = len(add_arrays)

    def body(*refs):
        i = pl.program_id(0)
        vals = [r[...] for r in refs[:nt + npar]]
        ct_refs = refs[nt + npar:nt + npar + nc]
        add_refs = refs[nt + npar + nc:nt + npar + nc + n_add]
        out_refs = refs[nt + npar + nc + n_add:]
        res, vjp = jax.vjp(f, *vals)
        grads = vjp(tuple(r[...].astype(o.dtype) for r, o in zip(ct_refs, res)))
        a = 0
        for j, k in enumerate(need_idx):
            g = grads[k]
            if adds[j] is not None:
                g = g + add_refs[a][...]
                a += 1
            out_refs[j][...] = g.astype(out_refs[j].dtype)
        for j in range(npar):
            r = out_refs[len(need_idx) + j]

            @pl.when(i == 0)
            def _():
                r[...] = jnp.zeros_like(r)

            r[...] += grads[nt + j]

    outs = pl.pallas_call(
        body, name=name, grid=(s // t,),
        in_specs=([_tile_spec(a.shape, l, t) for a, l in tiled] + [_full_spec(p.shape) for p in params]
                  + [_tile_spec(a.shape, l, t) for a, l in cts] + [_tile_spec(a.shape, l, t) for a, l in add_arrays]),
        out_specs=([_tile_spec(tiled[k][0].shape, tiled[k][1], t) for k in need_idx]
                   + [_full_spec(p.shape) for p in params]),
        out_shape=([jax.ShapeDtypeStruct(tiled[k][0].shape, F32) for k in need_idx]
                   + [jax.ShapeDtypeStruct(p.shape, F32) for p in params]),
        compiler_params=_params(("arbitrary",)),
    )(*[a for a, _ in tiled], *params, *[a for a, _ in cts], *[a for a, _ in add_arrays])
    return list(outs[:len(need_idx)]), list(outs[len(need_idx):])


def _linear_fwd(name, x, w, t, out_layout="nat", residual=None):
    s, k = x.shape
    n = w.shape[1]
    has_res = residual is not None

    def body(*refs):
        x_ref, w_ref = refs[0], refs[1]
        o_ref = refs[-1]
        y = _dot(x_ref[...].astype(BF16), w_ref[...])
        if has_res:
            y = y + refs[2][...]
        if out_layout == "hm":
            for h in range(n // HEAD):
                o_ref[h] = y[:, h * HEAD:(h + 1) * HEAD]
        else:
            o_ref[...] = y

    out_sh = (s, n) if out_layout == "nat" else (n // HEAD, s, HEAD)
    ins = [x, w] + ([residual] if has_res else [])
    in_specs = [_tile_spec(x.shape, "nat", t), _full_spec(w.shape)] + ([_tile_spec((s, n), "nat", t)] if has_res else [])
    return pl.pallas_call(
        body, name=name, grid=(s // t,), in_specs=in_specs,
        out_specs=_tile_spec(out_sh, out_layout, t), out_shape=jax.ShapeDtypeStruct(out_sh, F32),
        compiler_params=_params(("arbitrary",)),
    )(*ins)


def _linear_dx(name, dy, w, t, dy_layout="nat"):
    k, n = w.shape
    s = _seq_len(dy, dy_layout)

    def body(dy_ref, w_ref, o_ref):
        dy = _heads_to_nat(dy_ref[...].astype(BF16)) if dy_layout == "hm" else dy_ref[...].astype(BF16)
        o_ref[...] = _dot(dy, w_ref[...], NT2)

    return pl.pallas_call(
        body, name=name, grid=(s // t,),
        in_specs=[_tile_spec(dy.shape, dy_layout, t), _full_spec(w.shape)],
        out_specs=_tile_spec((s, k), "nat", t), out_shape=jax.ShapeDtypeStruct((s, k), F32),
        compiler_params=_params(("arbitrary",)),
    )(dy, w)


def _linear_dw(name, x, dy, t, nb, dy_layout="nat"):
    s, k = x.shape
    n = dy.shape[1] if dy_layout == "nat" else dy.shape[0] * HEAD

    def body(x_ref, dy_ref, o_ref):
        i = pl.program_id(1)

        @pl.when(i == 0)
        def _():
            o_ref[...] = jnp.zeros_like(o_ref)

        dy = _heads_to_nat(dy_ref[...].astype(BF16)) if dy_layout == "hm" else dy_ref[...].astype(BF16)
        o_ref[...] += _dot(x_ref[...].astype(BF16), dy, TN2)

    if dy_layout == "hm":
        dy_spec = pl.BlockSpec((nb // HEAD, t, HEAD), lambda j, i: (j, i, 0))
    else:
        dy_spec = pl.BlockSpec((t, nb), lambda j, i: (i, j))
    return pl.pallas_call(
        body, name=name, grid=(n // nb, s // t),
        in_specs=[pl.BlockSpec((t, k), lambda j, i: (i, 0)), dy_spec],
        out_specs=pl.BlockSpec((k, nb), lambda j, i: (0, j)), out_shape=jax.ShapeDtypeStruct((k, n), F32),
        compiler_params=_params(("arbitrary", "arbitrary")),
    )(x, dy)


def _ffn_fwd(name, x, gn, wg, wu, wd, t):
    s, d = x.shape
    f8 = wg.shape[-1]

    def body(x_ref, g_ref, wg_ref, wu_ref, wd_ref, o_ref, gk_ref, uk_ref, h_scr, acc):
        k = pl.program_id(1)

        @pl.when(k == 0)
        def _():
            h_scr[...] = _rms(x_ref[...], g_ref[...]).astype(BF16)
            acc[...] = jnp.zeros_like(acc)

        hb = h_scr[...]
        gk = _dot(hb, wg_ref[0])
        uk = _dot(hb, wu_ref[0])
        gk_ref[0] = gk
        uk_ref[0] = uk
        a = gk * jax.nn.sigmoid(gk) * uk
        acc[...] += _dot(a.astype(BF16), wd_ref[0])

        @pl.when(k == N_DEV - 1)
        def _():
            o_ref[...] = x_ref[...] + 0.5 * acc[...]

    wspec = lambda shp: pl.BlockSpec((1,) + shp, lambda i, k: (k, 0, 0))
    act = pl.BlockSpec((1, t, f8), lambda i, k: (k, i, 0))
    act_sh = jax.ShapeDtypeStruct((N_DEV, s, f8), F32)
    return pl.pallas_call(
        body, name=name, grid=(s // t, N_DEV),
        in_specs=[pl.BlockSpec((t, d), lambda i, k: (i, 0)), pl.BlockSpec((1, d), lambda i, k: (0, 0)),
                  wspec((d, f8)), wspec((d, f8)), wspec((f8, d))],
        out_specs=[pl.BlockSpec((t, d), lambda i, k: (i, 0)), act, act],
        out_shape=[jax.ShapeDtypeStruct((s, d), F32), act_sh, act_sh],
        scratch_shapes=[pltpu.VMEM((t, d), BF16), pltpu.VMEM((t, d), F32)],
        compiler_params=_params(("arbitrary", "arbitrary")),
    )(x, gn, wg, wu, wd)


def _ffn_bwd(name, x, dy, gn, wg, wu, wd, gact, uact, t):
    s, d = x.shape
    f8 = wg.shape[-1]
    last = N_DEV - 1

    def body(x_ref, dy_ref, g_ref, wg_ref, wu_ref, wd_ref, gk_ref, uk_ref,
             dx_ref, dg_ref, dwg_ref, dwu_ref, dwd_ref, dh_scr):
        k, i = pl.program_id(0), pl.program_id(1)
        x = x_ref[...]
        rs = lax.rsqrt(jnp.mean(x * x, axis=-1, keepdims=True) + NORM_EPS)
        xn = x * rs
        hb = (xn * g_ref[...]).astype(BF16)
        dob = (0.5 * dy_ref[...]).astype(BF16)
        wgk, wuk, wdk = wg_ref[0], wu_ref[0], wd_ref[0]
        gk, uk = gk_ref[0], uk_ref[0]
        sg = jax.nn.sigmoid(gk)
        sk = gk * sg
        da = _dot(dob, wdk, NT2)
        du = (da * sk).astype(BF16)
        dg = (da * uk * (sg * (1.0 + gk * (1.0 - sg)))).astype(BF16)

        dwd_c = _dot((sk * uk).astype(BF16), dob, TN2)
        dwg_c = _dot(hb, dg, TN2)
        dwu_c = _dot(hb, du, TN2)
        dh = _dot(dg, wgk, NT2) + _dot(du, wuk, NT2)
        rows = pl.ds(pl.multiple_of(i * t, t), t)

        @pl.when(i == 0)
        def _():
            dwg_ref[0], dwu_ref[0], dwd_ref[0] = dwg_c, dwu_c, dwd_c

        @pl.when(i > 0)
        def _():
            dwg_ref[0] += dwg_c
            dwu_ref[0] += dwu_c
            dwd_ref[0] += dwd_c

        @pl.when(k == 0)
        def _():
            dh_scr[rows, :] = dh

        @pl.when(k > 0)
        def _():
            dh_scr[rows, :] += dh

        @pl.when(jnp.logical_and(k == last, i == 0))
        def _():
            dg_ref[...] = jnp.zeros_like(dg_ref)

        @pl.when(k == last)
        def _():
            dht = dh_scr[rows, :]
            dg_ref[...] += jnp.sum(dht * xn, axis=0, keepdims=True)
            dxn = dht * g_ref[...]
            dx_ref[...] = dy_ref[...] + rs * (dxn - xn * jnp.mean(dxn * xn, axis=-1, keepdims=True))

    wspec = lambda shp: pl.BlockSpec((1,) + shp, lambda k, i: (k, 0, 0))
    tile = pl.BlockSpec((t, d), lambda k, i: (i, 0))
    act = pl.BlockSpec((1, t, f8), lambda k, i: (k, i, 0))
    return pl.pallas_call(
        body, name=name, grid=(N_DEV, s // t),
        in_specs=[tile, tile, pl.BlockSpec((1, d), lambda k, i: (0, 0)), wspec((d, f8)), wspec((d, f8)), wspec((f8, d)),
                  act, act],
        out_specs=[pl.BlockSpec((t, d), lambda k, i: (jnp.where(k == last, i, 0), 0)),
                   pl.BlockSpec((1, d), lambda k, i: (0, 0)),
                   pl.BlockSpec((1, d, f8), lambda k, i: (k, 0, 0)), pl.BlockSpec((1, d, f8), lambda k, i: (k, 0, 0)),
                   pl.BlockSpec((1, f8, d), lambda k, i: (k, 0, 0))],
        out_shape=[jax.ShapeDtypeStruct((s, d), F32), jax.ShapeDtypeStruct((1, d), F32),
                   jax.ShapeDtypeStruct((N_DEV, d, f8), F32), jax.ShapeDtypeStruct((N_DEV, d, f8), F32),
                   jax.ShapeDtypeStruct((N_DEV, f8, d), F32)],
        scratch_shapes=[pltpu.VMEM((s, d), F32)],
        compiler_params=_params(("arbitrary", "arbitrary")),
    )(x, dy, gn, wg, wu, wd, gact, uact)


def _loss_head(y, target, t):
    s, d = y.shape

    def body(y_ref, t_ref, dy_ref, l_ref):
        i = pl.program_id(0)
        err = y_ref[...] - t_ref[...]
        dy_ref[...] = err * (1.0 / d)

        @pl.when(i == 0)
        def _():
            l_ref[...] = jnp.zeros_like(l_ref)

        l_ref[...] += 0.5 * jnp.sum(jnp.mean(err * err, axis=-1, keepdims=True), axis=0, keepdims=True)

    tile = pl.BlockSpec((t, d), lambda i: (i, 0))
    dy, l = pl.pallas_call(
        body, name="loss_head", grid=(s // t,), in_specs=[tile, tile],
        out_specs=[tile, pl.BlockSpec((1, 1), lambda i: (0, 0))],
        out_shape=[jax.ShapeDtypeStruct((s, d), F32), jax.ShapeDtypeStruct((1, 1), F32)],
        compiler_params=_params(("arbitrary",)),
    )(y, target)
    return dy, l[0, 0]


SB_KEY_TILE = 1024
SB_HEADS_PER_STEP = 4


def _sb_scan_mats():
    row = lax.broadcasted_iota(jnp.int32, (QBLK, QBLK), 0)
    col = lax.broadcasted_iota(jnp.int32, (QBLK, QBLK), 1)
    return (row > col).astype(F32).astype(BF16), (row < col).astype(F32).astype(BF16)


def _sb_tile_scan(x, mat, reverse):
    nsub = x.shape[1] // QBLK
    outs, carry = [None] * nsub, jnp.zeros((x.shape[0], 1), F32)
    for i in (reversed(range(nsub)) if reverse else range(nsub)):
        xs = x[:, i * QBLK:(i + 1) * QBLK]
        hi = xs.astype(BF16)
        lo = (xs - hi.astype(F32)).astype(BF16)
        outs[i] = _dot(hi, mat) + _dot(lo, mat) + carry
        carry = carry + jnp.sum(xs, axis=1, keepdims=True)
    return jnp.concatenate(outs, axis=1), carry


def _sb_before_query(n, t, kt):
    row = lax.broadcasted_iota(jnp.int32, (QBLK, kt), 0)
    col = lax.broadcasted_iota(jnp.int32, (QBLK, kt), 1)
    return t * kt + col < n * QBLK + row


def _sb_fwd(q, k, v):
    _, s, _ = q.shape
    scale = HEAD ** -0.5
    kt = min(SB_KEY_TILE, s)

    def body(q_ref, k_ref, v_ref, o_ref):
        n = pl.program_id(1)
        suffix, _ = _sb_scan_mats()
        n_tiles = lax.div(n, jnp.int32(kt // QBLK)) + 1
        heads = range(SB_HEADS_PER_STEP)
        qb = [(q_ref[h] * scale).astype(q_ref.dtype) for h in heads]

        def tile(t, carry, diagonal):
            rows = pl.ds(pl.multiple_of(t * kt, kt), kt)
            out = []
            for h in heads:
                c, acc = carry[h]
                z = _dot(qb[h], k_ref[h, rows, :], NT2)
                lk = _log_sigmoid(-z)
                log_beta = z + lk
                if diagonal:
                    ok = _sb_before_query(n, t, kt)
                    lk = jnp.where(ok, lk, 0.0)
                later, total = _sb_tile_scan(lk, suffix, True)
                w = jnp.exp(log_beta + later + c)
                if diagonal:
                    w = jnp.where(ok, w, 0.0)
                out.append((c + total, acc + _dot(w.astype(BF16), v_ref[h, rows, :])))
            return tuple(out)

        zero = (jnp.zeros((QBLK, 1), F32), jnp.zeros((QBLK, HEAD), F32))
        carry = tile(n_tiles - 1, (zero,) * len(heads), True)
        carry = lax.fori_loop(1, n_tiles, lambda jj, cr: tile(n_tiles - 1 - jj, cr, False), carry)
        for h in heads:
            o_ref[h] = carry[h][1]

    hp = SB_HEADS_PER_STEP
    return pl.pallas_call(
        body, name="sb_fwd", grid=(SB_HEADS // hp, s // QBLK),
        in_specs=[pl.BlockSpec((hp, QBLK, HEAD), lambda h, n: (h, n, 0)),
                  pl.BlockSpec((hp, s, HEAD), lambda h, n: (h, 0, 0)),
                  pl.BlockSpec((hp, s, HEAD), lambda h, n: (h, 0, 0))],
        out_specs=pl.BlockSpec((hp, QBLK, HEAD), lambda h, n: (h, n, 0)),
        out_shape=jax.ShapeDtypeStruct((SB_HEADS, s, HEAD), F32),
        compiler_params=_params(("arbitrary", "arbitrary")),
    )(q, k, v)


def _sb_bwd(q, k, v, do):
    _, s, _ = q.shape
    scale = HEAD ** -0.5
    kt = min(SB_KEY_TILE, s)

    def body(q_ref, k_ref, v_ref, do_ref, dq_ref, dk_ref, dv_ref, e_scr, beta_scr):
        n = pl.program_id(1)

        @pl.when(n == 0)
        def _():
            dk_ref[...] = jnp.zeros_like(dk_ref)
            dv_ref[...] = jnp.zeros_like(dv_ref)

        suffix, prefix = _sb_scan_mats()
        n_tiles = lax.div(n, jnp.int32(kt // QBLK)) + 1
        heads = range(SB_HEADS_PER_STEP)
        qb = [(q_ref[h] * scale).astype(q_ref.dtype) for h in heads]
        dob = [do_ref[h].astype(BF16) for h in heads]

        def weights(t, cs, diagonal):
            rows = pl.ds(pl.multiple_of(t * kt, kt), kt)
            out, stores = [], []
            for h in heads:
                vb = v_ref[h, rows, :]
                z = _dot(qb[h], k_ref[h, rows, :], NT2)
                lk = _log_sigmoid(-z)
                log_beta = z + lk
                if diagonal:
                    ok = _sb_before_query(n, t, kt)
                    lk = jnp.where(ok, lk, 0.0)
                later, total = _sb_tile_scan(lk, suffix, True)
                w = jnp.exp(log_beta + later + cs[h])
                if diagonal:
                    w = jnp.where(ok, w, 0.0)
                stores.append((w * _dot(dob[h], vb, NT2), jnp.exp(log_beta), _dot(w.astype(BF16), dob[h], TN2)))
                out.append(cs[h] + total)
            for h in heads:
                e_scr[h, t], beta_scr[h, t] = stores[h][0], stores[h][1]
                dv_ref[h, rows, :] += stores[h][2]
            return tuple(out)

        col0 = jnp.zeros((QBLK, 1), F32)
        cs = weights(n_tiles - 1, (col0,) * len(heads), True)
        lax.fori_loop(1, n_tiles, lambda jj, c: weights(n_tiles - 1 - jj, c, False), cs)

        def grads(t, carry, diagonal):
            rows = pl.ds(pl.multiple_of(t * kt, kt), kt)
            out, dks = [], []
            for h in heads:
                pc, dq = carry[h]
                kb = k_ref[h, rows, :]
                e, beta = e_scr[h, t], beta_scr[h, t]
                before, total = _sb_tile_scan(e, prefix, False)
                dz = e * (1.0 - beta) - beta * (before + pc)
                if diagonal:
                    dz = jnp.where(_sb_before_query(n, t, kt), dz, 0.0)
                dz = dz.astype(BF16)
                dks.append(_dot(dz, qb[h], TN2))
                out.append((pc + total, dq + _dot(dz, kb)))
            for h in heads:
                dk_ref[h, rows, :] += dks[h]
            return tuple(out)

        zero = (col0, jnp.zeros((QBLK, HEAD), F32))
        carry = lax.fori_loop(0, n_tiles - 1, lambda t, cr: grads(t, cr, False), (zero,) * len(heads))
        carry = grads(n_tiles - 1, carry, True)
        for h in heads:
            dq_ref[h] = carry[h][1] * scale

    hp = SB_HEADS_PER_STEP
    qspec = pl.BlockSpec((hp, QBLK, HEAD), lambda h, n: (h, n, 0))
    full = pl.BlockSpec((hp, s, HEAD), lambda h, n: (h, 0, 0))
    sh = jax.ShapeDtypeStruct((SB_HEADS, s, HEAD), F32)
    tiles_sh = (hp, s // kt, QBLK, kt)
    return pl.pallas_call(
        body, name="sb_bwd", grid=(SB_HEADS // hp, s // QBLK),
        in_specs=[qspec, full, full, qspec],
        out_specs=[qspec, full, full], out_shape=[sh, sh, sh],
        scratch_shapes=[pltpu.VMEM(tiles_sh, F32), pltpu.VMEM(tiles_sh, F32)],
        compiler_params=_params(("arbitrary", "arbitrary")),
    )(q, k, v, do)


def _t5_bucket_np(dist):
    max_exact = N_BUCKETS // 2
    d = np.maximum(dist, 1).astype(np.float32)
    large = max_exact + (np.log(d / np.float32(max_exact)) / np.float32(math.log(MAX_DISTANCE / max_exact))
                         * np.float32(N_BUCKETS - max_exact)).astype(np.int32)
    large = np.minimum(large, N_BUCKETS - 1)
    return np.where(dist < max_exact, dist, large)


def _dil_layout(s):
    assert all(s % (QBLK * r) == 0 and window // r == QBLK for window, r in DIL_PATTERNS)
    tiles, buckets = [], []
    i = np.arange(QBLK)[:, None]
    j = np.arange(QBLK)[None, :]
    for g, (window, r) in enumerate(DIL_PATTERNS):
        for off in (0, 1):
            dist = QBLK * off + i - j
            ok = (dist >= 0) & (dist <= window // r)
            tiles.append((g, off))
            buckets.append(np.where(ok, _t5_bucket_np(np.maximum(dist, 0) * r), -1).astype(np.int32))
    return tiles, np.stack(buckets)


def _bias_tiles(rel_bias, s):
    tiles, buckets = _dil_layout(s)
    nt = len(tiles)
    present = [sorted(set(np.unique(buckets[k]).tolist()) - {-1}) for k in range(nt)]

    def body(rel_ref, b_ref, o_ref):
        j = pl.program_id(0)
        for k, (g, _) in enumerate(tiles):
            bk = b_ref[k]
            tile = jnp.full((QBLK, QBLK), NEG_INF, F32)
            for b in present[k]:
                tile = jnp.where(bk == b, rel_ref[b, g * DIL_GROUP + j], tile)
            o_ref[0, k] = tile

    return pl.pallas_call(
        body, name="bias_tiles", grid=(DIL_GROUP,),
        in_specs=[pl.BlockSpec(memory_space=pltpu.SMEM), pl.BlockSpec((nt, QBLK, QBLK), lambda j: (0, 0, 0))],
        out_specs=pl.BlockSpec((1, nt, QBLK, QBLK), lambda j: (j, 0, 0, 0)),
        out_shape=jax.ShapeDtypeStruct((DIL_GROUP, nt, QBLK, QBLK), F32),
        compiler_params=_params(("arbitrary",)),
    )(rel_bias, jnp.asarray(buckets))


def _bias_tiles_bwd(dbias, s):
    tiles, buckets = _dil_layout(s)
    nt = len(tiles)
    present = [sorted(set(np.unique(buckets[k]).tolist()) - {-1}) for k in range(nt)]

    def body(d_ref, b_ref, o_ref):
        j = pl.program_id(0)

        @pl.when(j == 0)
        def _():
            for b in range(N_BUCKETS):
                for col in range(3 * DIL_GROUP):
                    o_ref[b, col] = jnp.float32(0.0)

        for k, (g, _) in enumerate(tiles):
            bk, dk = b_ref[k], d_ref[0, k]
            for b in present[k]:
                o_ref[b, g * DIL_GROUP + j] += jnp.sum(jnp.where(bk == b, dk, 0.0))

    return pl.pallas_call(
        body, name="bias_tiles_bwd", grid=(DIL_GROUP,),
        in_specs=[pl.BlockSpec((1, nt, QBLK, QBLK), lambda j: (j, 0, 0, 0)),
                  pl.BlockSpec((nt, QBLK, QBLK), lambda j: (0, 0, 0))],
        out_specs=pl.BlockSpec(memory_space=pltpu.SMEM),
        out_shape=jax.ShapeDtypeStruct((N_BUCKETS, 3 * DIL_GROUP), F32),
        compiler_params=_params(("arbitrary",)),
    )(dbias, jnp.asarray(buckets))


DIL_PAIRS_PER_STEP = 8


def _dil_rows(g, s, pair):
    _, r = DIL_PATTERNS[g]
    nb = s // (QBLK * r)
    c, n = lax.div(pair, jnp.int32(nb)), lax.rem(pair, jnp.int32(nb))
    start = c + (r * QBLK) * n
    before = jnp.where(n > 0, start - r * QBLK, start)
    if r == 1:
        return pl.ds(start, QBLK), pl.ds(before, QBLK), n > 0
    return pl.ds(start, QBLK, stride=r), pl.ds(before, QBLK, stride=r), n > 0


def _dil_logits(qb, k_ref, rows, before, has_before, b_ref):
    k0, k1 = k_ref[0, rows, :].astype(BF16), k_ref[0, before, :].astype(BF16)
    l0 = _dot(qb, k0, NT2) + b_ref[0, 0]
    l1 = jnp.where(has_before, _dot(qb, k1, NT2) + b_ref[0, 1], NEG_INF)
    return k0, k1, l0, l1


def _dil_group_specs(g, s):
    head = pl.BlockSpec((1, s, HEAD), lambda j, p: (DIL_GROUP * g + j, 0, 0))
    return [head, head, head, pl.BlockSpec((1, 2, QBLK, QBLK), lambda j, p: (j, g, 0, 0))]


def _dil_group_fwd(g, qn, kn, v, bias):
    _, s, _ = qn.shape
    scale = HEAD ** -0.5
    steps = (s // QBLK) // DIL_PAIRS_PER_STEP

    def body(q_ref, k_ref, v_ref, b_ref, o_ref):
        pairs = [_dil_rows(g, s, pl.program_id(1) * DIL_PAIRS_PER_STEP + u) for u in range(DIL_PAIRS_PER_STEP)]
        loaded = []
        for rows, before, has_before in pairs:
            qb = (q_ref[0, rows, :] * scale).astype(BF16)
            _, _, l0, l1 = _dil_logits(qb, k_ref, rows, before, has_before, b_ref)
            loaded.append((l0, l1, v_ref[0, rows, :].astype(BF16), v_ref[0, before, :].astype(BF16)))
        results = []
        for l0, l1, v0, v1 in loaded:
            m = jnp.max(jnp.maximum(l0, l1), axis=1, keepdims=True)
            p0, p1 = jnp.exp(l0 - m), jnp.exp(l1 - m)
            den = jnp.sum(p0 + p1, axis=1, keepdims=True)
            inv = 1.0 / den
            o = _dot((p0 * inv).astype(BF16), v0) + _dot((p1 * inv).astype(BF16), v1)
            results.append(jnp.concatenate([o, jnp.broadcast_to(m + jnp.log(den), (QBLK, HEAD))], axis=1))
        for (rows, _, _), res in zip(pairs, results):
            o_ref[0, rows, :] = res

    return pl.pallas_call(
        body, name="dil%d_fwd" % g, grid=(DIL_GROUP, steps), in_specs=_dil_group_specs(g, s),
        out_specs=pl.BlockSpec((1, s, 2 * HEAD), lambda j, p: (j, 0, 0)),
        out_shape=jax.ShapeDtypeStruct((DIL_GROUP, s, 2 * HEAD), F32),
        compiler_params=_params(("arbitrary", "arbitrary")),
    )(qn, kn, v, bias)


def _dil_group_bwd(g, qn, kn, v, bias, ol, dol, prev):
    _, s, _ = qn.shape
    scale = HEAD ** -0.5
    steps = (s // QBLK) // DIL_PAIRS_PER_STEP
    prev = list(prev) if prev is not None else []

    def body(q_ref, k_ref, v_ref, b_ref, ol_ref, dol_ref, *rest):
        dq_ref, dk_ref, dv_ref, db_ref = rest[-4:]

        @pl.when(pl.program_id(1) == 0)
        def _():
            for r in (dk_ref, dv_ref, db_ref):
                r[...] = jnp.zeros_like(r)

        pairs = [_dil_rows(g, s, pl.program_id(1) * DIL_PAIRS_PER_STEP + u) for u in range(DIL_PAIRS_PER_STEP)]
        loaded = []
        for rows, before, has_before in pairs:
            qb = (q_ref[0, rows, :] * scale).astype(BF16)
            k0, k1, l0, l1 = _dil_logits(qb, k_ref, rows, before, has_before, b_ref)
            v0, v1 = v_ref[0, rows, :].astype(BF16), v_ref[0, before, :].astype(BF16)
            loaded.append((qb, k0, k1, l0, l1, v0, v1, ol_ref[0, rows, :], dol_ref[0, rows, :]))
        grads = []
        for qb, k0, k1, l0, l1, v0, v1, out_lse, d_out_lse in loaded:
            o, lse = out_lse[:, :HEAD], out_lse[:, HEAD:HEAD + 1]
            do, dlse = d_out_lse[:, :HEAD], d_out_lse[:, HEAD:HEAD + 1]
            dob = do.astype(BF16)
            p0, p1 = jnp.exp(l0 - lse), jnp.exp(l1 - lse)
            shift = dlse - jnp.sum(do * o, axis=1, keepdims=True)
            dl0 = p0 * (_dot(dob, v0, NT2) + shift)
            dl1 = p1 * (_dot(dob, v1, NT2) + shift)
            dl0b, dl1b = dl0.astype(BF16), dl1.astype(BF16)
            grads.append(((_dot(dl0b, k0) + _dot(dl1b, k1)) * scale,
                          _dot(dl0b, qb, TN2), _dot(dl1b, qb, TN2),
                          _dot(p0.astype(BF16), dob, TN2), _dot(p1.astype(BF16), dob, TN2), dl0, dl1))
        db0 = functools.reduce(jnp.add, [gr[5] for gr in grads])
        db1 = functools.reduce(jnp.add, [gr[6] for gr in grads])
        for (rows, before, _), (dq, dk0, dk1, dv0, dv1, _, _) in zip(pairs, grads):
            dq_ref[0, rows, :] = dq
            dk_ref[0, rows, :] += dk0
            dk_ref[0, before, :] += dk1
            dv_ref[0, rows, :] += dv0
            dv_ref[0, before, :] += dv1
        db_ref[0, 0] += db0
        db_ref[0, 1] += db1

    head_out = pl.BlockSpec((1, s, HEAD), lambda j, p: (DIL_GROUP * g + j, 0, 0))
    rows128 = pl.BlockSpec((1, s, 2 * HEAD), lambda j, p: (j, 0, 0))
    full_sh = jax.ShapeDtypeStruct(qn.shape, F32)
    return pl.pallas_call(
        body, name="dil%d_bwd" % g, grid=(DIL_GROUP, steps),
        in_specs=_dil_group_specs(g, s) + [rows128, rows128] + [pl.BlockSpec(memory_space=pl.ANY)] * len(prev),
        out_specs=[head_out, head_out, head_out, pl.BlockSpec((1, 2, QBLK, QBLK), lambda j, p: (j, 0, 0, 0))],
        out_shape=[full_sh, full_sh, full_sh, jax.ShapeDtypeStruct((DIL_GROUP, 2, QBLK, QBLK), F32)],
        input_output_aliases={6 + i: i for i in range(len(prev))},
        compiler_params=_params(("arbitrary", "arbitrary")),
    )(qn, kn, v, bias, ol, dol, *prev)


@functools.partial(jax.custom_vjp, nondiff_argnums=(2,))
def _bdot(a, b, dims):
    return _dot(a.astype(BF16), b.astype(BF16), dims)


def _bdot_fwd(a, b, dims):
    return _bdot(a, b, dims), (a, b)


def _bdot_bwd(dims, res, dc):
    a, b = res
    nn, nt, tn = (NN2, NT2, TN2) if dims in (NN2, NT2, TN2) else (NN3, NT3, TN3)
    if dims == nn:
        return _bdot(dc, b, nt), _bdot(a, dc, tn)
    if dims == nt:
        return _bdot(dc, b, nn), _bdot(dc, a, tn)
    return _bdot(b, dc, nt), _bdot(a, dc, nn)


_bdot.defvjp(_bdot_fwd, _bdot_bwd)


def _ones_dot(ones, x, dims):
    o = ones.astype(BF16)
    hi = x.astype(BF16)
    r1 = x - hi.astype(F32)
    mid = r1.astype(BF16)
    lo = (r1 - mid.astype(F32)).astype(BF16)
    return _dot(o, hi, dims) + _dot(o, mid, dims) + _dot(o, lo, dims)


@jax.custom_vjp
def _prefix_sums(x):
    c = x.shape[1]
    row = lax.broadcasted_iota(jnp.int32, (x.shape[0], c, c), 1)
    col = lax.broadcasted_iota(jnp.int32, (x.shape[0], c, c), 2)
    return _ones_dot((row >= col).astype(F32), x, NN3)


def _prefix_sums_fwd(x):
    return _prefix_sums(x), None


def _prefix_sums_bwd(_, dy):
    c = dy.shape[1]
    row = lax.broadcasted_iota(jnp.int32, (dy.shape[0], c, c), 1)
    col = lax.broadcasted_iota(jnp.int32, (dy.shape[0], c, c), 2)
    return (_ones_dot((row <= col).astype(F32), dy, NN3),)


_prefix_sums.defvjp(_prefix_sums_fwd, _prefix_sums_bwd)


def _rwkv_chunk(s0, r, lw, kraw, v, ag, kk_w, ka_w, rk_w, lng, lnb):
    hb, c, _ = r.shape
    kk = kraw * kk_w
    kk = kk / jnp.maximum(jnp.sqrt(jnp.sum(kk * kk, axis=-1, keepdims=True)), 1e-12)
    k = kraw * (1.0 + (ag - 1.0) * ka_w)
    a = -kk
    b = kk * ag
    row = lax.broadcasted_iota(jnp.int32, (hb, c, c), 1)
    col = lax.broadcasted_iota(jnp.int32, (hb, c, c), 2)
    lower, strict = row >= col, row > col
    cum = _prefix_sums(lw)
    ecum, einv = jnp.exp(cum), jnp.exp(-cum)
    rt, kt, bt = r * ecum, k * einv, b * einv
    at = a * jnp.exp(cum - lw)
    ar = jnp.concatenate([at, rt], axis=1)
    scores = _bdot(ar, jnp.concatenate([bt, kt], axis=1), NT3)
    a_ab = jnp.where(strict, scores[:, :c, :c], 0.0)
    a_ak = jnp.where(strict, scores[:, :c, c:], 0.0)
    p_rb = jnp.where(lower, scores[:, c:, :c], 0.0)
    p_rk = jnp.where(lower, scores[:, c:, c:], 0.0)
    from_s0 = _bdot(ar, s0, NT3)
    rhs = from_s0[:, :c] + _bdot(a_ak, v, NN3)
    inv = (row == col).astype(F32) + a_ab
    pw = a_ab
    for _ in range(int(math.log2(c)) - 1):
        pw = _bdot(pw, pw, NN3)
        inv = inv + _bdot(inv, pw, NN3)
    u = _bdot(inv, rhs, NN3)
    uv = jnp.concatenate([u, v], axis=1)
    y = from_s0[:, c:] + _bdot(jnp.concatenate([p_rb, p_rk], axis=2), uv, NN3)
    cum_end = cum[:, c - 1:c, :]
    dec = jnp.exp(cum_end - cum)
    s_end = s0 * jnp.exp(cum_end) + _bdot(uv, jnp.concatenate([b * dec, k * dec], axis=1), TN3)
    mu = jnp.mean(y, axis=-1, keepdims=True)
    var = jnp.mean(jnp.square(y - mu), axis=-1, keepdims=True)
    z = (y - mu) * lax.rsqrt(var + GN_EPS) * lng + lnb + jnp.sum(r * k * rk_w, axis=-1, keepdims=True) * v
    return z, s_end


def _rwkv_specs(nc, rev):
    cidx = (lambda c: nc - 1 - c) if rev else (lambda c: c)
    seq = pl.BlockSpec((RW_HB, RW_CHUNK, HEAD), lambda hg, c: (hg, cidx(c), 0))
    par = pl.BlockSpec((RW_HB, 1, HEAD), lambda hg, c: (hg, 0, 0))
    st = pl.BlockSpec((1, RW_HB, HEAD, HEAD), lambda hg, c: (cidx(c), hg, 0, 0))
    return seq, par, st


def _rwkv_fwd(seqs, pars):
    s = seqs[0].shape[1]
    nc = s // RW_CHUNK

    def body(*refs):
        seq_refs, par_refs = refs[:5], refs[5:10]
        z_ref, st_ref, state = refs[10:]
        c = pl.program_id(1)

        @pl.when(c == 0)
        def _():
            state[...] = jnp.zeros_like(state)

        s0 = state[...]
        st_ref[0] = s0
        z, s_end = _rwkv_chunk(s0, *[r[...] for r in seq_refs], *[r[...] for r in par_refs])
        z_ref[...] = z
        state[...] = s_end

    seq, par, st = _rwkv_specs(nc, False)
    return pl.pallas_call(
        body, name="rwkv_fwd", grid=(N_HEADS // RW_HB, nc),
        in_specs=[seq] * 5 + [par] * 5, out_specs=[seq, st],
        out_shape=[jax.ShapeDtypeStruct((N_HEADS, s, HEAD), F32), jax.ShapeDtypeStruct((nc, N_HEADS, HEAD, HEAD), F32)],
        scratch_shapes=[pltpu.VMEM((RW_HB, HEAD, HEAD), F32)],
        compiler_params=_params(("arbitrary", "arbitrary")),
    )(*seqs, *pars)


def _rwkv_bwd(seqs, pars, states, dz):
    s = seqs[0].shape[1]
    nc = s // RW_CHUNK

    def body(*refs):
        seq_refs, par_refs = refs[:5], refs[5:10]
        st_ref, dz_ref = refs[10:12]
        dseq_refs, dpar_refs, dstate = refs[12:17], refs[17:22], refs[22]
        c = pl.program_id(1)

        @pl.when(c == 0)
        def _():
            dstate[...] = jnp.zeros_like(dstate)
            for r in dpar_refs:
                r[...] = jnp.zeros_like(r)

        _, vjp = jax.vjp(_rwkv_chunk, st_ref[0], *[r[...] for r in seq_refs], *[r[...] for r in par_refs])
        g = vjp((dz_ref[...], dstate[...]))
        dstate[...] = g[0]
        for r, gs in zip(dseq_refs, g[1:6]):
            r[...] = gs
        for r, gp in zip(dpar_refs, g[6:]):
            r[...] += gp

    seq, par, st = _rwkv_specs(nc, True)
    seq_sh = jax.ShapeDtypeStruct((N_HEADS, s, HEAD), F32)
    par_sh = jax.ShapeDtypeStruct((N_HEADS, 1, HEAD), F32)
    outs = pl.pallas_call(
        body, name="rwkv_bwd", grid=(N_HEADS // RW_HB, nc),
        in_specs=[seq] * 5 + [par] * 5 + [st, seq],
        out_specs=[seq] * 5 + [par] * 5, out_shape=[seq_sh] * 5 + [par_sh] * 5,
        scratch_shapes=[pltpu.VMEM((RW_HB, HEAD, HEAD), F32)],
        compiler_params=_params(("arbitrary", "arbitrary")),
    )(*seqs, *pars, states, dz)
    return list(outs[:5]), list(outs[5:])


def _norm_fn(x, g):
    return (_rms(x, g),)


def _attn_prep_fn(proj, qn_w, kn_w):
    a, b = SB_HEADS, 3 * DIL_GROUP
    return (proj[0:a], proj[a:2 * a], proj[2 * a:3 * a],
            _rms(proj[3 * a:3 * a + b], qn_w), _rms(proj[3 * a + b:3 * a + 2 * b], kn_w), proj[3 * a + 2 * b:])


def _attn_merge_fn(o_sb, ol0, ol1, ol2):
    groups = (ol0, ol1, ol2)
    merged = []
    for j in range(DIL_GROUP):
        lses = [ol[j][:, HEAD:HEAD + 1] for ol in groups]
        m = functools.reduce(jnp.maximum, lses)
        es = [jnp.exp(l - m) for l in lses]
        inv = 1.0 / functools.reduce(jnp.add, es)
        merged.append(functools.reduce(jnp.add, [(e * inv) * ol[j][:, :HEAD] for e, ol in zip(es, groups)]))
    return (jnp.concatenate([_heads_to_nat(o_sb)] + merged, axis=-1),)


def _rw_mix_fn(x, xp, gn, mix, w0, w1, w2, a0, a1, a2, g1, g2):
    h = _rms(x, gn)
    xx = _rms(xp, gn) - h
    xr, xw, xk, xv, xa, xg = [h + xx * mix[i:i + 1] for i in range(6)]
    w_log = -jax.nn.softplus(-(w0 + _mm(jnp.tanh(_mm(xw, w1)), w2))) - 0.5
    lw = -jnp.exp(w_log)
    ag = jax.nn.sigmoid(a0 + _mm(_mm(xa, a1), a2))
    gate = _mm(jax.nn.sigmoid(_mm(xg, g1)), g2)
    return xr, xk, xv, _nat_to_heads(lw), _nat_to_heads(ag), gate


def _rw_gate_fn(z, gate):
    return (_heads_to_nat(z) * gate,)


def _adamw_update(w_ref, m_ref, v_ref, g_ref, go_ref, d_ref, mo_ref, vo_ref):
    g = g_ref[0].astype(F32)
    for j in range(1, N_DEV):
        g = g + g_ref[j].astype(F32)
    mn = ADAM_B1 * m_ref[...] + (1.0 - ADAM_B1) * g
    vn = ADAM_B2 * v_ref[...] + (1.0 - ADAM_B2) * jnp.square(g)
    m_hat = mn / (1.0 - ADAM_B1 ** ADAM_STEP)
    v_hat = vn / (1.0 - ADAM_B2 ** ADAM_STEP)
    go_ref[...] = g
    d_ref[...] = -ADAM_LR * (m_hat / (jnp.sqrt(v_hat) + ADAM_EPS) + ADAM_WD * w_ref[...])
    mo_ref[...] = mn
    vo_ref[...] = vn


def _adamw_many(name, items):
    n = len(items)

    def body(*refs):
        for i in range(n):
            _adamw_update(*refs[4 * i:4 * i + 4], *refs[4 * n + 4 * i:4 * n + 4 * i + 4])

    vmem = pl.BlockSpec(memory_space=pltpu.VMEM)
    outs = pl.pallas_call(
        body, name=name, in_specs=[vmem] * (4 * n), out_specs=[vmem] * (4 * n),
        out_shape=[jax.ShapeDtypeStruct(w.shape, F32) for w, _, _, _ in items for _ in range(4)],
        compiler_params=_params(),
    )(*[a for item in items for a in item])
    return [list(outs[4 * i:4 * i + 4]) for i in range(n)]


def _adamw(name, w, m, v, gparts, row0=0, prev=None):
    big_r, c = w.shape
    r = gparts.shape[1]
    tr = r
    if r % 8 == 0:
        tr = max(t for t in range(8, r + 1, 8) if r % t == 0 and (t * c * 4 <= (1 << 20) or t == 8))
    assert row0 % tr == 0 and (r == big_r or r % 8 == 0)
    off = row0 // tr

    def body(w_ref, m_ref, v_ref, g_ref, *rest):
        _adamw_update(w_ref, m_ref, v_ref, g_ref, *rest[-4:])

    tile = pl.BlockSpec((tr, c), lambda i: (i + off, 0))
    sh = jax.ShapeDtypeStruct((big_r, c), F32)
    prev = list(prev) if prev is not None else []
    return pl.pallas_call(
        body, name=name, grid=(r // tr,),
        in_specs=([tile, tile, tile, pl.BlockSpec((N_DEV, tr, c), lambda i: (0, i, 0))]
                  + [pl.BlockSpec(memory_space=pl.ANY)] * len(prev)),
        out_specs=[tile] * 4, out_shape=[sh] * 4,
        input_output_aliases={4 + j: j for j in range(len(prev))},
        compiler_params=_params(("arbitrary",)),
    )(w, m, v, gparts, *prev)


def _col_blocks_to_nat(g):
    return jnp.moveaxis(g, 0, 1).reshape(g.shape[1], -1)


def _nat_to_col_blocks(a):
    return jnp.moveaxis(a.reshape(a.shape[0], N_DEV, -1), 1, 0)


AG_GROUPS = ("f00", "att", "f01", "f10", "rw", "f11")
RS_GROUPS = ("f11", "rw", "f10", "f01", "f00", "att")
BF16_GRAD_GROUPS = ("att", "f00")
RW_SHARDED = ('rw_mix', 'rw_w0', 'rw_w1', 'rw_w2', 'rw_a0', 'rw_a1', 'rw_a2', 'rw_g1', 'rw_g2', 'rw_kk', 'rw_ka',
              'rw_wr', 'rw_wk', 'rw_wv', 'rw_wo', 'rw_lnx_g', 'rw_lnx_b')


def _step(x, target, rep, get, put):
    tied = lambda a, zero: a + zero[0, 0].astype(a.dtype)
    s, d = x.shape
    tf = min(512, s)
    tt = min(256, s)
    row = lambda a: a.reshape(1, -1)
    mix_norm = rep["mix_norm"]
    ffw = {(0, 0): get("f00", None)}
    ffn_norm = _col_blocks_to_nat(ffw[(0, 0)]["ffn_norm"].reshape(N_DEV, 4, -1))

    acts = {}

    def ffn(nm, xin, l, h):
        g = ffw[(l, h)]
        out, *acts[(l, h)] = _ffn_fwd(nm, xin, ffn_norm[2 * l + h][None], g["gate"], g["up"], g["down"], min(2 * tf, s))
        return out

    x1 = ffn("ffn00_fwd", x, 0, 0)
    att = get("att", x1)
    w_in = _col_blocks_to_nat(att["attn_w_in"])
    w_out = _col_blocks_to_nat(att["attn_w_out"])
    (h0,) = _tile_fwd("mixnorm0_fwd", _norm_fn, [(x1, "nat")], [mix_norm[0:1]], [((s, d), BF16, "nat")], tt)
    proj = _linear_fwd("attn_in_fwd", h0, w_in, tt, out_layout="hm")
    bias = _bias_tiles(rep["rel_bias"], s)
    prep_pars = [rep["attn_q_norm"], rep["attn_k_norm"]]
    sb_sh, dl_sh = (SB_HEADS, s, HEAD), (3 * DIL_GROUP, s, HEAD)
    sq, sk, sv, qn, kn, vd = _tile_fwd("attn_prep_fwd", _attn_prep_fn, [(proj, "hm")], prep_pars,
                                       [(sb_sh, BF16, "hm")] * 3 + [(dl_sh, F32, "hm")] * 3, tt // 2)
    o_sb = _sb_fwd(sq, sk, sv)
    ols = [_dil_group_fwd(g, qn, kn, vd, bias) for g in range(3)]
    merge_tiled = [(o_sb, "hm")] + [(ol, "hm") for ol in ols]
    (merged,) = _tile_fwd("merge_fwd", _attn_merge_fn, merge_tiled, [], [((s, 512), BF16, "nat")], tt)
    x2 = _linear_fwd("attn_out_fwd", merged, w_out, tt, residual=x1)
    ffw[(0, 1)] = get("f01", x2)
    x3 = ffn("ffn01_fwd", x2, 0, 1)
    ffw[(1, 0)] = get("f10", x3)
    x4 = ffn("ffn10_fwd", x3, 1, 0)
    rw = get("rw", x4)
    rw_mix = _col_blocks_to_nat(rw["rw_mix"])
    rw_w1, rw_a1, rw_g1 = (rw[k].reshape(d, -1) for k in ("rw_w1", "rw_a1", "rw_g1"))
    rw_w2, rw_a2, rw_g2 = (_col_blocks_to_nat(rw[k]) for k in ("rw_w2", "rw_a2", "rw_g2"))
    rw_w0, rw_a0 = row(rw["rw_w0"]), row(rw["rw_a0"])
    head_par = lambda a: a.reshape(N_HEADS, 1, HEAD)
    scan_pars = [head_par(rw["rw_kk"]), head_par(rw["rw_ka"]), head_par(rep["rw_rk"]),
                 head_par(rw["rw_lnx_g"]), head_par(rw["rw_lnx_b"])]
    w_rkv = [rw[k].reshape(d, d) for k in ("rw_wr", "rw_wk", "rw_wv")]
    w_o = rw["rw_wo"].reshape(d, d)
    x4p = jnp.pad(x4, ((1, 0), (0, 0)))[:-1]
    mix_tiled = [(x4, "nat"), (x4p, "nat")]
    mix_pars = [mix_norm[1:2], rw_mix, rw_w0, rw_w1, rw_w2, rw_a0, rw_a1, rw_a2, rw_g1, rw_g2]
    hm_sh = (N_HEADS, s, HEAD)
    xr, xk, xv, lw, ag, gate = _tile_fwd(
        "rw_mix_fwd", _rw_mix_fn, mix_tiled, mix_pars,
        [((s, d), BF16, "nat")] * 3 + [(hm_sh, F32, "hm")] * 2 + [((s, d), F32, "nat")], tt)
    r_h, k_h, v_h = [_linear_fwd("rw_%s_fwd" % nm, xi, wi, tt, out_layout="hm")
                     for nm, xi, wi in zip("rkv", (xr, xk, xv), w_rkv)]
    scan_seqs = [r_h, lw, k_h, v_h, ag]
    z, states = _rwkv_fwd(scan_seqs, scan_pars)
    (zg,) = _tile_fwd("rw_gate_fwd", _rw_gate_fn, [(z, "hm"), (gate, "nat")], [], [((s, d), BF16, "nat")], tt)
    x5 = _linear_fwd("rw_out_fwd", zg, w_o, tt, residual=x4)
    ffw[(1, 1)] = get("f11", x5)
    y = ffn("ffn11_fwd", x5, 1, 1)
    dy, loss = _loss_head(y, target, tf)

    G = {}
    dgn = {}

    def fb(nm, group, xin, dout, l, h, zero=None, extra=None):
        g = ffw[(l, h)]
        gn = ffn_norm[2 * l + h][None]
        dxin, dgn[(l, h)], dwg, dwu, dwd = _ffn_bwd(nm, xin, dout, gn if zero is None else tied(gn, zero),
                                                   g["gate"], g["up"], g["down"], *acts[(l, h)], tf)
        shard = {"gate": dwg, "up": dwu, "down": dwd}
        if extra is not None:
            shard.update(extra())
        return dxin, put(group, {}, shard)

    dx5, zero = fb("ffn11_bwd", "f11", x5, dy, 1, 1)
    dzg = _linear_dx("rw_out_dx", dx5, w_o, tt)
    G["rw_wo"] = _linear_dw("rw_out_dw", zg, dx5, tf, 512)
    (dz, dgate), _ = _tile_bwd("rw_gate_bwd", _rw_gate_fn, [(z, "hm"), (gate, "nat")], [], [(dzg, "nat")], tt, [True, True])
    (dr_h, dlw, dk_h, dv_h, dag), dscan = _rwkv_bwd(scan_seqs, [tied(scan_pars[0], zero)] + scan_pars[1:], states, dz)
    drkv = (dr_h, dk_h, dv_h)
    for k, gpar in zip(("rw_kk", "rw_ka", "rw_rk", "rw_lnx_g", "rw_lnx_b"), dscan):
        G[k] = gpar
    dxs = []
    for j, (nm, xi, wi) in enumerate(zip("rkv", (xr, xk, xv), w_rkv)):
        dxs.append(_linear_dx("rw_%s_dx" % nm, drkv[j], wi, tt, dy_layout="hm"))
        G["rw_w" + nm] = _linear_dw("rw_%s_dw" % nm, xi, drkv[j], tf, 512, dy_layout="hm")
    (dx4a, dx4p), dmix = _tile_bwd(
        "rw_mix_bwd", _rw_mix_fn, mix_tiled, mix_pars,
        [(dxs[0], "nat"), (dxs[1], "nat"), (dxs[2], "nat"), (dlw, "hm"), (dag, "hm"), (dgate, "nat")],
        tt, [True, True], adds=[dx5, None])
    d_mixn1 = dmix[0]
    for k, gpar in zip(("rw_mix", "rw_w0", "rw_w1", "rw_w2", "rw_a0", "rw_a1", "rw_a2", "rw_g1", "rw_g2"), dmix[1:]):
        G[k] = gpar
    dx4 = dx4a + jnp.pad(dx4p[1:], ((0, 1), (0, 0)))
    for k in ("rw_mix", "rw_w2", "rw_a2", "rw_g2"):
        G[k] = _nat_to_col_blocks(G[k])
    for k in ("rw_w1", "rw_a1", "rw_g1", "rw_wr", "rw_wk", "rw_wv", "rw_wo"):
        G[k] = G[k].reshape(N_DEV, d // N_DEV, -1)
    for k in ("rw_w0", "rw_a0", "rw_kk", "rw_ka", "rw_lnx_g", "rw_lnx_b"):
        G[k] = G[k].reshape(N_DEV, 1, d // N_DEV)
    zero = put("rw", {"rw_rk": G["rw_rk"].reshape(N_HEADS, HEAD)}, {k: G[k] for k in RW_SHARDED})
    dx3, zero = fb("ffn10_bwd", "f10", x3, dx4, 1, 0, zero)
    dx2, zero = fb("ffn01_bwd", "f01", x2, dx3, 0, 1, zero)
    dmerged = _linear_dx("attn_out_dx", dx2, tied(w_out, zero), tt)
    (do_sb, *dols), _ = _tile_bwd("merge_bwd", _attn_merge_fn, merge_tiled, [], [(dmerged, "nat")], tt, [True] * 4)
    dq_sb, dk_sb, dv_sb = _sb_bwd(sq, sk, sv, do_sb)
    dil_grads, dbias = None, []
    for g in range(3):
        *dil_grads, db = _dil_group_bwd(g, qn, kn, vd, bias, ols[g], dols[g], dil_grads)
        dbias.append(db)
    dqn, dkn, dvd = dil_grads
    dbias = jnp.concatenate(dbias, axis=1)
    (dproj,), (dqn_w, dkn_w) = _tile_bwd(
        "attn_prep_bwd", _attn_prep_fn, [(proj, "hm")], prep_pars,
        [(dq_sb, "hm"), (dk_sb, "hm"), (dv_sb, "hm"), (dqn, "hm"), (dkn, "hm"), (dvd, "hm")], tt // 2, [True])
    dh0 = _linear_dx("attn_in_dx", dproj, w_in, tt, dy_layout="hm")
    (dx1,), (d_mixn0,) = _tile_bwd("mixnorm0_bwd", _norm_fn, [(x1, "nat")], [mix_norm[0:1]], [(dh0, "nat")], tt,
                                   [True], adds=[dx2])
    order = [(0, 0), (0, 1), (1, 0), (1, 1)]
    norm_grads = lambda: {"ffn_norm": _nat_to_col_blocks(jnp.concatenate([dgn[o] for o in order], axis=0))}
    dx0, zero = fb("ffn00_bwd", "f00", x, dx1, 0, 0, extra=norm_grads)
    G["attn_w_out"] = _linear_dw("attn_out_dw", tied(merged, zero), dx2, tf, 512)
    G["attn_w_in"] = _linear_dw("attn_in_dw", tied(h0, zero), dproj, tf, 512, dy_layout="hm")
    rep_grads = {"mix_norm": jnp.concatenate([d_mixn0, d_mixn1], axis=0), "rel_bias": _bias_tiles_bwd(dbias, s),
                 "attn_q_norm": dqn_w, "attn_k_norm": dkn_w}
    zero = put("att", rep_grads, {k: _nat_to_col_blocks(G[k]) for k in ("attn_w_in", "attn_w_out")})
    return loss, dx0, zero


WEIGHTS = ['ffn_norm', 'ffn_w_gate', 'ffn_w_up', 'ffn_w_down', 'mix_norm', 'rel_bias', 'attn_w_in', 'attn_q_norm',
           'attn_k_norm', 'attn_w_out', 'rw_mix', 'rw_w0', 'rw_w1', 'rw_w2', 'rw_a0', 'rw_a1', 'rw_a2', 'rw_g1', 'rw_g2',
           'rw_kk', 'rw_ka', 'rw_rk', 'rw_wr', 'rw_wk', 'rw_wv', 'rw_wo', 'rw_lnx_g', 'rw_lnx_b']
REPLICATED = ('mix_norm', 'rel_bias', 'attn_q_norm', 'attn_k_norm', 'rw_rk')
BF16_WEIGHTS = ('ffn_w_gate', 'ffn_w_up', 'ffn_w_down', 'attn_w_in', 'attn_w_out', 'rw_wr', 'rw_wk', 'rw_wv', 'rw_wo')


def kernel(x, ffn_norm, ffn_w_gate, ffn_w_up, ffn_w_down, mix_norm, rel_bias, attn_w_in, attn_q_norm, attn_k_norm, attn_w_out, rw_mix, rw_w0, rw_w1, rw_w2, rw_a0, rw_a1, rw_a2, rw_g1, rw_g2, rw_kk, rw_ka, rw_rk, rw_wr, rw_wk, rw_wv, rw_wo, rw_lnx_g, rw_lnx_b, loss_target, m_ffn_norm, m_ffn_w_gate, m_ffn_w_up, m_ffn_w_down, m_mix_norm, m_rel_bias, m_attn_w_in, m_attn_q_norm, m_attn_k_norm, m_attn_w_out, m_rw_mix, m_rw_w0, m_rw_w1, m_rw_w2, m_rw_a0, m_rw_a1, m_rw_a2, m_rw_g1, m_rw_g2, m_rw_kk, m_rw_ka, m_rw_rk, m_rw_wr, m_rw_wk, m_rw_wv, m_rw_wo, m_rw_lnx_g, m_rw_lnx_b, v_ffn_norm, v_ffn_w_gate, v_ffn_w_up, v_ffn_w_down, v_mix_norm, v_rel_bias, v_attn_w_in, v_attn_q_norm, v_attn_k_norm, v_attn_w_out, v_rw_mix, v_rw_w0, v_rw_w1, v_rw_w2, v_rw_a0, v_rw_a1, v_rw_a2, v_rw_g1, v_rw_g2, v_rw_kk, v_rw_ka, v_rw_rk, v_rw_wr, v_rw_wk, v_rw_wv, v_rw_wo, v_rw_lnx_g, v_rw_lnx_b):
    args = locals()
    w = {k: args[k] for k in WEIGHTS}
    cast = lambda k, a: a.astype(BF16) if k in BF16_WEIGHTS else a

    sources = {}
    for l, h in ((0, 0), (0, 1), (1, 0), (1, 1)):
        sources["f%d%d" % (l, h)] = {"gate": cast("ffn_w_gate", ffn_w_gate[l, h]), "up": cast("ffn_w_up", ffn_w_up[l, h]),
                                     "down": cast("ffn_w_down", ffn_w_down[l, h])}
    sources["f00"]["ffn_norm"] = ffn_norm
    drop_lead = lambda a: a[0] if a.ndim == 3 else a
    sources["att"] = {k: cast(k, w[k][0]) for k in ("attn_w_in", "attn_w_out")}
    sources["rw"] = {k: cast(k, drop_lead(w[k])) for k in RW_SHARDED}
    ag, token = {}, None
    for group in AG_GROUPS:
        names, arrays = list(sources[group]), list(sources[group].values())
        if token is not None:
            arrays[0] = arrays[0] + token[0, 0].astype(arrays[0].dtype)
        ag[group] = (names, _exchange_start("ag_start_" + group, arrays, []))
        token = ag[group][1]["token"]
    last_ag_token = token

    def get(group, after):
        names, started = ag[group]
        gathered, _ = _exchange_wait("ag_wait_" + group, started, last_ag_token if after is None else after)
        return dict(zip(names, gathered))

    rs = {}

    def put(group, rep_grads, shard_grads):
        if group in BF16_GRAD_GROUPS:
            shard_grads = {k: v.astype(BF16) for k, v in shard_grads.items()}
        started = _exchange_start("rs_start_" + group, list(rep_grads.values()), list(shard_grads.values()))
        rs[group] = (list(rep_grads), list(shard_grads), started)
        return started["token"]

    loss, dx, last_zero = _step(x[0], loss_target[0], {k: w[k] for k in REPLICATED}, get, put)
    loss = lax.psum(loss, MESH_AXES)

    results = {}
    ffn_prev = {}

    def update(k, parts, row0=0, prev=None):
        c = w[k].shape[-1]
        as2d = lambda a: a.reshape(-1, c)
        return _adamw("adamw_%s_%d" % (k, row0), as2d(w[k]), as2d(args["m_" + k]), as2d(args["v_" + k]),
                      parts.reshape(N_DEV, -1, c), row0, prev)

    def update_small(name, named_parts):
        items = []
        for k, parts in named_parts:
            c = w[k].shape[-1]
            items.append((w[k].reshape(-1, c), args["m_" + k].reshape(-1, c), args["v_" + k].reshape(-1, c),
                          parts.reshape(N_DEV, -1, c)))
        for (k, _), res in zip(named_parts, _adamw_many(name, items)):
            results[k] = res

    after = last_zero
    for group in RS_GROUPS:
        rep_names, shard_names, started = rs[group]
        rep_parts, shard_parts = _exchange_wait("rs_wait_" + group, started, after)
        small = []
        for k, parts in list(zip(rep_names, rep_parts)) + list(zip(shard_names, shard_parts)):
            if k in ("gate", "up", "down"):
                full = "ffn_w_" + k
                piece = 2 * int(group[1]) + int(group[2])
                ffn_prev[full] = update(full, parts, piece * parts.shape[1], ffn_prev.get(full))
                results[full] = ffn_prev[full]
                after = results[full][0]
            elif k == "attn_w_in":
                results[k] = update(k, parts)
                after = results[k][0]
            else:
                small.append((k, parts))
        if small:
            update_small("adamw_small_" + group, small)
            after = results[small[0][0]][0]

    outs = [[results[k][j].reshape(w[k].shape) for k in WEIGHTS] for j in range(4)]
    return (loss, dx[None], *outs[0], *outs[1], *outs[2], *outs[3])
```

```python
import functools
import math

import numpy as np
import jax
import jax.numpy as jnp
from jax import lax
from jax.experimental import pallas as pl
from jax.experimental.pallas import tpu as pltpu

F32, BF16 = jnp.float32, jnp.bfloat16

N_DEV = 8
D_MODEL = 1024
HEAD = 64
N_HEADS = 16
SB_HEADS = 4
DIL_GROUP = 4
DIL_PATTERNS = ((128, 1), (512, 4), (2048, 16))
QBLK = 128
N_BUCKETS = 32
MAX_DISTANCE = 2048
NORM_EPS = 1e-6
GN_EPS = 64e-5
NEG_INF = -1e30
RW_CHUNK = 64
RW_HB = 16
ADAM_LR, ADAM_B1, ADAM_B2, ADAM_EPS, ADAM_WD, ADAM_STEP = 0.001, 0.9, 0.999, 1e-08, 0.01, 10
MESH_AXES = ("x", "y", "c")
VMEM_LIMIT_BYTES = 56 * 1024 * 1024

NN2 = (((1,), (0,)), ((), ()))
NT2 = (((1,), (1,)), ((), ()))
TN2 = (((0,), (0,)), ((), ()))
NN3 = (((2,), (1,)), ((0,), (0,)))
NT3 = (((2,), (2,)), ((0,), (0,)))
TN3 = (((1,), (1,)), ((0,), (0,)))


def _dot(a, b, dims=NN2, prec=None):
    return lax.dot_general(a, b, dims, precision=prec, preferred_element_type=F32)


def _params(sem=None):
    return pltpu.CompilerParams(dimension_semantics=sem, vmem_limit_bytes=VMEM_LIMIT_BYTES)


@jax.custom_vjp
def _mm(x, w):
    return _dot(x.astype(BF16), w.astype(BF16))


def _mm_fwd(x, w):
    return _mm(x, w), (x, w)


def _mm_bwd(res, dy):
    x, w = res
    dyb = dy.astype(BF16)
    return (_dot(dyb, w.astype(BF16), NT2).astype(x.dtype), _dot(x.astype(BF16), dyb, TN2).astype(w.dtype))


_mm.defvjp(_mm_fwd, _mm_bwd)


def _rms(x, g):
    return x * lax.rsqrt(jnp.mean(x * x, axis=-1, keepdims=True) + NORM_EPS) * g


def _log_sigmoid(z):
    return jnp.minimum(z, 0.0) - jnp.log(1.0 + jnp.exp(-jnp.abs(z)))


def _heads_to_nat(v3):
    return jnp.concatenate([v3[h] for h in range(v3.shape[0])], axis=-1)


def _nat_to_heads(v2):
    return jnp.stack([v2[:, h * HEAD:(h + 1) * HEAD] for h in range(v2.shape[1] // HEAD)], axis=0)


def _exchange(name, gathers, scatters):
    n_g = len(gathers)
    arrays = list(gathers) + list(scatters)
    n = len(arrays)
    out_shape = [jax.ShapeDtypeStruct((N_DEV,) + a.shape, a.dtype) for a in gathers]
    out_shape += [jax.ShapeDtypeStruct(a.shape, a.dtype) for a in scatters]

    def body(*refs):
        ins, outs = refs[:n], refs[n:2 * n]
        send_sems, recv_sems, local_sems = refs[2 * n:]
        x, y, c = lax.axis_index("x"), lax.axis_index("y"), lax.axis_index("c")
        me = 4 * x + 2 * y + c

        def src(i, idx):
            return ins[i] if i < n_g else ins[i].at[idx]

        local = [pltpu.make_async_copy(src(i, me), outs[i].at[me], local_sems.at[i]) for i in range(n)]
        for cp in local:
            cp.start()
        remote = []
        for m in range(1, N_DEV):
            px, py, pc = x ^ ((m >> 2) & 1), y ^ ((m >> 1) & 1), c ^ (m & 1)
            peer = 4 * px + 2 * py + pc
            for i in range(n):
                cp = pltpu.make_async_remote_copy(
                    src_ref=src(i, peer), dst_ref=outs[i].at[me],
                    send_sem=send_sems.at[i, m - 1], recv_sem=recv_sems.at[i, m - 1],
                    device_id=(px, py, pc), device_id_type=pl.DeviceIdType.MESH)
                cp.start()
                arrival = pltpu.make_async_remote_copy(
                    src_ref=src(i, peer), dst_ref=outs[i].at[peer],
                    send_sem=send_sems.at[i, m - 1], recv_sem=recv_sems.at[i, m - 1],
                    device_id=(px, py, pc), device_id_type=pl.DeviceIdType.MESH)
                remote.append((cp, arrival))
        for cp, arrival in remote:
            cp.wait_send()
            arrival.wait_recv()
        for cp in local:
            cp.wait()

    hbm = pl.BlockSpec(memory_space=pltpu.HBM)
    outs = pl.pallas_call(
        body, name=name, out_shape=out_shape,
        in_specs=[hbm] * n, out_specs=[hbm] * n,
        scratch_shapes=[pltpu.SemaphoreType.DMA((n, N_DEV - 1)), pltpu.SemaphoreType.DMA((n, N_DEV - 1)),
                        pltpu.SemaphoreType.DMA((n,))],
    )(*arrays)
    return list(outs[:n_g]), list(outs[n_g:])


def _mesh_peers():
    x, y, c = lax.axis_index("x"), lax.axis_index("y"), lax.axis_index("c")
    peers = []
    for m in range(1, N_DEV):
        px, py, pc = x ^ ((m >> 2) & 1), y ^ ((m >> 1) & 1), c ^ (m & 1)
        peers.append((m, (px, py, pc), 4 * px + 2 * py + pc))
    return 4 * x + 2 * y + c, peers


_HBM_SPEC = pl.BlockSpec(memory_space=pltpu.HBM)
_SEM_SPEC = pl.BlockSpec(memory_space=pltpu.SEMAPHORE)
_DATAFLOW = pltpu.SideEffectType.DATAFLOW_SIDE_EFFECTING


def _exchange_start(name, gathers, scatters):
    n_g = len(gathers)
    arrays = list(gathers) + list(scatters)
    n = len(arrays)
    lands = ([lax.empty((N_DEV,) + a.shape, a.dtype) for a in gathers] + [lax.empty(a.shape, a.dtype) for a in scatters])

    def body(*refs):
        ins, land = refs[:n], refs[n:2 * n]
        send_sems, recv_sems, local_sems, token = refs[2 * n], refs[2 * n + 1], refs[2 * n + 2], refs[-1]
        me, peers = _mesh_peers()
        for m, dev, peer in peers:
            for i in range(n):
                k = i * (N_DEV - 1) + m - 1
                pltpu.make_async_remote_copy(
                    src_ref=ins[i] if i < n_g else ins[i].at[peer], dst_ref=land[i].at[me],
                    send_sem=send_sems.at[k], recv_sem=recv_sems.at[k],
                    device_id=dev, device_id_type=pl.DeviceIdType.MESH).start()
        for i in range(n):
            pltpu.make_async_copy(ins[i] if i < n_g else ins[i].at[me], land[i].at[me], local_sems.at[i]).start()
        token[...] = jnp.zeros_like(token)

    sem = pltpu.SemaphoreType.DMA((n * (N_DEV - 1),))
    outs = pl.pallas_call(
        body, name=name,
        out_shape=([sem, sem, pltpu.SemaphoreType.DMA((n,))] + [pltpu.HBM(a.shape, a.dtype) for a in arrays]
                   + [pltpu.HBM(l.shape, l.dtype) for l in lands] + [jax.ShapeDtypeStruct((8, 128), F32)]),
        in_specs=[_HBM_SPEC] * (2 * n),
        out_specs=[_SEM_SPEC] * 3 + [_HBM_SPEC] * (2 * n) + [pl.BlockSpec(memory_space=pltpu.VMEM)],
        input_output_aliases={i: i + 3 for i in range(2 * n)},
        compiler_params=pltpu.CompilerParams(has_side_effects=_DATAFLOW),
    )(*[pltpu.with_memory_space_constraint(a, pltpu.HBM) for a in arrays],
      *[pltpu.with_memory_space_constraint(l, pltpu.HBM) for l in lands])
    return dict(n_g=n_g, n=n, send=outs[0], recv=outs[1], local=outs[2], srcs=list(outs[3:3 + n]),
                lands=list(outs[3 + n:3 + 2 * n]), token=outs[-1])


def _exchange_wait(name, started, after):
    n, n_g = started["n"], started["n_g"]

    def body(*refs):
        srcs, lands = refs[:n], refs[n:2 * n]
        send_sems, recv_sems, local_sems = refs[2 * n], refs[2 * n + 1], refs[2 * n + 2]
        me, peers = _mesh_peers()
        local = [pltpu.make_async_copy(srcs[i] if i < n_g else srcs[i].at[me], lands[i].at[me], local_sems.at[i])
                 for i in range(n)]
        for m, dev, peer in peers:
            for i in range(n):
                k = i * (N_DEV - 1) + m - 1
                cp = pltpu.make_async_remote_copy(
                    src_ref=srcs[i] if i < n_g else srcs[i].at[peer], dst_ref=lands[i].at[peer],
                    send_sem=send_sems.at[k], recv_sem=recv_sems.at[k],
                    device_id=dev, device_id_type=pl.DeviceIdType.MESH)
                cp.wait_send()
                cp.wait_recv()
        for cp in local:
            cp.wait()

    outs = pl.pallas_call(
        body, name=name,
        out_shape=([pltpu.HBM(a.shape, a.dtype) for a in started["srcs"]]
                   + [pltpu.HBM(l.shape, l.dtype) for l in started["lands"]]),
        in_specs=[_HBM_SPEC] * (2 * n) + [_SEM_SPEC] * 3 + [pl.BlockSpec(memory_space=pl.ANY)],
        out_specs=[_HBM_SPEC] * (2 * n), input_output_aliases={i: i for i in range(2 * n)},
        compiler_params=pltpu.CompilerParams(has_side_effects=_DATAFLOW),
    )(*started["srcs"], *started["lands"], started["send"], started["recv"], started["local"], after)
    return list(outs[n:n + n_g]), list(outs[n + n_g:])


def _tile_spec(shape, layout, t):
    if layout == "nat":
        return pl.BlockSpec((t, shape[1]), lambda i: (i, 0))
    return pl.BlockSpec((shape[0], t, shape[2]), lambda i: (0, i, 0))


def _full_spec(shape):
    nd = len(shape)
    return pl.BlockSpec(tuple(shape), lambda i: (0,) * nd)


def _seq_len(a, layout):
    return a.shape[0] if layout == "nat" else a.shape[1]


def _tile_fwd(name, f, tiled, params, outs, t):
    nt, npar = len(tiled), len(params)
    s = _seq_len(*tiled[0])

    def body(*refs):
        vals = [r[...] for r in refs[:nt + npar]]
        res = f(*vals)
        for r, o in zip(refs[nt + npar:], res):
            r[...] = o.astype(r.dtype)

    return pl.pallas_call(
        body, name=name, grid=(s // t,),
        in_specs=[_tile_spec(a.shape, l, t) for a, l in tiled] + [_full_spec(p.shape) for p in params],
        out_specs=[_tile_spec(sh, l, t) for sh, _, l in outs],
        out_shape=[jax.ShapeDtypeStruct(sh, dt) for sh, dt, _ in outs],
        compiler_params=_params(("arbitrary",)),
    )(*[a for a, _ in tiled], *params)


def _tile_bwd(name, f, tiled, params, cts, t, need, adds=None):
    nt, npar, nc = len(tiled), len(params), len(cts)
    s = _seq_len(*tiled[0])
    need_idx = [k for k in range(nt) if need[k]]
    adds = adds or [None] * len(need_idx)
    add_arrays = [(a, tiled[k][1]) for a, k in zip(adds, need_idx) if a is not None]
    n_add = len(add_arrays)

    def body(*refs):
        i = pl.program_id(0)
        vals = [r[...] for r in refs[:nt + npar]]
        ct_refs = refs[nt + npar:nt + npar + nc]
        add_refs = refs[nt + npar + nc:nt + npar + nc + n_add]
        out_refs = refs[nt + npar + nc + n_add:]
        res, vjp = jax.vjp(f, *vals)
        grads = vjp(tuple(r[...].astype(o.dtype) for r, o in zip(ct_refs, res)))
        a = 0
        for j, k in enumerate(need_idx):
            g = grads[k]
            if adds[j] is not None:
                g = g + add_refs[a][...]
                a += 1
            out_refs[j][...] = g.astype(out_refs[j].dtype)
        for j in range(npar):
            r = out_refs[len(need_idx) + j]

            @pl.when(i == 0)
            def _():
                r[...] = jnp.zeros_like(r)

            r[...] += grads[nt + j]

    outs = pl.pallas_call(
        body, name=name, grid=(s // t,),
        in_specs=([_tile_spec(a.shape, l, t) for a, l in tiled] + [_full_spec(p.shape) for p in params]
                  + [_tile_spec(a.shape, l, t) for a, l in cts] + [_tile_spec(a.shape, l, t) for a, l in add_arrays]),
        out_specs=([_tile_spec(tiled[k][0].shape, tiled[k][1], t) for k in need_idx]
                   + [_full_spec(p.shape) for p in params]),
        out_shape=([jax.ShapeDtypeStruct(tiled[k][0].shape, F32) for k in need_idx]
                   + [jax.ShapeDtypeStruct(p.shape, F32) for p in params]),
        compiler_params=_params(("arbitrary",)),
    )(*[a for a, _ in tiled], *params, *[a for a, _ in cts], *[a for a, _ in add_arrays])
    return list(outs[:len(need_idx)]), list(outs[len(need_idx):])


def _linear_fwd(name, x, w, t, out_layout="nat", residual=None):
    s, k = x.shape
    n = w.shape[1]
    has_res = residual is not None

    def body(*refs):
        x_ref, w_ref = refs[0], refs[1]
        o_ref = refs[-1]
        y = _dot(x_ref[...].astype(BF16), w_ref[...])
        if has_res:
            y = y + refs[2][...]
        if out_layout == "hm":
            for h in range(n // HEAD):
                o_ref[h] = y[:, h * HEAD:(h + 1) * HEAD]
        else:
            o_ref[...] = y

    out_sh = (s, n) if out_layout == "nat" else (n // HEAD, s, HEAD)
    ins = [x, w] + ([residual] if has_res else [])
    in_specs = [_tile_spec(x.shape, "nat", t), _full_spec(w.shape)] + ([_tile_spec((s, n), "nat", t)] if has_res else [])
    return pl.pallas_call(
        body, name=name, grid=(s // t,), in_specs=in_specs,
        out_specs=_tile_spec(out_sh, out_layout, t), out_shape=jax.ShapeDtypeStruct(out_sh, F32),
        compiler_params=_params(("arbitrary",)),
    )(*ins)


def _linear_dx(name, dy, w, t, dy_layout="nat"):
    k, n = w.shape
    s = _seq_len(dy, dy_layout)

    def body(dy_ref, w_ref, o_ref):
        dy = _heads_to_nat(dy_ref[...].astype(BF16)) if dy_layout == "hm" else dy_ref[...].astype(BF16)
        o_ref[...] = _dot(dy, w_ref[...], NT2)

    return pl.pallas_call(
        body, name=name, grid=(s // t,),
        in_specs=[_tile_spec(dy.shape, dy_layout, t), _full_spec(w.shape)],
        out_specs=_tile_spec((s, k), "nat", t), out_shape=jax.ShapeDtypeStruct((s, k), F32),
        compiler_params=_params(("arbitrary",)),
    )(dy, w)


def _linear_dw(name, x, dy, t, nb, dy_layout="nat"):
    s, k = x.shape
    n = dy.shape[1] if dy_layout == "nat" else dy.shape[0] * HEAD

    def body(x_ref, dy_ref, o_ref):
        i = pl.program_id(1)

        @pl.when(i == 0)
        def _():
            o_ref[...] = jnp.zeros_like(o_ref)

        dy = _heads_to_nat(dy_ref[...].astype(BF16)) if dy_layout == "hm" else dy_ref[...].astype(BF16)
        o_ref[...] += _dot(x_ref[...].astype(BF16), dy, TN2)

    if dy_layout == "hm":
        dy_spec = pl.BlockSpec((nb // HEAD, t, HEAD), lambda j, i: (j, i, 0))
    else:
        dy_spec = pl.BlockSpec((t, nb), lambda j, i: (i, j))
    return pl.pallas_call(
        body, name=name, grid=(n // nb, s // t),
        in_specs=[pl.BlockSpec((t, k), lambda j, i: (i, 0)), dy_spec],
        out_specs=pl.BlockSpec((k, nb), lambda j, i: (0, j)), out_shape=jax.ShapeDtypeStruct((k, n), F32),
        compiler_params=_params(("arbitrary", "arbitrary")),
    )(x, dy)


def _ffn_fwd(name, x, gn, wg, wu, wd, t):
    s, d = x.shape
    f8 = wg.shape[-1]

    def body(x_ref, g_ref, wg_ref, wu_ref, wd_ref, o_ref, gk_ref, uk_ref, h_scr, acc):
        k = pl.program_id(1)

        @pl.when(k == 0)
        def _():
            h_scr[...] = _rms(x_ref[...], g_ref[...]).astype(BF16)
            acc[...] = jnp.zeros_like(acc)

        hb = h_scr[...]
        gk = _dot(hb, wg_ref[0])
        uk = _dot(hb, wu_ref[0])
        gk_ref[0] = gk
        uk_ref[0] = uk
        a = gk * jax.nn.sigmoid(gk) * uk
        acc[...] += _dot(a.astype(BF16), wd_ref[0])

        @pl.when(k == N_DEV - 1)
        def _():
            o_ref[...] = x_ref[...] + 0.5 * acc[...]

    wspec = lambda shp: pl.BlockSpec((1,) + shp, lambda i, k: (k, 0, 0))
    act = pl.BlockSpec((1, t, f8), lambda i, k: (k, i, 0))
    act_sh = jax.ShapeDtypeStruct((N_DEV, s, f8), F32)
    return pl.pallas_call(
        body, name=name, grid=(s // t, N_DEV),
        in_specs=[pl.BlockSpec((t, d), lambda i, k: (i, 0)), pl.BlockSpec((1, d), lambda i, k: (0, 0)),
                  wspec((d, f8)), wspec((d, f8)), wspec((f8, d))],
        out_specs=[pl.BlockSpec((t, d), lambda i, k: (i, 0)), act, act],
        out_shape=[jax.ShapeDtypeStruct((s, d), F32), act_sh, act_sh],
        scratch_shapes=[pltpu.VMEM((t, d), BF16), pltpu.VMEM((t, d), F32)],
        compiler_params=_params(("arbitrary", "arbitrary")),
    )(x, gn, wg, wu, wd)


def _ffn_bwd(name, x, dy, gn, wg, wu, wd, gact, uact, t):
    s, d = x.shape
    f8 = wg.shape[-1]
    last = N_DEV - 1

    def body(x_ref, dy_ref, g_ref, wg_ref, wu_ref, wd_ref, gk_ref, uk_ref,
             dx_ref, dg_ref, dwg_ref, dwu_ref, dwd_ref, dh_scr):
        k, i = pl.program_id(0), pl.program_id(1)
        x = x_ref[...]
        rs = lax.rsqrt(jnp.mean(x * x, axis=-1, keepdims=True) + NORM_EPS)
        xn = x * rs
        hb = (xn * g_ref[...]).astype(BF16)
        dob = (0.5 * dy_ref[...]).astype(BF16)
        wgk, wuk, wdk = wg_ref[0], wu_ref[0], wd_ref[0]
        gk, uk = gk_ref[0], uk_ref[0]
        sg = jax.nn.sigmoid(gk)
        sk = gk * sg
        da = _dot(dob, wdk, NT2)
        du = (da * sk).astype(BF16)
        dg = (da * uk * (sg * (1.0 + gk * (1.0 - sg)))).astype(BF16)

        dwd_c = _dot((sk * uk).astype(BF16), dob, TN2)
        dwg_c = _dot(hb, dg, TN2)
        dwu_c = _dot(hb, du, TN2)
        dh = _dot(dg, wgk, NT2) + _dot(du, wuk, NT2)
        rows = pl.ds(pl.multiple_of(i * t, t), t)

        @pl.when(i == 0)
        def _():
            dwg_ref[0], dwu_ref[0], dwd_ref[0] = dwg_c, dwu_c, dwd_c

        @pl.when(i > 0)
        def _():
            dwg_ref[0] += dwg_c
            dwu_ref[0] += dwu_c
            dwd_ref[0] += dwd_c

        @pl.when(k == 0)
        def _():
            dh_scr[rows, :] = dh

        @pl.when(k > 0)
        def _():
            dh_scr[rows, :] += dh

        @pl.when(jnp.logical_and(k == last, i == 0))
        def _():
            dg_ref[...] = jnp.zeros_like(dg_ref)

        @pl.when(k == last)
        def _():
            dht = dh_scr[rows, :]
            dg_ref[...] += jnp.sum(dht * xn, axis=0, keepdims=True)
            dxn = dht * g_ref[...]
            dx_ref[...] = dy_ref[...] + rs * (dxn - xn * jnp.mean(dxn * xn, axis=-1, keepdims=True))

    wspec = lambda shp: pl.BlockSpec((1,) + shp, lambda k, i: (k, 0, 0))
    tile = pl.BlockSpec((t, d), lambda k, i: (i, 0))
    act = pl.BlockSpec((1, t, f8), lambda k, i: (k, i, 0))
    return pl.pallas_call(
        body, name=name, grid=(N_DEV, s // t),
        in_specs=[tile, tile, pl.BlockSpec((1, d), lambda k, i: (0, 0)), wspec((d, f8)), wspec((d, f8)), wspec((f8, d)),
                  act, act],
        out_specs=[pl.BlockSpec((t, d), lambda k, i: (jnp.where(k == last, i, 0), 0)),
                   pl.BlockSpec((1, d), lambda k, i: (0, 0)),
                   pl.BlockSpec((1, d, f8), lambda k, i: (k, 0, 0)), pl.BlockSpec((1, d, f8), lambda k, i: (k, 0, 0)),
                   pl.BlockSpec((1, f8, d), lambda k, i: (k, 0, 0))],
        out_shape=[jax.ShapeDtypeStruct((s, d), F32), jax.ShapeDtypeStruct((1, d), F32),
                   jax.ShapeDtypeStruct((N_DEV, d, f8), F32), jax.ShapeDtypeStruct((N_DEV, d, f8), F32),
                   jax.ShapeDtypeStruct((N_DEV, f8, d), F32)],
        scratch_shapes=[pltpu.VMEM((s, d), F32)],
        compiler_params=_params(("arbitrary", "arbitrary")),
    )(x, dy, gn, wg, wu, wd, gact, uact)


def _loss_head(y, target, t):
    s, d = y.shape

    def body(y_ref, t_ref, dy_ref, l_ref):
        i = pl.program_id(0)
        err = y_ref[...] - t_ref[...]
        dy_ref[...] = err * (1.0 / d)

        @pl.when(i == 0)
        def _():
            l_ref[...] = jnp.zeros_like(l_ref)

        l_ref[...] += 0.5 * jnp.sum(jnp.mean(err * err, axis=-1, keepdims=True), axis=0, keepdims=True)

    tile = pl.BlockSpec((t, d), lambda i: (i, 0))
    dy, l = pl.pallas_call(
        body, name="loss_head", grid=(s // t,), in_specs=[tile, tile],
        out_specs=[tile, pl.BlockSpec((1, 1), lambda i: (0, 0))],
        out_shape=[jax.ShapeDtypeStruct((s, d), F32), jax.ShapeDtypeStruct((1, 1), F32)],
        compiler_params=_params(("arbitrary",)),
    )(y, target)
    return dy, l[0, 0]


SB_KEY_TILE = 1024
SB_HEADS_PER_STEP = 4


def _sb_scan_mats():
    row = lax.broadcasted_iota(jnp.int32, (QBLK, QBLK), 0)
    col = lax.broadcasted_iota(jnp.int32, (QBLK, QBLK), 1)
    return (row > col).astype(F32).astype(BF16), (row < col).astype(F32).astype(BF16)


def _sb_tile_scan(x, mat, reverse):
    nsub = x.shape[1] // QBLK
    outs, carry = [None] * nsub, jnp.zeros((x.shape[0], 1), F32)
    for i in (reversed(range(nsub)) if reverse else range(nsub)):
        xs = x[:, i * QBLK:(i + 1) * QBLK]
        hi = xs.astype(BF16)
        lo = (xs - hi.astype(F32)).astype(BF16)
        outs[i] = _dot(hi, mat) + _dot(lo, mat) + carry
        carry = carry + jnp.sum(xs, axis=1, keepdims=True)
    return jnp.concatenate(outs, axis=1), carry


def _sb_before_query(n, t, kt):
    row = lax.broadcasted_iota(jnp.int32, (QBLK, kt), 0)
    col = lax.broadcasted_iota(jnp.int32, (QBLK, kt), 1)
    return t * kt + col < n * QBLK + row


def _sb_fwd(q, k, v):
    _, s, _ = q.shape
    scale = HEAD ** -0.5
    kt = min(SB_KEY_TILE, s)

    def body(q_ref, k_ref, v_ref, o_ref):
        n = pl.program_id(1)
        suffix, _ = _sb_scan_mats()
        n_tiles = lax.div(n, jnp.int32(kt // QBLK)) + 1
        heads = range(SB_HEADS_PER_STEP)
        qb = [(q_ref[h] * scale).astype(q_ref.dtype) for h in heads]

        def tile(t, carry, diagonal):
            rows = pl.ds(pl.multiple_of(t * kt, kt), kt)
            out = []
            for h in heads:
                c, acc = carry[h]
                z = _dot(qb[h], k_ref[h, rows, :], NT2)
                lk = _log_sigmoid(-z)
                log_beta = z + lk
                if diagonal:
                    ok = _sb_before_query(n, t, kt)
                    lk = jnp.where(ok, lk, 0.0)
                later, total = _sb_tile_scan(lk, suffix, True)
                w = jnp.exp(log_beta + later + c)
                if diagonal:
                    w = jnp.where(ok, w, 0.0)
                out.append((c + total, acc + _dot(w.astype(BF16), v_ref[h, rows, :])))
            return tuple(out)

        zero = (jnp.zeros((QBLK, 1), F32), jnp.zeros((QBLK, HEAD), F32))
        carry = tile(n_tiles - 1, (zero,) * len(heads), True)
        carry = lax.fori_loop(1, n_tiles, lambda jj, cr: tile(n_tiles - 1 - jj, cr, False), carry)
        for h in heads:
            o_ref[h] = carry[h][1]

    hp = SB_HEADS_PER_STEP
    return pl.pallas_call(
        body, name="sb_fwd", grid=(SB_HEADS // hp, s // QBLK),
        in_specs=[pl.BlockSpec((hp, QBLK, HEAD), lambda h, n: (h, n, 0)),
                  pl.BlockSpec((hp, s, HEAD), lambda h, n: (h, 0, 0)),
                  pl.BlockSpec((hp, s, HEAD), lambda h, n: (h, 0, 0))],
        out_specs=pl.BlockSpec((hp, QBLK, HEAD), lambda h, n: (h, n, 0)),
        out_shape=jax.ShapeDtypeStruct((SB_HEADS, s, HEAD), F32),
        compiler_params=_params(("arbitrary", "arbitrary")),
    )(q, k, v)


def _sb_bwd(q, k, v, do):
    _, s, _ = q.shape
    scale = HEAD ** -0.5
    kt = min(SB_KEY_TILE, s)

    def body(q_ref, k_ref, v_ref, do_ref, dq_ref, dk_ref, dv_ref, e_scr, beta_scr):
        n = pl.program_id(1)

        @pl.when(n == 0)
        def _():
            dk_ref[...] = jnp.zeros_like(dk_ref)
            dv_ref[...] = jnp.zeros_like(dv_ref)

        suffix, prefix = _sb_scan_mats()
        n_tiles = lax.div(n, jnp.int32(kt // QBLK)) + 1
        heads = range(SB_HEADS_PER_STEP)
        qb = [(q_ref[h] * scale).astype(q_ref.dtype) for h in heads]
        dob = [do_ref[h].astype(BF16) for h in heads]

        def weights(t, cs, diagonal):
            rows = pl.ds(pl.multiple_of(t * kt, kt), kt)
            out, stores = [], []
            for h in heads:
                vb = v_ref[h, rows, :]
                z = _dot(qb[h], k_ref[h, rows, :], NT2)
                lk = _log_sigmoid(-z)
                log_beta = z + lk
                if diagonal:
                    ok = _sb_before_query(n, t, kt)
                    lk = jnp.where(ok, lk, 0.0)
                later, total = _sb_tile_scan(lk, suffix, True)
                w = jnp.exp(log_beta + later + cs[h])
                if diagonal:
                    w = jnp.where(ok, w, 0.0)
                stores.append((w * _dot(dob[h], vb, NT2), jnp.exp(log_beta), _dot(w.astype(BF16), dob[h], TN2)))
                out.append(cs[h] + total)
            for h in heads:
                e_scr[h, t], beta_scr[h, t] = stores[h][0], stores[h][1]
                dv_ref[h, rows, :] += stores[h][2]
            return tuple(out)

        col0 = jnp.zeros((QBLK, 1), F32)
        cs = weights(n_tiles - 1, (col0,) * len(heads), True)
        lax.fori_loop(1, n_tiles, lambda jj, c: weights(n_tiles - 1 - jj, c, False), cs)

        def grads(t, carry, diagonal):
            rows = pl.ds(pl.multiple_of(t * kt, kt), kt)
            out, dks = [], []
            for h in heads:
                pc, dq = carry[h]
                kb = k_ref[h, rows, :]
                e, beta = e_scr[h, t], beta_scr[h, t]
                before, total = _sb_tile_scan(e, prefix, False)
                dz = e * (1.0 - beta) - beta * (before + pc)
                if diagonal:
                    dz = jnp.where(_sb_before_query(n, t, kt), dz, 0.0)
                dz = dz.astype(BF16)
                dks.append(_dot(dz, qb[h], TN2))
                out.append((pc + total, dq + _dot(dz, kb)))
            for h in heads:
                dk_ref[h, rows, :] += dks[h]
            return tuple(out)

        zero = (col0, jnp.zeros((QBLK, HEAD), F32))
        carry = lax.fori_loop(0, n_tiles - 1, lambda t, cr: grads(t, cr, False), (zero,) * len(heads))
        carry = grads(n_tiles - 1, carry, True)
        for h in heads:
            dq_ref[h] = carry[h][1] * scale

    hp = SB_HEADS_PER_STEP
    qspec = pl.BlockSpec((hp, QBLK, HEAD), lambda h, n: (h, n, 0))
    full = pl.BlockSpec((hp, s, HEAD), lambda h, n: (h, 0, 0))
    sh = jax.ShapeDtypeStruct((SB_HEADS, s, HEAD), F32)
    tiles_sh = (hp, s // kt, QBLK, kt)
    return pl.pallas_call(
        body, name="sb_bwd", grid=(SB_HEADS // hp, s // QBLK),
        in_specs=[qspec, full, full, qspec],
        out_specs=[qspec, full, full], out_shape=[sh, sh, sh],
        scratch_shapes=[pltpu.VMEM(tiles_sh, F32), pltpu.VMEM(tiles_sh, F32)],
        compiler_params=_params(("arbitrary", "arbitrary")),
    )(q, k, v, do)


def _t5_bucket_np(dist):
    max_exact = N_BUCKETS // 2
    d = np.maximum(dist, 1).astype(np.float32)
    large = max_exact + (np.log(d / np.float32(max_exact)) / np.float32(math.log(MAX_DISTANCE / max_exact))
                         * np.float32(N_BUCKETS - max_exact)).astype(np.int32)
    large = np.minimum(large, N_BUCKETS - 1)
    return np.where(dist < max_exact, dist, large)


def _dil_layout(s):
    assert all(s % (QBLK * r) == 0 and window // r == QBLK for window, r in DIL_PATTERNS)
    tiles, buckets = [], []
    i = np.arange(QBLK)[:, None]
    j = np.arange(QBLK)[None, :]
    for g, (window, r) in enumerate(DIL_PATTERNS):
        for off in (0, 1):
            dist = QBLK * off + i - j
            ok = (dist >= 0) & (dist <= window // r)
            tiles.append((g, off))
            buckets.append(np.where(ok, _t5_bucket_np(np.maximum(dist, 0) * r), -1).astype(np.int32))
    return tiles, np.stack(buckets)


def _bias_tiles(rel_bias, s):
    tiles, buckets = _dil_layout(s)
    nt = len(tiles)
    present = [sorted(set(np.unique(buckets[k]).tolist()) - {-1}) for k in range(nt)]

    def body(rel_ref, b_ref, o_ref):
        j = pl.program_id(0)
        for k, (g, _) in enumerate(tiles):
            bk = b_ref[k]
            tile = jnp.full((QBLK, QBLK), NEG_INF, F32)
            for b in present[k]:
                tile = jnp.where(bk == b, rel_ref[b, g * DIL_GROUP + j], tile)
            o_ref[0, k] = tile

    return pl.pallas_call(
        body, name="bias_tiles", grid=(DIL_GROUP,),
        in_specs=[pl.BlockSpec(memory_space=pltpu.SMEM), pl.BlockSpec((nt, QBLK, QBLK), lambda j: (0, 0, 0))],
        out_specs=pl.BlockSpec((1, nt, QBLK, QBLK), lambda j: (j, 0, 0, 0)),
        out_shape=jax.ShapeDtypeStruct((DIL_GROUP, nt, QBLK, QBLK), F32),
        compiler_params=_params(("arbitrary",)),
    )(rel_bias, jnp.asarray(buckets))


def _bias_tiles_bwd(dbias, s):
    tiles, buckets = _dil_layout(s)
    nt = len(tiles)
    present = [sorted(set(np.unique(buckets[k]).tolist()) - {-1}) for k in range(nt)]

    def body(d_ref, b_ref, o_ref):
        j = pl.program_id(0)

        @pl.when(j == 0)
        def _():
            for b in range(N_BUCKETS):
                for col in range(3 * DIL_GROUP):
                    o_ref[b, col] = jnp.float32(0.0)

        for k, (g, _) in enumerate(tiles):
            bk, dk = b_ref[k], d_ref[0, k]
            for b in present[k]:
                o_ref[b, g * DIL_GROUP + j] += jnp.sum(jnp.where(bk == b, dk, 0.0))

    return pl.pallas_call(
        body, name="bias_tiles_bwd", grid=(DIL_GROUP,),
        in_specs=[pl.BlockSpec((1, nt, QBLK, QBLK), lambda j: (j, 0, 0, 0)),
                  pl.BlockSpec((nt, QBLK, QBLK), lambda j: (0, 0, 0))],
        out_specs=pl.BlockSpec(memory_space=pltpu.SMEM),
        out_shape=jax.ShapeDtypeStruct((N_BUCKETS, 3 * DIL_GROUP), F32),
        compiler_params=_params(("arbitrary",)),
    )(dbias, jnp.asarray(buckets))


DIL_PAIRS_PER_STEP = 8


def _dil_rows(g, s, pair):
    _, r = DIL_PATTERNS[g]
    nb = s // (QBLK * r)
    c, n = lax.div(pair, jnp.int32(nb)), lax.rem(pair, jnp.int32(nb))
    start = c + (r * QBLK) * n
    before = jnp.where(n > 0, start - r * QBLK, start)
    if r == 1:
        return pl.ds(start, QBLK), pl.ds(before, QBLK), n > 0
    return pl.ds(start, QBLK, stride=r), pl.ds(before, QBLK, stride=r), n > 0


def _dil_logits(qb, k_ref, rows, before, has_before, b_ref):
    k0, k1 = k_ref[0, rows, :].astype(BF16), k_ref[0, before, :].astype(BF16)
    l0 = _dot(qb, k0, NT2) + b_ref[0, 0]
    l1 = jnp.where(has_before, _dot(qb, k1, NT2) + b_ref[0, 1], NEG_INF)
    return k0, k1, l0, l1


def _dil_group_specs(g, s):
    head = pl.BlockSpec((1, s, HEAD), lambda j, p: (DIL_GROUP * g + j, 0, 0))
    return [head, head, head, pl.BlockSpec((1, 2, QBLK, QBLK), lambda j, p: (j, g, 0, 0))]


def _dil_group_fwd(g, qn, kn, v, bias):
    _, s, _ = qn.shape
    scale = HEAD ** -0.5
    steps = (s // QBLK) // DIL_PAIRS_PER_STEP

    def body(q_ref, k_ref, v_ref, b_ref, o_ref):
        pairs = [_dil_rows(g, s, pl.program_id(1) * DIL_PAIRS_PER_STEP + u) for u in range(DIL_PAIRS_PER_STEP)]
        loaded = []
        for rows, before, has_before in pairs:
            qb = (q_ref[0, rows, :] * scale).astype(BF16)
            _, _, l0, l1 = _dil_logits(qb, k_ref, rows, before, has_before, b_ref)
            loaded.append((l0, l1, v_ref[0, rows, :].astype(BF16), v_ref[0, before, :].astype(BF16)))
        results = []
        for l0, l1, v0, v1 in loaded:
            m = jnp.max(jnp.maximum(l0, l1), axis=1, keepdims=True)
            p0, p1 = jnp.exp(l0 - m), jnp.exp(l1 - m)
            den = jnp.sum(p0 + p1, axis=1, keepdims=True)
            inv = 1.0 / den
            o = _dot((p0 * inv).astype(BF16), v0) + _dot((p1 * inv).astype(BF16), v1)
            results.append(jnp.concatenate([o, jnp.broadcast_to(m + jnp.log(den), (QBLK, HEAD))], axis=1))
        for (rows, _, _), res in zip(pairs, results):
            o_ref[0, rows, :] = res

    return pl.pallas_call(
        body, name="dil%d_fwd" % g, grid=(DIL_GROUP, steps), in_specs=_dil_group_specs(g, s),
        out_specs=pl.BlockSpec((1, s, 2 * HEAD), lambda j, p: (j, 0, 0)),
        out_shape=jax.ShapeDtypeStruct((DIL_GROUP, s, 2 * HEAD), F32),
        compiler_params=_params(("arbitrary", "arbitrary")),
    )(qn, kn, v, bias)


def _dil_group_bwd(g, qn, kn, v, bias, ol, dol, prev):
    _, s, _ = qn.shape
    scale = HEAD ** -0.5
    steps = (s // QBLK) // DIL_PAIRS_PER_STEP
    prev = list(prev) if prev is not None else []

    def body(q_ref, k_ref, v_ref, b_ref, ol_ref, dol_ref, *rest):
        dq_ref, dk_ref, dv_ref, db_ref = rest[-4:]

        @pl.when(pl.program_id(1) == 0)
        def _():
            for r in (dk_ref, dv_ref, db_ref):
                r[...] = jnp.zeros_like(r)

        pairs = [_dil_rows(g, s, pl.program_id(1) * DIL_PAIRS_PER_STEP + u) for u in range(DIL_PAIRS_PER_STEP)]
        loaded = []
        for rows, before, has_before in pairs:
            qb = (q_ref[0, rows, :] * scale).astype(BF16)
            k0, k1, l0, l1 = _dil_logits(qb, k_ref, rows, before, has_before, b_ref)
            v0, v1 = v_ref[0, rows, :].astype(BF16), v_ref[0, before, :].astype(BF16)
            loaded.append((qb, k0, k1, l0, l1, v0, v1, ol_ref[0, rows, :], dol_ref[0, rows, :]))
        grads = []
        for qb, k0, k1, l0, l1, v0, v1, out_lse, d_out_lse in loaded:
            o, lse = out_lse[:, :HEAD], out_lse[:, HEAD:HEAD + 1]
            do, dlse = d_out_lse[:, :HEAD], d_out_lse[:, HEAD:HEAD + 1]
            dob = do.astype(BF16)
            p0, p1 = jnp.exp(l0 - lse), jnp.exp(l1 - lse)
            shift = dlse - jnp.sum(do * o, axis=1, keepdims=True)
            dl0 = p0 * (_dot(dob, v0, NT2) + shift)
            dl1 = p1 * (_dot(dob, v1, NT2) + shift)
            dl0b, dl1b = dl0.astype(BF16), dl1.astype(BF16)
            grads.append(((_dot(dl0b, k0) + _dot(dl1b, k1)) * scale,
                          _dot(dl0b, qb, TN2), _dot(dl1b, qb, TN2),
                          _dot(p0.astype(BF16), dob, TN2), _dot(p1.astype(BF16), dob, TN2), dl0, dl1))
        db0 = functools.reduce(jnp.add, [gr[5] for gr in grads])
        db1 = functools.reduce(jnp.add, [gr[6] for gr in grads])
        for (rows, before, _), (dq, dk0, dk1, dv0, dv1, _, _) in zip(pairs, grads):
            dq_ref[0, rows, :] = dq
            dk_ref[0, rows, :] += dk0
            dk_ref[0, before, :] += dk1
            dv_ref[0, rows, :] += dv0
            dv_ref[0, before, :] += dv1
        db_ref[0, 0] += db0
        db_ref[0, 1] += db1

    head_out = pl.BlockSpec((1, s, HEAD), lambda j, p: (DIL_GROUP * g + j, 0, 0))
    rows128 = pl.BlockSpec((1, s, 2 * HEAD), lambda j, p: (j, 0, 0))
    full_sh = jax.ShapeDtypeStruct(qn.shape, F32)
    return pl.pallas_call(
        body, name="dil%d_bwd" % g, grid=(DIL_GROUP, steps),
        in_specs=_dil_group_specs(g, s) + [rows128, rows128] + [pl.BlockSpec(memory_space=pl.ANY)] * len(prev),
        out_specs=[head_out, head_out, head_out, pl.BlockSpec((1, 2, QBLK, QBLK), lambda j, p: (j, 0, 0, 0))],
        out_shape=[full_sh, full_sh, full_sh, jax.ShapeDtypeStruct((DIL_GROUP, 2, QBLK, QBLK), F32)],
        input_output_aliases={6 + i: i for i in range(len(prev))},
        compiler_params=_params(("arbitrary", "arbitrary")),
    )(qn, kn, v, bias, ol, dol, *prev)


@functools.partial(jax.custom_vjp, nondiff_argnums=(2,))
def _bdot(a, b, dims):
    return _dot(a.astype(BF16), b.astype(BF16), dims)


def _bdot_fwd(a, b, dims):
    return _bdot(a, b, dims), (a, b)


def _bdot_bwd(dims, res, dc):
    a, b = res
    nn, nt, tn = (NN2, NT2, TN2) if dims in (NN2, NT2, TN2) else (NN3, NT3, TN3)
    if dims == nn:
        return _bdot(dc, b, nt), _bdot(a, dc, tn)
    if dims == nt:
        return _bdot(dc, b, nn), _bdot(dc, a, tn)
    return _bdot(b, dc, nt), _bdot(a, dc, nn)


_bdot.defvjp(_bdot_fwd, _bdot_bwd)


def _ones_dot(ones, x, dims):
    o = ones.astype(BF16)
    hi = x.astype(BF16)
    r1 = x - hi.astype(F32)
    mid = r1.astype(BF16)
    lo = (r1 - mid.astype(F32)).astype(BF16)
    return _dot(o, hi, dims) + _dot(o, mid, dims) + _dot(o, lo, dims)


@jax.custom_vjp
def _prefix_sums(x):
    c = x.shape[1]
    row = lax.broadcasted_iota(jnp.int32, (x.shape[0], c, c), 1)
    col = lax.broadcasted_iota(jnp.int32, (x.shape[0], c, c), 2)
    return _ones_dot((row >= col).astype(F32), x, NN3)


def _prefix_sums_fwd(x):
    return _prefix_sums(x), None


def _prefix_sums_bwd(_, dy):
    c = dy.shape[1]
    row = lax.broadcasted_iota(jnp.int32, (dy.shape[0], c, c), 1)
    col = lax.broadcasted_iota(jnp.int32, (dy.shape[0], c, c), 2)
    return (_ones_dot((row <= col).astype(F32), dy, NN3),)


_prefix_sums.defvjp(_prefix_sums_fwd, _prefix_sums_bwd)


def _rwkv_chunk(s0, r, lw, kraw, v, ag, kk_w, ka_w, rk_w, lng, lnb):
    hb, c, _ = r.shape
    kk = kraw * kk_w
    kk = kk / jnp.maximum(jnp.sqrt(jnp.sum(kk * kk, axis=-1, keepdims=True)), 1e-12)
    k = kraw * (1.0 + (ag - 1.0) * ka_w)
    a = -kk
    b = kk * ag
    row = lax.broadcasted_iota(jnp.int32, (hb, c, c), 1)
    col = lax.broadcasted_iota(jnp.int32, (hb, c, c), 2)
    lower, strict = row >= col, row > col
    cum = _prefix_sums(lw)
    ecum, einv = jnp.exp(cum), jnp.exp(-cum)
    rt, kt, bt = r * ecum, k * einv, b * einv
    at = a * jnp.exp(cum - lw)
    ar = jnp.concatenate([at, rt], axis=1)
    scores = _bdot(ar, jnp.concatenate([bt, kt], axis=1), NT3)
    a_ab = jnp.where(strict, scores[:, :c, :c], 0.0)
    a_ak = jnp.where(strict, scores[:, :c, c:], 0.0)
    p_rb = jnp.where(lower, scores[:, c:, :c], 0.0)
    p_rk = jnp.where(lower, scores[:, c:, c:], 0.0)
    from_s0 = _bdot(ar, s0, NT3)
    rhs = from_s0[:, :c] + _bdot(a_ak, v, NN3)
    inv = (row == col).astype(F32) + a_ab
    pw = a_ab
    for _ in range(int(math.log2(c)) - 1):
        pw = _bdot(pw, pw, NN3)
        inv = inv + _bdot(inv, pw, NN3)
    u = _bdot(inv, rhs, NN3)
    uv = jnp.concatenate([u, v], axis=1)
    y = from_s0[:, c:] + _bdot(jnp.concatenate([p_rb, p_rk], axis=2), uv, NN3)
    cum_end = cum[:, c - 1:c, :]
    dec = jnp.exp(cum_end - cum)
    s_end = s0 * jnp.exp(cum_end) + _bdot(uv, jnp.concatenate([b * dec, k * dec], axis=1), TN3)
    mu = jnp.mean(y, axis=-1, keepdims=True)
    var = jnp.mean(jnp.square(y - mu), axis=-1, keepdims=True)
    z = (y - mu) * lax.rsqrt(var + GN_EPS) * lng + lnb + jnp.sum(r * k * rk_w, axis=-1, keepdims=True) * v
    return z, s_end


def _rwkv_specs(nc, rev):
    cidx = (lambda c: nc - 1 - c) if rev else (lambda c: c)
    seq = pl.BlockSpec((RW_HB, RW_CHUNK, HEAD), lambda hg, c: (hg, cidx(c), 0))
    par = pl.BlockSpec((RW_HB, 1, HEAD), lambda hg, c: (hg, 0, 0))
    st = pl.BlockSpec((1, RW_HB, HEAD, HEAD), lambda hg, c: (cidx(c), hg, 0, 0))
    return seq, par, st


def _rwkv_fwd(seqs, pars):
    s = seqs[0].shape[1]
    nc = s // RW_CHUNK

    def body(*refs):
        seq_refs, par_refs = refs[:5], refs[5:10]
        z_ref, st_ref, state = refs[10:]
        c = pl.program_id(1)

        @pl.when(c == 0)
        def _():
            state[...] = jnp.zeros_like(state)

        s0 = state[...]
        st_ref[0] = s0
        z, s_end = _rwkv_chunk(s0, *[r[...] for r in seq_refs], *[r[...] for r in par_refs])
        z_ref[...] = z
        state[...] = s_end

    seq, par, st = _rwkv_specs(nc, False)
    return pl.pallas_call(
        body, name="rwkv_fwd", grid=(N_HEADS // RW_HB, nc),
        in_specs=[seq] * 5 + [par] * 5, out_specs=[seq, st],
        out_shape=[jax.ShapeDtypeStruct((N_HEADS, s, HEAD), F32), jax.ShapeDtypeStruct((nc, N_HEADS, HEAD, HEAD), F32)],
        scratch_shapes=[pltpu.VMEM((RW_HB, HEAD, HEAD), F32)],
        compiler_params=_params(("arbitrary", "arbitrary")),
    )(*seqs, *pars)


def _rwkv_bwd(seqs, pars, states, dz):
    s = seqs[0].shape[1]
    nc = s // RW_CHUNK

    def body(*refs):
        seq_refs, par_refs = refs[:5], refs[5:10]
        st_ref, dz_ref = refs[10:12]
        dseq_refs, dpar_refs, dstate = refs[12:17], refs[17:22], refs[22]
        c = pl.program_id(1)

        @pl.when(c == 0)
        def _():
            dstate[...] = jnp.zeros_like(dstate)
            for r in dpar_refs:
                r[...] = jnp.zeros_like(r)

        _, vjp = jax.vjp(_rwkv_chunk, st_ref[0], *[r[...] for r in seq_refs], *[r[...] for r in par_refs])
        g = vjp((dz_ref[...], dstate[...]))
        dstate[...] = g[0]
        for r, gs in zip(dseq_refs, g[1:6]):
            r[...] = gs
        for r, gp in zip(dpar_refs, g[6:]):
            r[...] += gp

    seq, par, st = _rwkv_specs(nc, True)
    seq_sh = jax.ShapeDtypeStruct((N_HEADS, s, HEAD), F32)
    par_sh = jax.ShapeDtypeStruct((N_HEADS, 1, HEAD), F32)
    outs = pl.pallas_call(
        body, name="rwkv_bwd", grid=(N_HEADS // RW_HB, nc),
        in_specs=[seq] * 5 + [par] * 5 + [st, seq],
        out_specs=[seq] * 5 + [par] * 5, out_shape=[seq_sh] * 5 + [par_sh] * 5,
        scratch_shapes=[pltpu.VMEM((RW_HB, HEAD, HEAD), F32)],
        compiler_params=_params(("arbitrary", "arbitrary")),
    )(*seqs, *pars, states, dz)
    return list(outs[:5]), list(outs[5:])


def _norm_fn(x, g):
    return (_rms(x, g),)


def _attn_prep_fn(proj, qn_w, kn_w):
    a, b = SB_HEADS, 3 * DIL_GROUP
    return (proj[0:a], proj[a:2 * a], proj[2 * a:3 * a],
            _rms(proj[3 * a:3 * a + b], qn_w), _rms(proj[3 * a + b:3 * a + 2 * b], kn_w), proj[3 * a + 2 * b:])


def _attn_merge_fn(o_sb, ol0, ol1, ol2):
    groups = (ol0, ol1, ol2)
    merged = []
    for j in range(DIL_GROUP):
        lses = [ol[j][:, HEAD:HEAD + 1] for ol in groups]
        m = functools.reduce(jnp.maximum, lses)
        es = [jnp.exp(l - m) for l in lses]
        inv = 1.0 / functools.reduce(jnp.add, es)
        merged.append(functools.reduce(jnp.add, [(e * inv) * ol[j][:, :HEAD] for e, ol in zip(es, groups)]))
    return (jnp.concatenate([_heads_to_nat(o_sb)] + merged, axis=-1),)


def _rw_mix_fn(x, xp, gn, mix, w0, w1, w2, a0, a1, a2, g1, g2):
    h = _rms(x, gn)
    xx = _rms(xp, gn) - h
    xr, xw, xk, xv, xa, xg = [h + xx * mix[i:i + 1] for i in range(6)]
    w_log = -jax.nn.softplus(-(w0 + _mm(jnp.tanh(_mm(xw, w1)), w2))) - 0.5
    lw = -jnp.exp(w_log)
    ag = jax.nn.sigmoid(a0 + _mm(_mm(xa, a1), a2))
    gate = _mm(jax.nn.sigmoid(_mm(xg, g1)), g2)
    return xr, xk, xv, _nat_to_heads(lw), _nat_to_heads(ag), gate


def _rw_gate_fn(z, gate):
    return (_heads_to_nat(z) * gate,)


def _adamw_update(w_ref, m_ref, v_ref, g_ref, go_ref, d_ref, mo_ref, vo_ref):
    g = g_ref[0].astype(F32)
    for j in range(1, N_DEV):
        g = g + g_ref[j].astype(F32)
    mn = ADAM_B1 * m_ref[...] + (1.0 - ADAM_B1) * g
    vn = ADAM_B2 * v_ref[...] + (1.0 - ADAM_B2) * jnp.square(g)
    m_hat = mn / (1.0 - ADAM_B1 ** ADAM_STEP)
    v_hat = vn / (1.0 - ADAM_B2 ** ADAM_STEP)
    go_ref[...] = g
    d_ref[...] = -ADAM_LR * (m_hat / (jnp.sqrt(v_hat) + ADAM_EPS) + ADAM_WD * w_ref[...])
    mo_ref[...] = mn
    vo_ref[...] = vn


def _adamw_many(name, items):
    n = len(items)

    def body(*refs):
        for i in range(n):
            _adamw_update(*refs[4 * i:4 * i + 4], *refs[4 * n + 4 * i:4 * n + 4 * i + 4])

    vmem = pl.BlockSpec(memory_space=pltpu.VMEM)
    outs = pl.pallas_call(
        body, name=name, in_specs=[vmem] * (4 * n), out_specs=[vmem] * (4 * n),
        out_shape=[jax.ShapeDtypeStruct(w.shape, F32) for w, _, _, _ in items for _ in range(4)],
        compiler_params=_params(),
    )(*[a for item in items for a in item])
    return [list(outs[4 * i:4 * i + 4]) for i in range(n)]


def _adamw(name, w, m, v, gparts, row0=0, prev=None):
    big_r, c = w.shape
    r = gparts.shape[1]
    tr = r
    if r % 8 == 0:
        tr = max(t for t in range(8, r + 1, 8) if r % t == 0 and (t * c * 4 <= (1 << 20) or t == 8))
    assert row0 % tr == 0 and (r == big_r or r % 8 == 0)
    off = row0 // tr

    def body(w_ref, m_ref, v_ref, g_ref, *rest):
        _adamw_update(w_ref, m_ref, v_ref, g_ref, *rest[-4:])

    tile = pl.BlockSpec((tr, c), lambda i: (i + off, 0))
    sh = jax.ShapeDtypeStruct((big_r, c), F32)
    prev = list(prev) if prev is not None else []
    return pl.pallas_call(
        body, name=name, grid=(r // tr,),
        in_specs=([tile, tile, tile, pl.BlockSpec((N_DEV, tr, c), lambda i: (0, i, 0))]
                  + [pl.BlockSpec(memory_space=pl.ANY)] * len(prev)),
        out_specs=[tile] * 4, out_shape=[sh] * 4,
        input_output_aliases={4 + j: j for j in range(len(prev))},
        compiler_params=_params(("arbitrary",)),
    )(w, m, v, gparts, *prev)


def _col_blocks_to_nat(g):
    return jnp.moveaxis(g, 0, 1).reshape(g.shape[1], -1)


def _nat_to_col_blocks(a):
    return jnp.moveaxis(a.reshape(a.shape[0], N_DEV, -1), 1, 0)


AG_GROUPS = ("f00", "att", "f01", "f10", "rw", "f11")
RS_GROUPS = ("f11", "rw", "f10", "f01", "f00", "att")
BF16_GRAD_GROUPS = ("att", "f00")
RW_SHARDED = ('rw_mix', 'rw_w0', 'rw_w1', 'rw_w2', 'rw_a0', 'rw_a1', 'rw_a2', 'rw_g1', 'rw_g2', 'rw_kk', 'rw_ka',
              'rw_wr', 'rw_wk', 'rw_wv', 'rw_wo', 'rw_lnx_g', 'rw_lnx_b')


def _step(x, target, rep, get, put):
    tied = lambda a, zero: a + zero[0, 0].astype(a.dtype)
    s, d = x.shape
    tf = min(512, s)
    tt = min(256, s)
    row = lambda a: a.reshape(1, -1)
    mix_norm = rep["mix_norm"]
    ffw = {(0, 0): get("f00", None)}
    ffn_norm = _col_blocks_to_nat(ffw[(0, 0)]["ffn_norm"].reshape(N_DEV, 4, -1))

    acts = {}

    def ffn(nm, xin, l, h):
        g = ffw[(l, h)]
        out, *acts[(l, h)] = _ffn_fwd(nm, xin, ffn_norm[2 * l + h][None], g["gate"], g["up"], g["down"], min(2 * tf, s))
        return out

    x1 = ffn("ffn00_fwd", x, 0, 0)
    att = get("att", x1)
    w_in = _col_blocks_to_nat(att["attn_w_in"])
    w_out = _col_blocks_to_nat(att["attn_w_out"])
    (h0,) = _tile_fwd("mixnorm0_fwd", _norm_fn, [(x1, "nat")], [mix_norm[0:1]], [((s, d), BF16, "nat")], tt)
    proj = _linear_fwd("attn_in_fwd", h0, w_in, tf, out_layout="hm")
    bias = _bias_tiles(rep["rel_bias"], s)
    prep_pars = [rep["attn_q_norm"], rep["attn_k_norm"]]
    sb_sh, dl_sh = (SB_HEADS, s, HEAD), (3 * DIL_GROUP, s, HEAD)
    sq, sk, sv, qn, kn, vd = _tile_fwd("attn_prep_fwd", _attn_prep_fn, [(proj, "hm")], prep_pars,
                                       [(sb_sh, BF16, "hm")] * 3 + [(dl_sh, F32, "hm")] * 3, tt // 2)
    o_sb = _sb_fwd(sq, sk, sv)
    ols = [_dil_group_fwd(g, qn, kn, vd, bias) for g in range(3)]
    merge_tiled = [(o_sb, "hm")] + [(ol, "hm") for ol in ols]
    (merged,) = _tile_fwd("merge_fwd", _attn_merge_fn, merge_tiled, [], [((s, 512), BF16, "nat")], tt)
    x2 = _linear_fwd("attn_out_fwd", merged, w_out, tf, residual=x1)
    ffw[(0, 1)] = get("f01", x2)
    x3 = ffn("ffn01_fwd", x2, 0, 1)
    ffw[(1, 0)] = get("f10", x3)
    x4 = ffn("ffn10_fwd", x3, 1, 0)
    rw = get("rw", x4)
    rw_mix = _col_blocks_to_nat(rw["rw_mix"])
    rw_w1, rw_a1, rw_g1 = (rw[k].reshape(d, -1) for k in ("rw_w1", "rw_a1", "rw_g1"))
    rw_w2, rw_a2, rw_g2 = (_col_blocks_to_nat(rw[k]) for k in ("rw_w2", "rw_a2", "rw_g2"))
    rw_w0, rw_a0 = row(rw["rw_w0"]), row(rw["rw_a0"])
    head_par = lambda a: a.reshape(N_HEADS, 1, HEAD)
    scan_pars = [head_par(rw["rw_kk"]), head_par(rw["rw_ka"]), head_par(rep["rw_rk"]),
                 head_par(rw["rw_lnx_g"]), head_par(rw["rw_lnx_b"])]
    w_rkv = [rw[k].reshape(d, d) for k in ("rw_wr", "rw_wk", "rw_wv")]
    w_o = rw["rw_wo"].reshape(d, d)
    x4p = jnp.pad(x4, ((1, 0), (0, 0)))[:-1]
    mix_tiled = [(x4, "nat"), (x4p, "nat")]
    mix_pars = [mix_norm[1:2], rw_mix, rw_w0, rw_w1, rw_w2, rw_a0, rw_a1, rw_a2, rw_g1, rw_g2]
    hm_sh = (N_HEADS, s, HEAD)
    xr, xk, xv, lw, ag, gate = _tile_fwd(
        "rw_mix_fwd", _rw_mix_fn, mix_tiled, mix_pars,
        [((s, d), BF16, "nat")] * 3 + [(hm_sh, F32, "hm")] * 2 + [((s, d), F32, "nat")], tt)
    r_h, k_h, v_h = [_linear_fwd("rw_%s_fwd" % nm, xi, wi, tf, out_layout="hm")
                     for nm, xi, wi in zip("rkv", (xr, xk, xv), w_rkv)]
    scan_seqs = [r_h, lw, k_h, v_h, ag]
    z, states = _rwkv_fwd(scan_seqs, scan_pars)
    (zg,) = _tile_fwd("rw_gate_fwd", _rw_gate_fn, [(z, "hm"), (gate, "nat")], [], [((s, d), BF16, "nat")], tt)
    x5 = _linear_fwd("rw_out_fwd", zg, w_o, tf, residual=x4)
    ffw[(1, 1)] = get("f11", x5)
    y = ffn("ffn11_fwd", x5, 1, 1)
    dy, loss = _loss_head(y, target, tf)

    G = {}
    dgn = {}

    def fb(nm, group, xin, dout, l, h, zero=None, extra=None):
        g = ffw[(l, h)]
        gn = ffn_norm[2 * l + h][None]
        dxin, dgn[(l, h)], dwg, dwu, dwd = _ffn_bwd(nm, xin, dout, gn if zero is None else tied(gn, zero),
                                                   g["gate"], g["up"], g["down"], *acts[(l, h)], tf)
        shard = {"gate": dwg, "up": dwu, "down": dwd}
        if extra is not None:
            shard.update(extra())
        return dxin, put(group, {}, shard)

    dx5, zero = fb("ffn11_bwd", "f11", x5, dy, 1, 1)
    dzg = _linear_dx("rw_out_dx", dx5, w_o, tf)
    G["rw_wo"] = _linear_dw("rw_out_dw", zg, dx5, tf, 512)
    (dz, dgate), _ = _tile_bwd("rw_gate_bwd", _rw_gate_fn, [(z, "hm"), (gate, "nat")], [], [(dzg, "nat")], tt, [True, True])
    (dr_h, dlw, dk_h, dv_h, dag), dscan = _rwkv_bwd(scan_seqs, [tied(scan_pars[0], zero)] + scan_pars[1:], states, dz)
    drkv = (dr_h, dk_h, dv_h)
    for k, gpar in zip(("rw_kk", "rw_ka", "rw_rk", "rw_lnx_g", "rw_lnx_b"), dscan):
        G[k] = gpar
    dxs = []
    for j, (nm, xi, wi) in enumerate(zip("rkv", (xr, xk, xv), w_rkv)):
        dxs.append(_linear_dx("rw_%s_dx" % nm, drkv[j], wi, tf, dy_layout="hm"))
        G["rw_w" + nm] = _linear_dw("rw_%s_dw" % nm, xi, drkv[j], tf, 512, dy_layout="hm")
    (dx4a, dx4p), dmix = _tile_bwd(
        "rw_mix_bwd", _rw_mix_fn, mix_tiled, mix_pars,
        [(dxs[0], "nat"), (dxs[1], "nat"), (dxs[2], "nat"), (dlw, "hm"), (dag, "hm"), (dgate, "nat")],
        tt, [True, True], adds=[dx5, None])
    d_mixn1 = dmix[0]
    for k, gpar in zip(("rw_mix", "rw_w0", "rw_w1", "rw_w2", "rw_a0", "rw_a1", "rw_a2", "rw_g1", "rw_g2"), dmix[1:]):
        G[k] = gpar
    dx4 = dx4a + jnp.pad(dx4p[1:], ((0, 1), (0, 0)))
    for k in ("rw_mix", "rw_w2", "rw_a2", "rw_g2"):
        G[k] = _nat_to_col_blocks(G[k])
    for k in ("rw_w1", "rw_a1", "rw_g1", "rw_wr", "rw_wk", "rw_wv", "rw_wo"):
        G[k] = G[k].reshape(N_DEV, d // N_DEV, -1)
    for k in ("rw_w0", "rw_a0", "rw_kk", "rw_ka", "rw_lnx_g", "rw_lnx_b"):
        G[k] = G[k].reshape(N_DEV, 1, d // N_DEV)
    zero = put("rw", {"rw_rk": G["rw_rk"].reshape(N_HEADS, HEAD)}, {k: G[k] for k in RW_SHARDED})
    dx3, zero = fb("ffn10_bwd", "f10", x3, dx4, 1, 0, zero)
    dx2, zero = fb("ffn01_bwd", "f01", x2, dx3, 0, 1, zero)
    dmerged = _linear_dx("attn_out_dx", dx2, tied(w_out, zero), tf)
    (do_sb, *dols), _ = _tile_bwd("merge_bwd", _attn_merge_fn, merge_tiled, [], [(dmerged, "nat")], tt, [True] * 4)
    dq_sb, dk_sb, dv_sb = _sb_bwd(sq, sk, sv, do_sb)
    dil_grads, dbias = None, []
    for g in range(3):
        *dil_grads, db = _dil_group_bwd(g, qn, kn, vd, bias, ols[g], dols[g], dil_grads)
        dbias.append(db)
    dqn, dkn, dvd = dil_grads
    dbias = jnp.concatenate(dbias, axis=1)
    (dproj,), (dqn_w, dkn_w) = _tile_bwd(
        "attn_prep_bwd", _attn_prep_fn, [(proj, "hm")], prep_pars,
        [(dq_sb, "hm"), (dk_sb, "hm"), (dv_sb, "hm"), (dqn, "hm"), (dkn, "hm"), (dvd, "hm")], tt // 2, [True])
    dh0 = _linear_dx("attn_in_dx", dproj, w_in, tf, dy_layout="hm")
    (dx1,), (d_mixn0,) = _tile_bwd("mixnorm0_bwd", _norm_fn, [(x1, "nat")], [mix_norm[0:1]], [(dh0, "nat")], tt,
                                   [True], adds=[dx2])
    order = [(0, 0), (0, 1), (1, 0), (1, 1)]
    norm_grads = lambda: {"ffn_norm": _nat_to_col_blocks(jnp.concatenate([dgn[o] for o in order], axis=0))}
    dx0, zero = fb("ffn00_bwd", "f00", x, dx1, 0, 0, extra=norm_grads)
    G["attn_w_out"] = _linear_dw("attn_out_dw", tied(merged, zero), dx2, tf, 512)
    G["attn_w_in"] = _linear_dw("attn_in_dw", tied(h0, zero), dproj, tf, 512, dy_layout="hm")
    rep_grads = {"mix_norm": jnp.concatenate([d_mixn0, d_mixn1], axis=0), "rel_bias": _bias_tiles_bwd(dbias, s),
                 "attn_q_norm": dqn_w, "attn_k_norm": dkn_w}
    zero = put("att", rep_grads, {k: _nat_to_col_blocks(G[k]) for k in ("attn_w_in", "attn_w_out")})
    return loss, dx0, zero


WEIGHTS = ['ffn_norm', 'ffn_w_gate', 'ffn_w_up', 'ffn_w_down', 'mix_norm', 'rel_bias', 'attn_w_in', 'attn_q_norm',
           'attn_k_norm', 'attn_w_out', 'rw_mix', 'rw_w0', 'rw_w1', 'rw_w2', 'rw_a0', 'rw_a1', 'rw_a2', 'rw_g1', 'rw_g2',
           'rw_kk', 'rw_ka', 'rw_rk', 'rw_wr', 'rw_wk', 'rw_wv', 'rw_wo', 'rw_lnx_g', 'rw_lnx_b']
REPLICATED = ('mix_norm', 'rel_bias', 'attn_q_norm', 'attn_k_norm', 'rw_rk')
BF16_WEIGHTS = ('ffn_w_gate', 'ffn_w_up', 'ffn_w_down', 'attn_w_in', 'attn_w_out', 'rw_wr', 'rw_wk', 'rw_wv', 'rw_wo')


def kernel(x, ffn_norm, ffn_w_gate, ffn_w_up, ffn_w_down, mix_norm, rel_bias, attn_w_in, attn_q_norm, attn_k_norm, attn_w_out, rw_mix, rw_w0, rw_w1, rw_w2, rw_a0, rw_a1, rw_a2, rw_g1, rw_g2, rw_kk, rw_ka, rw_rk, rw_wr, rw_wk, rw_wv, rw_wo, rw_lnx_g, rw_lnx_b, loss_target, m_ffn_norm, m_ffn_w_gate, m_ffn_w_up, m_ffn_w_down, m_mix_norm, m_rel_bias, m_attn_w_in, m_attn_q_norm, m_attn_k_norm, m_attn_w_out, m_rw_mix, m_rw_w0, m_rw_w1, m_rw_w2, m_rw_a0, m_rw_a1, m_rw_a2, m_rw_g1, m_rw_g2, m_rw_kk, m_rw_ka, m_rw_rk, m_rw_wr, m_rw_wk, m_rw_wv, m_rw_wo, m_rw_lnx_g, m_rw_lnx_b, v_ffn_norm, v_ffn_w_gate, v_ffn_w_up, v_ffn_w_down, v_mix_norm, v_rel_bias, v_attn_w_in, v_attn_q_norm, v_attn_k_norm, v_attn_w_out, v_rw_mix, v_rw_w0, v_rw_w1, v_rw_w2, v_rw_a0, v_rw_a1, v_rw_a2, v_rw_g1, v_rw_g2, v_rw_kk, v_rw_ka, v_rw_rk, v_rw_wr, v_rw_wk, v_rw_wv, v_rw_wo, v_rw_lnx_g, v_rw_lnx_b):
    args = locals()
    w = {k: args[k] for k in WEIGHTS}
    cast = lambda k, a: a.astype(BF16) if k in BF16_WEIGHTS else a

    sources = {}
    for l, h in ((0, 0), (0, 1), (1, 0), (1, 1)):
        sources["f%d%d" % (l, h)] = {"gate": cast("ffn_w_gate", ffn_w_gate[l, h]), "up": cast("ffn_w_up", ffn_w_up[l, h]),
                                     "down": cast("ffn_w_down", ffn_w_down[l, h])}
    sources["f00"]["ffn_norm"] = ffn_norm
    drop_lead = lambda a: a[0] if a.ndim == 3 else a
    sources["att"] = {k: cast(k, w[k][0]) for k in ("attn_w_in", "attn_w_out")}
    sources["rw"] = {k: cast(k, drop_lead(w[k])) for k in RW_SHARDED}
    ag, token = {}, None
    for group in AG_GROUPS:
        names, arrays = list(sources[group]), list(sources[group].values())
        if token is not None:
            arrays[0] = arrays[0] + token[0, 0].astype(arrays[0].dtype)
        ag[group] = (names, _exchange_start("ag_start_" + group, arrays, []))
        token = ag[group][1]["token"]
    last_ag_token = token

    def get(group, after):
        names, started = ag[group]
        gathered, _ = _exchange_wait("ag_wait_" + group, started, last_ag_token if after is None else after)
        return dict(zip(names, gathered))

    rs = {}

    def put(group, rep_grads, shard_grads):
        if group in BF16_GRAD_GROUPS:
            shard_grads = {k: v.astype(BF16) for k, v in shard_grads.items()}
        started = _exchange_start("rs_start_" + group, list(rep_grads.values()), list(shard_grads.values()))
        rs[group] = (list(rep_grads), list(shard_grads), started)
        return started["token"]

    loss, dx, last_zero = _step(x[0], loss_target[0], {k: w[k] for k in REPLICATED}, get, put)
    loss = lax.psum(loss, MESH_AXES)

    results = {}
    ffn_prev = {}

    def update(k, parts, row0=0, prev=None):
        c = w[k].shape[-1]
        as2d = lambda a: a.reshape(-1, c)
        return _adamw("adamw_%s_%d" % (k, row0), as2d(w[k]), as2d(args["m_" + k]), as2d(args["v_" + k]),
                      parts.reshape(N_DEV, -1, c), row0, prev)

    def update_small(name, named_parts):
        items = []
        for k, parts in named_parts:
            c = w[k].shape[-1]
            items.append((w[k].reshape(-1, c), args["m_" + k].reshape(-1, c), args["v_" + k].reshape(-1, c),
                          parts.reshape(N_DEV, -1, c)))
        for (k, _), res in zip(named_parts, _adamw_many(name, items)):
            results[k] = res

    after = last_zero
    for group in RS_GROUPS:
        rep_names, shard_names, started = rs[group]
        rep_parts, shard_parts = _exchange_wait("rs_wait_" + group, started, after)
        small = []
        for k, parts in list(zip(rep_names, rep_parts)) + list(zip(shard_names, shard_parts)):
            if k in ("gate", "up", "down"):
                full = "ffn_w_" + k
                piece = 2 * int(group[1]) + int(group[2])
                ffn_prev[full] = update(full, parts, piece * parts.shape[1], ffn_prev.get(full))
                results[full] = ffn_prev[full]
                after = results[full][0]
            elif k == "attn_w_in":
                results[k] = update(k, parts)
                after = results[k][0]
            else:
                small.append((k, parts))
        if small:
            update_small("adamw_small_" + group, small)
            after = results[small[0][0]][0]

    outs = [[results[k][j].reshape(w[k].shape) for k in WEIGHTS] for j in range(4)]
    return (loss, dx[None], *outs[0], *outs[1], *outs[2], *outs[3])
```

```python
import functools
import math

import numpy as np
import jax
import jax.numpy as jnp
from jax import lax
from jax.experimental import pallas as pl
from jax.experimental.pallas import tpu as pltpu

F32, BF16 = jnp.float32, jnp.bfloat16

N_DEV = 8
D_MODEL = 1024
HEAD = 64
N_HEADS = 16
SB_HEADS = 4
DIL_GROUP = 4
DIL_PATTERNS = ((128, 1), (512, 4), (2048, 16))
QBLK = 128
N_BUCKETS = 32
MAX_DISTANCE = 2048
NORM_EPS = 1e-6
GN_EPS = 64e-5
NEG_INF = -1e30
RW_CHUNK = 64
RW_HB = 16
ADAM_LR, ADAM_B1, ADAM_B2, ADAM_EPS, ADAM_WD, ADAM_STEP = 0.001, 0.9, 0.999, 1e-08, 0.01, 10
MESH_AXES = ("x", "y", "c")
VMEM_LIMIT_BYTES = 56 * 1024 * 1024

NN2 = (((1,), (0,)), ((), ()))
NT2 = (((1,), (1,)), ((), ()))
TN2 = (((0,), (0,)), ((), ()))
NN3 = (((2,), (1,)), ((0,), (0,)))
NT3 = (((2,), (2,)), ((0,), (0,)))
TN3 = (((1,), (1,)), ((0,), (0,)))


def _dot(a, b, dims=NN2, prec=None):
    return lax.dot_general(a, b, dims, precision=prec, preferred_element_type=F32)


def _params(sem=None):
    return pltpu.CompilerParams(dimension_semantics=sem, vmem_limit_bytes=VMEM_LIMIT_BYTES)


@jax.custom_vjp
def _mm(x, w):
    return _dot(x.astype(BF16), w.astype(BF16))


def _mm_fwd(x, w):
    return _mm(x, w), (x, w)


def _mm_bwd(res, dy):
    x, w = res
    dyb = dy.astype(BF16)
    return (_dot(dyb, w.astype(BF16), NT2).astype(x.dtype), _dot(x.astype(BF16), dyb, TN2).astype(w.dtype))


_mm.defvjp(_mm_fwd, _mm_bwd)


def _rms(x, g):
    return x * lax.rsqrt(jnp.mean(x * x, axis=-1, keepdims=True) + NORM_EPS) * g


def _log_sigmoid(z):
    return jnp.minimum(z, 0.0) - jnp.log(1.0 + jnp.exp(-jnp.abs(z)))


def _heads_to_nat(v3):
    return jnp.concatenate([v3[h] for h in range(v3.shape[0])], axis=-1)


def _nat_to_heads(v2):
    return jnp.stack([v2[:, h * HEAD:(h + 1) * HEAD] for h in range(v2.shape[1] // HEAD)], axis=0)


def _exchange(name, gathers, scatters):
    n_g = len(gathers)
    arrays = list(gathers) + list(scatters)
    n = len(arrays)
    out_shape = [jax.ShapeDtypeStruct((N_DEV,) + a.shape, a.dtype) for a in gathers]
    out_shape += [jax.ShapeDtypeStruct(a.shape, a.dtype) for a in scatters]

    def body(*refs):
        ins, outs = refs[:n], refs[n:2 * n]
        send_sems, recv_sems, local_sems = refs[2 * n:]
        x, y, c = lax.axis_index("x"), lax.axis_index("y"), lax.axis_index("c")
        me = 4 * x + 2 * y + c

        def src(i, idx):
            return ins[i] if i < n_g else ins[i].at[idx]

        local = [pltpu.make_async_copy(src(i, me), outs[i].at[me], local_sems.at[i]) for i in range(n)]
        for cp in local:
            cp.start()
        remote = []
        for m in range(1, N_DEV):
            px, py, pc = x ^ ((m >> 2) & 1), y ^ ((m >> 1) & 1), c ^ (m & 1)
            peer = 4 * px + 2 * py + pc
            for i in range(n):
                cp = pltpu.make_async_remote_copy(
                    src_ref=src(i, peer), dst_ref=outs[i].at[me],
                    send_sem=send_sems.at[i, m - 1], recv_sem=recv_sems.at[i, m - 1],
                    device_id=(px, py, pc), device_id_type=pl.DeviceIdType.MESH)
                cp.start()
                arrival = pltpu.make_async_remote_copy(
                    src_ref=src(i, peer), dst_ref=outs[i].at[peer],
                    send_sem=send_sems.at[i, m - 1], recv_sem=recv_sems.at[i, m - 1],
                    device_id=(px, py, pc), device_id_type=pl.DeviceIdType.MESH)
                remote.append((cp, arrival))
        for cp, arrival in remote:
            cp.wait_send()
            arrival.wait_recv()
        for cp in local:
            cp.wait()

    hbm = pl.BlockSpec(memory_space=pltpu.HBM)
    outs = pl.pallas_call(
        body, name=name, out_shape=out_shape,
        in_specs=[hbm] * n, out_specs=[hbm] * n,
        scratch_shapes=[pltpu.SemaphoreType.DMA((n, N_DEV - 1)), pltpu.SemaphoreType.DMA((n, N_DEV - 1)),
                        pltpu.SemaphoreType.DMA((n,))],
    )(*arrays)
    return list(outs[:n_g]), list(outs[n_g:])


def _mesh_peers():
    x, y, c = lax.axis_index("x"), lax.axis_index("y"), lax.axis_index("c")
    peers = []
    for m in range(1, N_DEV):
        px, py, pc = x ^ ((m >> 2) & 1), y ^ ((m >> 1) & 1), c ^ (m & 1)
        peers.append((m, (px, py, pc), 4 * px + 2 * py + pc))
    return 4 * x + 2 * y + c, peers


_HBM_SPEC = pl.BlockSpec(memory_space=pltpu.HBM)
_SEM_SPEC = pl.BlockSpec(memory_space=pltpu.SEMAPHORE)
_DATAFLOW = pltpu.SideEffectType.DATAFLOW_SIDE_EFFECTING


def _exchange_start(name, gathers, scatters):
    n_g = len(gathers)
    arrays = list(gathers) + list(scatters)
    n = len(arrays)
    lands = ([lax.empty((N_DEV,) + a.shape, a.dtype) for a in gathers] + [lax.empty(a.shape, a.dtype) for a in scatters])

    def body(*refs):
        ins, land = refs[:n], refs[n:2 * n]
        send_sems, recv_sems, local_sems, token = refs[2 * n], refs[2 * n + 1], refs[2 * n + 2], refs[-1]
        me, peers = _mesh_peers()
        for m, dev, peer in peers:
            for i in range(n):
                k = i * (N_DEV - 1) + m - 1
                pltpu.make_async_remote_copy(
                    src_ref=ins[i] if i < n_g else ins[i].at[peer], dst_ref=land[i].at[me],
                    send_sem=send_sems.at[k], recv_sem=recv_sems.at[k],
                    device_id=dev, device_id_type=pl.DeviceIdType.MESH).start()
        for i in range(n):
            pltpu.make_async_copy(ins[i] if i < n_g else ins[i].at[me], land[i].at[me], local_sems.at[i]).start()
        token[...] = jnp.zeros_like(token)

    sem = pltpu.SemaphoreType.DMA((n * (N_DEV - 1),))
    outs = pl.pallas_call(
        body, name=name,
        out_shape=([sem, sem, pltpu.SemaphoreType.DMA((n,))] + [pltpu.HBM(a.shape, a.dtype) for a in arrays]
                   + [pltpu.HBM(l.shape, l.dtype) for l in lands] + [jax.ShapeDtypeStruct((8, 128), F32)]),
        in_specs=[_HBM_SPEC] * (2 * n),
        out_specs=[_SEM_SPEC] * 3 + [_HBM_SPEC] * (2 * n) + [pl.BlockSpec(memory_space=pltpu.VMEM)],
        input_output_aliases={i: i + 3 for i in range(2 * n)},
        compiler_params=pltpu.CompilerParams(has_side_effects=_DATAFLOW),
    )(*[pltpu.with_memory_space_constraint(a, pltpu.HBM) for a in arrays],
      *[pltpu.with_memory_space_constraint(l, pltpu.HBM) for l in lands])
    return dict(n_g=n_g, n=n, send=outs[0], recv=outs[1], local=outs[2], srcs=list(outs[3:3 + n]),
                lands=list(outs[3 + n:3 + 2 * n]), token=outs[-1])


def _exchange_wait(name, started, after):
    n, n_g = started["n"], started["n_g"]

    def body(*refs):
        srcs, lands = refs[:n], refs[n:2 * n]
        send_sems, recv_sems, local_sems = refs[2 * n], refs[2 * n + 1], refs[2 * n + 2]
        me, peers = _mesh_peers()
        local = [pltpu.make_async_copy(srcs[i] if i < n_g else srcs[i].at[me], lands[i].at[me], local_sems.at[i])
                 for i in range(n)]
        for m, dev, peer in peers:
            for i in range(n):
                k = i * (N_DEV - 1) + m - 1
                cp = pltpu.make_async_remote_copy(
                    src_ref=srcs[i] if i < n_g else srcs[i].at[peer], dst_ref=lands[i].at[peer],
                    send_sem=send_sems.at[k], recv_sem=recv_sems.at[k],
                    device_id=dev, device_id_type=pl.DeviceIdType.MESH)
                cp.wait_send()
                cp.wait_recv()
        for cp in local:
            cp.wait()

    outs = pl.pallas_call(
        body, name=name,
        out_shape=([pltpu.HBM(a.shape, a.dtype) for a in started["srcs"]]
                   + [pltpu.HBM(l.shape, l.dtype) for l in started["lands"]]),
        in_specs=[_HBM_SPEC] * (2 * n) + [_SEM_SPEC] * 3 + [pl.BlockSpec(memory_space=pl.ANY)],
        out_specs=[_HBM_SPEC] * (2 * n), input_output_aliases={i: i for i in range(2 * n)},
        compiler_params=pltpu.CompilerParams(has_side_effects=_DATAFLOW),
    )(*started["srcs"], *started["lands"], started["send"], started["recv"], started["local"], after)
    return list(outs[n:n + n_g]), list(outs[n + n_g:])


def _tile_spec(shape, layout, t):
    if layout == "nat":
        return pl.BlockSpec((t, shape[1]), lambda i: (i, 0))
    return pl.BlockSpec((shape[0], t, shape[2]), lambda i: (0, i, 0))


def _full_spec(shape):
    nd = len(shape)
    return pl.BlockSpec(tuple(shape), lambda i: (0,) * nd)


def _seq_len(a, layout):
    return a.shape[0] if layout == "nat" else a.shape[1]


def _tile_fwd(name, f, tiled, params, outs, t):
    nt, npar = len(tiled), len(params)
    s = _seq_len(*tiled[0])

    def body(*refs):
        vals = [r[...] for r in refs[:nt + npar]]
        res = f(*vals)
        for r, o in zip(refs[nt + npar:], res):
            r[...] = o.astype(r.dtype)

    return pl.pallas_call(
        body, name=name, grid=(s // t,),
        in_specs=[_tile_spec(a.shape, l, t) for a, l in tiled] + [_full_spec(p.shape) for p in params],
        out_specs=[_tile_spec(sh, l, t) for sh, _, l in outs],
        out_shape=[jax.ShapeDtypeStruct(sh, dt) for sh, dt, _ in outs],
        compiler_params=_params(("arbitrary",)),
    )(*[a for a, _ in tiled], *params)


def _tile_bwd(name, f, tiled, params, cts, t, need, adds=None):
    nt, npar, nc = len(tiled), len(params), len(cts)
    s = _seq_len(*tiled[0])
    need_idx = [k for k in range(nt) if need[k]]
    adds = adds or [None] * len(need_idx)
    add_arrays = [(a, tiled[k][1]) for a, k in zip(adds, need_idx) if a is not None]
    n_add = len(add_arrays)

    def body(*refs):
        i = pl.program_id(0)
        vals = [r[...] for r in refs[:nt + npar]]
        ct_refs = refs[nt + npar:nt + npar + nc]
        add_refs = refs[nt + npar + nc:nt + npar + nc + n_add]
        out_refs = refs[nt + npar + nc + n_add:]
        res, vjp = jax.vjp(f, *vals)
        grads = vjp(tuple(r[...].astype(o.dtype) for r, o in zip(ct_refs, res)))
        a = 0
        for j, k in enumerate(need_idx):
            g = grads[k]
            if adds[j] is not None:
                g = g + add_refs[a][...]
                a += 1
            out_refs[j][...] = g.astype(out_refs[j].dtype)
        for j in range(npar):
            r = out_refs[len(need_idx) + j]

            @pl.when(i == 0)
            def _():
                r[...] = jnp.zeros_like(r)

            r[...] += grads[nt + j]

    outs = pl.pallas_call(
        body, name=name, grid=(s // t,),
        in_specs=([_tile_spec(a.shape, l, t) for a, l in tiled] + [_full_spec(p.shape) for p in params]
                  + [_tile_spec(a.shape, l, t) for a, l in cts] + [_tile_spec(a.shape, l, t) for a, l in add_arrays]),
        out_specs=([_tile_spec(tiled[k][0].shape, tiled[k][1], t) for k in need_idx]
                   + [_full_spec(p.shape) for p in params]),
        out_shape=([jax.ShapeDtypeStruct(tiled[k][0].shape, F32) for k in need_idx]
                   + [jax.ShapeDtypeStruct(p.shape, F32) for p in params]),
        compiler_params=_params(("arbitrary",)),
    )(*[a for a, _ in tiled], *params, *[a for a, _ in cts], *[a for a, _ in add_arrays])
    return list(outs[:len(need_idx)]), list(outs[len(need_idx):])


def _linear_fwd(name, x, w, t, out_layout="nat", residual=None):
    s, k = x.shape
    n = w.shape[1]
    has_res = residual is not None

    def body(*refs):
        x_ref, w_ref = refs[0], refs[1]
        o_ref = refs[-1]
        y = _dot(x_ref[...].astype(BF16), w_ref[...])
        if has_res:
            y = y + refs[2][...]
        if out_layout == "hm":
            for h in range(n // HEAD):
                o_ref[h] = y[:, h * HEAD:(h + 1) * HEAD]
        else:
            o_ref[...] = y

    out_sh = (s, n) if out_layout == "nat" else (n // HEAD, s, HEAD)
    ins = [x, w] + ([residual] if has_res else [])
    in_specs = [_tile_spec(x.shape, "nat", t), _full_spec(w.shape)] + ([_tile_spec((s, n), "nat", t)] if has_res else [])
    return pl.pallas_call(
        body, name=name, grid=(s // t,), in_specs=in_specs,
        out_specs=_tile_spec(out_sh, out_layout, t), out_shape=jax.ShapeDtypeStruct(out_sh, F32),
        compiler_params=_params(("arbitrary",)),
    )(*ins)


def _linear_dx(name, dy, w, t, dy_layout="nat"):
    k, n = w.shape
    s = _seq_len(dy, dy_layout)

    def body(dy_ref, w_ref, o_ref):
        dy = _heads_to_nat(dy_ref[...].astype(BF16)) if dy_layout == "hm" else dy_ref[...].astype(BF16)
        o_ref[...] = _dot(dy, w_ref[...], NT2)

    return pl.pallas_call(
        body, name=name, grid=(s // t,),
        in_specs=[_tile_spec(dy.shape, dy_layout, t), _full_spec(w.shape)],
        out_specs=_tile_spec((s, k), "nat", t), out_shape=jax.ShapeDtypeStruct((s, k), F32),
        compiler_params=_params(("arbitrary",)),
    )(dy, w)


def _linear_dw(name, x, dy, t, nb, dy_layout="nat"):
    s, k = x.shape
    n = dy.shape[1] if dy_layout == "nat" else dy.shape[0] * HEAD

    def body(x_ref, dy_ref, o_ref):
        i = pl.program_id(1)

        @pl.when(i == 0)
        def _():
            o_ref[...] = jnp.zeros_like(o_ref)

        dy = _heads_to_nat(dy_ref[...].astype(BF16)) if dy_layout == "hm" else dy_ref[...].astype(BF16)
        o_ref[...] += _dot(x_ref[...].astype(BF16), dy, TN2)

    if dy_layout == "hm":
        dy_spec = pl.BlockSpec((nb // HEAD, t, HEAD), lambda j, i: (j, i, 0))
    else:
        dy_spec = pl.BlockSpec((t, nb), lambda j, i: (i, j))
    return pl.pallas_call(
        body, name=name, grid=(n // nb, s // t),
        in_specs=[pl.BlockSpec((t, k), lambda j, i: (i, 0)), dy_spec],
        out_specs=pl.BlockSpec((k, nb), lambda j, i: (0, j)), out_shape=jax.ShapeDtypeStruct((k, n), F32),
        compiler_params=_params(("arbitrary", "arbitrary")),
    )(x, dy)


def _ffn_fwd(name, x, gn, wg, wu, wd, t):
    s, d = x.shape
    f8 = wg.shape[-1]

    def body(x_ref, g_ref, wg_ref, wu_ref, wd_ref, o_ref, gk_ref, uk_ref, h_scr, acc):
        k = pl.program_id(1)

        @pl.when(k == 0)
        def _():
            h_scr[...] = _rms(x_ref[...], g_ref[...]).astype(BF16)
            acc[...] = jnp.zeros_like(acc)

        hb = h_scr[...]
        gk = _dot(hb, wg_ref[0])
        uk = _dot(hb, wu_ref[0])
        gk_ref[0] = gk
        uk_ref[0] = uk
        a = gk * jax.nn.sigmoid(gk) * uk
        acc[...] += _dot(a.astype(BF16), wd_ref[0])

        @pl.when(k == N_DEV - 1)
        def _():
            o_ref[...] = x_ref[...] + 0.5 * acc[...]

    wspec = lambda shp: pl.BlockSpec((1,) + shp, lambda i, k: (k, 0, 0))
    act = pl.BlockSpec((1, t, f8), lambda i, k: (k, i, 0))
    act_sh = jax.ShapeDtypeStruct((N_DEV, s, f8), F32)
    return pl.pallas_call(
        body, name=name, grid=(s // t, N_DEV),
        in_specs=[pl.BlockSpec((t, d), lambda i, k: (i, 0)), pl.BlockSpec((1, d), lambda i, k: (0, 0)),
                  wspec((d, f8)), wspec((d, f8)), wspec((f8, d))],
        out_specs=[pl.BlockSpec((t, d), lambda i, k: (i, 0)), act, act],
        out_shape=[jax.ShapeDtypeStruct((s, d), F32), act_sh, act_sh],
        scratch_shapes=[pltpu.VMEM((t, d), BF16), pltpu.VMEM((t, d), F32)],
        compiler_params=_params(("arbitrary", "arbitrary")),
    )(x, gn, wg, wu, wd)


def _ffn_bwd(name, x, dy, gn, wg, wu, wd, gact, uact, t):
    s, d = x.shape
    f8 = wg.shape[-1]
    last = N_DEV - 1

    def body(x_ref, dy_ref, g_ref, wg_ref, wu_ref, wd_ref, gk_ref, uk_ref,
             dx_ref, dg_ref, dwg_ref, dwu_ref, dwd_ref, dh_scr):
        k, i = pl.program_id(0), pl.program_id(1)
        x = x_ref[...]
        rs = lax.rsqrt(jnp.mean(x * x, axis=-1, keepdims=True) + NORM_EPS)
        xn = x * rs
        hb = (xn * g_ref[...]).astype(BF16)
        dob = (0.5 * dy_ref[...]).astype(BF16)
        wgk, wuk, wdk = wg_ref[0], wu_ref[0], wd_ref[0]
        gk, uk = gk_ref[0], uk_ref[0]
        sg = jax.nn.sigmoid(gk)
        sk = gk * sg
        da = _dot(dob, wdk, NT2)
        du = (da * sk).astype(BF16)
        dg = (da * uk * (sg * (1.0 + gk * (1.0 - sg)))).astype(BF16)

        dwd_c = _dot((sk * uk).astype(BF16), dob, TN2)
        dwg_c = _dot(hb, dg, TN2)
        dwu_c = _dot(hb, du, TN2)
        dh = _dot(dg, wgk, NT2) + _dot(du, wuk, NT2)
        rows = pl.ds(pl.multiple_of(i * t, t), t)

        @pl.when(i == 0)
        def _():
            dwg_ref[0], dwu_ref[0], dwd_ref[0] = dwg_c, dwu_c, dwd_c

        @pl.when(i > 0)
        def _():
            dwg_ref[0] += dwg_c
            dwu_ref[0] += dwu_c
            dwd_ref[0] += dwd_c

        @pl.when(k == 0)
        def _():
            dh_scr[rows, :] = dh

        @pl.when(k > 0)
        def _():
            dh_scr[rows, :] += dh

        @pl.when(jnp.logical_and(k == last, i == 0))
        def _():
            dg_ref[...] = jnp.zeros_like(dg_ref)

        @pl.when(k == last)
        def _():
            dht = dh_scr[rows, :]
            dg_ref[...] += jnp.sum(dht * xn, axis=0, keepdims=True)
            dxn = dht * g_ref[...]
            dx_ref[...] = dy_ref[...] + rs * (dxn - xn * jnp.mean(dxn * xn, axis=-1, keepdims=True))

    wspec = lambda shp: pl.BlockSpec((1,) + shp, lambda k, i: (k, 0, 0))
    tile = pl.BlockSpec((t, d), lambda k, i: (i, 0))
    act = pl.BlockSpec((1, t, f8), lambda k, i: (k, i, 0))
    return pl.pallas_call(
        body, name=name, grid=(N_DEV, s // t),
        in_specs=[tile, tile, pl.BlockSpec((1, d), lambda k, i: (0, 0)), wspec((d, f8)), wspec((d, f8)), wspec((f8, d)),
                  act, act],
        out_specs=[pl.BlockSpec((t, d), lambda k, i: (jnp.where(k == last, i, 0), 0)),
                   pl.BlockSpec((1, d), lambda k, i: (0, 0)),
                   pl.BlockSpec((1, d, f8), lambda k, i: (k, 0, 0)), pl.BlockSpec((1, d, f8), lambda k, i: (k, 0, 0)),
                   pl.BlockSpec((1, f8, d), lambda k, i: (k, 0, 0))],
        out_shape=[jax.ShapeDtypeStruct((s, d), F32), jax.ShapeDtypeStruct((1, d), F32),
                   jax.ShapeDtypeStruct((N_DEV, d, f8), F32), jax.ShapeDtypeStruct((N_DEV, d, f8), F32),
                   jax.ShapeDtypeStruct((N_DEV, f8, d), F32)],
        scratch_shapes=[pltpu.VMEM((s, d), F32)],
        compiler_params=_params(("arbitrary", "arbitrary")),
    )(x, dy, gn, wg, wu, wd, gact, uact)


def _loss_head(y, target, t):
    s, d = y.shape

    def body(y_ref, t_ref, dy_ref, l_ref):
        i = pl.program_id(0)
        err = y_ref[...] - t_ref[...]
        dy_ref[...] = err * (1.0 / d)

        @pl.when(i == 0)
        def _():
            l_ref[...] = jnp.zeros_like(l_ref)

        l_ref[...] += 0.5 * jnp.sum(jnp.mean(err * err, axis=-1, keepdims=True), axis=0, keepdims=True)

    tile = pl.BlockSpec((t, d), lambda i: (i, 0))
    dy, l = pl.pallas_call(
        body, name="loss_head", grid=(s // t,), in_specs=[tile, tile],
        out_specs=[tile, pl.BlockSpec((1, 1), lambda i: (0, 0))],
        out_shape=[jax.ShapeDtypeStruct((s, d), F32), jax.ShapeDtypeStruct((1, 1), F32)],
        compiler_params=_params(("arbitrary",)),
    )(y, target)
    return dy, l[0, 0]


SB_KEY_TILE = 1024
SB_HEADS_PER_STEP = 4


def _sb_scan_mats():
    row = lax.broadcasted_iota(jnp.int32, (QBLK, QBLK), 0)
    col = lax.broadcasted_iota(jnp.int32, (QBLK, QBLK), 1)
    return (row > col).astype(F32).astype(BF16), (row < col).astype(F32).astype(BF16)


def _sb_tile_scan(x, mat, reverse):
    nsub = x.shape[1] // QBLK
    outs, carry = [None] * nsub, jnp.zeros((x.shape[0], 1), F32)
    for i in (reversed(range(nsub)) if reverse else range(nsub)):
        xs = x[:, i * QBLK:(i + 1) * QBLK]
        hi = xs.astype(BF16)
        lo = (xs - hi.astype(F32)).astype(BF16)
        outs[i] = _dot(hi, mat) + _dot(lo, mat) + carry
        carry = carry + jnp.sum(xs, axis=1, keepdims=True)
    return jnp.concatenate(outs, axis=1), carry


def _sb_before_query(n, t, kt):
    row = lax.broadcasted_iota(jnp.int32, (QBLK, kt), 0)
    col = lax.broadcasted_iota(jnp.int32, (QBLK, kt), 1)
    return t * kt + col < n * QBLK + row


def _sb_fwd(q, k, v):
    _, s, _ = q.shape
    scale = HEAD ** -0.5
    kt = min(SB_KEY_TILE, s)

    def body(q_ref, k_ref, v_ref, o_ref):
        n = pl.program_id(1)
        suffix, _ = _sb_scan_mats()
        n_tiles = lax.div(n, jnp.int32(kt // QBLK)) + 1
        heads = range(SB_HEADS_PER_STEP)
        qb = [(q_ref[h] * scale).astype(q_ref.dtype) for h in heads]

        def tile(t, carry, diagonal):
            rows = pl.ds(pl.multiple_of(t * kt, kt), kt)
            out = []
            for h in heads:
                c, acc = carry[h]
                z = _dot(qb[h], k_ref[h, rows, :], NT2)
                lk = _log_sigmoid(-z)
                log_beta = z + lk
                if diagonal:
                    ok = _sb_before_query(n, t, kt)
                    lk = jnp.where(ok, lk, 0.0)
                later, total = _sb_tile_scan(lk, suffix, True)
                w = jnp.exp(log_beta + later + c)
                if diagonal:
                    w = jnp.where(ok, w, 0.0)
                out.append((c + total, acc + _dot(w.astype(BF16), v_ref[h, rows, :])))
            return tuple(out)

        zero = (jnp.zeros((QBLK, 1), F32), jnp.zeros((QBLK, HEAD), F32))
        carry = tile(n_tiles - 1, (zero,) * len(heads), True)
        carry = lax.fori_loop(1, n_tiles, lambda jj, cr: tile(n_tiles - 1 - jj, cr, False), carry)
        for h in heads:
            o_ref[h] = carry[h][1]

    hp = SB_HEADS_PER_STEP
    return pl.pallas_call(
        body, name="sb_fwd", grid=(SB_HEADS // hp, s // QBLK),
        in_specs=[pl.BlockSpec((hp, QBLK, HEAD), lambda h, n: (h, n, 0)),
                  pl.BlockSpec((hp, s, HEAD), lambda h, n: (h, 0, 0)),
                  pl.BlockSpec((hp, s, HEAD), lambda h, n: (h, 0, 0))],
        out_specs=pl.BlockSpec((hp, QBLK, HEAD), lambda h, n: (h, n, 0)),
        out_shape=jax.ShapeDtypeStruct((SB_HEADS, s, HEAD), F32),
        compiler_params=_params(("arbitrary", "arbitrary")),
    )(q, k, v)


def _sb_bwd(q, k, v, do):
    _, s, _ = q.shape
    scale = HEAD ** -0.5
    kt = min(SB_KEY_TILE, s)

    def body(q_ref, k_ref, v_ref, do_ref, dq_ref, dk_ref, dv_ref, e_scr, beta_scr):
        n = pl.program_id(1)

        @pl.when(n == 0)
        def _():
            dk_ref[...] = jnp.zeros_like(dk_ref)
            dv_ref[...] = jnp.zeros_like(dv_ref)

        suffix, prefix = _sb_scan_mats()
        n_tiles = lax.div(n, jnp.int32(kt // QBLK)) + 1
        heads = range(SB_HEADS_PER_STEP)
        qb = [(q_ref[h] * scale).astype(q_ref.dtype) for h in heads]
        dob = [do_ref[h].astype(BF16) for h in heads]

        def weights(t, cs, diagonal):
            rows = pl.ds(pl.multiple_of(t * kt, kt), kt)
            out, stores = [], []
            for h in heads:
                vb = v_ref[h, rows, :]
                z = _dot(qb[h], k_ref[h, rows, :], NT2)
                lk = _log_sigmoid(-z)
                log_beta = z + lk
                if diagonal:
                    ok = _sb_before_query(n, t, kt)
                    lk = jnp.where(ok, lk, 0.0)
                later, total = _sb_tile_scan(lk, suffix, True)
                w = jnp.exp(log_beta + later + cs[h])
                if diagonal:
                    w = jnp.where(ok, w, 0.0)
                stores.append((w * _dot(dob[h], vb, NT2), jnp.exp(log_beta), _dot(w.astype(BF16), dob[h], TN2)))
                out.append(cs[h] + total)
            for h in heads:
                e_scr[h, t], beta_scr[h, t] = stores[h][0], stores[h][1]
                dv_ref[h, rows, :] += stores[h][2]
            return tuple(out)

        col0 = jnp.zeros((QBLK, 1), F32)
        cs = weights(n_tiles - 1, (col0,) * len(heads), True)
        lax.fori_loop(1, n_tiles, lambda jj, c: weights(n_tiles - 1 - jj, c, False), cs)

        def grads(t, carry, diagonal):
            rows = pl.ds(pl.multiple_of(t * kt, kt), kt)
            out, dks = [], []
            for h in heads:
                pc, dq = carry[h]
                kb = k_ref[h, rows, :]
                e, beta = e_scr[h, t], beta_scr[h, t]
                before, total = _sb_tile_scan(e, prefix, False)
                dz = e * (1.0 - beta) - beta * (before + pc)
                if diagonal:
                    dz = jnp.where(_sb_before_query(n, t, kt), dz, 0.0)
                dz = dz.astype(BF16)
                dks.append(_dot(dz, qb[h], TN2))
                out.append((pc + total, dq + _dot(dz, kb)))
            for h in heads:
                dk_ref[h, rows, :] += dks[h]
            return tuple(out)

        zero = (col0, jnp.zeros((QBLK, HEAD), F32))
        carry = lax.fori_loop(0, n_tiles - 1, lambda t, cr: grads(t, cr, False), (zero,) * len(heads))
        carry = grads(n_tiles - 1, carry, True)
        for h in heads:
            dq_ref[h] = carry[h][1] * scale

    hp = SB_HEADS_PER_STEP
    qspec = pl.BlockSpec((hp, QBLK, HEAD), lambda h, n: (h, n, 0))
    full = pl.BlockSpec((hp, s, HEAD), lambda h, n: (h, 0, 0))
    sh = jax.ShapeDtypeStruct((SB_HEADS, s, HEAD), F32)
    tiles_sh = (hp, s // kt, QBLK, kt)
    return pl.pallas_call(
        body, name="sb_bwd", grid=(SB_HEADS // hp, s // QBLK),
        in_specs=[qspec, full, full, qspec],
        out_specs=[qspec, full, full], out_shape=[sh, sh, sh],
        scratch_shapes=[pltpu.VMEM(tiles_sh, F32), pltpu.VMEM(tiles_sh, F32)],
        compiler_params=_params(("arbitrary", "arbitrary")),
    )(q, k, v, do)


def _t5_bucket_np(dist):
    max_exact = N_BUCKETS // 2
    d = np.maximum(dist, 1).astype(np.float32)
    large = max_exact + (np.log(d / np.float32(max_exact)) / np.float32(math.log(MAX_DISTANCE / max_exact))
                         * np.float32(N_BUCKETS - max_exact)).astype(np.int32)
    large = np.minimum(large, N_BUCKETS - 1)
    return np.where(dist < max_exact, dist, large)


def _dil_layout(s):
    assert all(s % (QBLK * r) == 0 and window // r == QBLK for window, r in DIL_PATTERNS)
    tiles, buckets = [], []
    i = np.arange(QBLK)[:, None]
    j = np.arange(QBLK)[None, :]
    for g, (window, r) in enumerate(DIL_PATTERNS):
        for off in (0, 1):
            dist = QBLK * off + i - j
            ok = (dist >= 0) & (dist <= window // r)
            tiles.append((g, off))
            buckets.append(np.where(ok, _t5_bucket_np(np.maximum(dist, 0) * r), -1).astype(np.int32))
    return tiles, np.stack(buckets)


def _bias_tiles(rel_bias, s):
    tiles, buckets = _dil_layout(s)
    nt = len(tiles)
    present = [sorted(set(np.unique(buckets[k]).tolist()) - {-1}) for k in range(nt)]

    def body(rel_ref, b_ref, o_ref):
        j = pl.program_id(0)
        for k, (g, _) in enumerate(tiles):
            bk = b_ref[k]
            tile = jnp.full((QBLK, QBLK), NEG_INF, F32)
            for b in present[k]:
                tile = jnp.where(bk == b, rel_ref[b, g * DIL_GROUP + j], tile)
            o_ref[0, k] = tile

    return pl.pallas_call(
        body, name="bias_tiles", grid=(DIL_GROUP,),
        in_specs=[pl.BlockSpec(memory_space=pltpu.SMEM), pl.BlockSpec((nt, QBLK, QBLK), lambda j: (0, 0, 0))],
        out_specs=pl.BlockSpec((1, nt, QBLK, QBLK), lambda j: (j, 0, 0, 0)),
        out_shape=jax.ShapeDtypeStruct((DIL_GROUP, nt, QBLK, QBLK), F32),
        compiler_params=_params(("arbitrary",)),
    )(rel_bias, jnp.asarray(buckets))


def _bias_tiles_bwd(dbias, s):
    tiles, buckets = _dil_layout(s)
    nt = len(tiles)
    present = [sorted(set(np.unique(buckets[k]).tolist()) - {-1}) for k in range(nt)]

    def body(d_ref, b_ref, o_ref):
        j = pl.program_id(0)

        @pl.when(j == 0)
        def _():
            for b in range(N_BUCKETS):
                for col in range(3 * DIL_GROUP):
                    o_ref[b, col] = jnp.float32(0.0)

        for k, (g, _) in enumerate(tiles):
            bk, dk = b_ref[k], d_ref[0, k]
            for b in present[k]:
                o_ref[b, g * DIL_GROUP + j] += jnp.sum(jnp.where(bk == b, dk, 0.0))

    return pl.pallas_call(
        body, name="bias_tiles_bwd", grid=(DIL_GROUP,),
        in_specs=[pl.BlockSpec((1, nt, QBLK, QBLK), lambda j: (j, 0, 0, 0)),
                  pl.BlockSpec((nt, QBLK, QBLK), lambda j: (0, 0, 0))],
        out_specs=pl.BlockSpec(memory_space=pltpu.SMEM),
        out_shape=jax.ShapeDtypeStruct((N_BUCKETS, 3 * DIL_GROUP), F32),
        compiler_params=_params(("arbitrary",)),
    )(dbias, jnp.asarray(buckets))


DIL_PAIRS_PER_STEP = 8


def _dil_rows(g, s, pair):
    _, r = DIL_PATTERNS[g]
    nb = s // (QBLK * r)
    c, n = lax.div(pair, jnp.int32(nb)), lax.rem(pair, jnp.int32(nb))
    start = c + (r * QBLK) * n
    before = jnp.where(n > 0, start - r * QBLK, start)
    if r == 1:
        return pl.ds(start, QBLK), pl.ds(before, QBLK), n > 0
    return pl.ds(start, QBLK, stride=r), pl.ds(before, QBLK, stride=r), n > 0


def _dil_logits(qb, k_ref, rows, before, has_before, b_ref):
    k0, k1 = k_ref[0, rows, :].astype(BF16), k_ref[0, before, :].astype(BF16)
    l0 = _dot(qb, k0, NT2) + b_ref[0, 0]
    l1 = jnp.where(has_before, _dot(qb, k1, NT2) + b_ref[0, 1], NEG_INF)
    return k0, k1, l0, l1


def _dil_group_specs(g, s):
    head = pl.BlockSpec((1, s, HEAD), lambda j, p: (DIL_GROUP * g + j, 0, 0))
    return [head, head, head, pl.BlockSpec((1, 2, QBLK, QBLK), lambda j, p: (j, g, 0, 0))]


def _dil_group_fwd(g, qn, kn, v, bias):
    _, s, _ = qn.shape
    scale = HEAD ** -0.5
    steps = (s // QBLK) // DIL_PAIRS_PER_STEP

    def body(q_ref, k_ref, v_ref, b_ref, o_ref):
        pairs = [_dil_rows(g, s, pl.program_id(1) * DIL_PAIRS_PER_STEP + u) for u in range(DIL_PAIRS_PER_STEP)]
        loaded = []
        for rows, before, has_before in pairs:
            qb = (q_ref[0, rows, :] * scale).astype(BF16)
            _, _, l0, l1 = _dil_logits(qb, k_ref, rows, before, has_before, b_ref)
            loaded.append((l0, l1, v_ref[0, rows, :].astype(BF16), v_ref[0, before, :].astype(BF16)))
        results = []
        for l0, l1, v0, v1 in loaded:
            m = jnp.max(jnp.maximum(l0, l1), axis=1, keepdims=True)
            p0, p1 = jnp.exp(l0 - m), jnp.exp(l1 - m)
            den = jnp.sum(p0 + p1, axis=1, keepdims=True)
            inv = 1.0 / den
            o = _dot((p0 * inv).astype(BF16), v0) + _dot((p1 * inv).astype(BF16), v1)
            results.append(jnp.concatenate([o, jnp.broadcast_to(m + jnp.log(den), (QBLK, HEAD))], axis=1))
        for (rows, _, _), res in zip(pairs, results):
            o_ref[0, rows, :] = res

    return pl.pallas_call(
        body, name="dil%d_fwd" % g, grid=(DIL_GROUP, steps), in_specs=_dil_group_specs(g, s),
        out_specs=pl.BlockSpec((1, s, 2 * HEAD), lambda j, p: (j, 0, 0)),
        out_shape=jax.ShapeDtypeStruct((DIL_GROUP, s, 2 * HEAD), F32),
        compiler_params=_params(("arbitrary", "arbitrary")),
    )(qn, kn, v, bias)


def _dil_group_bwd(g, qn, kn, v, bias, ol, dol, prev):
    _, s, _ = qn.shape
    scale = HEAD ** -0.5
    steps = (s // QBLK) // DIL_PAIRS_PER_STEP
    prev = list(prev) if prev is not None else []

    def body(q_ref, k_ref, v_ref, b_ref, ol_ref, dol_ref, *rest):
        dq_ref, dk_ref, dv_ref, db_ref = rest[-4:]

        @pl.when(pl.program_id(1) == 0)
        def _():
            for r in (dk_ref, dv_ref, db_ref):
                r[...] = jnp.zeros_like(r)

        pairs = [_dil_rows(g, s, pl.program_id(1) * DIL_PAIRS_PER_STEP + u) for u in range(DIL_PAIRS_PER_STEP)]
        loaded = []
        for rows, before, has_before in pairs:
            qb = (q_ref[0, rows, :] * scale).astype(BF16)
            k0, k1, l0, l1 = _dil_logits(qb, k_ref, rows, before, has_before, b_ref)
            v0, v1 = v_ref[0, rows, :].astype(BF16), v_ref[0, before, :].astype(BF16)
            loaded.append((qb, k0, k1, l0, l1, v0, v1, ol_ref[0, rows, :], dol_ref[0, rows, :]))
        grads = []
        for qb, k0, k1, l0, l1, v0, v1, out_lse, d_out_lse in loaded:
            o, lse = out_lse[:, :HEAD], out_lse[:, HEAD:HEAD + 1]
            do, dlse = d_out_lse[:, :HEAD], d_out_lse[:, HEAD:HEAD + 1]
            dob = do.astype(BF16)
            p0, p1 = jnp.exp(l0 - lse), jnp.exp(l1 - lse)
            shift = dlse - jnp.sum(do * o, axis=1, keepdims=True)
            dl0 = p0 * (_dot(dob, v0, NT2) + shift)
            dl1 = p1 * (_dot(dob, v1, NT2) + shift)
            dl0b, dl1b = dl0.astype(BF16), dl1.astype(BF16)
            grads.append(((_dot(dl0b, k0) + _dot(dl1b, k1)) * scale,
                          _dot(dl0b, qb, TN2), _dot(dl1b, qb, TN2),
                          _dot(p0.astype(BF16), dob, TN2), _dot(p1.astype(BF16), dob, TN2), dl0, dl1))
        db0 = functools.reduce(jnp.add, [gr[5] for gr in grads])
        db1 = functools.reduce(jnp.add, [gr[6] for gr in grads])
        for (rows, before, _), (dq, dk0, dk1, dv0, dv1, _, _) in zip(pairs, grads):
            dq_ref[0, rows, :] = dq
            dk_ref[0, rows, :] += dk0
            dk_ref[0, before, :] += dk1
            dv_ref[0, rows, :] += dv0
            dv_ref[0, before, :] += dv1
        db_ref[0, 0] += db0
        db_ref[0, 1] += db1

    head_out = pl.BlockSpec((1, s, HEAD), lambda j, p: (DIL_GROUP * g + j, 0, 0))
    rows128 = pl.BlockSpec((1, s, 2 * HEAD), lambda j, p: (j, 0, 0))
    full_sh = jax.ShapeDtypeStruct(qn.shape, F32)
    return pl.pallas_call(
        body, name="dil%d_bwd" % g, grid=(DIL_GROUP, steps),
        in_specs=_dil_group_specs(g, s) + [rows128, rows128] + [pl.BlockSpec(memory_space=pl.ANY)] * len(prev),
        out_specs=[head_out, head_out, head_out, pl.BlockSpec((1, 2, QBLK, QBLK), lambda j, p: (j, 0, 0, 0))],
        out_shape=[full_sh, full_sh, full_sh, jax.ShapeDtypeStruct((DIL_GROUP, 2, QBLK, QBLK), F32)],
        input_output_aliases={6 + i: i for i in range(len(prev))},
        compiler_params=_params(("arbitrary", "arbitrary")),
    )(qn, kn, v, bias, ol, dol, *prev)


@functools.partial(jax.custom_vjp, nondiff_argnums=(2,))
def _bdot(a, b, dims):
    return _dot(a.astype(BF16), b.astype(BF16), dims)


def _bdot_fwd(a, b, dims):
    return _bdot(a, b, dims), (a, b)


def _bdot_bwd(dims, res, dc):
    a, b = res
    nn, nt, tn = (NN2, NT2, TN2) if dims in (NN2, NT2, TN2) else (NN3, NT3, TN3)
    if dims == nn:
        return _bdot(dc, b, nt), _bdot(a, dc, tn)
    if dims == nt:
        return _bdot(dc, b, nn), _bdot(dc, a, tn)
    return _bdot(b, dc, nt), _bdot(a, dc, nn)


_bdot.defvjp(_bdot_fwd, _bdot_bwd)


def _ones_dot(ones, x, dims):
    o = ones.astype(BF16)
    hi = x.astype(BF16)
    r1 = x - hi.astype(F32)
    mid = r1.astype(BF16)
    lo = (r1 - mid.astype(F32)).astype(BF16)
    return _dot(o, hi, dims) + _dot(o, mid, dims) + _dot(o, lo, dims)


@jax.custom_vjp
def _prefix_sums(x):
    c = x.shape[1]
    row = lax.broadcasted_iota(jnp.int32, (x.shape[0], c, c), 1)
    col = lax.broadcasted_iota(jnp.int32, (x.shape[0], c, c), 2)
    return _ones_dot((row >= col).astype(F32), x, NN3)


def _prefix_sums_fwd(x):
    return _prefix_sums(x), None


def _prefix_sums_bwd(_, dy):
    c = dy.shape[1]
    row = lax.broadcasted_iota(jnp.int32, (dy.shape[0], c, c), 1)
    col = lax.broadcasted_iota(jnp.int32, (dy.shape[0], c, c), 2)
    return (_ones_dot((row <= col).astype(F32), dy, NN3),)


_prefix_sums.defvjp(_prefix_sums_fwd, _prefix_sums_bwd)


def _rwkv_chunk(s0, r, lw, kraw, v, ag, kk_w, ka_w, rk_w, lng, lnb):
    hb, c, _ = r.shape
    kk = kraw * kk_w
    kk = kk / jnp.maximum(jnp.sqrt(jnp.sum(kk * kk, axis=-1, keepdims=True)), 1e-12)
    k = kraw * (1.0 + (ag - 1.0) * ka_w)
    a = -kk
    b = kk * ag
    row = lax.broadcasted_iota(jnp.int32, (hb, c, c), 1)
    col = lax.broadcasted_iota(jnp.int32, (hb, c, c), 2)
    lower, strict = row >= col, row > col
    cum = _prefix_sums(lw)
    ecum, einv = jnp.exp(cum), jnp.exp(-cum)
    rt, kt, bt = r * ecum, k * einv, b * einv
    at = a * jnp.exp(cum - lw)
    ar = jnp.concatenate([at, rt], axis=1)
    scores = _bdot(ar, jnp.concatenate([bt, kt], axis=1), NT3)
    a_ab = jnp.where(strict, scores[:, :c, :c], 0.0)
    a_ak = jnp.where(strict, scores[:, :c, c:], 0.0)
    p_rb = jnp.where(lower, scores[:, c:, :c], 0.0)
    p_rk = jnp.where(lower, scores[:, c:, c:], 0.0)
    from_s0 = _bdot(ar, s0, NT3)
    rhs = from_s0[:, :c] + _bdot(a_ak, v, NN3)
    inv = (row == col).astype(F32) + a_ab
    pw = a_ab
    for _ in range(int(math.log2(c)) - 1):
        pw = _bdot(pw, pw, NN3)
        inv = inv + _bdot(inv, pw, NN3)
    u = _bdot(inv, rhs, NN3)
    uv = jnp.concatenate([u, v], axis=1)
    y = from_s0[:, c:] + _bdot(jnp.concatenate([p_rb, p_rk], axis=2), uv, NN3)
    cum_end = cum[:, c - 1:c, :]
    dec = jnp.exp(cum_end - cum)
    s_end = s0 * jnp.exp(cum_end) + _bdot(uv, jnp.concatenate([b * dec, k * dec], axis=1), TN3)
    mu = jnp.mean(y, axis=-1, keepdims=True)
    var = jnp.mean(jnp.square(y - mu), axis=-1, keepdims=True)
    z = (y - mu) * lax.rsqrt(var + GN_EPS) * lng + lnb + jnp.sum(r * k * rk_w, axis=-1, keepdims=True) * v
    return z, s_end


def _rwkv_specs(nc, rev):
    cidx = (lambda c: nc - 1 - c) if rev else (lambda c: c)
    seq = pl.BlockSpec((RW_HB, RW_CHUNK, HEAD), lambda hg, c: (hg, cidx(c), 0))
    par = pl.BlockSpec((RW_HB, 1, HEAD), lambda hg, c: (hg, 0, 0))
    st = pl.BlockSpec((1, RW_HB, HEAD, HEAD), lambda hg, c: (cidx(c), hg, 0, 0))
    return seq, par, st


def _rwkv_fwd(seqs, pars):
    s = seqs[0].shape[1]
    nc = s // RW_CHUNK

    def body(*refs):
        seq_refs, par_refs = refs[:5], refs[5:10]
        z_ref, st_ref, state = refs[10:]
        c = pl.program_id(1)

        @pl.when(c == 0)
        def _():
            state[...] = jnp.zeros_like(state)

        s0 = state[...]
        st_ref[0] = s0
        z, s_end = _rwkv_chunk(s0, *[r[...] for r in seq_refs], *[r[...] for r in par_refs])
        z_ref[...] = z
        state[...] = s_end

    seq, par, st = _rwkv_specs(nc, False)
    return pl.pallas_call(
        body, name="rwkv_fwd", grid=(N_HEADS // RW_HB, nc),
        in_specs=[seq] * 5 + [par] * 5, out_specs=[seq, st],
        out_shape=[jax.ShapeDtypeStruct((N_HEADS, s, HEAD), F32), jax.ShapeDtypeStruct((nc, N_HEADS, HEAD, HEAD), F32)],
        scratch_shapes=[pltpu.VMEM((RW_HB, HEAD, HEAD), F32)],
        compiler_params=_params(("arbitrary", "arbitrary")),
    )(*seqs, *pars)


def _rwkv_bwd(seqs, pars, states, dz):
    s = seqs[0].shape[1]
    nc = s // RW_CHUNK

    def body(*refs):
        seq_refs, par_refs = refs[:5], refs[5:10]
        st_ref, dz_ref = refs[10:12]
        dseq_refs, dpar_refs, dstate = refs[12:17], refs[17:22], refs[22]
        c = pl.program_id(1)

        @pl.when(c == 0)
        def _():
            dstate[...] = jnp.zeros_like(dstate)
            for r in dpar_refs:
                r[...] = jnp.zeros_like(r)

        _, vjp = jax.vjp(_rwkv_chunk, st_ref[0], *[r[...] for r in seq_refs], *[r[...] for r in par_refs])
        g = vjp((dz_ref[...], dstate[...]))
        dstate[...] = g[0]
        for r, gs in zip(dseq_refs, g[1:6]):
            r[...] = gs
        for r, gp in zip(dpar_refs, g[6:]):
            r[...] += gp

    seq, par, st = _rwkv_specs(nc, True)
    seq_sh = jax.ShapeDtypeStruct((N_HEADS, s, HEAD), F32)
    par_sh = jax.ShapeDtypeStruct((N_HEADS, 1, HEAD), F32)
    outs = pl.pallas_call(
        body, name="rwkv_bwd", grid=(N_HEADS // RW_HB, nc),
        in_specs=[seq] * 5 + [par] * 5 + [st, seq],
        out_specs=[seq] * 5 + [par] * 5, out_shape=[seq_sh] * 5 + [par_sh] * 5,
        scratch_shapes=[pltpu.VMEM((RW_HB, HEAD, HEAD), F32)],
        compiler_params=_params(("arbitrary", "arbitrary")),
    )(*seqs, *pars, states, dz)
    return list(outs[:5]), list(outs[5:])


def _norm_fn(x, g):
    return (_rms(x, g),)


def _attn_prep_fn(proj, qn_w, kn_w):
    a, b = SB_HEADS, 3 * DIL_GROUP
    return (proj[0:a], proj[a:2 * a], proj[2 * a:3 * a],
            _rms(proj[3 * a:3 * a + b], qn_w), _rms(proj[3 * a + b:3 * a + 2 * b], kn_w), proj[3 * a + 2 * b:])


def _attn_merge_fn(o_sb, ol0, ol1, ol2):
    groups = (ol0, ol1, ol2)
    merged = []
    for j in range(DIL_GROUP):
        lses = [ol[j][:, HEAD:HEAD + 1] for ol in groups]
        m = functools.reduce(jnp.maximum, lses)
        es = [jnp.exp(l - m) for l in lses]
        inv = 1.0 / functools.reduce(jnp.add, es)
        merged.append(functools.reduce(jnp.add, [(e * inv) * ol[j][:, :HEAD] for e, ol in zip(es, groups)]))
    return (jnp.concatenate([_heads_to_nat(o_sb)] + merged, axis=-1),)


def _rw_mix_fn(x, xp, gn, mix, w0, w1, w2, a0, a1, a2, g1, g2):
    h = _rms(x, gn)
    xx = _rms(xp, gn) - h
    xr, xw, xk, xv, xa, xg = [h + xx * mix[i:i + 1] for i in range(6)]
    w_log = -jax.nn.softplus(-(w0 + _mm(jnp.tanh(_mm(xw, w1)), w2))) - 0.5
    lw = -jnp.exp(w_log)
    ag = jax.nn.sigmoid(a0 + _mm(_mm(xa, a1), a2))
    gate = _mm(jax.nn.sigmoid(_mm(xg, g1)), g2)
    return xr, xk, xv, _nat_to_heads(lw), _nat_to_heads(ag), gate


def _rw_gate_fn(z, gate):
    return (_heads_to_nat(z) * gate,)


def _adamw_update(w_ref, m_ref, v_ref, g_ref, go_ref, d_ref, mo_ref, vo_ref):
    g = g_ref[0].astype(F32)
    for j in range(1, N_DEV):
        g = g + g_ref[j].astype(F32)
    mn = ADAM_B1 * m_ref[...] + (1.0 - ADAM_B1) * g
    vn = ADAM_B2 * v_ref[...] + (1.0 - ADAM_B2) * jnp.square(g)
    m_hat = mn / (1.0 - ADAM_B1 ** ADAM_STEP)
    v_hat = vn / (1.0 - ADAM_B2 ** ADAM_STEP)
    go_ref[...] = g
    d_ref[...] = -ADAM_LR * (m_hat / (jnp.sqrt(v_hat) + ADAM_EPS) + ADAM_WD * w_ref[...])
    mo_ref[...] = mn
    vo_ref[...] = vn


def _adamw_many(name, items):
    n = len(items)

    def body(*refs):
        for i in range(n):
            _adamw_update(*refs[4 * i:4 * i + 4], *refs[4 * n + 4 * i:4 * n + 4 * i + 4])

    vmem = pl.BlockSpec(memory_space=pltpu.VMEM)
    outs = pl.pallas_call(
        body, name=name, in_specs=[vmem] * (4 * n), out_specs=[vmem] * (4 * n),
        out_shape=[jax.ShapeDtypeStruct(w.shape, F32) for w, _, _, _ in items for _ in range(4)],
        compiler_params=_params(),
    )(*[a for item in items for a in item])
    return [list(outs[4 * i:4 * i + 4]) for i in range(n)]


def _adamw(name, w, m, v, gparts, row0=0, prev=None):
    big_r, c = w.shape
    r = gparts.shape[1]
    tr = r
    if r % 8 == 0:
        tr = max(t for t in range(8, r + 1, 8) if r % t == 0 and (t * c * 4 <= (1 << 20) or t == 8))
    assert row0 % tr == 0 and (r == big_r or r % 8 == 0)
    off = row0 // tr

    def body(w_ref, m_ref, v_ref, g_ref, *rest):
        _adamw_update(w_ref, m_ref, v_ref, g_ref, *rest[-4:])

    tile = pl.BlockSpec((tr, c), lambda i: (i + off, 0))
    sh = jax.ShapeDtypeStruct((big_r, c), F32)
    prev = list(prev) if prev is not None else []
    return pl.pallas_call(
        body, name=name, grid=(r // tr,),
        in_specs=([tile, tile, tile, pl.BlockSpec((N_DEV, tr, c), lambda i: (0, i, 0))]
                  + [pl.BlockSpec(memory_space=pl.ANY)] * len(prev)),
        out_specs=[tile] * 4, out_shape=[sh] * 4,
        input_output_aliases={4 + j: j for j in range(len(prev))},
        compiler_params=_params(("arbitrary",)),
    )(w, m, v, gparts, *prev)


def _col_blocks_to_nat(g):
    return jnp.moveaxis(g, 0, 1).reshape(g.shape[1], -1)


def _nat_to_col_blocks(a):
    return jnp.moveaxis(a.reshape(a.shape[0], N_DEV, -1), 1, 0)


AG_GROUPS = ("f00", "att", "f01", "f10", "rw", "f11")
RS_GROUPS = ("f11", "rw", "f10", "f01", "f00", "att")
BF16_GRAD_GROUPS = ("att", "f00")
RW_SHARDED = ('rw_mix', 'rw_w0', 'rw_w1', 'rw_w2', 'rw_a0', 'rw_a1', 'rw_a2', 'rw_g1', 'rw_g2', 'rw_kk', 'rw_ka',
              'rw_wr', 'rw_wk', 'rw_wv', 'rw_wo', 'rw_lnx_g', 'rw_lnx_b')


def _step(x, target, rep, get, put):
    tied = lambda a, zero: a + zero[0, 0].astype(a.dtype)
    s, d = x.shape
    tf = min(512, s)
    tw = min(1024, s)
    tt = min(256, s)
    row = lambda a: a.reshape(1, -1)
    mix_norm = rep["mix_norm"]
    ffw = {(0, 0): get("f00", None)}
    ffn_norm = _col_blocks_to_nat(ffw[(0, 0)]["ffn_norm"].reshape(N_DEV, 4, -1))

    acts = {}

    def ffn(nm, xin, l, h):
        g = ffw[(l, h)]
        out, *acts[(l, h)] = _ffn_fwd(nm, xin, ffn_norm[2 * l + h][None], g["gate"], g["up"], g["down"], min(2 * tf, s))
        return out

    x1 = ffn("ffn00_fwd", x, 0, 0)
    att = get("att", x1)
    w_in = _col_blocks_to_nat(att["attn_w_in"])
    w_out = _col_blocks_to_nat(att["attn_w_out"])
    (h0,) = _tile_fwd("mixnorm0_fwd", _norm_fn, [(x1, "nat")], [mix_norm[0:1]], [((s, d), BF16, "nat")], tt)
    proj = _linear_fwd("attn_in_fwd", h0, w_in, tf, out_layout="hm")
    bias = _bias_tiles(rep["rel_bias"], s)
    prep_pars = [rep["attn_q_norm"], rep["attn_k_norm"]]
    sb_sh, dl_sh = (SB_HEADS, s, HEAD), (3 * DIL_GROUP, s, HEAD)
    sq, sk, sv, qn, kn, vd = _tile_fwd("attn_prep_fwd", _attn_prep_fn, [(proj, "hm")], prep_pars,
                                       [(sb_sh, BF16, "hm")] * 3 + [(dl_sh, F32, "hm")] * 3, tt // 2)
    o_sb = _sb_fwd(sq, sk, sv)
    ols = [_dil_group_fwd(g, qn, kn, vd, bias) for g in range(3)]
    merge_tiled = [(o_sb, "hm")] + [(ol, "hm") for ol in ols]
    (merged,) = _tile_fwd("merge_fwd", _attn_merge_fn, merge_tiled, [], [((s, 512), BF16, "nat")], tt)
    x2 = _linear_fwd("attn_out_fwd", merged, w_out, tf, residual=x1)
    ffw[(0, 1)] = get("f01", x2)
    x3 = ffn("ffn01_fwd", x2, 0, 1)
    ffw[(1, 0)] = get("f10", x3)
    x4 = ffn("ffn10_fwd", x3, 1, 0)
    rw = get("rw", x4)
    rw_mix = _col_blocks_to_nat(rw["rw_mix"])
    rw_w1, rw_a1, rw_g1 = (rw[k].reshape(d, -1) for k in ("rw_w1", "rw_a1", "rw_g1"))
    rw_w2, rw_a2, rw_g2 = (_col_blocks_to_nat(rw[k]) for k in ("rw_w2", "rw_a2", "rw_g2"))
    rw_w0, rw_a0 = row(rw["rw_w0"]), row(rw["rw_a0"])
    head_par = lambda a: a.reshape(N_HEADS, 1, HEAD)
    scan_pars = [head_par(rw["rw_kk"]), head_par(rw["rw_ka"]), head_par(rep["rw_rk"]),
                 head_par(rw["rw_lnx_g"]), head_par(rw["rw_lnx_b"])]
    w_rkv = [rw[k].reshape(d, d) for k in ("rw_wr", "rw_wk", "rw_wv")]
    w_o = rw["rw_wo"].reshape(d, d)
    x4p = jnp.pad(x4, ((1, 0), (0, 0)))[:-1]
    mix_tiled = [(x4, "nat"), (x4p, "nat")]
    mix_pars = [mix_norm[1:2], rw_mix, rw_w0, rw_w1, rw_w2, rw_a0, rw_a1, rw_a2, rw_g1, rw_g2]
    hm_sh = (N_HEADS, s, HEAD)
    xr, xk, xv, lw, ag, gate = _tile_fwd(
        "rw_mix_fwd", _rw_mix_fn, mix_tiled, mix_pars,
        [((s, d), BF16, "nat")] * 3 + [(hm_sh, F32, "hm")] * 2 + [((s, d), F32, "nat")], tt)
    r_h, k_h, v_h = [_linear_fwd("rw_%s_fwd" % nm, xi, wi, tf, out_layout="hm")
                     for nm, xi, wi in zip("rkv", (xr, xk, xv), w_rkv)]
    scan_seqs = [r_h, lw, k_h, v_h, ag]
    z, states = _rwkv_fwd(scan_seqs, scan_pars)
    (zg,) = _tile_fwd("rw_gate_fwd", _rw_gate_fn, [(z, "hm"), (gate, "nat")], [], [((s, d), BF16, "nat")], tt)
    x5 = _linear_fwd("rw_out_fwd", zg, w_o, tf, residual=x4)
    ffw[(1, 1)] = get("f11", x5)
    y = ffn("ffn11_fwd", x5, 1, 1)
    dy, loss = _loss_head(y, target, tf)

    G = {}
    dgn = {}

    def fb(nm, group, xin, dout, l, h, zero=None, extra=None):
        g = ffw[(l, h)]
        gn = ffn_norm[2 * l + h][None]
        dxin, dgn[(l, h)], dwg, dwu, dwd = _ffn_bwd(nm, xin, dout, gn if zero is None else tied(gn, zero),
                                                   g["gate"], g["up"], g["down"], *acts[(l, h)], tf)
        shard = {"gate": dwg, "up": dwu, "down": dwd}
        if extra is not None:
            shard.update(extra())
        return dxin, put(group, {}, shard)

    dx5, zero = fb("ffn11_bwd", "f11", x5, dy, 1, 1)
    dzg = _linear_dx("rw_out_dx", dx5, w_o, tf)
    G["rw_wo"] = _linear_dw("rw_out_dw", zg, dx5, tw, 512)
    (dz, dgate), _ = _tile_bwd("rw_gate_bwd", _rw_gate_fn, [(z, "hm"), (gate, "nat")], [], [(dzg, "nat")], tt, [True, True])
    (dr_h, dlw, dk_h, dv_h, dag), dscan = _rwkv_bwd(scan_seqs, [tied(scan_pars[0], zero)] + scan_pars[1:], states, dz)
    drkv = (dr_h, dk_h, dv_h)
    for k, gpar in zip(("rw_kk", "rw_ka", "rw_rk", "rw_lnx_g", "rw_lnx_b"), dscan):
        G[k] = gpar
    dxs = []
    for j, (nm, xi, wi) in enumerate(zip("rkv", (xr, xk, xv), w_rkv)):
        dxs.append(_linear_dx("rw_%s_dx" % nm, drkv[j], wi, tf, dy_layout="hm"))
        G["rw_w" + nm] = _linear_dw("rw_%s_dw" % nm, xi, drkv[j], tw, 512, dy_layout="hm")
    (dx4a, dx4p), dmix = _tile_bwd(
        "rw_mix_bwd", _rw_mix_fn, mix_tiled, mix_pars,
        [(dxs[0], "nat"), (dxs[1], "nat"), (dxs[2], "nat"), (dlw, "hm"), (dag, "hm"), (dgate, "nat")],
        tt, [True, True], adds=[dx5, None])
    d_mixn1 = dmix[0]
    for k, gpar in zip(("rw_mix", "rw_w0", "rw_w1", "rw_w2", "rw_a0", "rw_a1", "rw_a2", "rw_g1", "rw_g2"), dmix[1:]):
        G[k] = gpar
    dx4 = dx4a + jnp.pad(dx4p[1:], ((0, 1), (0, 0)))
    for k in ("rw_mix", "rw_w2", "rw_a2", "rw_g2"):
        G[k] = _nat_to_col_blocks(G[k])
    for k in ("rw_w1", "rw_a1", "rw_g1", "rw_wr", "rw_wk", "rw_wv", "rw_wo"):
        G[k] = G[k].reshape(N_DEV, d // N_DEV, -1)
    for k in ("rw_w0", "rw_a0", "rw_kk", "rw_ka", "rw_lnx_g", "rw_lnx_b"):
        G[k] = G[k].reshape(N_DEV, 1, d // N_DEV)
    zero = put("rw", {"rw_rk": G["rw_rk"].reshape(N_HEADS, HEAD)}, {k: G[k] for k in RW_SHARDED})
    dx3, zero = fb("ffn10_bwd", "f10", x3, dx4, 1, 0, zero)
    dx2, zero = fb("ffn01_bwd", "f01", x2, dx3, 0, 1, zero)
    dmerged = _linear_dx("attn_out_dx", dx2, tied(w_out, zero), tf)
    (do_sb, *dols), _ = _tile_bwd("merge_bwd", _attn_merge_fn, merge_tiled, [], [(dmerged, "nat")], tt, [True] * 4)
    dq_sb, dk_sb, dv_sb = _sb_bwd(sq, sk, sv, do_sb)
    dil_grads, dbias = None, []
    for g in range(3):
        *dil_grads, db = _dil_group_bwd(g, qn, kn, vd, bias, ols[g], dols[g], dil_grads)
        dbias.append(db)
    dqn, dkn, dvd = dil_grads
    dbias = jnp.concatenate(dbias, axis=1)
    (dproj,), (dqn_w, dkn_w) = _tile_bwd(
        "attn_prep_bwd", _attn_prep_fn, [(proj, "hm")], prep_pars,
        [(dq_sb, "hm"), (dk_sb, "hm"), (dv_sb, "hm"), (dqn, "hm"), (dkn, "hm"), (dvd, "hm")], tt // 2, [True])
    dh0 = _linear_dx("attn_in_dx", dproj, w_in, tf, dy_layout="hm")
    (dx1,), (d_mixn0,) = _tile_bwd("mixnorm0_bwd", _norm_fn, [(x1, "nat")], [mix_norm[0:1]], [(dh0, "nat")], tt,
                                   [True], adds=[dx2])
    order = [(0, 0), (0, 1), (1, 0), (1, 1)]
    norm_grads = lambda: {"ffn_norm": _nat_to_col_blocks(jnp.concatenate([dgn[o] for o in order], axis=0))}
    dx0, zero = fb("ffn00_bwd", "f00", x, dx1, 0, 0, extra=norm_grads)
    G["attn_w_out"] = _linear_dw("attn_out_dw", tied(merged, zero), dx2, tw, 512)
    G["attn_w_in"] = _linear_dw("attn_in_dw", tied(h0, zero), dproj, tw, 512, dy_layout="hm")
    rep_grads = {"mix_norm": jnp.concatenate([d_mixn0, d_mixn1], axis=0), "rel_bias": _bias_tiles_bwd(dbias, s),
                 "attn_q_norm": dqn_w, "attn_k_norm": dkn_w}
    zero = put("att", rep_grads, {k: _nat_to_col_blocks(G[k]) for k in ("attn_w_in", "attn_w_out")})
    return loss, dx0, zero


WEIGHTS = ['ffn_norm', 'ffn_w_gate', 'ffn_w_up', 'ffn_w_down', 'mix_norm', 'rel_bias', 'attn_w_in', 'attn_q_norm',
           'attn_k_norm', 'attn_w_out', 'rw_mix', 'rw_w0', 'rw_w1', 'rw_w2', 'rw_a0', 'rw_a1', 'rw_a2', 'rw_g1', 'rw_g2',
           'rw_kk', 'rw_ka', 'rw_rk', 'rw_wr', 'rw_wk', 'rw_wv', 'rw_wo', 'rw_lnx_g', 'rw_lnx_b']
REPLICATED = ('mix_norm', 'rel_bias', 'attn_q_norm', 'attn_k_norm', 'rw_rk')
BF16_WEIGHTS = ('ffn_w_gate', 'ffn_w_up', 'ffn_w_down', 'attn_w_in', 'attn_w_out', 'rw_wr', 'rw_wk', 'rw_wv', 'rw_wo')


def kernel(x, ffn_norm, ffn_w_gate, ffn_w_up, ffn_w_down, mix_norm, rel_bias, attn_w_in, attn_q_norm, attn_k_norm, attn_w_out, rw_mix, rw_w0, rw_w1, rw_w2, rw_a0, rw_a1, rw_a2, rw_g1, rw_g2, rw_kk, rw_ka, rw_rk, rw_wr, rw_wk, rw_wv, rw_wo, rw_lnx_g, rw_lnx_b, loss_target, m_ffn_norm, m_ffn_w_gate, m_ffn_w_up, m_ffn_w_down, m_mix_norm, m_rel_bias, m_attn_w_in, m_attn_q_norm, m_attn_k_norm, m_attn_w_out, m_rw_mix, m_rw_w0, m_rw_w1, m_rw_w2, m_rw_a0, m_rw_a1, m_rw_a2, m_rw_g1, m_rw_g2, m_rw_kk, m_rw_ka, m_rw_rk, m_rw_wr, m_rw_wk, m_rw_wv, m_rw_wo, m_rw_lnx_g, m_rw_lnx_b, v_ffn_norm, v_ffn_w_gate, v_ffn_w_up, v_ffn_w_down, v_mix_norm, v_rel_bias, v_attn_w_in, v_attn_q_norm, v_attn_k_norm, v_attn_w_out, v_rw_mix, v_rw_w0, v_rw_w1, v_rw_w2, v_rw_a0, v_rw_a1, v_rw_a2, v_rw_g1, v_rw_g2, v_rw_kk, v_rw_ka, v_rw_rk, v_rw_wr, v_rw_wk, v_rw_wv, v_rw_wo, v_rw_lnx_g, v_rw_lnx_b):
    args = locals()
    w = {k: args[k] for k in WEIGHTS}
    cast = lambda k, a: a.astype(BF16) if k in BF16_WEIGHTS else a

    sources = {}
    for l, h in ((0, 0), (0, 1), (1, 0), (1, 1)):
        sources["f%d%d" % (l, h)] = {"gate": cast("ffn_w_gate", ffn_w_gate[l, h]), "up": cast("ffn_w_up", ffn_w_up[l, h]),
                                     "down": cast("ffn_w_down", ffn_w_down[l, h])}
    sources["f00"]["ffn_norm"] = ffn_norm
    drop_lead = lambda a: a[0] if a.ndim == 3 else a
    sources["att"] = {k: cast(k, w[k][0]) for k in ("attn_w_in", "attn_w_out")}
    sources["rw"] = {k: cast(k, drop_lead(w[k])) for k in RW_SHARDED}
    ag, token = {}, None
    for group in AG_GROUPS:
        names, arrays = list(sources[group]), list(sources[group].values())
        if token is not None:
            arrays[0] = arrays[0] + token[0, 0].astype(arrays[0].dtype)
        ag[group] = (names, _exchange_start("ag_start_" + group, arrays, []))
        token = ag[group][1]["token"]
    last_ag_token = token

    def get(group, after):
        names, started = ag[group]
        gathered, _ = _exchange_wait("ag_wait_" + group, started, last_ag_token if after is None else after)
        return dict(zip(names, gathered))

    rs = {}

    def put(group, rep_grads, shard_grads):
        if group in BF16_GRAD_GROUPS:
            shard_grads = {k: v.astype(BF16) for k, v in shard_grads.items()}
        started = _exchange_start("rs_start_" + group, list(rep_grads.values()), list(shard_grads.values()))
        rs[group] = (list(rep_grads), list(shard_grads), started)
        return started["token"]

    loss, dx, last_zero = _step(x[0], loss_target[0], {k: w[k] for k in REPLICATED}, get, put)
    loss = lax.psum(loss, MESH_AXES)

    results = {}
    ffn_prev = {}

    def update(k, parts, row0=0, prev=None):
        c = w[k].shape[-1]
        as2d = lambda a: a.reshape(-1, c)
        return _adamw("adamw_%s_%d" % (k, row0), as2d(w[k]), as2d(args["m_" + k]), as2d(args["v_" + k]),
                      parts.reshape(N_DEV, -1, c), row0, prev)

    def update_small(name, named_parts):
        items = []
        for k, parts in named_parts:
            c = w[k].shape[-1]
            items.append((w[k].reshape(-1, c), args["m_" + k].reshape(-1, c), args["v_" + k].reshape(-1, c),
                          parts.reshape(N_DEV, -1, c)))
        for (k, _), res in zip(named_parts, _adamw_many(name, items)):
            results[k] = res

    after = last_zero
    for group in RS_GROUPS:
        rep_names, shard_names, started = rs[group]
        rep_parts, shard_parts = _exchange_wait("rs_wait_" + group, started, after)
        small = []
        for k, parts in list(zip(rep_names, rep_parts)) + list(zip(shard_names, shard_parts)):
            if k in ("gate", "up", "down"):
                full = "ffn_w_" + k
                piece = 2 * int(group[1]) + int(group[2])
                ffn_prev[full] = update(full, parts, piece * parts.shape[1], ffn_prev.get(full))
                results[full] = ffn_prev[full]
                after = results[full][0]
            elif k == "attn_w_in":
                results[k] = update(k, parts)
                after = results[k][0]
            else:
                small.append((k, parts))
        if small:
            update_small("adamw_small_" + group, small)
            after = results[small[0][0]][0]

    outs = [[results[k][j].reshape(w[k].shape) for k in WEIGHTS] for j in range(4)]
    return (loss, dx[None], *outs[0], *outs[1], *outs[2], *outs[3])
```

```python
import functools
import math

import numpy as np
import jax
import jax.numpy as jnp
from jax import lax
from jax.experimental import pallas as pl
from jax.experimental.pallas import tpu as pltpu

F32, BF16 = jnp.float32, jnp.bfloat16

N_DEV = 8
D_MODEL = 1024
HEAD = 64
N_HEADS = 16
SB_HEADS = 4
DIL_GROUP = 4
DIL_PATTERNS = ((128, 1), (512, 4), (2048, 16))
QBLK = 128
N_BUCKETS = 32
MAX_DISTANCE = 2048
NORM_EPS = 1e-6
GN_EPS = 64e-5
NEG_INF = -1e30
RW_CHUNK = 64
RW_HB = 16
ADAM_LR, ADAM_B1, ADAM_B2, ADAM_EPS, ADAM_WD, ADAM_STEP = 0.001, 0.9, 0.999, 1e-08, 0.01, 10
MESH_AXES = ("x", "y", "c")
VMEM_LIMIT_BYTES = 56 * 1024 * 1024

NN2 = (((1,), (0,)), ((), ()))
NT2 = (((1,), (1,)), ((), ()))
TN2 = (((0,), (0,)), ((), ()))
NN3 = (((2,), (1,)), ((0,), (0,)))
NT3 = (((2,), (2,)), ((0,), (0,)))
TN3 = (((1,), (1,)), ((0,), (0,)))


def _dot(a, b, dims=NN2, prec=None):
    return lax.dot_general(a, b, dims, precision=prec, preferred_element_type=F32)


def _params(sem=None):
    return pltpu.CompilerParams(dimension_semantics=sem, vmem_limit_bytes=VMEM_LIMIT_BYTES)


@jax.custom_vjp
def _mm(x, w):
    return _dot(x.astype(BF16), w.astype(BF16))


def _mm_fwd(x, w):
    return _mm(x, w), (x, w)


def _mm_bwd(res, dy):
    x, w = res
    dyb = dy.astype(BF16)
    return (_dot(dyb, w.astype(BF16), NT2).astype(x.dtype), _dot(x.astype(BF16), dyb, TN2).astype(w.dtype))


_mm.defvjp(_mm_fwd, _mm_bwd)


def _rms(x, g):
    return x * lax.rsqrt(jnp.mean(x * x, axis=-1, keepdims=True) + NORM_EPS) * g


def _log_sigmoid(z):
    return jnp.minimum(z, 0.0) - jnp.log(1.0 + jnp.exp(-jnp.abs(z)))


def _heads_to_nat(v3):
    return jnp.concatenate([v3[h] for h in range(v3.shape[0])], axis=-1)


def _nat_to_heads(v2):
    return jnp.stack([v2[:, h * HEAD:(h + 1) * HEAD] for h in range(v2.shape[1] // HEAD)], axis=0)


def _exchange(name, gathers, scatters):
    n_g = len(gathers)
    arrays = list(gathers) + list(scatters)
    n = len(arrays)
    out_shape = [jax.ShapeDtypeStruct((N_DEV,) + a.shape, a.dtype) for a in gathers]
    out_shape += [jax.ShapeDtypeStruct(a.shape, a.dtype) for a in scatters]

    def body(*refs):
        ins, outs = refs[:n], refs[n:2 * n]
        send_sems, recv_sems, local_sems = refs[2 * n:]
        x, y, c = lax.axis_index("x"), lax.axis_index("y"), lax.axis_index("c")
        me = 4 * x + 2 * y + c

        def src(i, idx):
            return ins[i] if i < n_g else ins[i].at[idx]

        local = [pltpu.make_async_copy(src(i, me), outs[i].at[me], local_sems.at[i]) for i in range(n)]
        for cp in local:
            cp.start()
        remote = []
        for m in range(1, N_DEV):
            px, py, pc = x ^ ((m >> 2) & 1), y ^ ((m >> 1) & 1), c ^ (m & 1)
            peer = 4 * px + 2 * py + pc
            for i in range(n):
                cp = pltpu.make_async_remote_copy(
                    src_ref=src(i, peer), dst_ref=outs[i].at[me],
                    send_sem=send_sems.at[i, m - 1], recv_sem=recv_sems.at[i, m - 1],
                    device_id=(px, py, pc), device_id_type=pl.DeviceIdType.MESH)
                cp.start()
                arrival = pltpu.make_async_remote_copy(
                    src_ref=src(i, peer), dst_ref=outs[i].at[peer],
                    send_sem=send_sems.at[i, m - 1], recv_sem=recv_sems.at[i, m - 1],
                    device_id=(px, py, pc), device_id_type=pl.DeviceIdType.MESH)
                remote.append((cp, arrival))
        for cp, arrival in remote:
            cp.wait_send()
            arrival.wait_recv()
        for cp in local:
            cp.wait()

    hbm = pl.BlockSpec(memory_space=pltpu.HBM)
    outs = pl.pallas_call(
        body, name=name, out_shape=out_shape,
        in_specs=[hbm] * n, out_specs=[hbm] * n,
        scratch_shapes=[pltpu.SemaphoreType.DMA((n, N_DEV - 1)), pltpu.SemaphoreType.DMA((n, N_DEV - 1)),
                        pltpu.SemaphoreType.DMA((n,))],
    )(*arrays)
    return list(outs[:n_g]), list(outs[n_g:])


def _mesh_peers():
    x, y, c = lax.axis_index("x"), lax.axis_index("y"), lax.axis_index("c")
    peers = []
    for m in range(1, N_DEV):
        px, py, pc = x ^ ((m >> 2) & 1), y ^ ((m >> 1) & 1), c ^ (m & 1)
        peers.append((m, (px, py, pc), 4 * px + 2 * py + pc))
    return 4 * x + 2 * y + c, peers


_HBM_SPEC = pl.BlockSpec(memory_space=pltpu.HBM)
_SEM_SPEC = pl.BlockSpec(memory_space=pltpu.SEMAPHORE)
_DATAFLOW = pltpu.SideEffectType.DATAFLOW_SIDE_EFFECTING


def _exchange_start(name, gathers, scatters):
    n_g = len(gathers)
    arrays = list(gathers) + list(scatters)
    n = len(arrays)
    lands = ([lax.empty((N_DEV,) + a.shape, a.dtype) for a in gathers] + [lax.empty(a.shape, a.dtype) for a in scatters])

    def body(*refs):
        ins, land = refs[:n], refs[n:2 * n]
        send_sems, recv_sems, local_sems, token = refs[2 * n], refs[2 * n + 1], refs[2 * n + 2], refs[-1]
        me, peers = _mesh_peers()
        for m, dev, peer in peers:
            for i in range(n):
                k = i * (N_DEV - 1) + m - 1
                pltpu.make_async_remote_copy(
                    src_ref=ins[i] if i < n_g else ins[i].at[peer], dst_ref=land[i].at[me],
                    send_sem=send_sems.at[k], recv_sem=recv_sems.at[k],
                    device_id=dev, device_id_type=pl.DeviceIdType.MESH).start()
        for i in range(n):
            pltpu.make_async_copy(ins[i] if i < n_g else ins[i].at[me], land[i].at[me], local_sems.at[i]).start()
        token[...] = jnp.zeros_like(token)

    sem = pltpu.SemaphoreType.DMA((n * (N_DEV - 1),))
    outs = pl.pallas_call(
        body, name=name,
        out_shape=([sem, sem, pltpu.SemaphoreType.DMA((n,))] + [pltpu.HBM(a.shape, a.dtype) for a in arrays]
                   + [pltpu.HBM(l.shape, l.dtype) for l in lands] + [jax.ShapeDtypeStruct((8, 128), F32)]),
        in_specs=[_HBM_SPEC] * (2 * n),
        out_specs=[_SEM_SPEC] * 3 + [_HBM_SPEC] * (2 * n) + [pl.BlockSpec(memory_space=pltpu.VMEM)],
        input_output_aliases={i: i + 3 for i in range(2 * n)},
        compiler_params=pltpu.CompilerParams(has_side_effects=_DATAFLOW),
    )(*[pltpu.with_memory_space_constraint(a, pltpu.HBM) for a in arrays],
      *[pltpu.with_memory_space_constraint(l, pltpu.HBM) for l in lands])
    return dict(n_g=n_g, n=n, send=outs[0], recv=outs[1], local=outs[2], srcs=list(outs[3:3 + n]),
                lands=list(outs[3 + n:3 + 2 * n]), token=outs[-1])


def _exchange_wait(name, started, after):
    n, n_g = started["n"], started["n_g"]

    def body(*refs):
        srcs, lands = refs[:n], refs[n:2 * n]
        send_sems, recv_sems, local_sems = refs[2 * n], refs[2 * n + 1], refs[2 * n + 2]
        me, peers = _mesh_peers()
        local = [pltpu.make_async_copy(srcs[i] if i < n_g else srcs[i].at[me], lands[i].at[me], local_sems.at[i])
                 for i in range(n)]
        for m, dev, peer in peers:
            for i in range(n):
                k = i * (N_DEV - 1) + m - 1
                cp = pltpu.make_async_remote_copy(
                    src_ref=srcs[i] if i < n_g else srcs[i].at[peer], dst_ref=lands[i].at[peer],
                    send_sem=send_sems.at[k], recv_sem=recv_sems.at[k],
                    device_id=dev, device_id_type=pl.DeviceIdType.MESH)
                cp.wait_send()
                cp.wait_recv()
        for cp in local:
            cp.wait()

    outs = pl.pallas_call(
        body, name=name,
        out_shape=([pltpu.HBM(a.shape, a.dtype) for a in started["srcs"]]
                   + [pltpu.HBM(l.shape, l.dtype) for l in started["lands"]]),
        in_specs=[_HBM_SPEC] * (2 * n) + [_SEM_SPEC] * 3 + [pl.BlockSpec(memory_space=pl.ANY)],
        out_specs=[_HBM_SPEC] * (2 * n), input_output_aliases={i: i for i in range(2 * n)},
        compiler_params=pltpu.CompilerParams(has_side_effects=_DATAFLOW),
    )(*started["srcs"], *started["lands"], started["send"], started["recv"], started["local"], after)
    return list(outs[n:n + n_g]), list(outs[n + n_g:])


def _tile_spec(shape, layout, t):
    if layout == "nat":
        return pl.BlockSpec((t, shape[1]), lambda i: (i, 0))
    return pl.BlockSpec((shape[0], t, shape[2]), lambda i: (0, i, 0))


def _full_spec(shape):
    nd = len(shape)
    return pl.BlockSpec(tuple(shape), lambda i: (0,) * nd)


def _seq_len(a, layout):
    return a.shape[0] if layout == "nat" else a.shape[1]


def _tile_fwd(name, f, tiled, params, outs, t):
    nt, npar = len(tiled), len(params)
    s = _seq_len(*tiled[0])

    def body(*refs):
        vals = [r[...] for r in refs[:nt + npar]]
        res = f(*vals)
        for r, o in zip(refs[nt + npar:], res):
            r[...] = o.astype(r.dtype)

    return pl.pallas_call(
        body, name=name, grid=(s // t,),
        in_specs=[_tile_spec(a.shape, l, t) for a, l in tiled] + [_full_spec(p.shape) for p in params],
        out_specs=[_tile_spec(sh, l, t) for sh, _, l in outs],
        out_shape=[jax.ShapeDtypeStruct(sh, dt) for sh, dt, _ in outs],
        compiler_params=_params(("arbitrary",)),
    )(*[a for a, _ in tiled], *params)


def _tile_bwd(name, f, tiled, params, cts, t, need, adds=None):
    nt, npar, nc = len(tiled), len(params), len(cts)
    s = _seq_len(*tiled[0])
    need_idx = [k for k in range(nt) if need[k]]
    adds = adds or [None] * len(need_idx)
    add_arrays = [(a, tiled[k][1]) for a, k in zip(adds, need_idx) if a is not None]
    n_add = len(add_arrays)

    def body(*refs):
        i = pl.program_id(0)
        vals = [r[...] for r in refs[:nt + npar]]
        ct_refs = refs[nt + npar:nt + npar + nc]
        add_refs = refs[nt + npar + nc:nt + npar + nc + n_add]
        out_refs = refs[nt + npar + nc + n_add:]
        res, vjp = jax.vjp(f, *vals)
        grads = vjp(tuple(r[...].astype(o.dtype) for r, o in zip(ct_refs, res)))
        a = 0
        for j, k in enumerate(need_idx):
            g = grads[k]
            if adds[j] is not None:
                g = g + add_refs[a][...]
                a += 1
            out_refs[j][...] = g.astype(out_refs[j].dtype)
        for j in range(npar):
            r = out_refs[len(need_idx) + j]

            @pl.when(i == 0)
            def _():
                r[...] = jnp.zeros_like(r)

            r[...] += grads[nt + j]

    outs = pl.pallas_call(
        body, name=name, grid=(s // t,),
        in_specs=([_tile_spec(a.shape, l, t) for a, l in tiled] + [_full_spec(p.shape) for p in params]
                  + [_tile_spec(a.shape, l, t) for a, l in cts] + [_tile_spec(a.shape, l, t) for a, l in add_arrays]),
        out_specs=([_tile_spec(tiled[k][0].shape, tiled[k][1], t) for k in need_idx]
                   + [_full_spec(p.shape) for p in params]),
        out_shape=([jax.ShapeDtypeStruct(tiled[k][0].shape, F32) for k in need_idx]
                   + [jax.ShapeDtypeStruct(p.shape, F32) for p in params]),
        compiler_params=_params(("arbitrary",)),
    )(*[a for a, _ in tiled], *params, *[a for a, _ in cts], *[a for a, _ in add_arrays])
    return list(outs[:len(need_idx)]), list(outs[len(need_idx):])


def _linear_fwd(name, x, w, t, out_layout="nat", residual=None):
    s, k = x.shape
    n = w.shape[1]
    has_res = residual is not None

    def body(*refs):
        x_ref, w_ref = refs[0], refs[1]
        o_ref = refs[-1]
        y = _dot(x_ref[...].astype(BF16), w_ref[...])
        if has_res:
            y = y + refs[2][...]
        if out_layout == "hm":
            for h in range(n // HEAD):
                o_ref[h] = y[:, h * HEAD:(h + 1) * HEAD]
        else:
            o_ref[...] = y

    out_sh = (s, n) if out_layout == "nat" else (n // HEAD, s, HEAD)
    ins = [x, w] + ([residual] if has_res else [])
    in_specs = [_tile_spec(x.shape, "nat", t), _full_spec(w.shape)] + ([_tile_spec((s, n), "nat", t)] if has_res else [])
    return pl.pallas_call(
        body, name=name, grid=(s // t,), in_specs=in_specs,
        out_specs=_tile_spec(out_sh, out_layout, t), out_shape=jax.ShapeDtypeStruct(out_sh, F32),
        compiler_params=_params(("arbitrary",)),
    )(*ins)


def _linear_dx(name, dy, w, t, dy_layout="nat"):
    k, n = w.shape
    s = _seq_len(dy, dy_layout)

    def body(dy_ref, w_ref, o_ref):
        dy = _heads_to_nat(dy_ref[...].astype(BF16)) if dy_layout == "hm" else dy_ref[...].astype(BF16)
        o_ref[...] = _dot(dy, w_ref[...], NT2)

    return pl.pallas_call(
        body, name=name, grid=(s // t,),
        in_specs=[_tile_spec(dy.shape, dy_layout, t), _full_spec(w.shape)],
        out_specs=_tile_spec((s, k), "nat", t), out_shape=jax.ShapeDtypeStruct((s, k), F32),
        compiler_params=_params(("arbitrary",)),
    )(dy, w)


def _linear_dw(name, x, dy, t, nb, dy_layout="nat"):
    s, k = x.shape
    n = dy.shape[1] if dy_layout == "nat" else dy.shape[0] * HEAD

    def body(x_ref, dy_ref, o_ref):
        i = pl.program_id(1)

        @pl.when(i == 0)
        def _():
            o_ref[...] = jnp.zeros_like(o_ref)

        dy = _heads_to_nat(dy_ref[...].astype(BF16)) if dy_layout == "hm" else dy_ref[...].astype(BF16)
        o_ref[...] += _dot(x_ref[...].astype(BF16), dy, TN2)

    if dy_layout == "hm":
        dy_spec = pl.BlockSpec((nb // HEAD, t, HEAD), lambda j, i: (j, i, 0))
    else:
        dy_spec = pl.BlockSpec((t, nb), lambda j, i: (i, j))
    return pl.pallas_call(
        body, name=name, grid=(n // nb, s // t),
        in_specs=[pl.BlockSpec((t, k), lambda j, i: (i, 0)), dy_spec],
        out_specs=pl.BlockSpec((k, nb), lambda j, i: (0, j)), out_shape=jax.ShapeDtypeStruct((k, n), F32),
        compiler_params=_params(("arbitrary", "arbitrary")),
    )(x, dy)


def _ffn_fwd(name, x, gn, wg, wu, wd, t):
    s, d = x.shape
    f8 = wg.shape[-1]

    def body(x_ref, g_ref, wg_ref, wu_ref, wd_ref, o_ref, gk_ref, uk_ref, h_scr, acc):
        k = pl.program_id(1)

        @pl.when(k == 0)
        def _():
            h_scr[...] = _rms(x_ref[...], g_ref[...]).astype(BF16)
            acc[...] = jnp.zeros_like(acc)

        hb = h_scr[...]
        gk = _dot(hb, wg_ref[0])
        uk = _dot(hb, wu_ref[0])
        gk_ref[0] = gk
        uk_ref[0] = uk
        a = gk * jax.nn.sigmoid(gk) * uk
        acc[...] += _dot(a.astype(BF16), wd_ref[0])

        @pl.when(k == N_DEV - 1)
        def _():
            o_ref[...] = x_ref[...] + 0.5 * acc[...]

    wspec = lambda shp: pl.BlockSpec((1,) + shp, lambda i, k: (k, 0, 0))
    act = pl.BlockSpec((1, t, f8), lambda i, k: (k, i, 0))
    act_sh = jax.ShapeDtypeStruct((N_DEV, s, f8), F32)
    return pl.pallas_call(
        body, name=name, grid=(s // t, N_DEV),
        in_specs=[pl.BlockSpec((t, d), lambda i, k: (i, 0)), pl.BlockSpec((1, d), lambda i, k: (0, 0)),
                  wspec((d, f8)), wspec((d, f8)), wspec((f8, d))],
        out_specs=[pl.BlockSpec((t, d), lambda i, k: (i, 0)), act, act],
        out_shape=[jax.ShapeDtypeStruct((s, d), F32), act_sh, act_sh],
        scratch_shapes=[pltpu.VMEM((t, d), BF16), pltpu.VMEM((t, d), F32)],
        compiler_params=_params(("arbitrary", "arbitrary")),
    )(x, gn, wg, wu, wd)


def _ffn_bwd(name, x, dy, gn, wg, wu, wd, gact, uact, t):
    s, d = x.shape
    f8 = wg.shape[-1]
    last = N_DEV - 1

    def body(x_ref, dy_ref, g_ref, wg_ref, wu_ref, wd_ref, gk_ref, uk_ref,
             dx_ref, dg_ref, dwg_ref, dwu_ref, dwd_ref, dh_scr):
        k, i = pl.program_id(0), pl.program_id(1)
        x = x_ref[...]
        rs = lax.rsqrt(jnp.mean(x * x, axis=-1, keepdims=True) + NORM_EPS)
        xn = x * rs
        hb = (xn * g_ref[...]).astype(BF16)
        dob = (0.5 * dy_ref[...]).astype(BF16)
        wgk, wuk, wdk = wg_ref[0], wu_ref[0], wd_ref[0]
        gk, uk = gk_ref[0], uk_ref[0]
        sg = jax.nn.sigmoid(gk)
        sk = gk * sg
        da = _dot(dob, wdk, NT2)
        du = (da * sk).astype(BF16)
        dg = (da * uk * (sg * (1.0 + gk * (1.0 - sg)))).astype(BF16)

        dwd_c = _dot((sk * uk).astype(BF16), dob, TN2)
        dwg_c = _dot(hb, dg, TN2)
        dwu_c = _dot(hb, du, TN2)
        dh = _dot(dg, wgk, NT2) + _dot(du, wuk, NT2)
        rows = pl.ds(pl.multiple_of(i * t, t), t)

        @pl.when(i == 0)
        def _():
            dwg_ref[0], dwu_ref[0], dwd_ref[0] = dwg_c, dwu_c, dwd_c

        @pl.when(i > 0)
        def _():
            dwg_ref[0] += dwg_c
            dwu_ref[0] += dwu_c
            dwd_ref[0] += dwd_c

        @pl.when(k == 0)
        def _():
            dh_scr[rows, :] = dh

        @pl.when(k > 0)
        def _():
            dh_scr[rows, :] += dh

        @pl.when(jnp.logical_and(k == last, i == 0))
        def _():
            dg_ref[...] = jnp.zeros_like(dg_ref)

        @pl.when(k == last)
        def _():
            dht = dh_scr[rows, :]
            dg_ref[...] += jnp.sum(dht * xn, axis=0, keepdims=True)
            dxn = dht * g_ref[...]
            dx_ref[...] = dy_ref[...] + rs * (dxn - xn * jnp.mean(dxn * xn, axis=-1, keepdims=True))

    wspec = lambda shp: pl.BlockSpec((1,) + shp, lambda k, i: (k, 0, 0))
    tile = pl.BlockSpec((t, d), lambda k, i: (i, 0))
    act = pl.BlockSpec((1, t, f8), lambda k, i: (k, i, 0))
    return pl.pallas_call(
        body, name=name, grid=(N_DEV, s // t),
        in_specs=[tile, tile, pl.BlockSpec((1, d), lambda k, i: (0, 0)), wspec((d, f8)), wspec((d, f8)), wspec((f8, d)),
                  act, act],
        out_specs=[pl.BlockSpec((t, d), lambda k, i: (jnp.where(k == last, i, 0), 0)),
                   pl.BlockSpec((1, d), lambda k, i: (0, 0)),
                   pl.BlockSpec((1, d, f8), lambda k, i: (k, 0, 0)), pl.BlockSpec((1, d, f8), lambda k, i: (k, 0, 0)),
                   pl.BlockSpec((1, f8, d), lambda k, i: (k, 0, 0))],
        out_shape=[jax.ShapeDtypeStruct((s, d), F32), jax.ShapeDtypeStruct((1, d), F32),
                   jax.ShapeDtypeStruct((N_DEV, d, f8), F32), jax.ShapeDtypeStruct((N_DEV, d, f8), F32),
                   jax.ShapeDtypeStruct((N_DEV, f8, d), F32)],
        scratch_shapes=[pltpu.VMEM((s, d), F32)],
        compiler_params=_params(("arbitrary", "arbitrary")),
    )(x, dy, gn, wg, wu, wd, gact, uact)


def _loss_head(y, target, t):
    s, d = y.shape

    def body(y_ref, t_ref, dy_ref, l_ref):
        i = pl.program_id(0)
        err = y_ref[...] - t_ref[...]
        dy_ref[...] = err * (1.0 / d)

        @pl.when(i == 0)
        def _():
            l_ref[...] = jnp.zeros_like(l_ref)

        l_ref[...] += 0.5 * jnp.sum(jnp.mean(err * err, axis=-1, keepdims=True), axis=0, keepdims=True)

    tile = pl.BlockSpec((t, d), lambda i: (i, 0))
    dy, l = pl.pallas_call(
        body, name="loss_head", grid=(s // t,), in_specs=[tile, tile],
        out_specs=[tile, pl.BlockSpec((1, 1), lambda i: (0, 0))],
        out_shape=[jax.ShapeDtypeStruct((s, d), F32), jax.ShapeDtypeStruct((1, 1), F32)],
        compiler_params=_params(("arbitrary",)),
    )(y, target)
    return dy, l[0, 0]


SB_KEY_TILE = 1024
SB_HEADS_PER_STEP = 4


def _sb_scan_mats():
    row = lax.broadcasted_iota(jnp.int32, (QBLK, QBLK), 0)
    col = lax.broadcasted_iota(jnp.int32, (QBLK, QBLK), 1)
    return (row > col).astype(F32).astype(BF16), (row < col).astype(F32).astype(BF16)


def _sb_tile_scan(x, mat, reverse):
    nsub = x.shape[1] // QBLK
    outs, carry = [None] * nsub, jnp.zeros((x.shape[0], 1), F32)
    for i in (reversed(range(nsub)) if reverse else range(nsub)):
        xs = x[:, i * QBLK:(i + 1) * QBLK]
        hi = xs.astype(BF16)
        lo = (xs - hi.astype(F32)).astype(BF16)
        outs[i] = _dot(hi, mat) + _dot(lo, mat) + carry
        carry = carry + jnp.sum(xs, axis=1, keepdims=True)
    return jnp.concatenate(outs, axis=1), carry


def _sb_before_query(n, t, kt):
    row = lax.broadcasted_iota(jnp.int32, (QBLK, kt), 0)
    col = lax.broadcasted_iota(jnp.int32, (QBLK, kt), 1)
    return t * kt + col < n * QBLK + row


def _sb_fwd(q, k, v):
    _, s, _ = q.shape
    scale = HEAD ** -0.5
    kt = min(SB_KEY_TILE, s)

    def body(q_ref, k_ref, v_ref, o_ref):
        n = pl.program_id(1)
        suffix, _ = _sb_scan_mats()
        n_tiles = lax.div(n, jnp.int32(kt // QBLK)) + 1
        heads = range(SB_HEADS_PER_STEP)
        qb = [(q_ref[h] * scale).astype(q_ref.dtype) for h in heads]

        def tile(t, carry, diagonal):
            rows = pl.ds(pl.multiple_of(t * kt, kt), kt)
            out = []
            for h in heads:
                c, acc = carry[h]
                z = _dot(qb[h], k_ref[h, rows, :], NT2)
                lk = _log_sigmoid(-z)
                log_beta = z + lk
                if diagonal:
                    ok = _sb_before_query(n, t, kt)
                    lk = jnp.where(ok, lk, 0.0)
                later, total = _sb_tile_scan(lk, suffix, True)
                w = jnp.exp(log_beta + later + c)
                if diagonal:
                    w = jnp.where(ok, w, 0.0)
                out.append((c + total, acc + _dot(w.astype(BF16), v_ref[h, rows, :])))
            return tuple(out)

        zero = (jnp.zeros((QBLK, 1), F32), jnp.zeros((QBLK, HEAD), F32))
        carry = tile(n_tiles - 1, (zero,) * len(heads), True)
        carry = lax.fori_loop(1, n_tiles, lambda jj, cr: tile(n_tiles - 1 - jj, cr, False), carry)
        for h in heads:
            o_ref[h] = carry[h][1]

    hp = SB_HEADS_PER_STEP
    return pl.pallas_call(
        body, name="sb_fwd", grid=(SB_HEADS // hp, s // QBLK),
        in_specs=[pl.BlockSpec((hp, QBLK, HEAD), lambda h, n: (h, n, 0)),
                  pl.BlockSpec((hp, s, HEAD), lambda h, n: (h, 0, 0)),
                  pl.BlockSpec((hp, s, HEAD), lambda h, n: (h, 0, 0))],
        out_specs=pl.BlockSpec((hp, QBLK, HEAD), lambda h, n: (h, n, 0)),
        out_shape=jax.ShapeDtypeStruct((SB_HEADS, s, HEAD), F32),
        compiler_params=_params(("arbitrary", "arbitrary")),
    )(q, k, v)


def _sb_bwd(q, k, v, do):
    _, s, _ = q.shape
    scale = HEAD ** -0.5
    kt = min(SB_KEY_TILE, s)

    def body(q_ref, k_ref, v_ref, do_ref, dq_ref, dk_ref, dv_ref, e_scr, beta_scr):
        n = pl.program_id(1)

        @pl.when(n == 0)
        def _():
            dk_ref[...] = jnp.zeros_like(dk_ref)
            dv_ref[...] = jnp.zeros_like(dv_ref)

        suffix, prefix = _sb_scan_mats()
        n_tiles = lax.div(n, jnp.int32(kt // QBLK)) + 1
        heads = range(SB_HEADS_PER_STEP)
        qb = [(q_ref[h] * scale).astype(q_ref.dtype) for h in heads]
        dob = [do_ref[h].astype(BF16) for h in heads]

        def weights(t, cs, diagonal):
            rows = pl.ds(pl.multiple_of(t * kt, kt), kt)
            out, stores = [], []
            for h in heads:
                vb = v_ref[h, rows, :]
                z = _dot(qb[h], k_ref[h, rows, :], NT2)
                lk = _log_sigmoid(-z)
                log_beta = z + lk
                if diagonal:
                    ok = _sb_before_query(n, t, kt)
                    lk = jnp.where(ok, lk, 0.0)
                later, total = _sb_tile_scan(lk, suffix, True)
                w = jnp.exp(log_beta + later + cs[h])
                if diagonal:
                    w = jnp.where(ok, w, 0.0)
                stores.append((w * _dot(dob[h], vb, NT2), jnp.exp(log_beta), _dot(w.astype(BF16), dob[h], TN2)))
                out.append(cs[h] + total)
            for h in heads:
                e_scr[h, t], beta_scr[h, t] = stores[h][0], stores[h][1]
                dv_ref[h, rows, :] += stores[h][2]
            return tuple(out)

        col0 = jnp.zeros((QBLK, 1), F32)
        cs = weights(n_tiles - 1, (col0,) * len(heads), True)
        lax.fori_loop(1, n_tiles, lambda jj, c: weights(n_tiles - 1 - jj, c, False), cs)

        def grads(t, carry, diagonal):
            rows = pl.ds(pl.multiple_of(t * kt, kt), kt)
            out, dks = [], []
            for h in heads:
                pc, dq = carry[h]
                kb = k_ref[h, rows, :]
                e, beta = e_scr[h, t], beta_scr[h, t]
                before, total = _sb_tile_scan(e, prefix, False)
                dz = e * (1.0 - beta) - beta * (before + pc)
                if diagonal:
                    dz = jnp.where(_sb_before_query(n, t, kt), dz, 0.0)
                dz = dz.astype(BF16)
                dks.append(_dot(dz, qb[h], TN2))
                out.append((pc + total, dq + _dot(dz, kb)))
            for h in heads:
                dk_ref[h, rows, :] += dks[h]
            return tuple(out)

        zero = (col0, jnp.zeros((QBLK, HEAD), F32))
        carry = lax.fori_loop(0, n_tiles - 1, lambda t, cr: grads(t, cr, False), (zero,) * len(heads))
        carry = grads(n_tiles - 1, carry, True)
        for h in heads:
            dq_ref[h] = carry[h][1] * scale

    hp = SB_HEADS_PER_STEP
    qspec = pl.BlockSpec((hp, QBLK, HEAD), lambda h, n: (h, n, 0))
    full = pl.BlockSpec((hp, s, HEAD), lambda h, n: (h, 0, 0))
    sh = jax.ShapeDtypeStruct((SB_HEADS, s, HEAD), F32)
    tiles_sh = (hp, s // kt, QBLK, kt)
    return pl.pallas_call(
        body, name="sb_bwd", grid=(SB_HEADS // hp, s // QBLK),
        in_specs=[qspec, full, full, qspec],
        out_specs=[qspec, full, full], out_shape=[sh, sh, sh],
        scratch_shapes=[pltpu.VMEM(tiles_sh, F32), pltpu.VMEM(tiles_sh, F32)],
        compiler_params=_params(("arbitrary", "arbitrary")),
    )(q, k, v, do)


def _t5_bucket_np(dist):
    max_exact = N_BUCKETS // 2
    d = np.maximum(dist, 1).astype(np.float32)
    large = max_exact + (np.log(d / np.float32(max_exact)) / np.float32(math.log(MAX_DISTANCE / max_exact))
                         * np.float32(N_BUCKETS - max_exact)).astype(np.int32)
    large = np.minimum(large, N_BUCKETS - 1)
    return np.where(dist < max_exact, dist, large)


def _dil_layout(s):
    assert all(s % (QBLK * r) == 0 and window // r == QBLK for window, r in DIL_PATTERNS)
    tiles, buckets = [], []
    i = np.arange(QBLK)[:, None]
    j = np.arange(QBLK)[None, :]
    for g, (window, r) in enumerate(DIL_PATTERNS):
        for off in (0, 1):
            dist = QBLK * off + i - j
            ok = (dist >= 0) & (dist <= window // r)
            tiles.append((g, off))
            buckets.append(np.where(ok, _t5_bucket_np(np.maximum(dist, 0) * r), -1).astype(np.int32))
    return tiles, np.stack(buckets)


def _bias_tiles(rel_bias, s):
    tiles, buckets = _dil_layout(s)
    nt = len(tiles)
    present = [sorted(set(np.unique(buckets[k]).tolist()) - {-1}) for k in range(nt)]

    def body(rel_ref, b_ref, o_ref):
        j = pl.program_id(0)
        for k, (g, _) in enumerate(tiles):
            bk = b_ref[k]
            tile = jnp.full((QBLK, QBLK), NEG_INF, F32)
            for b in present[k]:
                tile = jnp.where(bk == b, rel_ref[b, g * DIL_GROUP + j], tile)
            o_ref[0, k] = tile

    return pl.pallas_call(
        body, name="bias_tiles", grid=(DIL_GROUP,),
        in_specs=[pl.BlockSpec(memory_space=pltpu.SMEM), pl.BlockSpec((nt, QBLK, QBLK), lambda j: (0, 0, 0))],
        out_specs=pl.BlockSpec((1, nt, QBLK, QBLK), lambda j: (j, 0, 0, 0)),
        out_shape=jax.ShapeDtypeStruct((DIL_GROUP, nt, QBLK, QBLK), F32),
        compiler_params=_params(("arbitrary",)),
    )(rel_bias, jnp.asarray(buckets))


def _bias_tiles_bwd(dbias, s):
    tiles, buckets = _dil_layout(s)
    nt = len(tiles)
    present = [sorted(set(np.unique(buckets[k]).tolist()) - {-1}) for k in range(nt)]

    def body(d_ref, b_ref, o_ref):
        j = pl.program_id(0)

        @pl.when(j == 0)
        def _():
            for b in range(N_BUCKETS):
                for col in range(3 * DIL_GROUP):
                    o_ref[b, col] = jnp.float32(0.0)

        for k, (g, _) in enumerate(tiles):
            bk, dk = b_ref[k], d_ref[0, k]
            for b in present[k]:
                o_ref[b, g * DIL_GROUP + j] += jnp.sum(jnp.where(bk == b, dk, 0.0))

    return pl.pallas_call(
        body, name="bias_tiles_bwd", grid=(DIL_GROUP,),
        in_specs=[pl.BlockSpec((1, nt, QBLK, QBLK), lambda j: (j, 0, 0, 0)),
                  pl.BlockSpec((nt, QBLK, QBLK), lambda j: (0, 0, 0))],
        out_specs=pl.BlockSpec(memory_space=pltpu.SMEM),
        out_shape=jax.ShapeDtypeStruct((N_BUCKETS, 3 * DIL_GROUP), F32),
        compiler_params=_params(("arbitrary",)),
    )(dbias, jnp.asarray(buckets))


DIL_PAIRS_PER_STEP = 8


def _dil_rows(g, s, pair):
    _, r = DIL_PATTERNS[g]
    nb = s // (QBLK * r)
    c, n = lax.div(pair, jnp.int32(nb)), lax.rem(pair, jnp.int32(nb))
    start = c + (r * QBLK) * n
    before = jnp.where(n > 0, start - r * QBLK, start)
    if r == 1:
        return pl.ds(start, QBLK), pl.ds(before, QBLK), n > 0
    return pl.ds(start, QBLK, stride=r), pl.ds(before, QBLK, stride=r), n > 0


def _dil_logits(qb, k_ref, rows, before, has_before, b_ref):
    k0, k1 = k_ref[0, rows, :].astype(BF16), k_ref[0, before, :].astype(BF16)
    l0 = _dot(qb, k0, NT2) + b_ref[0, 0]
    l1 = jnp.where(has_before, _dot(qb, k1, NT2) + b_ref[0, 1], NEG_INF)
    return k0, k1, l0, l1


def _dil_group_specs(g, s):
    head = pl.BlockSpec((1, s, HEAD), lambda j, p: (DIL_GROUP * g + j, 0, 0))
    return [head, head, head, pl.BlockSpec((1, 2, QBLK, QBLK), lambda j, p: (j, g, 0, 0))]


def _dil_group_fwd(g, qn, kn, v, bias):
    _, s, _ = qn.shape
    scale = HEAD ** -0.5
    steps = (s // QBLK) // DIL_PAIRS_PER_STEP

    def body(q_ref, k_ref, v_ref, b_ref, o_ref):
        pairs = [_dil_rows(g, s, pl.program_id(1) * DIL_PAIRS_PER_STEP + u) for u in range(DIL_PAIRS_PER_STEP)]
        loaded = []
        for rows, before, has_before in pairs:
            qb = (q_ref[0, rows, :] * scale).astype(BF16)
            _, _, l0, l1 = _dil_logits(qb, k_ref, rows, before, has_before, b_ref)
            loaded.append((l0, l1, v_ref[0, rows, :].astype(BF16), v_ref[0, before, :].astype(BF16)))
        results = []
        for l0, l1, v0, v1 in loaded:
            m = jnp.max(jnp.maximum(l0, l1), axis=1, keepdims=True)
            p0, p1 = jnp.exp(l0 - m), jnp.exp(l1 - m)
            den = jnp.sum(p0 + p1, axis=1, keepdims=True)
            inv = 1.0 / den
            o = _dot((p0 * inv).astype(BF16), v0) + _dot((p1 * inv).astype(BF16), v1)
            results.append(jnp.concatenate([o, jnp.broadcast_to(m + jnp.log(den), (QBLK, HEAD))], axis=1))
        for (rows, _, _), res in zip(pairs, results):
            o_ref[0, rows, :] = res

    return pl.pallas_call(
        body, name="dil%d_fwd" % g, grid=(DIL_GROUP, steps), in_specs=_dil_group_specs(g, s),
        out_specs=pl.BlockSpec((1, s, 2 * HEAD), lambda j, p: (j, 0, 0)),
        out_shape=jax.ShapeDtypeStruct((DIL_GROUP, s, 2 * HEAD), F32),
        compiler_params=_params(("arbitrary", "arbitrary")),
    )(qn, kn, v, bias)


def _dil_group_bwd(g, qn, kn, v, bias, ol, dol, prev):
    _, s, _ = qn.shape
    scale = HEAD ** -0.5
    steps = (s // QBLK) // DIL_PAIRS_PER_STEP
    prev = list(prev) if prev is not None else []

    def body(q_ref, k_ref, v_ref, b_ref, ol_ref, dol_ref, *rest):
        dq_ref, dk_ref, dv_ref, db_ref = rest[-4:]

        @pl.when(pl.program_id(1) == 0)
        def _():
            for r in (dk_ref, dv_ref, db_ref):
                r[...] = jnp.zeros_like(r)

        pairs = [_dil_rows(g, s, pl.program_id(1) * DIL_PAIRS_PER_STEP + u) for u in range(DIL_PAIRS_PER_STEP)]
        loaded = []
        for rows, before, has_before in pairs:
            qb = (q_ref[0, rows, :] * scale).astype(BF16)
            k0, k1, l0, l1 = _dil_logits(qb, k_ref, rows, before, has_before, b_ref)
            v0, v1 = v_ref[0, rows, :].astype(BF16), v_ref[0, before, :].astype(BF16)
            loaded.append((qb, k0, k1, l0, l1, v0, v1, ol_ref[0, rows, :], dol_ref[0, rows, :]))
        grads = []
        for qb, k0, k1, l0, l1, v0, v1, out_lse, d_out_lse in loaded:
            o, lse = out_lse[:, :HEAD], out_lse[:, HEAD:HEAD + 1]
            do, dlse = d_out_lse[:, :HEAD], d_out_lse[:, HEAD:HEAD + 1]
            dob = do.astype(BF16)
            p0, p1 = jnp.exp(l0 - lse), jnp.exp(l1 - lse)
            shift = dlse - jnp.sum(do * o, axis=1, keepdims=True)
            dl0 = p0 * (_dot(dob, v0, NT2) + shift)
            dl1 = p1 * (_dot(dob, v1, NT2) + shift)
            dl0b, dl1b = dl0.astype(BF16), dl1.astype(BF16)
            grads.append(((_dot(dl0b, k0) + _dot(dl1b, k1)) * scale,
                          _dot(dl0b, qb, TN2), _dot(dl1b, qb, TN2),
                          _dot(p0.astype(BF16), dob, TN2), _dot(p1.astype(BF16), dob, TN2), dl0, dl1))
        db0 = functools.reduce(jnp.add, [gr[5] for gr in grads])
        db1 = functools.reduce(jnp.add, [gr[6] for gr in grads])
        for (rows, before, _), (dq, dk0, dk1, dv0, dv1, _, _) in zip(pairs, grads):
            dq_ref[0, rows, :] = dq
            dk_ref[0, rows, :] += dk0
            dk_ref[0, before, :] += dk1
            dv_ref[0, rows, :] += dv0
            dv_ref[0, before, :] += dv1
        db_ref[0, 0] += db0
        db_ref[0, 1] += db1

    head_out = pl.BlockSpec((1, s, HEAD), lambda j, p: (DIL_GROUP * g + j, 0, 0))
    rows128 = pl.BlockSpec((1, s, 2 * HEAD), lambda j, p: (j, 0, 0))
    full_sh = jax.ShapeDtypeStruct(qn.shape, F32)
    return pl.pallas_call(
        body, name="dil%d_bwd" % g, grid=(DIL_GROUP, steps),
        in_specs=_dil_group_specs(g, s) + [rows128, rows128] + [pl.BlockSpec(memory_space=pl.ANY)] * len(prev),
        out_specs=[head_out, head_out, head_out, pl.BlockSpec((1, 2, QBLK, QBLK), lambda j, p: (j, 0, 0, 0))],
        out_shape=[full_sh, full_sh, full_sh, jax.ShapeDtypeStruct((DIL_GROUP, 2, QBLK, QBLK), F32)],
        input_output_aliases={6 + i: i for i in range(len(prev))},
        compiler_params=_params(("arbitrary", "arbitrary")),
    )(qn, kn, v, bias, ol, dol, *prev)


@functools.partial(jax.custom_vjp, nondiff_argnums=(2,))
def _bdot(a, b, dims):
    return _dot(a.astype(BF16), b.astype(BF16), dims)


def _bdot_fwd(a, b, dims):
    return _bdot(a, b, dims), (a, b)


def _bdot_bwd(dims, res, dc):
    a, b = res
    nn, nt, tn = (NN2, NT2, TN2) if dims in (NN2, NT2, TN2) else (NN3, NT3, TN3)
    if dims == nn:
        return _bdot(dc, b, nt), _bdot(a, dc, tn)
    if dims == nt:
        return _bdot(dc, b, nn), _bdot(dc, a, tn)
    return _bdot(b, dc, nt), _bdot(a, dc, nn)


_bdot.defvjp(_bdot_fwd, _bdot_bwd)


def _ones_dot(ones, x, dims):
    o = ones.astype(BF16)
    hi = x.astype(BF16)
    r1 = x - hi.astype(F32)
    mid = r1.astype(BF16)
    lo = (r1 - mid.astype(F32)).astype(BF16)
    return _dot(o, hi, dims) + _dot(o, mid, dims) + _dot(o, lo, dims)


@jax.custom_vjp
def _prefix_sums(x):
    c = x.shape[1]
    row = lax.broadcasted_iota(jnp.int32, (x.shape[0], c, c), 1)
    col = lax.broadcasted_iota(jnp.int32, (x.shape[0], c, c), 2)
    return _ones_dot((row >= col).astype(F32), x, NN3)


def _prefix_sums_fwd(x):
    return _prefix_sums(x), None


def _prefix_sums_bwd(_, dy):
    c = dy.shape[1]
    row = lax.broadcasted_iota(jnp.int32, (dy.shape[0], c, c), 1)
    col = lax.broadcasted_iota(jnp.int32, (dy.shape[0], c, c), 2)
    return (_ones_dot((row <= col).astype(F32), dy, NN3),)


_prefix_sums.defvjp(_prefix_sums_fwd, _prefix_sums_bwd)


def _rwkv_chunk(s0, r, lw, kraw, v, ag, kk_w, ka_w, rk_w, lng, lnb):
    hb, c, _ = r.shape
    kk = kraw * kk_w
    kk = kk / jnp.maximum(jnp.sqrt(jnp.sum(kk * kk, axis=-1, keepdims=True)), 1e-12)
    k = kraw * (1.0 + (ag - 1.0) * ka_w)
    a = -kk
    b = kk * ag
    row = lax.broadcasted_iota(jnp.int32, (hb, c, c), 1)
    col = lax.broadcasted_iota(jnp.int32, (hb, c, c), 2)
    lower, strict = row >= col, row > col
    cum = _prefix_sums(lw)
    ecum, einv = jnp.exp(cum), jnp.exp(-cum)
    rt, kt, bt = r * ecum, k * einv, b * einv
    at = a * jnp.exp(cum - lw)
    ar = jnp.concatenate([at, rt], axis=1)
    scores = _bdot(ar, jnp.concatenate([bt, kt], axis=1), NT3)
    a_ab = jnp.where(strict, scores[:, :c, :c], 0.0)
    a_ak = jnp.where(strict, scores[:, :c, c:], 0.0)
    p_rb = jnp.where(lower, scores[:, c:, :c], 0.0)
    p_rk = jnp.where(lower, scores[:, c:, c:], 0.0)
    from_s0 = _bdot(ar, s0, NT3)
    rhs = from_s0[:, :c] + _bdot(a_ak, v, NN3)
    inv = (row == col).astype(F32) + a_ab
    pw = a_ab
    for _ in range(int(math.log2(c)) - 1):
        pw = _bdot(pw, pw, NN3)
        inv = inv + _bdot(inv, pw, NN3)
    u = _bdot(inv, rhs, NN3)
    uv = jnp.concatenate([u, v], axis=1)
    y = from_s0[:, c:] + _bdot(jnp.concatenate([p_rb, p_rk], axis=2), uv, NN3)
    cum_end = cum[:, c - 1:c, :]
    dec = jnp.exp(cum_end - cum)
    s_end = s0 * jnp.exp(cum_end) + _bdot(uv, jnp.concatenate([b * dec, k * dec], axis=1), TN3)
    mu = jnp.mean(y, axis=-1, keepdims=True)
    var = jnp.mean(jnp.square(y - mu), axis=-1, keepdims=True)
    z = (y - mu) * lax.rsqrt(var + GN_EPS) * lng + lnb + jnp.sum(r * k * rk_w, axis=-1, keepdims=True) * v
    return z, s_end


def _rwkv_specs(nc, rev):
    cidx = (lambda c: nc - 1 - c) if rev else (lambda c: c)
    seq = pl.BlockSpec((RW_HB, RW_CHUNK, HEAD), lambda hg, c: (hg, cidx(c), 0))
    par = pl.BlockSpec((RW_HB, 1, HEAD), lambda hg, c: (hg, 0, 0))
    st = pl.BlockSpec((1, RW_HB, HEAD, HEAD), lambda hg, c: (cidx(c), hg, 0, 0))
    return seq, par, st


def _rwkv_fwd(seqs, pars):
    s = seqs[0].shape[1]
    nc = s // RW_CHUNK

    def body(*refs):
        seq_refs, par_refs = refs[:5], refs[5:10]
        z_ref, st_ref, state = refs[10:]
        c = pl.program_id(1)

        @pl.when(c == 0)
        def _():
            state[...] = jnp.zeros_like(state)

        s0 = state[...]
        st_ref[0] = s0
        z, s_end = _rwkv_chunk(s0, *[r[...] for r in seq_refs], *[r[...] for r in par_refs])
        z_ref[...] = z
        state[...] = s_end

    seq, par, st = _rwkv_specs(nc, False)
    return pl.pallas_call(
        body, name="rwkv_fwd", grid=(N_HEADS // RW_HB, nc),
        in_specs=[seq] * 5 + [par] * 5, out_specs=[seq, st],
        out_shape=[jax.ShapeDtypeStruct((N_HEADS, s, HEAD), F32), jax.ShapeDtypeStruct((nc, N_HEADS, HEAD, HEAD), F32)],
        scratch_shapes=[pltpu.VMEM((RW_HB, HEAD, HEAD), F32)],
        compiler_params=_params(("arbitrary", "arbitrary")),
    )(*seqs, *pars)


def _rwkv_bwd(seqs, pars, states, dz):
    s = seqs[0].shape[1]
    nc = s // RW_CHUNK

    def body(*refs):
        seq_refs, par_refs = refs[:5], refs[5:10]
        st_ref, dz_ref = refs[10:12]
        dseq_refs, dpar_refs, dstate = refs[12:17], refs[17:22], refs[22]
        c = pl.program_id(1)

        @pl.when(c == 0)
        def _():
            dstate[...] = jnp.zeros_like(dstate)
            for r in dpar_refs:
                r[...] = jnp.zeros_like(r)

        _, vjp = jax.vjp(_rwkv_chunk, st_ref[0], *[r[...] for r in seq_refs], *[r[...] for r in par_refs])
        g = vjp((dz_ref[...], dstate[...]))
        dstate[...] = g[0]
        for r, gs in zip(dseq_refs, g[1:6]):
            r[...] = gs
        for r, gp in zip(dpar_refs, g[6:]):
            r[...] += gp

    seq, par, st = _rwkv_specs(nc, True)
    seq_sh = jax.ShapeDtypeStruct((N_HEADS, s, HEAD), F32)
    par_sh = jax.ShapeDtypeStruct((N_HEADS, 1, HEAD), F32)
    outs = pl.pallas_call(
        body, name="rwkv_bwd", grid=(N_HEADS // RW_HB, nc),
        in_specs=[seq] * 5 + [par] * 5 + [st, seq],
        out_specs=[seq] * 5 + [par] * 5, out_shape=[seq_sh] * 5 + [par_sh] * 5,
        scratch_shapes=[pltpu.VMEM((RW_HB, HEAD, HEAD), F32)],
        compiler_params=_params(("arbitrary", "arbitrary")),
    )(*seqs, *pars, states, dz)
    return list(outs[:5]), list(outs[5:])


def _norm_fn(x, g):
    return (_rms(x, g),)


def _attn_prep_fn(proj, qn_w, kn_w):
    a, b = SB_HEADS, 3 * DIL_GROUP
    return (proj[0:a], proj[a:2 * a], proj[2 * a:3 * a],
            _rms(proj[3 * a:3 * a + b], qn_w), _rms(proj[3 * a + b:3 * a + 2 * b], kn_w), proj[3 * a + 2 * b:])


def _attn_merge_fn(o_sb, ol0, ol1, ol2):
    groups = (ol0, ol1, ol2)
    merged = []
    for j in range(DIL_GROUP):
        lses = [ol[j][:, HEAD:HEAD + 1] for ol in groups]
        m = functools.reduce(jnp.maximum, lses)
        es = [jnp.exp(l - m) for l in lses]
        inv = 1.0 / functools.reduce(jnp.add, es)
        merged.append(functools.reduce(jnp.add, [(e * inv) * ol[j][:, :HEAD] for e, ol in zip(es, groups)]))
    return (jnp.concatenate([_heads_to_nat(o_sb)] + merged, axis=-1),)


def _rw_mix_fn(x, xp, gn, mix, w0, w1, w2, a0, a1, a2, g1, g2):
    h = _rms(x, gn)
    xx = _rms(xp, gn) - h
    xr, xw, xk, xv, xa, xg = [h + xx * mix[i:i + 1] for i in range(6)]
    w_log = -jax.nn.softplus(-(w0 + _mm(jnp.tanh(_mm(xw, w1)), w2))) - 0.5
    lw = -jnp.exp(w_log)
    ag = jax.nn.sigmoid(a0 + _mm(_mm(xa, a1), a2))
    gate = _mm(jax.nn.sigmoid(_mm(xg, g1)), g2)
    return xr, xk, xv, _nat_to_heads(lw), _nat_to_heads(ag), gate


def _rw_gate_fn(z, gate):
    return (_heads_to_nat(z) * gate,)


def _adamw_update(w_ref, m_ref, v_ref, g_ref, go_ref, d_ref, mo_ref, vo_ref):
    g = g_ref[0].astype(F32)
    for j in range(1, N_DEV):
        g = g + g_ref[j].astype(F32)
    mn = ADAM_B1 * m_ref[...] + (1.0 - ADAM_B1) * g
    vn = ADAM_B2 * v_ref[...] + (1.0 - ADAM_B2) * jnp.square(g)
    m_hat = mn / (1.0 - ADAM_B1 ** ADAM_STEP)
    v_hat = vn / (1.0 - ADAM_B2 ** ADAM_STEP)
    go_ref[...] = g
    d_ref[...] = -ADAM_LR * (m_hat / (jnp.sqrt(v_hat) + ADAM_EPS) + ADAM_WD * w_ref[...])
    mo_ref[...] = mn
    vo_ref[...] = vn


def _adamw_many(name, items):
    n = len(items)

    def body(*refs):
        for i in range(n):
            _adamw_update(*refs[4 * i:4 * i + 4], *refs[4 * n + 4 * i:4 * n + 4 * i + 4])

    vmem = pl.BlockSpec(memory_space=pltpu.VMEM)
    outs = pl.pallas_call(
        body, name=name, in_specs=[vmem] * (4 * n), out_specs=[vmem] * (4 * n),
        out_shape=[jax.ShapeDtypeStruct(w.shape, F32) for w, _, _, _ in items for _ in range(4)],
        compiler_params=_params(),
    )(*[a for item in items for a in item])
    return [list(outs[4 * i:4 * i + 4]) for i in range(n)]


def _adamw(name, w, m, v, gparts, row0=0, prev=None):
    big_r, c = w.shape
    r = gparts.shape[1]
    tr = r
    if r % 8 == 0:
        tr = max(t for t in range(8, r + 1, 8) if r % t == 0 and (t * c * 4 <= (1 << 20) or t == 8))
    assert row0 % tr == 0 and (r == big_r or r % 8 == 0)
    off = row0 // tr

    def body(w_ref, m_ref, v_ref, g_ref, *rest):
        _adamw_update(w_ref, m_ref, v_ref, g_ref, *rest[-4:])

    tile = pl.BlockSpec((tr, c), lambda i: (i + off, 0))
    sh = jax.ShapeDtypeStruct((big_r, c), F32)
    prev = list(prev) if prev is not None else []
    return pl.pallas_call(
        body, name=name, grid=(r // tr,),
        in_specs=([tile, tile, tile, pl.BlockSpec((N_DEV, tr, c), lambda i: (0, i, 0))]
                  + [pl.BlockSpec(memory_space=pl.ANY)] * len(prev)),
        out_specs=[tile] * 4, out_shape=[sh] * 4,
        input_output_aliases={4 + j: j for j in range(len(prev))},
        compiler_params=_params(("arbitrary",)),
    )(w, m, v, gparts, *prev)


def _col_blocks_to_nat(g):
    return jnp.moveaxis(g, 0, 1).reshape(g.shape[1], -1)


def _nat_to_col_blocks(a):
    return jnp.moveaxis(a.reshape(a.shape[0], N_DEV, -1), 1, 0)


AG_GROUPS = ("f00", "att", "f01", "f10", "rw", "f11")
RS_GROUPS = ("f11", "rw", "f10", "f01", "f00", "att")
BF16_GRAD_GROUPS = ("att", "f00")
RW_SHARDED = ('rw_mix', 'rw_w0', 'rw_w1', 'rw_w2', 'rw_a0', 'rw_a1', 'rw_a2', 'rw_g1', 'rw_g2', 'rw_kk', 'rw_ka',
              'rw_wr', 'rw_wk', 'rw_wv', 'rw_wo', 'rw_lnx_g', 'rw_lnx_b')


def _step(x, target, rep, get, put):
    tied = lambda a, zero: a + zero[0, 0].astype(a.dtype)
    s, d = x.shape
    tf = min(512, s)
    tw = min(1024, s)
    dw_cols = 1024
    tt = min(256, s)
    row = lambda a: a.reshape(1, -1)
    mix_norm = rep["mix_norm"]
    ffw = {(0, 0): get("f00", None)}
    ffn_norm = _col_blocks_to_nat(ffw[(0, 0)]["ffn_norm"].reshape(N_DEV, 4, -1))

    acts = {}

    def ffn(nm, xin, l, h):
        g = ffw[(l, h)]
        out, *acts[(l, h)] = _ffn_fwd(nm, xin, ffn_norm[2 * l + h][None], g["gate"], g["up"], g["down"], min(2 * tf, s))
        return out

    x1 = ffn("ffn00_fwd", x, 0, 0)
    att = get("att", x1)
    w_in = _col_blocks_to_nat(att["attn_w_in"])
    w_out = _col_blocks_to_nat(att["attn_w_out"])
    (h0,) = _tile_fwd("mixnorm0_fwd", _norm_fn, [(x1, "nat")], [mix_norm[0:1]], [((s, d), BF16, "nat")], tt)
    proj = _linear_fwd("attn_in_fwd", h0, w_in, tf, out_layout="hm")
    bias = _bias_tiles(rep["rel_bias"], s)
    prep_pars = [rep["attn_q_norm"], rep["attn_k_norm"]]
    sb_sh, dl_sh = (SB_HEADS, s, HEAD), (3 * DIL_GROUP, s, HEAD)
    sq, sk, sv, qn, kn, vd = _tile_fwd("attn_prep_fwd", _attn_prep_fn, [(proj, "hm")], prep_pars,
                                       [(sb_sh, BF16, "hm")] * 3 + [(dl_sh, F32, "hm")] * 3, tt // 2)
    o_sb = _sb_fwd(sq, sk, sv)
    ols = [_dil_group_fwd(g, qn, kn, vd, bias) for g in range(3)]
    merge_tiled = [(o_sb, "hm")] + [(ol, "hm") for ol in ols]
    (merged,) = _tile_fwd("merge_fwd", _attn_merge_fn, merge_tiled, [], [((s, 512), BF16, "nat")], tt)
    x2 = _linear_fwd("attn_out_fwd", merged, w_out, tf, residual=x1)
    ffw[(0, 1)] = get("f01", x2)
    x3 = ffn("ffn01_fwd", x2, 0, 1)
    ffw[(1, 0)] = get("f10", x3)
    x4 = ffn("ffn10_fwd", x3, 1, 0)
    rw = get("rw", x4)
    rw_mix = _col_blocks_to_nat(rw["rw_mix"])
    rw_w1, rw_a1, rw_g1 = (rw[k].reshape(d, -1) for k in ("rw_w1", "rw_a1", "rw_g1"))
    rw_w2, rw_a2, rw_g2 = (_col_blocks_to_nat(rw[k]) for k in ("rw_w2", "rw_a2", "rw_g2"))
    rw_w0, rw_a0 = row(rw["rw_w0"]), row(rw["rw_a0"])
    head_par = lambda a: a.reshape(N_HEADS, 1, HEAD)
    scan_pars = [head_par(rw["rw_kk"]), head_par(rw["rw_ka"]), head_par(rep["rw_rk"]),
                 head_par(rw["rw_lnx_g"]), head_par(rw["rw_lnx_b"])]
    w_rkv = [rw[k].reshape(d, d) for k in ("rw_wr", "rw_wk", "rw_wv")]
    w_o = rw["rw_wo"].reshape(d, d)
    x4p = jnp.pad(x4, ((1, 0), (0, 0)))[:-1]
    mix_tiled = [(x4, "nat"), (x4p, "nat")]
    mix_pars = [mix_norm[1:2], rw_mix, rw_w0, rw_w1, rw_w2, rw_a0, rw_a1, rw_a2, rw_g1, rw_g2]
    hm_sh = (N_HEADS, s, HEAD)
    xr, xk, xv, lw, ag, gate = _tile_fwd(
        "rw_mix_fwd", _rw_mix_fn, mix_tiled, mix_pars,
        [((s, d), BF16, "nat")] * 3 + [(hm_sh, F32, "hm")] * 2 + [((s, d), F32, "nat")], tt)
    r_h, k_h, v_h = [_linear_fwd("rw_%s_fwd" % nm, xi, wi, tf, out_layout="hm")
                     for nm, xi, wi in zip("rkv", (xr, xk, xv), w_rkv)]
    scan_seqs = [r_h, lw, k_h, v_h, ag]
    z, states = _rwkv_fwd(scan_seqs, scan_pars)
    (zg,) = _tile_fwd("rw_gate_fwd", _rw_gate_fn, [(z, "hm"), (gate, "nat")], [], [((s, d), BF16, "nat")], tt)
    x5 = _linear_fwd("rw_out_fwd", zg, w_o, tf, residual=x4)
    ffw[(1, 1)] = get("f11", x5)
    y = ffn("ffn11_fwd", x5, 1, 1)
    dy, loss = _loss_head(y, target, tf)

    G = {}
    dgn = {}

    def fb(nm, group, xin, dout, l, h, zero=None, extra=None):
        g = ffw[(l, h)]
        gn = ffn_norm[2 * l + h][None]
        dxin, dgn[(l, h)], dwg, dwu, dwd = _ffn_bwd(nm, xin, dout, gn if zero is None else tied(gn, zero),
                                                   g["gate"], g["up"], g["down"], *acts[(l, h)], tf)
        shard = {"gate": dwg, "up": dwu, "down": dwd}
        if extra is not None:
            shard.update(extra())
        return dxin, put(group, {}, shard)

    dx5, zero = fb("ffn11_bwd", "f11", x5, dy, 1, 1)
    dzg = _linear_dx("rw_out_dx", dx5, w_o, tf)
    G["rw_wo"] = _linear_dw("rw_out_dw", zg, dx5, tw, dw_cols)
    (dz, dgate), _ = _tile_bwd("rw_gate_bwd", _rw_gate_fn, [(z, "hm"), (gate, "nat")], [], [(dzg, "nat")], tt, [True, True])
    (dr_h, dlw, dk_h, dv_h, dag), dscan = _rwkv_bwd(scan_seqs, [tied(scan_pars[0], zero)] + scan_pars[1:], states, dz)
    drkv = (dr_h, dk_h, dv_h)
    for k, gpar in zip(("rw_kk", "rw_ka", "rw_rk", "rw_lnx_g", "rw_lnx_b"), dscan):
        G[k] = gpar
    dxs = []
    for j, (nm, xi, wi) in enumerate(zip("rkv", (xr, xk, xv), w_rkv)):
        dxs.append(_linear_dx("rw_%s_dx" % nm, drkv[j], wi, tf, dy_layout="hm"))
        G["rw_w" + nm] = _linear_dw("rw_%s_dw" % nm, xi, drkv[j], tw, dw_cols, dy_layout="hm")
    (dx4a, dx4p), dmix = _tile_bwd(
        "rw_mix_bwd", _rw_mix_fn, mix_tiled, mix_pars,
        [(dxs[0], "nat"), (dxs[1], "nat"), (dxs[2], "nat"), (dlw, "hm"), (dag, "hm"), (dgate, "nat")],
        tt, [True, True], adds=[dx5, None])
    d_mixn1 = dmix[0]
    for k, gpar in zip(("rw_mix", "rw_w0", "rw_w1", "rw_w2", "rw_a0", "rw_a1", "rw_a2", "rw_g1", "rw_g2"), dmix[1:]):
        G[k] = gpar
    dx4 = dx4a + jnp.pad(dx4p[1:], ((0, 1), (0, 0)))
    for k in ("rw_mix", "rw_w2", "rw_a2", "rw_g2"):
        G[k] = _nat_to_col_blocks(G[k])
    for k in ("rw_w1", "rw_a1", "rw_g1", "rw_wr", "rw_wk", "rw_wv", "rw_wo"):
        G[k] = G[k].reshape(N_DEV, d // N_DEV, -1)
    for k in ("rw_w0", "rw_a0", "rw_kk", "rw_ka", "rw_lnx_g", "rw_lnx_b"):
        G[k] = G[k].reshape(N_DEV, 1, d // N_DEV)
    zero = put("rw", {"rw_rk": G["rw_rk"].reshape(N_HEADS, HEAD)}, {k: G[k] for k in RW_SHARDED})
    dx3, zero = fb("ffn10_bwd", "f10", x3, dx4, 1, 0, zero)
    dx2, zero = fb("ffn01_bwd", "f01", x2, dx3, 0, 1, zero)
    dmerged = _linear_dx("attn_out_dx", dx2, tied(w_out, zero), tf)
    (do_sb, *dols), _ = _tile_bwd("merge_bwd", _attn_merge_fn, merge_tiled, [], [(dmerged, "nat")], tt, [True] * 4)
    dq_sb, dk_sb, dv_sb = _sb_bwd(sq, sk, sv, do_sb)
    dil_grads, dbias = None, []
    for g in range(3):
        *dil_grads, db = _dil_group_bwd(g, qn, kn, vd, bias, ols[g], dols[g], dil_grads)
        dbias.append(db)
    dqn, dkn, dvd = dil_grads
    dbias = jnp.concatenate(dbias, axis=1)
    (dproj,), (dqn_w, dkn_w) = _tile_bwd(
        "attn_prep_bwd", _attn_prep_fn, [(proj, "hm")], prep_pars,
        [(dq_sb, "hm"), (dk_sb, "hm"), (dv_sb, "hm"), (dqn, "hm"), (dkn, "hm"), (dvd, "hm")], tt // 2, [True])
    dh0 = _linear_dx("attn_in_dx", dproj, w_in, tf, dy_layout="hm")
    (dx1,), (d_mixn0,) = _tile_bwd("mixnorm0_bwd", _norm_fn, [(x1, "nat")], [mix_norm[0:1]], [(dh0, "nat")], tt,
                                   [True], adds=[dx2])
    order = [(0, 0), (0, 1), (1, 0), (1, 1)]
    norm_grads = lambda: {"ffn_norm": _nat_to_col_blocks(jnp.concatenate([dgn[o] for o in order], axis=0))}
    dx0, zero = fb("ffn00_bwd", "f00", x, dx1, 0, 0, extra=norm_grads)
    G["attn_w_out"] = _linear_dw("attn_out_dw", tied(merged, zero), dx2, tw, dw_cols)
    G["attn_w_in"] = _linear_dw("attn_in_dw", tied(h0, zero), dproj, tw, dw_cols, dy_layout="hm")
    rep_grads = {"mix_norm": jnp.concatenate([d_mixn0, d_mixn1], axis=0), "rel_bias": _bias_tiles_bwd(dbias, s),
                 "attn_q_norm": dqn_w, "attn_k_norm": dkn_w}
    zero = put("att", rep_grads, {k: _nat_to_col_blocks(G[k]) for k in ("attn_w_in", "attn_w_out")})
    return loss, dx0, zero


WEIGHTS = ['ffn_norm', 'ffn_w_gate', 'ffn_w_up', 'ffn_w_down', 'mix_norm', 'rel_bias', 'attn_w_in', 'attn_q_norm',
           'attn_k_norm', 'attn_w_out', 'rw_mix', 'rw_w0', 'rw_w1', 'rw_w2', 'rw_a0', 'rw_a1', 'rw_a2', 'rw_g1', 'rw_g2',
           'rw_kk', 'rw_ka', 'rw_rk', 'rw_wr', 'rw_wk', 'rw_wv', 'rw_wo', 'rw_lnx_g', 'rw_lnx_b']
REPLICATED = ('mix_norm', 'rel_bias', 'attn_q_norm', 'attn_k_norm', 'rw_rk')
BF16_WEIGHTS = ('ffn_w_gate', 'ffn_w_up', 'ffn_w_down', 'attn_w_in', 'attn_w_out', 'rw_wr', 'rw_wk', 'rw_wv', 'rw_wo')


def kernel(x, ffn_norm, ffn_w_gate, ffn_w_up, ffn_w_down, mix_norm, rel_bias, attn_w_in, attn_q_norm, attn_k_norm, attn_w_out, rw_mix, rw_w0, rw_w1, rw_w2, rw_a0, rw_a1, rw_a2, rw_g1, rw_g2, rw_kk, rw_ka, rw_rk, rw_wr, rw_wk, rw_wv, rw_wo, rw_lnx_g, rw_lnx_b, loss_target, m_ffn_norm, m_ffn_w_gate, m_ffn_w_up, m_ffn_w_down, m_mix_norm, m_rel_bias, m_attn_w_in, m_attn_q_norm, m_attn_k_norm, m_attn_w_out, m_rw_mix, m_rw_w0, m_rw_w1, m_rw_w2, m_rw_a0, m_rw_a1, m_rw_a2, m_rw_g1, m_rw_g2, m_rw_kk, m_rw_ka, m_rw_rk, m_rw_wr, m_rw_wk, m_rw_wv, m_rw_wo, m_rw_lnx_g, m_rw_lnx_b, v_ffn_norm, v_ffn_w_gate, v_ffn_w_up, v_ffn_w_down, v_mix_norm, v_rel_bias, v_attn_w_in, v_attn_q_norm, v_attn_k_norm, v_attn_w_out, v_rw_mix, v_rw_w0, v_rw_w1, v_rw_w2, v_rw_a0, v_rw_a1, v_rw_a2, v_rw_g1, v_rw_g2, v_rw_kk, v_rw_ka, v_rw_rk, v_rw_wr, v_rw_wk, v_rw_wv, v_rw_wo, v_rw_lnx_g, v_rw_lnx_b):
    args = locals()
    w = {k: args[k] for k in WEIGHTS}
    cast = lambda k, a: a.astype(BF16) if k in BF16_WEIGHTS else a

    sources = {}
    for l, h in ((0, 0), (0, 1), (1, 0), (1, 1)):
        sources["f%d%d" % (l, h)] = {"gate": cast("ffn_w_gate", ffn_w_gate[l, h]), "up": cast("ffn_w_up", ffn_w_up[l, h]),
                                     "down": cast("ffn_w_down", ffn_w_down[l, h])}
    sources["f00"]["ffn_norm"] = ffn_norm
    drop_lead = lambda a: a[0] if a.ndim == 3 else a
    sources["att"] = {k: cast(k, w[k][0]) for k in ("attn_w_in", "attn_w_out")}
    sources["rw"] = {k: cast(k, drop_lead(w[k])) for k in RW_SHARDED}
    ag, token = {}, None
    for group in AG_GROUPS:
        names, arrays = list(sources[group]), list(sources[group].values())
        if token is not None:
            arrays[0] = arrays[0] + token[0, 0].astype(arrays[0].dtype)
        ag[group] = (names, _exchange_start("ag_start_" + group, arrays, []))
        token = ag[group][1]["token"]
    last_ag_token = token

    def get(group, after):
        names, started = ag[group]
        gathered, _ = _exchange_wait("ag_wait_" + group, started, last_ag_token if after is None else after)
        return dict(zip(names, gathered))

    rs = {}

    def put(group, rep_grads, shard_grads):
        if group in BF16_GRAD_GROUPS:
            shard_grads = {k: v.astype(BF16) for k, v in shard_grads.items()}
        started = _exchange_start("rs_start_" + group, list(rep_grads.values()), list(shard_grads.values()))
        rs[group] = (list(rep_grads), list(shard_grads), started)
        return started["token"]

    loss, dx, last_zero = _step(x[0], loss_target[0], {k: w[k] for k in REPLICATED}, get, put)
    loss = lax.psum(loss, MESH_AXES)

    results = {}
    ffn_prev = {}

    def update(k, parts, row0=0, prev=None):
        c = w[k].shape[-1]
        as2d = lambda a: a.reshape(-1, c)
        return _adamw("adamw_%s_%d" % (k, row0), as2d(w[k]), as2d(args["m_" + k]), as2d(args["v_" + k]),
                      parts.reshape(N_DEV, -1, c), row0, prev)

    def update_small(name, named_parts):
        items = []
        for k, parts in named_parts:
            c = w[k].shape[-1]
            items.append((w[k].reshape(-1, c), args["m_" + k].reshape(-1, c), args["v_" + k].reshape(-1, c),
                          parts.reshape(N_DEV, -1, c)))
        for (k, _), res in zip(named_parts, _adamw_many(name, items)):
            results[k] = res

    after = last_zero
    for group in RS_GROUPS:
        rep_names, shard_names, started = rs[group]
        rep_parts, shard_parts = _exchange_wait("rs_wait_" + group, started, after)
        small = []
        for k, parts in list(zip(rep_names, rep_parts)) + list(zip(shard_names, shard_parts)):
            if k in ("gate", "up", "down"):
                full = "ffn_w_" + k
                piece = 2 * int(group[1]) + int(group[2])
                ffn_prev[full] = update(full, parts, piece * parts.shape[1], ffn_prev.get(full))
                results[full] = ffn_prev[full]
                after = results[full][0]
            elif k == "attn_w_in":
                results[k] = update(k, parts)
                after = results[k][0]
            else:
                small.append((k, parts))
        if small:
            update_small("adamw_small_" + group, small)
            after = results[small[0][0]][0]

    outs = [[results[k][j].reshape(w[k].shape) for k in WEIGHTS] for j in range(4)]
    return (loss, dx[None], *outs[0], *outs[1], *outs[2], *outs[3])
```

```python
import functools
import math

import numpy as np
import jax
import jax.numpy as jnp
from jax import lax
from jax.experimental import pallas as pl
from jax.experimental.pallas import tpu as pltpu

F32, BF16 = jnp.float32, jnp.bfloat16

N_DEV = 8
D_MODEL = 1024
HEAD = 64
N_HEADS = 16
SB_HEADS = 4
DIL_GROUP = 4
DIL_PATTERNS = ((128, 1), (512, 4), (2048, 16))
QBLK = 128
N_BUCKETS = 32
MAX_DISTANCE = 2048
NORM_EPS = 1e-6
GN_EPS = 64e-5
NEG_INF = -1e30
RW_CHUNK = 64
RW_HB = 16
ADAM_LR, ADAM_B1, ADAM_B2, ADAM_EPS, ADAM_WD, ADAM_STEP = 0.001, 0.9, 0.999, 1e-08, 0.01, 10
MESH_AXES = ("x", "y", "c")
VMEM_LIMIT_BYTES = 56 * 1024 * 1024

NN2 = (((1,), (0,)), ((), ()))
NT2 = (((1,), (1,)), ((), ()))
TN2 = (((0,), (0,)), ((), ()))
NN3 = (((2,), (1,)), ((0,), (0,)))
NT3 = (((2,), (2,)), ((0,), (0,)))
TN3 = (((1,), (1,)), ((0,), (0,)))


def _dot(a, b, dims=NN2, prec=None):
    return lax.dot_general(a, b, dims, precision=prec, preferred_element_type=F32)


def _params(sem=None):
    return pltpu.CompilerParams(dimension_semantics=sem, vmem_limit_bytes=VMEM_LIMIT_BYTES)


@jax.custom_vjp
def _mm(x, w):
    return _dot(x.astype(BF16), w.astype(BF16))


def _mm_fwd(x, w):
    return _mm(x, w), (x, w)


def _mm_bwd(res, dy):
    x, w = res
    dyb = dy.astype(BF16)
    return (_dot(dyb, w.astype(BF16), NT2).astype(x.dtype), _dot(x.astype(BF16), dyb, TN2).astype(w.dtype))


_mm.defvjp(_mm_fwd, _mm_bwd)


def _rms(x, g):
    return x * lax.rsqrt(jnp.mean(x * x, axis=-1, keepdims=True) + NORM_EPS) * g


def _log_sigmoid(z):
    return jnp.minimum(z, 0.0) - jnp.log(1.0 + jnp.exp(-jnp.abs(z)))


def _heads_to_nat(v3):
    return jnp.concatenate([v3[h] for h in range(v3.shape[0])], axis=-1)


def _nat_to_heads(v2):
    return jnp.stack([v2[:, h * HEAD:(h + 1) * HEAD] for h in range(v2.shape[1] // HEAD)], axis=0)


def _exchange(name, gathers, scatters):
    n_g = len(gathers)
    arrays = list(gathers) + list(scatters)
    n = len(arrays)
    out_shape = [jax.ShapeDtypeStruct((N_DEV,) + a.shape, a.dtype) for a in gathers]
    out_shape += [jax.ShapeDtypeStruct(a.shape, a.dtype) for a in scatters]

    def body(*refs):
        ins, outs = refs[:n], refs[n:2 * n]
        send_sems, recv_sems, local_sems = refs[2 * n:]
        x, y, c = lax.axis_index("x"), lax.axis_index("y"), lax.axis_index("c")
        me = 4 * x + 2 * y + c

        def src(i, idx):
            return ins[i] if i < n_g else ins[i].at[idx]

        local = [pltpu.make_async_copy(src(i, me), outs[i].at[me], local_sems.at[i]) for i in range(n)]
        for cp in local:
            cp.start()
        remote = []
        for m in range(1, N_DEV):
            px, py, pc = x ^ ((m >> 2) & 1), y ^ ((m >> 1) & 1), c ^ (m & 1)
            peer = 4 * px + 2 * py + pc
            for i in range(n):
                cp = pltpu.make_async_remote_copy(
                    src_ref=src(i, peer), dst_ref=outs[i].at[me],
                    send_sem=send_sems.at[i, m - 1], recv_sem=recv_sems.at[i, m - 1],
                    device_id=(px, py, pc), device_id_type=pl.DeviceIdType.MESH)
                cp.start()
                arrival = pltpu.make_async_remote_copy(
                    src_ref=src(i, peer), dst_ref=outs[i].at[peer],
                    send_sem=send_sems.at[i, m - 1], recv_sem=recv_sems.at[i, m - 1],
                    device_id=(px, py, pc), device_id_type=pl.DeviceIdType.MESH)
                remote.append((cp, arrival))
        for cp, arrival in remote:
            cp.wait_send()
            arrival.wait_recv()
        for cp in local:
            cp.wait()

    hbm = pl.BlockSpec(memory_space=pltpu.HBM)
    outs = pl.pallas_call(
        body, name=name, out_shape=out_shape,
        in_specs=[hbm] * n, out_specs=[hbm] * n,
        scratch_shapes=[pltpu.SemaphoreType.DMA((n, N_DEV - 1)), pltpu.SemaphoreType.DMA((n, N_DEV - 1)),
                        pltpu.SemaphoreType.DMA((n,))],
    )(*arrays)
    return list(outs[:n_g]), list(outs[n_g:])


def _mesh_peers():
    x, y, c = lax.axis_index("x"), lax.axis_index("y"), lax.axis_index("c")
    peers = []
    for m in range(1, N_DEV):
        px, py, pc = x ^ ((m >> 2) & 1), y ^ ((m >> 1) & 1), c ^ (m & 1)
        peers.append((m, (px, py, pc), 4 * px + 2 * py + pc))
    return 4 * x + 2 * y + c, peers


_HBM_SPEC = pl.BlockSpec(memory_space=pltpu.HBM)
_SEM_SPEC = pl.BlockSpec(memory_space=pltpu.SEMAPHORE)
_DATAFLOW = pltpu.SideEffectType.DATAFLOW_SIDE_EFFECTING


def _exchange_start(name, gathers, scatters):
    n_g = len(gathers)
    arrays = list(gathers) + list(scatters)
    n = len(arrays)
    lands = ([lax.empty((N_DEV,) + a.shape, a.dtype) for a in gathers] + [lax.empty(a.shape, a.dtype) for a in scatters])

    def body(*refs):
        ins, land = refs[:n], refs[n:2 * n]
        send_sems, recv_sems, local_sems, token = refs[2 * n], refs[2 * n + 1], refs[2 * n + 2], refs[-1]
        me, peers = _mesh_peers()
        for m, dev, peer in peers:
            for i in range(n):
                k = i * (N_DEV - 1) + m - 1
                pltpu.make_async_remote_copy(
                    src_ref=ins[i] if i < n_g else ins[i].at[peer], dst_ref=land[i].at[me],
                    send_sem=send_sems.at[k], recv_sem=recv_sems.at[k],
                    device_id=dev, device_id_type=pl.DeviceIdType.MESH).start()
        for i in range(n):
            pltpu.make_async_copy(ins[i] if i < n_g else ins[i].at[me], land[i].at[me], local_sems.at[i]).start()
        token[...] = jnp.zeros_like(token)

    sem = pltpu.SemaphoreType.DMA((n * (N_DEV - 1),))
    outs = pl.pallas_call(
        body, name=name,
        out_shape=([sem, sem, pltpu.SemaphoreType.DMA((n,))] + [pltpu.HBM(a.shape, a.dtype) for a in arrays]
                   + [pltpu.HBM(l.shape, l.dtype) for l in lands] + [jax.ShapeDtypeStruct((8, 128), F32)]),
        in_specs=[_HBM_SPEC] * (2 * n),
        out_specs=[_SEM_SPEC] * 3 + [_HBM_SPEC] * (2 * n) + [pl.BlockSpec(memory_space=pltpu.VMEM)],
        input_output_aliases={i: i + 3 for i in range(2 * n)},
        compiler_params=pltpu.CompilerParams(has_side_effects=_DATAFLOW),
    )(*[pltpu.with_memory_space_constraint(a, pltpu.HBM) for a in arrays],
      *[pltpu.with_memory_space_constraint(l, pltpu.HBM) for l in lands])
    return dict(n_g=n_g, n=n, send=outs[0], recv=outs[1], local=outs[2], srcs=list(outs[3:3 + n]),
                lands=list(outs[3 + n:3 + 2 * n]), token=outs[-1])


def _exchange_wait(name, started, after):
    n, n_g = started["n"], started["n_g"]

    def body(*refs):
        srcs, lands = refs[:n], refs[n:2 * n]
        send_sems, recv_sems, local_sems = refs[2 * n], refs[2 * n + 1], refs[2 * n + 2]
        me, peers = _mesh_peers()
        local = [pltpu.make_async_copy(srcs[i] if i < n_g else srcs[i].at[me], lands[i].at[me], local_sems.at[i])
                 for i in range(n)]
        for m, dev, peer in peers:
            for i in range(n):
                k = i * (N_DEV - 1) + m - 1
                cp = pltpu.make_async_remote_copy(
                    src_ref=srcs[i] if i < n_g else srcs[i].at[peer], dst_ref=lands[i].at[peer],
                    send_sem=send_sems.at[k], recv_sem=recv_sems.at[k],
                    device_id=dev, device_id_type=pl.DeviceIdType.MESH)
                cp.wait_send()
                cp.wait_recv()
        for cp in local:
            cp.wait()

    outs = pl.pallas_call(
        body, name=name,
        out_shape=([pltpu.HBM(a.shape, a.dtype) for a in started["srcs"]]
                   + [pltpu.HBM(l.shape, l.dtype) for l in started["lands"]]),
        in_specs=[_HBM_SPEC] * (2 * n) + [_SEM_SPEC] * 3 + [pl.BlockSpec(memory_space=pl.ANY)],
        out_specs=[_HBM_SPEC] * (2 * n), input_output_aliases={i: i for i in range(2 * n)},
        compiler_params=pltpu.CompilerParams(has_side_effects=_DATAFLOW),
    )(*started["srcs"], *started["lands"], started["send"], started["recv"], started["local"], after)
    return list(outs[n:n + n_g]), list(outs[n + n_g:])


def _tile_spec(shape, layout, t):
    if layout == "nat":
        return pl.BlockSpec((t, shape[1]), lambda i: (i, 0))
    return pl.BlockSpec((shape[0], t, shape[2]), lambda i: (0, i, 0))


def _full_spec(shape):
    nd = len(shape)
    return pl.BlockSpec(tuple(shape), lambda i: (0,) * nd)


def _seq_len(a, layout):
    return a.shape[0] if layout == "nat" else a.shape[1]


def _tile_fwd(name, f, tiled, params, outs, t):
    nt, npar = len(tiled), len(params)
    s = _seq_len(*tiled[0])

    def body(*refs):
        vals = [r[...] for r in refs[:nt + npar]]
        res = f(*vals)
        for r, o in zip(refs[nt + npar:], res):
            r[...] = o.astype(r.dtype)

    return pl.pallas_call(
        body, name=name, grid=(s // t,),
        in_specs=[_tile_spec(a.shape, l, t) for a, l in tiled] + [_full_spec(p.shape) for p in params],
        out_specs=[_tile_spec(sh, l, t) for sh, _, l in outs],
        out_shape=[jax.ShapeDtypeStruct(sh, dt) for sh, dt, _ in outs],
        compiler_params=_params(("arbitrary",)),
    )(*[a for a, _ in tiled], *params)


def _tile_bwd(name, f, tiled, params, cts, t, need, adds=None):
    nt, npar, nc = len(tiled), len(params), len(cts)
    s = _seq_len(*tiled[0])
    need_idx = [k for k in range(nt) if need[k]]
    adds = adds or [None] * len(need_idx)
    add_arrays = [(a, tiled[k][1]) for a, k in zip(adds, need_idx) if a is not None]
    n_add = len(add_arrays)

    def body(*refs):
        i = pl.program_id(0)
        vals = [r[...] for r in refs[:nt + npar]]
        ct_refs = refs[nt + npar:nt + npar + nc]
        add_refs = refs[nt + npar + nc:nt + npar + nc + n_add]
        out_refs = refs[nt + npar + nc + n_add:]
        res, vjp = jax.vjp(f, *vals)
        grads = vjp(tuple(r[...].astype(o.dtype) for r, o in zip(ct_refs, res)))
        a = 0
        for j, k in enumerate(need_idx):
            g = grads[k]
            if adds[j] is not None:
                g = g + add_refs[a][...]
                a += 1
            out_refs[j][...] = g.astype(out_refs[j].dtype)
        for j in range(npar):
            r = out_refs[len(need_idx) + j]

            @pl.when(i == 0)
            def _():
                r[...] = jnp.zeros_like(r)

            r[...] += grads[nt + j]

    outs = pl.pallas_call(
        body, name=name, grid=(s // t,),
        in_specs=([_tile_spec(a.shape, l, t) for a, l in tiled] + [_full_spec(p.shape) for p in params]
                  + [_tile_spec(a.shape, l, t) for a, l in cts] + [_tile_spec(a.shape, l, t) for a, l in add_arrays]),
        out_specs=([_tile_spec(tiled[k][0].shape, tiled[k][1], t) for k in need_idx]
                   + [_full_spec(p.shape) for p in params]),
        out_shape=([jax.ShapeDtypeStruct(tiled[k][0].shape, F32) for k in need_idx]
                   + [jax.ShapeDtypeStruct(p.shape, F32) for p in params]),
        compiler_params=_params(("arbitrary",)),
    )(*[a for a, _ in tiled], *params, *[a for a, _ in cts], *[a for a, _ in add_arrays])
    return list(outs[:len(need_idx)]), list(outs[len(need_idx):])


def _linear_fwd(name, x, w, t, out_layout="nat", residual=None):
    s, k = x.shape
    n = w.shape[1]
    has_res = residual is not None

    def body(*refs):
        x_ref, w_ref = refs[0], refs[1]
        o_ref = refs[-1]
        y = _dot(x_ref[...].astype(BF16), w_ref[...])
        if has_res:
            y = y + refs[2][...]
        if out_layout == "hm":
            for h in range(n // HEAD):
                o_ref[h] = y[:, h * HEAD:(h + 1) * HEAD]
        else:
            o_ref[...] = y

    out_sh = (s, n) if out_layout == "nat" else (n // HEAD, s, HEAD)
    ins = [x, w] + ([residual] if has_res else [])
    in_specs = [_tile_spec(x.shape, "nat", t), _full_spec(w.shape)] + ([_tile_spec((s, n), "nat", t)] if has_res else [])
    return pl.pallas_call(
        body, name=name, grid=(s // t,), in_specs=in_specs,
        out_specs=_tile_spec(out_sh, out_layout, t), out_shape=jax.ShapeDtypeStruct(out_sh, F32),
        compiler_params=_params(("arbitrary",)),
    )(*ins)


def _linear_dx(name, dy, w, t, dy_layout="nat"):
    k, n = w.shape
    s = _seq_len(dy, dy_layout)

    def body(dy_ref, w_ref, o_ref):
        dy = _heads_to_nat(dy_ref[...].astype(BF16)) if dy_layout == "hm" else dy_ref[...].astype(BF16)
        o_ref[...] = _dot(dy, w_ref[...], NT2)

    return pl.pallas_call(
        body, name=name, grid=(s // t,),
        in_specs=[_tile_spec(dy.shape, dy_layout, t), _full_spec(w.shape)],
        out_specs=_tile_spec((s, k), "nat", t), out_shape=jax.ShapeDtypeStruct((s, k), F32),
        compiler_params=_params(("arbitrary",)),
    )(dy, w)


def _linear_dw(name, x, dy, t, nb, dy_layout="nat"):
    s, k = x.shape
    n = dy.shape[1] if dy_layout == "nat" else dy.shape[0] * HEAD

    def body(x_ref, dy_ref, o_ref):
        i = pl.program_id(1)

        @pl.when(i == 0)
        def _():
            o_ref[...] = jnp.zeros_like(o_ref)

        dy = _heads_to_nat(dy_ref[...].astype(BF16)) if dy_layout == "hm" else dy_ref[...].astype(BF16)
        o_ref[...] += _dot(x_ref[...].astype(BF16), dy, TN2)

    if dy_layout == "hm":
        dy_spec = pl.BlockSpec((nb // HEAD, t, HEAD), lambda j, i: (j, i, 0))
    else:
        dy_spec = pl.BlockSpec((t, nb), lambda j, i: (i, j))
    return pl.pallas_call(
        body, name=name, grid=(n // nb, s // t),
        in_specs=[pl.BlockSpec((t, k), lambda j, i: (i, 0)), dy_spec],
        out_specs=pl.BlockSpec((k, nb), lambda j, i: (0, j)), out_shape=jax.ShapeDtypeStruct((k, n), F32),
        compiler_params=_params(("arbitrary", "arbitrary")),
    )(x, dy)


def _ffn_fwd(name, x, gn, wg, wu, wd, t):
    s, d = x.shape
    f8 = wg.shape[-1]

    def body(x_ref, g_ref, wg_ref, wu_ref, wd_ref, o_ref, gk_ref, uk_ref, h_scr, acc):
        k = pl.program_id(1)

        @pl.when(k == 0)
        def _():
            h_scr[...] = _rms(x_ref[...], g_ref[...]).astype(BF16)
            acc[...] = jnp.zeros_like(acc)

        hb = h_scr[...]
        gk = _dot(hb, wg_ref[0])
        uk = _dot(hb, wu_ref[0])
        gk_ref[0] = gk
        uk_ref[0] = uk
        a = gk * jax.nn.sigmoid(gk) * uk
        acc[...] += _dot(a.astype(BF16), wd_ref[0])

        @pl.when(k == N_DEV - 1)
        def _():
            o_ref[...] = x_ref[...] + 0.5 * acc[...]

    wspec = lambda shp: pl.BlockSpec((1,) + shp, lambda i, k: (k, 0, 0))
    act = pl.BlockSpec((1, t, f8), lambda i, k: (k, i, 0))
    act_sh = jax.ShapeDtypeStruct((N_DEV, s, f8), F32)
    return pl.pallas_call(
        body, name=name, grid=(s // t, N_DEV),
        in_specs=[pl.BlockSpec((t, d), lambda i, k: (i, 0)), pl.BlockSpec((1, d), lambda i, k: (0, 0)),
                  wspec((d, f8)), wspec((d, f8)), wspec((f8, d))],
        out_specs=[pl.BlockSpec((t, d), lambda i, k: (i, 0)), act, act],
        out_shape=[jax.ShapeDtypeStruct((s, d), F32), act_sh, act_sh],
        scratch_shapes=[pltpu.VMEM((t, d), BF16), pltpu.VMEM((t, d), F32)],
        compiler_params=_params(("arbitrary", "arbitrary")),
    )(x, gn, wg, wu, wd)


def _ffn_bwd(name, x, dy, gn, wg, wu, wd, gact, uact, t):
    s, d = x.shape
    f8 = wg.shape[-1]
    last = N_DEV - 1

    def body(x_ref, dy_ref, g_ref, wg_ref, wu_ref, wd_ref, gk_ref, uk_ref,
             dx_ref, dg_ref, dwg_ref, dwu_ref, dwd_ref, dh_scr):
        k, i = pl.program_id(0), pl.program_id(1)
        x = x_ref[...]
        rs = lax.rsqrt(jnp.mean(x * x, axis=-1, keepdims=True) + NORM_EPS)
        xn = x * rs
        hb = (xn * g_ref[...]).astype(BF16)
        dob = (0.5 * dy_ref[...]).astype(BF16)
        wgk, wuk, wdk = wg_ref[0], wu_ref[0], wd_ref[0]
        gk, uk = gk_ref[0], uk_ref[0]
        sg = jax.nn.sigmoid(gk)
        sk = gk * sg
        da = _dot(dob, wdk, NT2)
        du = (da * sk).astype(BF16)
        dg = (da * uk * (sg * (1.0 + gk * (1.0 - sg)))).astype(BF16)

        dwd_c = _dot((sk * uk).astype(BF16), dob, TN2)
        dwg_c = _dot(hb, dg, TN2)
        dwu_c = _dot(hb, du, TN2)
        dh = _dot(dg, wgk, NT2) + _dot(du, wuk, NT2)
        rows = pl.ds(pl.multiple_of(i * t, t), t)

        @pl.when(i == 0)
        def _():
            dwg_ref[0], dwu_ref[0], dwd_ref[0] = dwg_c, dwu_c, dwd_c

        @pl.when(i > 0)
        def _():
            dwg_ref[0] += dwg_c
            dwu_ref[0] += dwu_c
            dwd_ref[0] += dwd_c

        @pl.when(k == 0)
        def _():
            dh_scr[rows, :] = dh

        @pl.when(k > 0)
        def _():
            dh_scr[rows, :] += dh

        @pl.when(jnp.logical_and(k == last, i == 0))
        def _():
            dg_ref[...] = jnp.zeros_like(dg_ref)

        @pl.when(k == last)
        def _():
            dht = dh_scr[rows, :]
            dg_ref[...] += jnp.sum(dht * xn, axis=0, keepdims=True)
            dxn = dht * g_ref[...]
            dx_ref[...] = dy_ref[...] + rs * (dxn - xn * jnp.mean(dxn * xn, axis=-1, keepdims=True))

    wspec = lambda shp: pl.BlockSpec((1,) + shp, lambda k, i: (k, 0, 0))
    tile = pl.BlockSpec((t, d), lambda k, i: (i, 0))
    act = pl.BlockSpec((1, t, f8), lambda k, i: (k, i, 0))
    return pl.pallas_call(
        body, name=name, grid=(N_DEV, s // t),
        in_specs=[tile, tile, pl.BlockSpec((1, d), lambda k, i: (0, 0)), wspec((d, f8)), wspec((d, f8)), wspec((f8, d)),
                  act, act],
        out_specs=[pl.BlockSpec((t, d), lambda k, i: (jnp.where(k == last, i, 0), 0)),
                   pl.BlockSpec((1, d), lambda k, i: (0, 0)),
                   pl.BlockSpec((1, d, f8), lambda k, i: (k, 0, 0)), pl.BlockSpec((1, d, f8), lambda k, i: (k, 0, 0)),
                   pl.BlockSpec((1, f8, d), lambda k, i: (k, 0, 0))],
        out_shape=[jax.ShapeDtypeStruct((s, d), F32), jax.ShapeDtypeStruct((1, d), F32),
                   jax.ShapeDtypeStruct((N_DEV, d, f8), F32), jax.ShapeDtypeStruct((N_DEV, d, f8), F32),
                   jax.ShapeDtypeStruct((N_DEV, f8, d), F32)],
        scratch_shapes=[pltpu.VMEM((s, d), F32)],
        compiler_params=_params(("arbitrary", "arbitrary")),
    )(x, dy, gn, wg, wu, wd, gact, uact)


def _loss_head(y, target, t):
    s, d = y.shape

    def body(y_ref, t_ref, dy_ref, l_ref):
        i = pl.program_id(0)
        err = y_ref[...] - t_ref[...]
        dy_ref[...] = err * (1.0 / d)

        @pl.when(i == 0)
        def _():
            l_ref[...] = jnp.zeros_like(l_ref)

        l_ref[...] += 0.5 * jnp.sum(jnp.mean(err * err, axis=-1, keepdims=True), axis=0, keepdims=True)

    tile = pl.BlockSpec((t, d), lambda i: (i, 0))
    dy, l = pl.pallas_call(
        body, name="loss_head", grid=(s // t,), in_specs=[tile, tile],
        out_specs=[tile, pl.BlockSpec((1, 1), lambda i: (0, 0))],
        out_shape=[jax.ShapeDtypeStruct((s, d), F32), jax.ShapeDtypeStruct((1, 1), F32)],
        compiler_params=_params(("arbitrary",)),
    )(y, target)
    return dy, l[0, 0]


SB_KEY_TILE = 1024
SB_HEADS_PER_STEP = 4


def _sb_scan_mats():
    row = lax.broadcasted_iota(jnp.int32, (QBLK, QBLK), 0)
    col = lax.broadcasted_iota(jnp.int32, (QBLK, QBLK), 1)
    return (row > col).astype(F32).astype(BF16), (row < col).astype(F32).astype(BF16)


def _sb_tile_scan(x, mat, reverse):
    nsub = x.shape[1] // QBLK
    outs, carry = [None] * nsub, jnp.zeros((x.shape[0], 1), F32)
    for i in (reversed(range(nsub)) if reverse else range(nsub)):
        xs = x[:, i * QBLK:(i + 1) * QBLK]
        hi = xs.astype(BF16)
        lo = (xs - hi.astype(F32)).astype(BF16)
        outs[i] = _dot(hi, mat) + _dot(lo, mat) + carry
        carry = carry + jnp.sum(xs, axis=1, keepdims=True)
    return jnp.concatenate(outs, axis=1), carry


def _sb_before_query(n, t, kt):
    row = lax.broadcasted_iota(jnp.int32, (QBLK, kt), 0)
    col = lax.broadcasted_iota(jnp.int32, (QBLK, kt), 1)
    return t * kt + col < n * QBLK + row


def _sb_fwd(q, k, v):
    _, s, _ = q.shape
    scale = HEAD ** -0.5
    kt = min(SB_KEY_TILE, s)

    def body(q_ref, k_ref, v_ref, o_ref):
        n = pl.program_id(1)
        suffix, _ = _sb_scan_mats()
        n_tiles = lax.div(n, jnp.int32(kt // QBLK)) + 1
        heads = range(SB_HEADS_PER_STEP)
        qb = [(q_ref[h] * scale).astype(q_ref.dtype) for h in heads]

        def tile(t, carry, diagonal):
            rows = pl.ds(pl.multiple_of(t * kt, kt), kt)
            out = []
            for h in heads:
                c, acc = carry[h]
                z = _dot(qb[h], k_ref[h, rows, :], NT2)
                lk = _log_sigmoid(-z)
                log_beta = z + lk
                if diagonal:
                    ok = _sb_before_query(n, t, kt)
                    lk = jnp.where(ok, lk, 0.0)
                later, total = _sb_tile_scan(lk, suffix, True)
                w = jnp.exp(log_beta + later + c)
                if diagonal:
                    w = jnp.where(ok, w, 0.0)
                out.append((c + total, acc + _dot(w.astype(BF16), v_ref[h, rows, :])))
            return tuple(out)

        zero = (jnp.zeros((QBLK, 1), F32), jnp.zeros((QBLK, HEAD), F32))
        carry = tile(n_tiles - 1, (zero,) * len(heads), True)
        carry = lax.fori_loop(1, n_tiles, lambda jj, cr: tile(n_tiles - 1 - jj, cr, False), carry)
        for h in heads:
            o_ref[h] = carry[h][1]

    hp = SB_HEADS_PER_STEP
    return pl.pallas_call(
        body, name="sb_fwd", grid=(SB_HEADS // hp, s // QBLK),
        in_specs=[pl.BlockSpec((hp, QBLK, HEAD), lambda h, n: (h, n, 0)),
                  pl.BlockSpec((hp, s, HEAD), lambda h, n: (h, 0, 0)),
                  pl.BlockSpec((hp, s, HEAD), lambda h, n: (h, 0, 0))],
        out_specs=pl.BlockSpec((hp, QBLK, HEAD), lambda h, n: (h, n, 0)),
        out_shape=jax.ShapeDtypeStruct((SB_HEADS, s, HEAD), F32),
        compiler_params=_params(("arbitrary", "arbitrary")),
    )(q, k, v)


def _sb_bwd(q, k, v, do):
    _, s, _ = q.shape
    scale = HEAD ** -0.5
    kt = min(SB_KEY_TILE, s)

    def body(q_ref, k_ref, v_ref, do_ref, dq_ref, dk_ref, dv_ref, e_scr, beta_scr):
        n = pl.program_id(1)

        @pl.when(n == 0)
        def _():
            dk_ref[...] = jnp.zeros_like(dk_ref)
            dv_ref[...] = jnp.zeros_like(dv_ref)

        suffix, prefix = _sb_scan_mats()
        n_tiles = lax.div(n, jnp.int32(kt // QBLK)) + 1
        heads = range(SB_HEADS_PER_STEP)
        qb = [(q_ref[h] * scale).astype(q_ref.dtype) for h in heads]
        dob = [do_ref[h].astype(BF16) for h in heads]

        def weights(t, cs, diagonal):
            rows = pl.ds(pl.multiple_of(t * kt, kt), kt)
            out, stores = [], []
            for h in heads:
                vb = v_ref[h, rows, :]
                z = _dot(qb[h], k_ref[h, rows, :], NT2)
                lk = _log_sigmoid(-z)
                log_beta = z + lk
                if diagonal:
                    ok = _sb_before_query(n, t, kt)
                    lk = jnp.where(ok, lk, 0.0)
                later, total = _sb_tile_scan(lk, suffix, True)
                w = jnp.exp(log_beta + later + cs[h])
                if diagonal:
                    w = jnp.where(ok, w, 0.0)
                stores.append((w * _dot(dob[h], vb, NT2), jnp.exp(log_beta), _dot(w.astype(BF16), dob[h], TN2)))
                out.append(cs[h] + total)
            for h in heads:
                e_scr[h, t], beta_scr[h, t] = stores[h][0], stores[h][1]
                dv_ref[h, rows, :] += stores[h][2]
            return tuple(out)

        col0 = jnp.zeros((QBLK, 1), F32)
        cs = weights(n_tiles - 1, (col0,) * len(heads), True)
        lax.fori_loop(1, n_tiles, lambda jj, c: weights(n_tiles - 1 - jj, c, False), cs)

        def grads(t, carry, diagonal):
            rows = pl.ds(pl.multiple_of(t * kt, kt), kt)
            out, dks = [], []
            for h in heads:
                pc, dq = carry[h]
                kb = k_ref[h, rows, :]
                e, beta = e_scr[h, t], beta_scr[h, t]
                before, total = _sb_tile_scan(e, prefix, False)
                dz = e * (1.0 - beta) - beta * (before + pc)
                if diagonal:
                    dz = jnp.where(_sb_before_query(n, t, kt), dz, 0.0)
                dz = dz.astype(BF16)
                dks.append(_dot(dz, qb[h], TN2))
                out.append((pc + total, dq + _dot(dz, kb)))
            for h in heads:
                dk_ref[h, rows, :] += dks[h]
            return tuple(out)

        zero = (col0, jnp.zeros((QBLK, HEAD), F32))
        carry = lax.fori_loop(0, n_tiles - 1, lambda t, cr: grads(t, cr, False), (zero,) * len(heads))
        carry = grads(n_tiles - 1, carry, True)
        for h in heads:
            dq_ref[h] = carry[h][1] * scale

    hp = SB_HEADS_PER_STEP
    qspec = pl.BlockSpec((hp, QBLK, HEAD), lambda h, n: (h, n, 0))
    full = pl.BlockSpec((hp, s, HEAD), lambda h, n: (h, 0, 0))
    sh = jax.ShapeDtypeStruct((SB_HEADS, s, HEAD), F32)
    tiles_sh = (hp, s // kt, QBLK, kt)
    return pl.pallas_call(
        body, name="sb_bwd", grid=(SB_HEADS // hp, s // QBLK),
        in_specs=[qspec, full, full, qspec],
        out_specs=[qspec, full, full], out_shape=[sh, sh, sh],
        scratch_shapes=[pltpu.VMEM(tiles_sh, F32), pltpu.VMEM(tiles_sh, F32)],
        compiler_params=_params(("arbitrary", "arbitrary")),
    )(q, k, v, do)


def _t5_bucket_np(dist):
    max_exact = N_BUCKETS // 2
    d = np.maximum(dist, 1).astype(np.float32)
    large = max_exact + (np.log(d / np.float32(max_exact)) / np.float32(math.log(MAX_DISTANCE / max_exact))
                         * np.float32(N_BUCKETS - max_exact)).astype(np.int32)
    large = np.minimum(large, N_BUCKETS - 1)
    return np.where(dist < max_exact, dist, large)


def _dil_layout(s):
    assert all(s % (QBLK * r) == 0 and window // r == QBLK for window, r in DIL_PATTERNS)
    tiles, buckets = [], []
    i = np.arange(QBLK)[:, None]
    j = np.arange(QBLK)[None, :]
    for g, (window, r) in enumerate(DIL_PATTERNS):
        for off in (0, 1):
            dist = QBLK * off + i - j
            ok = (dist >= 0) & (dist <= window // r)
            tiles.append((g, off))
            buckets.append(np.where(ok, _t5_bucket_np(np.maximum(dist, 0) * r), -1).astype(np.int32))
    return tiles, np.stack(buckets)


def _bias_tiles(rel_bias, s):
    tiles, buckets = _dil_layout(s)
    nt = len(tiles)
    present = [sorted(set(np.unique(buckets[k]).tolist()) - {-1}) for k in range(nt)]

    def body(rel_ref, b_ref, o_ref):
        j = pl.program_id(0)
        for k, (g, _) in enumerate(tiles):
            bk = b_ref[k]
            tile = jnp.full((QBLK, QBLK), NEG_INF, F32)
            for b in present[k]:
                tile = jnp.where(bk == b, rel_ref[b, g * DIL_GROUP + j], tile)
            o_ref[0, k] = tile

    return pl.pallas_call(
        body, name="bias_tiles", grid=(DIL_GROUP,),
        in_specs=[pl.BlockSpec(memory_space=pltpu.SMEM), pl.BlockSpec((nt, QBLK, QBLK), lambda j: (0, 0, 0))],
        out_specs=pl.BlockSpec((1, nt, QBLK, QBLK), lambda j: (j, 0, 0, 0)),
        out_shape=jax.ShapeDtypeStruct((DIL_GROUP, nt, QBLK, QBLK), F32),
        compiler_params=_params(("arbitrary",)),
    )(rel_bias, jnp.asarray(buckets))


def _bias_tiles_bwd(dbias, s):
    tiles, buckets = _dil_layout(s)
    nt = len(tiles)
    present = [sorted(set(np.unique(buckets[k]).tolist()) - {-1}) for k in range(nt)]

    def body(d_ref, b_ref, o_ref):
        j = pl.program_id(0)

        @pl.when(j == 0)
        def _():
            for b in range(N_BUCKETS):
                for col in range(3 * DIL_GROUP):
                    o_ref[b, col] = jnp.float32(0.0)

        for k, (g, _) in enumerate(tiles):
            bk, dk = b_ref[k], d_ref[0, k]
            for b in present[k]:
                o_ref[b, g * DIL_GROUP + j] += jnp.sum(jnp.where(bk == b, dk, 0.0))

    return pl.pallas_call(
        body, name="bias_tiles_bwd", grid=(DIL_GROUP,),
        in_specs=[pl.BlockSpec((1, nt, QBLK, QBLK), lambda j: (j, 0, 0, 0)),
                  pl.BlockSpec((nt, QBLK, QBLK), lambda j: (0, 0, 0))],
        out_specs=pl.BlockSpec(memory_space=pltpu.SMEM),
        out_shape=jax.ShapeDtypeStruct((N_BUCKETS, 3 * DIL_GROUP), F32),
        compiler_params=_params(("arbitrary",)),
    )(dbias, jnp.asarray(buckets))


DIL_PAIRS_PER_STEP = 8


def _dil_rows(g, s, pair):
    _, r = DIL_PATTERNS[g]
    nb = s // (QBLK * r)
    c, n = lax.div(pair, jnp.int32(nb)), lax.rem(pair, jnp.int32(nb))
    start = c + (r * QBLK) * n
    before = jnp.where(n > 0, start - r * QBLK, start)
    if r == 1:
        return pl.ds(start, QBLK), pl.ds(before, QBLK), n > 0
    return pl.ds(start, QBLK, stride=r), pl.ds(before, QBLK, stride=r), n > 0


def _dil_logits(qb, k_ref, rows, before, has_before, b_ref):
    k0, k1 = k_ref[0, rows, :].astype(BF16), k_ref[0, before, :].astype(BF16)
    l0 = _dot(qb, k0, NT2) + b_ref[0, 0]
    l1 = jnp.where(has_before, _dot(qb, k1, NT2) + b_ref[0, 1], NEG_INF)
    return k0, k1, l0, l1


def _dil_group_specs(g, s):
    head = pl.BlockSpec((1, s, HEAD), lambda j, p: (DIL_GROUP * g + j, 0, 0))
    return [head, head, head, pl.BlockSpec((1, 2, QBLK, QBLK), lambda j, p: (j, g, 0, 0))]


def _dil_group_fwd(g, qn, kn, v, bias):
    _, s, _ = qn.shape
    scale = HEAD ** -0.5
    steps = (s // QBLK) // DIL_PAIRS_PER_STEP

    def body(q_ref, k_ref, v_ref, b_ref, o_ref):
        pairs = [_dil_rows(g, s, pl.program_id(1) * DIL_PAIRS_PER_STEP + u) for u in range(DIL_PAIRS_PER_STEP)]
        loaded = []
        for rows, before, has_before in pairs:
            qb = (q_ref[0, rows, :] * scale).astype(BF16)
            _, _, l0, l1 = _dil_logits(qb, k_ref, rows, before, has_before, b_ref)
            loaded.append((l0, l1, v_ref[0, rows, :].astype(BF16), v_ref[0, before, :].astype(BF16)))
        results = []
        for l0, l1, v0, v1 in loaded:
            m = jnp.max(jnp.maximum(l0, l1), axis=1, keepdims=True)
            p0, p1 = jnp.exp(l0 - m), jnp.exp(l1 - m)
            den = jnp.sum(p0 + p1, axis=1, keepdims=True)
            inv = 1.0 / den
            o = _dot((p0 * inv).astype(BF16), v0) + _dot((p1 * inv).astype(BF16), v1)
            results.append(jnp.concatenate([o, jnp.broadcast_to(m + jnp.log(den), (QBLK, HEAD))], axis=1))
        for (rows, _, _), res in zip(pairs, results):
            o_ref[0, rows, :] = res

    return pl.pallas_call(
        body, name="dil%d_fwd" % g, grid=(DIL_GROUP, steps), in_specs=_dil_group_specs(g, s),
        out_specs=pl.BlockSpec((1, s, 2 * HEAD), lambda j, p: (j, 0, 0)),
        out_shape=jax.ShapeDtypeStruct((DIL_GROUP, s, 2 * HEAD), F32),
        compiler_params=_params(("arbitrary", "arbitrary")),
    )(qn, kn, v, bias)


def _dil_group_bwd(g, qn, kn, v, bias, ol, dol, prev):
    _, s, _ = qn.shape
    scale = HEAD ** -0.5
    steps = (s // QBLK) // DIL_PAIRS_PER_STEP
    prev = list(prev) if prev is not None else []

    def body(q_ref, k_ref, v_ref, b_ref, ol_ref, dol_ref, *rest):
        dq_ref, dk_ref, dv_ref, db_ref = rest[-4:]

        @pl.when(pl.program_id(1) == 0)
        def _():
            for r in (dk_ref, dv_ref, db_ref):
                r[...] = jnp.zeros_like(r)

        pairs = [_dil_rows(g, s, pl.program_id(1) * DIL_PAIRS_PER_STEP + u) for u in range(DIL_PAIRS_PER_STEP)]
        loaded = []
        for rows, before, has_before in pairs:
            qb = (q_ref[0, rows, :] * scale).astype(BF16)
            k0, k1, l0, l1 = _dil_logits(qb, k_ref, rows, before, has_before, b_ref)
            v0, v1 = v_ref[0, rows, :].astype(BF16), v_ref[0, before, :].astype(BF16)
            loaded.append((qb, k0, k1, l0, l1, v0, v1, ol_ref[0, rows, :], dol_ref[0, rows, :]))
        grads = []
        for qb, k0, k1, l0, l1, v0, v1, out_lse, d_out_lse in loaded:
            o, lse = out_lse[:, :HEAD], out_lse[:, HEAD:HEAD + 1]
            do, dlse = d_out_lse[:, :HEAD], d_out_lse[:, HEAD:HEAD + 1]
            dob = do.astype(BF16)
            p0, p1 = jnp.exp(l0 - lse), jnp.exp(l1 - lse)
            shift = dlse - jnp.sum(do * o, axis=1, keepdims=True)
            dl0 = p0 * (_dot(dob, v0, NT2) + shift)
            dl1 = p1 * (_dot(dob, v1, NT2) + shift)
            dl0b, dl1b = dl0.astype(BF16), dl1.astype(BF16)
            grads.append(((_dot(dl0b, k0) + _dot(dl1b, k1)) * scale,
                          _dot(dl0b, qb, TN2), _dot(dl1b, qb, TN2),
                          _dot(p0.astype(BF16), dob, TN2), _dot(p1.astype(BF16), dob, TN2), dl0, dl1))
        db0 = functools.reduce(jnp.add, [gr[5] for gr in grads])
        db1 = functools.reduce(jnp.add, [gr[6] for gr in grads])
        for (rows, before, _), (dq, dk0, dk1, dv0, dv1, _, _) in zip(pairs, grads):
            dq_ref[0, rows, :] = dq
            dk_ref[0, rows, :] += dk0
            dk_ref[0, before, :] += dk1
            dv_ref[0, rows, :] += dv0
            dv_ref[0, before, :] += dv1
        db_ref[0, 0] += db0
        db_ref[0, 1] += db1

    head_out = pl.BlockSpec((1, s, HEAD), lambda j, p: (DIL_GROUP * g + j, 0, 0))
    rows128 = pl.BlockSpec((1, s, 2 * HEAD), lambda j, p: (j, 0, 0))
    full_sh = jax.ShapeDtypeStruct(qn.shape, F32)
    return pl.pallas_call(
        body, name="dil%d_bwd" % g, grid=(DIL_GROUP, steps),
        in_specs=_dil_group_specs(g, s) + [rows128, rows128] + [pl.BlockSpec(memory_space=pl.ANY)] * len(prev),
        out_specs=[head_out, head_out, head_out, pl.BlockSpec((1, 2, QBLK, QBLK), lambda j, p: (j, 0, 0, 0))],
        out_shape=[full_sh, full_sh, full_sh, jax.ShapeDtypeStruct((DIL_GROUP, 2, QBLK, QBLK), F32)],
        input_output_aliases={6 + i: i for i in range(len(prev))},
        compiler_params=_params(("arbitrary", "arbitrary")),
    )(qn, kn, v, bias, ol, dol, *prev)


@functools.partial(jax.custom_vjp, nondiff_argnums=(2,))
def _bdot(a, b, dims):
    return _dot(a.astype(BF16), b.astype(BF16), dims)


def _bdot_fwd(a, b, dims):
    return _bdot(a, b, dims), (a, b)


def _bdot_bwd(dims, res, dc):
    a, b = res
    nn, nt, tn = (NN2, NT2, TN2) if dims in (NN2, NT2, TN2) else (NN3, NT3, TN3)
    if dims == nn:
        return _bdot(dc, b, nt), _bdot(a, dc, tn)
    if dims == nt:
        return _bdot(dc, b, nn), _bdot(dc, a, tn)
    return _bdot(b, dc, nt), _bdot(a, dc, nn)


_bdot.defvjp(_bdot_fwd, _bdot_bwd)


def _ones_dot(ones, x, dims):
    o = ones.astype(BF16)
    hi = x.astype(BF16)
    r1 = x - hi.astype(F32)
    mid = r1.astype(BF16)
    lo = (r1 - mid.astype(F32)).astype(BF16)
    return _dot(o, hi, dims) + _dot(o, mid, dims) + _dot(o, lo, dims)


@jax.custom_vjp
def _prefix_sums(x):
    c = x.shape[1]
    row = lax.broadcasted_iota(jnp.int32, (x.shape[0], c, c), 1)
    col = lax.broadcasted_iota(jnp.int32, (x.shape[0], c, c), 2)
    return _ones_dot((row >= col).astype(F32), x, NN3)


def _prefix_sums_fwd(x):
    return _prefix_sums(x), None


def _prefix_sums_bwd(_, dy):
    c = dy.shape[1]
    row = lax.broadcasted_iota(jnp.int32, (dy.shape[0], c, c), 1)
    col = lax.broadcasted_iota(jnp.int32, (dy.shape[0], c, c), 2)
    return (_ones_dot((row <= col).astype(F32), dy, NN3),)


_prefix_sums.defvjp(_prefix_sums_fwd, _prefix_sums_bwd)


def _rwkv_chunk(s0, r, lw, kraw, v, ag, kk_w, ka_w, rk_w, lng, lnb):
    hb, c, _ = r.shape
    kk = kraw * kk_w
    kk = kk / jnp.maximum(jnp.sqrt(jnp.sum(kk * kk, axis=-1, keepdims=True)), 1e-12)
    k = kraw * (1.0 + (ag - 1.0) * ka_w)
    a = -kk
    b = kk * ag
    row = lax.broadcasted_iota(jnp.int32, (hb, c, c), 1)
    col = lax.broadcasted_iota(jnp.int32, (hb, c, c), 2)
    lower, strict = row >= col, row > col
    cum = _prefix_sums(lw)
    ecum, einv = jnp.exp(cum), jnp.exp(-cum)
    rt, kt, bt = r * ecum, k * einv, b * einv
    at = a * jnp.exp(cum - lw)
    ar = jnp.concatenate([at, rt], axis=1)
    scores = _bdot(ar, jnp.concatenate([bt, kt], axis=1), NT3)
    a_ab = jnp.where(strict, scores[:, :c, :c], 0.0)
    a_ak = jnp.where(strict, scores[:, :c, c:], 0.0)
    p_rb = jnp.where(lower, scores[:, c:, :c], 0.0)
    p_rk = jnp.where(lower, scores[:, c:, c:], 0.0)
    from_s0 = _bdot(ar, s0, NT3)
    rhs = from_s0[:, :c] + _bdot(a_ak, v, NN3)
    inv = (row == col).astype(F32) + a_ab
    pw = a_ab
    for _ in range(int(math.log2(c)) - 1):
        pw = _bdot(pw, pw, NN3)
        inv = inv + _bdot(inv, pw, NN3)
    u = _bdot(inv, rhs, NN3)
    uv = jnp.concatenate([u, v], axis=1)
    y = from_s0[:, c:] + _bdot(jnp.concatenate([p_rb, p_rk], axis=2), uv, NN3)
    cum_end = cum[:, c - 1:c, :]
    dec = jnp.exp(cum_end - cum)
    s_end = s0 * jnp.exp(cum_end) + _bdot(uv, jnp.concatenate([b * dec, k * dec], axis=1), TN3)
    mu = jnp.mean(y, axis=-1, keepdims=True)
    var = jnp.mean(jnp.square(y - mu), axis=-1, keepdims=True)
    z = (y - mu) * lax.rsqrt(var + GN_EPS) * lng + lnb + jnp.sum(r * k * rk_w, axis=-1, keepdims=True) * v
    return z, s_end


def _rwkv_specs(nc, rev):
    cidx = (lambda c: nc - 1 - c) if rev else (lambda c: c)
    seq = pl.BlockSpec((RW_HB, RW_CHUNK, HEAD), lambda hg, c: (hg, cidx(c), 0))
    par = pl.BlockSpec((RW_HB, 1, HEAD), lambda hg, c: (hg, 0, 0))
    st = pl.BlockSpec((1, RW_HB, HEAD, HEAD), lambda hg, c: (cidx(c), hg, 0, 0))
    return seq, par, st


def _rwkv_fwd(seqs, pars):
    s = seqs[0].shape[1]
    nc = s // RW_CHUNK

    def body(*refs):
        seq_refs, par_refs = refs[:5], refs[5:10]
        z_ref, st_ref, state = refs[10:]
        c = pl.program_id(1)

        @pl.when(c == 0)
        def _():
            state[...] = jnp.zeros_like(state)

        s0 = state[...]
        st_ref[0] = s0
        z, s_end = _rwkv_chunk(s0, *[r[...] for r in seq_refs], *[r[...] for r in par_refs])
        z_ref[...] = z
        state[...] = s_end

    seq, par, st = _rwkv_specs(nc, False)
    return pl.pallas_call(
        body, name="rwkv_fwd", grid=(N_HEADS // RW_HB, nc),
        in_specs=[seq] * 5 + [par] * 5, out_specs=[seq, st],
        out_shape=[jax.ShapeDtypeStruct((N_HEADS, s, HEAD), F32), jax.ShapeDtypeStruct((nc, N_HEADS, HEAD, HEAD), F32)],
        scratch_shapes=[pltpu.VMEM((RW_HB, HEAD, HEAD), F32)],
        compiler_params=_params(("arbitrary", "arbitrary")),
    )(*seqs, *pars)


def _rwkv_bwd(seqs, pars, states, dz):
    s = seqs[0].shape[1]
    nc = s // RW_CHUNK

    def body(*refs):
        seq_refs, par_refs = refs[:5], refs[5:10]
        st_ref, dz_ref = refs[10:12]
        dseq_refs, dpar_refs, dstate = refs[12:17], refs[17:22], refs[22]
        c = pl.program_id(1)

        @pl.when(c == 0)
        def _():
            dstate[...] = jnp.zeros_like(dstate)
            for r in dpar_refs:
                r[...] = jnp.zeros_like(r)

        _, vjp = jax.vjp(_rwkv_chunk, st_ref[0], *[r[...] for r in seq_refs], *[r[...] for r in par_refs])
        g = vjp((dz_ref[...], dstate[...]))
        dstate[...] = g[0]
        for r, gs in zip(dseq_refs, g[1:6]):
            r[...] = gs
        for r, gp in zip(dpar_refs, g[6:]):
            r[...] += gp

    seq, par, st = _rwkv_specs(nc, True)
    seq_sh = jax.ShapeDtypeStruct((N_HEADS, s, HEAD), F32)
    par_sh = jax.ShapeDtypeStruct((N_HEADS, 1, HEAD), F32)
    outs = pl.pallas_call(
        body, name="rwkv_bwd", grid=(N_HEADS // RW_HB, nc),
        in_specs=[seq] * 5 + [par] * 5 + [st, seq],
        out_specs=[seq] * 5 + [par] * 5, out_shape=[seq_sh] * 5 + [par_sh] * 5,
        scratch_shapes=[pltpu.VMEM((RW_HB, HEAD, HEAD), F32)],
        compiler_params=_params(("arbitrary", "arbitrary")),
    )(*seqs, *pars, states, dz)
    return list(outs[:5]), list(outs[5:])


def _norm_fn(x, g):
    return (_rms(x, g),)


def _attn_prep_fn(proj, qn_w, kn_w):
    a, b = SB_HEADS, 3 * DIL_GROUP
    return (proj[0:a], proj[a:2 * a], proj[2 * a:3 * a],
            _rms(proj[3 * a:3 * a + b], qn_w), _rms(proj[3 * a + b:3 * a + 2 * b], kn_w), proj[3 * a + 2 * b:])


def _attn_merge_fn(o_sb, ol0, ol1, ol2):
    groups = (ol0, ol1, ol2)
    merged = []
    for j in range(DIL_GROUP):
        lses = [ol[j][:, HEAD:HEAD + 1] for ol in groups]
        m = functools.reduce(jnp.maximum, lses)
        es = [jnp.exp(l - m) for l in lses]
        inv = 1.0 / functools.reduce(jnp.add, es)
        merged.append(functools.reduce(jnp.add, [(e * inv) * ol[j][:, :HEAD] for e, ol in zip(es, groups)]))
    return (jnp.concatenate([_heads_to_nat(o_sb)] + merged, axis=-1),)


def _rw_mix_fn(x, xp, gn, mix, w0, w1, w2, a0, a1, a2, g1, g2):
    h = _rms(x, gn)
    xx = _rms(xp, gn) - h
    xr, xw, xk, xv, xa, xg = [h + xx * mix[i:i + 1] for i in range(6)]
    w_log = -jax.nn.softplus(-(w0 + _mm(jnp.tanh(_mm(xw, w1)), w2))) - 0.5
    lw = -jnp.exp(w_log)
    ag = jax.nn.sigmoid(a0 + _mm(_mm(xa, a1), a2))
    gate = _mm(jax.nn.sigmoid(_mm(xg, g1)), g2)
    return xr, xk, xv, _nat_to_heads(lw), _nat_to_heads(ag), gate


def _rw_gate_fn(z, gate):
    return (_heads_to_nat(z) * gate,)


def _adamw_update(w_ref, m_ref, v_ref, g_ref, go_ref, d_ref, mo_ref, vo_ref):
    g = g_ref[0].astype(F32)
    for j in range(1, N_DEV):
        g = g + g_ref[j].astype(F32)
    mn = ADAM_B1 * m_ref[...] + (1.0 - ADAM_B1) * g
    vn = ADAM_B2 * v_ref[...] + (1.0 - ADAM_B2) * jnp.square(g)
    m_hat = mn / (1.0 - ADAM_B1 ** ADAM_STEP)
    v_hat = vn / (1.0 - ADAM_B2 ** ADAM_STEP)
    go_ref[...] = g
    d_ref[...] = -ADAM_LR * (m_hat / (jnp.sqrt(v_hat) + ADAM_EPS) + ADAM_WD * w_ref[...])
    mo_ref[...] = mn
    vo_ref[...] = vn


def _adamw_many(name, items):
    n = len(items)

    def body(*refs):
        for i in range(n):
            _adamw_update(*refs[4 * i:4 * i + 4], *refs[4 * n + 4 * i:4 * n + 4 * i + 4])

    vmem = pl.BlockSpec(memory_space=pltpu.VMEM)
    outs = pl.pallas_call(
        body, name=name, in_specs=[vmem] * (4 * n), out_specs=[vmem] * (4 * n),
        out_shape=[jax.ShapeDtypeStruct(w.shape, F32) for w, _, _, _ in items for _ in range(4)],
        compiler_params=_params(),
    )(*[a for item in items for a in item])
    return [list(outs[4 * i:4 * i + 4]) for i in range(n)]


def _adamw(name, w, m, v, gparts, row0=0, prev=None):
    big_r, c = w.shape
    r = gparts.shape[1]
    tr = r
    if r % 8 == 0:
        tr = max(t for t in range(8, r + 1, 8) if r % t == 0 and (t * c * 4 <= (1 << 20) or t == 8))
    assert row0 % tr == 0 and (r == big_r or r % 8 == 0)
    off = row0 // tr

    def body(w_ref, m_ref, v_ref, g_ref, *rest):
        _adamw_update(w_ref, m_ref, v_ref, g_ref, *rest[-4:])

    tile = pl.BlockSpec((tr, c), lambda i: (i + off, 0))
    sh = jax.ShapeDtypeStruct((big_r, c), F32)
    prev = list(prev) if prev is not None else []
    return pl.pallas_call(
        body, name=name, grid=(r // tr,),
        in_specs=([tile, tile, tile, pl.BlockSpec((N_DEV, tr, c), lambda i: (0, i, 0))]
                  + [pl.BlockSpec(memory_space=pl.ANY)] * len(prev)),
        out_specs=[tile] * 4, out_shape=[sh] * 4,
        input_output_aliases={4 + j: j for j in range(len(prev))},
        compiler_params=_params(("arbitrary",)),
    )(w, m, v, gparts, *prev)


def _col_blocks_to_nat(g):
    return jnp.moveaxis(g, 0, 1).reshape(g.shape[1], -1)


def _nat_to_col_blocks(a):
    return jnp.moveaxis(a.reshape(a.shape[0], N_DEV, -1), 1, 0)


AG_GROUPS = ("f00", "att", "f01", "f10", "rw", "f11")
RS_GROUPS = ("f11", "rw", "f10", "f01", "f00", "att")
BF16_GRAD_GROUPS = ("att", "f00")
RW_SHARDED = ('rw_mix', 'rw_w0', 'rw_w1', 'rw_w2', 'rw_a0', 'rw_a1', 'rw_a2', 'rw_g1', 'rw_g2', 'rw_kk', 'rw_ka',
              'rw_wr', 'rw_wk', 'rw_wv', 'rw_wo', 'rw_lnx_g', 'rw_lnx_b')


def _step(x, target, rep, get, put):
    tied = lambda a, zero: a + zero[0, 0].astype(a.dtype)
    s, d = x.shape
    tf = min(512, s)
    tw = min(1024, s)
    dw_cols = 1024
    tt = min(256, s)
    row = lambda a: a.reshape(1, -1)
    mix_norm = rep["mix_norm"]
    ffw = {(0, 0): get("f00", None)}
    ffn_norm = _col_blocks_to_nat(ffw[(0, 0)]["ffn_norm"].reshape(N_DEV, 4, -1))

    acts = {}

    def ffn(nm, xin, l, h):
        g = ffw[(l, h)]
        out, *acts[(l, h)] = _ffn_fwd(nm, xin, ffn_norm[2 * l + h][None], g["gate"], g["up"], g["down"], min(2 * tf, s))
        return out

    x1 = ffn("ffn00_fwd", x, 0, 0)
    att = get("att", x1)
    w_in = _col_blocks_to_nat(att["attn_w_in"])
    w_out = _col_blocks_to_nat(att["attn_w_out"])
    (h0,) = _tile_fwd("mixnorm0_fwd", _norm_fn, [(x1, "nat")], [mix_norm[0:1]], [((s, d), BF16, "nat")], tf)
    proj = _linear_fwd("attn_in_fwd", h0, w_in, tf, out_layout="hm")
    bias = _bias_tiles(rep["rel_bias"], s)
    prep_pars = [rep["attn_q_norm"], rep["attn_k_norm"]]
    sb_sh, dl_sh = (SB_HEADS, s, HEAD), (3 * DIL_GROUP, s, HEAD)
    sq, sk, sv, qn, kn, vd = _tile_fwd("attn_prep_fwd", _attn_prep_fn, [(proj, "hm")], prep_pars,
                                       [(sb_sh, BF16, "hm")] * 3 + [(dl_sh, F32, "hm")] * 3, tt // 2)
    o_sb = _sb_fwd(sq, sk, sv)
    ols = [_dil_group_fwd(g, qn, kn, vd, bias) for g in range(3)]
    merge_tiled = [(o_sb, "hm")] + [(ol, "hm") for ol in ols]
    (merged,) = _tile_fwd("merge_fwd", _attn_merge_fn, merge_tiled, [], [((s, 512), BF16, "nat")], tt)
    x2 = _linear_fwd("attn_out_fwd", merged, w_out, tf, residual=x1)
    ffw[(0, 1)] = get("f01", x2)
    x3 = ffn("ffn01_fwd", x2, 0, 1)
    ffw[(1, 0)] = get("f10", x3)
    x4 = ffn("ffn10_fwd", x3, 1, 0)
    rw = get("rw", x4)
    rw_mix = _col_blocks_to_nat(rw["rw_mix"])
    rw_w1, rw_a1, rw_g1 = (rw[k].reshape(d, -1) for k in ("rw_w1", "rw_a1", "rw_g1"))
    rw_w2, rw_a2, rw_g2 = (_col_blocks_to_nat(rw[k]) for k in ("rw_w2", "rw_a2", "rw_g2"))
    rw_w0, rw_a0 = row(rw["rw_w0"]), row(rw["rw_a0"])
    head_par = lambda a: a.reshape(N_HEADS, 1, HEAD)
    scan_pars = [head_par(rw["rw_kk"]), head_par(rw["rw_ka"]), head_par(rep["rw_rk"]),
                 head_par(rw["rw_lnx_g"]), head_par(rw["rw_lnx_b"])]
    w_rkv = [rw[k].reshape(d, d) for k in ("rw_wr", "rw_wk", "rw_wv")]
    w_o = rw["rw_wo"].reshape(d, d)
    x4p = jnp.pad(x4, ((1, 0), (0, 0)))[:-1]
    mix_tiled = [(x4, "nat"), (x4p, "nat")]
    mix_pars = [mix_norm[1:2], rw_mix, rw_w0, rw_w1, rw_w2, rw_a0, rw_a1, rw_a2, rw_g1, rw_g2]
    hm_sh = (N_HEADS, s, HEAD)
    xr, xk, xv, lw, ag, gate = _tile_fwd(
        "rw_mix_fwd", _rw_mix_fn, mix_tiled, mix_pars,
        [((s, d), BF16, "nat")] * 3 + [(hm_sh, F32, "hm")] * 2 + [((s, d), F32, "nat")], tt)
    r_h, k_h, v_h = [_linear_fwd("rw_%s_fwd" % nm, xi, wi, tf, out_layout="hm")
                     for nm, xi, wi in zip("rkv", (xr, xk, xv), w_rkv)]
    scan_seqs = [r_h, lw, k_h, v_h, ag]
    z, states = _rwkv_fwd(scan_seqs, scan_pars)
    (zg,) = _tile_fwd("rw_gate_fwd", _rw_gate_fn, [(z, "hm"), (gate, "nat")], [], [((s, d), BF16, "nat")], tf)
    x5 = _linear_fwd("rw_out_fwd", zg, w_o, tf, residual=x4)
    ffw[(1, 1)] = get("f11", x5)
    y = ffn("ffn11_fwd", x5, 1, 1)
    dy, loss = _loss_head(y, target, tf)

    G = {}
    dgn = {}

    def fb(nm, group, xin, dout, l, h, zero=None, extra=None):
        g = ffw[(l, h)]
        gn = ffn_norm[2 * l + h][None]
        dxin, dgn[(l, h)], dwg, dwu, dwd = _ffn_bwd(nm, xin, dout, gn if zero is None else tied(gn, zero),
                                                   g["gate"], g["up"], g["down"], *acts[(l, h)], tf)
        shard = {"gate": dwg, "up": dwu, "down": dwd}
        if extra is not None:
            shard.update(extra())
        return dxin, put(group, {}, shard)

    dx5, zero = fb("ffn11_bwd", "f11", x5, dy, 1, 1)
    dzg = _linear_dx("rw_out_dx", dx5, w_o, tf)
    G["rw_wo"] = _linear_dw("rw_out_dw", zg, dx5, tw, dw_cols)
    (dz, dgate), _ = _tile_bwd("rw_gate_bwd", _rw_gate_fn, [(z, "hm"), (gate, "nat")], [], [(dzg, "nat")], tf, [True, True])
    (dr_h, dlw, dk_h, dv_h, dag), dscan = _rwkv_bwd(scan_seqs, [tied(scan_pars[0], zero)] + scan_pars[1:], states, dz)
    drkv = (dr_h, dk_h, dv_h)
    for k, gpar in zip(("rw_kk", "rw_ka", "rw_rk", "rw_lnx_g", "rw_lnx_b"), dscan):
        G[k] = gpar
    dxs = []
    for j, (nm, xi, wi) in enumerate(zip("rkv", (xr, xk, xv), w_rkv)):
        dxs.append(_linear_dx("rw_%s_dx" % nm, drkv[j], wi, tf, dy_layout="hm"))
        G["rw_w" + nm] = _linear_dw("rw_%s_dw" % nm, xi, drkv[j], tw, dw_cols, dy_layout="hm")
    (dx4a, dx4p), dmix = _tile_bwd(
        "rw_mix_bwd", _rw_mix_fn, mix_tiled, mix_pars,
        [(dxs[0], "nat"), (dxs[1], "nat"), (dxs[2], "nat"), (dlw, "hm"), (dag, "hm"), (dgate, "nat")],
        tt, [True, True], adds=[dx5, None])
    d_mixn1 = dmix[0]
    for k, gpar in zip(("rw_mix", "rw_w0", "rw_w1", "rw_w2", "rw_a0", "rw_a1", "rw_a2", "rw_g1", "rw_g2"), dmix[1:]):
        G[k] = gpar
    dx4 = dx4a + jnp.pad(dx4p[1:], ((0, 1), (0, 0)))
    for k in ("rw_mix", "rw_w2", "rw_a2", "rw_g2"):
        G[k] = _nat_to_col_blocks(G[k])
    for k in ("rw_w1", "rw_a1", "rw_g1", "rw_wr", "rw_wk", "rw_wv", "rw_wo"):
        G[k] = G[k].reshape(N_DEV, d // N_DEV, -1)
    for k in ("rw_w0", "rw_a0", "rw_kk", "rw_ka", "rw_lnx_g", "rw_lnx_b"):
        G[k] = G[k].reshape(N_DEV, 1, d // N_DEV)
    zero = put("rw", {"rw_rk": G["rw_rk"].reshape(N_HEADS, HEAD)}, {k: G[k] for k in RW_SHARDED})
    dx3, zero = fb("ffn10_bwd", "f10", x3, dx4, 1, 0, zero)
    dx2, zero = fb("ffn01_bwd", "f01", x2, dx3, 0, 1, zero)
    dmerged = _linear_dx("attn_out_dx", dx2, tied(w_out, zero), tf)
    (do_sb, *dols), _ = _tile_bwd("merge_bwd", _attn_merge_fn, merge_tiled, [], [(dmerged, "nat")], tt, [True] * 4)
    dq_sb, dk_sb, dv_sb = _sb_bwd(sq, sk, sv, do_sb)
    dil_grads, dbias = None, []
    for g in range(3):
        *dil_grads, db = _dil_group_bwd(g, qn, kn, vd, bias, ols[g], dols[g], dil_grads)
        dbias.append(db)
    dqn, dkn, dvd = dil_grads
    dbias = jnp.concatenate(dbias, axis=1)
    (dproj,), (dqn_w, dkn_w) = _tile_bwd(
        "attn_prep_bwd", _attn_prep_fn, [(proj, "hm")], prep_pars,
        [(dq_sb, "hm"), (dk_sb, "hm"), (dv_sb, "hm"), (dqn, "hm"), (dkn, "hm"), (dvd, "hm")], tt // 2, [True])
    dh0 = _linear_dx("attn_in_dx", dproj, w_in, tf, dy_layout="hm")
    (dx1,), (d_mixn0,) = _tile_bwd("mixnorm0_bwd", _norm_fn, [(x1, "nat")], [mix_norm[0:1]], [(dh0, "nat")], tf,
                                   [True], adds=[dx2])
    order = [(0, 0), (0, 1), (1, 0), (1, 1)]
    norm_grads = lambda: {"ffn_norm": _nat_to_col_blocks(jnp.concatenate([dgn[o] for o in order], axis=0))}
    dx0, zero = fb("ffn00_bwd", "f00", x, dx1, 0, 0, extra=norm_grads)
    G["attn_w_out"] = _linear_dw("attn_out_dw", tied(merged, zero), dx2, tw, dw_cols)
    G["attn_w_in"] = _linear_dw("attn_in_dw", tied(h0, zero), dproj, tw, dw_cols, dy_layout="hm")
    rep_grads = {"mix_norm": jnp.concatenate([d_mixn0, d_mixn1], axis=0), "rel_bias": _bias_tiles_bwd(dbias, s),
                 "attn_q_norm": dqn_w, "attn_k_norm": dkn_w}
    zero = put("att", rep_grads, {k: _nat_to_col_blocks(G[k]) for k in ("attn_w_in", "attn_w_out")})
    return loss, dx0, zero


WEIGHTS = ['ffn_norm', 'ffn_w_gate', 'ffn_w_up', 'ffn_w_down', 'mix_norm', 'rel_bias', 'attn_w_in', 'attn_q_norm',
           'attn_k_norm', 'attn_w_out', 'rw_mix', 'rw_w0', 'rw_w1', 'rw_w2', 'rw_a0', 'rw_a1', 'rw_a2', 'rw_g1', 'rw_g2',
           'rw_kk', 'rw_ka', 'rw_rk', 'rw_wr', 'rw_wk', 'rw_wv', 'rw_wo', 'rw_lnx_g', 'rw_lnx_b']
REPLICATED = ('mix_norm', 'rel_bias', 'attn_q_norm', 'attn_k_norm', 'rw_rk')
BF16_WEIGHTS = ('ffn_w_gate', 'ffn_w_up', 'ffn_w_down', 'attn_w_in', 'attn_w_out', 'rw_wr', 'rw_wk', 'rw_wv', 'rw_wo')


def kernel(x, ffn_norm, ffn_w_gate, ffn_w_up, ffn_w_down, mix_norm, rel_bias, attn_w_in, attn_q_norm, attn_k_norm, attn_w_out, rw_mix, rw_w0, rw_w1, rw_w2, rw_a0, rw_a1, rw_a2, rw_g1, rw_g2, rw_kk, rw_ka, rw_rk, rw_wr, rw_wk, rw_wv, rw_wo, rw_lnx_g, rw_lnx_b, loss_target, m_ffn_norm, m_ffn_w_gate, m_ffn_w_up, m_ffn_w_down, m_mix_norm, m_rel_bias, m_attn_w_in, m_attn_q_norm, m_attn_k_norm, m_attn_w_out, m_rw_mix, m_rw_w0, m_rw_w1, m_rw_w2, m_rw_a0, m_rw_a1, m_rw_a2, m_rw_g1, m_rw_g2, m_rw_kk, m_rw_ka, m_rw_rk, m_rw_wr, m_rw_wk, m_rw_wv, m_rw_wo, m_rw_lnx_g, m_rw_lnx_b, v_ffn_norm, v_ffn_w_gate, v_ffn_w_up, v_ffn_w_down, v_mix_norm, v_rel_bias, v_attn_w_in, v_attn_q_norm, v_attn_k_norm, v_attn_w_out, v_rw_mix, v_rw_w0, v_rw_w1, v_rw_w2, v_rw_a0, v_rw_a1, v_rw_a2, v_rw_g1, v_rw_g2, v_rw_kk, v_rw_ka, v_rw_rk, v_rw_wr, v_rw_wk, v_rw_wv, v_rw_wo, v_rw_lnx_g, v_rw_lnx_b):
    args = locals()
    w = {k: args[k] for k in WEIGHTS}
    cast = lambda k, a: a.astype(BF16) if k in BF16_WEIGHTS else a

    sources = {}
    for l, h in ((0, 0), (0, 1), (1, 0), (1, 1)):
        sources["f%d%d" % (l, h)] = {"gate": cast("ffn_w_gate", ffn_w_gate[l, h]), "up": cast("ffn_w_up", ffn_w_up[l, h]),
                                     "down": cast("ffn_w_down", ffn_w_down[l, h])}
    sources["f00"]["ffn_norm"] = ffn_norm
    drop_lead = lambda a: a[0] if a.ndim == 3 else a
    sources["att"] = {k: cast(k, w[k][0]) for k in ("attn_w_in", "attn_w_out")}
    sources["rw"] = {k: cast(k, drop_lead(w[k])) for k in RW_SHARDED}
    ag, token = {}, None
    for group in AG_GROUPS:
        names, arrays = list(sources[group]), list(sources[group].values())
        if token is not None:
            arrays[0] = arrays[0] + token[0, 0].astype(arrays[0].dtype)
        ag[group] = (names, _exchange_start("ag_start_" + group, arrays, []))
        token = ag[group][1]["token"]
    last_ag_token = token

    def get(group, after):
        names, started = ag[group]
        gathered, _ = _exchange_wait("ag_wait_" + group, started, last_ag_token if after is None else after)
        return dict(zip(names, gathered))

    rs = {}

    def put(group, rep_grads, shard_grads):
        if group in BF16_GRAD_GROUPS:
            shard_grads = {k: v.astype(BF16) for k, v in shard_grads.items()}
        started = _exchange_start("rs_start_" + group, list(rep_grads.values()), list(shard_grads.values()))
        rs[group] = (list(rep_grads), list(shard_grads), started)
        return started["token"]

    loss, dx, last_zero = _step(x[0], loss_target[0], {k: w[k] for k in REPLICATED}, get, put)
    loss = lax.psum(loss, MESH_AXES)

    results = {}
    ffn_prev = {}

    def update(k, parts, row0=0, prev=None):
        c = w[k].shape[-1]
        as2d = lambda a: a.reshape(-1, c)
        return _adamw("adamw_%s_%d" % (k, row0), as2d(w[k]), as2d(args["m_" + k]), as2d(args["v_" + k]),
                      parts.reshape(N_DEV, -1, c), row0, prev)

    def update_small(name, named_parts):
        items = []
        for k, parts in named_parts:
            c = w[k].shape[-1]
            items.append((w[k].reshape(-1, c), args["m_" + k].reshape(-1, c), args["v_" + k].reshape(-1, c),
                          parts.reshape(N_DEV, -1, c)))
        for (k, _), res in zip(named_parts, _adamw_many(name, items)):
            results[k] = res

    after = last_zero
    for group in RS_GROUPS:
        rep_names, shard_names, started = rs[group]
        rep_parts, shard_parts = _exchange_wait("rs_wait_" + group, started, after)
        small = []
        for k, parts in list(zip(rep_names, rep_parts)) + list(zip(shard_names, shard_parts)):
            if k in ("gate", "up", "down"):
                full = "ffn_w_" + k
                piece = 2 * int(group[1]) + int(group[2])
                ffn_prev[full] = update(full, parts, piece * parts.shape[1], ffn_prev.get(full))
                results[full] = ffn_prev[full]
                after = results[full][0]
            elif k == "attn_w_in":
                results[k] = update(k, parts)
                after = results[k][0]
            else:
                small.append((k, parts))
        if small:
            update_small("adamw_small_" + group, small)
            after = results[small[0][0]][0]

    outs = [[results[k][j].reshape(w[k].shape) for k in WEIGHTS] for j in range(4)]
    return (loss, dx[None], *outs[0], *outs[1], *outs[2], *outs[3])
```

```python
import functools
import math

import numpy as np
import jax
import jax.numpy as jnp
from jax import lax
from jax.experimental import pallas as pl
from jax.experimental.pallas import tpu as pltpu

F32, BF16 = jnp.float32, jnp.bfloat16

N_DEV = 8
D_MODEL = 1024
HEAD = 64
N_HEADS = 16
SB_HEADS = 4
DIL_GROUP = 4
DIL_PATTERNS = ((128, 1), (512, 4), (2048, 16))
QBLK = 128
N_BUCKETS = 32
MAX_DISTANCE = 2048
NORM_EPS = 1e-6
GN_EPS = 64e-5
NEG_INF = -1e30
RW_CHUNK = 64
RW_HB = 16
ADAM_LR, ADAM_B1, ADAM_B2, ADAM_EPS, ADAM_WD, ADAM_STEP = 0.001, 0.9, 0.999, 1e-08, 0.01, 10
MESH_AXES = ("x", "y", "c")
VMEM_LIMIT_BYTES = 56 * 1024 * 1024

NN2 = (((1,), (0,)), ((), ()))
NT2 = (((1,), (1,)), ((), ()))
TN2 = (((0,), (0,)), ((), ()))
NN3 = (((2,), (1,)), ((0,), (0,)))
NT3 = (((2,), (2,)), ((0,), (0,)))
TN3 = (((1,), (1,)), ((0,), (0,)))


def _dot(a, b, dims=NN2, prec=None):
    return lax.dot_general(a, b, dims, precision=prec, preferred_element_type=F32)


def _params(sem=None):
    return pltpu.CompilerParams(dimension_semantics=sem, vmem_limit_bytes=VMEM_LIMIT_BYTES)


@jax.custom_vjp
def _mm(x, w):
    return _dot(x.astype(BF16), w.astype(BF16))


def _mm_fwd(x, w):
    return _mm(x, w), (x, w)


def _mm_bwd(res, dy):
    x, w = res
    dyb = dy.astype(BF16)
    return (_dot(dyb, w.astype(BF16), NT2).astype(x.dtype), _dot(x.astype(BF16), dyb, TN2).astype(w.dtype))


_mm.defvjp(_mm_fwd, _mm_bwd)


def _rms(x, g):
    return x * lax.rsqrt(jnp.mean(x * x, axis=-1, keepdims=True) + NORM_EPS) * g


def _log_sigmoid(z):
    return jnp.minimum(z, 0.0) - jnp.log(1.0 + jnp.exp(-jnp.abs(z)))


def _heads_to_nat(v3):
    return jnp.concatenate([v3[h] for h in range(v3.shape[0])], axis=-1)


def _nat_to_heads(v2):
    return jnp.stack([v2[:, h * HEAD:(h + 1) * HEAD] for h in range(v2.shape[1] // HEAD)], axis=0)


def _exchange(name, gathers, scatters):
    n_g = len(gathers)
    arrays = list(gathers) + list(scatters)
    n = len(arrays)
    out_shape = [jax.ShapeDtypeStruct((N_DEV,) + a.shape, a.dtype) for a in gathers]
    out_shape += [jax.ShapeDtypeStruct(a.shape, a.dtype) for a in scatters]

    def body(*refs):
        ins, outs = refs[:n], refs[n:2 * n]
        send_sems, recv_sems, local_sems = refs[2 * n:]
        x, y, c = lax.axis_index("x"), lax.axis_index("y"), lax.axis_index("c")
        me = 4 * x + 2 * y + c

        def src(i, idx):
            return ins[i] if i < n_g else ins[i].at[idx]

        local = [pltpu.make_async_copy(src(i, me), outs[i].at[me], local_sems.at[i]) for i in range(n)]
        for cp in local:
            cp.start()
        remote = []
        for m in range(1, N_DEV):
            px, py, pc = x ^ ((m >> 2) & 1), y ^ ((m >> 1) & 1), c ^ (m & 1)
            peer = 4 * px + 2 * py + pc
            for i in range(n):
                cp = pltpu.make_async_remote_copy(
                    src_ref=src(i, peer), dst_ref=outs[i].at[me],
                    send_sem=send_sems.at[i, m - 1], recv_sem=recv_sems.at[i, m - 1],
                    device_id=(px, py, pc), device_id_type=pl.DeviceIdType.MESH)
                cp.start()
                arrival = pltpu.make_async_remote_copy(
                    src_ref=src(i, peer), dst_ref=outs[i].at[peer],
                    send_sem=send_sems.at[i, m - 1], recv_sem=recv_sems.at[i, m - 1],
                    device_id=(px, py, pc), device_id_type=pl.DeviceIdType.MESH)
                remote.append((cp, arrival))
        for cp, arrival in remote:
            cp.wait_send()
            arrival.wait_recv()
        for cp in local:
            cp.wait()

    hbm = pl.BlockSpec(memory_space=pltpu.HBM)
    outs = pl.pallas_call(
        body, name=name, out_shape=out_shape,
        in_specs=[hbm] * n, out_specs=[hbm] * n,
        scratch_shapes=[pltpu.SemaphoreType.DMA((n, N_DEV - 1)), pltpu.SemaphoreType.DMA((n, N_DEV - 1)),
                        pltpu.SemaphoreType.DMA((n,))],
    )(*arrays)
    return list(outs[:n_g]), list(outs[n_g:])


def _mesh_peers():
    x, y, c = lax.axis_index("x"), lax.axis_index("y"), lax.axis_index("c")
    peers = []
    for m in range(1, N_DEV):
        px, py, pc = x ^ ((m >> 2) & 1), y ^ ((m >> 1) & 1), c ^ (m & 1)
        peers.append((m, (px, py, pc), 4 * px + 2 * py + pc))
    return 4 * x + 2 * y + c, peers


_HBM_SPEC = pl.BlockSpec(memory_space=pltpu.HBM)
_SEM_SPEC = pl.BlockSpec(memory_space=pltpu.SEMAPHORE)
_DATAFLOW = pltpu.SideEffectType.DATAFLOW_SIDE_EFFECTING


def _exchange_start(name, gathers, scatters):
    n_g = len(gathers)
    arrays = list(gathers) + list(scatters)
    n = len(arrays)
    lands = ([lax.empty((N_DEV,) + a.shape, a.dtype) for a in gathers] + [lax.empty(a.shape, a.dtype) for a in scatters])

    def body(*refs):
        ins, land = refs[:n], refs[n:2 * n]
        send_sems, recv_sems, local_sems, token = refs[2 * n], refs[2 * n + 1], refs[2 * n + 2], refs[-1]
        me, peers = _mesh_peers()
        for m, dev, peer in peers:
            for i in range(n):
                k = i * (N_DEV - 1) + m - 1
                pltpu.make_async_remote_copy(
                    src_ref=ins[i] if i < n_g else ins[i].at[peer], dst_ref=land[i].at[me],
                    send_sem=send_sems.at[k], recv_sem=recv_sems.at[k],
                    device_id=dev, device_id_type=pl.DeviceIdType.MESH).start()
        for i in range(n):
            pltpu.make_async_copy(ins[i] if i < n_g else ins[i].at[me], land[i].at[me], local_sems.at[i]).start()
        token[...] = jnp.zeros_like(token)

    sem = pltpu.SemaphoreType.DMA((n * (N_DEV - 1),))
    outs = pl.pallas_call(
        body, name=name,
        out_shape=([sem, sem, pltpu.SemaphoreType.DMA((n,))] + [pltpu.HBM(a.shape, a.dtype) for a in arrays]
                   + [pltpu.HBM(l.shape, l.dtype) for l in lands] + [jax.ShapeDtypeStruct((8, 128), F32)]),
        in_specs=[_HBM_SPEC] * (2 * n),
        out_specs=[_SEM_SPEC] * 3 + [_HBM_SPEC] * (2 * n) + [pl.BlockSpec(memory_space=pltpu.VMEM)],
        input_output_aliases={i: i + 3 for i in range(2 * n)},
        compiler_params=pltpu.CompilerParams(has_side_effects=_DATAFLOW),
    )(*[pltpu.with_memory_space_constraint(a, pltpu.HBM) for a in arrays],
      *[pltpu.with_memory_space_constraint(l, pltpu.HBM) for l in lands])
    return dict(n_g=n_g, n=n, send=outs[0], recv=outs[1], local=outs[2], srcs=list(outs[3:3 + n]),
                lands=list(outs[3 + n:3 + 2 * n]), token=outs[-1])


def _exchange_wait(name, started, after):
    n, n_g = started["n"], started["n_g"]

    def body(*refs):
        srcs, lands = refs[:n], refs[n:2 * n]
        send_sems, recv_sems, local_sems = refs[2 * n], refs[2 * n + 1], refs[2 * n + 2]
        me, peers = _mesh_peers()
        local = [pltpu.make_async_copy(srcs[i] if i < n_g else srcs[i].at[me], lands[i].at[me], local_sems.at[i])
                 for i in range(n)]
        for m, dev, peer in peers:
            for i in range(n):
                k = i * (N_DEV - 1) + m - 1
                cp = pltpu.make_async_remote_copy(
                    src_ref=srcs[i] if i < n_g else srcs[i].at[peer], dst_ref=lands[i].at[peer],
                    send_sem=send_sems.at[k], recv_sem=recv_sems.at[k],
                    device_id=dev, device_id_type=pl.DeviceIdType.MESH)
                cp.wait_send()
                cp.wait_recv()
        for cp in local:
            cp.wait()

    outs = pl.pallas_call(
        body, name=name,
        out_shape=([pltpu.HBM(a.shape, a.dtype) for a in started["srcs"]]
                   + [pltpu.HBM(l.shape, l.dtype) for l in started["lands"]]),
        in_specs=[_HBM_SPEC] * (2 * n) + [_SEM_SPEC] * 3 + [pl.BlockSpec(memory_space=pl.ANY)],
        out_specs=[_HBM_SPEC] * (2 * n), input_output_aliases={i: i for i in range(2 * n)},
        compiler_params=pltpu.CompilerParams(has_side_effects=_DATAFLOW),
    )(*started["srcs"], *started["lands"], started["send"], started["recv"], started["local"], after)
    return list(outs[n:n + n_g]), list(outs[n + n_g:])


def _tile_spec(shape, layout, t):
    if layout == "nat":
        return pl.BlockSpec((t, shape[1]), lambda i: (i, 0))
    return pl.BlockSpec((shape[0], t, shape[2]), lambda i: (0, i, 0))


def _full_spec(shape):
    nd = len(shape)
    return pl.BlockSpec(tuple(shape), lambda i: (0,) * nd)


def _seq_len(a, layout):
    return a.shape[0] if layout == "nat" else a.shape[1]


def _tile_fwd(name, f, tiled, params, outs, t):
    nt, npar = len(tiled), len(params)
    s = _seq_len(*tiled[0])

    def body(*refs):
        vals = [r[...] for r in refs[:nt + npar]]
        res = f(*vals)
        for r, o in zip(refs[nt + npar:], res):
            r[...] = o.astype(r.dtype)

    return pl.pallas_call(
        body, name=name, grid=(s // t,),
        in_specs=[_tile_spec(a.shape, l, t) for a, l in tiled] + [_full_spec(p.shape) for p in params],
        out_specs=[_tile_spec(sh, l, t) for sh, _, l in outs],
        out_shape=[jax.ShapeDtypeStruct(sh, dt) for sh, dt, _ in outs],
        compiler_params=_params(("arbitrary",)),
    )(*[a for a, _ in tiled], *params)


def _tile_bwd(name, f, tiled, params, cts, t, need, adds=None):
    nt, npar, nc = len(tiled), len(params), len(cts)
    s = _seq_len(*tiled[0])
    need_idx = [k for k in range(nt) if need[k]]
    adds = adds or [None] * len(need_idx)
    add_arrays = [(a, tiled[k][1]) for a, k in zip(adds, need_idx) if a is not None]
    n_add = len(add_arrays)

    def body(*refs):
        i = pl.program_id(0)
        vals = [r[...] for r in refs[:nt + npar]]
        ct_refs = refs[nt + npar:nt + npar + nc]
        add_refs = refs[nt + npar + nc:nt + npar + nc + n_add]
        out_refs = refs[nt + npar + nc + n_add:]
        res, vjp = jax.vjp(f, *vals)
        grads = vjp(tuple(r[...].astype(o.dtype) for r, o in zip(ct_refs, res)))
        a = 0
        for j, k in enumerate(need_idx):
            g = grads[k]
            if adds[j] is not None:
                g = g + add_refs[a][...]
                a += 1
            out_refs[j][...] = g.astype(out_refs[j].dtype)
        for j in range(npar):
            r = out_refs[len(need_idx) + j]

            @pl.when(i == 0)
            def _():
                r[...] = jnp.zeros_like(r)

            r[...] += grads[nt + j]

    outs = pl.pallas_call(
        body, name=name, grid=(s // t,),
        in_specs=([_tile_spec(a.shape, l, t) for a, l in tiled] + [_full_spec(p.shape) for p in params]
                  + [_tile_spec(a.shape, l, t) for a, l in cts] + [_tile_spec(a.shape, l, t) for a, l in add_arrays]),
        out_specs=([_tile_spec(tiled[k][0].shape, tiled[k][1], t) for k in need_idx]
                   + [_full_spec(p.shape) for p in params]),
        out_shape=([jax.ShapeDtypeStruct(tiled[k][0].shape, F32) for k in need_idx]
                   + [jax.ShapeDtypeStruct(p.shape, F32) for p in params]),
        compiler_params=_params(("arbitrary",)),
    )(*[a for a, _ in tiled], *params, *[a for a, _ in cts], *[a for a, _ in add_arrays])
    return list(outs[:len(need_idx)]), list(outs[len(need_idx):])


def _linear_fwd(name, x, w, t, out_layout="nat", residual=None):
    s, k = x.shape
    n = w.shape[1]
    has_res = residual is not None

    def body(*refs):
        x_ref, w_ref = refs[0], refs[1]
        o_ref = refs[-1]
        y = _dot(x_ref[...].astype(BF16), w_ref[...])
        if has_res:
            y = y + refs[2][...]
        if out_layout == "hm":
            for h in range(n // HEAD):
                o_ref[h] = y[:, h * HEAD:(h + 1) * HEAD]
        else:
            o_ref[...] = y

    out_sh = (s, n) if out_layout == "nat" else (n // HEAD, s, HEAD)
    ins = [x, w] + ([residual] if has_res else [])
    in_specs = [_tile_spec(x.shape, "nat", t), _full_spec(w.shape)] + ([_tile_spec((s, n), "nat", t)] if has_res else [])
    return pl.pallas_call(
        body, name=name, grid=(s // t,), in_specs=in_specs,
        out_specs=_tile_spec(out_sh, out_layout, t), out_shape=jax.ShapeDtypeStruct(out_sh, F32),
        compiler_params=_params(("arbitrary",)),
    )(*ins)


def _linear_dx(name, dy, w, t, dy_layout="nat"):
    k, n = w.shape
    s = _seq_len(dy, dy_layout)

    def body(dy_ref, w_ref, o_ref):
        dy = _heads_to_nat(dy_ref[...].astype(BF16)) if dy_layout == "hm" else dy_ref[...].astype(BF16)
        o_ref[...] = _dot(dy, w_ref[...], NT2)

    return pl.pallas_call(
        body, name=name, grid=(s // t,),
        in_specs=[_tile_spec(dy.shape, dy_layout, t), _full_spec(w.shape)],
        out_specs=_tile_spec((s, k), "nat", t), out_shape=jax.ShapeDtypeStruct((s, k), F32),
        compiler_params=_params(("arbitrary",)),
    )(dy, w)


def _linear_dw(name, x, dy, t, nb, dy_layout="nat"):
    s, k = x.shape
    n = dy.shape[1] if dy_layout == "nat" else dy.shape[0] * HEAD

    def body(x_ref, dy_ref, o_ref):
        i = pl.program_id(1)

        @pl.when(i == 0)
        def _():
            o_ref[...] = jnp.zeros_like(o_ref)

        dy = _heads_to_nat(dy_ref[...].astype(BF16)) if dy_layout == "hm" else dy_ref[...].astype(BF16)
        o_ref[...] += _dot(x_ref[...].astype(BF16), dy, TN2)

    if dy_layout == "hm":
        dy_spec = pl.BlockSpec((nb // HEAD, t, HEAD), lambda j, i: (j, i, 0))
    else:
        dy_spec = pl.BlockSpec((t, nb), lambda j, i: (i, j))
    return pl.pallas_call(
        body, name=name, grid=(n // nb, s // t),
        in_specs=[pl.BlockSpec((t, k), lambda j, i: (i, 0)), dy_spec],
        out_specs=pl.BlockSpec((k, nb), lambda j, i: (0, j)), out_shape=jax.ShapeDtypeStruct((k, n), F32),
        compiler_params=_params(("arbitrary", "arbitrary")),
    )(x, dy)


def _ffn_fwd(name, x, gn, wg, wu, wd, t):
    s, d = x.shape
    f8 = wg.shape[-1]

    def body(x_ref, g_ref, wg_ref, wu_ref, wd_ref, o_ref, gk_ref, uk_ref, h_scr, acc):
        k = pl.program_id(1)

        @pl.when(k == 0)
        def _():
            h_scr[...] = _rms(x_ref[...], g_ref[...]).astype(BF16)
            acc[...] = jnp.zeros_like(acc)

        hb = h_scr[...]
        gk = _dot(hb, wg_ref[0])
        uk = _dot(hb, wu_ref[0])
        gk_ref[0] = gk
        uk_ref[0] = uk
        a = gk * jax.nn.sigmoid(gk) * uk
        acc[...] += _dot(a.astype(BF16), wd_ref[0])

        @pl.when(k == N_DEV - 1)
        def _():
            o_ref[...] = x_ref[...] + 0.5 * acc[...]

    wspec = lambda shp: pl.BlockSpec((1,) + shp, lambda i, k: (k, 0, 0))
    act = pl.BlockSpec((1, t, f8), lambda i, k: (k, i, 0))
    act_sh = jax.ShapeDtypeStruct((N_DEV, s, f8), F32)
    return pl.pallas_call(
        body, name=name, grid=(s // t, N_DEV),
        in_specs=[pl.BlockSpec((t, d), lambda i, k: (i, 0)), pl.BlockSpec((1, d), lambda i, k: (0, 0)),
                  wspec((d, f8)), wspec((d, f8)), wspec((f8, d))],
        out_specs=[pl.BlockSpec((t, d), lambda i, k: (i, 0)), act, act],
        out_shape=[jax.ShapeDtypeStruct((s, d), F32), act_sh, act_sh],
        scratch_shapes=[pltpu.VMEM((t, d), BF16), pltpu.VMEM((t, d), F32)],
        compiler_params=_params(("arbitrary", "arbitrary")),
    )(x, gn, wg, wu, wd)


def _ffn_bwd(name, x, dy, gn, wg, wu, wd, gact, uact, t):
    s, d = x.shape
    f8 = wg.shape[-1]
    last = N_DEV - 1

    def body(x_ref, dy_ref, g_ref, wg_ref, wu_ref, wd_ref, gk_ref, uk_ref,
             dx_ref, dg_ref, dwg_ref, dwu_ref, dwd_ref, dh_scr):
        k, i = pl.program_id(0), pl.program_id(1)
        x = x_ref[...]
        rs = lax.rsqrt(jnp.mean(x * x, axis=-1, keepdims=True) + NORM_EPS)
        xn = x * rs
        hb = (xn * g_ref[...]).astype(BF16)
        dob = (0.5 * dy_ref[...]).astype(BF16)
        wgk, wuk, wdk = wg_ref[0], wu_ref[0], wd_ref[0]
        gk, uk = gk_ref[0], uk_ref[0]
        sg = jax.nn.sigmoid(gk)
        sk = gk * sg
        da = _dot(dob, wdk, NT2)
        du = (da * sk).astype(BF16)
        dg = (da * uk * (sg * (1.0 + gk * (1.0 - sg)))).astype(BF16)

        dwd_c = _dot((sk * uk).astype(BF16), dob, TN2)
        dwg_c = _dot(hb, dg, TN2)
        dwu_c = _dot(hb, du, TN2)
        dh = _dot(dg, wgk, NT2) + _dot(du, wuk, NT2)
        rows = pl.ds(pl.multiple_of(i * t, t), t)

        @pl.when(i == 0)
        def _():
            dwg_ref[0], dwu_ref[0], dwd_ref[0] = dwg_c, dwu_c, dwd_c

        @pl.when(i > 0)
        def _():
            dwg_ref[0] += dwg_c
            dwu_ref[0] += dwu_c
            dwd_ref[0] += dwd_c

        @pl.when(k == 0)
        def _():
            dh_scr[rows, :] = dh

        @pl.when(k > 0)
        def _():
            dh_scr[rows, :] += dh

        @pl.when(jnp.logical_and(k == last, i == 0))
        def _():
            dg_ref[...] = jnp.zeros_like(dg_ref)

        @pl.when(k == last)
        def _():
            dht = dh_scr[rows, :]
            dg_ref[...] += jnp.sum(dht * xn, axis=0, keepdims=True)
            dxn = dht * g_ref[...]
            dx_ref[...] = dy_ref[...] + rs * (dxn - xn * jnp.mean(dxn * xn, axis=-1, keepdims=True))

    wspec = lambda shp: pl.BlockSpec((1,) + shp, lambda k, i: (k, 0, 0))
    tile = pl.BlockSpec((t, d), lambda k, i: (i, 0))
    act = pl.BlockSpec((1, t, f8), lambda k, i: (k, i, 0))
    return pl.pallas_call(
        body, name=name, grid=(N_DEV, s // t),
        in_specs=[tile, tile, pl.BlockSpec((1, d), lambda k, i: (0, 0)), wspec((d, f8)), wspec((d, f8)), wspec((f8, d)),
                  act, act],
        out_specs=[pl.BlockSpec((t, d), lambda k, i: (jnp.where(k == last, i, 0), 0)),
                   pl.BlockSpec((1, d), lambda k, i: (0, 0)),
                   pl.BlockSpec((1, d, f8), lambda k, i: (k, 0, 0)), pl.BlockSpec((1, d, f8), lambda k, i: (k, 0, 0)),
                   pl.BlockSpec((1, f8, d), lambda k, i: (k, 0, 0))],
        out_shape=[jax.ShapeDtypeStruct((s, d), F32), jax.ShapeDtypeStruct((1, d), F32),
                   jax.ShapeDtypeStruct((N_DEV, d, f8), F32), jax.ShapeDtypeStruct((N_DEV, d, f8), F32),
                   jax.ShapeDtypeStruct((N_DEV, f8, d), F32)],
        scratch_shapes=[pltpu.VMEM((s, d), F32)],
        compiler_params=_params(("arbitrary", "arbitrary")),
    )(x, dy, gn, wg, wu, wd, gact, uact)


def _loss_head(y, target, t):
    s, d = y.shape

    def body(y_ref, t_ref, dy_ref, l_ref):
        i = pl.program_id(0)
        err = y_ref[...] - t_ref[...]
        dy_ref[...] = err * (1.0 / d)

        @pl.when(i == 0)
        def _():
            l_ref[...] = jnp.zeros_like(l_ref)

        l_ref[...] += 0.5 * jnp.sum(jnp.mean(err * err, axis=-1, keepdims=True), axis=0, keepdims=True)

    tile = pl.BlockSpec((t, d), lambda i: (i, 0))
    dy, l = pl.pallas_call(
        body, name="loss_head", grid=(s // t,), in_specs=[tile, tile],
        out_specs=[tile, pl.BlockSpec((1, 1), lambda i: (0, 0))],
        out_shape=[jax.ShapeDtypeStruct((s, d), F32), jax.ShapeDtypeStruct((1, 1), F32)],
        compiler_params=_params(("arbitrary",)),
    )(y, target)
    return dy, l[0, 0]


SB_KEY_TILE = 1024
SB_HEADS_PER_STEP = 4


def _sb_scan_mats():
    row = lax.broadcasted_iota(jnp.int32, (QBLK, QBLK), 0)
    col = lax.broadcasted_iota(jnp.int32, (QBLK, QBLK), 1)
    return (row > col).astype(F32).astype(BF16), (row < col).astype(F32).astype(BF16)


def _sb_tile_scan(x, mat, reverse):
    nsub = x.shape[1] // QBLK
    outs, carry = [None] * nsub, jnp.zeros((x.shape[0], 1), F32)
    for i in (reversed(range(nsub)) if reverse else range(nsub)):
        xs = x[:, i * QBLK:(i + 1) * QBLK]
        hi = xs.astype(BF16)
        lo = (xs - hi.astype(F32)).astype(BF16)
        outs[i] = _dot(hi, mat) + _dot(lo, mat) + carry
        carry = carry + jnp.sum(xs, axis=1, keepdims=True)
    return jnp.concatenate(outs, axis=1), carry


def _sb_before_query(n, t, kt):
    row = lax.broadcasted_iota(jnp.int32, (QBLK, kt), 0)
    col = lax.broadcasted_iota(jnp.int32, (QBLK, kt), 1)
    return t * kt + col < n * QBLK + row


def _sb_fwd(q, k, v):
    _, s, _ = q.shape
    scale = HEAD ** -0.5
    kt = min(SB_KEY_TILE, s)

    def body(q_ref, k_ref, v_ref, o_ref):
        n = pl.program_id(1)
        suffix, _ = _sb_scan_mats()
        n_tiles = lax.div(n, jnp.int32(kt // QBLK)) + 1
        heads = range(SB_HEADS_PER_STEP)
        qb = [(q_ref[h] * scale).astype(q_ref.dtype) for h in heads]

        def tile(t, carry, diagonal):
            rows = pl.ds(pl.multiple_of(t * kt, kt), kt)
            out = []
            for h in heads:
                c, acc = carry[h]
                z = _dot(qb[h], k_ref[h, rows, :], NT2)
                lk = _log_sigmoid(-z)
                log_beta = z + lk
                if diagonal:
                    ok = _sb_before_query(n, t, kt)
                    lk = jnp.where(ok, lk, 0.0)
                later, total = _sb_tile_scan(lk, suffix, True)
                w = jnp.exp(log_beta + later + c)
                if diagonal:
                    w = jnp.where(ok, w, 0.0)
                out.append((c + total, acc + _dot(w.astype(BF16), v_ref[h, rows, :])))
            return tuple(out)

        zero = (jnp.zeros((QBLK, 1), F32), jnp.zeros((QBLK, HEAD), F32))
        carry = tile(n_tiles - 1, (zero,) * len(heads), True)
        carry = lax.fori_loop(1, n_tiles, lambda jj, cr: tile(n_tiles - 1 - jj, cr, False), carry)
        for h in heads:
            o_ref[h] = carry[h][1]

    hp = SB_HEADS_PER_STEP
    return pl.pallas_call(
        body, name="sb_fwd", grid=(SB_HEADS // hp, s // QBLK),
        in_specs=[pl.BlockSpec((hp, QBLK, HEAD), lambda h, n: (h, n, 0)),
                  pl.BlockSpec((hp, s, HEAD), lambda h, n: (h, 0, 0)),
                  pl.BlockSpec((hp, s, HEAD), lambda h, n: (h, 0, 0))],
        out_specs=pl.BlockSpec((hp, QBLK, HEAD), lambda h, n: (h, n, 0)),
        out_shape=jax.ShapeDtypeStruct((SB_HEADS, s, HEAD), F32),
        compiler_params=_params(("arbitrary", "arbitrary")),
    )(q, k, v)


def _sb_bwd(q, k, v, do):
    _, s, _ = q.shape
    scale = HEAD ** -0.5
    kt = min(SB_KEY_TILE, s)

    def body(q_ref, k_ref, v_ref, do_ref, dq_ref, dk_ref, dv_ref, e_scr, beta_scr):
        n = pl.program_id(1)

        @pl.when(n == 0)
        def _():
            dk_ref[...] = jnp.zeros_like(dk_ref)
            dv_ref[...] = jnp.zeros_like(dv_ref)

        suffix, prefix = _sb_scan_mats()
        n_tiles = lax.div(n, jnp.int32(kt // QBLK)) + 1
        heads = range(SB_HEADS_PER_STEP)
        qb = [(q_ref[h] * scale).astype(q_ref.dtype) for h in heads]
        dob = [do_ref[h].astype(BF16) for h in heads]

        def weights(t, cs, diagonal):
            rows = pl.ds(pl.multiple_of(t * kt, kt), kt)
            out, stores = [], []
            for h in heads:
                vb = v_ref[h, rows, :]
                z = _dot(qb[h], k_ref[h, rows, :], NT2)
                lk = _log_sigmoid(-z)
                log_beta = z + lk
                if diagonal:
                    ok = _sb_before_query(n, t, kt)
                    lk = jnp.where(ok, lk, 0.0)
                later, total = _sb_tile_scan(lk, suffix, True)
                w = jnp.exp(log_beta + later + cs[h])
                if diagonal:
                    w = jnp.where(ok, w, 0.0)
                stores.append((w * _dot(dob[h], vb, NT2), jnp.exp(log_beta), _dot(w.astype(BF16), dob[h], TN2)))
                out.append(cs[h] + total)
            for h in heads:
                e_scr[h, t], beta_scr[h, t] = stores[h][0], stores[h][1]
                dv_ref[h, rows, :] += stores[h][2]
            return tuple(out)

        col0 = jnp.zeros((QBLK, 1), F32)
        cs = weights(n_tiles - 1, (col0,) * len(heads), True)
        lax.fori_loop(1, n_tiles, lambda jj, c: weights(n_tiles - 1 - jj, c, False), cs)

        def grads(t, carry, diagonal):
            rows = pl.ds(pl.multiple_of(t * kt, kt), kt)
            out, dks = [], []
            for h in heads:
                pc, dq = carry[h]
                kb = k_ref[h, rows, :]
                e, beta = e_scr[h, t], beta_scr[h, t]
                before, total = _sb_tile_scan(e, prefix, False)
                dz = e * (1.0 - beta) - beta * (before + pc)
                if diagonal:
                    dz = jnp.where(_sb_before_query(n, t, kt), dz, 0.0)
                dz = dz.astype(BF16)
                dks.append(_dot(dz, qb[h], TN2))
                out.append((pc + total, dq + _dot(dz, kb)))
            for h in heads:
                dk_ref[h, rows, :] += dks[h]
            return tuple(out)

        zero = (col0, jnp.zeros((QBLK, HEAD), F32))
        carry = lax.fori_loop(0, n_tiles - 1, lambda t, cr: grads(t, cr, False), (zero,) * len(heads))
        carry = grads(n_tiles - 1, carry, True)
        for h in heads:
            dq_ref[h] = carry[h][1] * scale

    hp = SB_HEADS_PER_STEP
    qspec = pl.BlockSpec((hp, QBLK, HEAD), lambda h, n: (h, n, 0))
    full = pl.BlockSpec((hp, s, HEAD), lambda h, n: (h, 0, 0))
    sh = jax.ShapeDtypeStruct((SB_HEADS, s, HEAD), F32)
    tiles_sh = (hp, s // kt, QBLK, kt)
    return pl.pallas_call(
        body, name="sb_bwd", grid=(SB_HEADS // hp, s // QBLK),
        in_specs=[qspec, full, full, qspec],
        out_specs=[qspec, full, full], out_shape=[sh, sh, sh],
        scratch_shapes=[pltpu.VMEM(tiles_sh, F32), pltpu.VMEM(tiles_sh, F32)],
        compiler_params=_params(("arbitrary", "arbitrary")),
    )(q, k, v, do)


def _t5_bucket_np(dist):
    max_exact = N_BUCKETS // 2
    d = np.maximum(dist, 1).astype(np.float32)
    large = max_exact + (np.log(d / np.float32(max_exact)) / np.float32(math.log(MAX_DISTANCE / max_exact))
                         * np.float32(N_BUCKETS - max_exact)).astype(np.int32)
    large = np.minimum(large, N_BUCKETS - 1)
    return np.where(dist < max_exact, dist, large)


def _dil_layout(s):
    assert all(s % (QBLK * r) == 0 and window // r == QBLK for window, r in DIL_PATTERNS)
    tiles, buckets = [], []
    i = np.arange(QBLK)[:, None]
    j = np.arange(QBLK)[None, :]
    for g, (window, r) in enumerate(DIL_PATTERNS):
        for off in (0, 1):
            dist = QBLK * off + i - j
            ok = (dist >= 0) & (dist <= window // r)
            tiles.append((g, off))
            buckets.append(np.where(ok, _t5_bucket_np(np.maximum(dist, 0) * r), -1).astype(np.int32))
    return tiles, np.stack(buckets)


def _bias_tiles(rel_bias, s):
    tiles, buckets = _dil_layout(s)
    nt = len(tiles)
    present = [sorted(set(np.unique(buckets[k]).tolist()) - {-1}) for k in range(nt)]

    def body(rel_ref, b_ref, o_ref):
        j = pl.program_id(0)
        for k, (g, _) in enumerate(tiles):
            bk = b_ref[k]
            tile = jnp.full((QBLK, QBLK), NEG_INF, F32)
            for b in present[k]:
                tile = jnp.where(bk == b, rel_ref[b, g * DIL_GROUP + j], tile)
            o_ref[0, k] = tile

    return pl.pallas_call(
        body, name="bias_tiles", grid=(DIL_GROUP,),
        in_specs=[pl.BlockSpec(memory_space=pltpu.SMEM), pl.BlockSpec((nt, QBLK, QBLK), lambda j: (0, 0, 0))],
        out_specs=pl.BlockSpec((1, nt, QBLK, QBLK), lambda j: (j, 0, 0, 0)),
        out_shape=jax.ShapeDtypeStruct((DIL_GROUP, nt, QBLK, QBLK), F32),
        compiler_params=_params(("arbitrary",)),
    )(rel_bias, jnp.asarray(buckets))


def _bias_tiles_bwd(dbias, s):
    tiles, buckets = _dil_layout(s)
    nt = len(tiles)
    present = [sorted(set(np.unique(buckets[k]).tolist()) - {-1}) for k in range(nt)]

    def body(d_ref, b_ref, o_ref):
        j = pl.program_id(0)

        @pl.when(j == 0)
        def _():
            for b in range(N_BUCKETS):
                for col in range(3 * DIL_GROUP):
                    o_ref[b, col] = jnp.float32(0.0)

        for k, (g, _) in enumerate(tiles):
            bk, dk = b_ref[k], d_ref[0, k]
            for b in present[k]:
                o_ref[b, g * DIL_GROUP + j] += jnp.sum(jnp.where(bk == b, dk, 0.0))

    return pl.pallas_call(
        body, name="bias_tiles_bwd", grid=(DIL_GROUP,),
        in_specs=[pl.BlockSpec((1, nt, QBLK, QBLK), lambda j: (j, 0, 0, 0)),
                  pl.BlockSpec((nt, QBLK, QBLK), lambda j: (0, 0, 0))],
        out_specs=pl.BlockSpec(memory_space=pltpu.SMEM),
        out_shape=jax.ShapeDtypeStruct((N_BUCKETS, 3 * DIL_GROUP), F32),
        compiler_params=_params(("arbitrary",)),
    )(dbias, jnp.asarray(buckets))


DIL_PAIRS_PER_STEP = 8


def _dil_rows(g, s, pair):
    _, r = DIL_PATTERNS[g]
    nb = s // (QBLK * r)
    c, n = lax.div(pair, jnp.int32(nb)), lax.rem(pair, jnp.int32(nb))
    start = c + (r * QBLK) * n
    before = jnp.where(n > 0, start - r * QBLK, start)
    if r == 1:
        return pl.ds(start, QBLK), pl.ds(before, QBLK), n > 0
    return pl.ds(start, QBLK, stride=r), pl.ds(before, QBLK, stride=r), n > 0


def _dil_logits(qb, k_ref, rows, before, has_before, b_ref):
    k0, k1 = k_ref[0, rows, :].astype(BF16), k_ref[0, before, :].astype(BF16)
    l0 = _dot(qb, k0, NT2) + b_ref[0, 0]
    l1 = jnp.where(has_before, _dot(qb, k1, NT2) + b_ref[0, 1], NEG_INF)
    return k0, k1, l0, l1


def _dil_group_specs(g, s):
    head = pl.BlockSpec((1, s, HEAD), lambda j, p: (DIL_GROUP * g + j, 0, 0))
    return [head, head, head, pl.BlockSpec((1, 2, QBLK, QBLK), lambda j, p: (j, g, 0, 0))]


def _dil_group_fwd(g, qn, kn, v, bias):
    _, s, _ = qn.shape
    scale = HEAD ** -0.5
    steps = (s // QBLK) // DIL_PAIRS_PER_STEP

    def body(q_ref, k_ref, v_ref, b_ref, o_ref):
        pairs = [_dil_rows(g, s, pl.program_id(1) * DIL_PAIRS_PER_STEP + u) for u in range(DIL_PAIRS_PER_STEP)]
        loaded = []
        for rows, before, has_before in pairs:
            qb = (q_ref[0, rows, :] * scale).astype(BF16)
            _, _, l0, l1 = _dil_logits(qb, k_ref, rows, before, has_before, b_ref)
            loaded.append((l0, l1, v_ref[0, rows, :].astype(BF16), v_ref[0, before, :].astype(BF16)))
        results = []
        for l0, l1, v0, v1 in loaded:
            m = jnp.max(jnp.maximum(l0, l1), axis=1, keepdims=True)
            p0, p1 = jnp.exp(l0 - m), jnp.exp(l1 - m)
            den = jnp.sum(p0 + p1, axis=1, keepdims=True)
            inv = 1.0 / den
            o = _dot((p0 * inv).astype(BF16), v0) + _dot((p1 * inv).astype(BF16), v1)
            results.append(jnp.concatenate([o, jnp.broadcast_to(m + jnp.log(den), (QBLK, HEAD))], axis=1))
        for (rows, _, _), res in zip(pairs, results):
            o_ref[0, rows, :] = res

    return pl.pallas_call(
        body, name="dil%d_fwd" % g, grid=(DIL_GROUP, steps), in_specs=_dil_group_specs(g, s),
        out_specs=pl.BlockSpec((1, s, 2 * HEAD), lambda j, p: (j, 0, 0)),
        out_shape=jax.ShapeDtypeStruct((DIL_GROUP, s, 2 * HEAD), F32),
        compiler_params=_params(("arbitrary", "arbitrary")),
    )(qn, kn, v, bias)


def _dil_group_bwd(g, qn, kn, v, bias, ol, dol, prev):
    _, s, _ = qn.shape
    scale = HEAD ** -0.5
    steps = (s // QBLK) // DIL_PAIRS_PER_STEP
    prev = list(prev) if prev is not None else []

    def body(q_ref, k_ref, v_ref, b_ref, ol_ref, dol_ref, *rest):
        dq_ref, dk_ref, dv_ref, db_ref = rest[-4:]

        @pl.when(pl.program_id(1) == 0)
        def _():
            for r in (dk_ref, dv_ref, db_ref):
                r[...] = jnp.zeros_like(r)

        pairs = [_dil_rows(g, s, pl.program_id(1) * DIL_PAIRS_PER_STEP + u) for u in range(DIL_PAIRS_PER_STEP)]
        loaded = []
        for rows, before, has_before in pairs:
            qb = (q_ref[0, rows, :] * scale).astype(BF16)
            k0, k1, l0, l1 = _dil_logits(qb, k_ref, rows, before, has_before, b_ref)
            v0, v1 = v_ref[0, rows, :].astype(BF16), v_ref[0, before, :].astype(BF16)
            loaded.append((qb, k0, k1, l0, l1, v0, v1, ol_ref[0, rows, :], dol_ref[0, rows, :]))
        grads = []
        for qb, k0, k1, l0, l1, v0, v1, out_lse, d_out_lse in loaded:
            o, lse = out_lse[:, :HEAD], out_lse[:, HEAD:HEAD + 1]
            do, dlse = d_out_lse[:, :HEAD], d_out_lse[:, HEAD:HEAD + 1]
            dob = do.astype(BF16)
            p0, p1 = jnp.exp(l0 - lse), jnp.exp(l1 - lse)
            shift = dlse - jnp.sum(do * o, axis=1, keepdims=True)
            dl0 = p0 * (_dot(dob, v0, NT2) + shift)
            dl1 = p1 * (_dot(dob, v1, NT2) + shift)
            dl0b, dl1b = dl0.astype(BF16), dl1.astype(BF16)
            grads.append(((_dot(dl0b, k0) + _dot(dl1b, k1)) * scale,
                          _dot(dl0b, qb, TN2), _dot(dl1b, qb, TN2),
                          _dot(p0.astype(BF16), dob, TN2), _dot(p1.astype(BF16), dob, TN2), dl0, dl1))
        db0 = functools.reduce(jnp.add, [gr[5] for gr in grads])
        db1 = functools.reduce(jnp.add, [gr[6] for gr in grads])
        for (rows, before, _), (dq, dk0, dk1, dv0, dv1, _, _) in zip(pairs, grads):
            dq_ref[0, rows, :] = dq
            dk_ref[0, rows, :] += dk0
            dk_ref[0, before, :] += dk1
            dv_ref[0, rows, :] += dv0
            dv_ref[0, before, :] += dv1
        db_ref[0, 0] += db0
        db_ref[0, 1] += db1

    head_out = pl.BlockSpec((1, s, HEAD), lambda j, p: (DIL_GROUP * g + j, 0, 0))
    rows128 = pl.BlockSpec((1, s, 2 * HEAD), lambda j, p: (j, 0, 0))
    full_sh = jax.ShapeDtypeStruct(qn.shape, F32)
    return pl.pallas_call(
        body, name="dil%d_bwd" % g, grid=(DIL_GROUP, steps),
        in_specs=_dil_group_specs(g, s) + [rows128, rows128] + [pl.BlockSpec(memory_space=pl.ANY)] * len(prev),
        out_specs=[head_out, head_out, head_out, pl.BlockSpec((1, 2, QBLK, QBLK), lambda j, p: (j, 0, 0, 0))],
        out_shape=[full_sh, full_sh, full_sh, jax.ShapeDtypeStruct((DIL_GROUP, 2, QBLK, QBLK), F32)],
        input_output_aliases={6 + i: i for i in range(len(prev))},
        compiler_params=_params(("arbitrary", "arbitrary")),
    )(qn, kn, v, bias, ol, dol, *prev)


@functools.partial(jax.custom_vjp, nondiff_argnums=(2,))
def _bdot(a, b, dims):
    return _dot(a.astype(BF16), b.astype(BF16), dims)


def _bdot_fwd(a, b, dims):
    return _bdot(a, b, dims), (a, b)


def _bdot_bwd(dims, res, dc):
    a, b = res
    nn, nt, tn = (NN2, NT2, TN2) if dims in (NN2, NT2, TN2) else (NN3, NT3, TN3)
    if dims == nn:
        return _bdot(dc, b, nt), _bdot(a, dc, tn)
    if dims == nt:
        return _bdot(dc, b, nn), _bdot(dc, a, tn)
    return _bdot(b, dc, nt), _bdot(a, dc, nn)


_bdot.defvjp(_bdot_fwd, _bdot_bwd)


def _ones_dot(ones, x, dims):
    o = ones.astype(BF16)
    hi = x.astype(BF16)
    r1 = x - hi.astype(F32)
    mid = r1.astype(BF16)
    lo = (r1 - mid.astype(F32)).astype(BF16)
    return _dot(o, hi, dims) + _dot(o, mid, dims) + _dot(o, lo, dims)


@jax.custom_vjp
def _prefix_sums(x):
    c = x.shape[1]
    row = lax.broadcasted_iota(jnp.int32, (x.shape[0], c, c), 1)
    col = lax.broadcasted_iota(jnp.int32, (x.shape[0], c, c), 2)
    return _ones_dot((row >= col).astype(F32), x, NN3)


def _prefix_sums_fwd(x):
    return _prefix_sums(x), None


def _prefix_sums_bwd(_, dy):
    c = dy.shape[1]
    row = lax.broadcasted_iota(jnp.int32, (dy.shape[0], c, c), 1)
    col = lax.broadcasted_iota(jnp.int32, (dy.shape[0], c, c), 2)
    return (_ones_dot((row <= col).astype(F32), dy, NN3),)


_prefix_sums.defvjp(_prefix_sums_fwd, _prefix_sums_bwd)


def _rwkv_chunk(s0, r, lw, kraw, v, ag, kk_w, ka_w, rk_w, lng, lnb):
    hb, c, _ = r.shape
    kk = kraw * kk_w
    kk = kk / jnp.maximum(jnp.sqrt(jnp.sum(kk * kk, axis=-1, keepdims=True)), 1e-12)
    k = kraw * (1.0 + (ag - 1.0) * ka_w)
    a = -kk
    b = kk * ag
    row = lax.broadcasted_iota(jnp.int32, (hb, c, c), 1)
    col = lax.broadcasted_iota(jnp.int32, (hb, c, c), 2)
    lower, strict = row >= col, row > col
    cum = _prefix_sums(lw)
    ecum, einv = jnp.exp(cum), jnp.exp(-cum)
    rt, kt, bt = r * ecum, k * einv, b * einv
    at = a * jnp.exp(cum - lw)
    ar = jnp.concatenate([at, rt], axis=1)
    scores = _bdot(ar, jnp.concatenate([bt, kt], axis=1), NT3)
    a_ab = jnp.where(strict, scores[:, :c, :c], 0.0)
    a_ak = jnp.where(strict, scores[:, :c, c:], 0.0)
    p_rb = jnp.where(lower, scores[:, c:, :c], 0.0)
    p_rk = jnp.where(lower, scores[:, c:, c:], 0.0)
    from_s0 = _bdot(ar, s0, NT3)
    rhs = from_s0[:, :c] + _bdot(a_ak, v, NN3)
    inv = (row == col).astype(F32) + a_ab
    pw = a_ab
    for _ in range(int(math.log2(c)) - 1):
        pw = _bdot(pw, pw, NN3)
        inv = inv + _bdot(inv, pw, NN3)
    u = _bdot(inv, rhs, NN3)
    uv = jnp.concatenate([u, v], axis=1)
    y = from_s0[:, c:] + _bdot(jnp.concatenate([p_rb, p_rk], axis=2), uv, NN3)
    cum_end = cum[:, c - 1:c, :]
    dec = jnp.exp(cum_end - cum)
    s_end = s0 * jnp.exp(cum_end) + _bdot(uv, jnp.concatenate([b * dec, k * dec], axis=1), TN3)
    mu = jnp.mean(y, axis=-1, keepdims=True)
    var = jnp.mean(jnp.square(y - mu), axis=-1, keepdims=True)
    z = (y - mu) * lax.rsqrt(var + GN_EPS) * lng + lnb + jnp.sum(r * k * rk_w, axis=-1, keepdims=True) * v
    return z, s_end


def _rwkv_specs(nc, rev):
    cidx = (lambda c: nc - 1 - c) if rev else (lambda c: c)
    seq = pl.BlockSpec((RW_HB, RW_CHUNK, HEAD), lambda hg, c: (hg, cidx(c), 0))
    par = pl.BlockSpec((RW_HB, 1, HEAD), lambda hg, c: (hg, 0, 0))
    st = pl.BlockSpec((1, RW_HB, HEAD, HEAD), lambda hg, c: (cidx(c), hg, 0, 0))
    return seq, par, st


def _rwkv_fwd(seqs, pars):
    s = seqs[0].shape[1]
    nc = s // RW_CHUNK

    def body(*refs):
        seq_refs, par_refs = refs[:5], refs[5:10]
        z_ref, st_ref, state = refs[10:]
        c = pl.program_id(1)

        @pl.when(c == 0)
        def _():
            state[...] = jnp.zeros_like(state)

        s0 = state[...]
        st_ref[0] = s0
        z, s_end = _rwkv_chunk(s0, *[r[...] for r in seq_refs], *[r[...] for r in par_refs])
        z_ref[...] = z
        state[...] = s_end

    seq, par, st = _rwkv_specs(nc, False)
    return pl.pallas_call(
        body, name="rwkv_fwd", grid=(N_HEADS // RW_HB, nc),
        in_specs=[seq] * 5 + [par] * 5, out_specs=[seq, st],
        out_shape=[jax.ShapeDtypeStruct((N_HEADS, s, HEAD), F32), jax.ShapeDtypeStruct((nc, N_HEADS, HEAD, HEAD), F32)],
        scratch_shapes=[pltpu.VMEM((RW_HB, HEAD, HEAD), F32)],
        compiler_params=_params(("arbitrary", "arbitrary")),
    )(*seqs, *pars)


def _rwkv_bwd(seqs, pars, states, dz):
    s = seqs[0].shape[1]
    nc = s // RW_CHUNK

    def body(*refs):
        seq_refs, par_refs = refs[:5], refs[5:10]
        st_ref, dz_ref = refs[10:12]
        dseq_refs, dpar_refs, dstate = refs[12:17], refs[17:22], refs[22]
        c = pl.program_id(1)

        @pl.when(c == 0)
        def _():
            dstate[...] = jnp.zeros_like(dstate)
            for r in dpar_refs:
                r[...] = jnp.zeros_like(r)

        _, vjp = jax.vjp(_rwkv_chunk, st_ref[0], *[r[...] for r in seq_refs], *[r[...] for r in par_refs])
        g = vjp((dz_ref[...], dstate[...]))
        dstate[...] = g[0]
        for r, gs in zip(dseq_refs, g[1:6]):
            r[...] = gs
        for r, gp in zip(dpar_refs, g[6:]):
            r[...] += gp

    seq, par, st = _rwkv_specs(nc, True)
    seq_sh = jax.ShapeDtypeStruct((N_HEADS, s, HEAD), F32)
    par_sh = jax.ShapeDtypeStruct((N_HEADS, 1, HEAD), F32)
    outs = pl.pallas_call(
        body, name="rwkv_bwd", grid=(N_HEADS // RW_HB, nc),
        in_specs=[seq] * 5 + [par] * 5 + [st, seq],
        out_specs=[seq] * 5 + [par] * 5, out_shape=[seq_sh] * 5 + [par_sh] * 5,
        scratch_shapes=[pltpu.VMEM((RW_HB, HEAD, HEAD), F32)],
        compiler_params=_params(("arbitrary", "arbitrary")),
    )(*seqs, *pars, states, dz)
    return list(outs[:5]), list(outs[5:])


def _norm_fn(x, g):
    return (_rms(x, g),)


def _attn_prep_fn(proj, qn_w, kn_w):
    a, b = SB_HEADS, 3 * DIL_GROUP
    return (proj[0:a], proj[a:2 * a], proj[2 * a:3 * a],
            _rms(proj[3 * a:3 * a + b], qn_w), _rms(proj[3 * a + b:3 * a + 2 * b], kn_w), proj[3 * a + 2 * b:])


def _attn_merge_fn(o_sb, ol0, ol1, ol2):
    groups = (ol0, ol1, ol2)
    merged = []
    for j in range(DIL_GROUP):
        lses = [ol[j][:, HEAD:HEAD + 1] for ol in groups]
        m = functools.reduce(jnp.maximum, lses)
        es = [jnp.exp(l - m) for l in lses]
        inv = 1.0 / functools.reduce(jnp.add, es)
        merged.append(functools.reduce(jnp.add, [(e * inv) * ol[j][:, :HEAD] for e, ol in zip(es, groups)]))
    return (jnp.concatenate([_heads_to_nat(o_sb)] + merged, axis=-1),)


def _rw_mix_fn(x, xp, gn, mix, w0, w1, w2, a0, a1, a2, g1, g2):
    h = _rms(x, gn)
    xx = _rms(xp, gn) - h
    xr, xw, xk, xv, xa, xg = [h + xx * mix[i:i + 1] for i in range(6)]
    w_log = -jax.nn.softplus(-(w0 + _mm(jnp.tanh(_mm(xw, w1)), w2))) - 0.5
    lw = -jnp.exp(w_log)
    ag = jax.nn.sigmoid(a0 + _mm(_mm(xa, a1), a2))
    gate = _mm(jax.nn.sigmoid(_mm(xg, g1)), g2)
    return xr, xk, xv, _nat_to_heads(lw), _nat_to_heads(ag), gate


def _rw_gate_fn(z, gate):
    return (_heads_to_nat(z) * gate,)


def _adamw_update(w_ref, m_ref, v_ref, g_ref, go_ref, d_ref, mo_ref, vo_ref):
    g = g_ref[0].astype(F32)
    for j in range(1, N_DEV):
        g = g + g_ref[j].astype(F32)
    mn = ADAM_B1 * m_ref[...] + (1.0 - ADAM_B1) * g
    vn = ADAM_B2 * v_ref[...] + (1.0 - ADAM_B2) * jnp.square(g)
    m_hat = mn / (1.0 - ADAM_B1 ** ADAM_STEP)
    v_hat = vn / (1.0 - ADAM_B2 ** ADAM_STEP)
    go_ref[...] = g
    d_ref[...] = -ADAM_LR * (m_hat / (jnp.sqrt(v_hat) + ADAM_EPS) + ADAM_WD * w_ref[...])
    mo_ref[...] = mn
    vo_ref[...] = vn


def _adamw_many(name, items):
    n = len(items)

    def body(*refs):
        for i in range(n):
            _adamw_update(*refs[4 * i:4 * i + 4], *refs[4 * n + 4 * i:4 * n + 4 * i + 4])

    vmem = pl.BlockSpec(memory_space=pltpu.VMEM)
    outs = pl.pallas_call(
        body, name=name, in_specs=[vmem] * (4 * n), out_specs=[vmem] * (4 * n),
        out_shape=[jax.ShapeDtypeStruct(w.shape, F32) for w, _, _, _ in items for _ in range(4)],
        compiler_params=_params(),
    )(*[a for item in items for a in item])
    return [list(outs[4 * i:4 * i + 4]) for i in range(n)]


def _adamw(name, w, m, v, gparts, row0=0, prev=None):
    big_r, c = w.shape
    r = gparts.shape[1]
    tr = r
    if r % 8 == 0:
        tr = max(t for t in range(8, r + 1, 8) if r % t == 0 and (t * c * 4 <= (1 << 20) or t == 8))
    assert row0 % tr == 0 and (r == big_r or r % 8 == 0)
    off = row0 // tr

    def body(w_ref, m_ref, v_ref, g_ref, *rest):
        _adamw_update(w_ref, m_ref, v_ref, g_ref, *rest[-4:])

    tile = pl.BlockSpec((tr, c), lambda i: (i + off, 0))
    sh = jax.ShapeDtypeStruct((big_r, c), F32)
    prev = list(prev) if prev is not None else []
    return pl.pallas_call(
        body, name=name, grid=(r // tr,),
        in_specs=([tile, tile, tile, pl.BlockSpec((N_DEV, tr, c), lambda i: (0, i, 0))]
                  + [pl.BlockSpec(memory_space=pl.ANY)] * len(prev)),
        out_specs=[tile] * 4, out_shape=[sh] * 4,
        input_output_aliases={4 + j: j for j in range(len(prev))},
        compiler_params=_params(("arbitrary",)),
    )(w, m, v, gparts, *prev)


def _col_blocks_to_nat(g):
    return jnp.moveaxis(g, 0, 1).reshape(g.shape[1], -1)


def _nat_to_col_blocks(a):
    return jnp.moveaxis(a.reshape(a.shape[0], N_DEV, -1), 1, 0)


AG_GROUPS = ("f00", "att", "f01", "f10", "rw", "f11")
RS_GROUPS = ("f11", "rw", "f10", "f01", "f00", "att")
BF16_GRAD_GROUPS = ("att", "f00")
RW_SHARDED = ('rw_mix', 'rw_w0', 'rw_w1', 'rw_w2', 'rw_a0', 'rw_a1', 'rw_a2', 'rw_g1', 'rw_g2', 'rw_kk', 'rw_ka',
              'rw_wr', 'rw_wk', 'rw_wv', 'rw_wo', 'rw_lnx_g', 'rw_lnx_b')


def _step(x, target, rep, get, put):
    tied = lambda a, zero: a + zero[0, 0].astype(a.dtype)
    s, d = x.shape
    tf = min(512, s)
    tw = min(1024, s)
    dw_cols = 1024
    tt = min(256, s)
    row = lambda a: a.reshape(1, -1)
    mix_norm = rep["mix_norm"]
    ffw = {(0, 0): get("f00", None)}
    ffn_norm = _col_blocks_to_nat(ffw[(0, 0)]["ffn_norm"].reshape(N_DEV, 4, -1))

    acts = {}

    def ffn(nm, xin, l, h):
        g = ffw[(l, h)]
        out, *acts[(l, h)] = _ffn_fwd(nm, xin, ffn_norm[2 * l + h][None], g["gate"], g["up"], g["down"], min(2 * tf, s))
        return out

    x1 = ffn("ffn00_fwd", x, 0, 0)
    att = get("att", x1)
    w_in = _col_blocks_to_nat(att["attn_w_in"])
    w_out = _col_blocks_to_nat(att["attn_w_out"])
    (h0,) = _tile_fwd("mixnorm0_fwd", _norm_fn, [(x1, "nat")], [mix_norm[0:1]], [((s, d), BF16, "nat")], tf)
    proj = _linear_fwd("attn_in_fwd", h0, w_in, tf, out_layout="hm")
    bias = _bias_tiles(rep["rel_bias"], s)
    prep_pars = [rep["attn_q_norm"], rep["attn_k_norm"]]
    sb_sh, dl_sh = (SB_HEADS, s, HEAD), (3 * DIL_GROUP, s, HEAD)
    sq, sk, sv, qn, kn, vd = _tile_fwd("attn_prep_fwd", _attn_prep_fn, [(proj, "hm")], prep_pars,
                                       [(sb_sh, BF16, "hm")] * 3 + [(dl_sh, F32, "hm")] * 3, tt // 2)
    o_sb = _sb_fwd(sq, sk, sv)
    ols = [_dil_group_fwd(g, qn, kn, vd, bias) for g in range(3)]
    merge_tiled = [(o_sb, "hm")] + [(ol, "hm") for ol in ols]
    (merged,) = _tile_fwd("merge_fwd", _attn_merge_fn, merge_tiled, [], [((s, 512), BF16, "nat")], tf)
    x2 = _linear_fwd("attn_out_fwd", merged, w_out, tf, residual=x1)
    ffw[(0, 1)] = get("f01", x2)
    x3 = ffn("ffn01_fwd", x2, 0, 1)
    ffw[(1, 0)] = get("f10", x3)
    x4 = ffn("ffn10_fwd", x3, 1, 0)
    rw = get("rw", x4)
    rw_mix = _col_blocks_to_nat(rw["rw_mix"])
    rw_w1, rw_a1, rw_g1 = (rw[k].reshape(d, -1) for k in ("rw_w1", "rw_a1", "rw_g1"))
    rw_w2, rw_a2, rw_g2 = (_col_blocks_to_nat(rw[k]) for k in ("rw_w2", "rw_a2", "rw_g2"))
    rw_w0, rw_a0 = row(rw["rw_w0"]), row(rw["rw_a0"])
    head_par = lambda a: a.reshape(N_HEADS, 1, HEAD)
    scan_pars = [head_par(rw["rw_kk"]), head_par(rw["rw_ka"]), head_par(rep["rw_rk"]),
                 head_par(rw["rw_lnx_g"]), head_par(rw["rw_lnx_b"])]
    w_rkv = [rw[k].reshape(d, d) for k in ("rw_wr", "rw_wk", "rw_wv")]
    w_o = rw["rw_wo"].reshape(d, d)
    x4p = jnp.pad(x4, ((1, 0), (0, 0)))[:-1]
    mix_tiled = [(x4, "nat"), (x4p, "nat")]
    mix_pars = [mix_norm[1:2], rw_mix, rw_w0, rw_w1, rw_w2, rw_a0, rw_a1, rw_a2, rw_g1, rw_g2]
    hm_sh = (N_HEADS, s, HEAD)
    xr, xk, xv, lw, ag, gate = _tile_fwd(
        "rw_mix_fwd", _rw_mix_fn, mix_tiled, mix_pars,
        [((s, d), BF16, "nat")] * 3 + [(hm_sh, F32, "hm")] * 2 + [((s, d), F32, "nat")], tt)
    r_h, k_h, v_h = [_linear_fwd("rw_%s_fwd" % nm, xi, wi, tf, out_layout="hm")
                     for nm, xi, wi in zip("rkv", (xr, xk, xv), w_rkv)]
    scan_seqs = [r_h, lw, k_h, v_h, ag]
    z, states = _rwkv_fwd(scan_seqs, scan_pars)
    (zg,) = _tile_fwd("rw_gate_fwd", _rw_gate_fn, [(z, "hm"), (gate, "nat")], [], [((s, d), BF16, "nat")], tf)
    x5 = _linear_fwd("rw_out_fwd", zg, w_o, tf, residual=x4)
    ffw[(1, 1)] = get("f11", x5)
    y = ffn("ffn11_fwd", x5, 1, 1)
    dy, loss = _loss_head(y, target, tw)

    G = {}
    dgn = {}

    def fb(nm, group, xin, dout, l, h, zero=None, extra=None):
        g = ffw[(l, h)]
        gn = ffn_norm[2 * l + h][None]
        dxin, dgn[(l, h)], dwg, dwu, dwd = _ffn_bwd(nm, xin, dout, gn if zero is None else tied(gn, zero),
                                                   g["gate"], g["up"], g["down"], *acts[(l, h)], tf)
        shard = {"gate": dwg, "up": dwu, "down": dwd}
        if extra is not None:
            shard.update(extra())
        return dxin, put(group, {}, shard)

    dx5, zero = fb("ffn11_bwd", "f11", x5, dy, 1, 1)
    dzg = _linear_dx("rw_out_dx", dx5, w_o, tf)
    G["rw_wo"] = _linear_dw("rw_out_dw", zg, dx5, tw, dw_cols)
    (dz, dgate), _ = _tile_bwd("rw_gate_bwd", _rw_gate_fn, [(z, "hm"), (gate, "nat")], [], [(dzg, "nat")], tf, [True, True])
    (dr_h, dlw, dk_h, dv_h, dag), dscan = _rwkv_bwd(scan_seqs, [tied(scan_pars[0], zero)] + scan_pars[1:], states, dz)
    drkv = (dr_h, dk_h, dv_h)
    for k, gpar in zip(("rw_kk", "rw_ka", "rw_rk", "rw_lnx_g", "rw_lnx_b"), dscan):
        G[k] = gpar
    dxs = []
    for j, (nm, xi, wi) in enumerate(zip("rkv", (xr, xk, xv), w_rkv)):
        dxs.append(_linear_dx("rw_%s_dx" % nm, drkv[j], wi, tf, dy_layout="hm"))
        G["rw_w" + nm] = _linear_dw("rw_%s_dw" % nm, xi, drkv[j], tw, dw_cols, dy_layout="hm")
    (dx4a, dx4p), dmix = _tile_bwd(
        "rw_mix_bwd", _rw_mix_fn, mix_tiled, mix_pars,
        [(dxs[0], "nat"), (dxs[1], "nat"), (dxs[2], "nat"), (dlw, "hm"), (dag, "hm"), (dgate, "nat")],
        tt, [True, True], adds=[dx5, None])
    d_mixn1 = dmix[0]
    for k, gpar in zip(("rw_mix", "rw_w0", "rw_w1", "rw_w2", "rw_a0", "rw_a1", "rw_a2", "rw_g1", "rw_g2"), dmix[1:]):
        G[k] = gpar
    dx4 = dx4a + jnp.pad(dx4p[1:], ((0, 1), (0, 0)))
    for k in ("rw_mix", "rw_w2", "rw_a2", "rw_g2"):
        G[k] = _nat_to_col_blocks(G[k])
    for k in ("rw_w1", "rw_a1", "rw_g1", "rw_wr", "rw_wk", "rw_wv", "rw_wo"):
        G[k] = G[k].reshape(N_DEV, d // N_DEV, -1)
    for k in ("rw_w0", "rw_a0", "rw_kk", "rw_ka", "rw_lnx_g", "rw_lnx_b"):
        G[k] = G[k].reshape(N_DEV, 1, d // N_DEV)
    zero = put("rw", {"rw_rk": G["rw_rk"].reshape(N_HEADS, HEAD)}, {k: G[k] for k in RW_SHARDED})
    dx3, zero = fb("ffn10_bwd", "f10", x3, dx4, 1, 0, zero)
    dx2, zero = fb("ffn01_bwd", "f01", x2, dx3, 0, 1, zero)
    dmerged = _linear_dx("attn_out_dx", dx2, tied(w_out, zero), tf)
    (do_sb, *dols), _ = _tile_bwd("merge_bwd", _attn_merge_fn, merge_tiled, [], [(dmerged, "nat")], tt, [True] * 4)
    dq_sb, dk_sb, dv_sb = _sb_bwd(sq, sk, sv, do_sb)
    dil_grads, dbias = None, []
    for g in range(3):
        *dil_grads, db = _dil_group_bwd(g, qn, kn, vd, bias, ols[g], dols[g], dil_grads)
        dbias.append(db)
    dqn, dkn, dvd = dil_grads
    dbias = jnp.concatenate(dbias, axis=1)
    (dproj,), (dqn_w, dkn_w) = _tile_bwd(
        "attn_prep_bwd", _attn_prep_fn, [(proj, "hm")], prep_pars,
        [(dq_sb, "hm"), (dk_sb, "hm"), (dv_sb, "hm"), (dqn, "hm"), (dkn, "hm"), (dvd, "hm")], tt // 2, [True])
    dh0 = _linear_dx("attn_in_dx", dproj, w_in, tf, dy_layout="hm")
    (dx1,), (d_mixn0,) = _tile_bwd("mixnorm0_bwd", _norm_fn, [(x1, "nat")], [mix_norm[0:1]], [(dh0, "nat")], tf,
                                   [True], adds=[dx2])
    order = [(0, 0), (0, 1), (1, 0), (1, 1)]
    norm_grads = lambda: {"ffn_norm": _nat_to_col_blocks(jnp.concatenate([dgn[o] for o in order], axis=0))}
    dx0, zero = fb("ffn00_bwd", "f00", x, dx1, 0, 0, extra=norm_grads)
    G["attn_w_out"] = _linear_dw("attn_out_dw", tied(merged, zero), dx2, tw, dw_cols)
    G["attn_w_in"] = _linear_dw("attn_in_dw", tied(h0, zero), dproj, tw, dw_cols, dy_layout="hm")
    rep_grads = {"mix_norm": jnp.concatenate([d_mixn0, d_mixn1], axis=0), "rel_bias": _bias_tiles_bwd(dbias, s),
                 "attn_q_norm": dqn_w, "attn_k_norm": dkn_w}
    zero = put("att", rep_grads, {k: _nat_to_col_blocks(G[k]) for k in ("attn_w_in", "attn_w_out")})
    return loss, dx0, zero


WEIGHTS = ['ffn_norm', 'ffn_w_gate', 'ffn_w_up', 'ffn_w_down', 'mix_norm', 'rel_bias', 'attn_w_in', 'attn_q_norm',
           'attn_k_norm', 'attn_w_out', 'rw_mix', 'rw_w0', 'rw_w1', 'rw_w2', 'rw_a0', 'rw_a1', 'rw_a2', 'rw_g1', 'rw_g2',
           'rw_kk', 'rw_ka', 'rw_rk', 'rw_wr', 'rw_wk', 'rw_wv', 'rw_wo', 'rw_lnx_g', 'rw_lnx_b']
REPLICATED = ('mix_norm', 'rel_bias', 'attn_q_norm', 'attn_k_norm', 'rw_rk')
BF16_WEIGHTS = ('ffn_w_gate', 'ffn_w_up', 'ffn_w_down', 'attn_w_in', 'attn_w_out', 'rw_wr', 'rw_wk', 'rw_wv', 'rw_wo')


def kernel(x, ffn_norm, ffn_w_gate, ffn_w_up, ffn_w_down, mix_norm, rel_bias, attn_w_in, attn_q_norm, attn_k_norm, attn_w_out, rw_mix, rw_w0, rw_w1, rw_w2, rw_a0, rw_a1, rw_a2, rw_g1, rw_g2, rw_kk, rw_ka, rw_rk, rw_wr, rw_wk, rw_wv, rw_wo, rw_lnx_g, rw_lnx_b, loss_target, m_ffn_norm, m_ffn_w_gate, m_ffn_w_up, m_ffn_w_down, m_mix_norm, m_rel_bias, m_attn_w_in, m_attn_q_norm, m_attn_k_norm, m_attn_w_out, m_rw_mix, m_rw_w0, m_rw_w1, m_rw_w2, m_rw_a0, m_rw_a1, m_rw_a2, m_rw_g1, m_rw_g2, m_rw_kk, m_rw_ka, m_rw_rk, m_rw_wr, m_rw_wk, m_rw_wv, m_rw_wo, m_rw_lnx_g, m_rw_lnx_b, v_ffn_norm, v_ffn_w_gate, v_ffn_w_up, v_ffn_w_down, v_mix_norm, v_rel_bias, v_attn_w_in, v_attn_q_norm, v_attn_k_norm, v_attn_w_out, v_rw_mix, v_rw_w0, v_rw_w1, v_rw_w2, v_rw_a0, v_rw_a1, v_rw_a2, v_rw_g1, v_rw_g2, v_rw_kk, v_rw_ka, v_rw_rk, v_rw_wr, v_rw_wk, v_rw_wv, v_rw_wo, v_rw_lnx_g, v_rw_lnx_b):
    args = locals()
    w = {k: args[k] for k in WEIGHTS}
    cast = lambda k, a: a.astype(BF16) if k in BF16_WEIGHTS else a

    sources = {}
    for l, h in ((0, 0), (0, 1), (1, 0), (1, 1)):
        sources["f%d%d" % (l, h)] = {"gate": cast("ffn_w_gate", ffn_w_gate[l, h]), "up": cast("ffn_w_up", ffn_w_up[l, h]),
                                     "down": cast("ffn_w_down", ffn_w_down[l, h])}
    sources["f00"]["ffn_norm"] = ffn_norm
    drop_lead = lambda a: a[0] if a.ndim == 3 else a
    sources["att"] = {k: cast(k, w[k][0]) for k in ("attn_w_in", "attn_w_out")}
    sources["rw"] = {k: cast(k, drop_lead(w[k])) for k in RW_SHARDED}
    ag, token = {}, None
    for group in AG_GROUPS:
        names, arrays = list(sources[group]), list(sources[group].values())
        if token is not None:
            arrays[0] = arrays[0] + token[0, 0].astype(arrays[0].dtype)
        ag[group] = (names, _exchange_start("ag_start_" + group, arrays, []))
        token = ag[group][1]["token"]
    last_ag_token = token

    def get(group, after):
        names, started = ag[group]
        gathered, _ = _exchange_wait("ag_wait_" + group, started, last_ag_token if after is None else after)
        return dict(zip(names, gathered))

    rs = {}

    def put(group, rep_grads, shard_grads):
        if group in BF16_GRAD_GROUPS:
            shard_grads = {k: v.astype(BF16) for k, v in shard_grads.items()}
        started = _exchange_start("rs_start_" + group, list(rep_grads.values()), list(shard_grads.values()))
        rs[group] = (list(rep_grads), list(shard_grads), started)
        return started["token"]

    loss, dx, last_zero = _step(x[0], loss_target[0], {k: w[k] for k in REPLICATED}, get, put)
    loss = lax.psum(loss, MESH_AXES)

    results = {}
    ffn_prev = {}

    def update(k, parts, row0=0, prev=None):
        c = w[k].shape[-1]
        as2d = lambda a: a.reshape(-1, c)
        return _adamw("adamw_%s_%d" % (k, row0), as2d(w[k]), as2d(args["m_" + k]), as2d(args["v_" + k]),
                      parts.reshape(N_DEV, -1, c), row0, prev)

    def update_small(name, named_parts):
        items = []
        for k, parts in named_parts:
            c = w[k].shape[-1]
            items.append((w[k].reshape(-1, c), args["m_" + k].reshape(-1, c), args["v_" + k].reshape(-1, c),
                          parts.reshape(N_DEV, -1, c)))
        for (k, _), res in zip(named_parts, _adamw_many(name, items)):
            results[k] = res

    after = last_zero
    for group in RS_GROUPS:
        rep_names, shard_names, started = rs[group]
        rep_parts, shard_parts = _exchange_wait("rs_wait_" + group, started, after)
        small = []
        for k, parts in list(zip(rep_names, rep_parts)) + list(zip(shard_names, shard_parts)):
            if k in ("gate", "up", "down"):
                full = "ffn_w_" + k
                piece = 2 * int(group[1]) + int(group[2])
                ffn_prev[full] = update(full, parts, piece * parts.shape[1], ffn_prev.get(full))
                results[full] = ffn_prev[full]
                after = results[full][0]
            elif k == "attn_w_in":
                results[k] = update(k, parts)
                after = results[k][0]
            else:
                small.append((k, parts))
        if small:
            update_small("adamw_small_" + group, small)
            after = results[small[0][0]][0]

    outs = [[results[k][j].reshape(w[k].shape) for k in WEIGHTS] for j in range(4)]
    return (loss, dx[None], *outs[0], *outs[1], *outs[2], *outs[3])
```
